```python
import math
import jax, jax.numpy as jnp
from jax import lax
import numpy as np

D_MODEL = 1024
BATCH = 16
SEQ = 2048
DEPTH = 1

HEAD_DIM = 64
SWA_Q_HEADS = 8
SWA_KV_HEADS = 2
SB_HEADS = 8
WINDOW = 128
BLOCK = 128
ROPE_THETA = 10000.0
EPS = 1e-6
SWA_WIDTH = SWA_Q_HEADS * HEAD_DIM
SWA_KV_WIDTH = SWA_KV_HEADS * HEAD_DIM
SB_WIDTH = SB_HEADS * HEAD_DIM
MIX_WIDTH = SWA_WIDTH + SB_WIDTH
SPLITS = (SWA_WIDTH, SWA_KV_WIDTH, SWA_KV_WIDTH, SWA_WIDTH, SB_WIDTH, SB_WIDTH, SB_WIDTH, SB_WIDTH)
IN_WIDTH = sum(SPLITS)

kernel_name = "hybrid_swa_sink_stickbreaking_heads"


def rmsnorm(x, g):
    xf = x.astype(jnp.float32)
    y = xf * lax.rsqrt(jnp.mean(xf * xf, axis=-1, keepdims=True) + EPS)
    return (y * g.astype(jnp.float32)).astype(x.dtype)


def rope(x, positions):
    half = x.shape[-1] // 2
    inv_freq = ROPE_THETA ** (-jnp.arange(half, dtype=jnp.float32) * 2.0 / x.shape[-1])
    ang = positions.astype(jnp.float32)[..., None] * inv_freq
    cos = jnp.cos(ang)[:, :, None, :]
    sin = jnp.sin(ang)[:, :, None, :]
    xf = x.astype(jnp.float32)
    x1, x2 = xf[..., :half], xf[..., half:]
    out = jnp.concatenate([x1 * cos - x2 * sin, x2 * cos + x1 * sin], axis=-1)
    return out.astype(x.dtype)


def swa_sink_attention(q, k, v, sinks):
    B, S, Hq, d = q.shape
    Hkv = k.shape[2]
    G = Hq // Hkv
    nb = S // BLOCK
    scale = 1.0 / math.sqrt(d)
    qb = q.reshape(B, nb, BLOCK, Hkv, G, d)
    kb = k.reshape(B, nb, BLOCK, Hkv, d)
    vb = v.reshape(B, nb, BLOCK, Hkv, d)
    kprev = jnp.concatenate([jnp.zeros_like(kb[:, :1]), kb[:, :-1]], axis=1)
    vprev = jnp.concatenate([jnp.zeros_like(vb[:, :1]), vb[:, :-1]], axis=1)
    kw = jnp.concatenate([kprev, kb], axis=2)
    vw = jnp.concatenate([vprev, vb], axis=2)
    scores = jnp.einsum('bnqhgd,bnkhd->bnhgqk', qb, kw).astype(jnp.float32) * scale
    q_idx = jnp.arange(nb)[:, None] * BLOCK + jnp.arange(BLOCK)[None, :]
    k_idx = jnp.arange(nb)[:, None] * BLOCK - BLOCK + jnp.arange(2 * BLOCK)[None, :]
    diff = q_idx[:, :, None] - k_idx[:, None, :]
    mask = (diff >= 0) & (diff < WINDOW) & (k_idx[:, None, :] >= 0)
    scores = jnp.where(mask[None, :, None, None], scores, -jnp.inf)
    sink = sinks.astype(jnp.float32).reshape(Hkv, G)[None, None, :, :, None, None]
    m = jnp.maximum(jnp.max(scores, axis=-1, keepdims=True), sink)
    p = jnp.exp(scores - m)
    probs = p / (jnp.sum(p, axis=-1, keepdims=True) + jnp.exp(sink - m))
    out = jnp.einsum('bnhgqk,bnkhd->bnqhgd', probs.astype(v.dtype), vw)
    return out.reshape(B, S, Hq * d)


def stick_breaking_attention(q, k, v):
    B, S, H, d = q.shape
    nb = S // BLOCK
    scale = 1.0 / math.sqrt(d)
    outs = []
    for i in range(nb):
        t0 = i * BLOCK
        L = t0 + BLOCK
        qi = q[:, t0:L]
        kp = k[:, :L]
        vp = v[:, :L]
        z = jnp.einsum('bqhd,bkhd->bhqk', qi, kp).astype(jnp.float32) * scale
        t_idx = t0 + jnp.arange(BLOCK)[:, None]
        s_idx = jnp.arange(L)[None, :]
        causal = s_idx < t_idx
        log_fail = jnp.where(causal, jax.nn.log_sigmoid(-z), 0.0)
        after = lax.cumsum(log_fail, axis=3, reverse=True) - log_fail
        w = jnp.where(causal, jnp.exp(jax.nn.log_sigmoid(z) + after), 0.0)
        outs.append(jnp.einsum('bhqk,bkhd->bqhd', w.astype(v.dtype), vp))
    out = jnp.concatenate(outs, axis=1)
    return out.reshape(B, S, H * d)


def _fwd_setup_inputs(seed: int = 0) -> dict:
    key = jax.random.key(seed)
    ks = jax.random.split(key, 8)
    x = jax.random.normal(ks[0], (BATCH, SEQ, D_MODEL), jnp.float32)
    positions = jnp.broadcast_to(jnp.arange(SEQ, dtype=jnp.int32)[None, :], (BATCH, SEQ)).astype(jnp.int32)
    norm_gain = 1.0 + 0.02 * jax.random.normal(ks[1], (DEPTH, D_MODEL), jnp.float32)
    w_in = jax.random.normal(ks[2], (DEPTH, D_MODEL, IN_WIDTH), jnp.float32) * D_MODEL ** -0.5
    q_norm_gain = 1.0 + 0.02 * jax.random.normal(ks[3], (DEPTH, HEAD_DIM), jnp.float32)
    k_norm_gain = 1.0 + 0.02 * jax.random.normal(ks[4], (DEPTH, HEAD_DIM), jnp.float32)
    sinks = 0.5 * jax.random.normal(ks[5], (DEPTH, SWA_Q_HEADS), jnp.float32)
    w_out = jax.random.normal(ks[6], (DEPTH, MIX_WIDTH, D_MODEL), jnp.float32) * MIX_WIDTH ** -0.5
    return {"x": x, "positions": positions, "norm_gain": norm_gain, "w_in": w_in,
            "q_norm_gain": q_norm_gain, "k_norm_gain": k_norm_gain, "sinks": sinks, "w_out": w_out}


def _fwd_reference(x, positions, norm_gain, w_in, q_norm_gain, k_norm_gain, sinks, w_out):
    B, S, _ = x.shape
    split_points = list(np.cumsum(SPLITS)[:-1])
    for l in range(DEPTH):
        h = rmsnorm(x, norm_gain[l])
        proj = jnp.einsum('bsd,de->bse', h, w_in[l])
        qa, ka, va, ga, qb, kb, vb, gb = jnp.split(proj, split_points, axis=-1)
        qa = rope(rmsnorm(qa.reshape(B, S, SWA_Q_HEADS, HEAD_DIM), q_norm_gain[l]), positions)
        ka = rope(rmsnorm(ka.reshape(B, S, SWA_KV_HEADS, HEAD_DIM), k_norm_gain[l]), positions)
        va = va.reshape(B, S, SWA_KV_HEADS, HEAD_DIM)
        ya = swa_sink_attention(qa, ka, va, sinks[l]) * jax.nn.silu(ga)
        qb = qb.reshape(B, S, SB_HEADS, HEAD_DIM)
        kb = kb.reshape(B, S, SB_HEADS, HEAD_DIM)
        vb = vb.reshape(B, S, SB_HEADS, HEAD_DIM)
        yb = stick_breaking_attention(qb, kb, vb) * jax.nn.silu(gb)
        y = jnp.concatenate([ya, yb], axis=-1)
        x = x + jnp.einsum('bse,ed->bsd', y, w_out[l])
    return x


import jax as _jax
import jax.numpy as _jnp

TWIN_FORMAT = 'train_step'
FWD_PARAMS = ['x', 'positions', 'norm_gain', 'w_in', 'q_norm_gain', 'k_norm_gain', 'sinks', 'w_out']
TWIN_WEIGHTS = ['norm_gain', 'w_in', 'q_norm_gain', 'k_norm_gain', 'sinks', 'w_out']
TWIN_DIFF_INPUT = 'x'
TWIN_INPUTS = ['x', 'positions', 'norm_gain', 'w_in', 'q_norm_gain', 'k_norm_gain', 'sinks', 'w_out', 'loss_target', 'm_norm_gain', 'm_w_in', 'm_q_norm_gain', 'm_k_norm_gain', 'm_sinks', 'm_w_out', 'v_norm_gain', 'v_w_in', 'v_q_norm_gain', 'v_k_norm_gain', 'v_sinks', 'v_w_out']
TWIN_OUTPUTS = ['loss', 'grad_x', 'grad_norm_gain', 'grad_w_in', 'grad_q_norm_gain', 'grad_k_norm_gain', 'grad_sinks', 'grad_w_out', 'delta_norm_gain', 'delta_w_in', 'delta_q_norm_gain', 'delta_k_norm_gain', 'delta_sinks', 'delta_w_out', 'new_m_norm_gain', 'new_m_w_in', 'new_m_q_norm_gain', 'new_m_k_norm_gain', 'new_m_sinks', 'new_m_w_out', 'new_v_norm_gain', 'new_v_w_in', 'new_v_q_norm_gain', 'new_v_k_norm_gain', 'new_v_sinks', 'new_v_w_out']
TWIN_LEAF_KINDS = {'loss': 'loss', 'grad_x': 'grad_x', 'grad_norm_gain': 'grad_w', 'grad_w_in': 'grad_w', 'grad_q_norm_gain': 'grad_w', 'grad_k_norm_gain': 'grad_w', 'grad_sinks': 'grad_w', 'grad_w_out': 'grad_w', 'delta_norm_gain': 'delta_w', 'delta_w_in': 'delta_w', 'delta_q_norm_gain': 'delta_w', 'delta_k_norm_gain': 'delta_w', 'delta_sinks': 'delta_w', 'delta_w_out': 'delta_w', 'new_m_norm_gain': 'new_m', 'new_m_w_in': 'new_m', 'new_m_q_norm_gain': 'new_m', 'new_m_k_norm_gain': 'new_m', 'new_m_sinks': 'new_m', 'new_m_w_out': 'new_m', 'new_v_norm_gain': 'new_v', 'new_v_w_in': 'new_v', 'new_v_q_norm_gain': 'new_v', 'new_v_k_norm_gain': 'new_v', 'new_v_sinks': 'new_v', 'new_v_w_out': 'new_v'}


def _forward(args):
    return _fwd_reference(*[args[k] for k in FWD_PARAMS])


def _output_shape():
    out = _jax.eval_shape(lambda: _forward(_fwd_setup_inputs(0)))
    return out.shape, out.dtype

N_MICROBATCH = 1
ADAM_LR = 0.001
ADAM_B1 = 0.9
ADAM_B2 = 0.999
ADAM_EPS = 1e-08
ADAM_WD = 0.01
ADAM_STEP = 10
PER_EXAMPLE_BATCH_AXIS = {'x': 0, 'positions': 0, 'loss_target': 0}
SHARED_INPUTS = []
_WEIGHT_DTYPES = {'norm_gain': _jnp.float32, 'w_in': _jnp.float32, 'q_norm_gain': _jnp.float32, 'k_norm_gain': _jnp.float32, 'sinks': _jnp.float32, 'w_out': _jnp.float32}
MOMENT_SCALE = {'norm_gain': 5.657210e+00, 'w_in': 9.153622e-02, 'q_norm_gain': 1.059892e+00, 'k_norm_gain': 1.061043e+00, 'sinks': 2.602511e-01, 'w_out': 8.780075e-02}


def _to_microbatches(a, axis):
    t = _jnp.moveaxis(a, axis, 0)
    t = t.reshape((N_MICROBATCH, t.shape[0] // N_MICROBATCH) + t.shape[1:])
    return _jnp.moveaxis(t, 1, axis + 1)


def setup_inputs(seed: int = 0) -> dict:
    inp = _fwd_setup_inputs(seed)
    key = _jax.random.fold_in(_jax.random.key(seed), 7919)
    shape, _ = _output_shape()
    out = dict(inp)
    out["loss_target"] = _jax.random.normal(_jax.random.fold_in(key, 0), shape, _jnp.float32)
    for i, name in enumerate(TWIN_WEIGHTS):
        w = inp[name].astype(_jnp.float32)
        if MOMENT_SCALE is None:
            s = _jnp.sqrt(_jnp.mean(_jnp.square(w)) + 1e-30)
        else:
            s = MOMENT_SCALE[name]
        km, kv = _jax.random.split(_jax.random.fold_in(key, i + 1))
        out[name] = w
        out["m_" + name] = s * _jax.random.normal(km, w.shape, _jnp.float32)
        out["v_" + name] = (s * s) * _jax.random.uniform(kv, w.shape, _jnp.float32, 0.5, 1.5)
    if N_MICROBATCH > 1:
        for name, axis in PER_EXAMPLE_BATCH_AXIS.items():
            out[name] = _to_microbatches(out[name], axis)
    return {'x': out['x'], 'positions': out['positions'], 'norm_gain': out['norm_gain'], 'w_in': out['w_in'], 'q_norm_gain': out['q_norm_gain'], 'k_norm_gain': out['k_norm_gain'], 'sinks': out['sinks'], 'w_out': out['w_out'], 'loss_target': out['loss_target'], 'm_norm_gain': out['m_norm_gain'], 'm_w_in': out['m_w_in'], 'm_q_norm_gain': out['m_q_norm_gain'], 'm_k_norm_gain': out['m_k_norm_gain'], 'm_sinks': out['m_sinks'], 'm_w_out': out['m_w_out'], 'v_norm_gain': out['v_norm_gain'], 'v_w_in': out['v_w_in'], 'v_q_norm_gain': out['v_q_norm_gain'], 'v_k_norm_gain': out['v_k_norm_gain'], 'v_sinks': out['v_sinks'], 'v_w_out': out['v_w_out']}


def _loss(weights, diff, rest, loss_target):
    with _jax.named_scope("forward"):
        args = {**rest, TWIN_DIFF_INPUT: diff, **{k: w.astype(_WEIGHT_DTYPES[k]) for k, w in weights.items()}}
        y = _forward(args)
    with _jax.named_scope("loss_head"):
        err = _jnp.square(y.astype(_jnp.float32) - loss_target)
        return 0.5 * _jnp.sum(_jnp.mean(err, axis=-1)) if err.ndim else 0.5 * err


def _adamw(w, g, m, v):
    m = ADAM_B1 * m + (1.0 - ADAM_B1) * g
    v = ADAM_B2 * v + (1.0 - ADAM_B2) * _jnp.square(g)
    m_hat = m / (1.0 - ADAM_B1 ** ADAM_STEP)
    v_hat = v / (1.0 - ADAM_B2 ** ADAM_STEP)
    delta = -ADAM_LR * (m_hat / (_jnp.sqrt(v_hat) + ADAM_EPS) + ADAM_WD * w)
    return delta, m, v


def reference(x, positions, norm_gain, w_in, q_norm_gain, k_norm_gain, sinks, w_out, loss_target, m_norm_gain, m_w_in, m_q_norm_gain, m_k_norm_gain, m_sinks, m_w_out, v_norm_gain, v_w_in, v_q_norm_gain, v_k_norm_gain, v_sinks, v_w_out):
    given = dict(x=x, positions=positions, norm_gain=norm_gain, w_in=w_in, q_norm_gain=q_norm_gain, k_norm_gain=k_norm_gain, sinks=sinks, w_out=w_out, loss_target=loss_target, m_norm_gain=m_norm_gain, m_w_in=m_w_in, m_q_norm_gain=m_q_norm_gain, m_k_norm_gain=m_k_norm_gain, m_sinks=m_sinks, m_w_out=m_w_out, v_norm_gain=v_norm_gain, v_w_in=v_w_in, v_q_norm_gain=v_q_norm_gain, v_k_norm_gain=v_k_norm_gain, v_sinks=v_sinks, v_w_out=v_w_out)
    weights = {n: given[n] for n in TWIN_WEIGHTS}
    shared = {n: given[n] for n in SHARED_INPUTS}
    per_example = {n: given[n] for n in ['x', 'positions']}
    grad_fn = _jax.value_and_grad(_loss, argnums=(0, 1))

    def one_microbatch(ex, loss_target):
        ex = dict(ex)
        diff = ex.pop(TWIN_DIFF_INPUT)
        return grad_fn(weights, diff, {**shared, **ex}, loss_target)

    if N_MICROBATCH == 1:
        loss, (grad_w, grad_x) = one_microbatch(per_example, given["loss_target"])
    else:
        def body(carry, xs):
            loss_sum, grad_sum = carry
            l_k, (gw_k, gx_k) = one_microbatch(xs[0], xs[1])
            with _jax.named_scope("update"):
                return (loss_sum + l_k, _jax.tree.map(_jnp.add, grad_sum, gw_k)), gx_k

        init = (_jnp.zeros((), _jnp.float32), _jax.tree.map(_jnp.zeros_like, weights))
        (loss, grad_w), grad_x = _jax.lax.scan(body, init, (per_example, given["loss_target"]))
    with _jax.named_scope("update"):
        delta_w, new_m, new_v = {}, {}, {}
        for n in TWIN_WEIGHTS:
            delta_w[n], new_m[n], new_v[n] = _adamw(weights[n], grad_w[n], given["m_" + n], given["v_" + n])
    return (loss, grad_x, *[grad_w[n] for n in TWIN_WEIGHTS], *[delta_w[n] for n in TWIN_WEIGHTS],
            *[new_m[n] for n in TWIN_WEIGHTS], *[new_v[n] for n in TWIN_WEIGHTS])
```

```python
import functools

import jax
import jax.numpy as jnp
from jax import lax
from jax.experimental import pallas as pl
from jax.experimental.pallas import tpu as pltpu

F32 = jnp.float32
BF16 = jnp.bfloat16

N_DEV = 8
D_MODEL = 1024
SEQ = 2048
B_LOC = 2
T_LOC = B_LOC * SEQ
HEAD_DIM = 64
HEAD_SHIFT = 6
BLK = 128
N_BLK = SEQ // BLK
SLAB = 128
IN_WIDTH = 3328
IN_SHARD = IN_WIDTH // N_DEV
OUT_SHARD = D_MODEL // N_DEV
EPS = 1e-6
ROPE_THETA = 10000.0
Q_SCALE = 0.125
R_QA, R_KA, R_VA, R_GA, R_QB, R_KB, R_VB, R_GB, R_END = 0, 512, 640, 768, 1280, 1792, 2304, 2816, 3328
SMALL_W = 1536
ADAM_LR, ADAM_B1, ADAM_B2, ADAM_EPS, ADAM_WD, ADAM_STEP = 0.001, 0.9, 0.999, 1e-08, 0.01, 10
TM = 256
ACC_ROWS = 256
VMEM_LIMIT = 56 * 1024 * 1024

MESH = pl.DeviceIdType.MESH
NT = (((1,), (1,)), ((), ()))
TN = (((0,), (0,)), ((), ()))


def _params(sem, limit=VMEM_LIMIT):
    return pltpu.CompilerParams(dimension_semantics=sem, vmem_limit_bytes=limit)


def _dot(a, b, dims=None):
    if dims is None:
        return jnp.dot(a, b, preferred_element_type=F32)
    return lax.dot_general(a, b, dims, preferred_element_type=F32)


def _split(x):
    hi = x.astype(BF16)
    return hi, (x - hi.astype(F32)).astype(BF16)


def _lane(shape):
    return lax.broadcasted_iota(jnp.int32, shape, len(shape) - 1)


def _row(shape):
    return lax.broadcasted_iota(jnp.int32, shape, 0)


def _head_blockdiag():
    return ((_row((SLAB, SLAB)) >> HEAD_SHIFT) == (_lane((SLAB, SLAB)) >> HEAD_SHIFT)).astype(BF16)


def _head_sum(x, bd):
    hi, lo = _split(x)
    return _dot(hi, bd) + _dot(lo, bd)


def _swap_half(y, lane):
    return jnp.where((lane & 32) != 0, pltpu.roll(y, 32, 1), pltpu.roll(y, 96, 1))


def _stack_heads(q, lane):
    zero = jnp.zeros_like(q)
    return jnp.concatenate([jnp.where(lane < HEAD_DIM, q, zero), jnp.where(lane >= HEAD_DIM, q, zero)], axis=0)


def _unstack_heads(x2, lane):
    return jnp.where(lane < HEAD_DIM, x2[:BLK], x2[BLK:])


def _sigmoid(x):
    return 1.0 / (1.0 + jnp.exp(-x))


def _mesh_pos():
    return lax.axis_index("x"), lax.axis_index("y"), lax.axis_index("c")


def _flip(pos, mask):
    return tuple(1 - p if m else p for p, m in zip(pos, mask))


def _lin(pos):
    return 4 * pos[0] + 2 * pos[1] + pos[2]


def gather_weights(win_t_shard, wout_shard):
    shards = (win_t_shard, wout_shard)
    n_arr = len(shards)

    def body(a_ref, b_ref, oa_ref, ob_ref, send_sems, recv_sems, local_sems):
        x, y, c = _mesh_pos()
        me, sibling = (x, y, c), (x, y, 1 - c)
        chips = [(1 - x, y), (x, 1 - y), (1 - x, 1 - y)]
        ins, outs = (a_ref, b_ref), (oa_ref, ob_ref)

        def rows(a, pos):
            m = ins[a].shape[0]
            return outs[a].at[pl.ds(_lin(pos) * m, m), :]

        def copy(a, k, block, to, src=None):
            return pltpu.make_async_remote_copy(
                src_ref=rows(a, block) if src is None else src, dst_ref=rows(a, block),
                send_sem=send_sems.at[a, k], recv_sem=recv_sems.at[a, k], device_id=to, device_id_type=MESH)

        mine = [pltpu.make_async_copy(ins[a], rows(a, me), local_sems.at[a]) for a in range(n_arr)]
        for cp in mine:
            cp.start()
        first = []
        for a in range(n_arr):
            first.append(copy(a, 0, me, sibling, src=ins[a]))
            first += [copy(a, 1 + j, me, (*chip, c), src=ins[a]) for j, chip in enumerate(chips)]
        for cp in first:
            cp.start()
        passed = [[copy(a, 4 + j, (*chip, c), sibling) for j, chip in enumerate(chips)] for a in range(n_arr)]
        for j, chip in enumerate(chips):
            for a in range(n_arr):
                copy(a, 1 + j, (*chip, c), me).wait_recv()
                passed[a][j].start()
        for a in range(n_arr):
            copy(a, 0, sibling, me).wait_recv()
            for j, chip in enumerate(chips):
                copy(a, 4 + j, (*chip, 1 - c), me).wait_recv()
        for cp in first + [p for ps in passed for p in ps]:
            cp.wait_send()
        for cp in mine:
            cp.wait()

    vmem = pl.BlockSpec(memory_space=pltpu.VMEM)
    return pl.pallas_call(
        body, name="gather_weights",
        out_shape=[jax.ShapeDtypeStruct((N_DEV * s.shape[0], s.shape[1]), s.dtype) for s in shards],
        in_specs=[vmem] * n_arr, out_specs=[vmem] * n_arr,
        scratch_shapes=[pltpu.SemaphoreType.DMA((n_arr, 7)), pltpu.SemaphoreType.DMA((n_arr, 7)),
                        pltpu.SemaphoreType.DMA((n_arr,))],
        compiler_params=pltpu.CompilerParams(vmem_limit_bytes=VMEM_LIMIT),
    )(*shards)


def exchange_grads(dwin_t, dwout, small):
    srcs = (dwin_t, dwout, small)
    blocks = (IN_SHARD, OUT_SHARD, small.shape[0])
    scattered = (True, True, False)
    n_arr = len(srcs)
    masks = [(mx, my, mc) for mx in (0, 1) for my in (0, 1) for mc in (0, 1)][1:]

    def body(a_ref, b_ref, s_ref, ra_ref, rb_ref, rs_ref, send_sems, recv_sems, local_sems):
        me = _mesh_pos()
        ins, outs = (a_ref, b_ref, s_ref), (ra_ref, rb_ref, rs_ref)

        def block_for(a, pos):
            if scattered[a]:
                return ins[a].at[pl.ds(_lin(pos) * blocks[a], blocks[a]), :]
            return ins[a]

        def copy(a, k, to):
            return pltpu.make_async_remote_copy(
                src_ref=block_for(a, to), dst_ref=outs[a].at[_lin(me)],
                send_sem=send_sems.at[a, k], recv_sem=recv_sems.at[a, k], device_id=to, device_id_type=MESH)

        def landed(a, k, frm):
            return pltpu.make_async_remote_copy(
                src_ref=block_for(a, frm), dst_ref=outs[a].at[_lin(frm)],
                send_sem=send_sems.at[a, k], recv_sem=recv_sems.at[a, k], device_id=frm, device_id_type=MESH)

        mine = [pltpu.make_async_copy(block_for(a, me), outs[a].at[_lin(me)], local_sems.at[a]) for a in range(n_arr)]
        for cp in mine:
            cp.start()
        sent = [copy(a, k, _flip(me, mask)) for k, mask in enumerate(masks) for a in range(n_arr)]
        for cp in sent:
            cp.start()
        for k, mask in enumerate(masks):
            for a in range(n_arr):
                landed(a, k, _flip(me, mask)).wait_recv()
        for cp in sent:
            cp.wait_send()
        for cp in mine:
            cp.wait()

    hbm = pl.BlockSpec(memory_space=pl.ANY)
    return pl.pallas_call(
        body, name="exchange_grads",
        out_shape=[jax.ShapeDtypeStruct((N_DEV, blocks[a], srcs[a].shape[1]), srcs[a].dtype) for a in range(n_arr)],
        in_specs=[hbm] * n_arr, out_specs=[hbm] * n_arr,
        scratch_shapes=[pltpu.SemaphoreType.DMA((n_arr, 7)), pltpu.SemaphoreType.DMA((n_arr, 7)),
                        pltpu.SemaphoreType.DMA((n_arr,))],
    )(*srcs)


def _norm_rope(xs, gain2, cos, sin_s, bd, lane):
    r = lax.rsqrt(_head_sum(xs * xs, bd) * (1.0 / HEAD_DIM) + EPS)
    y = xs * r * gain2
    return y * cos + _swap_half(y, lane) * sin_s


def _dup_heads(xs, lane):
    r = pltpu.roll(xs, HEAD_DIM, 1)
    lo = lane < HEAD_DIM
    return jnp.concatenate([jnp.where(lo, xs, r), jnp.where(lo, r, xs)], axis=1)


def fwd_proj(x, pos, norm_gain, win_t, inv_freq, sin_sign, q_gain2, k_gain2):
    def body(x_ref, pos_ref, ng_ref, w_ref, if_ref, sg_ref, qg_ref, kg_ref,
             qa_raw_ref, ka_raw_ref, q_rot_ref, k_dup_ref, v_dup_ref, ga_ref, qb_ref, kb_ref, vb_ref, gb_ref,
             cos_ref, sin_ref):
        xv = x_ref[...]
        rstd = lax.rsqrt(jnp.mean(xv * xv, axis=-1, keepdims=True) + EPS)
        h = (xv * rstd * ng_ref[...]).astype(BF16)

        def proj(r0, r1):
            return _dot(h, w_ref[r0:r1, :], NT)

        ang = pos_ref[...].astype(F32) * if_ref[...]
        cos = jnp.cos(ang)
        sin_s = jnp.sin(ang) * sg_ref[...]
        cos_ref[...] = cos
        sin_ref[...] = sin_s
        lane = _lane((TM, SLAB))
        bd = _head_blockdiag()

        qa = proj(R_QA, R_KA)
        qa_raw_ref[...] = qa
        for p in range(4):
            sl = slice(p * SLAB, (p + 1) * SLAB)
            q_rot_ref[:, sl] = (_norm_rope(qa[:, sl], qg_ref[...], cos, sin_s, bd, lane) * Q_SCALE).astype(BF16)
        ka = proj(R_KA, R_VA)
        ka_raw_ref[...] = ka
        k_dup_ref[...] = _dup_heads(_norm_rope(ka, kg_ref[...], cos, sin_s, bd, lane), lane).astype(BF16)
        v_dup_ref[...] = _dup_heads(proj(R_VA, R_GA), lane).astype(BF16)
        ga_ref[...] = proj(R_GA, R_QB)
        qb_ref[...] = (proj(R_QB, R_KB) * Q_SCALE).astype(BF16)
        kb_ref[...] = proj(R_KB, R_VB).astype(BF16)
        vb_ref[...] = proj(R_VB, R_GB).astype(BF16)
        gb_ref[...] = proj(R_GB, R_END)

    def tile(w):
        return pl.BlockSpec((TM, w), lambda i: (i, 0))

    def whole(a):
        return pl.BlockSpec(a.shape, lambda i: (0, 0))

    widths = [(512, F32), (128, F32), (512, BF16), (256, BF16), (256, BF16), (512, F32), (512, BF16), (512, BF16),
              (512, BF16), (512, F32), (128, F32), (128, F32)]
    return pl.pallas_call(
        body, name="fwd_proj", grid=(T_LOC // TM,),
        in_specs=[tile(D_MODEL), tile(1), whole(norm_gain), whole(win_t), whole(inv_freq), whole(sin_sign),
                  whole(q_gain2), whole(k_gain2)],
        out_specs=[tile(w) for w, _ in widths],
        out_shape=[jax.ShapeDtypeStruct((T_LOC, w), dt) for w, dt in widths],
        compiler_params=_params(("arbitrary",)),
    )(x, pos, norm_gain, win_t, inv_freq, sin_sign, q_gain2, k_gain2)


def _swa_probs(q_ref, kp_ref, kc_ref, sinks_ref, p, i, lane):
    g = p // 2
    sl = slice(p * SLAB, (p + 1) * SLAB)
    gsl = slice(g * SLAB, (g + 1) * SLAB)
    q2 = _stack_heads(q_ref[:, sl], lane)
    keys = jnp.concatenate([kp_ref[:, gsl], kc_ref[:, gsl]], axis=0)
    s = _dot(q2, keys, NT)
    shape = (2 * BLK, 2 * BLK)
    r = _row(shape) & (BLK - 1)
    cidx = _lane(shape)
    valid = (cidx > r) & (cidx <= r + BLK) & ((cidx >= BLK) | (i > 0))
    s = jnp.where(valid, s, -jnp.inf)
    sink = jnp.where(_row((2 * BLK, 1)) < BLK, sinks_ref[0, 2 * p], sinks_ref[0, 2 * p + 1])
    m = jnp.maximum(jnp.max(s, axis=-1, keepdims=True), sink)
    e = jnp.exp(s - m)
    e_sink = jnp.exp(sink - m)
    den = jnp.sum(e, axis=-1, keepdims=True) + e_sink
    return q2, keys, e / den, e_sink / den


def _swa_specs():
    def cur(w):
        return pl.BlockSpec((BLK, w), lambda b, i: (b * N_BLK + i, 0))

    def prev(w):
        return pl.BlockSpec((BLK, w), lambda b, i: (b * N_BLK + jnp.maximum(i - 1, 0), 0))

    return cur, prev


def swa_fwd(q_rot, k_dup, v_dup, sinks):
    def body(q_ref, kp_ref, kc_ref, vp_ref, vc_ref, sinks_ref, o_ref):
        i = pl.program_id(1)
        lane = _lane((BLK, SLAB))
        for p in range(4):
            g = p // 2
            gsl = slice(g * SLAB, (g + 1) * SLAB)
            _, _, pn, _ = _swa_probs(q_ref, kp_ref, kc_ref, sinks_ref, p, i, lane)
            vals = jnp.concatenate([vp_ref[:, gsl], vc_ref[:, gsl]], axis=0)
            o_ref[:, p * SLAB:(p + 1) * SLAB] = _unstack_heads(_dot(pn.astype(BF16), vals), lane)

    cur, prev = _swa_specs()
    return pl.pallas_call(
        body, name="swa_fwd", grid=(B_LOC, N_BLK),
        in_specs=[cur(512), prev(256), cur(256), prev(256), cur(256), pl.BlockSpec(memory_space=pltpu.SMEM)],
        out_specs=cur(512),
        out_shape=jax.ShapeDtypeStruct((T_LOC, 512), F32),
        compiler_params=_params(("arbitrary", "arbitrary")),
    )(q_rot, k_dup, k_dup, v_dup, v_dup, sinks)


def _tri(suffix):
    r, cidx = _row((BLK + 16, BLK)), _lane((BLK + 16, BLK))
    tri = (cidx > r) if suffix else (cidx < r)
    return (tri | (r >= BLK)).astype(BF16)


def _key_sums(tri, x):
    hi, lo = _split(x)
    res = _dot(tri, jnp.concatenate([hi, lo], axis=1))
    w = x.shape[1]
    return res[:BLK, :w] + res[:BLK, w:], res[BLK:BLK + 1, :w] + res[BLK:BLK + 1, w:]


def _sb_block(zt, tri, carry, valid):
    lf = -(jnp.maximum(zt, 0.0) + jnp.log1p(jnp.exp(-jnp.abs(zt))))
    if valid is not None:
        lf = jnp.where(valid, lf, 0.0)
    after, tot = _key_sums(tri, lf)
    w = jnp.exp(zt + lf + after + carry)
    if valid is not None:
        w = jnp.where(valid, w, 0.0)
    return lf, w, tot


def _sb_specs():
    return pl.BlockSpec((SEQ, SLAB), lambda b, p: (b, p))


def _carry_spec():
    return pl.BlockSpec((N_BLK, N_BLK, 2 * BLK), lambda b, p: (b * 4 + p, 0, 0))


def _blk(j):
    return pl.ds(pl.multiple_of(j * BLK, BLK), BLK)


def _causal_t():
    return _row((BLK, 2 * BLK)) < (_lane((BLK, 2 * BLK)) & (BLK - 1))


def sb_fwd(qb, kb, vb):
    def body(q_ref, k_ref, v_ref, o_ref, c_ref, vt_ref):
        for j in range(N_BLK):
            vt_ref[j] = v_ref[j * BLK:(j + 1) * BLK, :].T
        lane = _lane((BLK, SLAB))
        tri = _tri(True)
        valid = _causal_t()
        jrow = _row((N_BLK, 2 * BLK))

        def q_block(i, _):
            q2 = _stack_heads(q_ref[_blk(i), :], lane)

            def key_block(j, carry, ot, mask):
                zt = _dot(k_ref[_blk(j), :], q2, NT)
                _, w, tot = _sb_block(zt, tri, carry, mask)
                return carry + tot, ot + _dot(vt_ref[j], w.astype(BF16))

            def earlier(jj, state):
                carry, ot, saved = state
                j = i - 1 - jj
                saved = jnp.where(jrow == j, carry, saved)
                carry, ot = key_block(j, carry, ot, None)
                return carry, ot, saved

            carry, ot = key_block(i, jnp.zeros((1, 2 * BLK), F32), jnp.zeros((SLAB, 2 * BLK), F32), valid)
            _, ot, saved = lax.fori_loop(0, i, earlier, (carry, ot, jnp.zeros((N_BLK, 2 * BLK), F32)))
            o_ref[_blk(i), :] = _unstack_heads(ot.T, lane)
            c_ref[i] = saved
            return 0

        lax.fori_loop(0, N_BLK, q_block, 0)

    spec = _sb_specs()
    return pl.pallas_call(
        body, name="sb_fwd", grid=(B_LOC, 4),
        in_specs=[spec] * 3, out_specs=[spec, _carry_spec()],
        out_shape=[jax.ShapeDtypeStruct((T_LOC, 512), F32),
                   jax.ShapeDtypeStruct((B_LOC * 4 * N_BLK, N_BLK, 2 * BLK), F32)],
        scratch_shapes=[pltpu.VMEM((N_BLK, SLAB, BLK), BF16)],
        compiler_params=_params(("arbitrary", "arbitrary")),
    )(qb, kb, vb)


def out_loss(o_a, o_b, ga, gb, x, target, wout):
    n_tiles = T_LOC // TM

    def body(oa_ref, ob_ref, ga_ref, gb_ref, x_ref, t_ref, w_ref,
             dout_ref, doa_ref, dob_ref, dga_ref, dgb_ref, dw_ref, loss_ref, acc_ref):
        step = pl.program_id(0)

        @pl.when(step == 0)
        def _():
            acc_ref[...] = jnp.zeros_like(acc_ref)
            loss_ref[...] = jnp.zeros_like(loss_ref)

        oa, ob, gav, gbv = oa_ref[...], ob_ref[...], ga_ref[...], gb_ref[...]
        sa, sb = _sigmoid(gav), _sigmoid(gbv)
        silu_a, silu_b = gav * sa, gbv * sb
        y = jnp.concatenate([oa * silu_a, ob * silu_b], axis=1).astype(BF16)
        err = x_ref[...] + _dot(y, w_ref[...]) - t_ref[...]
        e2 = err * err
        part = jnp.sum(e2.reshape(TM // 8, 8, D_MODEL), axis=0)
        loss_ref[...] += functools.reduce(lambda a, b: a + b, [part[:, k * 128:(k + 1) * 128] for k in range(8)])
        dout = err * (1.0 / D_MODEL)
        dout_ref[...] = dout
        dob16 = dout.astype(BF16)
        for r0 in range(0, D_MODEL, ACC_ROWS):
            acc_ref[r0:r0 + ACC_ROWS, :] += _dot(y[:, r0:r0 + ACC_ROWS], dob16, TN)
        dy = _dot(dob16, w_ref[...], NT)
        dya, dyb = dy[:, :512], dy[:, 512:]
        doa_ref[...] = (dya * silu_a).astype(BF16)
        dob_ref[...] = (dyb * silu_b).astype(BF16)
        dga_ref[...] = (dya * oa * (sa * (1.0 + gav * (1.0 - sa)))).astype(BF16)
        dgb_ref[...] = (dyb * ob * (sb * (1.0 + gbv * (1.0 - sb)))).astype(BF16)

        @pl.when(step == n_tiles - 1)
        def _():
            dw_ref[...] = acc_ref[...].astype(BF16)

    def tile(w):
        return pl.BlockSpec((TM, w), lambda i: (i, 0))

    const = lambda i: (0, 0)
    return pl.pallas_call(
        body, name="out_loss", grid=(n_tiles,),
        in_specs=[tile(512)] * 4 + [tile(D_MODEL)] * 2 + [pl.BlockSpec((D_MODEL, D_MODEL), const)],
        out_specs=[tile(D_MODEL), tile(512), tile(512), tile(512), tile(512),
                   pl.BlockSpec((D_MODEL, D_MODEL), const), pl.BlockSpec((8, 128), const)],
        out_shape=[jax.ShapeDtypeStruct((T_LOC, D_MODEL), F32)] + [jax.ShapeDtypeStruct((T_LOC, 512), BF16)] * 4
        + [jax.ShapeDtypeStruct((D_MODEL, D_MODEL), BF16), jax.ShapeDtypeStruct((8, 128), F32)],
        scratch_shapes=[pltpu.VMEM((D_MODEL, D_MODEL), F32)],
        compiler_params=_params(("arbitrary",)),
    )(o_a, o_b, ga, gb, x, target, wout)


def swa_bwd(q_rot, k_dup, v_dup, o_a, d_oa, sinks):
    def body(q_ref, kp_ref, kc_ref, vp_ref, vc_ref, o_ref, do_ref, sinks_ref, dq_ref, dk_ref, dv_ref, dsink_ref):
        b, i = pl.program_id(0), pl.program_id(1)

        @pl.when(i == 0)
        def _():
            dk_ref[...] = jnp.zeros_like(dk_ref)
            dv_ref[...] = jnp.zeros_like(dv_ref)

        @pl.when((i == 0) & (b == 0))
        def _():
            dsink_ref[...] = jnp.zeros_like(dsink_ref)

        lane = _lane((BLK, SLAB))
        rows_prev, rows_cur = _blk(jnp.maximum(i - 1, 0)), _blk(i)
        for p in range(4):
            g = p // 2
            sl = slice(p * SLAB, (p + 1) * SLAB)
            gsl = slice(g * SLAB, (g + 1) * SLAB)
            q2, keys, pn, p_sink = _swa_probs(q_ref, kp_ref, kc_ref, sinks_ref, p, i, lane)
            vals = jnp.concatenate([vp_ref[:, gsl], vc_ref[:, gsl]], axis=0)
            do2 = _stack_heads(do_ref[:, sl], lane)
            o = o_ref[:, sl]
            delta = jnp.sum(do2.astype(F32) * jnp.concatenate([o, o], axis=0), axis=-1, keepdims=True)
            ds = (pn * (_dot(do2, vals, NT) - delta)).astype(BF16)
            dq_ref[:, sl] = _unstack_heads(_dot(ds, keys), lane) * Q_SCALE
            dk2 = _dot(ds, q2, TN)
            dv2 = _dot(pn.astype(BF16), do2, TN)
            dk_ref[rows_prev, gsl] += dk2[:BLK]
            dk_ref[rows_cur, gsl] += dk2[BLK:]
            dv_ref[rows_prev, gsl] += dv2[:BLK]
            dv_ref[rows_cur, gsl] += dv2[BLK:]
            col = -p_sink * delta
            for e in range(2):
                dsink_ref[2 * p + e:2 * p + e + 1, :] += jnp.sum(col[e * BLK:(e + 1) * BLK], axis=0, keepdims=True)

    cur, prev = _swa_specs()
    per_seq = pl.BlockSpec((SEQ, 256), lambda b, i: (b, 0))
    return pl.pallas_call(
        body, name="swa_bwd", grid=(B_LOC, N_BLK),
        in_specs=[cur(512), prev(256), cur(256), prev(256), cur(256), cur(512), cur(512),
                  pl.BlockSpec(memory_space=pltpu.SMEM)],
        out_specs=[cur(512), per_seq, per_seq, pl.BlockSpec((8, 128), lambda b, i: (0, 0))],
        out_shape=[jax.ShapeDtypeStruct((T_LOC, 512), F32), jax.ShapeDtypeStruct((T_LOC, 256), F32),
                   jax.ShapeDtypeStruct((T_LOC, 256), F32), jax.ShapeDtypeStruct((8, 128), F32)],
        compiler_params=_params(("arbitrary", "arbitrary")),
    )(q_rot, k_dup, k_dup, v_dup, v_dup, o_a, d_oa, sinks)


def sb_bwd(qb, kb, vb, d_ob, carries):
    def body(q_ref, k_ref, v_ref, do_ref, c_ref, dq_ref, dk_ref, dv_ref, kt_ref):
        for j in range(N_BLK):
            kt_ref[j] = k_ref[j * BLK:(j + 1) * BLK, :].T
        dk_ref[...] = jnp.zeros_like(dk_ref)
        dv_ref[...] = jnp.zeros_like(dv_ref)
        lane = _lane((BLK, SLAB))
        tri_after, tri_before = _tri(True), _tri(False)
        valid = _causal_t()
        jrow = _row((N_BLK, 2 * BLK))

        def q_block(i, _):
            q2 = _stack_heads(q_ref[_blk(i), :], lane)
            do2 = _stack_heads(do_ref[_blk(i), :], lane)
            saved = c_ref[i]

            def key_block(j, carry_lf, before_u, dqt, mask):
                rows = _blk(j)
                zt = _dot(k_ref[rows, :], q2, NT)
                lf, w, _ = _sb_block(zt, tri_after, carry_lf, mask)
                u = _dot(v_ref[rows, :], do2, NT) * w
                pre_u, tot_u = _key_sums(tri_before, u)
                sig = jnp.exp(zt + lf)
                dz = u * (1.0 - sig) - (before_u + pre_u) * sig
                if mask is not None:
                    dz = jnp.where(mask, dz, 0.0)
                dz16 = dz.astype(BF16)
                dv_ref[rows, :] += _dot(w.astype(BF16), do2)
                dk_ref[rows, :] += _dot(dz16, q2)
                return before_u + tot_u, dqt + _dot(kt_ref[j], dz16)

            def earlier(j, state):
                carry_lf = jnp.sum(jnp.where(jrow == j, saved, 0.0), axis=0, keepdims=True)
                return key_block(j, carry_lf, state[0], state[1], None)

            zero = jnp.zeros((1, 2 * BLK), F32)
            state = lax.fori_loop(0, i, earlier, (zero, jnp.zeros((SLAB, 2 * BLK), F32)))
            _, dqt = key_block(i, zero, state[0], state[1], valid)
            dq_ref[_blk(i), :] = _unstack_heads(dqt.T, lane) * Q_SCALE
            return 0

        lax.fori_loop(0, N_BLK, q_block, 0)

    spec = _sb_specs()
    return pl.pallas_call(
        body, name="sb_bwd", grid=(B_LOC, 4),
        in_specs=[spec] * 4 + [_carry_spec()],
        out_specs=[spec] * 3,
        out_shape=[jax.ShapeDtypeStruct((T_LOC, 512), F32)] * 3,
        scratch_shapes=[pltpu.VMEM((N_BLK, SLAB, BLK), BF16)],
        compiler_params=_params(("arbitrary", "arbitrary")),
    )(qb, kb, vb, d_ob, carries)


def bwd_in(x, dout, norm_gain, win_t, dq_rot, dk_dup, dv_dup, qa_raw, ka_raw, cos, sin_s, q_gain2, k_gain2,
           dga, dgb, dqb, dkb, dvb):
    n_tiles = T_LOC // TM

    def body(x_ref, dout_ref, ng_ref, w_hbm, dq_ref, dk_ref, dv_ref, qa_ref, ka_ref, cos_ref, sin_ref, qg_ref, kg_ref,
             dga_ref, dgb_ref, dqb_ref, dkb_ref, dvb_ref,
             gx_ref, dw_hbm, dng_ref, dqg_ref, dkg_ref, w_ref, acc_ref, stage_ref, w_sem):
        step = pl.program_id(0)

        @pl.when(step == 0)
        def _():
            cp = pltpu.make_async_copy(w_hbm, w_ref, w_sem)
            cp.start()
            acc_ref[...] = jnp.zeros_like(acc_ref)
            dng_ref[...] = jnp.zeros_like(dng_ref)
            dqg_ref[...] = jnp.zeros_like(dqg_ref)
            dkg_ref[...] = jnp.zeros_like(dkg_ref)
            cp.wait()

        lane = _lane((TM, SLAB))
        bd = _head_blockdiag()
        cos, sin_s = cos_ref[...], sin_ref[...]

        def norm_rope_bwd(d_rot, raw, gain2):
            dy = d_rot * cos + _swap_half(d_rot * sin_s, lane)
            r = lax.rsqrt(_head_sum(raw * raw, bd) * (1.0 / HEAD_DIM) + EPS)
            xhat = raw * r
            dgain = jnp.sum(dy * xhat, axis=0, keepdims=True)
            dxh = dy * gain2
            mean = _head_sum(dxh * xhat, bd) * (1.0 / HEAD_DIM)
            return r * (dxh - xhat * mean), dgain

        def fold_dup(d_dup):
            a, b2 = d_dup[:, :SLAB], d_dup[:, SLAB:]
            return jnp.where(lane < HEAD_DIM, a + pltpu.roll(a, HEAD_DIM, 1), b2 + pltpu.roll(b2, HEAD_DIM, 1))

        pieces = []
        dqg = jnp.zeros((1, SLAB), F32)
        for p in range(4):
            sl = slice(p * SLAB, (p + 1) * SLAB)
            d_raw, dg = norm_rope_bwd(dq_ref[:, sl], qa_ref[:, sl], qg_ref[...])
            pieces.append(d_raw.astype(BF16))
            dqg = dqg + dg
        d_raw, dkg = norm_rope_bwd(fold_dup(dk_ref[...]), ka_ref[...], kg_ref[...])
        pieces.append(d_raw.astype(BF16))
        pieces.append(fold_dup(dv_ref[...]).astype(BF16))
        pieces += [dga_ref[...], dqb_ref[...].astype(BF16), dkb_ref[...].astype(BF16), dvb_ref[...].astype(BF16),
                   dgb_ref[...]]
        dproj = jnp.concatenate(pieces, axis=1)
        dqg_ref[0:1, :] += dqg + pltpu.roll(dqg, HEAD_DIM, 1)
        dkg_ref[0:1, :] += dkg + pltpu.roll(dkg, HEAD_DIM, 1)

        xv = x_ref[...]
        rstd = lax.rsqrt(jnp.mean(xv * xv, axis=-1, keepdims=True) + EPS)
        xhat = xv * rstd
        gain = ng_ref[...]
        h = (xhat * gain).astype(BF16)
        for r0 in range(0, IN_WIDTH, ACC_ROWS):
            acc_ref[r0:r0 + ACC_ROWS, :] += _dot(dproj[:, r0:r0 + ACC_ROWS], h, TN)
        dh = _dot(dproj, w_ref[...])
        dng_ref[0:1, :] += jnp.sum(dh * xhat, axis=0, keepdims=True)
        dxh = dh * gain
        gx_ref[...] = dout_ref[...] + rstd * (dxh - xhat * jnp.mean(dxh * xhat, axis=-1, keepdims=True))

        @pl.when(step == n_tiles - 1)
        def _():
            for r0 in range(0, IN_WIDTH, ACC_ROWS):
                stage_ref[...] = acc_ref[r0:r0 + ACC_ROWS, :].astype(BF16)
                pltpu.sync_copy(stage_ref, dw_hbm.at[r0:r0 + ACC_ROWS, :])

    def tile(w):
        return pl.BlockSpec((TM, w), lambda i: (i, 0))

    def whole(a):
        return pl.BlockSpec(a.shape, lambda i: (0, 0))

    const = lambda i: (0, 0)
    return pl.pallas_call(
        body, name="bwd_in", grid=(n_tiles,),
        in_specs=[tile(D_MODEL), tile(D_MODEL), whole(norm_gain), pl.BlockSpec(memory_space=pl.ANY),
                  tile(512), tile(256), tile(256), tile(512), tile(128), tile(128), tile(128),
                  whole(q_gain2), whole(k_gain2), tile(512), tile(512), tile(512), tile(512), tile(512)],
        out_specs=[tile(D_MODEL), pl.BlockSpec(memory_space=pl.ANY), pl.BlockSpec((8, D_MODEL), const),
                   pl.BlockSpec((8, SLAB), const), pl.BlockSpec((8, SLAB), const)],
        out_shape=[jax.ShapeDtypeStruct((T_LOC, D_MODEL), F32), jax.ShapeDtypeStruct((IN_WIDTH, D_MODEL), BF16),
                   jax.ShapeDtypeStruct((8, D_MODEL), F32), jax.ShapeDtypeStruct((8, SLAB), F32),
                   jax.ShapeDtypeStruct((8, SLAB), F32)],
        scratch_shapes=[pltpu.VMEM((IN_WIDTH, D_MODEL), BF16), pltpu.VMEM((IN_WIDTH, D_MODEL), F32),
                        pltpu.VMEM((ACC_ROWS, D_MODEL), BF16), pltpu.SemaphoreType.DMA],
        compiler_params=_params(("arbitrary",)),
    )(x, dout, norm_gain, win_t, dq_rot, dk_dup, dv_dup, qa_raw, ka_raw, cos, sin_s, q_gain2, k_gain2,
      dga, dgb, dqb, dkb, dvb)


def _adamw(w, g, m, v):
    m = ADAM_B1 * m + (1.0 - ADAM_B1) * g
    v = ADAM_B2 * v + (1.0 - ADAM_B2) * (g * g)
    m_hat = m / (1.0 - ADAM_B1 ** ADAM_STEP)
    v_hat = v / (1.0 - ADAM_B2 ** ADAM_STEP)
    delta = -ADAM_LR * (m_hat / (jnp.sqrt(v_hat) + ADAM_EPS) + ADAM_WD * w)
    return delta, m, v


def _sum_slots(r_ref):
    g = r_ref[0].astype(F32)
    for s in range(1, N_DEV):
        g = g + r_ref[s].astype(F32)
    return g


def adamw_rows(name, recv, w, m, v, rows_per_step):
    n_rows, n_cols = w.shape

    def body(r_ref, w_ref, m_ref, v_ref, g_ref, d_ref, nm_ref, nv_ref):
        g = _sum_slots(r_ref)
        g_ref[...] = g
        d_ref[...], nm_ref[...], nv_ref[...] = _adamw(w_ref[...], g, m_ref[...], v_ref[...])

    spec = pl.BlockSpec((rows_per_step, n_cols), lambda i: (i, 0))
    return pl.pallas_call(
        body, name=name, grid=(n_rows // rows_per_step,),
        in_specs=[pl.BlockSpec((N_DEV, rows_per_step, n_cols), lambda i: (0, i, 0)), spec, spec, spec],
        out_specs=[spec] * 4,
        out_shape=[jax.ShapeDtypeStruct((n_rows, n_cols), F32)] * 4,
        compiler_params=_params(("arbitrary",)),
    )(recv, w, m, v)


def adamw_small(recv, w_pack, m_pack, v_pack):
    def body(r_ref, w_ref, m_ref, v_ref, g_ref, d_ref, nm_ref, nv_ref, loss_ref):
        s = _sum_slots(r_ref)
        eye = (_row((8, SLAB)) == _lane((8, SLAB))).astype(F32)
        sinks = jnp.sum(s[:, 1280:1408] * eye, axis=0, keepdims=True)
        pad = jnp.zeros((1, D_MODEL - SLAB), F32)
        g = jnp.concatenate([
            s[0:1, :D_MODEL],
            jnp.concatenate([s[0:1, 1024:1152], pad], axis=1),
            jnp.concatenate([s[0:1, 1152:1280], pad], axis=1),
            jnp.concatenate([sinks, pad], axis=1),
            jnp.zeros((4, D_MODEL), F32)], axis=0)
        g_ref[...] = g
        d_ref[...], nm_ref[...], nv_ref[...] = _adamw(w_ref[...], g, m_ref[...], v_ref[...])
        loss = jnp.sum(jnp.sum(s[:, 1408:1536], axis=1, keepdims=True), axis=0, keepdims=True) * (0.5 / D_MODEL)
        loss_ref[...] = jnp.broadcast_to(loss, (8, SLAB))

    return pl.pallas_call(
        body, name="adamw_small",
        out_shape=[jax.ShapeDtypeStruct((8, D_MODEL), F32)] * 4 + [jax.ShapeDtypeStruct((8, SLAB), F32)],
        compiler_params=pltpu.CompilerParams(vmem_limit_bytes=VMEM_LIMIT),
    )(recv, w_pack, m_pack, v_pack)


def _pack_small(ng, qg, kg, sk):
    def row(a):
        return jnp.pad(a, ((0, 0), (0, D_MODEL - a.shape[1])))
    return jnp.concatenate([row(ng), row(qg), row(kg), row(sk), jnp.zeros((4, D_MODEL), F32)], axis=0)


def kernel(x, positions, norm_gain, w_in, q_norm_gain, k_norm_gain, sinks, w_out, loss_target, m_norm_gain, m_w_in, m_q_norm_gain, m_k_norm_gain, m_sinks, m_w_out, v_norm_gain, v_w_in, v_q_norm_gain, v_k_norm_gain, v_sinks, v_w_out):
    x2 = x.reshape(T_LOC, D_MODEL)
    tgt2 = loss_target.reshape(T_LOC, D_MODEL)
    pos2 = positions.reshape(T_LOC, 1)
    half = HEAD_DIM // 2
    inv_freq = ROPE_THETA ** (-jnp.arange(half, dtype=F32) * 2.0 / HEAD_DIM)
    inv_freq = jnp.tile(inv_freq, SLAB // half).reshape(1, SLAB)
    sin_sign = jnp.tile(jnp.concatenate([-jnp.ones((half,), F32), jnp.ones((half,), F32)]), 2).reshape(1, SLAB)
    q_gain2 = jnp.tile(q_norm_gain, (1, 2))
    k_gain2 = jnp.tile(k_norm_gain, (1, 2))

    win_t, wout = gather_weights(w_in[0].T.astype(BF16), w_out[0].astype(BF16))

    (qa_raw, ka_raw, q_rot, k_dup, v_dup, ga, qb, kb, vb, gb, cos, sin_s) = fwd_proj(
        x2, pos2, norm_gain, win_t, inv_freq, sin_sign, q_gain2, k_gain2)
    o_a = swa_fwd(q_rot, k_dup, v_dup, sinks)
    o_b, carries = sb_fwd(qb, kb, vb)
    dout, d_oa, d_ob, dga, dgb, dwout, loss_part = out_loss(o_a, o_b, ga, gb, x2, tgt2, wout)
    dq_rot, dk_dup, dv_dup, dsink = swa_bwd(q_rot, k_dup, v_dup, o_a, d_oa, sinks)
    dqb, dkb, dvb = sb_bwd(qb, kb, vb, d_ob, carries)
    grad_x, dwin_t, dng, dqg, dkg = bwd_in(
        x2, dout, norm_gain, win_t, dq_rot, dk_dup, dv_dup, qa_raw, ka_raw, cos, sin_s, q_gain2, k_gain2,
        dga, dgb, dqb, dkb, dvb)

    small = jnp.concatenate([dng, dqg, dkg, dsink, loss_part], axis=1)
    r_win, r_out, r_small = exchange_grads(dwin_t, dwout, small)

    g_win_t, d_win_t, nm_win_t, nv_win_t = adamw_rows(
        "adamw_w_in", r_win, w_in[0].T, m_w_in[0].T, v_w_in[0].T, IN_SHARD // 2)
    g_wout, d_wout, nm_wout, nv_wout = adamw_rows("adamw_w_out", r_out, w_out[0], m_w_out[0], v_w_out[0], OUT_SHARD)
    g_s, d_s, nm_s, nv_s, loss = adamw_small(
        r_small, _pack_small(norm_gain, q_norm_gain, k_norm_gain, sinks),
        _pack_small(m_norm_gain, m_q_norm_gain, m_k_norm_gain, m_sinks),
        _pack_small(v_norm_gain, v_q_norm_gain, v_k_norm_gain, v_sinks))

    def unpack(p, big_in_t, big_out):
        return (p[0:1, :], big_in_t.T[None], p[1:2, :HEAD_DIM], p[2:3, :HEAD_DIM], p[3:4, :8], big_out[None])

    return (loss[0, 0], grad_x.reshape(B_LOC, SEQ, D_MODEL),
            *unpack(g_s, g_win_t, g_wout), *unpack(d_s, d_win_t, d_wout),
            *unpack(nm_s, nm_win_t, nm_wout), *unpack(nv_s, nv_win_t, nv_wout))
```

```python
import functools

import jax
import jax.numpy as jnp
from jax import lax
from jax.experimental import pallas as pl
from jax.experimental.pallas import tpu as pltpu

F32 = jnp.float32
BF16 = jnp.bfloat16

N_DEV = 8
D_MODEL = 1024
SEQ = 2048
B_LOC = 2
T_LOC = B_LOC * SEQ
HEAD_DIM = 64
HEAD_SHIFT = 6
BLK = 128
N_BLK = SEQ // BLK
SLAB = 128
IN_WIDTH = 3328
IN_SHARD = IN_WIDTH // N_DEV
OUT_SHARD = D_MODEL // N_DEV
EPS = 1e-6
ROPE_THETA = 10000.0
Q_SCALE = 0.125
R_QA, R_KA, R_VA, R_GA, R_QB, R_KB, R_VB, R_GB, R_END = 0, 512, 640, 768, 1280, 1792, 2304, 2816, 3328
SMALL_W = 1536
ADAM_LR, ADAM_B1, ADAM_B2, ADAM_EPS, ADAM_WD, ADAM_STEP = 0.001, 0.9, 0.999, 1e-08, 0.01, 10
TM = 256
ACC_ROWS = 256
SB_NP = 4
VMEM_LIMIT = 56 * 1024 * 1024

MESH = pl.DeviceIdType.MESH
NT = (((1,), (1,)), ((), ()))
TN = (((0,), (0,)), ((), ()))


def _params(sem, limit=VMEM_LIMIT):
    return pltpu.CompilerParams(dimension_semantics=sem, vmem_limit_bytes=limit)


def _dot(a, b, dims=None):
    if dims is None:
        return jnp.dot(a, b, preferred_element_type=F32)
    return lax.dot_general(a, b, dims, preferred_element_type=F32)


def _split(x):
    hi = x.astype(BF16)
    return hi, (x - hi.astype(F32)).astype(BF16)


def _lane(shape):
    return lax.broadcasted_iota(jnp.int32, shape, len(shape) - 1)


def _row(shape):
    return lax.broadcasted_iota(jnp.int32, shape, 0)


def _head_blockdiag():
    return ((_row((SLAB, SLAB)) >> HEAD_SHIFT) == (_lane((SLAB, SLAB)) >> HEAD_SHIFT)).astype(BF16)


def _head_sum(x, bd):
    hi, lo = _split(x)
    return _dot(hi, bd) + _dot(lo, bd)


def _swap_half(y, lane):
    return jnp.where((lane & 32) != 0, pltpu.roll(y, 32, 1), pltpu.roll(y, 96, 1))


def _stack_heads(q, lane):
    zero = jnp.zeros_like(q)
    return jnp.concatenate([jnp.where(lane < HEAD_DIM, q, zero), jnp.where(lane >= HEAD_DIM, q, zero)], axis=0)


def _unstack_heads(x2, lane):
    return jnp.where(lane < HEAD_DIM, x2[:BLK], x2[BLK:])


def _sigmoid(x):
    return 1.0 / (1.0 + jnp.exp(-x))


def _mesh_pos():
    return lax.axis_index("x"), lax.axis_index("y"), lax.axis_index("c")


def _flip(pos, mask):
    return tuple(1 - p if m else p for p, m in zip(pos, mask))


def _lin(pos):
    return 4 * pos[0] + 2 * pos[1] + pos[2]


def gather_weights(win_t_shard, wout_shard):
    shards = (win_t_shard, wout_shard)
    n_arr = len(shards)

    def body(a_ref, b_ref, oa_ref, ob_ref, send_sems, recv_sems, local_sems):
        x, y, c = _mesh_pos()
        me, sibling = (x, y, c), (x, y, 1 - c)
        chips = [(1 - x, y), (x, 1 - y), (1 - x, 1 - y)]
        ins, outs = (a_ref, b_ref), (oa_ref, ob_ref)

        def rows(a, pos):
            m = ins[a].shape[0]
            return outs[a].at[pl.ds(_lin(pos) * m, m), :]

        def copy(a, k, block, to, src=None):
            return pltpu.make_async_remote_copy(
                src_ref=rows(a, block) if src is None else src, dst_ref=rows(a, block),
                send_sem=send_sems.at[a, k], recv_sem=recv_sems.at[a, k], device_id=to, device_id_type=MESH)

        mine = [pltpu.make_async_copy(ins[a], rows(a, me), local_sems.at[a]) for a in range(n_arr)]
        for cp in mine:
            cp.start()
        first = []
        for a in range(n_arr):
            first.append(copy(a, 0, me, sibling, src=ins[a]))
            first += [copy(a, 1 + j, me, (*chip, c), src=ins[a]) for j, chip in enumerate(chips)]
        for cp in first:
            cp.start()
        passed = [[copy(a, 4 + j, (*chip, c), sibling) for j, chip in enumerate(chips)] for a in range(n_arr)]
        for j, chip in enumerate(chips):
            for a in range(n_arr):
                copy(a, 1 + j, (*chip, c), me).wait_recv()
                passed[a][j].start()
        for a in range(n_arr):
            copy(a, 0, sibling, me).wait_recv()
            for j, chip in enumerate(chips):
                copy(a, 4 + j, (*chip, 1 - c), me).wait_recv()
        for cp in first + [p for ps in passed for p in ps]:
            cp.wait_send()
        for cp in mine:
            cp.wait()

    vmem = pl.BlockSpec(memory_space=pltpu.VMEM)
    return pl.pallas_call(
        body, name="gather_weights",
        out_shape=[jax.ShapeDtypeStruct((N_DEV * s.shape[0], s.shape[1]), s.dtype) for s in shards],
        in_specs=[vmem] * n_arr, out_specs=[vmem] * n_arr,
        scratch_shapes=[pltpu.SemaphoreType.DMA((n_arr, 7)), pltpu.SemaphoreType.DMA((n_arr, 7)),
                        pltpu.SemaphoreType.DMA((n_arr,))],
        compiler_params=pltpu.CompilerParams(vmem_limit_bytes=VMEM_LIMIT),
    )(*shards)


def exchange_grads(dwin_t, dwout, small):
    srcs = (dwin_t, dwout, small)
    blocks = (IN_SHARD, OUT_SHARD, small.shape[0])
    scattered = (True, True, False)
    n_arr = len(srcs)
    masks = [(mx, my, mc) for mx in (0, 1) for my in (0, 1) for mc in (0, 1)][1:]

    def body(a_ref, b_ref, s_ref, ra_ref, rb_ref, rs_ref, send_sems, recv_sems, local_sems):
        me = _mesh_pos()
        ins, outs = (a_ref, b_ref, s_ref), (ra_ref, rb_ref, rs_ref)

        def block_for(a, pos):
            if scattered[a]:
                return ins[a].at[pl.ds(_lin(pos) * blocks[a], blocks[a]), :]
            return ins[a]

        def copy(a, k, to):
            return pltpu.make_async_remote_copy(
                src_ref=block_for(a, to), dst_ref=outs[a].at[_lin(me)],
                send_sem=send_sems.at[a, k], recv_sem=recv_sems.at[a, k], device_id=to, device_id_type=MESH)

        def landed(a, k, frm):
            return pltpu.make_async_remote_copy(
                src_ref=block_for(a, frm), dst_ref=outs[a].at[_lin(frm)],
                send_sem=send_sems.at[a, k], recv_sem=recv_sems.at[a, k], device_id=frm, device_id_type=MESH)

        mine = [pltpu.make_async_copy(block_for(a, me), outs[a].at[_lin(me)], local_sems.at[a]) for a in range(n_arr)]
        for cp in mine:
            cp.start()
        sent = [copy(a, k, _flip(me, mask)) for k, mask in enumerate(masks) for a in range(n_arr)]
        for cp in sent:
            cp.start()
        for k, mask in enumerate(masks):
            for a in range(n_arr):
                landed(a, k, _flip(me, mask)).wait_recv()
        for cp in sent:
            cp.wait_send()
        for cp in mine:
            cp.wait()

    hbm = pl.BlockSpec(memory_space=pl.ANY)
    return pl.pallas_call(
        body, name="exchange_grads",
        out_shape=[jax.ShapeDtypeStruct((N_DEV, blocks[a], srcs[a].shape[1]), srcs[a].dtype) for a in range(n_arr)],
        in_specs=[hbm] * n_arr, out_specs=[hbm] * n_arr,
        scratch_shapes=[pltpu.SemaphoreType.DMA((n_arr, 7)), pltpu.SemaphoreType.DMA((n_arr, 7)),
                        pltpu.SemaphoreType.DMA((n_arr,))],
    )(*srcs)


def _norm_rope(xs, gain2, cos, sin_s, bd, lane):
    r = lax.rsqrt(_head_sum(xs * xs, bd) * (1.0 / HEAD_DIM) + EPS)
    y = xs * r * gain2
    return y * cos + _swap_half(y, lane) * sin_s


def _dup_heads(xs, lane):
    r = pltpu.roll(xs, HEAD_DIM, 1)
    lo = lane < HEAD_DIM
    return jnp.concatenate([jnp.where(lo, xs, r), jnp.where(lo, r, xs)], axis=1)


def fwd_proj(x, pos, norm_gain, win_t, inv_freq, sin_sign, q_gain2, k_gain2):
    def body(x_ref, pos_ref, ng_ref, w_ref, if_ref, sg_ref, qg_ref, kg_ref,
             qa_raw_ref, ka_raw_ref, q_rot_ref, k_dup_ref, v_dup_ref, ga_ref, qb_ref, kb_ref, vb_ref, gb_ref,
             cos_ref, sin_ref):
        xv = x_ref[...]
        rstd = lax.rsqrt(jnp.mean(xv * xv, axis=-1, keepdims=True) + EPS)
        h = (xv * rstd * ng_ref[...]).astype(BF16)

        def proj(r0, r1):
            return _dot(h, w_ref[r0:r1, :], NT)

        ang = pos_ref[...].astype(F32) * if_ref[...]
        cos = jnp.cos(ang)
        sin_s = jnp.sin(ang) * sg_ref[...]
        cos_ref[...] = cos
        sin_ref[...] = sin_s
        lane = _lane((TM, SLAB))
        bd = _head_blockdiag()

        qa = proj(R_QA, R_KA)
        qa_raw_ref[...] = qa
        for p in range(4):
            sl = slice(p * SLAB, (p + 1) * SLAB)
            q_rot_ref[:, sl] = (_norm_rope(qa[:, sl], qg_ref[...], cos, sin_s, bd, lane) * Q_SCALE).astype(BF16)
        ka = proj(R_KA, R_VA)
        ka_raw_ref[...] = ka
        k_dup_ref[...] = _dup_heads(_norm_rope(ka, kg_ref[...], cos, sin_s, bd, lane), lane).astype(BF16)
        v_dup_ref[...] = _dup_heads(proj(R_VA, R_GA), lane).astype(BF16)
        ga_ref[...] = proj(R_GA, R_QB)
        qb_ref[...] = (proj(R_QB, R_KB) * Q_SCALE).astype(BF16)
        kb_ref[...] = proj(R_KB, R_VB).astype(BF16)
        vb_ref[...] = proj(R_VB, R_GB).astype(BF16)
        gb_ref[...] = proj(R_GB, R_END)

    def tile(w):
        return pl.BlockSpec((TM, w), lambda i: (i, 0))

    def whole(a):
        return pl.BlockSpec(a.shape, lambda i: (0, 0))

    widths = [(512, F32), (128, F32), (512, BF16), (256, BF16), (256, BF16), (512, F32), (512, BF16), (512, BF16),
              (512, BF16), (512, F32), (128, F32), (128, F32)]
    return pl.pallas_call(
        body, name="fwd_proj", grid=(T_LOC // TM,),
        in_specs=[tile(D_MODEL), tile(1), whole(norm_gain), whole(win_t), whole(inv_freq), whole(sin_sign),
                  whole(q_gain2), whole(k_gain2)],
        out_specs=[tile(w) for w, _ in widths],
        out_shape=[jax.ShapeDtypeStruct((T_LOC, w), dt) for w, dt in widths],
        compiler_params=_params(("arbitrary",)),
    )(x, pos, norm_gain, win_t, inv_freq, sin_sign, q_gain2, k_gain2)


def _swa_probs(q_ref, kp_ref, kc_ref, sinks_ref, p, i, lane):
    g = p // 2
    sl = slice(p * SLAB, (p + 1) * SLAB)
    gsl = slice(g * SLAB, (g + 1) * SLAB)
    q2 = _stack_heads(q_ref[:, sl], lane)
    keys = jnp.concatenate([kp_ref[:, gsl], kc_ref[:, gsl]], axis=0)
    s = _dot(q2, keys, NT)
    shape = (2 * BLK, 2 * BLK)
    r = _row(shape) & (BLK - 1)
    cidx = _lane(shape)
    valid = (cidx > r) & (cidx <= r + BLK) & ((cidx >= BLK) | (i > 0))
    s = jnp.where(valid, s, -jnp.inf)
    sink = jnp.where(_row((2 * BLK, 1)) < BLK, sinks_ref[0, 2 * p], sinks_ref[0, 2 * p + 1])
    m = jnp.maximum(jnp.max(s, axis=-1, keepdims=True), sink)
    e = jnp.exp(s - m)
    e_sink = jnp.exp(sink - m)
    den = jnp.sum(e, axis=-1, keepdims=True) + e_sink
    return q2, keys, e / den, e_sink / den


def _swa_specs():
    def cur(w):
        return pl.BlockSpec((BLK, w), lambda b, i: (b * N_BLK + i, 0))

    def prev(w):
        return pl.BlockSpec((BLK, w), lambda b, i: (b * N_BLK + jnp.maximum(i - 1, 0), 0))

    return cur, prev


def swa_fwd(q_rot, k_dup, v_dup, sinks):
    def body(q_ref, kp_ref, kc_ref, vp_ref, vc_ref, sinks_ref, o_ref):
        i = pl.program_id(1)
        lane = _lane((BLK, SLAB))
        for p in range(4):
            g = p // 2
            gsl = slice(g * SLAB, (g + 1) * SLAB)
            _, _, pn, _ = _swa_probs(q_ref, kp_ref, kc_ref, sinks_ref, p, i, lane)
            vals = jnp.concatenate([vp_ref[:, gsl], vc_ref[:, gsl]], axis=0)
            o_ref[:, p * SLAB:(p + 1) * SLAB] = _unstack_heads(_dot(pn.astype(BF16), vals), lane)

    cur, prev = _swa_specs()
    return pl.pallas_call(
        body, name="swa_fwd", grid=(B_LOC, N_BLK),
        in_specs=[cur(512), prev(256), cur(256), prev(256), cur(256), pl.BlockSpec(memory_space=pltpu.SMEM)],
        out_specs=cur(512),
        out_shape=jax.ShapeDtypeStruct((T_LOC, 512), F32),
        compiler_params=_params(("arbitrary", "arbitrary")),
    )(q_rot, k_dup, k_dup, v_dup, v_dup, sinks)


def _tri(suffix):
    r, cidx = _row((BLK + 16, BLK)), _lane((BLK + 16, BLK))
    tri = (cidx > r) if suffix else (cidx < r)
    return (tri | (r >= BLK)).astype(BF16)


def _key_sums(tri, x):
    hi, lo = _split(x)
    res = _dot(tri, jnp.concatenate([hi, lo], axis=1))
    w = x.shape[1]
    return res[:BLK, :w] + res[:BLK, w:], res[BLK:BLK + 1, :w] + res[BLK:BLK + 1, w:]


def _sb_log_fail(zt, valid):
    lf = -(jnp.maximum(zt, 0.0) + jnp.log1p(jnp.exp(-jnp.abs(zt))))
    return lf if valid is None else jnp.where(valid, lf, 0.0)


def _sb_weights(zt, lf, after, valid):
    w = jnp.exp(zt + lf + after)
    return w if valid is None else jnp.where(valid, w, 0.0)


def _sb_specs():
    return pl.BlockSpec((SEQ, SB_NP * SLAB), lambda b, p: (b, p))


def _carry_spec():
    return pl.BlockSpec((SB_NP * N_BLK, N_BLK, 2 * BLK), lambda b, p: (b * (4 // SB_NP) + p, 0, 0))


def _slab(pp):
    return slice(pp * SLAB, (pp + 1) * SLAB)


def _blk(j):
    return pl.ds(pl.multiple_of(j * BLK, BLK), BLK)


def _causal_t():
    return _row((BLK, 2 * BLK)) < (_lane((BLK, 2 * BLK)) & (BLK - 1))


def sb_fwd(qb, kb, vb):
    def body(q_ref, k_ref, v_ref, o_ref, c_ref, vt_ref, ot_ref):
        for pp in range(SB_NP):
            for j in range(N_BLK):
                vt_ref[pp, j] = v_ref[j * BLK:(j + 1) * BLK, _slab(pp)].T
        lane = _lane((BLK, SLAB))
        tri = _tri(True)
        valid = _causal_t()
        jrow = _row((N_BLK, 2 * BLK))
        chains = range(SB_NP)

        def q_block(i, _):
            q2 = [_stack_heads(q_ref[_blk(i), _slab(pp)], lane) for pp in chains]

            def key_block(j, carry, mask, first):
                zt = [_dot(k_ref[_blk(j), _slab(pp)], q2[pp], NT) for pp in chains]
                lf = [_sb_log_fail(zt[pp], mask) for pp in chains]
                sums = [_key_sums(tri, lf[pp]) for pp in chains]
                w = [_sb_weights(zt[pp], lf[pp], sums[pp][0] + carry[pp], mask) for pp in chains]
                for pp in chains:
                    pv = _dot(vt_ref[pp, j], w[pp].astype(BF16))
                    if first:
                        ot_ref[pp] = pv
                    else:
                        ot_ref[pp] += pv
                return tuple(carry[pp] + sums[pp][1] for pp in chains)

            def earlier(jj, state):
                carry, saved = state
                j = i - 1 - jj
                saved = tuple(jnp.where(jrow == j, carry[pp], saved[pp]) for pp in chains)
                return key_block(j, carry, None, False), saved

            zero = tuple(jnp.zeros((1, 2 * BLK), F32) for _ in chains)
            carry = key_block(i, zero, valid, True)
            _, saved = lax.fori_loop(0, i, earlier, (carry, tuple(jnp.zeros((N_BLK, 2 * BLK), F32) for _ in chains)))
            for pp in chains:
                o_ref[_blk(i), _slab(pp)] = _unstack_heads(ot_ref[pp].T, lane)
                c_ref[pp * N_BLK + i] = saved[pp]
            return 0

        lax.fori_loop(0, N_BLK, q_block, 0)

    spec = _sb_specs()
    return pl.pallas_call(
        body, name="sb_fwd", grid=(B_LOC, 4 // SB_NP),
        in_specs=[spec] * 3, out_specs=[spec, _carry_spec()],
        out_shape=[jax.ShapeDtypeStruct((T_LOC, 512), F32),
                   jax.ShapeDtypeStruct((B_LOC * 4 * N_BLK, N_BLK, 2 * BLK), F32)],
        scratch_shapes=[pltpu.VMEM((SB_NP, N_BLK, SLAB, BLK), BF16), pltpu.VMEM((SB_NP, SLAB, 2 * BLK), F32)],
        compiler_params=_params(("arbitrary", "arbitrary")),
    )(qb, kb, vb)


def out_loss(o_a, o_b, ga, gb, x, target, wout):
    n_tiles = T_LOC // TM

    def body(oa_ref, ob_ref, ga_ref, gb_ref, x_ref, t_ref, w_ref,
             dout_ref, doa_ref, dob_ref, dga_ref, dgb_ref, dw_ref, loss_ref, acc_ref):
        step = pl.program_id(0)

        @pl.when(step == 0)
        def _():
            acc_ref[...] = jnp.zeros_like(acc_ref)
            loss_ref[...] = jnp.zeros_like(loss_ref)

        oa, ob, gav, gbv = oa_ref[...], ob_ref[...], ga_ref[...], gb_ref[...]
        sa, sb = _sigmoid(gav), _sigmoid(gbv)
        silu_a, silu_b = gav * sa, gbv * sb
        y = jnp.concatenate([oa * silu_a, ob * silu_b], axis=1).astype(BF16)
        err = x_ref[...] + _dot(y, w_ref[...]) - t_ref[...]
        e2 = err * err
        part = jnp.sum(e2.reshape(TM // 8, 8, D_MODEL), axis=0)
        loss_ref[...] += functools.reduce(lambda a, b: a + b, [part[:, k * 128:(k + 1) * 128] for k in range(8)])
        dout = err * (1.0 / D_MODEL)
        dout_ref[...] = dout
        dob16 = dout.astype(BF16)
        for r0 in range(0, D_MODEL, ACC_ROWS):
            acc_ref[r0:r0 + ACC_ROWS, :] += _dot(y[:, r0:r0 + ACC_ROWS], dob16, TN)
        dy = _dot(dob16, w_ref[...], NT)
        dya, dyb = dy[:, :512], dy[:, 512:]
        doa_ref[...] = (dya * silu_a).astype(BF16)
        dob_ref[...] = (dyb * silu_b).astype(BF16)
        dga_ref[...] = (dya * oa * (sa * (1.0 + gav * (1.0 - sa)))).astype(BF16)
        dgb_ref[...] = (dyb * ob * (sb * (1.0 + gbv * (1.0 - sb)))).astype(BF16)

        @pl.when(step == n_tiles - 1)
        def _():
            dw_ref[...] = acc_ref[...].astype(BF16)

    def tile(w):
        return pl.BlockSpec((TM, w), lambda i: (i, 0))

    const = lambda i: (0, 0)
    return pl.pallas_call(
        body, name="out_loss", grid=(n_tiles,),
        in_specs=[tile(512)] * 4 + [tile(D_MODEL)] * 2 + [pl.BlockSpec((D_MODEL, D_MODEL), const)],
        out_specs=[tile(D_MODEL), tile(512), tile(512), tile(512), tile(512),
                   pl.BlockSpec((D_MODEL, D_MODEL), const), pl.BlockSpec((8, 128), const)],
        out_shape=[jax.ShapeDtypeStruct((T_LOC, D_MODEL), F32)] + [jax.ShapeDtypeStruct((T_LOC, 512), BF16)] * 4
        + [jax.ShapeDtypeStruct((D_MODEL, D_MODEL), BF16), jax.ShapeDtypeStruct((8, 128), F32)],
        scratch_shapes=[pltpu.VMEM((D_MODEL, D_MODEL), F32)],
        compiler_params=_params(("arbitrary",)),
    )(o_a, o_b, ga, gb, x, target, wout)


def swa_bwd(q_rot, k_dup, v_dup, o_a, d_oa, sinks):
    def body(q_ref, kp_ref, kc_ref, vp_ref, vc_ref, o_ref, do_ref, sinks_ref, dq_ref, dk_ref, dv_ref, dsink_ref):
        b, i = pl.program_id(0), pl.program_id(1)

        @pl.when(i == 0)
        def _():
            dk_ref[...] = jnp.zeros_like(dk_ref)
            dv_ref[...] = jnp.zeros_like(dv_ref)

        @pl.when((i == 0) & (b == 0))
        def _():
            dsink_ref[...] = jnp.zeros_like(dsink_ref)

        lane = _lane((BLK, SLAB))
        rows_prev, rows_cur = _blk(jnp.maximum(i - 1, 0)), _blk(i)
        for p in range(4):
            g = p // 2
            sl = slice(p * SLAB, (p + 1) * SLAB)
            gsl = slice(g * SLAB, (g + 1) * SLAB)
            q2, keys, pn, p_sink = _swa_probs(q_ref, kp_ref, kc_ref, sinks_ref, p, i, lane)
            vals = jnp.concatenate([vp_ref[:, gsl], vc_ref[:, gsl]], axis=0)
            do2 = _stack_heads(do_ref[:, sl], lane)
            o = o_ref[:, sl]
            delta = jnp.sum(do2.astype(F32) * jnp.concatenate([o, o], axis=0), axis=-1, keepdims=True)
            ds = (pn * (_dot(do2, vals, NT) - delta)).astype(BF16)
            dq_ref[:, sl] = _unstack_heads(_dot(ds, keys), lane) * Q_SCALE
            dk2 = _dot(ds, q2, TN)
            dv2 = _dot(pn.astype(BF16), do2, TN)
            dk_ref[rows_prev, gsl] += dk2[:BLK]
            dk_ref[rows_cur, gsl] += dk2[BLK:]
            dv_ref[rows_prev, gsl] += dv2[:BLK]
            dv_ref[rows_cur, gsl] += dv2[BLK:]
            col = -p_sink * delta
            for e in range(2):
                dsink_ref[2 * p + e:2 * p + e + 1, :] += jnp.sum(col[e * BLK:(e + 1) * BLK], axis=0, keepdims=True)

    cur, prev = _swa_specs()
    per_seq = pl.BlockSpec((SEQ, 256), lambda b, i: (b, 0))
    return pl.pallas_call(
        body, name="swa_bwd", grid=(B_LOC, N_BLK),
        in_specs=[cur(512), prev(256), cur(256), prev(256), cur(256), cur(512), cur(512),
                  pl.BlockSpec(memory_space=pltpu.SMEM)],
        out_specs=[cur(512), per_seq, per_seq, pl.BlockSpec((8, 128), lambda b, i: (0, 0))],
        out_shape=[jax.ShapeDtypeStruct((T_LOC, 512), F32), jax.ShapeDtypeStruct((T_LOC, 256), F32),
                   jax.ShapeDtypeStruct((T_LOC, 256), F32), jax.ShapeDtypeStruct((8, 128), F32)],
        compiler_params=_params(("arbitrary", "arbitrary")),
    )(q_rot, k_dup, k_dup, v_dup, v_dup, o_a, d_oa, sinks)


def sb_bwd(qb, kb, vb, d_ob, carries):
    def body(q_ref, k_ref, v_ref, do_ref, c_ref, dq_ref, dk_ref, dv_ref, kt_ref, dqt_ref):
        for pp in range(SB_NP):
            for j in range(N_BLK):
                kt_ref[pp, j] = k_ref[j * BLK:(j + 1) * BLK, _slab(pp)].T
        dk_ref[...] = jnp.zeros_like(dk_ref)
        dv_ref[...] = jnp.zeros_like(dv_ref)
        dqt_ref[...] = jnp.zeros_like(dqt_ref)
        lane = _lane((BLK, SLAB))
        tri_after, tri_before = _tri(True), _tri(False)
        valid = _causal_t()
        jrow = _row((N_BLK, 2 * BLK))
        chains = range(SB_NP)

        def q_block(i, _):
            q2 = [_stack_heads(q_ref[_blk(i), _slab(pp)], lane) for pp in chains]
            do2 = [_stack_heads(do_ref[_blk(i), _slab(pp)], lane) for pp in chains]

            def key_block(j, carry_lf, before_u, mask):
                rows = _blk(j)
                zt = [_dot(k_ref[rows, _slab(pp)], q2[pp], NT) for pp in chains]
                dw = [_dot(v_ref[rows, _slab(pp)], do2[pp], NT) for pp in chains]
                lf = [_sb_log_fail(zt[pp], mask) for pp in chains]
                after = [_key_sums(tri_after, lf[pp])[0] for pp in chains]
                w = [_sb_weights(zt[pp], lf[pp], after[pp] + carry_lf[pp], mask) for pp in chains]
                u = [dw[pp] * w[pp] for pp in chains]
                for pp in chains:
                    dv_ref[rows, _slab(pp)] += _dot(w[pp].astype(BF16), do2[pp])
                sums = [_key_sums(tri_before, u[pp]) for pp in chains]
                dz16 = []
                for pp in chains:
                    sig = jnp.exp(zt[pp] + lf[pp])
                    dz = u[pp] * (1.0 - sig) - (before_u[pp] + sums[pp][0]) * sig
                    if mask is not None:
                        dz = jnp.where(mask, dz, 0.0)
                    dz16.append(dz.astype(BF16))
                for pp in chains:
                    dk_ref[rows, _slab(pp)] += _dot(dz16[pp], q2[pp])
                    dqt_ref[pp] += _dot(kt_ref[pp, j], dz16[pp])
                return tuple(before_u[pp] + sums[pp][1] for pp in chains)

            def earlier(j, before_u):
                carry_lf = [jnp.sum(jnp.where(jrow == j, c_ref[pp * N_BLK + i], 0.0), axis=0, keepdims=True)
                            for pp in chains]
                return key_block(j, carry_lf, before_u, None)

            zero = tuple(jnp.zeros((1, 2 * BLK), F32) for _ in chains)
            before_u = lax.fori_loop(0, i, earlier, zero)
            key_block(i, zero, before_u, valid)
            for pp in chains:
                dq_ref[_blk(i), _slab(pp)] = _unstack_heads(dqt_ref[pp].T, lane) * Q_SCALE
                dqt_ref[pp] = jnp.zeros((SLAB, 2 * BLK), F32)
            return 0

        lax.fori_loop(0, N_BLK, q_block, 0)

    spec = _sb_specs()
    return pl.pallas_call(
        body, name="sb_bwd", grid=(B_LOC, 4 // SB_NP),
        in_specs=[spec] * 4 + [_carry_spec()],
        out_specs=[spec] * 3,
        out_shape=[jax.ShapeDtypeStruct((T_LOC, 512), F32)] * 3,
        scratch_shapes=[pltpu.VMEM((SB_NP, N_BLK, SLAB, BLK), BF16), pltpu.VMEM((SB_NP, SLAB, 2 * BLK), F32)],
        compiler_params=_params(("arbitrary", "arbitrary")),
    )(qb, kb, vb, d_ob, carries)


def bwd_in(x, dout, norm_gain, win_t, dq_rot, dk_dup, dv_dup, qa_raw, ka_raw, cos, sin_s, q_gain2, k_gain2,
           dga, dgb, dqb, dkb, dvb):
    n_tiles = T_LOC // TM

    def body(x_ref, dout_ref, ng_ref, w_hbm, dq_ref, dk_ref, dv_ref, qa_ref, ka_ref, cos_ref, sin_ref, qg_ref, kg_ref,
             dga_ref, dgb_ref, dqb_ref, dkb_ref, dvb_ref,
             gx_ref, dw_hbm, dng_ref, dqg_ref, dkg_ref, w_ref, acc_ref, stage_ref, w_sem):
        step = pl.program_id(0)

        @pl.when(step == 0)
        def _():
            cp = pltpu.make_async_copy(w_hbm, w_ref, w_sem)
            cp.start()
            acc_ref[...] = jnp.zeros_like(acc_ref)
            dng_ref[...] = jnp.zeros_like(dng_ref)
            dqg_ref[...] = jnp.zeros_like(dqg_ref)
            dkg_ref[...] = jnp.zeros_like(dkg_ref)
            cp.wait()

        lane = _lane((TM, SLAB))
        bd = _head_blockdiag()
        cos, sin_s = cos_ref[...], sin_ref[...]

        def norm_rope_bwd(d_rot, raw, gain2):
            dy = d_rot * cos + _swap_half(d_rot * sin_s, lane)
            r = lax.rsqrt(_head_sum(raw * raw, bd) * (1.0 / HEAD_DIM) + EPS)
            xhat = raw * r
            dgain = jnp.sum(dy * xhat, axis=0, keepdims=True)
            dxh = dy * gain2
            mean = _head_sum(dxh * xhat, bd) * (1.0 / HEAD_DIM)
            return r * (dxh - xhat * mean), dgain

        def fold_dup(d_dup):
            a, b2 = d_dup[:, :SLAB], d_dup[:, SLAB:]
            return jnp.where(lane < HEAD_DIM, a + pltpu.roll(a, HEAD_DIM, 1), b2 + pltpu.roll(b2, HEAD_DIM, 1))

        pieces = []
        dqg = jnp.zeros((1, SLAB), F32)
        for p in range(4):
            sl = slice(p * SLAB, (p + 1) * SLAB)
            d_raw, dg = norm_rope_bwd(dq_ref[:, sl], qa_ref[:, sl], qg_ref[...])
            pieces.append(d_raw.astype(BF16))
            dqg = dqg + dg
        d_raw, dkg = norm_rope_bwd(fold_dup(dk_ref[...]), ka_ref[...], kg_ref[...])
        pieces.append(d_raw.astype(BF16))
        pieces.append(fold_dup(dv_ref[...]).astype(BF16))
        pieces += [dga_ref[...], dqb_ref[...].astype(BF16), dkb_ref[...].astype(BF16), dvb_ref[...].astype(BF16),
                   dgb_ref[...]]
        dproj = jnp.concatenate(pieces, axis=1)
        dqg_ref[0:1, :] += dqg + pltpu.roll(dqg, HEAD_DIM, 1)
        dkg_ref[0:1, :] += dkg + pltpu.roll(dkg, HEAD_DIM, 1)

        xv = x_ref[...]
        rstd = lax.rsqrt(jnp.mean(xv * xv, axis=-1, keepdims=True) + EPS)
        xhat = xv * rstd
        gain = ng_ref[...]
        h = (xhat * gain).astype(BF16)
        for r0 in range(0, IN_WIDTH, ACC_ROWS):
            acc_ref[r0:r0 + ACC_ROWS, :] += _dot(dproj[:, r0:r0 + ACC_ROWS], h, TN)
        dh = _dot(dproj, w_ref[...])
        dng_ref[0:1, :] += jnp.sum(dh * xhat, axis=0, keepdims=True)
        dxh = dh * gain
        gx_ref[...] = dout_ref[...] + rstd * (dxh - xhat * jnp.mean(dxh * xhat, axis=-1, keepdims=True))

        @pl.when(step == n_tiles - 1)
        def _():
            for r0 in range(0, IN_WIDTH, ACC_ROWS):
                stage_ref[...] = acc_ref[r0:r0 + ACC_ROWS, :].astype(BF16)
                pltpu.sync_copy(stage_ref, dw_hbm.at[r0:r0 + ACC_ROWS, :])

    def tile(w):
        return pl.BlockSpec((TM, w), lambda i: (i, 0))

    def whole(a):
        return pl.BlockSpec(a.shape, lambda i: (0, 0))

    const = lambda i: (0, 0)
    return pl.pallas_call(
        body, name="bwd_in", grid=(n_tiles,),
        in_specs=[tile(D_MODEL), tile(D_MODEL), whole(norm_gain), pl.BlockSpec(memory_space=pl.ANY),
                  tile(512), tile(256), tile(256), tile(512), tile(128), tile(128), tile(128),
                  whole(q_gain2), whole(k_gain2), tile(512), tile(512), tile(512), tile(512), tile(512)],
        out_specs=[tile(D_MODEL), pl.BlockSpec(memory_space=pl.ANY), pl.BlockSpec((8, D_MODEL), const),
                   pl.BlockSpec((8, SLAB), const), pl.BlockSpec((8, SLAB), const)],
        out_shape=[jax.ShapeDtypeStruct((T_LOC, D_MODEL), F32), jax.ShapeDtypeStruct((IN_WIDTH, D_MODEL), BF16),
                   jax.ShapeDtypeStruct((8, D_MODEL), F32), jax.ShapeDtypeStruct((8, SLAB), F32),
                   jax.ShapeDtypeStruct((8, SLAB), F32)],
        scratch_shapes=[pltpu.VMEM((IN_WIDTH, D_MODEL), BF16), pltpu.VMEM((IN_WIDTH, D_MODEL), F32),
                        pltpu.VMEM((ACC_ROWS, D_MODEL), BF16), pltpu.SemaphoreType.DMA],
        compiler_params=_params(("arbitrary",)),
    )(x, dout, norm_gain, win_t, dq_rot, dk_dup, dv_dup, qa_raw, ka_raw, cos, sin_s, q_gain2, k_gain2,
      dga, dgb, dqb, dkb, dvb)


def _adamw(w, g, m, v):
    m = ADAM_B1 * m + (1.0 - ADAM_B1) * g
    v = ADAM_B2 * v + (1.0 - ADAM_B2) * (g * g)
    m_hat = m / (1.0 - ADAM_B1 ** ADAM_STEP)
    v_hat = v / (1.0 - ADAM_B2 ** ADAM_STEP)
    delta = -ADAM_LR * (m_hat / (jnp.sqrt(v_hat) + ADAM_EPS) + ADAM_WD * w)
    return delta, m, v


def _sum_slots(r_ref):
    g = r_ref[0].astype(F32)
    for s in range(1, N_DEV):
        g = g + r_ref[s].astype(F32)
    return g


def adamw_rows(name, recv, w, m, v, rows_per_step):
    n_rows, n_cols = w.shape

    def body(r_ref, w_ref, m_ref, v_ref, g_ref, d_ref, nm_ref, nv_ref):
        g = _sum_slots(r_ref)
        g_ref[...] = g
        d_ref[...], nm_ref[...], nv_ref[...] = _adamw(w_ref[...], g, m_ref[...], v_ref[...])

    spec = pl.BlockSpec((rows_per_step, n_cols), lambda i: (i, 0))
    return pl.pallas_call(
        body, name=name, grid=(n_rows // rows_per_step,),
        in_specs=[pl.BlockSpec((N_DEV, rows_per_step, n_cols), lambda i: (0, i, 0)), spec, spec, spec],
        out_specs=[spec] * 4,
        out_shape=[jax.ShapeDtypeStruct((n_rows, n_cols), F32)] * 4,
        compiler_params=_params(("arbitrary",)),
    )(recv, w, m, v)


def adamw_small(recv, w_pack, m_pack, v_pack):
    def body(r_ref, w_ref, m_ref, v_ref, g_ref, d_ref, nm_ref, nv_ref, loss_ref):
        s = _sum_slots(r_ref)
        eye = (_row((8, SLAB)) == _lane((8, SLAB))).astype(F32)
        sinks = jnp.sum(s[:, 1280:1408] * eye, axis=0, keepdims=True)
        pad = jnp.zeros((1, D_MODEL - SLAB), F32)
        g = jnp.concatenate([
            s[0:1, :D_MODEL],
            jnp.concatenate([s[0:1, 1024:1152], pad], axis=1),
            jnp.concatenate([s[0:1, 1152:1280], pad], axis=1),
            jnp.concatenate([sinks, pad], axis=1),
            jnp.zeros((4, D_MODEL), F32)], axis=0)
        g_ref[...] = g
        d_ref[...], nm_ref[...], nv_ref[...] = _adamw(w_ref[...], g, m_ref[...], v_ref[...])
        loss = jnp.sum(jnp.sum(s[:, 1408:1536], axis=1, keepdims=True), axis=0, keepdims=True) * (0.5 / D_MODEL)
        loss_ref[...] = jnp.broadcast_to(loss, (8, SLAB))

    return pl.pallas_call(
        body, name="adamw_small",
        out_shape=[jax.ShapeDtypeStruct((8, D_MODEL), F32)] * 4 + [jax.ShapeDtypeStruct((8, SLAB), F32)],
        compiler_params=pltpu.CompilerParams(vmem_limit_bytes=VMEM_LIMIT),
    )(recv, w_pack, m_pack, v_pack)


def _pack_small(ng, qg, kg, sk):
    def row(a):
        return jnp.pad(a, ((0, 0), (0, D_MODEL - a.shape[1])))
    return jnp.concatenate([row(ng), row(qg), row(kg), row(sk), jnp.zeros((4, D_MODEL), F32)], axis=0)


def kernel(x, positions, norm_gain, w_in, q_norm_gain, k_norm_gain, sinks, w_out, loss_target, m_norm_gain, m_w_in, m_q_norm_gain, m_k_norm_gain, m_sinks, m_w_out, v_norm_gain, v_w_in, v_q_norm_gain, v_k_norm_gain, v_sinks, v_w_out):
    x2 = x.reshape(T_LOC, D_MODEL)
    tgt2 = loss_target.reshape(T_LOC, D_MODEL)
    pos2 = positions.reshape(T_LOC, 1)
    half = HEAD_DIM // 2
    inv_freq = ROPE_THETA ** (-jnp.arange(half, dtype=F32) * 2.0 / HEAD_DIM)
    inv_freq = jnp.tile(inv_freq, SLAB // half).reshape(1, SLAB)
    sin_sign = jnp.tile(jnp.concatenate([-jnp.ones((half,), F32), jnp.ones((half,), F32)]), 2).reshape(1, SLAB)
    q_gain2 = jnp.tile(q_norm_gain, (1, 2))
    k_gain2 = jnp.tile(k_norm_gain, (1, 2))

    win_t, wout = gather_weights(w_in[0].T.astype(BF16), w_out[0].astype(BF16))

    (qa_raw, ka_raw, q_rot, k_dup, v_dup, ga, qb, kb, vb, gb, cos, sin_s) = fwd_proj(
        x2, pos2, norm_gain, win_t, inv_freq, sin_sign, q_gain2, k_gain2)
    o_a = swa_fwd(q_rot, k_dup, v_dup, sinks)
    o_b, carries = sb_fwd(qb, kb, vb)
    dout, d_oa, d_ob, dga, dgb, dwout, loss_part = out_loss(o_a, o_b, ga, gb, x2, tgt2, wout)
    dq_rot, dk_dup, dv_dup, dsink = swa_bwd(q_rot, k_dup, v_dup, o_a, d_oa, sinks)
    dqb, dkb, dvb = sb_bwd(qb, kb, vb, d_ob, carries)
    grad_x, dwin_t, dng, dqg, dkg = bwd_in(
        x2, dout, norm_gain, win_t, dq_rot, dk_dup, dv_dup, qa_raw, ka_raw, cos, sin_s, q_gain2, k_gain2,
        dga, dgb, dqb, dkb, dvb)

    small = jnp.concatenate([dng, dqg, dkg, dsink, loss_part], axis=1)
    r_win, r_out, r_small = exchange_grads(dwin_t, dwout, small)

    g_win_t, d_win_t, nm_win_t, nv_win_t = adamw_rows(
        "adamw_w_in", r_win, w_in[0].T, m_w_in[0].T, v_w_in[0].T, IN_SHARD // 2)
    g_wout, d_wout, nm_wout, nv_wout = adamw_rows("adamw_w_out", r_out, w_out[0], m_w_out[0], v_w_out[0], OUT_SHARD)
    g_s, d_s, nm_s, nv_s, loss = adamw_small(
        r_small, _pack_small(norm_gain, q_norm_gain, k_norm_gain, sinks),
        _pack_small(m_norm_gain, m_q_norm_gain, m_k_norm_gain, m_sinks),
        _pack_small(v_norm_gain, v_q_norm_gain, v_k_norm_gain, v_sinks))

    def unpack(p, big_in_t, big_out):
        return (p[0:1, :], big_in_t.T[None], p[1:2, :HEAD_DIM], p[2:3, :HEAD_DIM], p[3:4, :8], big_out[None])

    return (loss[0, 0], grad_x.reshape(B_LOC, SEQ, D_MODEL),
            *unpack(g_s, g_win_t, g_wout), *unpack(d_s, d_win_t, d_wout),
            *unpack(nm_s, nm_win_t, nm_wout), *unpack(nv_s, nv_win_t, nv_wout))
```

```python
import functools

import jax
import jax.numpy as jnp
from jax import lax
from jax.experimental import pallas as pl
from jax.experimental.pallas import tpu as pltpu

F32 = jnp.float32
BF16 = jnp.bfloat16

N_DEV = 8
D_MODEL = 1024
SEQ = 2048
B_LOC = 2
T_LOC = B_LOC * SEQ
HEAD_DIM = 64
HEAD_SHIFT = 6
BLK = 128
N_BLK = SEQ // BLK
SLAB = 128
IN_WIDTH = 3328
IN_SHARD = IN_WIDTH // N_DEV
OUT_SHARD = D_MODEL // N_DEV
EPS = 1e-6
ROPE_THETA = 10000.0
Q_SCALE = 0.125
R_QA, R_KA, R_VA, R_GA, R_QB, R_KB, R_VB, R_GB, R_END = 0, 512, 640, 768, 1280, 1792, 2304, 2816, 3328
SMALL_W = 1536
ADAM_LR, ADAM_B1, ADAM_B2, ADAM_EPS, ADAM_WD, ADAM_STEP = 0.001, 0.9, 0.999, 1e-08, 0.01, 10
TM = 256
ACC_ROWS = 256
VMEM_LIMIT = 56 * 1024 * 1024

MESH = pl.DeviceIdType.MESH
NT = (((1,), (1,)), ((), ()))
TN = (((0,), (0,)), ((), ()))


def _params(sem, limit=VMEM_LIMIT):
    return pltpu.CompilerParams(dimension_semantics=sem, vmem_limit_bytes=limit)


def _dot(a, b, dims=None):
    if dims is None:
        return jnp.dot(a, b, preferred_element_type=F32)
    return lax.dot_general(a, b, dims, preferred_element_type=F32)


def _split(x):
    hi = x.astype(BF16)
    return hi, (x - hi.astype(F32)).astype(BF16)


def _lane(shape):
    return lax.broadcasted_iota(jnp.int32, shape, len(shape) - 1)


def _row(shape):
    return lax.broadcasted_iota(jnp.int32, shape, 0)


def _head_blockdiag():
    return ((_row((SLAB, SLAB)) >> HEAD_SHIFT) == (_lane((SLAB, SLAB)) >> HEAD_SHIFT)).astype(BF16)


def _head_sum(x, bd):
    hi, lo = _split(x)
    return _dot(hi, bd) + _dot(lo, bd)


def _swap_half(y, lane):
    return jnp.where((lane & 32) != 0, pltpu.roll(y, 32, 1), pltpu.roll(y, 96, 1))


def _stack_heads(q, lane):
    zero = jnp.zeros_like(q)
    return jnp.concatenate([jnp.where(lane < HEAD_DIM, q, zero), jnp.where(lane >= HEAD_DIM, q, zero)], axis=0)


def _unstack_heads(x2, lane):
    return jnp.where(lane < HEAD_DIM, x2[:BLK], x2[BLK:])


def _sigmoid(x):
    return 1.0 / (1.0 + jnp.exp(-x))


def _mesh_pos():
    return lax.axis_index("x"), lax.axis_index("y"), lax.axis_index("c")


def _flip(pos, mask):
    return tuple(1 - p if m else p for p, m in zip(pos, mask))


def _lin(pos):
    return 4 * pos[0] + 2 * pos[1] + pos[2]


def gather_weights(win_t_shard, wout_shard):
    shards = (win_t_shard, wout_shard)
    n_arr = len(shards)

    def body(a_ref, b_ref, oa_ref, ob_ref, send_sems, recv_sems, local_sems):
        x, y, c = _mesh_pos()
        me, sibling = (x, y, c), (x, y, 1 - c)
        chips = [(1 - x, y), (x, 1 - y), (1 - x, 1 - y)]
        ins, outs = (a_ref, b_ref), (oa_ref, ob_ref)

        def rows(a, pos):
            m = ins[a].shape[0]
            return outs[a].at[pl.ds(_lin(pos) * m, m), :]

        def copy(a, k, block, to, src=None):
            return pltpu.make_async_remote_copy(
                src_ref=rows(a, block) if src is None else src, dst_ref=rows(a, block),
                send_sem=send_sems.at[a, k], recv_sem=recv_sems.at[a, k], device_id=to, device_id_type=MESH)

        mine = [pltpu.make_async_copy(ins[a], rows(a, me), local_sems.at[a]) for a in range(n_arr)]
        for cp in mine:
            cp.start()
        first = []
        for a in range(n_arr):
            first.append(copy(a, 0, me, sibling, src=ins[a]))
            first += [copy(a, 1 + j, me, (*chip, c), src=ins[a]) for j, chip in enumerate(chips)]
        for cp in first:
            cp.start()
        passed = [[copy(a, 4 + j, (*chip, c), sibling) for j, chip in enumerate(chips)] for a in range(n_arr)]
        for j, chip in enumerate(chips):
            for a in range(n_arr):
                copy(a, 1 + j, (*chip, c), me).wait_recv()
                passed[a][j].start()
        for a in range(n_arr):
            copy(a, 0, sibling, me).wait_recv()
            for j, chip in enumerate(chips):
                copy(a, 4 + j, (*chip, 1 - c), me).wait_recv()
        for cp in first + [p for ps in passed for p in ps]:
            cp.wait_send()
        for cp in mine:
            cp.wait()

    vmem = pl.BlockSpec(memory_space=pltpu.VMEM)
    return pl.pallas_call(
        body, name="gather_weights",
        out_shape=[jax.ShapeDtypeStruct((N_DEV * s.shape[0], s.shape[1]), s.dtype) for s in shards],
        in_specs=[vmem] * n_arr, out_specs=[vmem] * n_arr,
        scratch_shapes=[pltpu.SemaphoreType.DMA((n_arr, 7)), pltpu.SemaphoreType.DMA((n_arr, 7)),
                        pltpu.SemaphoreType.DMA((n_arr,))],
        compiler_params=pltpu.CompilerParams(vmem_limit_bytes=VMEM_LIMIT),
    )(*shards)


def exchange_grads(dwin_t, dwout, small):
    srcs = (dwin_t, dwout, small)
    blocks = (IN_SHARD, OUT_SHARD, small.shape[0])
    scattered = (True, True, False)
    n_arr = len(srcs)
    masks = [(mx, my, mc) for mx in (0, 1) for my in (0, 1) for mc in (0, 1)][1:]

    def body(a_ref, b_ref, s_ref, ra_ref, rb_ref, rs_ref, send_sems, recv_sems, local_sems):
        me = _mesh_pos()
        ins, outs = (a_ref, b_ref, s_ref), (ra_ref, rb_ref, rs_ref)

        def block_for(a, pos):
            if scattered[a]:
                return ins[a].at[pl.ds(_lin(pos) * blocks[a], blocks[a]), :]
            return ins[a]

        def copy(a, k, to):
            return pltpu.make_async_remote_copy(
                src_ref=block_for(a, to), dst_ref=outs[a].at[_lin(me)],
                send_sem=send_sems.at[a, k], recv_sem=recv_sems.at[a, k], device_id=to, device_id_type=MESH)

        def landed(a, k, frm):
            return pltpu.make_async_remote_copy(
                src_ref=block_for(a, frm), dst_ref=outs[a].at[_lin(frm)],
                send_sem=send_sems.at[a, k], recv_sem=recv_sems.at[a, k], device_id=frm, device_id_type=MESH)

        mine = [pltpu.make_async_copy(block_for(a, me), outs[a].at[_lin(me)], local_sems.at[a]) for a in range(n_arr)]
        for cp in mine:
            cp.start()
        sent = [copy(a, k, _flip(me, mask)) for k, mask in enumerate(masks) for a in range(n_arr)]
        for cp in sent:
            cp.start()
        for k, mask in enumerate(masks):
            for a in range(n_arr):
                landed(a, k, _flip(me, mask)).wait_recv()
        for cp in sent:
            cp.wait_send()
        for cp in mine:
            cp.wait()

    hbm = pl.BlockSpec(memory_space=pl.ANY)
    return pl.pallas_call(
        body, name="exchange_grads",
        out_shape=[jax.ShapeDtypeStruct((N_DEV, blocks[a], srcs[a].shape[1]), srcs[a].dtype) for a in range(n_arr)],
        in_specs=[hbm] * n_arr, out_specs=[hbm] * n_arr,
        scratch_shapes=[pltpu.SemaphoreType.DMA((n_arr, 7)), pltpu.SemaphoreType.DMA((n_arr, 7)),
                        pltpu.SemaphoreType.DMA((n_arr,))],
    )(*srcs)


def _norm_rope(xs, gain2, cos, sin_s, bd, lane):
    r = lax.rsqrt(_head_sum(xs * xs, bd) * (1.0 / HEAD_DIM) + EPS)
    y = xs * r * gain2
    return y * cos + _swap_half(y, lane) * sin_s


def _dup_heads(xs, lane):
    r = pltpu.roll(xs, HEAD_DIM, 1)
    lo = lane < HEAD_DIM
    return jnp.concatenate([jnp.where(lo, xs, r), jnp.where(lo, r, xs)], axis=1)


def fwd_proj(x, pos, norm_gain, win_t, inv_freq, sin_sign, q_gain2, k_gain2):
    def body(x_ref, pos_ref, ng_ref, w_ref, if_ref, sg_ref, qg_ref, kg_ref,
             qa_raw_ref, ka_raw_ref, q_rot_ref, k_dup_ref, v_dup_ref, ga_ref, qb_ref, kb_ref, vb_ref, gb_ref,
             cos_ref, sin_ref):
        xv = x_ref[...]
        rstd = lax.rsqrt(jnp.mean(xv * xv, axis=-1, keepdims=True) + EPS)
        h = (xv * rstd * ng_ref[...]).astype(BF16)

        def proj(r0, r1):
            return _dot(h, w_ref[r0:r1, :], NT)

        ang = pos_ref[...].astype(F32) * if_ref[...]
        cos = jnp.cos(ang)
        sin_s = jnp.sin(ang) * sg_ref[...]
        cos_ref[...] = cos
        sin_ref[...] = sin_s
        lane = _lane((TM, SLAB))
        bd = _head_blockdiag()

        qa = proj(R_QA, R_KA)
        qa_raw_ref[...] = qa
        for p in range(4):
            sl = slice(p * SLAB, (p + 1) * SLAB)
            q_rot_ref[:, sl] = (_norm_rope(qa[:, sl], qg_ref[...], cos, sin_s, bd, lane) * Q_SCALE).astype(BF16)
        ka = proj(R_KA, R_VA)
        ka_raw_ref[...] = ka
        k_dup_ref[...] = _dup_heads(_norm_rope(ka, kg_ref[...], cos, sin_s, bd, lane), lane).astype(BF16)
        v_dup_ref[...] = _dup_heads(proj(R_VA, R_GA), lane).astype(BF16)
        ga_ref[...] = proj(R_GA, R_QB)
        qb_ref[...] = (proj(R_QB, R_KB) * Q_SCALE).astype(BF16)
        kb_ref[...] = proj(R_KB, R_VB).astype(BF16)
        vb_ref[...] = proj(R_VB, R_GB).astype(BF16)
        gb_ref[...] = proj(R_GB, R_END)

    def tile(w):
        return pl.BlockSpec((TM, w), lambda i: (i, 0))

    def whole(a):
        return pl.BlockSpec(a.shape, lambda i: (0, 0))

    widths = [(512, F32), (128, F32), (512, BF16), (256, BF16), (256, BF16), (512, F32), (512, BF16), (512, BF16),
              (512, BF16), (512, F32), (128, F32), (128, F32)]
    return pl.pallas_call(
        body, name="fwd_proj", grid=(T_LOC // TM,),
        in_specs=[tile(D_MODEL), tile(1), whole(norm_gain), whole(win_t), whole(inv_freq), whole(sin_sign),
                  whole(q_gain2), whole(k_gain2)],
        out_specs=[tile(w) for w, _ in widths],
        out_shape=[jax.ShapeDtypeStruct((T_LOC, w), dt) for w, dt in widths],
        compiler_params=_params(("arbitrary",)),
    )(x, pos, norm_gain, win_t, inv_freq, sin_sign, q_gain2, k_gain2)


def _swa_window(prev_ref, cur_ref, p):
    gsl = _slab(p // 2)
    return jnp.concatenate([prev_ref[:, gsl], cur_ref[:, gsl]], axis=0)


def _swa_probs(s, sinks_ref, p, i):
    shape = (2 * BLK, 2 * BLK)
    r = _row(shape) & (BLK - 1)
    cidx = _lane(shape)
    valid = (cidx > r) & (cidx <= r + BLK) & ((cidx >= BLK) | (i > 0))
    s = jnp.where(valid, s, -jnp.inf)
    sink = jnp.where(_row((2 * BLK, 1)) < BLK, sinks_ref[0, 2 * p], sinks_ref[0, 2 * p + 1])
    m = jnp.maximum(jnp.max(s, axis=-1, keepdims=True), sink)
    e = jnp.exp(s - m)
    e_sink = jnp.exp(sink - m)
    den = jnp.sum(e, axis=-1, keepdims=True) + e_sink
    return e / den, e_sink / den


def _swa_specs():
    def cur(w):
        return pl.BlockSpec((BLK, w), lambda b, i: (b * N_BLK + i, 0))

    def prev(w):
        return pl.BlockSpec((BLK, w), lambda b, i: (b * N_BLK + jnp.maximum(i - 1, 0), 0))

    return cur, prev


def swa_fwd(q_rot, k_dup, v_dup, sinks):
    def body(q_ref, kp_ref, kc_ref, vp_ref, vc_ref, sinks_ref, o_ref):
        i = pl.program_id(1)
        lane = _lane((BLK, SLAB))
        pairs = range(4)
        s = [_dot(_stack_heads(q_ref[:, _slab(p)], lane), _swa_window(kp_ref, kc_ref, p), NT) for p in pairs]
        pn = [_swa_probs(s[p], sinks_ref, p, i)[0].astype(BF16) for p in pairs]
        for p in pairs:
            o_ref[:, _slab(p)] = _unstack_heads(_dot(pn[p], _swa_window(vp_ref, vc_ref, p)), lane)

    cur, prev = _swa_specs()
    return pl.pallas_call(
        body, name="swa_fwd", grid=(B_LOC, N_BLK),
        in_specs=[cur(512), prev(256), cur(256), prev(256), cur(256), pl.BlockSpec(memory_space=pltpu.SMEM)],
        out_specs=cur(512),
        out_shape=jax.ShapeDtypeStruct((T_LOC, 512), F32),
        compiler_params=_params(("arbitrary", "arbitrary")),
    )(q_rot, k_dup, k_dup, v_dup, v_dup, sinks)


def _tri(suffix):
    r, cidx = _row((BLK + 16, 2 * BLK)), _lane((BLK + 16, 2 * BLK)) & (BLK - 1)
    tri = (cidx > r) if suffix else (cidx < r)
    return (tri | (r >= BLK)).astype(BF16)


def _key_sums(tri, x):
    hi, lo = _split(x)
    res = _dot(tri, jnp.concatenate([hi, lo], axis=0))
    return res[:BLK], res[BLK:BLK + 1]


def _sb_softplus(zt, valid):
    neg_abs = lax.bitcast_convert_type(lax.bitcast_convert_type(zt, jnp.uint32) | jnp.uint32(0x80000000), F32)
    sp = jnp.maximum(zt, 0.0) + jnp.log(1.0 + jnp.exp(neg_abs))
    return sp if valid is None else jnp.where(valid, sp, 0.0)


def _sb_weights(zt, sp, later, valid):
    w = jnp.exp(zt - sp - later)
    return w if valid is None else jnp.where(valid, w, 0.0)


def _slab(pp):
    return slice(pp * SLAB, (pp + 1) * SLAB)


def _blk(j):
    return pl.ds(pl.multiple_of(j * BLK, BLK), BLK)


def _causal_t():
    return _row((BLK, 2 * BLK)) < (_lane((BLK, 2 * BLK)) & (BLK - 1))


def _sb_rows(b, j):
    return pl.ds(pl.multiple_of(b * SEQ + j * BLK, BLK), BLK)


SB_CHAINS = [(b, pp) for b in range(B_LOC) for pp in range(4)]


def sb_fwd(qb, kb, vb):
    def body(q_ref, k_ref, v_ref, o_ref, c_ref, vt_ref, ot_ref):
        for c, (b, pp) in enumerate(SB_CHAINS):
            for j in range(N_BLK):
                vt_ref[c, j] = v_ref[b * SEQ + j * BLK:b * SEQ + (j + 1) * BLK, _slab(pp)].T
        lane = _lane((BLK, SLAB))
        tri = _tri(True)
        valid = _causal_t()
        jrow = _row((N_BLK, 2 * BLK))
        chains = range(len(SB_CHAINS))

        def q_block(i, _):
            q2 = [_stack_heads(q_ref[_sb_rows(b, i), _slab(pp)], lane) for b, pp in SB_CHAINS]

            def key_block(j, carry, mask, first):
                zt = [_dot(k_ref[_sb_rows(b, j), _slab(pp)], q2[c], NT) for c, (b, pp) in enumerate(SB_CHAINS)]
                sp = [_sb_softplus(zt[c], mask) for c in chains]
                sums = [_key_sums(tri, sp[c]) for c in chains]
                w = [_sb_weights(zt[c], sp[c], sums[c][0] + carry[c], mask) for c in chains]
                for c in chains:
                    pv = _dot(vt_ref[c, j], w[c].astype(BF16))
                    if first:
                        ot_ref[c] = pv
                    else:
                        ot_ref[c] += pv
                return tuple(carry[c] + sums[c][1] for c in chains)

            def earlier(jj, state):
                carry, saved = state
                j = i - 1 - jj
                saved = tuple(jnp.where(jrow == j, carry[c], saved[c]) for c in chains)
                return key_block(j, carry, None, False), saved

            zero = tuple(jnp.zeros((1, 2 * BLK), F32) for _ in chains)
            carry = key_block(i, zero, valid, True)
            _, saved = lax.fori_loop(0, i, earlier, (carry, tuple(jnp.zeros((N_BLK, 2 * BLK), F32) for _ in chains)))
            for c, (b, pp) in enumerate(SB_CHAINS):
                o_ref[_sb_rows(b, i), _slab(pp)] = _unstack_heads(ot_ref[c].T, lane)
                c_ref[c * N_BLK + i] = saved[c]
            return 0

        lax.fori_loop(0, N_BLK, q_block, 0)

    n_ch = len(SB_CHAINS)
    vmem = pl.BlockSpec(memory_space=pltpu.VMEM)
    return pl.pallas_call(
        body, name="sb_fwd",
        in_specs=[vmem] * 3, out_specs=[vmem] * 2,
        out_shape=[jax.ShapeDtypeStruct((T_LOC, 512), F32), jax.ShapeDtypeStruct((n_ch * N_BLK, N_BLK, 2 * BLK), F32)],
        scratch_shapes=[pltpu.VMEM((n_ch, N_BLK, SLAB, BLK), BF16), pltpu.VMEM((n_ch, SLAB, 2 * BLK), F32)],
        compiler_params=pltpu.CompilerParams(vmem_limit_bytes=VMEM_LIMIT),
    )(qb, kb, vb)


def out_loss(o_a, o_b, ga, gb, x, target, wout):
    n_tiles = T_LOC // TM

    def body(oa_ref, ob_ref, ga_ref, gb_ref, x_ref, t_ref, w_ref,
             dout_ref, doa_ref, dob_ref, dga_ref, dgb_ref, dw_ref, loss_ref, acc_ref):
        step = pl.program_id(0)

        @pl.when(step == 0)
        def _():
            acc_ref[...] = jnp.zeros_like(acc_ref)
            loss_ref[...] = jnp.zeros_like(loss_ref)

        oa, ob, gav, gbv = oa_ref[...], ob_ref[...], ga_ref[...], gb_ref[...]
        sa, sb = _sigmoid(gav), _sigmoid(gbv)
        silu_a, silu_b = gav * sa, gbv * sb
        y = jnp.concatenate([oa * silu_a, ob * silu_b], axis=1).astype(BF16)
        err = x_ref[...] + _dot(y, w_ref[...]) - t_ref[...]
        e2 = err * err
        part = jnp.sum(e2.reshape(TM // 8, 8, D_MODEL), axis=0)
        loss_ref[...] += functools.reduce(lambda a, b: a + b, [part[:, k * 128:(k + 1) * 128] for k in range(8)])
        dout = err * (1.0 / D_MODEL)
        dout_ref[...] = dout
        dob16 = dout.astype(BF16)
        for r0 in range(0, D_MODEL, ACC_ROWS):
            acc_ref[r0:r0 + ACC_ROWS, :] += _dot(y[:, r0:r0 + ACC_ROWS], dob16, TN)
        dy = _dot(dob16, w_ref[...], NT)
        dya, dyb = dy[:, :512], dy[:, 512:]
        doa_ref[...] = (dya * silu_a).astype(BF16)
        dob_ref[...] = (dyb * silu_b).astype(BF16)
        dga_ref[...] = (dya * oa * (sa * (1.0 + gav * (1.0 - sa)))).astype(BF16)
        dgb_ref[...] = (dyb * ob * (sb * (1.0 + gbv * (1.0 - sb)))).astype(BF16)

        @pl.when(step == n_tiles - 1)
        def _():
            dw_ref[...] = acc_ref[...].astype(BF16)

    def tile(w):
        return pl.BlockSpec((TM, w), lambda i: (i, 0))

    const = lambda i: (0, 0)
    return pl.pallas_call(
        body, name="out_loss", grid=(n_tiles,),
        in_specs=[tile(512)] * 4 + [tile(D_MODEL)] * 2 + [pl.BlockSpec((D_MODEL, D_MODEL), const)],
        out_specs=[tile(D_MODEL), tile(512), tile(512), tile(512), tile(512),
                   pl.BlockSpec((D_MODEL, D_MODEL), const), pl.BlockSpec((8, 128), const)],
        out_shape=[jax.ShapeDtypeStruct((T_LOC, D_MODEL), F32)] + [jax.ShapeDtypeStruct((T_LOC, 512), BF16)] * 4
        + [jax.ShapeDtypeStruct((D_MODEL, D_MODEL), BF16), jax.ShapeDtypeStruct((8, 128), F32)],
        scratch_shapes=[pltpu.VMEM((D_MODEL, D_MODEL), F32)],
        compiler_params=_params(("arbitrary",)),
    )(o_a, o_b, ga, gb, x, target, wout)


def swa_bwd(q_rot, k_dup, v_dup, o_a, d_oa, sinks):
    def body(q_ref, kp_ref, kc_ref, vp_ref, vc_ref, o_ref, do_ref, sinks_ref, dq_ref, dk_ref, dv_ref, dsink_ref):
        b, i = pl.program_id(0), pl.program_id(1)

        @pl.when(i == 0)
        def _():
            dk_ref[...] = jnp.zeros_like(dk_ref)
            dv_ref[...] = jnp.zeros_like(dv_ref)

        @pl.when((i == 0) & (b == 0))
        def _():
            dsink_ref[...] = jnp.zeros_like(dsink_ref)

        lane = _lane((BLK, SLAB))
        rows_prev, rows_cur = _blk(jnp.maximum(i - 1, 0)), _blk(i)
        pairs = range(4)
        q2 = [_stack_heads(q_ref[:, _slab(p)], lane) for p in pairs]
        do2 = [_stack_heads(do_ref[:, _slab(p)], lane) for p in pairs]
        keys = [_swa_window(kp_ref, kc_ref, p) for p in pairs]
        s = [_dot(q2[p], keys[p], NT) for p in pairs]
        dp = [_dot(do2[p], _swa_window(vp_ref, vc_ref, p), NT) for p in pairs]
        ds, pn16, cols = [], [], []
        for p in pairs:
            pn, p_sink = _swa_probs(s[p], sinks_ref, p, i)
            o = o_ref[:, _slab(p)]
            delta = jnp.sum(do2[p].astype(F32) * jnp.concatenate([o, o], axis=0), axis=-1, keepdims=True)
            ds.append((pn * (dp[p] - delta)).astype(BF16))
            pn16.append(pn.astype(BF16))
            cols.append(-p_sink * delta)
        for p in pairs:
            dq_ref[:, _slab(p)] = _unstack_heads(_dot(ds[p], keys[p]), lane) * Q_SCALE
        dk2 = [_dot(ds[p], q2[p], TN) for p in pairs]
        dv2 = [_dot(pn16[p], do2[p], TN) for p in pairs]
        for p in pairs:
            gsl = _slab(p // 2)
            dk_ref[rows_prev, gsl] += dk2[p][:BLK]
            dk_ref[rows_cur, gsl] += dk2[p][BLK:]
            dv_ref[rows_prev, gsl] += dv2[p][:BLK]
            dv_ref[rows_cur, gsl] += dv2[p][BLK:]
            for e in range(2):
                dsink_ref[2 * p + e:2 * p + e + 1, :] += jnp.sum(cols[p][e * BLK:(e + 1) * BLK], axis=0, keepdims=True)

    cur, prev = _swa_specs()
    per_seq = pl.BlockSpec((SEQ, 256), lambda b, i: (b, 0))
    return pl.pallas_call(
        body, name="swa_bwd", grid=(B_LOC, N_BLK),
        in_specs=[cur(512), prev(256), cur(256), prev(256), cur(256), cur(512), cur(512),
                  pl.BlockSpec(memory_space=pltpu.SMEM)],
        out_specs=[cur(512), per_seq, per_seq, pl.BlockSpec((8, 128), lambda b, i: (0, 0))],
        out_shape=[jax.ShapeDtypeStruct((T_LOC, 512), F32), jax.ShapeDtypeStruct((T_LOC, 256), F32),
                   jax.ShapeDtypeStruct((T_LOC, 256), F32), jax.ShapeDtypeStruct((8, 128), F32)],
        compiler_params=_params(("arbitrary", "arbitrary")),
    )(q_rot, k_dup, k_dup, v_dup, v_dup, o_a, d_oa, sinks)


def sb_bwd(qb, kb, vb, d_ob, carries):
    def body(q_ref, k_ref, v_ref, do_ref, c_ref, dq_ref, dk_ref, dv_ref, kt_ref, dqt_ref):
        for c, (b, pp) in enumerate(SB_CHAINS):
            for j in range(N_BLK):
                kt_ref[c, j] = k_ref[b * SEQ + j * BLK:b * SEQ + (j + 1) * BLK, _slab(pp)].T
        dk_ref[...] = jnp.zeros_like(dk_ref)
        dv_ref[...] = jnp.zeros_like(dv_ref)
        dqt_ref[...] = jnp.zeros_like(dqt_ref)
        lane = _lane((BLK, SLAB))
        tri_after, tri_before = _tri(True), _tri(False)
        valid = _causal_t()
        jrow = _row((N_BLK, 2 * BLK))
        chains = range(len(SB_CHAINS))

        def q_block(i, _):
            q2 = [_stack_heads(q_ref[_sb_rows(b, i), _slab(pp)], lane) for b, pp in SB_CHAINS]
            do2 = [_stack_heads(do_ref[_sb_rows(b, i), _slab(pp)], lane) for b, pp in SB_CHAINS]

            def key_block(j, carry_sp, before_u, mask):
                at = [(_sb_rows(b, j), _slab(pp)) for b, pp in SB_CHAINS]
                zt = [_dot(k_ref[at[c]], q2[c], NT) for c in chains]
                dw = [_dot(v_ref[at[c]], do2[c], NT) for c in chains]
                sp = [_sb_softplus(zt[c], mask) for c in chains]
                later = [_key_sums(tri_after, sp[c])[0] for c in chains]
                w = [_sb_weights(zt[c], sp[c], later[c] + carry_sp[c], mask) for c in chains]
                u = [dw[c] * w[c] for c in chains]
                for c in chains:
                    dv_ref[at[c]] += _dot(w[c].astype(BF16), do2[c])
                sums = [_key_sums(tri_before, u[c]) for c in chains]
                dz16 = []
                for c in chains:
                    sig = jnp.exp(zt[c] - sp[c])
                    dz = u[c] * (1.0 - sig) - (before_u[c] + sums[c][0]) * sig
                    if mask is not None:
                        dz = jnp.where(mask, dz, 0.0)
                    dz16.append(dz.astype(BF16))
                for c in chains:
                    dk_ref[at[c]] += _dot(dz16[c], q2[c])
                    dqt_ref[c] += _dot(kt_ref[c, j], dz16[c])
                return tuple(before_u[c] + sums[c][1] for c in chains)

            def earlier(j, before_u):
                carry_sp = [jnp.sum(jnp.where(jrow == j, c_ref[c * N_BLK + i], 0.0), axis=0, keepdims=True)
                            for c in chains]
                return key_block(j, carry_sp, before_u, None)

            zero = tuple(jnp.zeros((1, 2 * BLK), F32) for _ in chains)
            before_u = lax.fori_loop(0, i, earlier, zero)
            key_block(i, zero, before_u, valid)
            for c, (b, pp) in enumerate(SB_CHAINS):
                dq_ref[_sb_rows(b, i), _slab(pp)] = (_unstack_heads(dqt_ref[c].T, lane) * Q_SCALE).astype(BF16)
                dqt_ref[c] = jnp.zeros((SLAB, 2 * BLK), F32)
            return 0

        lax.fori_loop(0, N_BLK, q_block, 0)

    n_ch = len(SB_CHAINS)
    vmem = pl.BlockSpec(memory_space=pltpu.VMEM)
    return pl.pallas_call(
        body, name="sb_bwd",
        in_specs=[vmem] * 5, out_specs=[vmem] * 3,
        out_shape=[jax.ShapeDtypeStruct((T_LOC, 512), BF16)] + [jax.ShapeDtypeStruct((T_LOC, 512), F32)] * 2,
        scratch_shapes=[pltpu.VMEM((n_ch, N_BLK, SLAB, BLK), BF16), pltpu.VMEM((n_ch, SLAB, 2 * BLK), F32)],
        compiler_params=pltpu.CompilerParams(vmem_limit_bytes=VMEM_LIMIT),
    )(qb, kb, vb, d_ob, carries)


def bwd_in(x, dout, norm_gain, win_t, dq_rot, dk_dup, dv_dup, qa_raw, ka_raw, cos, sin_s, q_gain2, k_gain2,
           dga, dgb, dqb, dkb, dvb):
    n_tiles = T_LOC // TM

    def body(x_ref, dout_ref, ng_ref, w_hbm, dq_ref, dk_ref, dv_ref, qa_ref, ka_ref, cos_ref, sin_ref, qg_ref, kg_ref,
             dga_ref, dgb_ref, dqb_ref, dkb_ref, dvb_ref,
             gx_ref, dw_hbm, dng_ref, dqg_ref, dkg_ref, w_ref, acc_ref, stage_ref, w_sem):
        step = pl.program_id(0)

        @pl.when(step == 0)
        def _():
            cp = pltpu.make_async_copy(w_hbm, w_ref, w_sem)
            cp.start()
            acc_ref[...] = jnp.zeros_like(acc_ref)
            dng_ref[...] = jnp.zeros_like(dng_ref)
            dqg_ref[...] = jnp.zeros_like(dqg_ref)
            dkg_ref[...] = jnp.zeros_like(dkg_ref)
            cp.wait()

        lane = _lane((TM, SLAB))
        bd = _head_blockdiag()
        cos, sin_s = cos_ref[...], sin_ref[...]

        def norm_rope_bwd(d_rot, raw, gain2):
            dy = d_rot * cos + _swap_half(d_rot * sin_s, lane)
            r = lax.rsqrt(_head_sum(raw * raw, bd) * (1.0 / HEAD_DIM) + EPS)
            xhat = raw * r
            dgain = jnp.sum(dy * xhat, axis=0, keepdims=True)
            dxh = dy * gain2
            mean = _head_sum(dxh * xhat, bd) * (1.0 / HEAD_DIM)
            return r * (dxh - xhat * mean), dgain

        def fold_dup(d_dup):
            a, b2 = d_dup[:, :SLAB], d_dup[:, SLAB:]
            return jnp.where(lane < HEAD_DIM, a + pltpu.roll(a, HEAD_DIM, 1), b2 + pltpu.roll(b2, HEAD_DIM, 1))

        pieces = []
        dqg = jnp.zeros((1, SLAB), F32)
        for p in range(4):
            sl = slice(p * SLAB, (p + 1) * SLAB)
            d_raw, dg = norm_rope_bwd(dq_ref[:, sl], qa_ref[:, sl], qg_ref[...])
            pieces.append(d_raw.astype(BF16))
            dqg = dqg + dg
        d_raw, dkg = norm_rope_bwd(fold_dup(dk_ref[...]), ka_ref[...], kg_ref[...])
        pieces.append(d_raw.astype(BF16))
        pieces.append(fold_dup(dv_ref[...]).astype(BF16))
        pieces += [dga_ref[...], dqb_ref[...], dkb_ref[...].astype(BF16), dvb_ref[...].astype(BF16),
                   dgb_ref[...]]
        dproj = jnp.concatenate(pieces, axis=1)
        dqg_ref[0:1, :] += dqg + pltpu.roll(dqg, HEAD_DIM, 1)
        dkg_ref[0:1, :] += dkg + pltpu.roll(dkg, HEAD_DIM, 1)

        xv = x_ref[...]
        rstd = lax.rsqrt(jnp.mean(xv * xv, axis=-1, keepdims=True) + EPS)
        xhat = xv * rstd
        gain = ng_ref[...]
        h = (xhat * gain).astype(BF16)
        for r0 in range(0, IN_WIDTH, ACC_ROWS):
            acc_ref[r0:r0 + ACC_ROWS, :] += _dot(dproj[:, r0:r0 + ACC_ROWS], h, TN)
        dh = _dot(dproj, w_ref[...])
        dng_ref[0:1, :] += jnp.sum(dh * xhat, axis=0, keepdims=True)
        dxh = dh * gain
        gx_ref[...] = dout_ref[...] + rstd * (dxh - xhat * jnp.mean(dxh * xhat, axis=-1, keepdims=True))

        @pl.when(step == n_tiles - 1)
        def _():
            for r0 in range(0, IN_WIDTH, ACC_ROWS):
                stage_ref[...] = acc_ref[r0:r0 + ACC_ROWS, :].astype(BF16)
                pltpu.sync_copy(stage_ref, dw_hbm.at[r0:r0 + ACC_ROWS, :])

    def tile(w):
        return pl.BlockSpec((TM, w), lambda i: (i, 0))

    def whole(a):
        return pl.BlockSpec(a.shape, lambda i: (0, 0))

    const = lambda i: (0, 0)
    return pl.pallas_call(
        body, name="bwd_in", grid=(n_tiles,),
        in_specs=[tile(D_MODEL), tile(D_MODEL), whole(norm_gain), pl.BlockSpec(memory_space=pl.ANY),
                  tile(512), tile(256), tile(256), tile(512), tile(128), tile(128), tile(128),
                  whole(q_gain2), whole(k_gain2), tile(512), tile(512), tile(512), tile(512), tile(512)],
        out_specs=[tile(D_MODEL), pl.BlockSpec(memory_space=pl.ANY), pl.BlockSpec((8, D_MODEL), const),
                   pl.BlockSpec((8, SLAB), const), pl.BlockSpec((8, SLAB), const)],
        out_shape=[jax.ShapeDtypeStruct((T_LOC, D_MODEL), F32), jax.ShapeDtypeStruct((IN_WIDTH, D_MODEL), BF16),
                   jax.ShapeDtypeStruct((8, D_MODEL), F32), jax.ShapeDtypeStruct((8, SLAB), F32),
                   jax.ShapeDtypeStruct((8, SLAB), F32)],
        scratch_shapes=[pltpu.VMEM((IN_WIDTH, D_MODEL), BF16), pltpu.VMEM((IN_WIDTH, D_MODEL), F32),
                        pltpu.VMEM((ACC_ROWS, D_MODEL), BF16), pltpu.SemaphoreType.DMA],
        compiler_params=_params(("arbitrary",)),
    )(x, dout, norm_gain, win_t, dq_rot, dk_dup, dv_dup, qa_raw, ka_raw, cos, sin_s, q_gain2, k_gain2,
      dga, dgb, dqb, dkb, dvb)


def _adamw(w, g, m, v):
    m = ADAM_B1 * m + (1.0 - ADAM_B1) * g
    v = ADAM_B2 * v + (1.0 - ADAM_B2) * (g * g)
    m_hat = m / (1.0 - ADAM_B1 ** ADAM_STEP)
    v_hat = v / (1.0 - ADAM_B2 ** ADAM_STEP)
    delta = -ADAM_LR * (m_hat / (jnp.sqrt(v_hat) + ADAM_EPS) + ADAM_WD * w)
    return delta, m, v


def _sum_slots(r_ref):
    g = r_ref[0].astype(F32)
    for s in range(1, N_DEV):
        g = g + r_ref[s].astype(F32)
    return g


def adamw_rows(name, recv, w, m, v, rows_per_step):
    n_rows, n_cols = w.shape

    def body(r_ref, w_ref, m_ref, v_ref, g_ref, d_ref, nm_ref, nv_ref):
        g = _sum_slots(r_ref)
        g_ref[...] = g
        d_ref[...], nm_ref[...], nv_ref[...] = _adamw(w_ref[...], g, m_ref[...], v_ref[...])

    spec = pl.BlockSpec((rows_per_step, n_cols), lambda i: (i, 0))
    return pl.pallas_call(
        body, name=name, grid=(n_rows // rows_per_step,),
        in_specs=[pl.BlockSpec((N_DEV, rows_per_step, n_cols), lambda i: (0, i, 0)), spec, spec, spec],
        out_specs=[spec] * 4,
        out_shape=[jax.ShapeDtypeStruct((n_rows, n_cols), F32)] * 4,
        compiler_params=_params(("arbitrary",)),
    )(recv, w, m, v)


def adamw_small(recv, w_pack, m_pack, v_pack):
    def body(r_ref, w_ref, m_ref, v_ref, g_ref, d_ref, nm_ref, nv_ref, loss_ref):
        s = _sum_slots(r_ref)
        eye = (_row((8, SLAB)) == _lane((8, SLAB))).astype(F32)
        sinks = jnp.sum(s[:, 1280:1408] * eye, axis=0, keepdims=True)
        pad = jnp.zeros((1, D_MODEL - SLAB), F32)
        g = jnp.concatenate([
            s[0:1, :D_MODEL],
            jnp.concatenate([s[0:1, 1024:1152], pad], axis=1),
            jnp.concatenate([s[0:1, 1152:1280], pad], axis=1),
            jnp.concatenate([sinks, pad], axis=1),
            jnp.zeros((4, D_MODEL), F32)], axis=0)
        g_ref[...] = g
        d_ref[...], nm_ref[...], nv_ref[...] = _adamw(w_ref[...], g, m_ref[...], v_ref[...])
        loss = jnp.sum(jnp.sum(s[:, 1408:1536], axis=1, keepdims=True), axis=0, keepdims=True) * (0.5 / D_MODEL)
        loss_ref[...] = jnp.broadcast_to(loss, (8, SLAB))

    return pl.pallas_call(
        body, name="adamw_small",
        out_shape=[jax.ShapeDtypeStruct((8, D_MODEL), F32)] * 4 + [jax.ShapeDtypeStruct((8, SLAB), F32)],
        compiler_params=pltpu.CompilerParams(vmem_limit_bytes=VMEM_LIMIT),
    )(recv, w_pack, m_pack, v_pack)


def _pack_small(ng, qg, kg, sk):
    def row(a):
        return jnp.pad(a, ((0, 0), (0, D_MODEL - a.shape[1])))
    return jnp.concatenate([row(ng), row(qg), row(kg), row(sk), jnp.zeros((4, D_MODEL), F32)], axis=0)


def kernel(x, positions, norm_gain, w_in, q_norm_gain, k_norm_gain, sinks, w_out, loss_target, m_norm_gain, m_w_in, m_q_norm_gain, m_k_norm_gain, m_sinks, m_w_out, v_norm_gain, v_w_in, v_q_norm_gain, v_k_norm_gain, v_sinks, v_w_out):
    x2 = x.reshape(T_LOC, D_MODEL)
    tgt2 = loss_target.reshape(T_LOC, D_MODEL)
    pos2 = positions.reshape(T_LOC, 1)
    half = HEAD_DIM // 2
    inv_freq = ROPE_THETA ** (-jnp.arange(half, dtype=F32) * 2.0 / HEAD_DIM)
    inv_freq = jnp.tile(inv_freq, SLAB // half).reshape(1, SLAB)
    sin_sign = jnp.tile(jnp.concatenate([-jnp.ones((half,), F32), jnp.ones((half,), F32)]), 2).reshape(1, SLAB)
    q_gain2 = jnp.tile(q_norm_gain, (1, 2))
    k_gain2 = jnp.tile(k_norm_gain, (1, 2))

    win_t, wout = gather_weights(w_in[0].T.astype(BF16), w_out[0].astype(BF16))

    (qa_raw, ka_raw, q_rot, k_dup, v_dup, ga, qb, kb, vb, gb, cos, sin_s) = fwd_proj(
        x2, pos2, norm_gain, win_t, inv_freq, sin_sign, q_gain2, k_gain2)
    o_a = swa_fwd(q_rot, k_dup, v_dup, sinks)
    o_b, carries = sb_fwd(qb, kb, vb)
    dout, d_oa, d_ob, dga, dgb, dwout, loss_part = out_loss(o_a, o_b, ga, gb, x2, tgt2, wout)
    dq_rot, dk_dup, dv_dup, dsink = swa_bwd(q_rot, k_dup, v_dup, o_a, d_oa, sinks)
    dqb, dkb, dvb = sb_bwd(qb, kb, vb, d_ob, carries)
    grad_x, dwin_t, dng, dqg, dkg = bwd_in(
        x2, dout, norm_gain, win_t, dq_rot, dk_dup, dv_dup, qa_raw, ka_raw, cos, sin_s, q_gain2, k_gain2,
        dga, dgb, dqb, dkb, dvb)

    small = jnp.concatenate([dng, dqg, dkg, dsink, loss_part], axis=1)
    r_win, r_out, r_small = exchange_grads(dwin_t, dwout, small)

    g_win_t, d_win_t, nm_win_t, nv_win_t = adamw_rows(
        "adamw_w_in", r_win, w_in[0].T, m_w_in[0].T, v_w_in[0].T, IN_SHARD // 2)
    g_wout, d_wout, nm_wout, nv_wout = adamw_rows("adamw_w_out", r_out, w_out[0], m_w_out[0], v_w_out[0], OUT_SHARD)
    g_s, d_s, nm_s, nv_s, loss = adamw_small(
        r_small, _pack_small(norm_gain, q_norm_gain, k_norm_gain, sinks),
        _pack_small(m_norm_gain, m_q_norm_gain, m_k_norm_gain, m_sinks),
        _pack_small(v_norm_gain, v_q_norm_gain, v_k_norm_gain, v_sinks))

    def unpack(p, big_in_t, big_out):
        return (p[0:1, :], big_in_t.T[None], p[1:2, :HEAD_DIM], p[2:3, :HEAD_DIM], p[3:4, :8], big_out[None])

    return (loss[0, 0], grad_x.reshape(B_LOC, SEQ, D_MODEL),
            *unpack(g_s, g_win_t, g_wout), *unpack(d_s, d_win_t, d_wout),
            *unpack(nm_s, nm_win_t, nm_wout), *unpack(nv_s, nv_win_t, nv_wout))
```

```python
import functools

import jax
import jax.numpy as jnp
from jax import lax
from jax.experimental import pallas as pl
from jax.experimental.pallas import tpu as pltpu

F32 = jnp.float32
BF16 = jnp.bfloat16

N_DEV = 8
D_MODEL = 1024
SEQ = 2048
B_LOC = 2
T_LOC = B_LOC * SEQ
HEAD_DIM = 64
HEAD_SHIFT = 6
BLK = 128
N_BLK = SEQ // BLK
SLAB = 128
IN_WIDTH = 3328
IN_SHARD = IN_WIDTH // N_DEV
OUT_SHARD = D_MODEL // N_DEV
EPS = 1e-6
ROPE_THETA = 10000.0
Q_SCALE = 0.125
R_QA, R_KA, R_VA, R_GA, R_QB, R_KB, R_VB, R_GB, R_END = 0, 512, 640, 768, 1280, 1792, 2304, 2816, 3328
SMALL_W = 1536
ADAM_LR, ADAM_B1, ADAM_B2, ADAM_EPS, ADAM_WD, ADAM_STEP = 0.001, 0.9, 0.999, 1e-08, 0.01, 10
TM = 256
ACC_ROWS = 256
VMEM_LIMIT = 56 * 1024 * 1024

MESH = pl.DeviceIdType.MESH
NT = (((1,), (1,)), ((), ()))
TN = (((0,), (0,)), ((), ()))


def _params(sem, limit=VMEM_LIMIT):
    return pltpu.CompilerParams(dimension_semantics=sem, vmem_limit_bytes=limit)


def _dot(a, b, dims=None):
    if dims is None:
        return jnp.dot(a, b, preferred_element_type=F32)
    return lax.dot_general(a, b, dims, preferred_element_type=F32)


def _split(x):
    hi = x.astype(BF16)
    return hi, (x - hi.astype(F32)).astype(BF16)


def _lane(shape):
    return lax.broadcasted_iota(jnp.int32, shape, len(shape) - 1)


def _row(shape):
    return lax.broadcasted_iota(jnp.int32, shape, 0)


def _head_blockdiag():
    return ((_row((SLAB, SLAB)) >> HEAD_SHIFT) == (_lane((SLAB, SLAB)) >> HEAD_SHIFT)).astype(BF16)


def _head_sum(x, bd):
    hi, lo = _split(x)
    return _dot(hi, bd) + _dot(lo, bd)


def _swap_half(y, lane):
    return jnp.where((lane & 32) != 0, pltpu.roll(y, 32, 1), pltpu.roll(y, 96, 1))


def _stack_heads(q, lane):
    zero = jnp.zeros_like(q)
    return jnp.concatenate([jnp.where(lane < HEAD_DIM, q, zero), jnp.where(lane >= HEAD_DIM, q, zero)], axis=0)


def _unstack_heads(x2, lane):
    return jnp.where(lane < HEAD_DIM, x2[:BLK], x2[BLK:])


def _sigmoid(x):
    return 1.0 / (1.0 + jnp.exp(-x))


def _mesh_pos():
    return lax.axis_index("x"), lax.axis_index("y"), lax.axis_index("c")


def _flip(pos, mask):
    return tuple(1 - p if m else p for p, m in zip(pos, mask))


def _lin(pos):
    return 4 * pos[0] + 2 * pos[1] + pos[2]


CHIP_FLIPS = [(0, 0), (1, 0), (0, 1), (1, 1)]


def gather_weights(win_t_shard, wout_shard):
    shards = (win_t_shard, wout_shard)
    n_arr = len(shards)

    def body(a_ref, b_ref, oa_ref, ob_ref, send_sems, recv_sems, local_sems):
        x, y, c = _mesh_pos()
        me, sibling = (x, y, c), (x, y, 1 - c)
        chips = [(1 - x, y), (x, 1 - y), (1 - x, 1 - y)]
        ins, outs = (a_ref, b_ref), (oa_ref, ob_ref)

        def rows(a, pos):
            m = ins[a].shape[0]
            return outs[a].at[pl.ds(_lin(pos) * m, m), :]

        def copy(a, k, block, to, src=None):
            return pltpu.make_async_remote_copy(
                src_ref=rows(a, block) if src is None else src, dst_ref=rows(a, block),
                send_sem=send_sems.at[a, k], recv_sem=recv_sems.at[a, k], device_id=to, device_id_type=MESH)

        mine = [pltpu.make_async_copy(ins[a], rows(a, me), local_sems.at[a]) for a in range(n_arr)]
        for cp in mine:
            cp.start()
        first = []
        for a in range(n_arr):
            first.append(copy(a, 0, me, sibling, src=ins[a]))
            first += [copy(a, 1 + j, me, (*chip, c), src=ins[a]) for j, chip in enumerate(chips)]
        for cp in first:
            cp.start()
        passed = [[copy(a, 4 + j, (*chip, c), sibling) for j, chip in enumerate(chips)] for a in range(n_arr)]
        for j, chip in enumerate(chips):
            for a in range(n_arr):
                copy(a, 1 + j, (*chip, c), me).wait_recv()
                passed[a][j].start()
        for a in range(n_arr):
            copy(a, 0, sibling, me).wait_recv()
            for j, chip in enumerate(chips):
                copy(a, 4 + j, (*chip, 1 - c), me).wait_recv()
        for cp in first + [p for ps in passed for p in ps]:
            cp.wait_send()
        for cp in mine:
            cp.wait()

    vmem = pl.BlockSpec(memory_space=pltpu.VMEM)
    return pl.pallas_call(
        body, name="gather_weights",
        out_shape=[jax.ShapeDtypeStruct((N_DEV * s.shape[0], s.shape[1]), s.dtype) for s in shards],
        in_specs=[vmem] * n_arr, out_specs=[vmem] * n_arr,
        scratch_shapes=[pltpu.SemaphoreType.DMA((n_arr, 7)), pltpu.SemaphoreType.DMA((n_arr, 7)),
                        pltpu.SemaphoreType.DMA((n_arr,))],
        compiler_params=pltpu.CompilerParams(vmem_limit_bytes=VMEM_LIMIT),
    )(*shards)


def exchange_grads(dwin_t, dwout, small):
    srcs = (dwin_t, dwout)
    blocks = (IN_SHARD, OUT_SHARD)
    n_arr = len(srcs)
    dev_masks = [(mx, my, mc) for mx in (0, 1) for my in (0, 1) for mc in (0, 1)][1:]

    def body(a_hbm, b_hbm, s_ref, ra_ref, rb_ref, rs_ref, own_a, own_b, sib_a, sib_b, snd_a, snd_b,
             d2d_send, d2d_recv, ici_send, ici_recv, own_sems, s_send, s_recv):
        x, y, c = _mesh_pos()
        me, sibling = (x, y, c), (x, y, 1 - c)
        ins, outs = (a_hbm, b_hbm), (ra_ref, rb_ref)
        own, sib, snd = (own_a, own_b), (sib_a, sib_b), (snd_a, snd_b)
        chips = [_flip((x, y), f) for f in CHIP_FLIPS]

        def rows(a, pos):
            return ins[a].at[pl.ds(_lin(pos) * blocks[a], blocks[a]), :]

        def to_sibling(a, k):
            return pltpu.make_async_remote_copy(
                src_ref=rows(a, (*chips[k], 1 - c)), dst_ref=sib[a].at[k],
                send_sem=d2d_send.at[a, k], recv_sem=d2d_recv.at[a, k], device_id=sibling, device_id_type=MESH)

        def to_chip(a, k):
            return pltpu.make_async_remote_copy(
                src_ref=snd[a].at[k - 1], dst_ref=outs[a].at[k],
                send_sem=ici_send.at[a, k - 1], recv_sem=ici_recv.at[a, k - 1],
                device_id=(*chips[k], c), device_id_type=MESH)

        def small_to(k, to):
            return pltpu.make_async_remote_copy(
                src_ref=s_ref, dst_ref=rs_ref.at[_lin(me)], send_sem=s_send.at[k], recv_sem=s_recv.at[k],
                device_id=to, device_id_type=MESH)

        def small_from(k, frm):
            return pltpu.make_async_remote_copy(
                src_ref=s_ref, dst_ref=rs_ref.at[_lin(frm)], send_sem=s_send.at[k], recv_sem=s_recv.at[k],
                device_id=frm, device_id_type=MESH)

        order = (1, 2, 3, 0)
        swaps = [to_sibling(a, k) for k in order for a in range(n_arr)]
        for cp in swaps:
            cp.start()
        mine = {(a, k): pltpu.make_async_copy(rows(a, (*chips[k], c)), own[a].at[k], own_sems.at[a, k])
                for k in order for a in range(n_arr)}
        for cp in mine.values():
            cp.start()
        smalls = [small_to(k, _flip(me, mask)) for k, mask in enumerate(dev_masks)]
        for cp in smalls:
            cp.start()
        rs_ref[_lin(me)] = s_ref[...]

        sent = []
        for k in order:
            for a in range(n_arr):
                to_sibling(a, k).wait_recv()
                mine[(a, k)].wait()
                total = (own[a][k].astype(F32) + sib[a][k].astype(F32)).astype(BF16)
                if k == 0:
                    outs[a][0] = total
                else:
                    snd[a][k - 1] = total
                    sent.append(to_chip(a, k))
                    sent[-1].start()
        for k in (1, 2, 3):
            for a in range(n_arr):
                to_chip(a, k).wait_recv()
        for k, mask in enumerate(dev_masks):
            small_from(k, _flip(me, mask)).wait_recv()
        for cp in swaps + sent + smalls:
            cp.wait_send()

    def bufs(n):
        return [pltpu.VMEM((n, blocks[a], D_MODEL), BF16) for a in range(n_arr)]

    vmem, hbm = pl.BlockSpec(memory_space=pltpu.VMEM), pl.BlockSpec(memory_space=pl.ANY)
    return pl.pallas_call(
        body, name="exchange_grads",
        out_shape=[jax.ShapeDtypeStruct((4, blocks[a], D_MODEL), BF16) for a in range(n_arr)]
        + [jax.ShapeDtypeStruct((N_DEV,) + small.shape, small.dtype)],
        in_specs=[hbm, hbm, vmem], out_specs=[vmem] * 3,
        scratch_shapes=bufs(4) + bufs(4) + bufs(3) + [
            pltpu.SemaphoreType.DMA((n_arr, 4)), pltpu.SemaphoreType.DMA((n_arr, 4)),
            pltpu.SemaphoreType.DMA((n_arr, 3)), pltpu.SemaphoreType.DMA((n_arr, 3)),
            pltpu.SemaphoreType.DMA((n_arr, 4)), pltpu.SemaphoreType.DMA((7,)), pltpu.SemaphoreType.DMA((7,))],
        compiler_params=pltpu.CompilerParams(vmem_limit_bytes=VMEM_LIMIT),
    )(dwin_t, dwout, small)


def _norm_rope(xs, gain2, cos, sin_s, bd, lane):
    r = lax.rsqrt(_head_sum(xs * xs, bd) * (1.0 / HEAD_DIM) + EPS)
    y = xs * r * gain2
    return y * cos + _swap_half(y, lane) * sin_s


def _dup_heads(xs, lane):
    r = pltpu.roll(xs, HEAD_DIM, 1)
    lo = lane < HEAD_DIM
    return jnp.concatenate([jnp.where(lo, xs, r), jnp.where(lo, r, xs)], axis=1)


def fwd_proj(x, pos, norm_gain, win_t, inv_freq, sin_sign, q_gain2, k_gain2):
    def body(x_ref, pos_ref, ng_ref, w_ref, if_ref, sg_ref, qg_ref, kg_ref,
             qa_raw_ref, ka_raw_ref, q_rot_ref, k_dup_ref, v_dup_ref, ga_ref, qb_ref, kb_ref, vb_ref, gb_ref,
             cos_ref, sin_ref):
        xv = x_ref[...]
        rstd = lax.rsqrt(jnp.mean(xv * xv, axis=-1, keepdims=True) + EPS)
        h = (xv * rstd * ng_ref[...]).astype(BF16)

        def proj(r0, r1):
            return _dot(h, w_ref[r0:r1, :], NT)

        ang = pos_ref[...].astype(F32) * if_ref[...]
        cos = jnp.cos(ang)
        sin_s = jnp.sin(ang) * sg_ref[...]
        cos_ref[...] = cos
        sin_ref[...] = sin_s
        lane = _lane((TM, SLAB))
        bd = _head_blockdiag()

        qa = proj(R_QA, R_KA)
        qa_raw_ref[...] = qa
        for p in range(4):
            sl = slice(p * SLAB, (p + 1) * SLAB)
            q_rot_ref[:, sl] = (_norm_rope(qa[:, sl], qg_ref[...], cos, sin_s, bd, lane) * Q_SCALE).astype(BF16)
        ka = proj(R_KA, R_VA)
        ka_raw_ref[...] = ka
        k_dup_ref[...] = _dup_heads(_norm_rope(ka, kg_ref[...], cos, sin_s, bd, lane), lane).astype(BF16)
        v_dup_ref[...] = _dup_heads(proj(R_VA, R_GA), lane).astype(BF16)
        ga_ref[...] = proj(R_GA, R_QB)
        qb_ref[...] = (proj(R_QB, R_KB) * Q_SCALE).astype(BF16)
        kb_ref[...] = proj(R_KB, R_VB).astype(BF16)
        vb_ref[...] = proj(R_VB, R_GB).astype(BF16)
        gb_ref[...] = proj(R_GB, R_END)

    def tile(w):
        return pl.BlockSpec((TM, w), lambda i: (i, 0))

    def whole(a):
        return pl.BlockSpec(a.shape, lambda i: (0, 0))

    widths = [(512, F32), (128, F32), (512, BF16), (256, BF16), (256, BF16), (512, F32), (512, BF16), (512, BF16),
              (512, BF16), (512, F32), (128, F32), (128, F32)]
    return pl.pallas_call(
        body, name="fwd_proj", grid=(T_LOC // TM,),
        in_specs=[tile(D_MODEL), tile(1), whole(norm_gain), whole(win_t), whole(inv_freq), whole(sin_sign),
                  whole(q_gain2), whole(k_gain2)],
        out_specs=[tile(w) for w, _ in widths],
        out_shape=[jax.ShapeDtypeStruct((T_LOC, w), dt) for w, dt in widths],
        compiler_params=_params(("arbitrary",)),
    )(x, pos, norm_gain, win_t, inv_freq, sin_sign, q_gain2, k_gain2)


def _swa_window(prev_ref, cur_ref, p):
    gsl = _slab(p // 2)
    return jnp.concatenate([prev_ref[:, gsl], cur_ref[:, gsl]], axis=0)


def _swa_probs(s, sinks_ref, p, i):
    shape = (2 * BLK, 2 * BLK)
    r = _row(shape) & (BLK - 1)
    cidx = _lane(shape)
    valid = (cidx > r) & (cidx <= r + BLK) & ((cidx >= BLK) | (i > 0))
    s = jnp.where(valid, s, -jnp.inf)
    sink = jnp.where(_row((2 * BLK, 1)) < BLK, sinks_ref[0, 2 * p], sinks_ref[0, 2 * p + 1])
    m = jnp.maximum(jnp.max(s, axis=-1, keepdims=True), sink)
    e = jnp.exp(s - m)
    e_sink = jnp.exp(sink - m)
    den = jnp.sum(e, axis=-1, keepdims=True) + e_sink
    return e / den, e_sink / den


def _swa_specs():
    def cur(w):
        return pl.BlockSpec((BLK, w), lambda b, i: (b * N_BLK + i, 0))

    def prev(w):
        return pl.BlockSpec((BLK, w), lambda b, i: (b * N_BLK + jnp.maximum(i - 1, 0), 0))

    return cur, prev


def swa_fwd(q_rot, k_dup, v_dup, sinks):
    def body(q_ref, kp_ref, kc_ref, vp_ref, vc_ref, sinks_ref, o_ref):
        i = pl.program_id(1)
        lane = _lane((BLK, SLAB))
        pairs = range(4)
        s = [_dot(_stack_heads(q_ref[:, _slab(p)], lane), _swa_window(kp_ref, kc_ref, p), NT) for p in pairs]
        pn = [_swa_probs(s[p], sinks_ref, p, i)[0].astype(BF16) for p in pairs]
        for p in pairs:
            o_ref[:, _slab(p)] = _unstack_heads(_dot(pn[p], _swa_window(vp_ref, vc_ref, p)), lane)

    cur, prev = _swa_specs()
    return pl.pallas_call(
        body, name="swa_fwd", grid=(B_LOC, N_BLK),
        in_specs=[cur(512), prev(256), cur(256), prev(256), cur(256), pl.BlockSpec(memory_space=pltpu.SMEM)],
        out_specs=cur(512),
        out_shape=jax.ShapeDtypeStruct((T_LOC, 512), F32),
        compiler_params=_params(("arbitrary", "arbitrary")),
    )(q_rot, k_dup, k_dup, v_dup, v_dup, sinks)


def _tri(suffix):
    r, cidx = _row((BLK + 16, 2 * BLK)), _lane((BLK + 16, 2 * BLK)) & (BLK - 1)
    tri = (cidx > r) if suffix else (cidx < r)
    return (tri | (r >= BLK)).astype(BF16)


def _key_sums(tri, x):
    hi, lo = _split(x)
    res = _dot(tri, jnp.concatenate([hi, lo], axis=0))
    return res[:BLK], res[BLK:BLK + 1]


def _sb_softplus(zt, valid):
    neg_abs = lax.bitcast_convert_type(lax.bitcast_convert_type(zt, jnp.uint32) | jnp.uint32(0x80000000), F32)
    sp = jnp.maximum(zt, 0.0) + jnp.log(1.0 + jnp.exp(neg_abs))
    return sp if valid is None else jnp.where(valid, sp, 0.0)


def _sb_weights(zt, sp, later, valid):
    w = jnp.exp(zt - sp - later)
    return w if valid is None else jnp.where(valid, w, 0.0)


def _slab(pp):
    return slice(pp * SLAB, (pp + 1) * SLAB)


def _blk(j):
    return pl.ds(pl.multiple_of(j * BLK, BLK), BLK)


def _causal_t():
    return _row((BLK, 2 * BLK)) < (_lane((BLK, 2 * BLK)) & (BLK - 1))


def _sb_rows(b, j):
    return pl.ds(pl.multiple_of(b * SEQ + j * BLK, BLK), BLK)


SB_CHAINS = [(b, pp) for b in range(B_LOC) for pp in range(4)]


def sb_fwd(qb, kb, vb):
    def body(q_ref, k_ref, v_ref, o_ref, c_ref, vt_ref, ot_ref):
        for c, (b, pp) in enumerate(SB_CHAINS):
            for j in range(N_BLK):
                vt_ref[c, j] = v_ref[b * SEQ + j * BLK:b * SEQ + (j + 1) * BLK, _slab(pp)].T
        lane = _lane((BLK, SLAB))
        tri = _tri(True)
        valid = _causal_t()
        jrow = _row((N_BLK, 2 * BLK))
        chains = range(len(SB_CHAINS))

        def q_block(i, _):
            q2 = [_stack_heads(q_ref[_sb_rows(b, i), _slab(pp)], lane) for b, pp in SB_CHAINS]

            def key_block(j, carry, mask, first):
                zt = [_dot(k_ref[_sb_rows(b, j), _slab(pp)], q2[c], NT) for c, (b, pp) in enumerate(SB_CHAINS)]
                sp = [_sb_softplus(zt[c], mask) for c in chains]
                sums = [_key_sums(tri, sp[c]) for c in chains]
                w = [_sb_weights(zt[c], sp[c], sums[c][0] + carry[c], mask) for c in chains]
                for c in chains:
                    pv = _dot(vt_ref[c, j], w[c].astype(BF16))
                    if first:
                        ot_ref[c] = pv
                    else:
                        ot_ref[c] += pv
                return tuple(carry[c] + sums[c][1] for c in chains)

            def earlier(jj, state):
                carry, saved = state
                j = i - 1 - jj
                saved = tuple(jnp.where(jrow == j, carry[c], saved[c]) for c in chains)
                return key_block(j, carry, None, False), saved

            zero = tuple(jnp.zeros((1, 2 * BLK), F32) for _ in chains)
            carry = key_block(i, zero, valid, True)
            _, saved = lax.fori_loop(0, i, earlier, (carry, tuple(jnp.zeros((N_BLK, 2 * BLK), F32) for _ in chains)))
            for c, (b, pp) in enumerate(SB_CHAINS):
                o_ref[_sb_rows(b, i), _slab(pp)] = _unstack_heads(ot_ref[c].T, lane)
                c_ref[c * N_BLK + i] = saved[c]
            return 0

        lax.fori_loop(0, N_BLK, q_block, 0)

    n_ch = len(SB_CHAINS)
    vmem = pl.BlockSpec(memory_space=pltpu.VMEM)
    return pl.pallas_call(
        body, name="sb_fwd",
        in_specs=[vmem] * 3, out_specs=[vmem] * 2,
        out_shape=[jax.ShapeDtypeStruct((T_LOC, 512), F32), jax.ShapeDtypeStruct((n_ch * N_BLK, N_BLK, 2 * BLK), F32)],
        scratch_shapes=[pltpu.VMEM((n_ch, N_BLK, SLAB, BLK), BF16), pltpu.VMEM((n_ch, SLAB, 2 * BLK), F32)],
        compiler_params=pltpu.CompilerParams(vmem_limit_bytes=VMEM_LIMIT),
    )(qb, kb, vb)


def out_loss(o_a, o_b, ga, gb, x, target, wout):
    n_tiles = T_LOC // TM

    def body(oa_ref, ob_ref, ga_ref, gb_ref, x_ref, t_ref, w_ref,
             dout_ref, doa_ref, dob_ref, dga_ref, dgb_ref, dw_ref, loss_ref, acc_ref):
        step = pl.program_id(0)

        @pl.when(step == 0)
        def _():
            acc_ref[...] = jnp.zeros_like(acc_ref)
            loss_ref[...] = jnp.zeros_like(loss_ref)

        oa, ob, gav, gbv = oa_ref[...], ob_ref[...], ga_ref[...], gb_ref[...]
        sa, sb = _sigmoid(gav), _sigmoid(gbv)
        silu_a, silu_b = gav * sa, gbv * sb
        y = jnp.concatenate([oa * silu_a, ob * silu_b], axis=1).astype(BF16)
        err = x_ref[...] + _dot(y, w_ref[...]) - t_ref[...]
        e2 = err * err
        part = jnp.sum(e2.reshape(TM // 8, 8, D_MODEL), axis=0)
        loss_ref[...] += functools.reduce(lambda a, b: a + b, [part[:, k * 128:(k + 1) * 128] for k in range(8)])
        dout = err * (1.0 / D_MODEL)
        dout_ref[...] = dout
        dob16 = dout.astype(BF16)
        for r0 in range(0, D_MODEL, ACC_ROWS):
            acc_ref[r0:r0 + ACC_ROWS, :] += _dot(y[:, r0:r0 + ACC_ROWS], dob16, TN)
        dy = _dot(dob16, w_ref[...], NT)
        dya, dyb = dy[:, :512], dy[:, 512:]
        doa_ref[...] = (dya * silu_a).astype(BF16)
        dob_ref[...] = (dyb * silu_b).astype(BF16)
        dga_ref[...] = (dya * oa * (sa * (1.0 + gav * (1.0 - sa)))).astype(BF16)
        dgb_ref[...] = (dyb * ob * (sb * (1.0 + gbv * (1.0 - sb)))).astype(BF16)

        @pl.when(step == n_tiles - 1)
        def _():
            dw_ref[...] = acc_ref[...].astype(BF16)

    def tile(w):
        return pl.BlockSpec((TM, w), lambda i: (i, 0))

    const = lambda i: (0, 0)
    return pl.pallas_call(
        body, name="out_loss", grid=(n_tiles,),
        in_specs=[tile(512)] * 4 + [tile(D_MODEL)] * 2 + [pl.BlockSpec((D_MODEL, D_MODEL), const)],
        out_specs=[tile(D_MODEL), tile(512), tile(512), tile(512), tile(512),
                   pl.BlockSpec((D_MODEL, D_MODEL), const), pl.BlockSpec((8, 128), const)],
        out_shape=[jax.ShapeDtypeStruct((T_LOC, D_MODEL), F32)] + [jax.ShapeDtypeStruct((T_LOC, 512), BF16)] * 4
        + [jax.ShapeDtypeStruct((D_MODEL, D_MODEL), BF16), jax.ShapeDtypeStruct((8, 128), F32)],
        scratch_shapes=[pltpu.VMEM((D_MODEL, D_MODEL), F32)],
        compiler_params=_params(("arbitrary",)),
    )(o_a, o_b, ga, gb, x, target, wout)


def swa_bwd(q_rot, k_dup, v_dup, o_a, d_oa, sinks):
    def body(q_ref, kp_ref, kc_ref, vp_ref, vc_ref, o_ref, do_ref, sinks_ref, dq_ref, dk_ref, dv_ref, dsink_ref):
        b, i = pl.program_id(0), pl.program_id(1)

        @pl.when(i == 0)
        def _():
            dk_ref[...] = jnp.zeros_like(dk_ref)
            dv_ref[...] = jnp.zeros_like(dv_ref)

        @pl.when((i == 0) & (b == 0))
        def _():
            dsink_ref[...] = jnp.zeros_like(dsink_ref)

        lane = _lane((BLK, SLAB))
        rows_prev, rows_cur = _blk(jnp.maximum(i - 1, 0)), _blk(i)
        pairs = range(4)
        q2 = [_stack_heads(q_ref[:, _slab(p)], lane) for p in pairs]
        do2 = [_stack_heads(do_ref[:, _slab(p)], lane) for p in pairs]
        keys = [_swa_window(kp_ref, kc_ref, p) for p in pairs]
        s = [_dot(q2[p], keys[p], NT) for p in pairs]
        dp = [_dot(do2[p], _swa_window(vp_ref, vc_ref, p), NT) for p in pairs]
        ds, pn16, cols = [], [], []
        for p in pairs:
            pn, p_sink = _swa_probs(s[p], sinks_ref, p, i)
            o = o_ref[:, _slab(p)]
            delta = jnp.sum(do2[p].astype(F32) * jnp.concatenate([o, o], axis=0), axis=-1, keepdims=True)
            ds.append((pn * (dp[p] - delta)).astype(BF16))
            pn16.append(pn.astype(BF16))
            cols.append(-p_sink * delta)
        for p in pairs:
            dq_ref[:, _slab(p)] = _unstack_heads(_dot(ds[p], keys[p]), lane) * Q_SCALE
        dk2 = [_dot(ds[p], q2[p], TN) for p in pairs]
        dv2 = [_dot(pn16[p], do2[p], TN) for p in pairs]
        for p in pairs:
            gsl = _slab(p // 2)
            dk_ref[rows_prev, gsl] += dk2[p][:BLK]
            dk_ref[rows_cur, gsl] += dk2[p][BLK:]
            dv_ref[rows_prev, gsl] += dv2[p][:BLK]
            dv_ref[rows_cur, gsl] += dv2[p][BLK:]
            for e in range(2):
                dsink_ref[2 * p + e:2 * p + e + 1, :] += jnp.sum(cols[p][e * BLK:(e + 1) * BLK], axis=0, keepdims=True)

    cur, prev = _swa_specs()
    per_seq = pl.BlockSpec((SEQ, 256), lambda b, i: (b, 0))
    return pl.pallas_call(
        body, name="swa_bwd", grid=(B_LOC, N_BLK),
        in_specs=[cur(512), prev(256), cur(256), prev(256), cur(256), cur(512), cur(512),
                  pl.BlockSpec(memory_space=pltpu.SMEM)],
        out_specs=[cur(512), per_seq, per_seq, pl.BlockSpec((8, 128), lambda b, i: (0, 0))],
        out_shape=[jax.ShapeDtypeStruct((T_LOC, 512), F32), jax.ShapeDtypeStruct((T_LOC, 256), F32),
                   jax.ShapeDtypeStruct((T_LOC, 256), F32), jax.ShapeDtypeStruct((8, 128), F32)],
        compiler_params=_params(("arbitrary", "arbitrary")),
    )(q_rot, k_dup, k_dup, v_dup, v_dup, o_a, d_oa, sinks)


def sb_bwd(qb, kb, vb, d_ob, carries):
    def body(q_ref, k_ref, v_ref, do_ref, c_ref, dq_ref, dk_ref, dv_ref, kt_ref, dqt_ref):
        for c, (b, pp) in enumerate(SB_CHAINS):
            for j in range(N_BLK):
                kt_ref[c, j] = k_ref[b * SEQ + j * BLK:b * SEQ + (j + 1) * BLK, _slab(pp)].T
        dk_ref[...] = jnp.zeros_like(dk_ref)
        dv_ref[...] = jnp.zeros_like(dv_ref)
        dqt_ref[...] = jnp.zeros_like(dqt_ref)
        lane = _lane((BLK, SLAB))
        tri_after, tri_before = _tri(True), _tri(False)
        valid = _causal_t()
        jrow = _row((N_BLK, 2 * BLK))
        chains = range(len(SB_CHAINS))

        def q_block(i, _):
            q2 = [_stack_heads(q_ref[_sb_rows(b, i), _slab(pp)], lane) for b, pp in SB_CHAINS]
            do2 = [_stack_heads(do_ref[_sb_rows(b, i), _slab(pp)], lane) for b, pp in SB_CHAINS]

            def key_block(j, carry_sp, before_u, mask):
                at = [(_sb_rows(b, j), _slab(pp)) for b, pp in SB_CHAINS]
                zt = [_dot(k_ref[at[c]], q2[c], NT) for c in chains]
                dw = [_dot(v_ref[at[c]], do2[c], NT) for c in chains]
                sp = [_sb_softplus(zt[c], mask) for c in chains]
                later = [_key_sums(tri_after, sp[c])[0] for c in chains]
                w = [_sb_weights(zt[c], sp[c], later[c] + carry_sp[c], mask) for c in chains]
                u = [dw[c] * w[c] for c in chains]
                for c in chains:
                    dv_ref[at[c]] += _dot(w[c].astype(BF16), do2[c])
                sums = [_key_sums(tri_before, u[c]) for c in chains]
                dz16 = []
                for c in chains:
                    sig = jnp.exp(zt[c] - sp[c])
                    dz = u[c] * (1.0 - sig) - (before_u[c] + sums[c][0]) * sig
                    if mask is not None:
                        dz = jnp.where(mask, dz, 0.0)
                    dz16.append(dz.astype(BF16))
                for c in chains:
                    dk_ref[at[c]] += _dot(dz16[c], q2[c])
                    dqt_ref[c] += _dot(kt_ref[c, j], dz16[c])
                return tuple(before_u[c] + sums[c][1] for c in chains)

            def earlier(j, before_u):
                carry_sp = [jnp.sum(jnp.where(jrow == j, c_ref[c * N_BLK + i], 0.0), axis=0, keepdims=True)
                            for c in chains]
                return key_block(j, carry_sp, before_u, None)

            zero = tuple(jnp.zeros((1, 2 * BLK), F32) for _ in chains)
            before_u = lax.fori_loop(0, i, earlier, zero)
            key_block(i, zero, before_u, valid)
            for c, (b, pp) in enumerate(SB_CHAINS):
                dq_ref[_sb_rows(b, i), _slab(pp)] = (_unstack_heads(dqt_ref[c].T, lane) * Q_SCALE).astype(BF16)
                dqt_ref[c] = jnp.zeros((SLAB, 2 * BLK), F32)
            return 0

        lax.fori_loop(0, N_BLK, q_block, 0)

    n_ch = len(SB_CHAINS)
    vmem = pl.BlockSpec(memory_space=pltpu.VMEM)
    return pl.pallas_call(
        body, name="sb_bwd",
        in_specs=[vmem] * 5, out_specs=[vmem] * 3,
        out_shape=[jax.ShapeDtypeStruct((T_LOC, 512), BF16)] + [jax.ShapeDtypeStruct((T_LOC, 512), F32)] * 2,
        scratch_shapes=[pltpu.VMEM((n_ch, N_BLK, SLAB, BLK), BF16), pltpu.VMEM((n_ch, SLAB, 2 * BLK), F32)],
        compiler_params=pltpu.CompilerParams(vmem_limit_bytes=VMEM_LIMIT),
    )(qb, kb, vb, d_ob, carries)


def bwd_in(x, dout, norm_gain, win_t, dq_rot, dk_dup, dv_dup, qa_raw, ka_raw, cos, sin_s, q_gain2, k_gain2,
           dga, dgb, dqb, dkb, dvb):
    n_tiles = T_LOC // TM

    def body(x_ref, dout_ref, ng_ref, w_hbm, dq_ref, dk_ref, dv_ref, qa_ref, ka_ref, cos_ref, sin_ref, qg_ref, kg_ref,
             dga_ref, dgb_ref, dqb_ref, dkb_ref, dvb_ref,
             gx_ref, dw_hbm, dng_ref, dqg_ref, dkg_ref, w_ref, acc_ref, stage_ref, w_sem):
        step = pl.program_id(0)

        @pl.when(step == 0)
        def _():
            cp = pltpu.make_async_copy(w_hbm, w_ref, w_sem)
            cp.start()
            acc_ref[...] = jnp.zeros_like(acc_ref)
            dng_ref[...] = jnp.zeros_like(dng_ref)
            dqg_ref[...] = jnp.zeros_like(dqg_ref)
            dkg_ref[...] = jnp.zeros_like(dkg_ref)
            cp.wait()

        lane = _lane((TM, SLAB))
        bd = _head_blockdiag()
        cos, sin_s = cos_ref[...], sin_ref[...]

        def norm_rope_bwd(d_rot, raw, gain2):
            dy = d_rot * cos + _swap_half(d_rot * sin_s, lane)
            r = lax.rsqrt(_head_sum(raw * raw, bd) * (1.0 / HEAD_DIM) + EPS)
            xhat = raw * r
            dgain = jnp.sum(dy * xhat, axis=0, keepdims=True)
            dxh = dy * gain2
            mean = _head_sum(dxh * xhat, bd) * (1.0 / HEAD_DIM)
            return r * (dxh - xhat * mean), dgain

        def fold_dup(d_dup):
            a, b2 = d_dup[:, :SLAB], d_dup[:, SLAB:]
            return jnp.where(lane < HEAD_DIM, a + pltpu.roll(a, HEAD_DIM, 1), b2 + pltpu.roll(b2, HEAD_DIM, 1))

        pieces = []
        dqg = jnp.zeros((1, SLAB), F32)
        for p in range(4):
            sl = slice(p * SLAB, (p + 1) * SLAB)
            d_raw, dg = norm_rope_bwd(dq_ref[:, sl], qa_ref[:, sl], qg_ref[...])
            pieces.append(d_raw.astype(BF16))
            dqg = dqg + dg
        d_raw, dkg = norm_rope_bwd(fold_dup(dk_ref[...]), ka_ref[...], kg_ref[...])
        pieces.append(d_raw.astype(BF16))
        pieces.append(fold_dup(dv_ref[...]).astype(BF16))
        pieces += [dga_ref[...], dqb_ref[...], dkb_ref[...].astype(BF16), dvb_ref[...].astype(BF16),
                   dgb_ref[...]]
        dproj = jnp.concatenate(pieces, axis=1)
        dqg_ref[0:1, :] += dqg + pltpu.roll(dqg, HEAD_DIM, 1)
        dkg_ref[0:1, :] += dkg + pltpu.roll(dkg, HEAD_DIM, 1)

        xv = x_ref[...]
        rstd = lax.rsqrt(jnp.mean(xv * xv, axis=-1, keepdims=True) + EPS)
        xhat = xv * rstd
        gain = ng_ref[...]
        h = (xhat * gain).astype(BF16)
        for r0 in range(0, IN_WIDTH, ACC_ROWS):
            acc_ref[r0:r0 + ACC_ROWS, :] += _dot(dproj[:, r0:r0 + ACC_ROWS], h, TN)
        dh = _dot(dproj, w_ref[...])
        dng_ref[0:1, :] += jnp.sum(dh * xhat, axis=0, keepdims=True)
        dxh = dh * gain
        gx_ref[...] = dout_ref[...] + rstd * (dxh - xhat * jnp.mean(dxh * xhat, axis=-1, keepdims=True))

        @pl.when(step == n_tiles - 1)
        def _():
            for r0 in range(0, IN_WIDTH, ACC_ROWS):
                stage_ref[...] = acc_ref[r0:r0 + ACC_ROWS, :].astype(BF16)
                pltpu.sync_copy(stage_ref, dw_hbm.at[r0:r0 + ACC_ROWS, :])

    def tile(w):
        return pl.BlockSpec((TM, w), lambda i: (i, 0))

    def whole(a):
        return pl.BlockSpec(a.shape, lambda i: (0, 0))

    const = lambda i: (0, 0)
    return pl.pallas_call(
        body, name="bwd_in", grid=(n_tiles,),
        in_specs=[tile(D_MODEL), tile(D_MODEL), whole(norm_gain), pl.BlockSpec(memory_space=pl.ANY),
                  tile(512), tile(256), tile(256), tile(512), tile(128), tile(128), tile(128),
                  whole(q_gain2), whole(k_gain2), tile(512), tile(512), tile(512), tile(512), tile(512)],
        out_specs=[tile(D_MODEL), pl.BlockSpec(memory_space=pl.ANY), pl.BlockSpec((8, D_MODEL), const),
                   pl.BlockSpec((8, SLAB), const), pl.BlockSpec((8, SLAB), const)],
        out_shape=[jax.ShapeDtypeStruct((T_LOC, D_MODEL), F32), jax.ShapeDtypeStruct((IN_WIDTH, D_MODEL), BF16),
                   jax.ShapeDtypeStruct((8, D_MODEL), F32), jax.ShapeDtypeStruct((8, SLAB), F32),
                   jax.ShapeDtypeStruct((8, SLAB), F32)],
        scratch_shapes=[pltpu.VMEM((IN_WIDTH, D_MODEL), BF16), pltpu.VMEM((IN_WIDTH, D_MODEL), F32),
                        pltpu.VMEM((ACC_ROWS, D_MODEL), BF16), pltpu.SemaphoreType.DMA],
        compiler_params=_params(("arbitrary",)),
    )(x, dout, norm_gain, win_t, dq_rot, dk_dup, dv_dup, qa_raw, ka_raw, cos, sin_s, q_gain2, k_gain2,
      dga, dgb, dqb, dkb, dvb)


def _adamw(w, g, m, v):
    m = ADAM_B1 * m + (1.0 - ADAM_B1) * g
    v = ADAM_B2 * v + (1.0 - ADAM_B2) * (g * g)
    m_hat = m / (1.0 - ADAM_B1 ** ADAM_STEP)
    v_hat = v / (1.0 - ADAM_B2 ** ADAM_STEP)
    delta = -ADAM_LR * (m_hat / (jnp.sqrt(v_hat) + ADAM_EPS) + ADAM_WD * w)
    return delta, m, v


def _sum_slots(r_ref):
    g = r_ref[0].astype(F32)
    for s in range(1, r_ref.shape[0]):
        g = g + r_ref[s].astype(F32)
    return g


def adamw_w_in(recv, w, m, v):
    def body(r_ref, w_ref, m_ref, v_ref, g_ref, d_ref, nm_ref, nv_ref):
        g = _sum_slots(r_ref).T
        g_ref[...] = g
        d_ref[...], nm_ref[...], nv_ref[...] = _adamw(w_ref[...], g, m_ref[...], v_ref[...])

    return pl.pallas_call(
        body, name="adamw_w_in",
        out_shape=[jax.ShapeDtypeStruct(w.shape, F32)] * 4,
        compiler_params=pltpu.CompilerParams(vmem_limit_bytes=VMEM_LIMIT),
    )(recv, w, m, v)


def adamw_w_out(recv, w, m, v):
    def body(r_ref, w_ref, m_ref, v_ref, g_ref, d_ref, nm_ref, nv_ref):
        g = _sum_slots(r_ref)
        g_ref[...] = g
        d_ref[...], nm_ref[...], nv_ref[...] = _adamw(w_ref[...], g, m_ref[...], v_ref[...])

    return pl.pallas_call(
        body, name="adamw_w_out",
        out_shape=[jax.ShapeDtypeStruct(w.shape, F32)] * 4,
        compiler_params=pltpu.CompilerParams(vmem_limit_bytes=VMEM_LIMIT),
    )(recv, w, m, v)


def adamw_small(recv, weights, moments_m, moments_v):
    n = len(weights)

    def body(r_ref, *refs):
        ins, outs = refs[:3 * n], refs[3 * n:]
        s = _sum_slots(r_ref)
        eye = (_row((8, SLAB)) == _lane((8, SLAB))).astype(F32)
        sinks = jnp.sum(s[:, 1280:1408] * eye, axis=0, keepdims=True)
        grads = [s[0:1, :D_MODEL], s[0:1, 1024:1024 + HEAD_DIM], s[0:1, 1152:1152 + HEAD_DIM], sinks[:, :8]]
        for k in range(n):
            outs[k][...] = grads[k]
            outs[n + k][...], outs[2 * n + k][...], outs[3 * n + k][...] = _adamw(
                ins[k][...], grads[k], ins[n + k][...], ins[2 * n + k][...])
        loss = jnp.sum(jnp.sum(s[:, 1408:1536], axis=1, keepdims=True), axis=0, keepdims=True) * (0.5 / D_MODEL)
        outs[4 * n][...] = loss

    res = pl.pallas_call(
        body, name="adamw_small",
        out_shape=[jax.ShapeDtypeStruct(w.shape, F32) for w in weights] * 4 + [jax.ShapeDtypeStruct((1, 1), F32)],
        compiler_params=pltpu.CompilerParams(vmem_limit_bytes=VMEM_LIMIT),
    )(recv, *weights, *moments_m, *moments_v)
    return res[:n], res[n:2 * n], res[2 * n:3 * n], res[3 * n:4 * n], res[4 * n]


def kernel(x, positions, norm_gain, w_in, q_norm_gain, k_norm_gain, sinks, w_out, loss_target, m_norm_gain, m_w_in, m_q_norm_gain, m_k_norm_gain, m_sinks, m_w_out, v_norm_gain, v_w_in, v_q_norm_gain, v_k_norm_gain, v_sinks, v_w_out):
    x2 = x.reshape(T_LOC, D_MODEL)
    tgt2 = loss_target.reshape(T_LOC, D_MODEL)
    pos2 = positions.reshape(T_LOC, 1)
    half = HEAD_DIM // 2
    inv_freq = ROPE_THETA ** (-jnp.arange(half, dtype=F32) * 2.0 / HEAD_DIM)
    inv_freq = jnp.tile(inv_freq, SLAB // half).reshape(1, SLAB)
    sin_sign = jnp.tile(jnp.concatenate([-jnp.ones((half,), F32), jnp.ones((half,), F32)]), 2).reshape(1, SLAB)
    q_gain2 = jnp.tile(q_norm_gain, (1, 2))
    k_gain2 = jnp.tile(k_norm_gain, (1, 2))

    win_t, wout = gather_weights(w_in.reshape(D_MODEL, IN_SHARD).T.astype(BF16),
                                 w_out.reshape(OUT_SHARD, D_MODEL).astype(BF16))

    (qa_raw, ka_raw, q_rot, k_dup, v_dup, ga, qb, kb, vb, gb, cos, sin_s) = fwd_proj(
        x2, pos2, norm_gain, win_t, inv_freq, sin_sign, q_gain2, k_gain2)
    o_a = swa_fwd(q_rot, k_dup, v_dup, sinks)
    o_b, carries = sb_fwd(qb, kb, vb)
    dout, d_oa, d_ob, dga, dgb, dwout, loss_part = out_loss(o_a, o_b, ga, gb, x2, tgt2, wout)
    dq_rot, dk_dup, dv_dup, dsink = swa_bwd(q_rot, k_dup, v_dup, o_a, d_oa, sinks)
    dqb, dkb, dvb = sb_bwd(qb, kb, vb, d_ob, carries)
    grad_x, dwin_t, dng, dqg, dkg = bwd_in(
        x2, dout, norm_gain, win_t, dq_rot, dk_dup, dv_dup, qa_raw, ka_raw, cos, sin_s, q_gain2, k_gain2,
        dga, dgb, dqb, dkb, dvb)

    small = jnp.concatenate([dng, dqg, dkg, dsink, loss_part], axis=1)
    r_win, r_out, r_small = exchange_grads(dwin_t, dwout, small)

    w_in2, m_in2, v_in2 = (a.reshape(D_MODEL, IN_SHARD) for a in (w_in, m_w_in, v_w_in))
    w_out2, m_out2, v_out2 = (a.reshape(OUT_SHARD, D_MODEL) for a in (w_out, m_w_out, v_w_out))
    big_in = adamw_w_in(r_win, w_in2, m_in2, v_in2)
    big_out = adamw_w_out(r_out, w_out2, m_out2, v_out2)
    *small_out, loss = adamw_small(
        r_small, (norm_gain, q_norm_gain, k_norm_gain, sinks),
        (m_norm_gain, m_q_norm_gain, m_k_norm_gain, m_sinks), (v_norm_gain, v_q_norm_gain, v_k_norm_gain, v_sinks))

    def leaves(k):
        ng, qg, kg, sk = small_out[k]
        return (ng, big_in[k].reshape(1, D_MODEL, IN_SHARD), qg, kg, sk, big_out[k].reshape(1, OUT_SHARD, D_MODEL))

    return (loss.reshape(()), grad_x.reshape(B_LOC, SEQ, D_MODEL), *leaves(0), *leaves(1), *leaves(2), *leaves(3))
```

```python
import functools

import jax
import jax.numpy as jnp
from jax import lax
from jax.experimental import pallas as pl
from jax.experimental.pallas import tpu as pltpu

F32 = jnp.float32
BF16 = jnp.bfloat16

N_DEV = 8
D_MODEL = 1024
SEQ = 2048
B_LOC = 2
T_LOC = B_LOC * SEQ
HEAD_DIM = 64
HEAD_SHIFT = 6
BLK = 128
N_BLK = SEQ // BLK
SLAB = 128
IN_WIDTH = 3328
IN_SHARD = IN_WIDTH // N_DEV
OUT_SHARD = D_MODEL // N_DEV
EPS = 1e-6
ROPE_THETA = 10000.0
Q_SCALE = 0.125
R_QA, R_KA, R_VA, R_GA, R_QB, R_KB, R_VB, R_GB, R_END = 0, 512, 640, 768, 1280, 1792, 2304, 2816, 3328
SMALL_W = 1536
ADAM_LR, ADAM_B1, ADAM_B2, ADAM_EPS, ADAM_WD, ADAM_STEP = 0.001, 0.9, 0.999, 1e-08, 0.01, 10
TM = 256
ACC_ROWS = 256
VMEM_LIMIT = 56 * 1024 * 1024

MESH = pl.DeviceIdType.MESH
NT = (((1,), (1,)), ((), ()))
TN = (((0,), (0,)), ((), ()))


def _params(sem, limit=VMEM_LIMIT):
    return pltpu.CompilerParams(dimension_semantics=sem, vmem_limit_bytes=limit)


def _dot(a, b, dims=None):
    if dims is None:
        return jnp.dot(a, b, preferred_element_type=F32)
    return lax.dot_general(a, b, dims, preferred_element_type=F32)


def _split(x):
    hi = x.astype(BF16)
    return hi, (x - hi.astype(F32)).astype(BF16)


def _lane(shape):
    return lax.broadcasted_iota(jnp.int32, shape, len(shape) - 1)


def _row(shape):
    return lax.broadcasted_iota(jnp.int32, shape, 0)


def _head_blockdiag():
    return ((_row((SLAB, SLAB)) >> HEAD_SHIFT) == (_lane((SLAB, SLAB)) >> HEAD_SHIFT)).astype(BF16)


def _head_sum(x, bd):
    hi, lo = _split(x)
    return _dot(hi, bd) + _dot(lo, bd)


def _swap_half(y, lane):
    return jnp.where((lane & 32) != 0, pltpu.roll(y, 32, 1), pltpu.roll(y, 96, 1))


def _stack_heads(q, lane):
    zero = jnp.zeros_like(q)
    return jnp.concatenate([jnp.where(lane < HEAD_DIM, q, zero), jnp.where(lane >= HEAD_DIM, q, zero)], axis=0)


def _unstack_heads(x2, lane):
    return jnp.where(lane < HEAD_DIM, x2[:BLK], x2[BLK:])


def _sigmoid(x):
    return 1.0 / (1.0 + jnp.exp(-x))


def _mesh_pos():
    return lax.axis_index("x"), lax.axis_index("y"), lax.axis_index("c")


def _flip(pos, mask):
    return tuple(1 - p if m else p for p, m in zip(pos, mask))


def _lin(pos):
    return 4 * pos[0] + 2 * pos[1] + pos[2]


CHIP_FLIPS = [(0, 0), (1, 0), (0, 1), (1, 1)]


def gather_weights(win_t_shard, wout_shard):
    shards = (win_t_shard, wout_shard)
    n_arr = len(shards)

    def body(a_ref, b_ref, oa_ref, ob_ref, send_sems, recv_sems, local_sems):
        x, y, c = _mesh_pos()
        me, sibling = (x, y, c), (x, y, 1 - c)
        chips = [(1 - x, y), (x, 1 - y), (1 - x, 1 - y)]
        ins, outs = (a_ref, b_ref), (oa_ref, ob_ref)

        def rows(a, pos):
            m = ins[a].shape[0]
            return outs[a].at[pl.ds(_lin(pos) * m, m), :]

        def copy(a, k, block, to, src=None):
            return pltpu.make_async_remote_copy(
                src_ref=rows(a, block) if src is None else src, dst_ref=rows(a, block),
                send_sem=send_sems.at[a, k], recv_sem=recv_sems.at[a, k], device_id=to, device_id_type=MESH)

        mine = [pltpu.make_async_copy(ins[a], rows(a, me), local_sems.at[a]) for a in range(n_arr)]
        for cp in mine:
            cp.start()
        first = []
        for a in range(n_arr):
            first.append(copy(a, 0, me, sibling, src=ins[a]))
            first += [copy(a, 1 + j, me, (*chip, c), src=ins[a]) for j, chip in enumerate(chips)]
        for cp in first:
            cp.start()
        passed = [[copy(a, 4 + j, (*chip, c), sibling) for j, chip in enumerate(chips)] for a in range(n_arr)]
        for j, chip in enumerate(chips):
            for a in range(n_arr):
                copy(a, 1 + j, (*chip, c), me).wait_recv()
                passed[a][j].start()
        for a in range(n_arr):
            copy(a, 0, sibling, me).wait_recv()
            for j, chip in enumerate(chips):
                copy(a, 4 + j, (*chip, 1 - c), me).wait_recv()
        for cp in first + [p for ps in passed for p in ps]:
            cp.wait_send()
        for cp in mine:
            cp.wait()

    vmem = pl.BlockSpec(memory_space=pltpu.VMEM)
    return pl.pallas_call(
        body, name="gather_weights",
        out_shape=[jax.ShapeDtypeStruct((N_DEV * s.shape[0], s.shape[1]), s.dtype) for s in shards],
        in_specs=[vmem] * n_arr, out_specs=[vmem] * n_arr,
        scratch_shapes=[pltpu.SemaphoreType.DMA((n_arr, 7)), pltpu.SemaphoreType.DMA((n_arr, 7)),
                        pltpu.SemaphoreType.DMA((n_arr,))],
        compiler_params=pltpu.CompilerParams(vmem_limit_bytes=VMEM_LIMIT),
    )(*shards)


def exchange_grads(dwin_t, dwout, small):
    srcs = (dwin_t, dwout)
    blocks = (IN_SHARD, OUT_SHARD)
    n_arr = len(srcs)
    dev_masks = [(mx, my, mc) for mx in (0, 1) for my in (0, 1) for mc in (0, 1)][1:]

    def body(a_hbm, b_hbm, s_ref, ra_ref, rb_ref, rs_ref, own_a, own_b, sib_a, sib_b, snd_a, snd_b,
             d2d_send, d2d_recv, ici_send, ici_recv, own_sems, s_send, s_recv):
        x, y, c = _mesh_pos()
        me, sibling = (x, y, c), (x, y, 1 - c)
        ins, outs = (a_hbm, b_hbm), (ra_ref, rb_ref)
        own, sib, snd = (own_a, own_b), (sib_a, sib_b), (snd_a, snd_b)
        chips = [_flip((x, y), f) for f in CHIP_FLIPS]

        def rows(a, pos):
            return ins[a].at[pl.ds(_lin(pos) * blocks[a], blocks[a]), :]

        def to_sibling(a, k):
            return pltpu.make_async_remote_copy(
                src_ref=rows(a, (*chips[k], 1 - c)), dst_ref=sib[a].at[k],
                send_sem=d2d_send.at[a, k], recv_sem=d2d_recv.at[a, k], device_id=sibling, device_id_type=MESH)

        def to_chip(a, k):
            return pltpu.make_async_remote_copy(
                src_ref=snd[a].at[k - 1], dst_ref=outs[a].at[k],
                send_sem=ici_send.at[a, k - 1], recv_sem=ici_recv.at[a, k - 1],
                device_id=(*chips[k], c), device_id_type=MESH)

        def small_to(k, to):
            return pltpu.make_async_remote_copy(
                src_ref=s_ref, dst_ref=rs_ref.at[_lin(me)], send_sem=s_send.at[k], recv_sem=s_recv.at[k],
                device_id=to, device_id_type=MESH)

        def small_from(k, frm):
            return pltpu.make_async_remote_copy(
                src_ref=s_ref, dst_ref=rs_ref.at[_lin(frm)], send_sem=s_send.at[k], recv_sem=s_recv.at[k],
                device_id=frm, device_id_type=MESH)

        order = (1, 2, 3, 0)
        swaps = [to_sibling(a, k) for k in order for a in range(n_arr)]
        for cp in swaps:
            cp.start()
        mine = {(a, k): pltpu.make_async_copy(rows(a, (*chips[k], c)), own[a].at[k], own_sems.at[a, k])
                for k in order for a in range(n_arr)}
        for cp in mine.values():
            cp.start()
        smalls = [small_to(k, _flip(me, mask)) for k, mask in enumerate(dev_masks)]
        for cp in smalls:
            cp.start()
        rs_ref[_lin(me)] = s_ref[...]

        sent = []
        for k in order:
            for a in range(n_arr):
                to_sibling(a, k).wait_recv()
                mine[(a, k)].wait()
                total = (own[a][k].astype(F32) + sib[a][k].astype(F32)).astype(BF16)
                if k == 0:
                    outs[a][0] = total
                else:
                    snd[a][k - 1] = total
                    sent.append(to_chip(a, k))
                    sent[-1].start()
        for k in (1, 2, 3):
            for a in range(n_arr):
                to_chip(a, k).wait_recv()
        for k, mask in enumerate(dev_masks):
            small_from(k, _flip(me, mask)).wait_recv()
        for cp in swaps + sent + smalls:
            cp.wait_send()

    def bufs(n):
        return [pltpu.VMEM((n, blocks[a], D_MODEL), BF16) for a in range(n_arr)]

    vmem, hbm = pl.BlockSpec(memory_space=pltpu.VMEM), pl.BlockSpec(memory_space=pl.ANY)
    return pl.pallas_call(
        body, name="exchange_grads",
        out_shape=[jax.ShapeDtypeStruct((4, blocks[a], D_MODEL), BF16) for a in range(n_arr)]
        + [jax.ShapeDtypeStruct((N_DEV,) + small.shape, small.dtype)],
        in_specs=[hbm, hbm, vmem], out_specs=[vmem] * 3,
        scratch_shapes=bufs(4) + bufs(4) + bufs(3) + [
            pltpu.SemaphoreType.DMA((n_arr, 4)), pltpu.SemaphoreType.DMA((n_arr, 4)),
            pltpu.SemaphoreType.DMA((n_arr, 3)), pltpu.SemaphoreType.DMA((n_arr, 3)),
            pltpu.SemaphoreType.DMA((n_arr, 4)), pltpu.SemaphoreType.DMA((7,)), pltpu.SemaphoreType.DMA((7,))],
        compiler_params=pltpu.CompilerParams(vmem_limit_bytes=VMEM_LIMIT),
    )(dwin_t, dwout, small)


def _norm_rope(xs, gain2, cos, sin_s, bd, lane):
    r = lax.rsqrt(_head_sum(xs * xs, bd) * (1.0 / HEAD_DIM) + EPS)
    y = xs * r * gain2
    return y * cos + _swap_half(y, lane) * sin_s


def _dup_heads(xs, lane):
    r = pltpu.roll(xs, HEAD_DIM, 1)
    lo = lane < HEAD_DIM
    return jnp.concatenate([jnp.where(lo, xs, r), jnp.where(lo, r, xs)], axis=1)


def fwd_proj(x, pos, norm_gain, win_t, inv_freq, sin_sign, q_gain2, k_gain2):
    def body(x_ref, pos_ref, ng_ref, w_ref, if_ref, sg_ref, qg_ref, kg_ref,
             qa_raw_ref, ka_raw_ref, q_rot_ref, k_dup_ref, v_dup_ref, ga_ref, qb_ref, kb_ref, vb_ref, gb_ref,
             cos_ref, sin_ref):
        xv = x_ref[...]
        rstd = lax.rsqrt(jnp.mean(xv * xv, axis=-1, keepdims=True) + EPS)
        h = (xv * rstd * ng_ref[...]).astype(BF16)

        def proj(r0, r1):
            return _dot(h, w_ref[r0:r1, :], NT)

        ang = pos_ref[...].astype(F32) * if_ref[...]
        cos = jnp.cos(ang)
        sin_s = jnp.sin(ang) * sg_ref[...]
        cos_ref[...] = cos
        sin_ref[...] = sin_s
        lane = _lane((TM, SLAB))
        bd = _head_blockdiag()

        qa = proj(R_QA, R_KA)
        qa_raw_ref[...] = qa
        for p in range(4):
            sl = slice(p * SLAB, (p + 1) * SLAB)
            q_rot_ref[:, sl] = (_norm_rope(qa[:, sl], qg_ref[...], cos, sin_s, bd, lane) * Q_SCALE).astype(BF16)
        ka = proj(R_KA, R_VA)
        ka_raw_ref[...] = ka
        k_dup_ref[...] = _dup_heads(_norm_rope(ka, kg_ref[...], cos, sin_s, bd, lane), lane).astype(BF16)
        v_dup_ref[...] = _dup_heads(proj(R_VA, R_GA), lane).astype(BF16)
        ga_ref[...] = proj(R_GA, R_QB)
        qb_ref[...] = (proj(R_QB, R_KB) * Q_SCALE).astype(BF16)
        kb_ref[...] = proj(R_KB, R_VB).astype(BF16)
        vb_ref[...] = proj(R_VB, R_GB).astype(BF16)
        gb_ref[...] = proj(R_GB, R_END)

    def tile(w):
        return pl.BlockSpec((TM, w), lambda i: (i, 0))

    def whole(a):
        return pl.BlockSpec(a.shape, lambda i: (0, 0))

    widths = [(512, F32), (128, F32), (512, BF16), (256, BF16), (256, BF16), (512, F32), (512, BF16), (512, BF16),
              (512, BF16), (512, F32), (128, F32), (128, F32)]
    return pl.pallas_call(
        body, name="fwd_proj", grid=(T_LOC // TM,),
        in_specs=[tile(D_MODEL), tile(1), whole(norm_gain), whole(win_t), whole(inv_freq), whole(sin_sign),
                  whole(q_gain2), whole(k_gain2)],
        out_specs=[tile(w) for w, _ in widths],
        out_shape=[jax.ShapeDtypeStruct((T_LOC, w), dt) for w, dt in widths],
        compiler_params=_params(("arbitrary",)),
    )(x, pos, norm_gain, win_t, inv_freq, sin_sign, q_gain2, k_gain2)


def _swa_window(prev_ref, cur_ref, p):
    gsl = _slab(p // 2)
    return jnp.concatenate([prev_ref[:, gsl], cur_ref[:, gsl]], axis=0)


def _swa_probs(s, sinks_ref, p, i):
    shape = (2 * BLK, 2 * BLK)
    r = _row(shape) & (BLK - 1)
    cidx = _lane(shape)
    valid = (cidx > r) & (cidx <= r + BLK) & ((cidx >= BLK) | (i > 0))
    s = jnp.where(valid, s, -jnp.inf)
    sink = jnp.where(_row((2 * BLK, 1)) < BLK, sinks_ref[0, 2 * p], sinks_ref[0, 2 * p + 1])
    m = jnp.maximum(jnp.max(s, axis=-1, keepdims=True), sink)
    e = jnp.exp(s - m)
    e_sink = jnp.exp(sink - m)
    den = jnp.sum(e, axis=-1, keepdims=True) + e_sink
    return e / den, e_sink / den


def _swa_specs():
    def cur(w):
        return pl.BlockSpec((BLK, w), lambda b, i: (b * N_BLK + i, 0))

    def prev(w):
        return pl.BlockSpec((BLK, w), lambda b, i: (b * N_BLK + jnp.maximum(i - 1, 0), 0))

    return cur, prev


def swa_fwd(q_rot, k_dup, v_dup, sinks):
    def body(q_ref, kp_ref, kc_ref, vp_ref, vc_ref, sinks_ref, o_ref):
        i = pl.program_id(1)
        lane = _lane((BLK, SLAB))
        pairs = range(4)
        s = [_dot(_stack_heads(q_ref[:, _slab(p)], lane), _swa_window(kp_ref, kc_ref, p), NT) for p in pairs]
        pn = [_swa_probs(s[p], sinks_ref, p, i)[0].astype(BF16) for p in pairs]
        for p in pairs:
            o_ref[:, _slab(p)] = _unstack_heads(_dot(pn[p], _swa_window(vp_ref, vc_ref, p)), lane)

    cur, prev = _swa_specs()
    return pl.pallas_call(
        body, name="swa_fwd", grid=(B_LOC, N_BLK),
        in_specs=[cur(512), prev(256), cur(256), prev(256), cur(256), pl.BlockSpec(memory_space=pltpu.SMEM)],
        out_specs=cur(512),
        out_shape=jax.ShapeDtypeStruct((T_LOC, 512), F32),
        compiler_params=_params(("arbitrary", "arbitrary")),
    )(q_rot, k_dup, k_dup, v_dup, v_dup, sinks)


def _tri(suffix):
    r, cidx = _row((BLK + 16, 2 * BLK)), _lane((BLK + 16, 2 * BLK)) & (BLK - 1)
    tri = (cidx > r) if suffix else (cidx < r)
    return (tri | (r >= BLK)).astype(BF16)


def _key_sums(tri, x):
    hi, lo = _split(x)
    res = _dot(tri, jnp.concatenate([hi, lo], axis=0))
    return res[:BLK], res[BLK:BLK + 1]


def _sb_softplus(zt, valid):
    neg_abs = lax.bitcast_convert_type(lax.bitcast_convert_type(zt, jnp.uint32) | jnp.uint32(0x80000000), F32)
    sp = jnp.maximum(zt, 0.0) + jnp.log(1.0 + jnp.exp(neg_abs))
    return sp if valid is None else jnp.where(valid, sp, 0.0)


def _sb_weights(zt, sp, later, valid):
    w = jnp.exp(zt - sp - later)
    return w if valid is None else jnp.where(valid, w, 0.0)


def _slab(pp):
    return slice(pp * SLAB, (pp + 1) * SLAB)


def _blk(j):
    return pl.ds(pl.multiple_of(j * BLK, BLK), BLK)


def _causal_t():
    return _row((BLK, 2 * BLK)) < (_lane((BLK, 2 * BLK)) & (BLK - 1))


def _sb_rows(b, j):
    return pl.ds(pl.multiple_of(b * SEQ + j * BLK, BLK), BLK)


SB_CHAINS = [(b, pp) for b in range(B_LOC) for pp in range(4)]


def sb_fwd(qb, kb, vb):
    def body(q_ref, k_ref, v_ref, o_ref, c_ref, vt_ref, ot_ref):
        for c, (b, pp) in enumerate(SB_CHAINS):
            for j in range(N_BLK):
                vt_ref[c, j] = v_ref[b * SEQ + j * BLK:b * SEQ + (j + 1) * BLK, _slab(pp)].T
        lane = _lane((BLK, SLAB))
        tri = _tri(True)
        valid = _causal_t()
        jrow = _row((N_BLK, 2 * BLK))
        chains = range(len(SB_CHAINS))

        def q_block(i, _):
            q2 = [_stack_heads(q_ref[_sb_rows(b, i), _slab(pp)], lane) for b, pp in SB_CHAINS]

            def key_block(j, carry, mask, first):
                zt = [_dot(k_ref[_sb_rows(b, j), _slab(pp)], q2[c], NT) for c, (b, pp) in enumerate(SB_CHAINS)]
                sp = [_sb_softplus(zt[c], mask) for c in chains]
                sums = [_key_sums(tri, sp[c]) for c in chains]
                w = [_sb_weights(zt[c], sp[c], sums[c][0] + carry[c], mask) for c in chains]
                for c in chains:
                    pv = _dot(vt_ref[c, j], w[c].astype(BF16))
                    if first:
                        ot_ref[c] = pv
                    else:
                        ot_ref[c] += pv
                return tuple(carry[c] + sums[c][1] for c in chains)

            def earlier(jj, state):
                carry, saved = state
                j = i - 1 - jj
                saved = tuple(jnp.where(jrow == j, carry[c], saved[c]) for c in chains)
                return key_block(j, carry, None, False), saved

            zero = tuple(jnp.zeros((1, 2 * BLK), F32) for _ in chains)
            carry = key_block(i, zero, valid, True)
            _, saved = lax.fori_loop(0, i, earlier, (carry, tuple(jnp.zeros((N_BLK, 2 * BLK), F32) for _ in chains)))
            for c, (b, pp) in enumerate(SB_CHAINS):
                o_ref[_sb_rows(b, i), _slab(pp)] = _unstack_heads(ot_ref[c].T, lane)
                c_ref[c * N_BLK + i] = saved[c]
            return 0

        lax.fori_loop(0, N_BLK, q_block, 0)

    n_ch = len(SB_CHAINS)
    vmem = pl.BlockSpec(memory_space=pltpu.VMEM)
    return pl.pallas_call(
        body, name="sb_fwd",
        in_specs=[vmem] * 3, out_specs=[vmem] * 2,
        out_shape=[jax.ShapeDtypeStruct((T_LOC, 512), F32), jax.ShapeDtypeStruct((n_ch * N_BLK, N_BLK, 2 * BLK), F32)],
        scratch_shapes=[pltpu.VMEM((n_ch, N_BLK, SLAB, BLK), BF16), pltpu.VMEM((n_ch, SLAB, 2 * BLK), F32)],
        compiler_params=pltpu.CompilerParams(vmem_limit_bytes=VMEM_LIMIT),
    )(qb, kb, vb)


def out_loss(o_a, o_b, ga, gb, x, target, wout):
    n_tiles = T_LOC // TM

    def body(oa_ref, ob_ref, ga_ref, gb_ref, x_ref, t_ref, w_ref,
             dout_ref, doa_ref, dob_ref, dga_ref, dgb_ref, dw_ref, loss_ref, acc_ref):
        step = pl.program_id(0)

        @pl.when(step == 0)
        def _():
            acc_ref[...] = jnp.zeros_like(acc_ref)
            loss_ref[...] = jnp.zeros_like(loss_ref)

        oa, ob, gav, gbv = oa_ref[...], ob_ref[...], ga_ref[...], gb_ref[...]
        sa, sb = _sigmoid(gav), _sigmoid(gbv)
        silu_a, silu_b = gav * sa, gbv * sb
        y = jnp.concatenate([oa * silu_a, ob * silu_b], axis=1).astype(BF16)
        err = x_ref[...] + _dot(y, w_ref[...]) - t_ref[...]
        e2 = err * err
        part = jnp.sum(e2.reshape(TM // 8, 8, D_MODEL), axis=0)
        loss_ref[...] += functools.reduce(lambda a, b: a + b, [part[:, k * 128:(k + 1) * 128] for k in range(8)])
        dout = err * (1.0 / D_MODEL)
        dout_ref[...] = dout
        dob16 = dout.astype(BF16)
        for r0 in range(0, D_MODEL, ACC_ROWS):
            acc_ref[r0:r0 + ACC_ROWS, :] += _dot(y[:, r0:r0 + ACC_ROWS], dob16, TN)
        dy = _dot(dob16, w_ref[...], NT)
        dya, dyb = dy[:, :512], dy[:, 512:]
        doa_ref[...] = (dya * silu_a).astype(BF16)
        dob_ref[...] = (dyb * silu_b).astype(BF16)
        dga_ref[...] = (dya * oa * (sa * (1.0 + gav * (1.0 - sa)))).astype(BF16)
        dgb_ref[...] = (dyb * ob * (sb * (1.0 + gbv * (1.0 - sb)))).astype(BF16)

        @pl.when(step == n_tiles - 1)
        def _():
            dw_ref[...] = acc_ref[...].astype(BF16)

    def tile(w):
        return pl.BlockSpec((TM, w), lambda i: (i, 0))

    const = lambda i: (0, 0)
    return pl.pallas_call(
        body, name="out_loss", grid=(n_tiles,),
        in_specs=[tile(512)] * 4 + [tile(D_MODEL)] * 2 + [pl.BlockSpec((D_MODEL, D_MODEL), const)],
        out_specs=[tile(D_MODEL), tile(512), tile(512), tile(512), tile(512),
                   pl.BlockSpec((D_MODEL, D_MODEL), const), pl.BlockSpec((8, 128), const)],
        out_shape=[jax.ShapeDtypeStruct((T_LOC, D_MODEL), F32)] + [jax.ShapeDtypeStruct((T_LOC, 512), BF16)] * 4
        + [jax.ShapeDtypeStruct((D_MODEL, D_MODEL), BF16), jax.ShapeDtypeStruct((8, 128), F32)],
        scratch_shapes=[pltpu.VMEM((D_MODEL, D_MODEL), F32)],
        compiler_params=_params(("arbitrary",)),
    )(o_a, o_b, ga, gb, x, target, wout)


def swa_bwd(q_rot, k_dup, v_dup, o_a, d_oa, sinks):
    def body(q_ref, kp_ref, kc_ref, vp_ref, vc_ref, o_ref, do_ref, sinks_ref, dq_ref, dk_ref, dv_ref, dsink_ref):
        b, i = pl.program_id(0), pl.program_id(1)

        @pl.when(i == 0)
        def _():
            dk_ref[...] = jnp.zeros_like(dk_ref)
            dv_ref[...] = jnp.zeros_like(dv_ref)

        @pl.when((i == 0) & (b == 0))
        def _():
            dsink_ref[...] = jnp.zeros_like(dsink_ref)

        lane = _lane((BLK, SLAB))
        rows_prev, rows_cur = _blk(jnp.maximum(i - 1, 0)), _blk(i)
        pairs = range(4)
        q2 = [_stack_heads(q_ref[:, _slab(p)], lane) for p in pairs]
        do2 = [_stack_heads(do_ref[:, _slab(p)], lane) for p in pairs]
        keys = [_swa_window(kp_ref, kc_ref, p) for p in pairs]
        s = [_dot(q2[p], keys[p], NT) for p in pairs]
        dp = [_dot(do2[p], _swa_window(vp_ref, vc_ref, p), NT) for p in pairs]
        ds, pn16, cols = [], [], []
        for p in pairs:
            pn, p_sink = _swa_probs(s[p], sinks_ref, p, i)
            o = o_ref[:, _slab(p)]
            delta = jnp.sum(do2[p].astype(F32) * jnp.concatenate([o, o], axis=0), axis=-1, keepdims=True)
            ds.append((pn * (dp[p] - delta)).astype(BF16))
            pn16.append(pn.astype(BF16))
            cols.append(-p_sink * delta)
        for p in pairs:
            dq_ref[:, _slab(p)] = _unstack_heads(_dot(ds[p], keys[p]), lane) * Q_SCALE
        dk2 = [_dot(ds[p], q2[p], TN) for p in pairs]
        dv2 = [_dot(pn16[p], do2[p], TN) for p in pairs]
        for p in pairs:
            gsl = _slab(p // 2)
            dk_ref[rows_prev, gsl] += dk2[p][:BLK]
            dk_ref[rows_cur, gsl] += dk2[p][BLK:]
            dv_ref[rows_prev, gsl] += dv2[p][:BLK]
            dv_ref[rows_cur, gsl] += dv2[p][BLK:]
            for e in range(2):
                dsink_ref[2 * p + e:2 * p + e + 1, :] += jnp.sum(cols[p][e * BLK:(e + 1) * BLK], axis=0, keepdims=True)

    cur, prev = _swa_specs()
    per_seq = pl.BlockSpec((SEQ, 256), lambda b, i: (b, 0))
    return pl.pallas_call(
        body, name="swa_bwd", grid=(B_LOC, N_BLK),
        in_specs=[cur(512), prev(256), cur(256), prev(256), cur(256), cur(512), cur(512),
                  pl.BlockSpec(memory_space=pltpu.SMEM)],
        out_specs=[cur(512), per_seq, per_seq, pl.BlockSpec((8, 128), lambda b, i: (0, 0))],
        out_shape=[jax.ShapeDtypeStruct((T_LOC, 512), F32), jax.ShapeDtypeStruct((T_LOC, 256), F32),
                   jax.ShapeDtypeStruct((T_LOC, 256), F32), jax.ShapeDtypeStruct((8, 128), F32)],
        compiler_params=_params(("arbitrary", "arbitrary")),
    )(q_rot, k_dup, k_dup, v_dup, v_dup, o_a, d_oa, sinks)


def sb_bwd(qb, kb, vb, d_ob, carries):
    def body(q_ref, k_ref, v_ref, do_ref, c_ref, dq_ref, dk_ref, dv_ref, kt_ref, dqt_ref):
        for c, (b, pp) in enumerate(SB_CHAINS):
            for j in range(N_BLK):
                kt_ref[c, j] = k_ref[b * SEQ + j * BLK:b * SEQ + (j + 1) * BLK, _slab(pp)].T
        dk_ref[...] = jnp.zeros_like(dk_ref)
        dv_ref[...] = jnp.zeros_like(dv_ref)
        dqt_ref[...] = jnp.zeros_like(dqt_ref)
        lane = _lane((BLK, SLAB))
        tri_after, tri_before = _tri(True), _tri(False)
        valid = _causal_t()
        jrow = _row((N_BLK, 2 * BLK))
        chains = range(len(SB_CHAINS))

        def q_block(i, _):
            q2 = [_stack_heads(q_ref[_sb_rows(b, i), _slab(pp)], lane) for b, pp in SB_CHAINS]
            do2 = [_stack_heads(do_ref[_sb_rows(b, i), _slab(pp)], lane) for b, pp in SB_CHAINS]

            def key_block(j, carry_sp, before_u, mask):
                at = [(_sb_rows(b, j), _slab(pp)) for b, pp in SB_CHAINS]
                zt = [_dot(k_ref[at[c]], q2[c], NT) for c in chains]
                dw = [_dot(v_ref[at[c]], do2[c], NT) for c in chains]
                sp = [_sb_softplus(zt[c], mask) for c in chains]
                later = [_key_sums(tri_after, sp[c])[0] for c in chains]
                w = [_sb_weights(zt[c], sp[c], later[c] + carry_sp[c], mask) for c in chains]
                u = [dw[c] * w[c] for c in chains]
                for c in chains:
                    dv_ref[at[c]] += _dot(w[c].astype(BF16), do2[c])
                sums = [_key_sums(tri_before, u[c]) for c in chains]
                dz16 = []
                for c in chains:
                    sig = jnp.exp(zt[c] - sp[c])
                    dz = u[c] * (1.0 - sig) - (before_u[c] + sums[c][0]) * sig
                    if mask is not None:
                        dz = jnp.where(mask, dz, 0.0)
                    dz16.append(dz.astype(BF16))
                for c in chains:
                    dk_ref[at[c]] += _dot(dz16[c], q2[c])
                    dqt_ref[c] += _dot(kt_ref[c, j], dz16[c])
                return tuple(before_u[c] + sums[c][1] for c in chains)

            def earlier(j, before_u):
                carry_sp = [jnp.sum(jnp.where(jrow == j, c_ref[c * N_BLK + i], 0.0), axis=0, keepdims=True)
                            for c in chains]
                return key_block(j, carry_sp, before_u, None)

            zero = tuple(jnp.zeros((1, 2 * BLK), F32) for _ in chains)
            before_u = lax.fori_loop(0, i, earlier, zero)
            key_block(i, zero, before_u, valid)
            for c, (b, pp) in enumerate(SB_CHAINS):
                dq_ref[_sb_rows(b, i), _slab(pp)] = (_unstack_heads(dqt_ref[c].T, lane) * Q_SCALE).astype(BF16)
                dqt_ref[c] = jnp.zeros((SLAB, 2 * BLK), F32)
            return 0

        lax.fori_loop(0, N_BLK, q_block, 0)

    n_ch = len(SB_CHAINS)
    vmem = pl.BlockSpec(memory_space=pltpu.VMEM)
    return pl.pallas_call(
        body, name="sb_bwd",
        in_specs=[vmem] * 5, out_specs=[vmem] * 3,
        out_shape=[jax.ShapeDtypeStruct((T_LOC, 512), BF16)] + [jax.ShapeDtypeStruct((T_LOC, 512), F32)] * 2,
        scratch_shapes=[pltpu.VMEM((n_ch, N_BLK, SLAB, BLK), BF16), pltpu.VMEM((n_ch, SLAB, 2 * BLK), F32)],
        compiler_params=pltpu.CompilerParams(vmem_limit_bytes=VMEM_LIMIT),
    )(qb, kb, vb, d_ob, carries)


def bwd_in(x, dout, norm_gain, win_t, dq_rot, dk_dup, dv_dup, qa_raw, ka_raw, cos, sin_s, q_gain2, k_gain2,
           dga, dgb, dqb, dkb, dvb):
    n_tiles = T_LOC // TM

    def body(x_ref, dout_ref, ng_ref, w_hbm, dq_ref, dk_ref, dv_ref, qa_ref, ka_ref, cos_ref, sin_ref, qg_ref, kg_ref,
             dga_ref, dgb_ref, dqb_ref, dkb_ref, dvb_ref,
             gx_ref, dw_hbm, dng_ref, dqg_ref, dkg_ref, w_ref, acc_ref, stage_ref, dp_even_ref, dp_odd_ref, w_sem):
        step = pl.program_id(0)

        @pl.when(step == 0)
        def _():
            cp = pltpu.make_async_copy(w_hbm, w_ref, w_sem)
            cp.start()
            acc_ref[...] = jnp.zeros_like(acc_ref)
            dng_ref[...] = jnp.zeros_like(dng_ref)
            dqg_ref[...] = jnp.zeros_like(dqg_ref)
            dkg_ref[...] = jnp.zeros_like(dkg_ref)
            dp_odd_ref[...] = jnp.zeros_like(dp_odd_ref)
            cp.wait()

        lane = _lane((TM, SLAB))
        bd = _head_blockdiag()

        def norm_rope_bwd(d_rot, raw, gain2, cos, sin_s):
            dy = d_rot * cos + _swap_half(d_rot * sin_s, lane)
            r = lax.rsqrt(_head_sum(raw * raw, bd) * (1.0 / HEAD_DIM) + EPS)
            xhat = raw * r
            dgain = jnp.sum(dy * xhat, axis=0, keepdims=True)
            dxh = dy * gain2
            mean = _head_sum(dxh * xhat, bd) * (1.0 / HEAD_DIM)
            return r * (dxh - xhat * mean), dgain

        def fold_dup(d_dup):
            a, b2 = d_dup[:, :SLAB], d_dup[:, SLAB:]
            return jnp.where(lane < HEAD_DIM, a + pltpu.roll(a, HEAD_DIM, 1), b2 + pltpu.roll(b2, HEAD_DIM, 1))

        def assemble():
            cos, sin_s = cos_ref[...], sin_ref[...]
            pieces = []
            dqg = jnp.zeros((1, SLAB), F32)
            for p in range(4):
                d_raw, dg = norm_rope_bwd(dq_ref[:, _slab(p)], qa_ref[:, _slab(p)], qg_ref[...], cos, sin_s)
                pieces.append(d_raw.astype(BF16))
                dqg = dqg + dg
            d_raw, dkg = norm_rope_bwd(fold_dup(dk_ref[...]), ka_ref[...], kg_ref[...], cos, sin_s)
            pieces.append(d_raw.astype(BF16))
            pieces.append(fold_dup(dv_ref[...]).astype(BF16))
            pieces += [dga_ref[...], dqb_ref[...], dkb_ref[...].astype(BF16), dvb_ref[...].astype(BF16),
                       dgb_ref[...]]
            return jnp.concatenate(pieces, axis=1), dqg, dkg

        def backprop(dproj):
            xv = x_ref[...]
            rstd = lax.rsqrt(jnp.mean(xv * xv, axis=-1, keepdims=True) + EPS)
            xhat = xv * rstd
            gain = ng_ref[...]
            h = (xhat * gain).astype(BF16)
            for r0 in range(0, IN_WIDTH, ACC_ROWS):
                acc_ref[r0:r0 + ACC_ROWS, :] += _dot(dproj[:, r0:r0 + ACC_ROWS], h, TN)
            dh = _dot(dproj, w_ref[...])
            dng_ref[0:1, :] += jnp.sum(dh * xhat, axis=0, keepdims=True)
            dxh = dh * gain
            gx_ref[...] = dout_ref[...] + rstd * (dxh - xhat * jnp.mean(dxh * xhat, axis=-1, keepdims=True))

        fresh = (step < n_tiles).astype(F32)

        def skewed(read_ref, write_ref):
            backprop(read_ref[...])
            dproj, dqg, dkg = assemble()
            write_ref[...] = dproj
            dqg_ref[0:1, :] += fresh * (dqg + pltpu.roll(dqg, HEAD_DIM, 1))
            dkg_ref[0:1, :] += fresh * (dkg + pltpu.roll(dkg, HEAD_DIM, 1))

        @pl.when((step & 1) == 0)
        def _():
            skewed(dp_odd_ref, dp_even_ref)

        @pl.when((step & 1) == 1)
        def _():
            skewed(dp_even_ref, dp_odd_ref)

        @pl.when(step == n_tiles)
        def _():
            for r0 in range(0, IN_WIDTH, ACC_ROWS):
                stage_ref[...] = acc_ref[r0:r0 + ACC_ROWS, :].astype(BF16)
                pltpu.sync_copy(stage_ref, dw_hbm.at[r0:r0 + ACC_ROWS, :])

    def ahead(w):
        return pl.BlockSpec((TM, w), lambda i: (jnp.minimum(i, n_tiles - 1), 0))

    def behind(w):
        return pl.BlockSpec((TM, w), lambda i: (jnp.maximum(i - 1, 0), 0))

    def whole(a):
        return pl.BlockSpec(a.shape, lambda i: (0, 0))

    const = lambda i: (0, 0)
    return pl.pallas_call(
        body, name="bwd_in", grid=(n_tiles + 1,),
        in_specs=[behind(D_MODEL), behind(D_MODEL), whole(norm_gain), pl.BlockSpec(memory_space=pl.ANY),
                  ahead(512), ahead(256), ahead(256), ahead(512), ahead(128), ahead(128), ahead(128),
                  whole(q_gain2), whole(k_gain2), ahead(512), ahead(512), ahead(512), ahead(512), ahead(512)],
        out_specs=[behind(D_MODEL), pl.BlockSpec(memory_space=pl.ANY), pl.BlockSpec((8, D_MODEL), const),
                   pl.BlockSpec((8, SLAB), const), pl.BlockSpec((8, SLAB), const)],
        out_shape=[jax.ShapeDtypeStruct((T_LOC, D_MODEL), F32), jax.ShapeDtypeStruct((IN_WIDTH, D_MODEL), BF16),
                   jax.ShapeDtypeStruct((8, D_MODEL), F32), jax.ShapeDtypeStruct((8, SLAB), F32),
                   jax.ShapeDtypeStruct((8, SLAB), F32)],
        scratch_shapes=[pltpu.VMEM((IN_WIDTH, D_MODEL), BF16), pltpu.VMEM((IN_WIDTH, D_MODEL), F32),
                        pltpu.VMEM((ACC_ROWS, D_MODEL), BF16), pltpu.VMEM((TM, IN_WIDTH), BF16),
                        pltpu.VMEM((TM, IN_WIDTH), BF16), pltpu.SemaphoreType.DMA],
        compiler_params=_params(("arbitrary",)),
    )(x, dout, norm_gain, win_t, dq_rot, dk_dup, dv_dup, qa_raw, ka_raw, cos, sin_s, q_gain2, k_gain2,
      dga, dgb, dqb, dkb, dvb)


def _adamw(w, g, m, v):
    m = ADAM_B1 * m + (1.0 - ADAM_B1) * g
    v = ADAM_B2 * v + (1.0 - ADAM_B2) * (g * g)
    m_hat = m / (1.0 - ADAM_B1 ** ADAM_STEP)
    v_hat = v / (1.0 - ADAM_B2 ** ADAM_STEP)
    delta = -ADAM_LR * (m_hat / (jnp.sqrt(v_hat) + ADAM_EPS) + ADAM_WD * w)
    return delta, m, v


def _sum_slots(r_ref):
    g = r_ref[0].astype(F32)
    for s in range(1, r_ref.shape[0]):
        g = g + r_ref[s].astype(F32)
    return g


def adamw_rows(name, recv, w, m, v):
    def body(r_ref, w_ref, m_ref, v_ref, g_ref, d_ref, nm_ref, nv_ref):
        g = _sum_slots(r_ref)
        g_ref[...] = g
        d_ref[...], nm_ref[...], nv_ref[...] = _adamw(w_ref[...], g, m_ref[...], v_ref[...])

    return pl.pallas_call(
        body, name=name,
        out_shape=[jax.ShapeDtypeStruct(w.shape, F32)] * 4,
        compiler_params=pltpu.CompilerParams(vmem_limit_bytes=VMEM_LIMIT),
    )(recv, w, m, v)


def adamw_small(recv, weights, moments_m, moments_v):
    n = len(weights)

    def body(r_ref, *refs):
        ins, outs = refs[:3 * n], refs[3 * n:]
        s = _sum_slots(r_ref)
        eye = (_row((8, SLAB)) == _lane((8, SLAB))).astype(F32)
        sinks = jnp.sum(s[:, 1280:1408] * eye, axis=0, keepdims=True)
        grads = [s[0:1, :D_MODEL], s[0:1, 1024:1024 + HEAD_DIM], s[0:1, 1152:1152 + HEAD_DIM], sinks[:, :8]]
        for k in range(n):
            outs[k][...] = grads[k]
            outs[n + k][...], outs[2 * n + k][...], outs[3 * n + k][...] = _adamw(
                ins[k][...], grads[k], ins[n + k][...], ins[2 * n + k][...])
        loss = jnp.sum(jnp.sum(s[:, 1408:1536], axis=1, keepdims=True), axis=0, keepdims=True) * (0.5 / D_MODEL)
        outs[4 * n][...] = loss

    res = pl.pallas_call(
        body, name="adamw_small",
        out_shape=[jax.ShapeDtypeStruct(w.shape, F32) for w in weights] * 4 + [jax.ShapeDtypeStruct((1, 1), F32)],
        compiler_params=pltpu.CompilerParams(vmem_limit_bytes=VMEM_LIMIT),
    )(recv, *weights, *moments_m, *moments_v)
    return res[:n], res[n:2 * n], res[2 * n:3 * n], res[3 * n:4 * n], res[4 * n]


def kernel(x, positions, norm_gain, w_in, q_norm_gain, k_norm_gain, sinks, w_out, loss_target, m_norm_gain, m_w_in, m_q_norm_gain, m_k_norm_gain, m_sinks, m_w_out, v_norm_gain, v_w_in, v_q_norm_gain, v_k_norm_gain, v_sinks, v_w_out):
    x2 = x.reshape(T_LOC, D_MODEL)
    tgt2 = loss_target.reshape(T_LOC, D_MODEL)
    pos2 = positions.reshape(T_LOC, 1)
    half = HEAD_DIM // 2
    inv_freq = ROPE_THETA ** (-jnp.arange(half, dtype=F32) * 2.0 / HEAD_DIM)
    inv_freq = jnp.tile(inv_freq, SLAB // half).reshape(1, SLAB)
    sin_sign = jnp.tile(jnp.concatenate([-jnp.ones((half,), F32), jnp.ones((half,), F32)]), 2).reshape(1, SLAB)
    q_gain2 = jnp.tile(q_norm_gain, (1, 2))
    k_gain2 = jnp.tile(k_norm_gain, (1, 2))

    win_t, wout = gather_weights(w_in.reshape(D_MODEL, IN_SHARD).T.astype(BF16),
                                 w_out.reshape(OUT_SHARD, D_MODEL).astype(BF16))

    (qa_raw, ka_raw, q_rot, k_dup, v_dup, ga, qb, kb, vb, gb, cos, sin_s) = fwd_proj(
        x2, pos2, norm_gain, win_t, inv_freq, sin_sign, q_gain2, k_gain2)
    o_a = swa_fwd(q_rot, k_dup, v_dup, sinks)
    o_b, carries = sb_fwd(qb, kb, vb)
    dout, d_oa, d_ob, dga, dgb, dwout, loss_part = out_loss(o_a, o_b, ga, gb, x2, tgt2, wout)
    dq_rot, dk_dup, dv_dup, dsink = swa_bwd(q_rot, k_dup, v_dup, o_a, d_oa, sinks)
    dqb, dkb, dvb = sb_bwd(qb, kb, vb, d_ob, carries)
    grad_x, dwin_t, dng, dqg, dkg = bwd_in(
        x2, dout, norm_gain, win_t, dq_rot, dk_dup, dv_dup, qa_raw, ka_raw, cos, sin_s, q_gain2, k_gain2,
        dga, dgb, dqb, dkb, dvb)

    small = jnp.concatenate([dng, dqg, dkg, dsink, loss_part], axis=1)
    r_win, r_out, r_small = exchange_grads(dwin_t, dwout, small)

    w_in2, m_in2, v_in2 = (a.reshape(D_MODEL, IN_SHARD).T for a in (w_in, m_w_in, v_w_in))
    w_out2, m_out2, v_out2 = (a.reshape(OUT_SHARD, D_MODEL) for a in (w_out, m_w_out, v_w_out))
    big_in = adamw_rows("adamw_w_in", r_win, w_in2, m_in2, v_in2)
    big_out = adamw_rows("adamw_w_out", r_out, w_out2, m_out2, v_out2)
    *small_out, loss = adamw_small(
        r_small, (norm_gain, q_norm_gain, k_norm_gain, sinks),
        (m_norm_gain, m_q_norm_gain, m_k_norm_gain, m_sinks), (v_norm_gain, v_q_norm_gain, v_k_norm_gain, v_sinks))

    def leaves(k):
        ng, qg, kg, sk = small_out[k]
        return (ng, big_in[k].T.reshape(1, D_MODEL, IN_SHARD), qg, kg, sk, big_out[k].reshape(1, OUT_SHARD, D_MODEL))

    return (loss.reshape(()), grad_x.reshape(B_LOC, SEQ, D_MODEL), *leaves(0), *leaves(1), *leaves(2), *leaves(3))
```

```python
import functools

import jax
import jax.numpy as jnp
from jax import lax
from jax.experimental import pallas as pl
from jax.experimental.pallas import tpu as pltpu

F32 = jnp.float32
BF16 = jnp.bfloat16

N_DEV = 8
D_MODEL = 1024
SEQ = 2048
B_LOC = 2
T_LOC = B_LOC * SEQ
HEAD_DIM = 64
HEAD_SHIFT = 6
BLK = 128
N_BLK = SEQ // BLK
SLAB = 128
IN_WIDTH = 3328
IN_SHARD = IN_WIDTH // N_DEV
OUT_SHARD = D_MODEL // N_DEV
EPS = 1e-6
ROPE_THETA = 10000.0
Q_SCALE = 0.125
R_QA, R_KA, R_VA, R_GA, R_QB, R_KB, R_VB, R_GB, R_END = 0, 512, 640, 768, 1280, 1792, 2304, 2816, 3328
SMALL_W = 1536
ADAM_LR, ADAM_B1, ADAM_B2, ADAM_EPS, ADAM_WD, ADAM_STEP = 0.001, 0.9, 0.999, 1e-08, 0.01, 10
TM = 256
ACC_ROWS = 256
VMEM_LIMIT = 56 * 1024 * 1024

MESH = pl.DeviceIdType.MESH
NT = (((1,), (1,)), ((), ()))
TN = (((0,), (0,)), ((), ()))


def _params(sem, limit=VMEM_LIMIT):
    return pltpu.CompilerParams(dimension_semantics=sem, vmem_limit_bytes=limit)


def _dot(a, b, dims=None):
    if dims is None:
        return jnp.dot(a, b, preferred_element_type=F32)
    return lax.dot_general(a, b, dims, preferred_element_type=F32)


def _split(x):
    hi = x.astype(BF16)
    return hi, (x - hi.astype(F32)).astype(BF16)


def _lane(shape):
    return lax.broadcasted_iota(jnp.int32, shape, len(shape) - 1)


def _row(shape):
    return lax.broadcasted_iota(jnp.int32, shape, 0)


def _head_blockdiag():
    return ((_row((SLAB, SLAB)) >> HEAD_SHIFT) == (_lane((SLAB, SLAB)) >> HEAD_SHIFT)).astype(BF16)


def _head_sum(x, bd):
    hi, lo = _split(x)
    return _dot(hi, bd) + _dot(lo, bd)


def _swap_half(y, lane):
    return jnp.where((lane & 32) != 0, pltpu.roll(y, 32, 1), pltpu.roll(y, 96, 1))


def _stack_heads(q, lane):
    zero = jnp.zeros_like(q)
    return jnp.concatenate([jnp.where(lane < HEAD_DIM, q, zero), jnp.where(lane >= HEAD_DIM, q, zero)], axis=0)


def _unstack_heads(x2, lane):
    return jnp.where(lane < HEAD_DIM, x2[:BLK], x2[BLK:])


def _sigmoid(x):
    return 1.0 / (1.0 + jnp.exp(-x))


def _mesh_pos():
    return lax.axis_index("x"), lax.axis_index("y"), lax.axis_index("c")


def _flip(pos, mask):
    return tuple(1 - p if m else p for p, m in zip(pos, mask))


def _lin(pos):
    return 4 * pos[0] + 2 * pos[1] + pos[2]


CHIP_FLIPS = [(0, 0), (1, 0), (0, 1), (1, 1)]


def gather_weights(win_t_shard, wout_shard):
    shards = (win_t_shard, wout_shard)
    n_arr = len(shards)

    def body(a_ref, b_ref, oa_ref, ob_ref, send_sems, recv_sems, local_sems):
        x, y, c = _mesh_pos()
        me, sibling = (x, y, c), (x, y, 1 - c)
        chips = [(1 - x, y), (x, 1 - y), (1 - x, 1 - y)]
        ins, outs = (a_ref, b_ref), (oa_ref, ob_ref)

        def rows(a, pos):
            m = ins[a].shape[0]
            return outs[a].at[pl.ds(_lin(pos) * m, m), :]

        def copy(a, k, block, to, src=None):
            return pltpu.make_async_remote_copy(
                src_ref=rows(a, block) if src is None else src, dst_ref=rows(a, block),
                send_sem=send_sems.at[a, k], recv_sem=recv_sems.at[a, k], device_id=to, device_id_type=MESH)

        mine = [pltpu.make_async_copy(ins[a], rows(a, me), local_sems.at[a]) for a in range(n_arr)]
        for cp in mine:
            cp.start()
        first = []
        for a in range(n_arr):
            first.append(copy(a, 0, me, sibling, src=ins[a]))
            first += [copy(a, 1 + j, me, (*chip, c), src=ins[a]) for j, chip in enumerate(chips)]
        for cp in first:
            cp.start()
        passed = [[copy(a, 4 + j, (*chip, c), sibling) for j, chip in enumerate(chips)] for a in range(n_arr)]
        for j, chip in enumerate(chips):
            for a in range(n_arr):
                copy(a, 1 + j, (*chip, c), me).wait_recv()
                passed[a][j].start()
        for a in range(n_arr):
            copy(a, 0, sibling, me).wait_recv()
            for j, chip in enumerate(chips):
                copy(a, 4 + j, (*chip, 1 - c), me).wait_recv()
        for cp in first + [p for ps in passed for p in ps]:
            cp.wait_send()
        for cp in mine:
            cp.wait()

    vmem = pl.BlockSpec(memory_space=pltpu.VMEM)
    return pl.pallas_call(
        body, name="gather_weights",
        out_shape=[jax.ShapeDtypeStruct((N_DEV * s.shape[0], s.shape[1]), s.dtype) for s in shards],
        in_specs=[vmem] * n_arr, out_specs=[vmem] * n_arr,
        scratch_shapes=[pltpu.SemaphoreType.DMA((n_arr, 7)), pltpu.SemaphoreType.DMA((n_arr, 7)),
                        pltpu.SemaphoreType.DMA((n_arr,))],
        compiler_params=pltpu.CompilerParams(vmem_limit_bytes=VMEM_LIMIT),
    )(*shards)


def exchange_grads(dwin_t, dwout, small):
    srcs = (dwin_t, dwout)
    blocks = (IN_SHARD, OUT_SHARD)
    n_arr = len(srcs)
    dev_masks = [(mx, my, mc) for mx in (0, 1) for my in (0, 1) for mc in (0, 1)][1:]

    def body(a_hbm, b_hbm, s_ref, ra_ref, rb_ref, rs_ref, own_a, own_b, sib_a, sib_b, snd_a, snd_b,
             d2d_send, d2d_recv, ici_send, ici_recv, own_sems, s_send, s_recv):
        x, y, c = _mesh_pos()
        me, sibling = (x, y, c), (x, y, 1 - c)
        ins, outs = (a_hbm, b_hbm), (ra_ref, rb_ref)
        own, sib, snd = (own_a, own_b), (sib_a, sib_b), (snd_a, snd_b)
        chips = [_flip((x, y), f) for f in CHIP_FLIPS]

        def rows(a, pos):
            return ins[a].at[pl.ds(_lin(pos) * blocks[a], blocks[a]), :]

        def to_sibling(a, k):
            return pltpu.make_async_remote_copy(
                src_ref=rows(a, (*chips[k], 1 - c)), dst_ref=sib[a].at[k],
                send_sem=d2d_send.at[a, k], recv_sem=d2d_recv.at[a, k], device_id=sibling, device_id_type=MESH)

        def to_chip(a, k):
            return pltpu.make_async_remote_copy(
                src_ref=snd[a].at[k - 1], dst_ref=outs[a].at[k],
                send_sem=ici_send.at[a, k - 1], recv_sem=ici_recv.at[a, k - 1],
                device_id=(*chips[k], c), device_id_type=MESH)

        def small_to(k, to):
            return pltpu.make_async_remote_copy(
                src_ref=s_ref, dst_ref=rs_ref.at[_lin(me)], send_sem=s_send.at[k], recv_sem=s_recv.at[k],
                device_id=to, device_id_type=MESH)

        def small_from(k, frm):
            return pltpu.make_async_remote_copy(
                src_ref=s_ref, dst_ref=rs_ref.at[_lin(frm)], send_sem=s_send.at[k], recv_sem=s_recv.at[k],
                device_id=frm, device_id_type=MESH)

        order = (1, 2, 3, 0)
        swaps = [to_sibling(a, k) for k in order for a in range(n_arr)]
        for cp in swaps:
            cp.start()
        mine = {(a, k): pltpu.make_async_copy(rows(a, (*chips[k], c)), own[a].at[k], own_sems.at[a, k])
                for k in order for a in range(n_arr)}
        for cp in mine.values():
            cp.start()
        smalls = [small_to(k, _flip(me, mask)) for k, mask in enumerate(dev_masks)]
        for cp in smalls:
            cp.start()
        rs_ref[_lin(me)] = s_ref[...]

        sent = []
        for k in order:
            for a in range(n_arr):
                to_sibling(a, k).wait_recv()
                mine[(a, k)].wait()
                total = (own[a][k].astype(F32) + sib[a][k].astype(F32)).astype(BF16)
                if k == 0:
                    outs[a][0] = total
                else:
                    snd[a][k - 1] = total
                    sent.append(to_chip(a, k))
                    sent[-1].start()
        for k in (1, 2, 3):
            for a in range(n_arr):
                to_chip(a, k).wait_recv()
        for k, mask in enumerate(dev_masks):
            small_from(k, _flip(me, mask)).wait_recv()
        for cp in swaps + sent + smalls:
            cp.wait_send()

    def bufs(n):
        return [pltpu.VMEM((n, blocks[a], D_MODEL), BF16) for a in range(n_arr)]

    vmem, hbm = pl.BlockSpec(memory_space=pltpu.VMEM), pl.BlockSpec(memory_space=pl.ANY)
    return pl.pallas_call(
        body, name="exchange_grads",
        out_shape=[jax.ShapeDtypeStruct((4, blocks[a], D_MODEL), BF16) for a in range(n_arr)]
        + [jax.ShapeDtypeStruct((N_DEV,) + small.shape, small.dtype)],
        in_specs=[hbm, hbm, vmem], out_specs=[vmem] * 3,
        scratch_shapes=bufs(4) + bufs(4) + bufs(3) + [
            pltpu.SemaphoreType.DMA((n_arr, 4)), pltpu.SemaphoreType.DMA((n_arr, 4)),
            pltpu.SemaphoreType.DMA((n_arr, 3)), pltpu.SemaphoreType.DMA((n_arr, 3)),
            pltpu.SemaphoreType.DMA((n_arr, 4)), pltpu.SemaphoreType.DMA((7,)), pltpu.SemaphoreType.DMA((7,))],
        compiler_params=pltpu.CompilerParams(vmem_limit_bytes=VMEM_LIMIT),
    )(dwin_t, dwout, small)


def _norm_rope(xs, gain2, cos, sin_s, bd, lane):
    r = lax.rsqrt(_head_sum(xs * xs, bd) * (1.0 / HEAD_DIM) + EPS)
    y = xs * r * gain2
    return y * cos + _swap_half(y, lane) * sin_s


def _dup_heads(xs, lane):
    r = pltpu.roll(xs, HEAD_DIM, 1)
    lo = lane < HEAD_DIM
    return jnp.concatenate([jnp.where(lo, xs, r), jnp.where(lo, r, xs)], axis=1)


def fwd_proj(x, pos, norm_gain, win_t, inv_freq, sin_sign, q_gain2, k_gain2):
    def body(x_ref, pos_ref, ng_ref, w_ref, if_ref, sg_ref, qg_ref, kg_ref,
             qa_raw_ref, ka_raw_ref, q_rot_ref, k_dup_ref, v_dup_ref, ga_ref, qb_ref, kb_ref, vb_ref, gb_ref,
             cos_ref, sin_ref):
        xv = x_ref[...]
        rstd = lax.rsqrt(jnp.mean(xv * xv, axis=-1, keepdims=True) + EPS)
        h = (xv * rstd * ng_ref[...]).astype(BF16)

        def proj(r0, r1):
            return _dot(h, w_ref[r0:r1, :], NT)

        ang = pos_ref[...].astype(F32) * if_ref[...]
        cos = jnp.cos(ang)
        sin_s = jnp.sin(ang) * sg_ref[...]
        cos_ref[...] = cos
        sin_ref[...] = sin_s
        lane = _lane((TM, SLAB))
        bd = _head_blockdiag()

        qa = proj(R_QA, R_KA)
        qa_raw_ref[...] = qa
        for p in range(4):
            sl = slice(p * SLAB, (p + 1) * SLAB)
            q_rot_ref[:, sl] = (_norm_rope(qa[:, sl], qg_ref[...], cos, sin_s, bd, lane) * Q_SCALE).astype(BF16)
        ka = proj(R_KA, R_VA)
        ka_raw_ref[...] = ka
        k_dup_ref[...] = _dup_heads(_norm_rope(ka, kg_ref[...], cos, sin_s, bd, lane), lane).astype(BF16)
        v_dup_ref[...] = _dup_heads(proj(R_VA, R_GA), lane).astype(BF16)
        ga_ref[...] = proj(R_GA, R_QB)
        qb_ref[...] = (proj(R_QB, R_KB) * Q_SCALE).astype(BF16)
        kb_ref[...] = proj(R_KB, R_VB).astype(BF16)
        vb_ref[...] = proj(R_VB, R_GB).astype(BF16)
        gb_ref[...] = proj(R_GB, R_END)

    def tile(w):
        return pl.BlockSpec((TM, w), lambda i: (i, 0))

    def whole(a):
        return pl.BlockSpec(a.shape, lambda i: (0, 0))

    widths = [(512, F32), (128, F32), (512, BF16), (256, BF16), (256, BF16), (512, F32), (512, BF16), (512, BF16),
              (512, BF16), (512, F32), (128, F32), (128, F32)]
    return pl.pallas_call(
        body, name="fwd_proj", grid=(T_LOC // TM,),
        in_specs=[tile(D_MODEL), tile(1), whole(norm_gain), whole(win_t), whole(inv_freq), whole(sin_sign),
                  whole(q_gain2), whole(k_gain2)],
        out_specs=[tile(w) for w, _ in widths],
        out_shape=[jax.ShapeDtypeStruct((T_LOC, w), dt) for w, dt in widths],
        compiler_params=_params(("arbitrary",)),
    )(x, pos, norm_gain, win_t, inv_freq, sin_sign, q_gain2, k_gain2)


def _swa_window(prev_ref, cur_ref, p):
    gsl = _slab(p // 2)
    return jnp.concatenate([prev_ref[:, gsl], cur_ref[:, gsl]], axis=0)


def _swa_probs(s, sinks_ref, p, i):
    shape = (2 * BLK, 2 * BLK)
    r = _row(shape) & (BLK - 1)
    cidx = _lane(shape)
    valid = (cidx > r) & (cidx <= r + BLK) & ((cidx >= BLK) | (i > 0))
    s = jnp.where(valid, s, -jnp.inf)
    sink = jnp.where(_row((2 * BLK, 1)) < BLK, sinks_ref[0, 2 * p], sinks_ref[0, 2 * p + 1])
    m = jnp.maximum(jnp.max(s, axis=-1, keepdims=True), sink)
    e = jnp.exp(s - m)
    e_sink = jnp.exp(sink - m)
    den = jnp.sum(e, axis=-1, keepdims=True) + e_sink
    return e / den, e_sink / den


SWA_CHAINS = [(b, p) for b in range(B_LOC) for p in range(4)]


def _swa_specs():
    def cur(w):
        return pl.BlockSpec((B_LOC, BLK, w), lambda i: (0, i, 0))

    def prev(w):
        return pl.BlockSpec((B_LOC, BLK, w), lambda i: (0, jnp.maximum(i - 1, 0), 0))

    return cur, prev


def swa_fwd(q_rot, k_dup, v_dup, sinks):
    def body(q_ref, kp_ref, kc_ref, vp_ref, vc_ref, sinks_ref, o_ref):
        i = pl.program_id(0)
        lane = _lane((BLK, SLAB))
        s = [_dot(_stack_heads(q_ref[b, :, _slab(p)], lane), _swa_window(kp_ref.at[b], kc_ref.at[b], p), NT)
             for b, p in SWA_CHAINS]
        pn = [_swa_probs(s[c], sinks_ref, p, i)[0].astype(BF16) for c, (b, p) in enumerate(SWA_CHAINS)]
        for c, (b, p) in enumerate(SWA_CHAINS):
            o_ref[b, :, _slab(p)] = _unstack_heads(_dot(pn[c], _swa_window(vp_ref.at[b], vc_ref.at[b], p)), lane)

    cur, prev = _swa_specs()
    q3, k3, v3 = (a.reshape(B_LOC, SEQ, a.shape[1]) for a in (q_rot, k_dup, v_dup))
    return pl.pallas_call(
        body, name="swa_fwd", grid=(N_BLK,),
        in_specs=[cur(512), prev(256), cur(256), prev(256), cur(256), pl.BlockSpec(memory_space=pltpu.SMEM)],
        out_specs=cur(512),
        out_shape=jax.ShapeDtypeStruct((B_LOC, SEQ, 512), F32),
        compiler_params=_params(("arbitrary",)),
    )(q3, k3, k3, v3, v3, sinks).reshape(T_LOC, 512)


def _tri(suffix):
    r, cidx = _row((BLK + 16, BLK)), _lane((BLK + 16, BLK))
    tri = (cidx > r) if suffix else (cidx < r)
    return (tri | (r >= BLK)).astype(BF16)


def _key_sums(tri, x):
    res = _dot(tri, x.astype(BF16))
    return res[:BLK], res[BLK:BLK + 1]


def _sb_softplus(zt, valid):
    neg_abs = lax.bitcast_convert_type(lax.bitcast_convert_type(zt, jnp.uint32) | jnp.uint32(0x80000000), F32)
    sp = jnp.maximum(zt, 0.0) + jnp.log(1.0 + jnp.exp(neg_abs))
    return sp if valid is None else jnp.where(valid, sp, 0.0)


def _sb_weights(zt, sp, later, valid):
    w = jnp.exp(zt - sp - later)
    return w if valid is None else jnp.where(valid, w, 0.0)


def _slab(pp):
    return slice(pp * SLAB, (pp + 1) * SLAB)


def _blk(j):
    return pl.ds(pl.multiple_of(j * BLK, BLK), BLK)


def _causal_t():
    return _row((BLK, 2 * BLK)) < (_lane((BLK, 2 * BLK)) & (BLK - 1))


def _sb_rows(b, j):
    return pl.ds(pl.multiple_of(b * SEQ + j * BLK, BLK), BLK)


SB_CHAINS = [(b, pp) for b in range(B_LOC) for pp in range(4)]


def sb_fwd(qb, kb, vb):
    def body(q_ref, k_ref, v_ref, o_ref, c_ref, vt_ref, ot_ref):
        for c, (b, pp) in enumerate(SB_CHAINS):
            for j in range(N_BLK):
                vt_ref[c, j] = v_ref[b * SEQ + j * BLK:b * SEQ + (j + 1) * BLK, _slab(pp)].T
        lane = _lane((BLK, SLAB))
        tri = _tri(True)
        valid = _causal_t()
        jrow = _row((N_BLK, 2 * BLK))
        chains = range(len(SB_CHAINS))

        def q_block(i, _):
            q2 = [_stack_heads(q_ref[_sb_rows(b, i), _slab(pp)], lane) for b, pp in SB_CHAINS]

            def key_block(j, carry, mask, first):
                zt = [_dot(k_ref[_sb_rows(b, j), _slab(pp)], q2[c], NT) for c, (b, pp) in enumerate(SB_CHAINS)]
                sp = [_sb_softplus(zt[c], mask) for c in chains]
                sums = [_key_sums(tri, sp[c]) for c in chains]
                w = [_sb_weights(zt[c], sp[c], sums[c][0] + carry[c], mask) for c in chains]
                for c in chains:
                    pv = _dot(vt_ref[c, j], w[c].astype(BF16))
                    if first:
                        ot_ref[c] = pv
                    else:
                        ot_ref[c] += pv
                return tuple(carry[c] + sums[c][1] for c in chains)

            def earlier(jj, state):
                carry, saved = state
                j = i - 1 - jj
                saved = tuple(jnp.where(jrow == j, carry[c], saved[c]) for c in chains)
                return key_block(j, carry, None, False), saved

            zero = tuple(jnp.zeros((1, 2 * BLK), F32) for _ in chains)
            carry = key_block(i, zero, valid, True)
            _, saved = lax.fori_loop(0, i, earlier, (carry, tuple(jnp.zeros((N_BLK, 2 * BLK), F32) for _ in chains)))
            for c, (b, pp) in enumerate(SB_CHAINS):
                o_ref[_sb_rows(b, i), _slab(pp)] = _unstack_heads(ot_ref[c].T, lane)
                c_ref[c * N_BLK + i] = saved[c]
            return 0

        lax.fori_loop(0, N_BLK, q_block, 0)

    n_ch = len(SB_CHAINS)
    vmem = pl.BlockSpec(memory_space=pltpu.VMEM)
    return pl.pallas_call(
        body, name="sb_fwd",
        in_specs=[vmem] * 3, out_specs=[vmem] * 2,
        out_shape=[jax.ShapeDtypeStruct((T_LOC, 512), F32), jax.ShapeDtypeStruct((n_ch * N_BLK, N_BLK, 2 * BLK), F32)],
        scratch_shapes=[pltpu.VMEM((n_ch, N_BLK, SLAB, BLK), BF16), pltpu.VMEM((n_ch, SLAB, 2 * BLK), F32)],
        compiler_params=pltpu.CompilerParams(vmem_limit_bytes=VMEM_LIMIT),
    )(qb, kb, vb)


def out_loss(o_a, o_b, ga, gb, x, target, wout):
    n_tiles = T_LOC // TM

    def body(oa_ref, ob_ref, ga_ref, gb_ref, x_ref, t_ref, w_ref,
             dout_ref, doa_ref, dob_ref, dga_ref, dgb_ref, dw_ref, loss_ref, acc_ref):
        step = pl.program_id(0)

        @pl.when(step == 0)
        def _():
            acc_ref[...] = jnp.zeros_like(acc_ref)
            loss_ref[...] = jnp.zeros_like(loss_ref)

        oa, ob, gav, gbv = oa_ref[...], ob_ref[...], ga_ref[...], gb_ref[...]
        sa, sb = _sigmoid(gav), _sigmoid(gbv)
        silu_a, silu_b = gav * sa, gbv * sb
        y = jnp.concatenate([oa * silu_a, ob * silu_b], axis=1).astype(BF16)
        err = x_ref[...] + _dot(y, w_ref[...]) - t_ref[...]
        e2 = err * err
        part = jnp.sum(e2.reshape(TM // 8, 8, D_MODEL), axis=0)
        loss_ref[...] += functools.reduce(lambda a, b: a + b, [part[:, k * 128:(k + 1) * 128] for k in range(8)])
        dout = err * (1.0 / D_MODEL)
        dout_ref[...] = dout
        dob16 = dout.astype(BF16)
        for r0 in range(0, D_MODEL, ACC_ROWS):
            acc_ref[r0:r0 + ACC_ROWS, :] += _dot(y[:, r0:r0 + ACC_ROWS], dob16, TN)
        dy = _dot(dob16, w_ref[...], NT)
        dya, dyb = dy[:, :512], dy[:, 512:]
        doa_ref[...] = (dya * silu_a).astype(BF16)
        dob_ref[...] = (dyb * silu_b).astype(BF16)
        dga_ref[...] = (dya * oa * (sa * (1.0 + gav * (1.0 - sa)))).astype(BF16)
        dgb_ref[...] = (dyb * ob * (sb * (1.0 + gbv * (1.0 - sb)))).astype(BF16)

        @pl.when(step == n_tiles - 1)
        def _():
            dw_ref[...] = acc_ref[...].astype(BF16)

    def tile(w):
        return pl.BlockSpec((TM, w), lambda i: (i, 0))

    const = lambda i: (0, 0)
    return pl.pallas_call(
        body, name="out_loss", grid=(n_tiles,),
        in_specs=[tile(512)] * 4 + [tile(D_MODEL)] * 2 + [pl.BlockSpec((D_MODEL, D_MODEL), const)],
        out_specs=[tile(D_MODEL), tile(512), tile(512), tile(512), tile(512),
                   pl.BlockSpec((D_MODEL, D_MODEL), const), pl.BlockSpec((8, 128), const)],
        out_shape=[jax.ShapeDtypeStruct((T_LOC, D_MODEL), F32)] + [jax.ShapeDtypeStruct((T_LOC, 512), BF16)] * 4
        + [jax.ShapeDtypeStruct((D_MODEL, D_MODEL), BF16), jax.ShapeDtypeStruct((8, 128), F32)],
        scratch_shapes=[pltpu.VMEM((D_MODEL, D_MODEL), F32)],
        compiler_params=_params(("arbitrary",)),
    )(o_a, o_b, ga, gb, x, target, wout)


def swa_bwd(q_rot, k_dup, v_dup, o_a, d_oa, sinks):
    def body(q_ref, kp_ref, kc_ref, vp_ref, vc_ref, o_ref, do_ref, sinks_ref, dq_ref, dk_ref, dv_ref, dsink_ref):
        i = pl.program_id(0)

        @pl.when(i == 0)
        def _():
            dk_ref[...] = jnp.zeros_like(dk_ref)
            dv_ref[...] = jnp.zeros_like(dv_ref)
            dsink_ref[...] = jnp.zeros_like(dsink_ref)

        lane = _lane((BLK, SLAB))
        rows_prev, rows_cur = _blk(jnp.maximum(i - 1, 0)), _blk(i)
        chains = range(len(SWA_CHAINS))
        q2 = [_stack_heads(q_ref[b, :, _slab(p)], lane) for b, p in SWA_CHAINS]
        do2 = [_stack_heads(do_ref[b, :, _slab(p)], lane) for b, p in SWA_CHAINS]
        keys = [_swa_window(kp_ref.at[b], kc_ref.at[b], p) for b, p in SWA_CHAINS]
        s = [_dot(q2[c], keys[c], NT) for c in chains]
        dp = [_dot(do2[c], _swa_window(vp_ref.at[b], vc_ref.at[b], p), NT) for c, (b, p) in enumerate(SWA_CHAINS)]
        ds, pn16, cols = [], [], []
        for c, (b, p) in enumerate(SWA_CHAINS):
            pn, p_sink = _swa_probs(s[c], sinks_ref, p, i)
            o = o_ref[b, :, _slab(p)]
            delta = jnp.sum(do2[c].astype(F32) * jnp.concatenate([o, o], axis=0), axis=-1, keepdims=True)
            ds.append((pn * (dp[c] - delta)).astype(BF16))
            pn16.append(pn.astype(BF16))
            cols.append(-p_sink * delta)
        for c, (b, p) in enumerate(SWA_CHAINS):
            dq_ref[b, :, _slab(p)] = _unstack_heads(_dot(ds[c], keys[c]), lane) * Q_SCALE
        dk2 = [_dot(ds[c], q2[c], TN) for c in chains]
        dv2 = [_dot(pn16[c], do2[c], TN) for c in chains]
        for c, (b, p) in enumerate(SWA_CHAINS):
            gsl = _slab(p // 2)
            dk_ref[b, rows_prev, gsl] += dk2[c][:BLK]
            dk_ref[b, rows_cur, gsl] += dk2[c][BLK:]
            dv_ref[b, rows_prev, gsl] += dv2[c][:BLK]
            dv_ref[b, rows_cur, gsl] += dv2[c][BLK:]
            for e in range(2):
                dsink_ref[2 * p + e:2 * p + e + 1, :] += jnp.sum(cols[c][e * BLK:(e + 1) * BLK], axis=0, keepdims=True)

    cur, prev = _swa_specs()
    whole = pl.BlockSpec((B_LOC, SEQ, 256), lambda i: (0, 0, 0))
    q3, k3, v3, o3, do3 = (a.reshape(B_LOC, SEQ, a.shape[1]) for a in (q_rot, k_dup, v_dup, o_a, d_oa))
    dq, dk, dv, dsink = pl.pallas_call(
        body, name="swa_bwd", grid=(N_BLK,),
        in_specs=[cur(512), prev(256), cur(256), prev(256), cur(256), cur(512), cur(512),
                  pl.BlockSpec(memory_space=pltpu.SMEM)],
        out_specs=[cur(512), whole, whole, pl.BlockSpec((8, 128), lambda i: (0, 0))],
        out_shape=[jax.ShapeDtypeStruct((B_LOC, SEQ, 512), F32), jax.ShapeDtypeStruct((B_LOC, SEQ, 256), F32),
                   jax.ShapeDtypeStruct((B_LOC, SEQ, 256), F32), jax.ShapeDtypeStruct((8, 128), F32)],
        compiler_params=_params(("arbitrary",)),
    )(q3, k3, k3, v3, v3, o3, do3, sinks)
    return dq.reshape(T_LOC, 512), dk.reshape(T_LOC, 256), dv.reshape(T_LOC, 256), dsink


def sb_bwd(qb, kb, vb, d_ob, carries):
    def body(q_ref, k_ref, v_ref, do_ref, c_ref, dq_ref, dk_ref, dv_ref, kt_ref, dqt_ref):
        for c, (b, pp) in enumerate(SB_CHAINS):
            for j in range(N_BLK):
                kt_ref[c, j] = k_ref[b * SEQ + j * BLK:b * SEQ + (j + 1) * BLK, _slab(pp)].T
        dk_ref[...] = jnp.zeros_like(dk_ref)
        dv_ref[...] = jnp.zeros_like(dv_ref)
        dqt_ref[...] = jnp.zeros_like(dqt_ref)
        lane = _lane((BLK, SLAB))
        tri_after, tri_before = _tri(True), _tri(False)
        valid = _causal_t()
        jrow = _row((N_BLK, 2 * BLK))
        chains = range(len(SB_CHAINS))

        def q_block(i, _):
            q2 = [_stack_heads(q_ref[_sb_rows(b, i), _slab(pp)], lane) for b, pp in SB_CHAINS]
            do2 = [_stack_heads(do_ref[_sb_rows(b, i), _slab(pp)], lane) for b, pp in SB_CHAINS]

            def key_block(j, carry_sp, before_u, mask):
                at = [(_sb_rows(b, j), _slab(pp)) for b, pp in SB_CHAINS]
                zt = [_dot(k_ref[at[c]], q2[c], NT) for c in chains]
                dw = [_dot(v_ref[at[c]], do2[c], NT) for c in chains]
                sp = [_sb_softplus(zt[c], mask) for c in chains]
                later = [_key_sums(tri_after, sp[c])[0] for c in chains]
                w = [_sb_weights(zt[c], sp[c], later[c] + carry_sp[c], mask) for c in chains]
                u = [dw[c] * w[c] for c in chains]
                for c in chains:
                    dv_ref[at[c]] += _dot(w[c].astype(BF16), do2[c])
                sums = [_key_sums(tri_before, u[c]) for c in chains]
                dz16 = []
                for c in chains:
                    sig = jnp.exp(zt[c] - sp[c])
                    dz = u[c] - sig * (u[c] + before_u[c] + sums[c][0])
                    if mask is not None:
                        dz = jnp.where(mask, dz, 0.0)
                    dz16.append(dz.astype(BF16))
                for c in chains:
                    dk_ref[at[c]] += _dot(dz16[c], q2[c])
                    dqt_ref[c] += _dot(kt_ref[c, j], dz16[c])
                return tuple(before_u[c] + sums[c][1] for c in chains)

            def earlier(j, before_u):
                carry_sp = [jnp.sum(jnp.where(jrow == j, c_ref[c * N_BLK + i], 0.0), axis=0, keepdims=True)
                            for c in chains]
                return key_block(j, carry_sp, before_u, None)

            zero = tuple(jnp.zeros((1, 2 * BLK), F32) for _ in chains)
            before_u = lax.fori_loop(0, i, earlier, zero)
            key_block(i, zero, before_u, valid)
            for c, (b, pp) in enumerate(SB_CHAINS):
                dq_ref[_sb_rows(b, i), _slab(pp)] = (_unstack_heads(dqt_ref[c].T, lane) * Q_SCALE).astype(BF16)
                dqt_ref[c] = jnp.zeros((SLAB, 2 * BLK), F32)
            return 0

        lax.fori_loop(0, N_BLK, q_block, 0)

    n_ch = len(SB_CHAINS)
    vmem = pl.BlockSpec(memory_space=pltpu.VMEM)
    return pl.pallas_call(
        body, name="sb_bwd",
        in_specs=[vmem] * 5, out_specs=[vmem] * 3,
        out_shape=[jax.ShapeDtypeStruct((T_LOC, 512), BF16)] + [jax.ShapeDtypeStruct((T_LOC, 512), F32)] * 2,
        scratch_shapes=[pltpu.VMEM((n_ch, N_BLK, SLAB, BLK), BF16), pltpu.VMEM((n_ch, SLAB, 2 * BLK), F32)],
        compiler_params=pltpu.CompilerParams(vmem_limit_bytes=VMEM_LIMIT),
    )(qb, kb, vb, d_ob, carries)


def bwd_in(x, dout, norm_gain, win_t, dq_rot, dk_dup, dv_dup, qa_raw, ka_raw, cos, sin_s, q_gain2, k_gain2,
           dga, dgb, dqb, dkb, dvb):
    n_tiles = T_LOC // TM

    def body(x_ref, dout_ref, ng_ref, w_hbm, dq_ref, dk_ref, dv_ref, qa_ref, ka_ref, cos_ref, sin_ref, qg_ref, kg_ref,
             dga_ref, dgb_ref, dqb_ref, dkb_ref, dvb_ref,
             gx_ref, dw_hbm, dng_ref, dqg_ref, dkg_ref, w_ref, acc_ref, stage_ref, w_sem):
        step = pl.program_id(0)

        @pl.when(step == 0)
        def _():
            cp = pltpu.make_async_copy(w_hbm, w_ref, w_sem)
            cp.start()
            acc_ref[...] = jnp.zeros_like(acc_ref)
            dng_ref[...] = jnp.zeros_like(dng_ref)
            dqg_ref[...] = jnp.zeros_like(dqg_ref)
            dkg_ref[...] = jnp.zeros_like(dkg_ref)
            cp.wait()

        lane = _lane((TM, SLAB))
        bd = _head_blockdiag()
        cos, sin_s = cos_ref[...], sin_ref[...]

        def norm_rope_bwd(d_rot, raw, gain2):
            dy = d_rot * cos + _swap_half(d_rot * sin_s, lane)
            r = lax.rsqrt(_head_sum(raw * raw, bd) * (1.0 / HEAD_DIM) + EPS)
            xhat = raw * r
            dgain = jnp.sum(dy * xhat, axis=0, keepdims=True)
            dxh = dy * gain2
            mean = _head_sum(dxh * xhat, bd) * (1.0 / HEAD_DIM)
            return r * (dxh - xhat * mean), dgain

        def fold_dup(d_dup):
            a, b2 = d_dup[:, :SLAB], d_dup[:, SLAB:]
            return jnp.where(lane < HEAD_DIM, a + pltpu.roll(a, HEAD_DIM, 1), b2 + pltpu.roll(b2, HEAD_DIM, 1))

        pieces = []
        dqg = jnp.zeros((1, SLAB), F32)
        for p in range(4):
            sl = slice(p * SLAB, (p + 1) * SLAB)
            d_raw, dg = norm_rope_bwd(dq_ref[:, sl], qa_ref[:, sl], qg_ref[...])
            pieces.append(d_raw.astype(BF16))
            dqg = dqg + dg
        d_raw, dkg = norm_rope_bwd(fold_dup(dk_ref[...]), ka_ref[...], kg_ref[...])
        pieces.append(d_raw.astype(BF16))
        pieces.append(fold_dup(dv_ref[...]).astype(BF16))
        pieces += [dga_ref[...], dqb_ref[...], dkb_ref[...].astype(BF16), dvb_ref[...].astype(BF16),
                   dgb_ref[...]]
        dproj = jnp.concatenate(pieces, axis=1)
        dqg_ref[0:1, :] += dqg + pltpu.roll(dqg, HEAD_DIM, 1)
        dkg_ref[0:1, :] += dkg + pltpu.roll(dkg, HEAD_DIM, 1)

        xv = x_ref[...]
        rstd = lax.rsqrt(jnp.mean(xv * xv, axis=-1, keepdims=True) + EPS)
        xhat = xv * rstd
        gain = ng_ref[...]
        h = (xhat * gain).astype(BF16)
        for r0 in range(0, IN_WIDTH, ACC_ROWS):
            acc_ref[r0:r0 + ACC_ROWS, :] += _dot(dproj[:, r0:r0 + ACC_ROWS], h, TN)
        dh = _dot(dproj, w_ref[...])
        dng_ref[0:1, :] += jnp.sum(dh * xhat, axis=0, keepdims=True)
        dxh = dh * gain
        gx_ref[...] = dout_ref[...] + rstd * (dxh - xhat * jnp.mean(dxh * xhat, axis=-1, keepdims=True))

        @pl.when(step == n_tiles - 1)
        def _():
            for r0 in range(0, IN_WIDTH, ACC_ROWS):
                stage_ref[...] = acc_ref[r0:r0 + ACC_ROWS, :].astype(BF16)
                pltpu.sync_copy(stage_ref, dw_hbm.at[r0:r0 + ACC_ROWS, :])

    def tile(w):
        return pl.BlockSpec((TM, w), lambda i: (i, 0))

    def whole(a):
        return pl.BlockSpec(a.shape, lambda i: (0, 0))

    const = lambda i: (0, 0)
    return pl.pallas_call(
        body, name="bwd_in", grid=(n_tiles,),
        in_specs=[tile(D_MODEL), tile(D_MODEL), whole(norm_gain), pl.BlockSpec(memory_space=pl.ANY),
                  tile(512), tile(256), tile(256), tile(512), tile(128), tile(128), tile(128),
                  whole(q_gain2), whole(k_gain2), tile(512), tile(512), tile(512), tile(512), tile(512)],
        out_specs=[tile(D_MODEL), pl.BlockSpec(memory_space=pl.ANY), pl.BlockSpec((8, D_MODEL), const),
                   pl.BlockSpec((8, SLAB), const), pl.BlockSpec((8, SLAB), const)],
        out_shape=[jax.ShapeDtypeStruct((T_LOC, D_MODEL), F32), jax.ShapeDtypeStruct((IN_WIDTH, D_MODEL), BF16),
                   jax.ShapeDtypeStruct((8, D_MODEL), F32), jax.ShapeDtypeStruct((8, SLAB), F32),
                   jax.ShapeDtypeStruct((8, SLAB), F32)],
        scratch_shapes=[pltpu.VMEM((IN_WIDTH, D_MODEL), BF16), pltpu.VMEM((IN_WIDTH, D_MODEL), F32),
                        pltpu.VMEM((ACC_ROWS, D_MODEL), BF16), pltpu.SemaphoreType.DMA],
        compiler_params=_params(("arbitrary",)),
    )(x, dout, norm_gain, win_t, dq_rot, dk_dup, dv_dup, qa_raw, ka_raw, cos, sin_s, q_gain2, k_gain2,
      dga, dgb, dqb, dkb, dvb)


def _adamw(w, g, m, v):
    m = ADAM_B1 * m + (1.0 - ADAM_B1) * g
    v = ADAM_B2 * v + (1.0 - ADAM_B2) * (g * g)
    m_hat = m / (1.0 - ADAM_B1 ** ADAM_STEP)
    v_hat = v / (1.0 - ADAM_B2 ** ADAM_STEP)
    delta = -ADAM_LR * (m_hat / (jnp.sqrt(v_hat) + ADAM_EPS) + ADAM_WD * w)
    return delta, m, v


def _sum_slots(r_ref):
    g = r_ref[0].astype(F32)
    for s in range(1, r_ref.shape[0]):
        g = g + r_ref[s].astype(F32)
    return g


def adamw_rows(name, recv, w, m, v):
    def body(r_ref, w_ref, m_ref, v_ref, g_ref, d_ref, nm_ref, nv_ref):
        g = _sum_slots(r_ref)
        g_ref[...] = g
        d_ref[...], nm_ref[...], nv_ref[...] = _adamw(w_ref[...], g, m_ref[...], v_ref[...])

    return pl.pallas_call(
        body, name=name,
        out_shape=[jax.ShapeDtypeStruct(w.shape, F32)] * 4,
        compiler_params=pltpu.CompilerParams(vmem_limit_bytes=VMEM_LIMIT),
    )(recv, w, m, v)


def adamw_small(recv, weights, moments_m, moments_v):
    n = len(weights)

    def body(r_ref, *refs):
        ins, outs = refs[:3 * n], refs[3 * n:]
        s = _sum_slots(r_ref)
        eye = (_row((8, SLAB)) == _lane((8, SLAB))).astype(F32)
        sinks = jnp.sum(s[:, 1280:1408] * eye, axis=0, keepdims=True)
        grads = [s[0:1, :D_MODEL], s[0:1, 1024:1024 + HEAD_DIM], s[0:1, 1152:1152 + HEAD_DIM], sinks[:, :8]]
        for k in range(n):
            outs[k][...] = grads[k]
            outs[n + k][...], outs[2 * n + k][...], outs[3 * n + k][...] = _adamw(
                ins[k][...], grads[k], ins[n + k][...], ins[2 * n + k][...])
        loss = jnp.sum(jnp.sum(s[:, 1408:1536], axis=1, keepdims=True), axis=0, keepdims=True) * (0.5 / D_MODEL)
        outs[4 * n][...] = loss

    res = pl.pallas_call(
        body, name="adamw_small",
        out_shape=[jax.ShapeDtypeStruct(w.shape, F32) for w in weights] * 4 + [jax.ShapeDtypeStruct((1, 1), F32)],
        compiler_params=pltpu.CompilerParams(vmem_limit_bytes=VMEM_LIMIT),
    )(recv, *weights, *moments_m, *moments_v)
    return res[:n], res[n:2 * n], res[2 * n:3 * n], res[3 * n:4 * n], res[4 * n]


def kernel(x, positions, norm_gain, w_in, q_norm_gain, k_norm_gain, sinks, w_out, loss_target, m_norm_gain, m_w_in, m_q_norm_gain, m_k_norm_gain, m_sinks, m_w_out, v_norm_gain, v_w_in, v_q_norm_gain, v_k_norm_gain, v_sinks, v_w_out):
    x2 = x.reshape(T_LOC, D_MODEL)
    tgt2 = loss_target.reshape(T_LOC, D_MODEL)
    pos2 = positions.reshape(T_LOC, 1)
    half = HEAD_DIM // 2
    inv_freq = ROPE_THETA ** (-jnp.arange(half, dtype=F32) * 2.0 / HEAD_DIM)
    inv_freq = jnp.tile(inv_freq, SLAB // half).reshape(1, SLAB)
    sin_sign = jnp.tile(jnp.concatenate([-jnp.ones((half,), F32), jnp.ones((half,), F32)]), 2).reshape(1, SLAB)
    q_gain2 = jnp.tile(q_norm_gain, (1, 2))
    k_gain2 = jnp.tile(k_norm_gain, (1, 2))

    win_t, wout = gather_weights(w_in.reshape(D_MODEL, IN_SHARD).T.astype(BF16),
                                 w_out.reshape(OUT_SHARD, D_MODEL).astype(BF16))

    (qa_raw, ka_raw, q_rot, k_dup, v_dup, ga, qb, kb, vb, gb, cos, sin_s) = fwd_proj(
        x2, pos2, norm_gain, win_t, inv_freq, sin_sign, q_gain2, k_gain2)
    o_a = swa_fwd(q_rot, k_dup, v_dup, sinks)
    o_b, carries = sb_fwd(qb, kb, vb)
    dout, d_oa, d_ob, dga, dgb, dwout, loss_part = out_loss(o_a, o_b, ga, gb, x2, tgt2, wout)
    dq_rot, dk_dup, dv_dup, dsink = swa_bwd(q_rot, k_dup, v_dup, o_a, d_oa, sinks)
    dqb, dkb, dvb = sb_bwd(qb, kb, vb, d_ob, carries)
    grad_x, dwin_t, dng, dqg, dkg = bwd_in(
        x2, dout, norm_gain, win_t, dq_rot, dk_dup, dv_dup, qa_raw, ka_raw, cos, sin_s, q_gain2, k_gain2,
        dga, dgb, dqb, dkb, dvb)

    small = jnp.concatenate([dng, dqg, dkg, dsink, loss_part], axis=1)
    r_win, r_out, r_small = exchange_grads(dwin_t, dwout, small)

    w_in2, m_in2, v_in2 = (a.reshape(D_MODEL, IN_SHARD).T for a in (w_in, m_w_in, v_w_in))
    w_out2, m_out2, v_out2 = (a.reshape(OUT_SHARD, D_MODEL) for a in (w_out, m_w_out, v_w_out))
    big_in = adamw_rows("adamw_w_in", r_win, w_in2, m_in2, v_in2)
    big_out = adamw_rows("adamw_w_out", r_out, w_out2, m_out2, v_out2)
    *small_out, loss = adamw_small(
        r_small, (norm_gain, q_norm_gain, k_norm_gain, sinks),
        (m_norm_gain, m_q_norm_gain, m_k_norm_gain, m_sinks), (v_norm_gain, v_q_norm_gain, v_k_norm_gain, v_sinks))

    def leaves(k):
        ng, qg, kg, sk = small_out[k]
        return (ng, big_in[k].T.reshape(1, D_MODEL, IN_SHARD), qg, kg, sk, big_out[k].reshape(1, OUT_SHARD, D_MODEL))

    return (loss.reshape(()), grad_x.reshape(B_LOC, SEQ, D_MODEL), *leaves(0), *leaves(1), *leaves(2), *leaves(3))
```

```python
import functools

import jax
import jax.numpy as jnp
from jax import lax
from jax.experimental import pallas as pl
from jax.experimental.pallas import tpu as pltpu

F32 = jnp.float32
BF16 = jnp.bfloat16

N_DEV = 8
D_MODEL = 1024
SEQ = 2048
B_LOC = 2
T_LOC = B_LOC * SEQ
HEAD_DIM = 64
HEAD_SHIFT = 6
BLK = 128
N_BLK = SEQ // BLK
SLAB = 128
IN_WIDTH = 3328
IN_SHARD = IN_WIDTH // N_DEV
OUT_SHARD = D_MODEL // N_DEV
EPS = 1e-6
ROPE_THETA = 10000.0
Q_SCALE = 0.125
R_QA, R_KA, R_VA, R_GA, R_QB, R_KB, R_VB, R_GB, R_END = 0, 512, 640, 768, 1280, 1792, 2304, 2816, 3328
SMALL_W = 1536
ADAM_LR, ADAM_B1, ADAM_B2, ADAM_EPS, ADAM_WD, ADAM_STEP = 0.001, 0.9, 0.999, 1e-08, 0.01, 10
TM = 256
ACC_ROWS = 256
VMEM_LIMIT = 56 * 1024 * 1024

MESH = pl.DeviceIdType.MESH
NT = (((1,), (1,)), ((), ()))
TN = (((0,), (0,)), ((), ()))


def _params(sem, limit=VMEM_LIMIT):
    return pltpu.CompilerParams(dimension_semantics=sem, vmem_limit_bytes=limit)


def _dot(a, b, dims=None):
    if dims is None:
        return jnp.dot(a, b, preferred_element_type=F32)
    return lax.dot_general(a, b, dims, preferred_element_type=F32)


def _split(x):
    hi = x.astype(BF16)
    return hi, (x - hi.astype(F32)).astype(BF16)


def _lane(shape):
    return lax.broadcasted_iota(jnp.int32, shape, len(shape) - 1)


def _row(shape):
    return lax.broadcasted_iota(jnp.int32, shape, 0)


def _head_blockdiag():
    return ((_row((SLAB, SLAB)) >> HEAD_SHIFT) == (_lane((SLAB, SLAB)) >> HEAD_SHIFT)).astype(BF16)


def _head_sum(x, bd):
    hi, lo = _split(x)
    return _dot(hi, bd) + _dot(lo, bd)


def _swap_half(y, lane):
    return jnp.where((lane & 32) != 0, pltpu.roll(y, 32, 1), pltpu.roll(y, 96, 1))


def _stack_heads(q, lane):
    zero = jnp.zeros_like(q)
    return jnp.concatenate([jnp.where(lane < HEAD_DIM, q, zero), jnp.where(lane >= HEAD_DIM, q, zero)], axis=0)


def _unstack_heads(x2, lane):
    return jnp.where(lane < HEAD_DIM, x2[:BLK], x2[BLK:])


def _sigmoid(x):
    return 1.0 / (1.0 + jnp.exp(-x))


def _mesh_pos():
    return lax.axis_index("x"), lax.axis_index("y"), lax.axis_index("c")


def _flip(pos, mask):
    return tuple(1 - p if m else p for p, m in zip(pos, mask))


def _lin(pos):
    return 4 * pos[0] + 2 * pos[1] + pos[2]


CHIP_FLIPS = [(0, 0), (1, 0), (0, 1), (1, 1)]
DEV_FLIPS = [(fx, fy, fc) for fx in (0, 1) for fy in (0, 1) for fc in (0, 1)][1:]


def _direct_exchange(src_for, dst_slot, send_sems, recv_sems, local_sem):
    me = _mesh_pos()

    def copy(k, to):
        return pltpu.make_async_remote_copy(
            src_ref=src_for(to), dst_ref=dst_slot(me), send_sem=send_sems.at[k], recv_sem=recv_sems.at[k],
            device_id=to, device_id_type=MESH)

    def landed(k, frm):
        return pltpu.make_async_remote_copy(
            src_ref=src_for(frm), dst_ref=dst_slot(frm), send_sem=send_sems.at[k], recv_sem=recv_sems.at[k],
            device_id=frm, device_id_type=MESH)

    local = pltpu.make_async_copy(src_for(me), dst_slot(me), local_sem)
    peers = [_flip(me, f) for f in DEV_FLIPS]

    def start():
        local.start()
        for k, to in enumerate(peers):
            copy(k, to).start()

    def finish():
        for k, frm in enumerate(peers):
            landed(k, frm).wait_recv()
        for k, to in enumerate(peers):
            copy(k, to).wait_send()
        local.wait()

    return start, finish


def gather_weights(shard):
    m = shard.shape[0]

    def body(a_ref, o_ref, ici_send, ici_recv, d2d_send, d2d_recv, local_sem):
        x, y, c = _mesh_pos()
        me, sibling = (x, y, c), (x, y, 1 - c)
        chip_x, chip_y, chip_d = (1 - x, y), (x, 1 - y), (1 - x, 1 - y)

        def rows(pos):
            return o_ref.at[pl.ds(_lin(pos) * m, m), :]

        def ici(k, block, to, src=None):
            return pltpu.make_async_remote_copy(
                src_ref=rows(block) if src is None else src, dst_ref=rows(block),
                send_sem=ici_send.at[k], recv_sem=ici_recv.at[k], device_id=to, device_id_type=MESH)

        def d2d(k, chip, mine, src=None):
            block = (*chip, c) if mine else (*chip, 1 - c)
            return pltpu.make_async_remote_copy(
                src_ref=rows(block) if src is None else src, dst_ref=rows(block),
                send_sem=d2d_send.at[k], recv_sem=d2d_recv.at[k], device_id=sibling, device_id_type=MESH)

        local = pltpu.make_async_copy(a_ref, rows(me), local_sem)
        local.start()
        sends = [ici(0, me, (*chip_x, c), src=a_ref), ici(1, me, (*chip_y, c), src=a_ref),
                 d2d(0, (x, y), True, src=a_ref)]
        for cp in sends:
            cp.start()

        def pass_on(first, k_first, second, k_second, onward):
            ici(k_first, (*first, c), me).wait_recv()
            relay = ici(2, (*first, c), (*onward, c))
            relay.start()
            hand = [d2d(1 + k_first, first, True)]
            hand[0].start()
            ici(k_second, (*second, c), me).wait_recv()
            hand.append(d2d(1 + k_second, second, True))
            hand[1].start()
            ici(2, (*chip_d, c), me).wait_recv()
            hand.append(d2d(3, chip_d, True))
            hand[2].start()
            for cp in [relay] + hand:
                cp.wait_send()

        @pl.when(c == 0)
        def _():
            pass_on(chip_y, 1, chip_x, 0, chip_x)

        @pl.when(c == 1)
        def _():
            pass_on(chip_x, 0, chip_y, 1, chip_y)

        for k, chip in enumerate([(x, y), chip_x, chip_y, chip_d]):
            d2d(k, chip, False).wait_recv()
        for cp in sends:
            cp.wait_send()
        local.wait()

    vmem = pl.BlockSpec(memory_space=pltpu.VMEM)
    return pl.pallas_call(
        body, name="gather_weights",
        out_shape=jax.ShapeDtypeStruct((N_DEV * m, shard.shape[1]), shard.dtype),
        in_specs=[vmem], out_specs=vmem,
        scratch_shapes=[pltpu.SemaphoreType.DMA((3,)), pltpu.SemaphoreType.DMA((3,)),
                        pltpu.SemaphoreType.DMA((4,)), pltpu.SemaphoreType.DMA((4,)), pltpu.SemaphoreType.DMA],
        compiler_params=pltpu.CompilerParams(vmem_limit_bytes=VMEM_LIMIT),
    )(shard)


def exchange_grads(dwin_t, small):
    srcs = (dwin_t,)
    blocks = (IN_SHARD,)
    n_arr = len(srcs)

    def body(a_hbm, s_ref, ra_ref, rs_ref, own_a, sib_a, snd_a,
             d2d_send, d2d_recv, ici_send, ici_recv, own_sems, s_send, s_recv):
        x, y, c = _mesh_pos()
        me, sibling = (x, y, c), (x, y, 1 - c)
        ins, outs = (a_hbm,), (ra_ref,)
        own, sib, snd = (own_a,), (sib_a,), (snd_a,)
        chips = [_flip((x, y), f) for f in CHIP_FLIPS]

        def rows(a, pos):
            return ins[a].at[pl.ds(_lin(pos) * blocks[a], blocks[a]), :]

        def to_sibling(a, k):
            return pltpu.make_async_remote_copy(
                src_ref=rows(a, (*chips[k], 1 - c)), dst_ref=sib[a].at[k],
                send_sem=d2d_send.at[a, k], recv_sem=d2d_recv.at[a, k], device_id=sibling, device_id_type=MESH)

        def to_chip(a, k):
            return pltpu.make_async_remote_copy(
                src_ref=snd[a].at[k - 1], dst_ref=outs[a].at[k],
                send_sem=ici_send.at[a, k - 1], recv_sem=ici_recv.at[a, k - 1],
                device_id=(*chips[k], c), device_id_type=MESH)

        def small_to(k, to):
            return pltpu.make_async_remote_copy(
                src_ref=s_ref, dst_ref=rs_ref.at[_lin(me)], send_sem=s_send.at[k], recv_sem=s_recv.at[k],
                device_id=to, device_id_type=MESH)

        def small_from(k, frm):
            return pltpu.make_async_remote_copy(
                src_ref=s_ref, dst_ref=rs_ref.at[_lin(frm)], send_sem=s_send.at[k], recv_sem=s_recv.at[k],
                device_id=frm, device_id_type=MESH)

        order = (1, 2, 3, 0)
        swaps = [to_sibling(a, k) for k in order for a in range(n_arr)]
        for cp in swaps:
            cp.start()
        mine = {(a, k): pltpu.make_async_copy(rows(a, (*chips[k], c)), own[a].at[k], own_sems.at[a, k])
                for k in order for a in range(n_arr)}
        for cp in mine.values():
            cp.start()
        smalls = [small_to(k, _flip(me, mask)) for k, mask in enumerate(DEV_FLIPS)]
        for cp in smalls:
            cp.start()
        rs_ref[_lin(me)] = s_ref[...]

        sent = []
        for k in order:
            for a in range(n_arr):
                to_sibling(a, k).wait_recv()
                mine[(a, k)].wait()
                total = (own[a][k].astype(F32) + sib[a][k].astype(F32)).astype(BF16)
                if k == 0:
                    outs[a][0] = total
                else:
                    snd[a][k - 1] = total
                    sent.append(to_chip(a, k))
                    sent[-1].start()
        for k in (1, 2, 3):
            for a in range(n_arr):
                to_chip(a, k).wait_recv()
        for k, mask in enumerate(DEV_FLIPS):
            small_from(k, _flip(me, mask)).wait_recv()
        for cp in swaps + sent + smalls:
            cp.wait_send()

    def bufs(n):
        return [pltpu.VMEM((n, blocks[a], D_MODEL), BF16) for a in range(n_arr)]

    vmem, hbm = pl.BlockSpec(memory_space=pltpu.VMEM), pl.BlockSpec(memory_space=pl.ANY)
    return pl.pallas_call(
        body, name="exchange_grads",
        out_shape=[jax.ShapeDtypeStruct((4, blocks[a], D_MODEL), BF16) for a in range(n_arr)]
        + [jax.ShapeDtypeStruct((N_DEV,) + small.shape, small.dtype)],
        in_specs=[hbm, vmem], out_specs=[vmem] * 2,
        scratch_shapes=bufs(4) + bufs(4) + bufs(3) + [
            pltpu.SemaphoreType.DMA((n_arr, 4)), pltpu.SemaphoreType.DMA((n_arr, 4)),
            pltpu.SemaphoreType.DMA((n_arr, 3)), pltpu.SemaphoreType.DMA((n_arr, 3)),
            pltpu.SemaphoreType.DMA((n_arr, 4)), pltpu.SemaphoreType.DMA((7,)), pltpu.SemaphoreType.DMA((7,))],
        compiler_params=pltpu.CompilerParams(vmem_limit_bytes=VMEM_LIMIT),
    )(dwin_t, small)


def _norm_rope(xs, gain2, cos, sin_s, bd, lane):
    r = lax.rsqrt(_head_sum(xs * xs, bd) * (1.0 / HEAD_DIM) + EPS)
    y = xs * r * gain2
    return y * cos + _swap_half(y, lane) * sin_s


def _dup_heads(xs, lane):
    r = pltpu.roll(xs, HEAD_DIM, 1)
    lo = lane < HEAD_DIM
    return jnp.concatenate([jnp.where(lo, xs, r), jnp.where(lo, r, xs)], axis=1)


def fwd_proj(x, pos, norm_gain, win_t, inv_freq, sin_sign, q_gain2, k_gain2, wout_shard):
    n_tiles = T_LOC // TM

    def body(x_ref, pos_ref, ng_ref, w_ref, if_ref, sg_ref, qg_ref, kg_ref, ws_hbm,
             qa_raw_ref, ka_raw_ref, q_rot_ref, k_dup_ref, v_dup_ref, ga_ref, qb_ref, kb_ref, vb_ref, gb_ref,
             cos_ref, sin_ref, wo_hbm, wo_send, wo_recv, wo_local):
        start_wout, finish_wout = _direct_exchange(
            lambda dev: ws_hbm, lambda dev: wo_hbm.at[pl.ds(_lin(dev) * OUT_SHARD, OUT_SHARD), :],
            wo_send, wo_recv, wo_local)
        pl.when(pl.program_id(0) == 0)(start_wout)

        xv = x_ref[...]
        rstd = lax.rsqrt(jnp.mean(xv * xv, axis=-1, keepdims=True) + EPS)
        h = (xv * rstd * ng_ref[...]).astype(BF16)

        def proj(r0, r1):
            return _dot(h, w_ref[r0:r1, :], NT)

        ang = pos_ref[...].astype(F32) * if_ref[...]
        cos = jnp.cos(ang)
        sin_s = jnp.sin(ang) * sg_ref[...]
        cos_ref[...] = cos
        sin_ref[...] = sin_s
        lane = _lane((TM, SLAB))
        bd = _head_blockdiag()

        qa = proj(R_QA, R_KA)
        qa_raw_ref[...] = qa
        for p in range(4):
            sl = slice(p * SLAB, (p + 1) * SLAB)
            q_rot_ref[:, sl] = (_norm_rope(qa[:, sl], qg_ref[...], cos, sin_s, bd, lane) * Q_SCALE).astype(BF16)
        ka = proj(R_KA, R_VA)
        ka_raw_ref[...] = ka
        k_dup_ref[...] = _dup_heads(_norm_rope(ka, kg_ref[...], cos, sin_s, bd, lane), lane).astype(BF16)
        v_dup_ref[...] = _dup_heads(proj(R_VA, R_GA), lane).astype(BF16)
        ga_ref[...] = proj(R_GA, R_QB)
        qb_ref[...] = (proj(R_QB, R_KB) * Q_SCALE).astype(BF16)
        kb_ref[...] = proj(R_KB, R_VB).astype(BF16)
        vb_ref[...] = proj(R_VB, R_GB).astype(BF16)
        gb_ref[...] = proj(R_GB, R_END)
        pl.when(pl.program_id(0) == n_tiles - 1)(finish_wout)

    def tile(w):
        return pl.BlockSpec((TM, w), lambda i: (i, 0))

    def whole(a):
        return pl.BlockSpec(a.shape, lambda i: (0, 0))

    hbm = pl.BlockSpec(memory_space=pl.ANY)
    widths = [(512, F32), (128, F32), (512, BF16), (256, BF16), (256, BF16), (512, F32), (512, BF16), (512, BF16),
              (512, BF16), (512, F32), (128, F32), (128, F32)]
    return pl.pallas_call(
        body, name="fwd_proj", grid=(n_tiles,),
        in_specs=[tile(D_MODEL), tile(1), whole(norm_gain), whole(win_t), whole(inv_freq), whole(sin_sign),
                  whole(q_gain2), whole(k_gain2), hbm],
        out_specs=[tile(w) for w, _ in widths] + [hbm],
        out_shape=[jax.ShapeDtypeStruct((T_LOC, w), dt) for w, dt in widths]
        + [jax.ShapeDtypeStruct((D_MODEL, D_MODEL), BF16)],
        scratch_shapes=[pltpu.SemaphoreType.DMA((7,)), pltpu.SemaphoreType.DMA((7,)), pltpu.SemaphoreType.DMA],
        compiler_params=_params(("arbitrary",)),
    )(x, pos, norm_gain, win_t, inv_freq, sin_sign, q_gain2, k_gain2, wout_shard)


def _swa_window(prev_ref, cur_ref, p):
    gsl = _slab(p // 2)
    return jnp.concatenate([prev_ref[:, gsl], cur_ref[:, gsl]], axis=0)


def _swa_probs(s, sinks_ref, p, i):
    shape = (2 * BLK, 2 * BLK)
    r = _row(shape) & (BLK - 1)
    cidx = _lane(shape)
    valid = (cidx > r) & (cidx <= r + BLK) & ((cidx >= BLK) | (i > 0))
    s = jnp.where(valid, s, -jnp.inf)
    sink = jnp.where(_row((2 * BLK, 1)) < BLK, sinks_ref[0, 2 * p], sinks_ref[0, 2 * p + 1])
    m = jnp.maximum(jnp.max(s, axis=-1, keepdims=True), sink)
    e = jnp.exp(s - m)
    e_sink = jnp.exp(sink - m)
    den = jnp.sum(e, axis=-1, keepdims=True) + e_sink
    return e / den, e_sink / den


SWA_CHAINS = [(b, p) for b in range(B_LOC) for p in range(4)]


def _swa_specs():
    def cur(w):
        return pl.BlockSpec((B_LOC, BLK, w), lambda i: (0, i, 0))

    def prev(w):
        return pl.BlockSpec((B_LOC, BLK, w), lambda i: (0, jnp.maximum(i - 1, 0), 0))

    return cur, prev


def swa_fwd(q_rot, k_dup, v_dup, sinks):
    def body(q_ref, kp_ref, kc_ref, vp_ref, vc_ref, sinks_ref, o_ref):
        i = pl.program_id(0)
        lane = _lane((BLK, SLAB))
        s = [_dot(_stack_heads(q_ref[b, :, _slab(p)], lane), _swa_window(kp_ref.at[b], kc_ref.at[b], p), NT)
             for b, p in SWA_CHAINS]
        pn = [_swa_probs(s[c], sinks_ref, p, i)[0].astype(BF16) for c, (b, p) in enumerate(SWA_CHAINS)]
        for c, (b, p) in enumerate(SWA_CHAINS):
            o_ref[b, :, _slab(p)] = _unstack_heads(_dot(pn[c], _swa_window(vp_ref.at[b], vc_ref.at[b], p)), lane)

    cur, prev = _swa_specs()
    q3, k3, v3 = (a.reshape(B_LOC, SEQ, a.shape[1]) for a in (q_rot, k_dup, v_dup))
    return pl.pallas_call(
        body, name="swa_fwd", grid=(N_BLK,),
        in_specs=[cur(512), prev(256), cur(256), prev(256), cur(256), pl.BlockSpec(memory_space=pltpu.SMEM)],
        out_specs=cur(512),
        out_shape=jax.ShapeDtypeStruct((B_LOC, SEQ, 512), F32),
        compiler_params=_params(("arbitrary",)),
    )(q3, k3, k3, v3, v3, sinks).reshape(T_LOC, 512)


def _tri(suffix):
    r, cidx = _row((BLK + 16, BLK)), _lane((BLK + 16, BLK))
    tri = (cidx > r) if suffix else (cidx < r)
    return (tri | (r >= BLK)).astype(BF16)


def _key_sums(tri, x):
    res = _dot(tri, x.astype(BF16))
    return res[:BLK], res[BLK:BLK + 1]


def _sb_softplus(zt, valid):
    neg_abs = lax.bitcast_convert_type(lax.bitcast_convert_type(zt, jnp.uint32) | jnp.uint32(0x80000000), F32)
    sp = jnp.maximum(zt, 0.0) + jnp.log(1.0 + jnp.exp(neg_abs))
    return sp if valid is None else jnp.where(valid, sp, 0.0)


def _sb_weights(zt, sp, later, valid):
    w = jnp.exp(zt - sp - later)
    return w if valid is None else jnp.where(valid, w, 0.0)


def _slab(pp):
    return slice(pp * SLAB, (pp + 1) * SLAB)


def _blk(j):
    return pl.ds(pl.multiple_of(j * BLK, BLK), BLK)


def _causal_t():
    return _row((BLK, 2 * BLK)) < (_lane((BLK, 2 * BLK)) & (BLK - 1))


def _sb_rows(b, j):
    return pl.ds(pl.multiple_of(b * SEQ + j * BLK, BLK), BLK)


SB_CHAINS = [(b, pp) for b in range(B_LOC) for pp in range(4)]


def sb_fwd(qb, kb, vb):
    def body(q_ref, k_ref, v_ref, o_ref, c_ref, vt_ref, ot_ref):
        for c, (b, pp) in enumerate(SB_CHAINS):
            for j in range(N_BLK):
                vt_ref[c, j] = v_ref[b * SEQ + j * BLK:b * SEQ + (j + 1) * BLK, _slab(pp)].T
        lane = _lane((BLK, SLAB))
        tri = _tri(True)
        valid = _causal_t()
        jrow = _row((N_BLK, 2 * BLK))
        chains = range(len(SB_CHAINS))

        def q_block(i, _):
            q2 = [_stack_heads(q_ref[_sb_rows(b, i), _slab(pp)], lane) for b, pp in SB_CHAINS]

            def key_block(j, carry, mask, first):
                zt = [_dot(k_ref[_sb_rows(b, j), _slab(pp)], q2[c], NT) for c, (b, pp) in enumerate(SB_CHAINS)]
                sp = [_sb_softplus(zt[c], mask) for c in chains]
                sums = [_key_sums(tri, sp[c]) for c in chains]
                w = [_sb_weights(zt[c], sp[c], sums[c][0] + carry[c], mask) for c in chains]
                for c in chains:
                    pv = _dot(vt_ref[c, j], w[c].astype(BF16))
                    if first:
                        ot_ref[c] = pv
                    else:
                        ot_ref[c] += pv
                return tuple(carry[c] + sums[c][1] for c in chains)

            def earlier(jj, state):
                carry, saved = state
                j = i - 1 - jj
                saved = tuple(jnp.where(jrow == j, carry[c], saved[c]) for c in chains)
                return key_block(j, carry, None, False), saved

            zero = tuple(jnp.zeros((1, 2 * BLK), F32) for _ in chains)
            carry = key_block(i, zero, valid, True)
            _, saved = lax.fori_loop(0, i, earlier, (carry, tuple(jnp.zeros((N_BLK, 2 * BLK), F32) for _ in chains)))
            for c, (b, pp) in enumerate(SB_CHAINS):
                o_ref[_sb_rows(b, i), _slab(pp)] = _unstack_heads(ot_ref[c].T, lane)
                c_ref[c * N_BLK + i] = saved[c]
            return 0

        lax.fori_loop(0, N_BLK, q_block, 0)

    n_ch = len(SB_CHAINS)
    vmem = pl.BlockSpec(memory_space=pltpu.VMEM)
    return pl.pallas_call(
        body, name="sb_fwd",
        in_specs=[vmem] * 3, out_specs=[vmem] * 2,
        out_shape=[jax.ShapeDtypeStruct((T_LOC, 512), F32), jax.ShapeDtypeStruct((n_ch * N_BLK, N_BLK, 2 * BLK), F32)],
        scratch_shapes=[pltpu.VMEM((n_ch, N_BLK, SLAB, BLK), BF16), pltpu.VMEM((n_ch, SLAB, 2 * BLK), F32)],
        compiler_params=pltpu.CompilerParams(vmem_limit_bytes=VMEM_LIMIT),
    )(qb, kb, vb)


def out_loss(o_a, o_b, ga, gb, x, target, wout):
    n_tiles = T_LOC // TM

    def body(oa_ref, ob_ref, ga_ref, gb_ref, x_ref, t_ref, w_ref,
             dout_ref, doa_ref, dob_ref, dga_ref, dgb_ref, dw_ref, loss_ref, acc_ref):
        step = pl.program_id(0)

        @pl.when(step == 0)
        def _():
            acc_ref[...] = jnp.zeros_like(acc_ref)
            loss_ref[...] = jnp.zeros_like(loss_ref)

        oa, ob, gav, gbv = oa_ref[...], ob_ref[...], ga_ref[...], gb_ref[...]
        sa, sb = _sigmoid(gav), _sigmoid(gbv)
        silu_a, silu_b = gav * sa, gbv * sb
        y = jnp.concatenate([oa * silu_a, ob * silu_b], axis=1).astype(BF16)
        err = x_ref[...] + _dot(y, w_ref[...]) - t_ref[...]
        e2 = err * err
        part = jnp.sum(e2.reshape(TM // 8, 8, D_MODEL), axis=0)
        loss_ref[...] += functools.reduce(lambda a, b: a + b, [part[:, k * 128:(k + 1) * 128] for k in range(8)])
        dout = err * (1.0 / D_MODEL)
        dout_ref[...] = dout
        dob16 = dout.astype(BF16)
        for r0 in range(0, D_MODEL, ACC_ROWS):
            acc_ref[r0:r0 + ACC_ROWS, :] += _dot(y[:, r0:r0 + ACC_ROWS], dob16, TN)
        dy = _dot(dob16, w_ref[...], NT)
        dya, dyb = dy[:, :512], dy[:, 512:]
        doa_ref[...] = (dya * silu_a).astype(BF16)
        dob_ref[...] = (dyb * silu_b).astype(BF16)
        dga_ref[...] = (dya * oa * (sa * (1.0 + gav * (1.0 - sa)))).astype(BF16)
        dgb_ref[...] = (dyb * ob * (sb * (1.0 + gbv * (1.0 - sb)))).astype(BF16)

        @pl.when(step == n_tiles - 1)
        def _():
            dw_ref[...] = acc_ref[...].astype(BF16)

    def tile(w):
        return pl.BlockSpec((TM, w), lambda i: (i, 0))

    const = lambda i: (0, 0)
    return pl.pallas_call(
        body, name="out_loss", grid=(n_tiles,),
        in_specs=[tile(512)] * 4 + [tile(D_MODEL)] * 2 + [pl.BlockSpec((D_MODEL, D_MODEL), const)],
        out_specs=[tile(D_MODEL), tile(512), tile(512), tile(512), tile(512),
                   pl.BlockSpec((D_MODEL, D_MODEL), const), pl.BlockSpec((8, 128), const)],
        out_shape=[jax.ShapeDtypeStruct((T_LOC, D_MODEL), F32)] + [jax.ShapeDtypeStruct((T_LOC, 512), BF16)] * 4
        + [jax.ShapeDtypeStruct((D_MODEL, D_MODEL), BF16), jax.ShapeDtypeStruct((8, 128), F32)],
        scratch_shapes=[pltpu.VMEM((D_MODEL, D_MODEL), F32)],
        compiler_params=_params(("arbitrary",)),
    )(o_a, o_b, ga, gb, x, target, wout)


def swa_bwd(q_rot, k_dup, v_dup, o_a, d_oa, sinks):
    def body(q_ref, kp_ref, kc_ref, vp_ref, vc_ref, o_ref, do_ref, sinks_ref, dq_ref, dk_ref, dv_ref, dsink_ref):
        i = pl.program_id(0)

        @pl.when(i == 0)
        def _():
            dk_ref[...] = jnp.zeros_like(dk_ref)
            dv_ref[...] = jnp.zeros_like(dv_ref)
            dsink_ref[...] = jnp.zeros_like(dsink_ref)

        lane = _lane((BLK, SLAB))
        rows_prev, rows_cur = _blk(jnp.maximum(i - 1, 0)), _blk(i)
        chains = range(len(SWA_CHAINS))
        q2 = [_stack_heads(q_ref[b, :, _slab(p)], lane) for b, p in SWA_CHAINS]
        do2 = [_stack_heads(do_ref[b, :, _slab(p)], lane) for b, p in SWA_CHAINS]
        keys = [_swa_window(kp_ref.at[b], kc_ref.at[b], p) for b, p in SWA_CHAINS]
        s = [_dot(q2[c], keys[c], NT) for c in chains]
        dp = [_dot(do2[c], _swa_window(vp_ref.at[b], vc_ref.at[b], p), NT) for c, (b, p) in enumerate(SWA_CHAINS)]
        ds, pn16, cols = [], [], []
        for c, (b, p) in enumerate(SWA_CHAINS):
            pn, p_sink = _swa_probs(s[c], sinks_ref, p, i)
            o = o_ref[b, :, _slab(p)]
            delta = jnp.sum(do2[c].astype(F32) * jnp.concatenate([o, o], axis=0), axis=-1, keepdims=True)
            ds.append((pn * (dp[c] - delta)).astype(BF16))
            pn16.append(pn.astype(BF16))
            cols.append(-p_sink * delta)
        for c, (b, p) in enumerate(SWA_CHAINS):
            dq_ref[b, :, _slab(p)] = _unstack_heads(_dot(ds[c], keys[c]), lane) * Q_SCALE
        dk2 = [_dot(ds[c], q2[c], TN) for c in chains]
        dv2 = [_dot(pn16[c], do2[c], TN) for c in chains]
        for c, (b, p) in enumerate(SWA_CHAINS):
            gsl = _slab(p // 2)
            dk_ref[b, rows_prev, gsl] += dk2[c][:BLK]
            dk_ref[b, rows_cur, gsl] += dk2[c][BLK:]
            dv_ref[b, rows_prev, gsl] += dv2[c][:BLK]
            dv_ref[b, rows_cur, gsl] += dv2[c][BLK:]
            for e in range(2):
                dsink_ref[2 * p + e:2 * p + e + 1, :] += jnp.sum(cols[c][e * BLK:(e + 1) * BLK], axis=0, keepdims=True)

    cur, prev = _swa_specs()
    whole = pl.BlockSpec((B_LOC, SEQ, 256), lambda i: (0, 0, 0))
    q3, k3, v3, o3, do3 = (a.reshape(B_LOC, SEQ, a.shape[1]) for a in (q_rot, k_dup, v_dup, o_a, d_oa))
    dq, dk, dv, dsink = pl.pallas_call(
        body, name="swa_bwd", grid=(N_BLK,),
        in_specs=[cur(512), prev(256), cur(256), prev(256), cur(256), cur(512), cur(512),
                  pl.BlockSpec(memory_space=pltpu.SMEM)],
        out_specs=[cur(512), whole, whole, pl.BlockSpec((8, 128), lambda i: (0, 0))],
        out_shape=[jax.ShapeDtypeStruct((B_LOC, SEQ, 512), F32), jax.ShapeDtypeStruct((B_LOC, SEQ, 256), F32),
                   jax.ShapeDtypeStruct((B_LOC, SEQ, 256), F32), jax.ShapeDtypeStruct((8, 128), F32)],
        compiler_params=_params(("arbitrary",)),
    )(q3, k3, k3, v3, v3, o3, do3, sinks)
    return dq.reshape(T_LOC, 512), dk.reshape(T_LOC, 256), dv.reshape(T_LOC, 256), dsink


def sb_bwd(qb, kb, vb, d_ob, carries, dwout):
    def body(q_ref, k_ref, v_ref, do_ref, c_ref, dw_hbm, dq_ref, dk_ref, dv_ref, rw_hbm, kt_ref, dqt_ref,
             rw_send, rw_recv, rw_local):
        start_dwout, finish_dwout = _direct_exchange(
            lambda dev: dw_hbm.at[pl.ds(_lin(dev) * OUT_SHARD, OUT_SHARD), :], lambda dev: rw_hbm.at[_lin(dev)],
            rw_send, rw_recv, rw_local)
        start_dwout()
        for c, (b, pp) in enumerate(SB_CHAINS):
            for j in range(N_BLK):
                kt_ref[c, j] = k_ref[b * SEQ + j * BLK:b * SEQ + (j + 1) * BLK, _slab(pp)].T
        dk_ref[...] = jnp.zeros_like(dk_ref)
        dv_ref[...] = jnp.zeros_like(dv_ref)
        dqt_ref[...] = jnp.zeros_like(dqt_ref)
        lane = _lane((BLK, SLAB))
        tri_after, tri_before = _tri(True), _tri(False)
        valid = _causal_t()
        jrow = _row((N_BLK, 2 * BLK))
        chains = range(len(SB_CHAINS))

        def q_block(i, _):
            q2 = [_stack_heads(q_ref[_sb_rows(b, i), _slab(pp)], lane) for b, pp in SB_CHAINS]
            do2 = [_stack_heads(do_ref[_sb_rows(b, i), _slab(pp)], lane) for b, pp in SB_CHAINS]

            def key_block(j, carry_sp, before_u, mask):
                at = [(_sb_rows(b, j), _slab(pp)) for b, pp in SB_CHAINS]
                zt = [_dot(k_ref[at[c]], q2[c], NT) for c in chains]
                dw = [_dot(v_ref[at[c]], do2[c], NT) for c in chains]
                sp = [_sb_softplus(zt[c], mask) for c in chains]
                later = [_key_sums(tri_after, sp[c])[0] for c in chains]
                w = [_sb_weights(zt[c], sp[c], later[c] + carry_sp[c], mask) for c in chains]
                u = [dw[c] * w[c] for c in chains]
                for c in chains:
                    dv_ref[at[c]] += _dot(w[c].astype(BF16), do2[c])
                sums = [_key_sums(tri_before, u[c]) for c in chains]
                dz16 = []
                for c in chains:
                    sig = jnp.exp(zt[c] - sp[c])
                    dz = u[c] - sig * (u[c] + before_u[c] + sums[c][0])
                    if mask is not None:
                        dz = jnp.where(mask, dz, 0.0)
                    dz16.append(dz.astype(BF16))
                for c in chains:
                    dk_ref[at[c]] += _dot(dz16[c], q2[c])
                    dqt_ref[c] += _dot(kt_ref[c, j], dz16[c])
                return tuple(before_u[c] + sums[c][1] for c in chains)

            def earlier(j, before_u):
                carry_sp = [jnp.sum(jnp.where(jrow == j, c_ref[c * N_BLK + i], 0.0), axis=0, keepdims=True)
                            for c in chains]
                return key_block(j, carry_sp, before_u, None)

            zero = tuple(jnp.zeros((1, 2 * BLK), F32) for _ in chains)
            before_u = lax.fori_loop(0, i, earlier, zero)
            key_block(i, zero, before_u, valid)
            for c, (b, pp) in enumerate(SB_CHAINS):
                dq_ref[_sb_rows(b, i), _slab(pp)] = (_unstack_heads(dqt_ref[c].T, lane) * Q_SCALE).astype(BF16)
                dqt_ref[c] = jnp.zeros((SLAB, 2 * BLK), F32)
            return 0

        lax.fori_loop(0, N_BLK, q_block, 0)
        finish_dwout()

    n_ch = len(SB_CHAINS)
    vmem, hbm = pl.BlockSpec(memory_space=pltpu.VMEM), pl.BlockSpec(memory_space=pl.ANY)
    return pl.pallas_call(
        body, name="sb_bwd",
        in_specs=[vmem] * 5 + [hbm], out_specs=[vmem] * 3 + [hbm],
        out_shape=[jax.ShapeDtypeStruct((T_LOC, 512), BF16)] + [jax.ShapeDtypeStruct((T_LOC, 512), F32)] * 2
        + [jax.ShapeDtypeStruct((N_DEV, OUT_SHARD, D_MODEL), BF16)],
        scratch_shapes=[pltpu.VMEM((n_ch, N_BLK, SLAB, BLK), BF16), pltpu.VMEM((n_ch, SLAB, 2 * BLK), F32),
                        pltpu.SemaphoreType.DMA((7,)), pltpu.SemaphoreType.DMA((7,)), pltpu.SemaphoreType.DMA],
        compiler_params=pltpu.CompilerParams(vmem_limit_bytes=VMEM_LIMIT),
    )(qb, kb, vb, d_ob, carries, dwout)


def bwd_in(x, dout, norm_gain, win_t, dq_rot, dk_dup, dv_dup, qa_raw, ka_raw, cos, sin_s, q_gain2, k_gain2,
           dga, dgb, dqb, dkb, dvb):
    n_tiles = T_LOC // TM

    def body(x_ref, dout_ref, ng_ref, w_hbm, dq_ref, dk_ref, dv_ref, qa_ref, ka_ref, cos_ref, sin_ref, qg_ref, kg_ref,
             dga_ref, dgb_ref, dqb_ref, dkb_ref, dvb_ref,
             gx_ref, dw_hbm, dng_ref, dqg_ref, dkg_ref, w_ref, acc_ref, stage_ref, w_sem):
        step = pl.program_id(0)

        @pl.when(step == 0)
        def _():
            cp = pltpu.make_async_copy(w_hbm, w_ref, w_sem)
            cp.start()
            acc_ref[...] = jnp.zeros_like(acc_ref)
            dng_ref[...] = jnp.zeros_like(dng_ref)
            dqg_ref[...] = jnp.zeros_like(dqg_ref)
            dkg_ref[...] = jnp.zeros_like(dkg_ref)
            cp.wait()

        lane = _lane((TM, SLAB))
        bd = _head_blockdiag()
        cos, sin_s = cos_ref[...], sin_ref[...]

        def norm_rope_bwd(d_rot, raw, gain2):
            dy = d_rot * cos + _swap_half(d_rot * sin_s, lane)
            r = lax.rsqrt(_head_sum(raw * raw, bd) * (1.0 / HEAD_DIM) + EPS)
            xhat = raw * r
            dgain = jnp.sum(dy * xhat, axis=0, keepdims=True)
            dxh = dy * gain2
            mean = _head_sum(dxh * xhat, bd) * (1.0 / HEAD_DIM)
            return r * (dxh - xhat * mean), dgain

        def fold_dup(d_dup):
            a, b2 = d_dup[:, :SLAB], d_dup[:, SLAB:]
            return jnp.where(lane < HEAD_DIM, a + pltpu.roll(a, HEAD_DIM, 1), b2 + pltpu.roll(b2, HEAD_DIM, 1))

        pieces = []
        dqg = jnp.zeros((1, SLAB), F32)
        for p in range(4):
            sl = slice(p * SLAB, (p + 1) * SLAB)
            d_raw, dg = norm_rope_bwd(dq_ref[:, sl], qa_ref[:, sl], qg_ref[...])
            pieces.append(d_raw.astype(BF16))
            dqg = dqg + dg
        d_raw, dkg = norm_rope_bwd(fold_dup(dk_ref[...]), ka_ref[...], kg_ref[...])
        pieces.append(d_raw.astype(BF16))
        pieces.append(fold_dup(dv_ref[...]).astype(BF16))
        pieces += [dga_ref[...], dqb_ref[...], dkb_ref[...].astype(BF16), dvb_ref[...].astype(BF16),
                   dgb_ref[...]]
        dproj = jnp.concatenate(pieces, axis=1)
        dqg_ref[0:1, :] += dqg + pltpu.roll(dqg, HEAD_DIM, 1)
        dkg_ref[0:1, :] += dkg + pltpu.roll(dkg, HEAD_DIM, 1)

        xv = x_ref[...]
        rstd = lax.rsqrt(jnp.mean(xv * xv, axis=-1, keepdims=True) + EPS)
        xhat = xv * rstd
        gain = ng_ref[...]
        h = (xhat * gain).astype(BF16)
        for r0 in range(0, IN_WIDTH, ACC_ROWS):
            acc_ref[r0:r0 + ACC_ROWS, :] += _dot(dproj[:, r0:r0 + ACC_ROWS], h, TN)
        dh = _dot(dproj, w_ref[...])
        dng_ref[0:1, :] += jnp.sum(dh * xhat, axis=0, keepdims=True)
        dxh = dh * gain
        gx_ref[...] = dout_ref[...] + rstd * (dxh - xhat * jnp.mean(dxh * xhat, axis=-1, keepdims=True))

        @pl.when(step == n_tiles - 1)
        def _():
            for r0 in range(0, IN_WIDTH, ACC_ROWS):
                stage_ref[...] = acc_ref[r0:r0 + ACC_ROWS, :].astype(BF16)
                pltpu.sync_copy(stage_ref, dw_hbm.at[r0:r0 + ACC_ROWS, :])

    def tile(w):
        return pl.BlockSpec((TM, w), lambda i: (i, 0))

    def whole(a):
        return pl.BlockSpec(a.shape, lambda i: (0, 0))

    const = lambda i: (0, 0)
    return pl.pallas_call(
        body, name="bwd_in", grid=(n_tiles,),
        in_specs=[tile(D_MODEL), tile(D_MODEL), whole(norm_gain), pl.BlockSpec(memory_space=pl.ANY),
                  tile(512), tile(256), tile(256), tile(512), tile(128), tile(128), tile(128),
                  whole(q_gain2), whole(k_gain2), tile(512), tile(512), tile(512), tile(512), tile(512)],
        out_specs=[tile(D_MODEL), pl.BlockSpec(memory_space=pl.ANY), pl.BlockSpec((8, D_MODEL), const),
                   pl.BlockSpec((8, SLAB), const), pl.BlockSpec((8, SLAB), const)],
        out_shape=[jax.ShapeDtypeStruct((T_LOC, D_MODEL), F32), jax.ShapeDtypeStruct((IN_WIDTH, D_MODEL), BF16),
                   jax.ShapeDtypeStruct((8, D_MODEL), F32), jax.ShapeDtypeStruct((8, SLAB), F32),
                   jax.ShapeDtypeStruct((8, SLAB), F32)],
        scratch_shapes=[pltpu.VMEM((IN_WIDTH, D_MODEL), BF16), pltpu.VMEM((IN_WIDTH, D_MODEL), F32),
                        pltpu.VMEM((ACC_ROWS, D_MODEL), BF16), pltpu.SemaphoreType.DMA],
        compiler_params=_params(("arbitrary",)),
    )(x, dout, norm_gain, win_t, dq_rot, dk_dup, dv_dup, qa_raw, ka_raw, cos, sin_s, q_gain2, k_gain2,
      dga, dgb, dqb, dkb, dvb)


def _adamw(w, g, m, v):
    m = ADAM_B1 * m + (1.0 - ADAM_B1) * g
    v = ADAM_B2 * v + (1.0 - ADAM_B2) * (g * g)
    m_hat = m / (1.0 - ADAM_B1 ** ADAM_STEP)
    v_hat = v / (1.0 - ADAM_B2 ** ADAM_STEP)
    delta = -ADAM_LR * (m_hat / (jnp.sqrt(v_hat) + ADAM_EPS) + ADAM_WD * w)
    return delta, m, v


def _sum_slots(r_ref):
    g = r_ref[0].astype(F32)
    for s in range(1, r_ref.shape[0]):
        g = g + r_ref[s].astype(F32)
    return g


def adamw_rows(name, recv, w, m, v):
    def body(r_ref, w_ref, m_ref, v_ref, g_ref, d_ref, nm_ref, nv_ref):
        g = _sum_slots(r_ref)
        g_ref[...] = g
        d_ref[...], nm_ref[...], nv_ref[...] = _adamw(w_ref[...], g, m_ref[...], v_ref[...])

    return pl.pallas_call(
        body, name=name,
        out_shape=[jax.ShapeDtypeStruct(w.shape, F32)] * 4,
        compiler_params=pltpu.CompilerParams(vmem_limit_bytes=VMEM_LIMIT),
    )(recv, w, m, v)


def adamw_small(recv, weights, moments_m, moments_v):
    n = len(weights)

    def body(r_ref, *refs):
        ins, outs = refs[:3 * n], refs[3 * n:]
        s = _sum_slots(r_ref)
        eye = (_row((8, SLAB)) == _lane((8, SLAB))).astype(F32)
        sinks = jnp.sum(s[:, 1280:1408] * eye, axis=0, keepdims=True)
        grads = [s[0:1, :D_MODEL], s[0:1, 1024:1024 + HEAD_DIM], s[0:1, 1152:1152 + HEAD_DIM], sinks[:, :8]]
        for k in range(n):
            outs[k][...] = grads[k]
            outs[n + k][...], outs[2 * n + k][...], outs[3 * n + k][...] = _adamw(
                ins[k][...], grads[k], ins[n + k][...], ins[2 * n + k][...])
        loss = jnp.sum(jnp.sum(s[:, 1408:1536], axis=1, keepdims=True), axis=0, keepdims=True) * (0.5 / D_MODEL)
        outs[4 * n][...] = loss

    res = pl.pallas_call(
        body, name="adamw_small",
        out_shape=[jax.ShapeDtypeStruct(w.shape, F32) for w in weights] * 4 + [jax.ShapeDtypeStruct((1, 1), F32)],
        compiler_params=pltpu.CompilerParams(vmem_limit_bytes=VMEM_LIMIT),
    )(recv, *weights, *moments_m, *moments_v)
    return res[:n], res[n:2 * n], res[2 * n:3 * n], res[3 * n:4 * n], res[4 * n]


def kernel(x, positions, norm_gain, w_in, q_norm_gain, k_norm_gain, sinks, w_out, loss_target, m_norm_gain, m_w_in, m_q_norm_gain, m_k_norm_gain, m_sinks, m_w_out, v_norm_gain, v_w_in, v_q_norm_gain, v_k_norm_gain, v_sinks, v_w_out):
    x2 = x.reshape(T_LOC, D_MODEL)
    tgt2 = loss_target.reshape(T_LOC, D_MODEL)
    pos2 = positions.reshape(T_LOC, 1)
    half = HEAD_DIM // 2
    inv_freq = ROPE_THETA ** (-jnp.arange(half, dtype=F32) * 2.0 / HEAD_DIM)
    inv_freq = jnp.tile(inv_freq, SLAB // half).reshape(1, SLAB)
    sin_sign = jnp.tile(jnp.concatenate([-jnp.ones((half,), F32), jnp.ones((half,), F32)]), 2).reshape(1, SLAB)
    q_gain2 = jnp.tile(q_norm_gain, (1, 2))
    k_gain2 = jnp.tile(k_norm_gain, (1, 2))

    win_t = gather_weights(w_in.reshape(D_MODEL, IN_SHARD).T.astype(BF16))

    (qa_raw, ka_raw, q_rot, k_dup, v_dup, ga, qb, kb, vb, gb, cos, sin_s, wout) = fwd_proj(
        x2, pos2, norm_gain, win_t, inv_freq, sin_sign, q_gain2, k_gain2, w_out.reshape(OUT_SHARD, D_MODEL).astype(BF16))
    o_a = swa_fwd(q_rot, k_dup, v_dup, sinks)
    o_b, carries = sb_fwd(qb, kb, vb)
    dout, d_oa, d_ob, dga, dgb, dwout, loss_part = out_loss(o_a, o_b, ga, gb, x2, tgt2, wout)
    dq_rot, dk_dup, dv_dup, dsink = swa_bwd(q_rot, k_dup, v_dup, o_a, d_oa, sinks)
    dqb, dkb, dvb, r_out = sb_bwd(qb, kb, vb, d_ob, carries, dwout)
    grad_x, dwin_t, dng, dqg, dkg = bwd_in(
        x2, dout, norm_gain, win_t, dq_rot, dk_dup, dv_dup, qa_raw, ka_raw, cos, sin_s, q_gain2, k_gain2,
        dga, dgb, dqb, dkb, dvb)

    small = jnp.concatenate([dng, dqg, dkg, dsink, loss_part], axis=1)
    r_win, r_small = exchange_grads(dwin_t, small)

    w_in2, m_in2, v_in2 = (a.reshape(D_MODEL, IN_SHARD).T for a in (w_in, m_w_in, v_w_in))
    w_out2, m_out2, v_out2 = (a.reshape(OUT_SHARD, D_MODEL) for a in (w_out, m_w_out, v_w_out))
    big_in = adamw_rows("adamw_w_in", r_win, w_in2, m_in2, v_in2)
    big_out = adamw_rows("adamw_w_out", r_out, w_out2, m_out2, v_out2)
    *small_out, loss = adamw_small(
        r_small, (norm_gain, q_norm_gain, k_norm_gain, sinks),
        (m_norm_gain, m_q_norm_gain, m_k_norm_gain, m_sinks), (v_norm_gain, v_q_norm_gain, v_k_norm_gain, v_sinks))

    def leaves(k):
        ng, qg, kg, sk = small_out[k]
        return (ng, big_in[k].T.reshape(1, D_MODEL, IN_SHARD), qg, kg, sk, big_out[k].reshape(1, OUT_SHARD, D_MODEL))

    return (loss.reshape(()), grad_x.reshape(B_LOC, SEQ, D_MODEL), *leaves(0), *leaves(1), *leaves(2), *leaves(3))
```

```python
import functools

import jax
import jax.numpy as jnp
from jax import lax
from jax.experimental import pallas as pl
from jax.experimental.pallas import tpu as pltpu

F32 = jnp.float32
BF16 = jnp.bfloat16

N_DEV = 8
D_MODEL = 1024
SEQ = 2048
B_LOC = 2
T_LOC = B_LOC * SEQ
HEAD_DIM = 64
HEAD_SHIFT = 6
BLK = 128
N_BLK = SEQ // BLK
SLAB = 128
IN_WIDTH = 3328
IN_SHARD = IN_WIDTH // N_DEV
OUT_SHARD = D_MODEL // N_DEV
EPS = 1e-6
ROPE_THETA = 10000.0
Q_SCALE = 0.125
R_QA, R_KA, R_VA, R_GA, R_QB, R_KB, R_VB, R_GB, R_END = 0, 512, 640, 768, 1280, 1792, 2304, 2816, 3328
SMALL_W = 1536
ADAM_LR, ADAM_B1, ADAM_B2, ADAM_EPS, ADAM_WD, ADAM_STEP = 0.001, 0.9, 0.999, 1e-08, 0.01, 10
TM = 256
ACC_ROWS = 256
VMEM_LIMIT = 56 * 1024 * 1024

MESH = pl.DeviceIdType.MESH
NT = (((1,), (1,)), ((), ()))
TN = (((0,), (0,)), ((), ()))


def _params(sem, limit=VMEM_LIMIT):
    return pltpu.CompilerParams(dimension_semantics=sem, vmem_limit_bytes=limit)


def _dot(a, b, dims=None):
    if dims is None:
        return jnp.dot(a, b, preferred_element_type=F32)
    return lax.dot_general(a, b, dims, preferred_element_type=F32)


def _split(x):
    hi = x.astype(BF16)
    return hi, (x - hi.astype(F32)).astype(BF16)


def _lane(shape):
    return lax.broadcasted_iota(jnp.int32, shape, len(shape) - 1)


def _row(shape):
    return lax.broadcasted_iota(jnp.int32, shape, 0)


def _head_blockdiag():
    return ((_row((SLAB, SLAB)) >> HEAD_SHIFT) == (_lane((SLAB, SLAB)) >> HEAD_SHIFT)).astype(BF16)


def _head_sum(x, bd):
    hi, lo = _split(x)
    return _dot(hi, bd) + _dot(lo, bd)


def _swap_half(y, lane):
    return jnp.where((lane & 32) != 0, pltpu.roll(y, 32, 1), pltpu.roll(y, 96, 1))


def _stack_heads(q, lane):
    zero = jnp.zeros_like(q)
    return jnp.concatenate([jnp.where(lane < HEAD_DIM, q, zero), jnp.where(lane >= HEAD_DIM, q, zero)], axis=0)


def _unstack_heads(x2, lane):
    return jnp.where(lane < HEAD_DIM, x2[:BLK], x2[BLK:])


def _sigmoid(x):
    return 1.0 / (1.0 + jnp.exp(-x))


def _mesh_pos():
    return lax.axis_index("x"), lax.axis_index("y"), lax.axis_index("c")


def _flip(pos, mask):
    return tuple(1 - p if m else p for p, m in zip(pos, mask))


def _lin(pos):
    return 4 * pos[0] + 2 * pos[1] + pos[2]


CHIP_FLIPS = [(0, 0), (1, 0), (0, 1), (1, 1)]
DEV_FLIPS = [(fx, fy, fc) for fx in (0, 1) for fy in (0, 1) for fc in (0, 1)][1:]


def _direct_exchange(src_for, dst_slot, send_sems, recv_sems, local_sem):
    me = _mesh_pos()

    def copy(k, to):
        return pltpu.make_async_remote_copy(
            src_ref=src_for(to), dst_ref=dst_slot(me), send_sem=send_sems.at[k], recv_sem=recv_sems.at[k],
            device_id=to, device_id_type=MESH)

    def landed(k, frm):
        return pltpu.make_async_remote_copy(
            src_ref=src_for(frm), dst_ref=dst_slot(frm), send_sem=send_sems.at[k], recv_sem=recv_sems.at[k],
            device_id=frm, device_id_type=MESH)

    local = pltpu.make_async_copy(src_for(me), dst_slot(me), local_sem)
    peers = [_flip(me, f) for f in DEV_FLIPS]

    def start():
        local.start()
        for k, to in enumerate(peers):
            copy(k, to).start()

    def finish():
        for k, frm in enumerate(peers):
            landed(k, frm).wait_recv()
        for k, to in enumerate(peers):
            copy(k, to).wait_send()
        local.wait()

    return start, finish


def gather_weights(shard):
    m = shard.shape[0]

    def body(a_ref, o_ref, ici_send, ici_recv, d2d_send, d2d_recv, local_sem):
        x, y, c = _mesh_pos()
        me, sibling = (x, y, c), (x, y, 1 - c)
        chip_x, chip_y, chip_d = (1 - x, y), (x, 1 - y), (1 - x, 1 - y)

        def rows(pos):
            return o_ref.at[pl.ds(_lin(pos) * m, m), :]

        def ici(k, block, to, src=None):
            return pltpu.make_async_remote_copy(
                src_ref=rows(block) if src is None else src, dst_ref=rows(block),
                send_sem=ici_send.at[k], recv_sem=ici_recv.at[k], device_id=to, device_id_type=MESH)

        def d2d(k, chip, mine, src=None):
            block = (*chip, c) if mine else (*chip, 1 - c)
            return pltpu.make_async_remote_copy(
                src_ref=rows(block) if src is None else src, dst_ref=rows(block),
                send_sem=d2d_send.at[k], recv_sem=d2d_recv.at[k], device_id=sibling, device_id_type=MESH)

        local = pltpu.make_async_copy(a_ref, rows(me), local_sem)
        local.start()
        sends = [ici(0, me, (*chip_x, c), src=a_ref), ici(1, me, (*chip_y, c), src=a_ref),
                 d2d(0, (x, y), True, src=a_ref)]
        for cp in sends:
            cp.start()

        def pass_on(first, k_first, second, k_second, onward):
            ici(k_first, (*first, c), me).wait_recv()
            relay = ici(2, (*first, c), (*onward, c))
            relay.start()
            hand = [d2d(1 + k_first, first, True)]
            hand[0].start()
            ici(k_second, (*second, c), me).wait_recv()
            hand.append(d2d(1 + k_second, second, True))
            hand[1].start()
            ici(2, (*chip_d, c), me).wait_recv()
            hand.append(d2d(3, chip_d, True))
            hand[2].start()
            for cp in [relay] + hand:
                cp.wait_send()

        @pl.when(c == 0)
        def _():
            pass_on(chip_y, 1, chip_x, 0, chip_x)

        @pl.when(c == 1)
        def _():
            pass_on(chip_x, 0, chip_y, 1, chip_y)

        for k, chip in enumerate([(x, y), chip_x, chip_y, chip_d]):
            d2d(k, chip, False).wait_recv()
        for cp in sends:
            cp.wait_send()
        local.wait()

    vmem = pl.BlockSpec(memory_space=pltpu.VMEM)
    return pl.pallas_call(
        body, name="gather_weights",
        out_shape=jax.ShapeDtypeStruct((N_DEV * m, shard.shape[1]), shard.dtype),
        in_specs=[vmem], out_specs=vmem,
        scratch_shapes=[pltpu.SemaphoreType.DMA((3,)), pltpu.SemaphoreType.DMA((3,)),
                        pltpu.SemaphoreType.DMA((4,)), pltpu.SemaphoreType.DMA((4,)), pltpu.SemaphoreType.DMA],
        compiler_params=pltpu.CompilerParams(vmem_limit_bytes=VMEM_LIMIT),
    )(shard)


def exchange_grads(dwin_t, small):
    srcs = (dwin_t,)
    blocks = (IN_SHARD,)
    n_arr = len(srcs)

    def body(a_hbm, s_ref, ra_ref, rs_ref, own_a, sib_a, snd_a,
             d2d_send, d2d_recv, ici_send, ici_recv, own_sems, s_send, s_recv):
        x, y, c = _mesh_pos()
        me, sibling = (x, y, c), (x, y, 1 - c)
        ins, outs = (a_hbm,), (ra_ref,)
        own, sib, snd = (own_a,), (sib_a,), (snd_a,)
        chips = [_flip((x, y), f) for f in CHIP_FLIPS]

        def rows(a, pos):
            return ins[a].at[pl.ds(_lin(pos) * blocks[a], blocks[a]), :]

        def to_sibling(a, k):
            return pltpu.make_async_remote_copy(
                src_ref=rows(a, (*chips[k], 1 - c)), dst_ref=sib[a].at[k],
                send_sem=d2d_send.at[a, k], recv_sem=d2d_recv.at[a, k], device_id=sibling, device_id_type=MESH)

        def to_chip(a, k):
            return pltpu.make_async_remote_copy(
                src_ref=snd[a].at[k - 1], dst_ref=outs[a].at[k],
                send_sem=ici_send.at[a, k - 1], recv_sem=ici_recv.at[a, k - 1],
                device_id=(*chips[k], c), device_id_type=MESH)

        def small_to(k, to):
            return pltpu.make_async_remote_copy(
                src_ref=s_ref, dst_ref=rs_ref.at[_lin(me)], send_sem=s_send.at[k], recv_sem=s_recv.at[k],
                device_id=to, device_id_type=MESH)

        def small_from(k, frm):
            return pltpu.make_async_remote_copy(
                src_ref=s_ref, dst_ref=rs_ref.at[_lin(frm)], send_sem=s_send.at[k], recv_sem=s_recv.at[k],
                device_id=frm, device_id_type=MESH)

        order = (1, 2, 3, 0)
        swaps = [to_sibling(a, k) for k in order for a in range(n_arr)]
        for cp in swaps:
            cp.start()
        mine = {(a, k): pltpu.make_async_copy(rows(a, (*chips[k], c)), own[a].at[k], own_sems.at[a, k])
                for k in order for a in range(n_arr)}
        for cp in mine.values():
            cp.start()
        smalls = [small_to(k, _flip(me, mask)) for k, mask in enumerate(DEV_FLIPS)]
        for cp in smalls:
            cp.start()
        rs_ref[_lin(me)] = s_ref[...]

        sent = []
        for k in order:
            for a in range(n_arr):
                to_sibling(a, k).wait_recv()
                mine[(a, k)].wait()
                total = (own[a][k].astype(F32) + sib[a][k].astype(F32)).astype(BF16)
                if k == 0:
                    outs[a][0] = total
                else:
                    snd[a][k - 1] = total
                    sent.append(to_chip(a, k))
                    sent[-1].start()
        for k in (1, 2, 3):
            for a in range(n_arr):
                to_chip(a, k).wait_recv()
        for k, mask in enumerate(DEV_FLIPS):
            small_from(k, _flip(me, mask)).wait_recv()
        for cp in swaps + sent + smalls:
            cp.wait_send()

    def bufs(n):
        return [pltpu.VMEM((n, blocks[a], D_MODEL), BF16) for a in range(n_arr)]

    vmem, hbm = pl.BlockSpec(memory_space=pltpu.VMEM), pl.BlockSpec(memory_space=pl.ANY)
    return pl.pallas_call(
        body, name="exchange_grads",
        out_shape=[jax.ShapeDtypeStruct((4, blocks[a], D_MODEL), BF16) for a in range(n_arr)]
        + [jax.ShapeDtypeStruct((N_DEV,) + small.shape, small.dtype)],
        in_specs=[hbm, vmem], out_specs=[vmem] * 2,
        scratch_shapes=bufs(4) + bufs(4) + bufs(3) + [
            pltpu.SemaphoreType.DMA((n_arr, 4)), pltpu.SemaphoreType.DMA((n_arr, 4)),
            pltpu.SemaphoreType.DMA((n_arr, 3)), pltpu.SemaphoreType.DMA((n_arr, 3)),
            pltpu.SemaphoreType.DMA((n_arr, 4)), pltpu.SemaphoreType.DMA((7,)), pltpu.SemaphoreType.DMA((7,))],
        compiler_params=pltpu.CompilerParams(vmem_limit_bytes=VMEM_LIMIT),
    )(dwin_t, small)


def _norm_rope(xs, gain2, cos, sin_s, bd, lane):
    r = lax.rsqrt(_head_sum(xs * xs, bd) * (1.0 / HEAD_DIM) + EPS)
    y = xs * r * gain2
    return y * cos + _swap_half(y, lane) * sin_s


def _dup_heads(xs, lane):
    r = pltpu.roll(xs, HEAD_DIM, 1)
    lo = lane < HEAD_DIM
    return jnp.concatenate([jnp.where(lo, xs, r), jnp.where(lo, r, xs)], axis=1)


def fwd_proj(x, pos, norm_gain, win_t, inv_freq, sin_sign, q_gain2, k_gain2, wout_shard):
    n_tiles = T_LOC // TM

    def body(x_ref, pos_ref, ng_ref, w_ref, if_ref, sg_ref, qg_ref, kg_ref, ws_hbm,
             qa_raw_ref, ka_raw_ref, q_rot_ref, k_dup_ref, v_dup_ref, ga_ref, qb_ref, kb_ref, vb_ref, gb_ref,
             cos_ref, sin_ref, wo_hbm, wo_send, wo_recv, wo_local):
        start_wout, finish_wout = _direct_exchange(
            lambda dev: ws_hbm, lambda dev: wo_hbm.at[pl.ds(_lin(dev) * OUT_SHARD, OUT_SHARD), :],
            wo_send, wo_recv, wo_local)
        pl.when(pl.program_id(0) == 0)(start_wout)

        xv = x_ref[...]
        rstd = lax.rsqrt(jnp.mean(xv * xv, axis=-1, keepdims=True) + EPS)
        h = (xv * rstd * ng_ref[...]).astype(BF16)

        def proj(r0, r1):
            return _dot(h, w_ref[r0:r1, :], NT)

        ang = pos_ref[...].astype(F32) * if_ref[...]
        cos = jnp.cos(ang)
        sin_s = jnp.sin(ang) * sg_ref[...]
        cos_ref[...] = cos
        sin_ref[...] = sin_s
        lane = _lane((TM, SLAB))
        bd = _head_blockdiag()

        qa = proj(R_QA, R_KA)
        qa_raw_ref[...] = qa
        for p in range(4):
            sl = slice(p * SLAB, (p + 1) * SLAB)
            q_rot_ref[:, sl] = (_norm_rope(qa[:, sl], qg_ref[...], cos, sin_s, bd, lane) * Q_SCALE).astype(BF16)
        ka = proj(R_KA, R_VA)
        ka_raw_ref[...] = ka
        k_dup_ref[...] = _dup_heads(_norm_rope(ka, kg_ref[...], cos, sin_s, bd, lane), lane).astype(BF16)
        v_dup_ref[...] = _dup_heads(proj(R_VA, R_GA), lane).astype(BF16)
        ga_ref[...] = proj(R_GA, R_QB)
        qb_ref[...] = (proj(R_QB, R_KB) * Q_SCALE).astype(BF16)
        kb_ref[...] = proj(R_KB, R_VB).astype(BF16)
        vb_ref[...] = proj(R_VB, R_GB).astype(BF16)
        gb_ref[...] = proj(R_GB, R_END)
        pl.when(pl.program_id(0) == n_tiles - 1)(finish_wout)

    def tile(w):
        return pl.BlockSpec((TM, w), lambda i: (i, 0))

    def whole(a):
        return pl.BlockSpec(a.shape, lambda i: (0, 0))

    hbm = pl.BlockSpec(memory_space=pl.ANY)
    widths = [(512, F32), (128, F32), (512, BF16), (256, BF16), (256, BF16), (512, F32), (512, BF16), (512, BF16),
              (512, BF16), (512, F32), (128, F32), (128, F32)]
    return pl.pallas_call(
        body, name="fwd_proj", grid=(n_tiles,),
        in_specs=[tile(D_MODEL), tile(1), whole(norm_gain), whole(win_t), whole(inv_freq), whole(sin_sign),
                  whole(q_gain2), whole(k_gain2), hbm],
        out_specs=[tile(w) for w, _ in widths] + [hbm],
        out_shape=[jax.ShapeDtypeStruct((T_LOC, w), dt) for w, dt in widths]
        + [jax.ShapeDtypeStruct((D_MODEL, D_MODEL), BF16)],
        scratch_shapes=[pltpu.SemaphoreType.DMA((7,)), pltpu.SemaphoreType.DMA((7,)), pltpu.SemaphoreType.DMA],
        compiler_params=_params(("arbitrary",)),
    )(x, pos, norm_gain, win_t, inv_freq, sin_sign, q_gain2, k_gain2, wout_shard)


def _swa_window(prev_ref, cur_ref, p):
    gsl = _slab(p // 2)
    return jnp.concatenate([prev_ref[:, gsl], cur_ref[:, gsl]], axis=0)


def _swa_probs(s, sinks_ref, p, i):
    shape = (2 * BLK, 2 * BLK)
    r = _row(shape) & (BLK - 1)
    cidx = _lane(shape)
    valid = (cidx > r) & (cidx <= r + BLK) & ((cidx >= BLK) | (i > 0))
    s = jnp.where(valid, s, -jnp.inf)
    sink = jnp.where(_row((2 * BLK, 1)) < BLK, sinks_ref[0, 2 * p], sinks_ref[0, 2 * p + 1])
    m = jnp.maximum(jnp.max(s, axis=-1, keepdims=True), sink)
    e = jnp.exp(s - m)
    e_sink = jnp.exp(sink - m)
    den = jnp.sum(e, axis=-1, keepdims=True) + e_sink
    return e / den, e_sink / den


SWA_CHAINS = [(b, p) for b in range(B_LOC) for p in range(4)]


def _swa_specs():
    def cur(w):
        return pl.BlockSpec((B_LOC, BLK, w), lambda i: (0, i, 0))

    def prev(w):
        return pl.BlockSpec((B_LOC, BLK, w), lambda i: (0, jnp.maximum(i - 1, 0), 0))

    return cur, prev


def swa_fwd(q_rot, k_dup, v_dup, sinks):
    def body(q_ref, kp_ref, kc_ref, vp_ref, vc_ref, sinks_ref, o_ref):
        i = pl.program_id(0)
        lane = _lane((BLK, SLAB))
        s = [_dot(_stack_heads(q_ref[b, :, _slab(p)], lane), _swa_window(kp_ref.at[b], kc_ref.at[b], p), NT)
             for b, p in SWA_CHAINS]
        pn = [_swa_probs(s[c], sinks_ref, p, i)[0].astype(BF16) for c, (b, p) in enumerate(SWA_CHAINS)]
        for c, (b, p) in enumerate(SWA_CHAINS):
            o_ref[b, :, _slab(p)] = _unstack_heads(_dot(pn[c], _swa_window(vp_ref.at[b], vc_ref.at[b], p)), lane)

    cur, prev = _swa_specs()
    q3, k3, v3 = (a.reshape(B_LOC, SEQ, a.shape[1]) for a in (q_rot, k_dup, v_dup))
    return pl.pallas_call(
        body, name="swa_fwd", grid=(N_BLK,),
        in_specs=[cur(512), prev(256), cur(256), prev(256), cur(256), pl.BlockSpec(memory_space=pltpu.SMEM)],
        out_specs=cur(512),
        out_shape=jax.ShapeDtypeStruct((B_LOC, SEQ, 512), F32),
        compiler_params=_params(("arbitrary",)),
    )(q3, k3, k3, v3, v3, sinks).reshape(T_LOC, 512)


def _tri(suffix):
    r, cidx = _row((BLK + 16, BLK)), _lane((BLK + 16, BLK))
    tri = (cidx > r) if suffix else (cidx < r)
    return (tri | (r >= BLK)).astype(BF16)


def _key_sums(tri, x):
    res = _dot(tri, x.astype(BF16))
    return res[:BLK], res[BLK:BLK + 1]


def _sb_softplus(zt, valid):
    neg_abs = lax.bitcast_convert_type(lax.bitcast_convert_type(zt, jnp.uint32) | jnp.uint32(0x80000000), F32)
    sp = jnp.maximum(zt, 0.0) + jnp.log(1.0 + jnp.exp(neg_abs))
    return sp if valid is None else jnp.where(valid, sp, 0.0)


def _sb_weights(zt, sp, later, valid):
    w = jnp.exp(zt - sp - later)
    return w if valid is None else jnp.where(valid, w, 0.0)


def _slab(pp):
    return slice(pp * SLAB, (pp + 1) * SLAB)


def _blk(j):
    return pl.ds(pl.multiple_of(j * BLK, BLK), BLK)


def _causal_t():
    return _row((BLK, 2 * BLK)) < (_lane((BLK, 2 * BLK)) & (BLK - 1))


def _sb_rows(b, j):
    return pl.ds(pl.multiple_of(b * SEQ + j * BLK, BLK), BLK)


SB_CHAINS = [(b, pp) for b in range(B_LOC) for pp in range(4)]


def sb_fwd(qb, kb, vb):
    def body(q_ref, k_ref, v_ref, o_ref, c_ref, vt_ref, ot_ref, kc_ref):
        for c, (b, pp) in enumerate(SB_CHAINS):
            for j in range(N_BLK):
                vt_ref[c, j] = v_ref[b * SEQ + j * BLK:b * SEQ + (j + 1) * BLK, _slab(pp)].T
                kc_ref[c, j] = k_ref[b * SEQ + j * BLK:b * SEQ + (j + 1) * BLK, _slab(pp)]
        lane = _lane((BLK, SLAB))
        tri = _tri(True)
        valid = _causal_t()
        jrow = _row((N_BLK, 2 * BLK))
        chains = range(len(SB_CHAINS))

        def q_block(i, _):
            q2 = [_stack_heads(q_ref[_sb_rows(b, i), _slab(pp)], lane) for b, pp in SB_CHAINS]

            def key_block(j, carry, mask, first):
                zt = [_dot(kc_ref[c, j], q2[c], NT) for c in chains]
                sp = [_sb_softplus(zt[c], mask) for c in chains]
                sums = [_key_sums(tri, sp[c]) for c in chains]
                w = [_sb_weights(zt[c], sp[c], sums[c][0] + carry[c], mask) for c in chains]
                for c in chains:
                    pv = _dot(vt_ref[c, j], w[c].astype(BF16))
                    if first:
                        ot_ref[c] = pv
                    else:
                        ot_ref[c] += pv
                return tuple(carry[c] + sums[c][1] for c in chains)

            def earlier(jj, state):
                carry, saved = state
                j = i - 1 - jj
                saved = tuple(jnp.where(jrow == j, carry[c], saved[c]) for c in chains)
                return key_block(j, carry, None, False), saved

            zero = tuple(jnp.zeros((1, 2 * BLK), F32) for _ in chains)
            carry = key_block(i, zero, valid, True)
            _, saved = lax.fori_loop(0, i, earlier, (carry, tuple(jnp.zeros((N_BLK, 2 * BLK), F32) for _ in chains)))
            for c, (b, pp) in enumerate(SB_CHAINS):
                o_ref[_sb_rows(b, i), _slab(pp)] = _unstack_heads(ot_ref[c].T, lane)
                c_ref[c * N_BLK + i] = saved[c]
            return 0

        lax.fori_loop(0, N_BLK, q_block, 0)

    n_ch = len(SB_CHAINS)
    vmem = pl.BlockSpec(memory_space=pltpu.VMEM)
    return pl.pallas_call(
        body, name="sb_fwd",
        in_specs=[vmem] * 3, out_specs=[vmem] * 2,
        out_shape=[jax.ShapeDtypeStruct((T_LOC, 512), F32), jax.ShapeDtypeStruct((n_ch * N_BLK, N_BLK, 2 * BLK), F32)],
        scratch_shapes=[pltpu.VMEM((n_ch, N_BLK, SLAB, BLK), BF16), pltpu.VMEM((n_ch, SLAB, 2 * BLK), F32),
                        pltpu.VMEM((n_ch, N_BLK, BLK, SLAB), BF16)],
        compiler_params=pltpu.CompilerParams(vmem_limit_bytes=VMEM_LIMIT),
    )(qb, kb, vb)


def out_loss(o_a, o_b, ga, gb, x, target, wout):
    n_tiles = T_LOC // TM

    def body(oa_ref, ob_ref, ga_ref, gb_ref, x_ref, t_ref, w_ref,
             dout_ref, doa_ref, dob_ref, dga_ref, dgb_ref, dw_ref, loss_ref, acc_ref):
        step = pl.program_id(0)

        @pl.when(step == 0)
        def _():
            acc_ref[...] = jnp.zeros_like(acc_ref)
            loss_ref[...] = jnp.zeros_like(loss_ref)

        oa, ob, gav, gbv = oa_ref[...], ob_ref[...], ga_ref[...], gb_ref[...]
        sa, sb = _sigmoid(gav), _sigmoid(gbv)
        silu_a, silu_b = gav * sa, gbv * sb
        y = jnp.concatenate([oa * silu_a, ob * silu_b], axis=1).astype(BF16)
        err = x_ref[...] + _dot(y, w_ref[...]) - t_ref[...]
        e2 = err * err
        part = jnp.sum(e2.reshape(TM // 8, 8, D_MODEL), axis=0)
        loss_ref[...] += functools.reduce(lambda a, b: a + b, [part[:, k * 128:(k + 1) * 128] for k in range(8)])
        dout = err * (1.0 / D_MODEL)
        dout_ref[...] = dout
        dob16 = dout.astype(BF16)
        for r0 in range(0, D_MODEL, ACC_ROWS):
            acc_ref[r0:r0 + ACC_ROWS, :] += _dot(y[:, r0:r0 + ACC_ROWS], dob16, TN)
        dy = _dot(dob16, w_ref[...], NT)
        dya, dyb = dy[:, :512], dy[:, 512:]
        doa_ref[...] = (dya * silu_a).astype(BF16)
        dob_ref[...] = (dyb * silu_b).astype(BF16)
        dga_ref[...] = (dya * oa * (sa * (1.0 + gav * (1.0 - sa)))).astype(BF16)
        dgb_ref[...] = (dyb * ob * (sb * (1.0 + gbv * (1.0 - sb)))).astype(BF16)

        @pl.when(step == n_tiles - 1)
        def _():
            dw_ref[...] = acc_ref[...].astype(BF16)

    def tile(w):
        return pl.BlockSpec((TM, w), lambda i: (i, 0))

    const = lambda i: (0, 0)
    return pl.pallas_call(
        body, name="out_loss", grid=(n_tiles,),
        in_specs=[tile(512)] * 4 + [tile(D_MODEL)] * 2 + [pl.BlockSpec((D_MODEL, D_MODEL), const)],
        out_specs=[tile(D_MODEL), tile(512), tile(512), tile(512), tile(512),
                   pl.BlockSpec((D_MODEL, D_MODEL), const), pl.BlockSpec((8, 128), const)],
        out_shape=[jax.ShapeDtypeStruct((T_LOC, D_MODEL), F32)] + [jax.ShapeDtypeStruct((T_LOC, 512), BF16)] * 4
        + [jax.ShapeDtypeStruct((D_MODEL, D_MODEL), BF16), jax.ShapeDtypeStruct((8, 128), F32)],
        scratch_shapes=[pltpu.VMEM((D_MODEL, D_MODEL), F32)],
        compiler_params=_params(("arbitrary",)),
    )(o_a, o_b, ga, gb, x, target, wout)


def swa_bwd(q_rot, k_dup, v_dup, o_a, d_oa, sinks):
    def body(q_ref, kp_ref, kc_ref, vp_ref, vc_ref, o_ref, do_ref, sinks_ref, dq_ref, dk_ref, dv_ref, dsink_ref):
        i = pl.program_id(0)

        @pl.when(i == 0)
        def _():
            dk_ref[...] = jnp.zeros_like(dk_ref)
            dv_ref[...] = jnp.zeros_like(dv_ref)
            dsink_ref[...] = jnp.zeros_like(dsink_ref)

        lane = _lane((BLK, SLAB))
        rows_prev, rows_cur = _blk(jnp.maximum(i - 1, 0)), _blk(i)
        chains = range(len(SWA_CHAINS))
        q2 = [_stack_heads(q_ref[b, :, _slab(p)], lane) for b, p in SWA_CHAINS]
        do2 = [_stack_heads(do_ref[b, :, _slab(p)], lane) for b, p in SWA_CHAINS]
        keys = [_swa_window(kp_ref.at[b], kc_ref.at[b], p) for b, p in SWA_CHAINS]
        s = [_dot(q2[c], keys[c], NT) for c in chains]
        dp = [_dot(do2[c], _swa_window(vp_ref.at[b], vc_ref.at[b], p), NT) for c, (b, p) in enumerate(SWA_CHAINS)]
        ds, pn16, cols = [], [], []
        for c, (b, p) in enumerate(SWA_CHAINS):
            pn, p_sink = _swa_probs(s[c], sinks_ref, p, i)
            o = o_ref[b, :, _slab(p)]
            delta = jnp.sum(do2[c].astype(F32) * jnp.concatenate([o, o], axis=0), axis=-1, keepdims=True)
            ds.append((pn * (dp[c] - delta)).astype(BF16))
            pn16.append(pn.astype(BF16))
            cols.append(-p_sink * delta)
        for c, (b, p) in enumerate(SWA_CHAINS):
            dq_ref[b, :, _slab(p)] = _unstack_heads(_dot(ds[c], keys[c]), lane) * Q_SCALE
        dk2 = [_dot(ds[c], q2[c], TN) for c in chains]
        dv2 = [_dot(pn16[c], do2[c], TN) for c in chains]
        for c, (b, p) in enumerate(SWA_CHAINS):
            gsl = _slab(p // 2)
            dk_ref[b, rows_prev, gsl] += dk2[c][:BLK]
            dk_ref[b, rows_cur, gsl] += dk2[c][BLK:]
            dv_ref[b, rows_prev, gsl] += dv2[c][:BLK]
            dv_ref[b, rows_cur, gsl] += dv2[c][BLK:]
            for e in range(2):
                dsink_ref[2 * p + e:2 * p + e + 1, :] += jnp.sum(cols[c][e * BLK:(e + 1) * BLK], axis=0, keepdims=True)

    cur, prev = _swa_specs()
    whole = pl.BlockSpec((B_LOC, SEQ, 256), lambda i: (0, 0, 0))
    q3, k3, v3, o3, do3 = (a.reshape(B_LOC, SEQ, a.shape[1]) for a in (q_rot, k_dup, v_dup, o_a, d_oa))
    dq, dk, dv, dsink = pl.pallas_call(
        body, name="swa_bwd", grid=(N_BLK,),
        in_specs=[cur(512), prev(256), cur(256), prev(256), cur(256), cur(512), cur(512),
                  pl.BlockSpec(memory_space=pltpu.SMEM)],
        out_specs=[cur(512), whole, whole, pl.BlockSpec((8, 128), lambda i: (0, 0))],
        out_shape=[jax.ShapeDtypeStruct((B_LOC, SEQ, 512), F32), jax.ShapeDtypeStruct((B_LOC, SEQ, 256), F32),
                   jax.ShapeDtypeStruct((B_LOC, SEQ, 256), F32), jax.ShapeDtypeStruct((8, 128), F32)],
        compiler_params=_params(("arbitrary",)),
    )(q3, k3, k3, v3, v3, o3, do3, sinks)
    return dq.reshape(T_LOC, 512), dk.reshape(T_LOC, 256), dv.reshape(T_LOC, 256), dsink


def sb_bwd(qb, kb, vb, d_ob, carries, dwout):
    def body(q_ref, k_ref, v_ref, do_ref, c_ref, dw_hbm, dq_ref, dk_ref, dv_ref, rw_hbm, kt_ref, dqt_ref,
             rw_send, rw_recv, rw_local):
        start_dwout, finish_dwout = _direct_exchange(
            lambda dev: dw_hbm.at[pl.ds(_lin(dev) * OUT_SHARD, OUT_SHARD), :], lambda dev: rw_hbm.at[_lin(dev)],
            rw_send, rw_recv, rw_local)
        start_dwout()
        for c, (b, pp) in enumerate(SB_CHAINS):
            for j in range(N_BLK):
                kt_ref[c, j] = k_ref[b * SEQ + j * BLK:b * SEQ + (j + 1) * BLK, _slab(pp)].T
        dk_ref[...] = jnp.zeros_like(dk_ref)
        dv_ref[...] = jnp.zeros_like(dv_ref)
        dqt_ref[...] = jnp.zeros_like(dqt_ref)
        lane = _lane((BLK, SLAB))
        tri_after, tri_before = _tri(True), _tri(False)
        valid = _causal_t()
        jrow = _row((N_BLK, 2 * BLK))
        chains = range(len(SB_CHAINS))

        def q_block(i, _):
            q2 = [_stack_heads(q_ref[_sb_rows(b, i), _slab(pp)], lane) for b, pp in SB_CHAINS]
            do2 = [_stack_heads(do_ref[_sb_rows(b, i), _slab(pp)], lane) for b, pp in SB_CHAINS]

            def key_block(j, carry_sp, before_u, mask):
                at = [(_sb_rows(b, j), _slab(pp)) for b, pp in SB_CHAINS]
                zt = [_dot(k_ref[at[c]], q2[c], NT) for c in chains]
                dw = [_dot(v_ref[at[c]], do2[c], NT) for c in chains]
                sp = [_sb_softplus(zt[c], mask) for c in chains]
                later = [_key_sums(tri_after, sp[c])[0] for c in chains]
                w = [_sb_weights(zt[c], sp[c], later[c] + carry_sp[c], mask) for c in chains]
                u = [dw[c] * w[c] for c in chains]
                for c in chains:
                    dv_ref[at[c]] += _dot(w[c].astype(BF16), do2[c])
                sums = [_key_sums(tri_before, u[c]) for c in chains]
                dz16 = []
                for c in chains:
                    sig = jnp.exp(zt[c] - sp[c])
                    dz = u[c] - sig * (u[c] + before_u[c] + sums[c][0])
                    if mask is not None:
                        dz = jnp.where(mask, dz, 0.0)
                    dz16.append(dz.astype(BF16))
                for c in chains:
                    dk_ref[at[c]] += _dot(dz16[c], q2[c])
                    dqt_ref[c] += _dot(kt_ref[c, j], dz16[c])
                return tuple(before_u[c] + sums[c][1] for c in chains)

            def earlier(j, before_u):
                carry_sp = [jnp.sum(jnp.where(jrow == j, c_ref[c * N_BLK + i], 0.0), axis=0, keepdims=True)
                            for c in chains]
                return key_block(j, carry_sp, before_u, None)

            zero = tuple(jnp.zeros((1, 2 * BLK), F32) for _ in chains)
            before_u = lax.fori_loop(0, i, earlier, zero)
            key_block(i, zero, before_u, valid)
            for c, (b, pp) in enumerate(SB_CHAINS):
                dq_ref[_sb_rows(b, i), _slab(pp)] = (_unstack_heads(dqt_ref[c].T, lane) * Q_SCALE).astype(BF16)
                dqt_ref[c] = jnp.zeros((SLAB, 2 * BLK), F32)
            return 0

        lax.fori_loop(0, N_BLK, q_block, 0)
        finish_dwout()

    n_ch = len(SB_CHAINS)
    vmem, hbm = pl.BlockSpec(memory_space=pltpu.VMEM), pl.BlockSpec(memory_space=pl.ANY)
    return pl.pallas_call(
        body, name="sb_bwd",
        in_specs=[vmem] * 5 + [hbm], out_specs=[vmem] * 3 + [hbm],
        out_shape=[jax.ShapeDtypeStruct((T_LOC, 512), BF16)] + [jax.ShapeDtypeStruct((T_LOC, 512), F32)] * 2
        + [jax.ShapeDtypeStruct((N_DEV, OUT_SHARD, D_MODEL), BF16)],
        scratch_shapes=[pltpu.VMEM((n_ch, N_BLK, SLAB, BLK), BF16), pltpu.VMEM((n_ch, SLAB, 2 * BLK), F32),
                        pltpu.SemaphoreType.DMA((7,)), pltpu.SemaphoreType.DMA((7,)), pltpu.SemaphoreType.DMA],
        compiler_params=pltpu.CompilerParams(vmem_limit_bytes=VMEM_LIMIT),
    )(qb, kb, vb, d_ob, carries, dwout)


def bwd_in(x, dout, norm_gain, win_t, dq_rot, dk_dup, dv_dup, qa_raw, ka_raw, cos, sin_s, q_gain2, k_gain2,
           dga, dgb, dqb, dkb, dvb):
    n_tiles = T_LOC // TM

    def body(x_ref, dout_ref, ng_ref, w_hbm, dq_ref, dk_ref, dv_ref, qa_ref, ka_ref, cos_ref, sin_ref, qg_ref, kg_ref,
             dga_ref, dgb_ref, dqb_ref, dkb_ref, dvb_ref,
             gx_ref, dw_hbm, dng_ref, dqg_ref, dkg_ref, w_ref, acc_ref, stage_ref, w_sem):
        step = pl.program_id(0)

        @pl.when(step == 0)
        def _():
            cp = pltpu.make_async_copy(w_hbm, w_ref, w_sem)
            cp.start()
            acc_ref[...] = jnp.zeros_like(acc_ref)
            dng_ref[...] = jnp.zeros_like(dng_ref)
            dqg_ref[...] = jnp.zeros_like(dqg_ref)
            dkg_ref[...] = jnp.zeros_like(dkg_ref)
            cp.wait()

        lane = _lane((TM, SLAB))
        bd = _head_blockdiag()
        cos, sin_s = cos_ref[...], sin_ref[...]

        def norm_rope_bwd(d_rot, raw, gain2):
            dy = d_rot * cos + _swap_half(d_rot * sin_s, lane)
            r = lax.rsqrt(_head_sum(raw * raw, bd) * (1.0 / HEAD_DIM) + EPS)
            xhat = raw * r
            dgain = jnp.sum(dy * xhat, axis=0, keepdims=True)
            dxh = dy * gain2
            mean = _head_sum(dxh * xhat, bd) * (1.0 / HEAD_DIM)
            return r * (dxh - xhat * mean), dgain

        def fold_dup(d_dup):
            a, b2 = d_dup[:, :SLAB], d_dup[:, SLAB:]
            return jnp.where(lane < HEAD_DIM, a + pltpu.roll(a, HEAD_DIM, 1), b2 + pltpu.roll(b2, HEAD_DIM, 1))

        pieces = []
        dqg = jnp.zeros((1, SLAB), F32)
        for p in range(4):
            sl = slice(p * SLAB, (p + 1) * SLAB)
            d_raw, dg = norm_rope_bwd(dq_ref[:, sl], qa_ref[:, sl], qg_ref[...])
            pieces.append(d_raw.astype(BF16))
            dqg = dqg + dg
        d_raw, dkg = norm_rope_bwd(fold_dup(dk_ref[...]), ka_ref[...], kg_ref[...])
        pieces.append(d_raw.astype(BF16))
        pieces.append(fold_dup(dv_ref[...]).astype(BF16))
        pieces += [dga_ref[...], dqb_ref[...], dkb_ref[...].astype(BF16), dvb_ref[...].astype(BF16),
                   dgb_ref[...]]
        dproj = jnp.concatenate(pieces, axis=1)
        dqg_ref[0:1, :] += dqg + pltpu.roll(dqg, HEAD_DIM, 1)
        dkg_ref[0:1, :] += dkg + pltpu.roll(dkg, HEAD_DIM, 1)

        xv = x_ref[...]
        rstd = lax.rsqrt(jnp.mean(xv * xv, axis=-1, keepdims=True) + EPS)
        xhat = xv * rstd
        gain = ng_ref[...]
        h = (xhat * gain).astype(BF16)
        for r0 in range(0, IN_WIDTH, ACC_ROWS):
            acc_ref[r0:r0 + ACC_ROWS, :] += _dot(dproj[:, r0:r0 + ACC_ROWS], h, TN)
        dh = _dot(dproj, w_ref[...])
        dng_ref[0:1, :] += jnp.sum(dh * xhat, axis=0, keepdims=True)
        dxh = dh * gain
        gx_ref[...] = dout_ref[...] + rstd * (dxh - xhat * jnp.mean(dxh * xhat, axis=-1, keepdims=True))

        @pl.when(step == n_tiles - 1)
        def _():
            for r0 in range(0, IN_WIDTH, ACC_ROWS):
                stage_ref[...] = acc_ref[r0:r0 + ACC_ROWS, :].astype(BF16)
                pltpu.sync_copy(stage_ref, dw_hbm.at[r0:r0 + ACC_ROWS, :])

    def tile(w):
        return pl.BlockSpec((TM, w), lambda i: (i, 0))

    def whole(a):
        return pl.BlockSpec(a.shape, lambda i: (0, 0))

    const = lambda i: (0, 0)
    return pl.pallas_call(
        body, name="bwd_in", grid=(n_tiles,),
        in_specs=[tile(D_MODEL), tile(D_MODEL), whole(norm_gain), pl.BlockSpec(memory_space=pl.ANY),
                  tile(512), tile(256), tile(256), tile(512), tile(128), tile(128), tile(128),
                  whole(q_gain2), whole(k_gain2), tile(512), tile(512), tile(512), tile(512), tile(512)],
        out_specs=[tile(D_MODEL), pl.BlockSpec(memory_space=pl.ANY), pl.BlockSpec((8, D_MODEL), const),
                   pl.BlockSpec((8, SLAB), const), pl.BlockSpec((8, SLAB), const)],
        out_shape=[jax.ShapeDtypeStruct((T_LOC, D_MODEL), F32), jax.ShapeDtypeStruct((IN_WIDTH, D_MODEL), BF16),
                   jax.ShapeDtypeStruct((8, D_MODEL), F32), jax.ShapeDtypeStruct((8, SLAB), F32),
                   jax.ShapeDtypeStruct((8, SLAB), F32)],
        scratch_shapes=[pltpu.VMEM((IN_WIDTH, D_MODEL), BF16), pltpu.VMEM((IN_WIDTH, D_MODEL), F32),
                        pltpu.VMEM((ACC_ROWS, D_MODEL), BF16), pltpu.SemaphoreType.DMA],
        compiler_params=_params(("arbitrary",)),
    )(x, dout, norm_gain, win_t, dq_rot, dk_dup, dv_dup, qa_raw, ka_raw, cos, sin_s, q_gain2, k_gain2,
      dga, dgb, dqb, dkb, dvb)


def _adamw(w, g, m, v):
    m = ADAM_B1 * m + (1.0 - ADAM_B1) * g
    v = ADAM_B2 * v + (1.0 - ADAM_B2) * (g * g)
    m_hat = m / (1.0 - ADAM_B1 ** ADAM_STEP)
    v_hat = v / (1.0 - ADAM_B2 ** ADAM_STEP)
    delta = -ADAM_LR * (m_hat / (jnp.sqrt(v_hat) + ADAM_EPS) + ADAM_WD * w)
    return delta, m, v


def _sum_slots(r_ref):
    g = r_ref[0].astype(F32)
    for s in range(1, r_ref.shape[0]):
        g = g + r_ref[s].astype(F32)
    return g


def adamw_rows(name, recv, w, m, v):
    def body(r_ref, w_ref, m_ref, v_ref, g_ref, d_ref, nm_ref, nv_ref):
        g = _sum_slots(r_ref)
        g_ref[...] = g
        d_ref[...], nm_ref[...], nv_ref[...] = _adamw(w_ref[...], g, m_ref[...], v_ref[...])

    return pl.pallas_call(
        body, name=name,
        out_shape=[jax.ShapeDtypeStruct(w.shape, F32)] * 4,
        compiler_params=pltpu.CompilerParams(vmem_limit_bytes=VMEM_LIMIT),
    )(recv, w, m, v)


def adamw_small(recv, weights, moments_m, moments_v):
    n = len(weights)

    def body(r_ref, *refs):
        ins, outs = refs[:3 * n], refs[3 * n:]
        s = _sum_slots(r_ref)
        eye = (_row((8, SLAB)) == _lane((8, SLAB))).astype(F32)
        sinks = jnp.sum(s[:, 1280:1408] * eye, axis=0, keepdims=True)
        grads = [s[0:1, :D_MODEL], s[0:1, 1024:1024 + HEAD_DIM], s[0:1, 1152:1152 + HEAD_DIM], sinks[:, :8]]
        for k in range(n):
            outs[k][...] = grads[k]
            outs[n + k][...], outs[2 * n + k][...], outs[3 * n + k][...] = _adamw(
                ins[k][...], grads[k], ins[n + k][...], ins[2 * n + k][...])
        loss = jnp.sum(jnp.sum(s[:, 1408:1536], axis=1, keepdims=True), axis=0, keepdims=True) * (0.5 / D_MODEL)
        outs[4 * n][...] = loss

    res = pl.pallas_call(
        body, name="adamw_small",
        out_shape=[jax.ShapeDtypeStruct(w.shape, F32) for w in weights] * 4 + [jax.ShapeDtypeStruct((1, 1), F32)],
        compiler_params=pltpu.CompilerParams(vmem_limit_bytes=VMEM_LIMIT),
    )(recv, *weights, *moments_m, *moments_v)
    return res[:n], res[n:2 * n], res[2 * n:3 * n], res[3 * n:4 * n], res[4 * n]


def kernel(x, positions, norm_gain, w_in, q_norm_gain, k_norm_gain, sinks, w_out, loss_target, m_norm_gain, m_w_in, m_q_norm_gain, m_k_norm_gain, m_sinks, m_w_out, v_norm_gain, v_w_in, v_q_norm_gain, v_k_norm_gain, v_sinks, v_w_out):
    x2 = x.reshape(T_LOC, D_MODEL)
    tgt2 = loss_target.reshape(T_LOC, D_MODEL)
    pos2 = positions.reshape(T_LOC, 1)
    half = HEAD_DIM // 2
    inv_freq = ROPE_THETA ** (-jnp.arange(half, dtype=F32) * 2.0 / HEAD_DIM)
    inv_freq = jnp.tile(inv_freq, SLAB // half).reshape(1, SLAB)
    sin_sign = jnp.tile(jnp.concatenate([-jnp.ones((half,), F32), jnp.ones((half,), F32)]), 2).reshape(1, SLAB)
    q_gain2 = jnp.tile(q_norm_gain, (1, 2))
    k_gain2 = jnp.tile(k_norm_gain, (1, 2))

    win_t = gather_weights(w_in.reshape(D_MODEL, IN_SHARD).T.astype(BF16))

    (qa_raw, ka_raw, q_rot, k_dup, v_dup, ga, qb, kb, vb, gb, cos, sin_s, wout) = fwd_proj(
        x2, pos2, norm_gain, win_t, inv_freq, sin_sign, q_gain2, k_gain2, w_out.reshape(OUT_SHARD, D_MODEL).astype(BF16))
    o_a = swa_fwd(q_rot, k_dup, v_dup, sinks)
    o_b, carries = sb_fwd(qb, kb, vb)
    dout, d_oa, d_ob, dga, dgb, dwout, loss_part = out_loss(o_a, o_b, ga, gb, x2, tgt2, wout)
    dq_rot, dk_dup, dv_dup, dsink = swa_bwd(q_rot, k_dup, v_dup, o_a, d_oa, sinks)
    dqb, dkb, dvb, r_out = sb_bwd(qb, kb, vb, d_ob, carries, dwout)
    grad_x, dwin_t, dng, dqg, dkg = bwd_in(
        x2, dout, norm_gain, win_t, dq_rot, dk_dup, dv_dup, qa_raw, ka_raw, cos, sin_s, q_gain2, k_gain2,
        dga, dgb, dqb, dkb, dvb)

    small = jnp.concatenate([dng, dqg, dkg, dsink, loss_part], axis=1)
    r_win, r_small = exchange_grads(dwin_t, small)

    w_in2, m_in2, v_in2 = (a.reshape(D_MODEL, IN_SHARD).T for a in (w_in, m_w_in, v_w_in))
    w_out2, m_out2, v_out2 = (a.reshape(OUT_SHARD, D_MODEL) for a in (w_out, m_w_out, v_w_out))
    big_in = adamw_rows("adamw_w_in", r_win, w_in2, m_in2, v_in2)
    big_out = adamw_rows("adamw_w_out", r_out, w_out2, m_out2, v_out2)
    *small_out, loss = adamw_small(
        r_small, (norm_gain, q_norm_gain, k_norm_gain, sinks),
        (m_norm_gain, m_q_norm_gain, m_k_norm_gain, m_sinks), (v_norm_gain, v_q_norm_gain, v_k_norm_gain, v_sinks))

    def leaves(k):
        ng, qg, kg, sk = small_out[k]
        return (ng, big_in[k].T.reshape(1, D_MODEL, IN_SHARD), qg, kg, sk, big_out[k].reshape(1, OUT_SHARD, D_MODEL))

    return (loss.reshape(()), grad_x.reshape(B_LOC, SEQ, D_MODEL), *leaves(0), *leaves(1), *leaves(2), *leaves(3))
```

```python
import functools

import jax
import jax.numpy as jnp
from jax import lax
from jax.experimental import pallas as pl
from jax.experimental.pallas import tpu as pltpu

F32 = jnp.float32
BF16 = jnp.bfloat16

N_DEV = 8
D_MODEL = 1024
SEQ = 2048
B_LOC = 2
T_LOC = B_LOC * SEQ
HEAD_DIM = 64
HEAD_SHIFT = 6
BLK = 128
N_BLK = SEQ // BLK
SLAB = 128
IN_WIDTH = 3328
IN_SHARD = IN_WIDTH // N_DEV
OUT_SHARD = D_MODEL // N_DEV
EPS = 1e-6
ROPE_THETA = 10000.0
Q_SCALE = 0.125
R_QA, R_KA, R_VA, R_GA, R_QB, R_KB, R_VB, R_GB, R_END = 0, 512, 640, 768, 1280, 1792, 2304, 2816, 3328
SMALL_W = 1536
ADAM_LR, ADAM_B1, ADAM_B2, ADAM_EPS, ADAM_WD, ADAM_STEP = 0.001, 0.9, 0.999, 1e-08, 0.01, 10
TM = 256
TM_FWD = 512
ACC_ROWS = 256
VMEM_LIMIT = 56 * 1024 * 1024

MESH = pl.DeviceIdType.MESH
NT = (((1,), (1,)), ((), ()))
TN = (((0,), (0,)), ((), ()))


def _params(sem, limit=VMEM_LIMIT):
    return pltpu.CompilerParams(dimension_semantics=sem, vmem_limit_bytes=limit)


def _dot(a, b, dims=None):
    if dims is None:
        return jnp.dot(a, b, preferred_element_type=F32)
    return lax.dot_general(a, b, dims, preferred_element_type=F32)


def _split(x):
    hi = x.astype(BF16)
    return hi, (x - hi.astype(F32)).astype(BF16)


def _lane(shape):
    return lax.broadcasted_iota(jnp.int32, shape, len(shape) - 1)


def _row(shape):
    return lax.broadcasted_iota(jnp.int32, shape, 0)


def _head_blockdiag():
    return ((_row((SLAB, SLAB)) >> HEAD_SHIFT) == (_lane((SLAB, SLAB)) >> HEAD_SHIFT)).astype(BF16)


def _head_sum(x, bd):
    hi, lo = _split(x)
    return _dot(hi, bd) + _dot(lo, bd)


def _swap_half(y, lane):
    return jnp.where((lane & 32) != 0, pltpu.roll(y, 32, 1), pltpu.roll(y, 96, 1))


def _stack_heads(q, lane):
    zero = jnp.zeros_like(q)
    return jnp.concatenate([jnp.where(lane < HEAD_DIM, q, zero), jnp.where(lane >= HEAD_DIM, q, zero)], axis=0)


def _unstack_heads(x2, lane):
    return jnp.where(lane < HEAD_DIM, x2[:BLK], x2[BLK:])


def _sigmoid(x):
    return 1.0 / (1.0 + jnp.exp(-x))


def _mesh_pos():
    return lax.axis_index("x"), lax.axis_index("y"), lax.axis_index("c")


def _flip(pos, mask):
    return tuple(1 - p if m else p for p, m in zip(pos, mask))


def _lin(pos):
    return 4 * pos[0] + 2 * pos[1] + pos[2]


DEV_FLIPS = [(fx, fy, fc) for fx in (0, 1) for fy in (0, 1) for fc in (0, 1)][1:]


def _direct_exchange(src_for, dst_slot, send_sems, recv_sems, local_sem):
    me = _mesh_pos()

    def copy(k, to):
        return pltpu.make_async_remote_copy(
            src_ref=src_for(to), dst_ref=dst_slot(me), send_sem=send_sems.at[k], recv_sem=recv_sems.at[k],
            device_id=to, device_id_type=MESH)

    def landed(k, frm):
        return pltpu.make_async_remote_copy(
            src_ref=src_for(frm), dst_ref=dst_slot(frm), send_sem=send_sems.at[k], recv_sem=recv_sems.at[k],
            device_id=frm, device_id_type=MESH)

    local = None if local_sem is None else pltpu.make_async_copy(src_for(me), dst_slot(me), local_sem)
    peers = [_flip(me, f) for f in DEV_FLIPS]

    def start():
        if local is not None:
            local.start()
        for k, to in enumerate(peers):
            copy(k, to).start()

    def finish():
        for k, frm in enumerate(peers):
            landed(k, frm).wait_recv()
        for k, to in enumerate(peers):
            copy(k, to).wait_send()
        if local is not None:
            local.wait()

    return start, finish


def gather_weights(shard):
    m = shard.shape[0]

    def body(a_ref, o_ref, ici_send, ici_recv, d2d_send, d2d_recv, local_sem):
        x, y, c = _mesh_pos()
        me, sibling = (x, y, c), (x, y, 1 - c)
        chip_x, chip_y, chip_d = (1 - x, y), (x, 1 - y), (1 - x, 1 - y)

        def rows(pos):
            return o_ref.at[pl.ds(_lin(pos) * m, m), :]

        def ici(k, block, to, src=None):
            return pltpu.make_async_remote_copy(
                src_ref=rows(block) if src is None else src, dst_ref=rows(block),
                send_sem=ici_send.at[k], recv_sem=ici_recv.at[k], device_id=to, device_id_type=MESH)

        def d2d(k, chip, mine, src=None):
            block = (*chip, c) if mine else (*chip, 1 - c)
            return pltpu.make_async_remote_copy(
                src_ref=rows(block) if src is None else src, dst_ref=rows(block),
                send_sem=d2d_send.at[k], recv_sem=d2d_recv.at[k], device_id=sibling, device_id_type=MESH)

        local = pltpu.make_async_copy(a_ref, rows(me), local_sem)
        local.start()
        sends = [ici(0, me, (*chip_x, c), src=a_ref), ici(1, me, (*chip_y, c), src=a_ref),
                 d2d(0, (x, y), True, src=a_ref)]
        for cp in sends:
            cp.start()

        def pass_on(first, k_first, second, k_second, onward):
            ici(k_first, (*first, c), me).wait_recv()
            relay = ici(2, (*first, c), (*onward, c))
            relay.start()
            hand = [d2d(1 + k_first, first, True)]
            hand[0].start()
            ici(k_second, (*second, c), me).wait_recv()
            hand.append(d2d(1 + k_second, second, True))
            hand[1].start()
            ici(2, (*chip_d, c), me).wait_recv()
            hand.append(d2d(3, chip_d, True))
            hand[2].start()
            for cp in [relay] + hand:
                cp.wait_send()

        @pl.when(c == 0)
        def _():
            pass_on(chip_y, 1, chip_x, 0, chip_x)

        @pl.when(c == 1)
        def _():
            pass_on(chip_x, 0, chip_y, 1, chip_y)

        for k, chip in enumerate([(x, y), chip_x, chip_y, chip_d]):
            d2d(k, chip, False).wait_recv()
        for cp in sends:
            cp.wait_send()
        local.wait()

    vmem = pl.BlockSpec(memory_space=pltpu.VMEM)
    return pl.pallas_call(
        body, name="gather_weights",
        out_shape=jax.ShapeDtypeStruct((N_DEV * m, shard.shape[1]), shard.dtype),
        in_specs=[vmem], out_specs=vmem,
        scratch_shapes=[pltpu.SemaphoreType.DMA((3,)), pltpu.SemaphoreType.DMA((3,)),
                        pltpu.SemaphoreType.DMA((4,)), pltpu.SemaphoreType.DMA((4,)), pltpu.SemaphoreType.DMA],
        compiler_params=pltpu.CompilerParams(vmem_limit_bytes=VMEM_LIMIT),
    )(shard)


def exchange_grads(dwin_t, small):
    rows_per = IN_SHARD

    def body(a_hbm, s_ref, ra_ref, rs_ref, own_ref, sib_ref, snd_ref, extra_ref,
             d2d_send, d2d_recv, ici_send, ici_recv, own_sems, s_send, s_recv):
        x, y, c = _mesh_pos()
        me, sibling = (x, y, c), (x, y, 1 - c)
        chips = {"own": (x, y), "x": (1 - x, y), "y": (x, 1 - y), "d": (1 - x, 1 - y)}
        index = {"own": 0, "x": 1, "y": 2, "d": 3}

        def rows(pos):
            return a_hbm.at[pl.ds(_lin(pos) * rows_per, rows_per), :]

        def to_sibling(k):
            return pltpu.make_async_remote_copy(
                src_ref=rows((*chips[k], 1 - c)), dst_ref=sib_ref.at[index[k]],
                send_sem=d2d_send.at[index[k]], recv_sem=d2d_recv.at[index[k]], device_id=sibling, device_id_type=MESH)

        def ici(n, to_chip, dst):
            return pltpu.make_async_remote_copy(
                src_ref=snd_ref.at[n], dst_ref=dst, send_sem=ici_send.at[n], recv_sem=ici_recv.at[n],
                device_id=(*chips[to_chip], c), device_id_type=MESH)

        order = ("d", "x", "y", "own")
        swaps = [to_sibling(k) for k in order]
        for cp in swaps:
            cp.start()
        mine = {k: pltpu.make_async_copy(rows((*chips[k], c)), own_ref.at[index[k]], own_sems.at[index[k]])
                for k in order}
        for cp in mine.values():
            cp.start()
        start_small, finish_small = _direct_exchange(
            lambda dev: s_ref, lambda dev: rs_ref.at[_lin(dev)], s_send, s_recv, None)
        start_small()
        rs_ref[_lin(me)] = s_ref[...]

        def chip_sum(k):
            to_sibling(k).wait_recv()
            mine[k].wait()
            return own_ref[index[k]].astype(F32) + sib_ref[index[k]].astype(F32)

        def scatter(direct, via):
            snd_ref[0] = chip_sum("d").astype(BF16)
            sends = [ici(0, direct, extra_ref)]
            sends[0].start()
            snd_ref[1] = chip_sum(direct).astype(BF16)
            sends.append(ici(1, direct, ra_ref.at[index[direct]]))
            sends[1].start()
            merged = chip_sum(via)
            ici(0, direct, extra_ref).wait_recv()
            snd_ref[2] = (merged + extra_ref[...].astype(F32)).astype(BF16)
            sends.append(ici(2, via, ra_ref.at[index[via]]))
            sends[2].start()
            ra_ref[0] = chip_sum("own").astype(BF16)
            ici(1, direct, ra_ref.at[index[direct]]).wait_recv()
            ici(2, via, ra_ref.at[index[via]]).wait_recv()
            for cp in sends:
                cp.wait_send()

        @pl.when(c == 0)
        def _():
            scatter("x", "y")

        @pl.when(c == 1)
        def _():
            scatter("y", "x")

        finish_small()
        for cp in swaps:
            cp.wait_send()

    vmem, hbm = pl.BlockSpec(memory_space=pltpu.VMEM), pl.BlockSpec(memory_space=pl.ANY)
    block = (rows_per, D_MODEL)
    return pl.pallas_call(
        body, name="exchange_grads",
        out_shape=[jax.ShapeDtypeStruct((3,) + block, BF16), jax.ShapeDtypeStruct((N_DEV,) + small.shape, small.dtype)],
        in_specs=[hbm, vmem], out_specs=[vmem] * 2,
        scratch_shapes=[pltpu.VMEM((4,) + block, BF16), pltpu.VMEM((4,) + block, BF16), pltpu.VMEM((3,) + block, BF16),
                        pltpu.VMEM(block, BF16),
                        pltpu.SemaphoreType.DMA((4,)), pltpu.SemaphoreType.DMA((4,)),
                        pltpu.SemaphoreType.DMA((3,)), pltpu.SemaphoreType.DMA((3,)),
                        pltpu.SemaphoreType.DMA((4,)), pltpu.SemaphoreType.DMA((7,)), pltpu.SemaphoreType.DMA((7,))],
        compiler_params=pltpu.CompilerParams(vmem_limit_bytes=VMEM_LIMIT),
    )(dwin_t, small)


def _norm_rope(xs, gain2, cos, sin_s, bd, lane):
    r = lax.rsqrt(_head_sum(xs * xs, bd) * (1.0 / HEAD_DIM) + EPS)
    y = xs * r * gain2
    return y * cos + _swap_half(y, lane) * sin_s


def _dup_heads(xs, lane):
    r = pltpu.roll(xs, HEAD_DIM, 1)
    lo = lane < HEAD_DIM
    return jnp.concatenate([jnp.where(lo, xs, r), jnp.where(lo, r, xs)], axis=1)


def fwd_proj(x, pos, norm_gain, win_t, inv_freq, sin_sign, q_gain2, k_gain2, wout_shard):
    n_tiles = T_LOC // TM_FWD

    def body(x_ref, pos_ref, ng_ref, w_ref, if_ref, sg_ref, qg_ref, kg_ref, ws_hbm,
             qa_raw_ref, ka_raw_ref, q_rot_ref, k_dup_ref, v_dup_ref, ga_ref, qb_ref, kb_ref, vb_ref, gb_ref,
             cos_ref, sin_ref, wo_hbm, wo_send, wo_recv, wo_local):
        start_wout, finish_wout = _direct_exchange(
            lambda dev: ws_hbm, lambda dev: wo_hbm.at[pl.ds(_lin(dev) * OUT_SHARD, OUT_SHARD), :],
            wo_send, wo_recv, wo_local)
        pl.when(pl.program_id(0) == 0)(start_wout)

        xv = x_ref[...]
        rstd = lax.rsqrt(jnp.mean(xv * xv, axis=-1, keepdims=True) + EPS)
        h = (xv * rstd * ng_ref[...]).astype(BF16)

        def proj(r0, r1):
            return _dot(h, w_ref[r0:r1, :], NT)

        ang = pos_ref[...].astype(F32) * if_ref[...]
        cos = jnp.cos(ang)
        sin_s = jnp.sin(ang) * sg_ref[...]
        cos_ref[...] = cos
        sin_ref[...] = sin_s
        lane = _lane((TM_FWD, SLAB))
        bd = _head_blockdiag()

        qa = proj(R_QA, R_KA)
        qa_raw_ref[...] = qa
        for p in range(4):
            sl = slice(p * SLAB, (p + 1) * SLAB)
            q_rot_ref[:, sl] = (_norm_rope(qa[:, sl], qg_ref[...], cos, sin_s, bd, lane) * Q_SCALE).astype(BF16)
        ka = proj(R_KA, R_VA)
        ka_raw_ref[...] = ka
        k_dup_ref[...] = _dup_heads(_norm_rope(ka, kg_ref[...], cos, sin_s, bd, lane), lane).astype(BF16)
        v_dup_ref[...] = _dup_heads(proj(R_VA, R_GA), lane).astype(BF16)
        ga_ref[...] = proj(R_GA, R_QB)
        qb_ref[...] = (proj(R_QB, R_KB) * Q_SCALE).astype(BF16)
        kb_ref[...] = proj(R_KB, R_VB).astype(BF16)
        vb_ref[...] = proj(R_VB, R_GB).astype(BF16)
        gb_ref[...] = proj(R_GB, R_END)
        pl.when(pl.program_id(0) == n_tiles - 1)(finish_wout)

    def tile(w):
        return pl.BlockSpec((TM_FWD, w), lambda i: (i, 0))

    def whole(a):
        return pl.BlockSpec(a.shape, lambda i: (0, 0))

    hbm = pl.BlockSpec(memory_space=pl.ANY)
    widths = [(512, F32), (128, F32), (512, BF16), (256, BF16), (256, BF16), (512, F32), (512, BF16), (512, BF16),
              (512, BF16), (512, F32), (128, F32), (128, F32)]
    return pl.pallas_call(
        body, name="fwd_proj", grid=(n_tiles,),
        in_specs=[tile(D_MODEL), tile(1), whole(norm_gain), whole(win_t), whole(inv_freq), whole(sin_sign),
                  whole(q_gain2), whole(k_gain2), hbm],
        out_specs=[tile(w) for w, _ in widths] + [hbm],
        out_shape=[jax.ShapeDtypeStruct((T_LOC, w), dt) for w, dt in widths]
        + [jax.ShapeDtypeStruct((D_MODEL, D_MODEL), BF16)],
        scratch_shapes=[pltpu.SemaphoreType.DMA((7,)), pltpu.SemaphoreType.DMA((7,)), pltpu.SemaphoreType.DMA],
        compiler_params=_params(("arbitrary",)),
    )(x, pos, norm_gain, win_t, inv_freq, sin_sign, q_gain2, k_gain2, wout_shard)


def _swa_window(prev_ref, cur_ref, p):
    gsl = _slab(p // 2)
    return jnp.concatenate([prev_ref[:, gsl], cur_ref[:, gsl]], axis=0)


def _swa_probs(s, sinks_ref, p, i):
    shape = (2 * BLK, 2 * BLK)
    r = _row(shape) & (BLK - 1)
    cidx = _lane(shape)
    valid = (cidx > r) & (cidx <= r + BLK) & ((cidx >= BLK) | (i > 0))
    s = jnp.where(valid, s, -jnp.inf)
    sink = jnp.where(_row((2 * BLK, 1)) < BLK, sinks_ref[0, 2 * p], sinks_ref[0, 2 * p + 1])
    m = jnp.maximum(jnp.max(s, axis=-1, keepdims=True), sink)
    e = jnp.exp(s - m)
    e_sink = jnp.exp(sink - m)
    den = jnp.sum(e, axis=-1, keepdims=True) + e_sink
    return e / den, e_sink / den


SWA_CHAINS = [(b, p) for b in range(B_LOC) for p in range(4)]


def _swa_specs():
    def cur(w):
        return pl.BlockSpec((B_LOC, BLK, w), lambda i: (0, i, 0))

    def prev(w):
        return pl.BlockSpec((B_LOC, BLK, w), lambda i: (0, jnp.maximum(i - 1, 0), 0))

    return cur, prev


def swa_fwd(q_rot, k_dup, v_dup, sinks):
    def body(q_ref, kp_ref, kc_ref, vp_ref, vc_ref, sinks_ref, o_ref):
        i = pl.program_id(0)
        lane = _lane((BLK, SLAB))
        s = [_dot(_stack_heads(q_ref[b, :, _slab(p)], lane), _swa_window(kp_ref.at[b], kc_ref.at[b], p), NT)
             for b, p in SWA_CHAINS]
        pn = [_swa_probs(s[c], sinks_ref, p, i)[0].astype(BF16) for c, (b, p) in enumerate(SWA_CHAINS)]
        for c, (b, p) in enumerate(SWA_CHAINS):
            o_ref[b, :, _slab(p)] = _unstack_heads(_dot(pn[c], _swa_window(vp_ref.at[b], vc_ref.at[b], p)), lane)

    cur, prev = _swa_specs()
    q3, k3, v3 = (a.reshape(B_LOC, SEQ, a.shape[1]) for a in (q_rot, k_dup, v_dup))
    return pl.pallas_call(
        body, name="swa_fwd", grid=(N_BLK,),
        in_specs=[cur(512), prev(256), cur(256), prev(256), cur(256), pl.BlockSpec(memory_space=pltpu.SMEM)],
        out_specs=cur(512),
        out_shape=jax.ShapeDtypeStruct((B_LOC, SEQ, 512), F32),
        compiler_params=_params(("arbitrary",)),
    )(q3, k3, k3, v3, v3, sinks).reshape(T_LOC, 512)


def _tri(suffix):
    r, cidx = _row((BLK + 16, BLK)), _lane((BLK + 16, BLK))
    tri = (cidx > r) if suffix else (cidx < r)
    return (tri | (r >= BLK)).astype(BF16)


def _key_sums(tri, x):
    res = _dot(tri, x.astype(BF16))
    return res[:BLK], res[BLK:BLK + 1]


def _sb_softplus(zt, valid):
    neg_abs = lax.bitcast_convert_type(lax.bitcast_convert_type(zt, jnp.uint32) | jnp.uint32(0x80000000), F32)
    sp = jnp.maximum(zt, 0.0) + jnp.log(1.0 + jnp.exp(neg_abs))
    return sp if valid is None else jnp.where(valid, sp, 0.0)


def _sb_weights(zt, sp, later, valid):
    w = jnp.exp(zt - sp - later)
    return w if valid is None else jnp.where(valid, w, 0.0)


def _slab(pp):
    return slice(pp * SLAB, (pp + 1) * SLAB)


def _blk(j):
    return pl.ds(pl.multiple_of(j * BLK, BLK), BLK)


def _causal_t():
    return _row((BLK, 2 * BLK)) < (_lane((BLK, 2 * BLK)) & (BLK - 1))


def _sb_rows(b, j):
    return pl.ds(pl.multiple_of(b * SEQ + j * BLK, BLK), BLK)


SB_CHAINS = [(b, pp) for b in range(B_LOC) for pp in range(4)]


def sb_fwd(qb, kb, vb):
    def body(q_ref, k_ref, v_ref, o_ref, c_ref, vt_ref, ot_ref):
        for c, (b, pp) in enumerate(SB_CHAINS):
            for j in range(N_BLK):
                vt_ref[c, j] = v_ref[b * SEQ + j * BLK:b * SEQ + (j + 1) * BLK, _slab(pp)].T
        lane = _lane((BLK, SLAB))
        tri = _tri(True)
        valid = _causal_t()
        jrow = _row((N_BLK, 2 * BLK))
        chains = range(len(SB_CHAINS))

        def q_block(i, _):
            q2 = [_stack_heads(q_ref[_sb_rows(b, i), _slab(pp)], lane) for b, pp in SB_CHAINS]

            def key_block(j, carry, mask, first):
                zt = [_dot(k_ref[_sb_rows(b, j), _slab(pp)], q2[c], NT) for c, (b, pp) in enumerate(SB_CHAINS)]
                sp = [_sb_softplus(zt[c], mask) for c in chains]
                sums = [_key_sums(tri, sp[c]) for c in chains]
                w = [_sb_weights(zt[c], sp[c], sums[c][0] + carry[c], mask) for c in chains]
                for c in chains:
                    pv = _dot(vt_ref[c, j], w[c].astype(BF16))
                    if first:
                        ot_ref[c] = pv
                    else:
                        ot_ref[c] += pv
                return tuple(carry[c] + sums[c][1] for c in chains)

            def earlier(jj, state):
                carry, saved = state
                j = i - 1 - jj
                saved = tuple(jnp.where(jrow == j, carry[c], saved[c]) for c in chains)
                return key_block(j, carry, None, False), saved

            zero = tuple(jnp.zeros((1, 2 * BLK), F32) for _ in chains)
            carry = key_block(i, zero, valid, True)
            _, saved = lax.fori_loop(0, i, earlier, (carry, tuple(jnp.zeros((N_BLK, 2 * BLK), F32) for _ in chains)))
            for c, (b, pp) in enumerate(SB_CHAINS):
                o_ref[_sb_rows(b, i), _slab(pp)] = _unstack_heads(ot_ref[c].T, lane)
                c_ref[c * N_BLK + i] = saved[c]
            return 0

        lax.fori_loop(0, N_BLK, q_block, 0)

    n_ch = len(SB_CHAINS)
    vmem = pl.BlockSpec(memory_space=pltpu.VMEM)
    return pl.pallas_call(
        body, name="sb_fwd",
        in_specs=[vmem] * 3, out_specs=[vmem] * 2,
        out_shape=[jax.ShapeDtypeStruct((T_LOC, 512), F32), jax.ShapeDtypeStruct((n_ch * N_BLK, N_BLK, 2 * BLK), F32)],
        scratch_shapes=[pltpu.VMEM((n_ch, N_BLK, SLAB, BLK), BF16), pltpu.VMEM((n_ch, SLAB, 2 * BLK), F32)],
        compiler_params=pltpu.CompilerParams(vmem_limit_bytes=VMEM_LIMIT),
    )(qb, kb, vb)


def out_loss(o_a, o_b, ga, gb, x, target, wout):
    n_tiles = T_LOC // TM_FWD

    def body(oa_ref, ob_ref, ga_ref, gb_ref, x_ref, t_ref, w_ref,
             dout_ref, doa_ref, dob_ref, dga_ref, dgb_ref, dw_ref, loss_ref, acc_ref):
        step = pl.program_id(0)

        @pl.when(step == 0)
        def _():
            acc_ref[...] = jnp.zeros_like(acc_ref)
            loss_ref[...] = jnp.zeros_like(loss_ref)

        oa, ob, gav, gbv = oa_ref[...], ob_ref[...], ga_ref[...], gb_ref[...]
        sa, sb = _sigmoid(gav), _sigmoid(gbv)
        silu_a, silu_b = gav * sa, gbv * sb
        y = jnp.concatenate([oa * silu_a, ob * silu_b], axis=1).astype(BF16)
        err = x_ref[...] + _dot(y, w_ref[...]) - t_ref[...]
        e2 = err * err
        part = jnp.sum(e2.reshape(TM_FWD // 8, 8, D_MODEL), axis=0)
        loss_ref[...] += functools.reduce(lambda a, b: a + b, [part[:, k * 128:(k + 1) * 128] for k in range(8)])
        dout = err * (1.0 / D_MODEL)
        dout_ref[...] = dout
        dob16 = dout.astype(BF16)
        for r0 in range(0, D_MODEL, ACC_ROWS):
            acc_ref[r0:r0 + ACC_ROWS, :] += _dot(y[:, r0:r0 + ACC_ROWS], dob16, TN)
        dy = _dot(dob16, w_ref[...], NT)
        dya, dyb = dy[:, :512], dy[:, 512:]
        doa_ref[...] = (dya * silu_a).astype(BF16)
        dob_ref[...] = (dyb * silu_b).astype(BF16)
        dga_ref[...] = (dya * oa * (sa * (1.0 + gav * (1.0 - sa)))).astype(BF16)
        dgb_ref[...] = (dyb * ob * (sb * (1.0 + gbv * (1.0 - sb)))).astype(BF16)

        @pl.when(step == n_tiles - 1)
        def _():
            dw_ref[...] = acc_ref[...].astype(BF16)

    def tile(w):
        return pl.BlockSpec((TM_FWD, w), lambda i: (i, 0))

    const = lambda i: (0, 0)
    return pl.pallas_call(
        body, name="out_loss", grid=(n_tiles,),
        in_specs=[tile(512)] * 4 + [tile(D_MODEL)] * 2 + [pl.BlockSpec((D_MODEL, D_MODEL), const)],
        out_specs=[tile(D_MODEL), tile(512), tile(512), tile(512), tile(512),
                   pl.BlockSpec((D_MODEL, D_MODEL), const), pl.BlockSpec((8, 128), const)],
        out_shape=[jax.ShapeDtypeStruct((T_LOC, D_MODEL), F32)] + [jax.ShapeDtypeStruct((T_LOC, 512), BF16)] * 4
        + [jax.ShapeDtypeStruct((D_MODEL, D_MODEL), BF16), jax.ShapeDtypeStruct((8, 128), F32)],
        scratch_shapes=[pltpu.VMEM((D_MODEL, D_MODEL), F32)],
        compiler_params=_params(("arbitrary",)),
    )(o_a, o_b, ga, gb, x, target, wout)


def swa_bwd(q_rot, k_dup, v_dup, o_a, d_oa, sinks):
    def body(q_ref, kp_ref, kc_ref, vp_ref, vc_ref, o_ref, do_ref, sinks_ref, dq_ref, dk_ref, dv_ref, dsink_ref):
        i = pl.program_id(0)

        @pl.when(i == 0)
        def _():
            dk_ref[...] = jnp.zeros_like(dk_ref)
            dv_ref[...] = jnp.zeros_like(dv_ref)
            dsink_ref[...] = jnp.zeros_like(dsink_ref)

        lane = _lane((BLK, SLAB))
        rows_prev, rows_cur = _blk(jnp.maximum(i - 1, 0)), _blk(i)
        chains = range(len(SWA_CHAINS))
        q2 = [_stack_heads(q_ref[b, :, _slab(p)], lane) for b, p in SWA_CHAINS]
        do2 = [_stack_heads(do_ref[b, :, _slab(p)], lane) for b, p in SWA_CHAINS]
        keys = [_swa_window(kp_ref.at[b], kc_ref.at[b], p) for b, p in SWA_CHAINS]
        s = [_dot(q2[c], keys[c], NT) for c in chains]
        dp = [_dot(do2[c], _swa_window(vp_ref.at[b], vc_ref.at[b], p), NT) for c, (b, p) in enumerate(SWA_CHAINS)]
        ds, pn16, cols = [], [], []
        for c, (b, p) in enumerate(SWA_CHAINS):
            pn, p_sink = _swa_probs(s[c], sinks_ref, p, i)
            o = o_ref[b, :, _slab(p)]
            delta = jnp.sum(do2[c].astype(F32) * jnp.concatenate([o, o], axis=0), axis=-1, keepdims=True)
            ds.append((pn * (dp[c] - delta)).astype(BF16))
            pn16.append(pn.astype(BF16))
            cols.append(-p_sink * delta)
        for c, (b, p) in enumerate(SWA_CHAINS):
            dq_ref[b, :, _slab(p)] = _unstack_heads(_dot(ds[c], keys[c]), lane) * Q_SCALE
        dk2 = [_dot(ds[c], q2[c], TN) for c in chains]
        dv2 = [_dot(pn16[c], do2[c], TN) for c in chains]
        for c, (b, p) in enumerate(SWA_CHAINS):
            gsl = _slab(p // 2)
            dk_ref[b, rows_prev, gsl] += dk2[c][:BLK]
            dk_ref[b, rows_cur, gsl] += dk2[c][BLK:]
            dv_ref[b, rows_prev, gsl] += dv2[c][:BLK]
            dv_ref[b, rows_cur, gsl] += dv2[c][BLK:]
            for e in range(2):
                dsink_ref[2 * p + e:2 * p + e + 1, :] += jnp.sum(cols[c][e * BLK:(e + 1) * BLK], axis=0, keepdims=True)

    cur, prev = _swa_specs()
    whole = pl.BlockSpec((B_LOC, SEQ, 256), lambda i: (0, 0, 0))
    q3, k3, v3, o3, do3 = (a.reshape(B_LOC, SEQ, a.shape[1]) for a in (q_rot, k_dup, v_dup, o_a, d_oa))
    dq, dk, dv, dsink = pl.pallas_call(
        body, name="swa_bwd", grid=(N_BLK,),
        in_specs=[cur(512), prev(256), cur(256), prev(256), cur(256), cur(512), cur(512),
                  pl.BlockSpec(memory_space=pltpu.SMEM)],
        out_specs=[cur(512), whole, whole, pl.BlockSpec((8, 128), lambda i: (0, 0))],
        out_shape=[jax.ShapeDtypeStruct((B_LOC, SEQ, 512), F32), jax.ShapeDtypeStruct((B_LOC, SEQ, 256), F32),
                   jax.ShapeDtypeStruct((B_LOC, SEQ, 256), F32), jax.ShapeDtypeStruct((8, 128), F32)],
        compiler_params=_params(("arbitrary",)),
    )(q3, k3, k3, v3, v3, o3, do3, sinks)
    return dq.reshape(T_LOC, 512), dk.reshape(T_LOC, 256), dv.reshape(T_LOC, 256), dsink


def sb_bwd(qb, kb, vb, d_ob, carries, dwout):
    def body(q_ref, k_ref, v_ref, do_ref, c_ref, dw_hbm, dq_ref, dk_ref, dv_ref, rw_hbm, kt_ref, dqt_ref,
             rw_send, rw_recv, rw_local):
        start_dwout, finish_dwout = _direct_exchange(
            lambda dev: dw_hbm.at[pl.ds(_lin(dev) * OUT_SHARD, OUT_SHARD), :], lambda dev: rw_hbm.at[_lin(dev)],
            rw_send, rw_recv, rw_local)
        start_dwout()
        for c, (b, pp) in enumerate(SB_CHAINS):
            for j in range(N_BLK):
                kt_ref[c, j] = k_ref[b * SEQ + j * BLK:b * SEQ + (j + 1) * BLK, _slab(pp)].T
        dk_ref[...] = jnp.zeros_like(dk_ref)
        dv_ref[...] = jnp.zeros_like(dv_ref)
        dqt_ref[...] = jnp.zeros_like(dqt_ref)
        lane = _lane((BLK, SLAB))
        tri_after, tri_before = _tri(True), _tri(False)
        valid = _causal_t()
        jrow = _row((N_BLK, 2 * BLK))
        chains = range(len(SB_CHAINS))

        def q_block(i, _):
            q2 = [_stack_heads(q_ref[_sb_rows(b, i), _slab(pp)], lane) for b, pp in SB_CHAINS]
            do2 = [_stack_heads(do_ref[_sb_rows(b, i), _slab(pp)], lane) for b, pp in SB_CHAINS]

            def key_block(j, carry_sp, before_u, mask):
                at = [(_sb_rows(b, j), _slab(pp)) for b, pp in SB_CHAINS]
                zt = [_dot(k_ref[at[c]], q2[c], NT) for c in chains]
                dw = [_dot(v_ref[at[c]], do2[c], NT) for c in chains]
                sp = [_sb_softplus(zt[c], mask) for c in chains]
                later = [_key_sums(tri_after, sp[c])[0] for c in chains]
                w = [_sb_weights(zt[c], sp[c], later[c] + carry_sp[c], mask) for c in chains]
                u = [dw[c] * w[c] for c in chains]
                for c in chains:
                    dv_ref[at[c]] += _dot(w[c].astype(BF16), do2[c])
                sums = [_key_sums(tri_before, u[c]) for c in chains]
                dz16 = []
                for c in chains:
                    sig = jnp.exp(zt[c] - sp[c])
                    dz = u[c] - sig * (u[c] + before_u[c] + sums[c][0])
                    if mask is not None:
                        dz = jnp.where(mask, dz, 0.0)
                    dz16.append(dz.astype(BF16))
                for c in chains:
                    dk_ref[at[c]] += _dot(dz16[c], q2[c])
                    dqt_ref[c] += _dot(kt_ref[c, j], dz16[c])
                return tuple(before_u[c] + sums[c][1] for c in chains)

            def earlier(j, before_u):
                carry_sp = [jnp.sum(jnp.where(jrow == j, c_ref[c * N_BLK + i], 0.0), axis=0, keepdims=True)
                            for c in chains]
                return key_block(j, carry_sp, before_u, None)

            zero = tuple(jnp.zeros((1, 2 * BLK), F32) for _ in chains)
            before_u = lax.fori_loop(0, i, earlier, zero)
            key_block(i, zero, before_u, valid)
            for c, (b, pp) in enumerate(SB_CHAINS):
                dq_ref[_sb_rows(b, i), _slab(pp)] = (_unstack_heads(dqt_ref[c].T, lane) * Q_SCALE).astype(BF16)
                dqt_ref[c] = jnp.zeros((SLAB, 2 * BLK), F32)
            return 0

        lax.fori_loop(0, N_BLK, q_block, 0)
        finish_dwout()

    n_ch = len(SB_CHAINS)
    vmem, hbm = pl.BlockSpec(memory_space=pltpu.VMEM), pl.BlockSpec(memory_space=pl.ANY)
    return pl.pallas_call(
        body, name="sb_bwd",
        in_specs=[vmem] * 5 + [hbm], out_specs=[vmem] * 3 + [hbm],
        out_shape=[jax.ShapeDtypeStruct((T_LOC, 512), BF16)] + [jax.ShapeDtypeStruct((T_LOC, 512), F32)] * 2
        + [jax.ShapeDtypeStruct((N_DEV, OUT_SHARD, D_MODEL), BF16)],
        scratch_shapes=[pltpu.VMEM((n_ch, N_BLK, SLAB, BLK), BF16), pltpu.VMEM((n_ch, SLAB, 2 * BLK), F32),
                        pltpu.SemaphoreType.DMA((7,)), pltpu.SemaphoreType.DMA((7,)), pltpu.SemaphoreType.DMA],
        compiler_params=pltpu.CompilerParams(vmem_limit_bytes=VMEM_LIMIT),
    )(qb, kb, vb, d_ob, carries, dwout)


def bwd_in(x, dout, norm_gain, win_t, dq_rot, dk_dup, dv_dup, qa_raw, ka_raw, cos, sin_s, q_gain2, k_gain2,
           dga, dgb, dqb, dkb, dvb):
    n_tiles = T_LOC // TM

    def body(x_ref, dout_ref, ng_ref, w_hbm, dq_ref, dk_ref, dv_ref, qa_ref, ka_ref, cos_ref, sin_ref, qg_ref, kg_ref,
             dga_ref, dgb_ref, dqb_ref, dkb_ref, dvb_ref,
             gx_ref, dw_hbm, dng_ref, dqg_ref, dkg_ref, w_ref, acc_ref, stage_ref, w_sem):
        step = pl.program_id(0)

        @pl.when(step == 0)
        def _():
            cp = pltpu.make_async_copy(w_hbm, w_ref, w_sem)
            cp.start()
            acc_ref[...] = jnp.zeros_like(acc_ref)
            dng_ref[...] = jnp.zeros_like(dng_ref)
            dqg_ref[...] = jnp.zeros_like(dqg_ref)
            dkg_ref[...] = jnp.zeros_like(dkg_ref)
            cp.wait()

        lane = _lane((TM, SLAB))
        bd = _head_blockdiag()
        cos, sin_s = cos_ref[...], sin_ref[...]

        def norm_rope_bwd(d_rot, raw, gain2):
            dy = d_rot * cos + _swap_half(d_rot * sin_s, lane)
            r = lax.rsqrt(_head_sum(raw * raw, bd) * (1.0 / HEAD_DIM) + EPS)
            xhat = raw * r
            dgain = jnp.sum(dy * xhat, axis=0, keepdims=True)
            dxh = dy * gain2
            mean = _head_sum(dxh * xhat, bd) * (1.0 / HEAD_DIM)
            return r * (dxh - xhat * mean), dgain

        def fold_dup(d_dup):
            a, b2 = d_dup[:, :SLAB], d_dup[:, SLAB:]
            return jnp.where(lane < HEAD_DIM, a + pltpu.roll(a, HEAD_DIM, 1), b2 + pltpu.roll(b2, HEAD_DIM, 1))

        pieces = []
        dqg = jnp.zeros((1, SLAB), F32)
        for p in range(4):
            sl = slice(p * SLAB, (p + 1) * SLAB)
            d_raw, dg = norm_rope_bwd(dq_ref[:, sl], qa_ref[:, sl], qg_ref[...])
            pieces.append(d_raw.astype(BF16))
            dqg = dqg + dg
        d_raw, dkg = norm_rope_bwd(fold_dup(dk_ref[...]), ka_ref[...], kg_ref[...])
        pieces.append(d_raw.astype(BF16))
        pieces.append(fold_dup(dv_ref[...]).astype(BF16))
        pieces += [dga_ref[...], dqb_ref[...], dkb_ref[...].astype(BF16), dvb_ref[...].astype(BF16),
                   dgb_ref[...]]
        dproj = jnp.concatenate(pieces, axis=1)
        dqg_ref[0:1, :] += dqg + pltpu.roll(dqg, HEAD_DIM, 1)
        dkg_ref[0:1, :] += dkg + pltpu.roll(dkg, HEAD_DIM, 1)

        xv = x_ref[...]
        rstd = lax.rsqrt(jnp.mean(xv * xv, axis=-1, keepdims=True) + EPS)
        xhat = xv * rstd
        gain = ng_ref[...]
        h = (xhat * gain).astype(BF16)
        for r0 in range(0, IN_WIDTH, ACC_ROWS):
            acc_ref[r0:r0 + ACC_ROWS, :] += _dot(dproj[:, r0:r0 + ACC_ROWS], h, TN)
        dh = _dot(dproj, w_ref[...])
        dng_ref[0:1, :] += jnp.sum(dh * xhat, axis=0, keepdims=True)
        dxh = dh * gain
        gx_ref[...] = dout_ref[...] + rstd * (dxh - xhat * jnp.mean(dxh * xhat, axis=-1, keepdims=True))

        @pl.when(step == n_tiles - 1)
        def _():
            for r0 in range(0, IN_WIDTH, ACC_ROWS):
                stage_ref[...] = acc_ref[r0:r0 + ACC_ROWS, :].astype(BF16)
                pltpu.sync_copy(stage_ref, dw_hbm.at[r0:r0 + ACC_ROWS, :])

    def tile(w):
        return pl.BlockSpec((TM, w), lambda i: (i, 0))

    def whole(a):
        return pl.BlockSpec(a.shape, lambda i: (0, 0))

    const = lambda i: (0, 0)
    return pl.pallas_call(
        body, name="bwd_in", grid=(n_tiles,),
        in_specs=[tile(D_MODEL), tile(D_MODEL), whole(norm_gain), pl.BlockSpec(memory_space=pl.ANY),
                  tile(512), tile(256), tile(256), tile(512), tile(128), tile(128), tile(128),
                  whole(q_gain2), whole(k_gain2), tile(512), tile(512), tile(512), tile(512), tile(512)],
        out_specs=[tile(D_MODEL), pl.BlockSpec(memory_space=pl.ANY), pl.BlockSpec((8, D_MODEL), const),
                   pl.BlockSpec((8, SLAB), const), pl.BlockSpec((8, SLAB), const)],
        out_shape=[jax.ShapeDtypeStruct((T_LOC, D_MODEL), F32), jax.ShapeDtypeStruct((IN_WIDTH, D_MODEL), BF16),
                   jax.ShapeDtypeStruct((8, D_MODEL), F32), jax.ShapeDtypeStruct((8, SLAB), F32),
                   jax.ShapeDtypeStruct((8, SLAB), F32)],
        scratch_shapes=[pltpu.VMEM((IN_WIDTH, D_MODEL), BF16), pltpu.VMEM((IN_WIDTH, D_MODEL), F32),
                        pltpu.VMEM((ACC_ROWS, D_MODEL), BF16), pltpu.SemaphoreType.DMA],
        compiler_params=_params(("arbitrary",)),
    )(x, dout, norm_gain, win_t, dq_rot, dk_dup, dv_dup, qa_raw, ka_raw, cos, sin_s, q_gain2, k_gain2,
      dga, dgb, dqb, dkb, dvb)


def _adamw(w, g, m, v):
    m = ADAM_B1 * m + (1.0 - ADAM_B1) * g
    v = ADAM_B2 * v + (1.0 - ADAM_B2) * (g * g)
    m_hat = m / (1.0 - ADAM_B1 ** ADAM_STEP)
    v_hat = v / (1.0 - ADAM_B2 ** ADAM_STEP)
    delta = -ADAM_LR * (m_hat / (jnp.sqrt(v_hat) + ADAM_EPS) + ADAM_WD * w)
    return delta, m, v


def _sum_slots(r_ref):
    g = r_ref[0].astype(F32)
    for s in range(1, r_ref.shape[0]):
        g = g + r_ref[s].astype(F32)
    return g


def adamw_rows(name, recv, w, m, v):
    def body(r_ref, w_ref, m_ref, v_ref, g_ref, d_ref, nm_ref, nv_ref):
        g = _sum_slots(r_ref)
        g_ref[...] = g
        d_ref[...], nm_ref[...], nv_ref[...] = _adamw(w_ref[...], g, m_ref[...], v_ref[...])

    return pl.pallas_call(
        body, name=name,
        out_shape=[jax.ShapeDtypeStruct(w.shape, F32)] * 4,
        compiler_params=pltpu.CompilerParams(vmem_limit_bytes=VMEM_LIMIT),
    )(recv, w, m, v)


def adamw_small(recv, weights, moments_m, moments_v):
    n = len(weights)

    def body(r_ref, *refs):
        ins, outs = refs[:3 * n], refs[3 * n:]
        s = _sum_slots(r_ref)
        eye = (_row((8, SLAB)) == _lane((8, SLAB))).astype(F32)
        sinks = jnp.sum(s[:, 1280:1408] * eye, axis=0, keepdims=True)
        grads = [s[0:1, :D_MODEL], s[0:1, 1024:1024 + HEAD_DIM], s[0:1, 1152:1152 + HEAD_DIM], sinks[:, :8]]
        for k in range(n):
            outs[k][...] = grads[k]
            outs[n + k][...], outs[2 * n + k][...], outs[3 * n + k][...] = _adamw(
                ins[k][...], grads[k], ins[n + k][...], ins[2 * n + k][...])
        loss = jnp.sum(jnp.sum(s[:, 1408:1536], axis=1, keepdims=True), axis=0, keepdims=True) * (0.5 / D_MODEL)
        outs[4 * n][...] = loss

    res = pl.pallas_call(
        body, name="adamw_small",
        out_shape=[jax.ShapeDtypeStruct(w.shape, F32) for w in weights] * 4 + [jax.ShapeDtypeStruct((1, 1), F32)],
        compiler_params=pltpu.CompilerParams(vmem_limit_bytes=VMEM_LIMIT),
    )(recv, *weights, *moments_m, *moments_v)
    return res[:n], res[n:2 * n], res[2 * n:3 * n], res[3 * n:4 * n], res[4 * n]


def kernel(x, positions, norm_gain, w_in, q_norm_gain, k_norm_gain, sinks, w_out, loss_target, m_norm_gain, m_w_in, m_q_norm_gain, m_k_norm_gain, m_sinks, m_w_out, v_norm_gain, v_w_in, v_q_norm_gain, v_k_norm_gain, v_sinks, v_w_out):
    x2 = x.reshape(T_LOC, D_MODEL)
    tgt2 = loss_target.reshape(T_LOC, D_MODEL)
    pos2 = positions.reshape(T_LOC, 1)
    half = HEAD_DIM // 2
    inv_freq = ROPE_THETA ** (-jnp.arange(half, dtype=F32) * 2.0 / HEAD_DIM)
    inv_freq = jnp.tile(inv_freq, SLAB // half).reshape(1, SLAB)
    sin_sign = jnp.tile(jnp.concatenate([-jnp.ones((half,), F32), jnp.ones((half,), F32)]), 2).reshape(1, SLAB)
    q_gain2 = jnp.tile(q_norm_gain, (1, 2))
    k_gain2 = jnp.tile(k_norm_gain, (1, 2))

    win_t = gather_weights(w_in.reshape(D_MODEL, IN_SHARD).T.astype(BF16))

    (qa_raw, ka_raw, q_rot, k_dup, v_dup, ga, qb, kb, vb, gb, cos, sin_s, wout) = fwd_proj(
        x2, pos2, norm_gain, win_t, inv_freq, sin_sign, q_gain2, k_gain2, w_out.reshape(OUT_SHARD, D_MODEL).astype(BF16))
    o_a = swa_fwd(q_rot, k_dup, v_dup, sinks)
    o_b, carries = sb_fwd(qb, kb, vb)
    dout, d_oa, d_ob, dga, dgb, dwout, loss_part = out_loss(o_a, o_b, ga, gb, x2, tgt2, wout)
    dq_rot, dk_dup, dv_dup, dsink = swa_bwd(q_rot, k_dup, v_dup, o_a, d_oa, sinks)
    dqb, dkb, dvb, r_out = sb_bwd(qb, kb, vb, d_ob, carries, dwout)
    grad_x, dwin_t, dng, dqg, dkg = bwd_in(
        x2, dout, norm_gain, win_t, dq_rot, dk_dup, dv_dup, qa_raw, ka_raw, cos, sin_s, q_gain2, k_gain2,
        dga, dgb, dqb, dkb, dvb)

    small = jnp.concatenate([dng, dqg, dkg, dsink, loss_part], axis=1)
    r_win, r_small = exchange_grads(dwin_t, small)

    w_in2, m_in2, v_in2 = (a.reshape(D_MODEL, IN_SHARD).T for a in (w_in, m_w_in, v_w_in))
    w_out2, m_out2, v_out2 = (a.reshape(OUT_SHARD, D_MODEL) for a in (w_out, m_w_out, v_w_out))
    big_in = adamw_rows("adamw_w_in", r_win, w_in2, m_in2, v_in2)
    big_out = adamw_rows("adamw_w_out", r_out, w_out2, m_out2, v_out2)
    *small_out, loss = adamw_small(
        r_small, (norm_gain, q_norm_gain, k_norm_gain, sinks),
        (m_norm_gain, m_q_norm_gain, m_k_norm_gain, m_sinks), (v_norm_gain, v_q_norm_gain, v_k_norm_gain, v_sinks))

    def leaves(k):
        ng, qg, kg, sk = small_out[k]
        return (ng, big_in[k].T.reshape(1, D_MODEL, IN_SHARD), qg, kg, sk, big_out[k].reshape(1, OUT_SHARD, D_MODEL))

    return (loss.reshape(()), grad_x.reshape(B_LOC, SEQ, D_MODEL), *leaves(0), *leaves(1), *leaves(2), *leaves(3))
```

```python
import functools

import jax
import jax.numpy as jnp
from jax import lax
from jax.experimental import pallas as pl
from jax.experimental.pallas import tpu as pltpu

F32 = jnp.float32
BF16 = jnp.bfloat16

N_DEV = 8
D_MODEL = 1024
SEQ = 2048
B_LOC = 2
T_LOC = B_LOC * SEQ
HEAD_DIM = 64
HEAD_SHIFT = 6
BLK = 128
N_BLK = SEQ // BLK
SLAB = 128
IN_WIDTH = 3328
IN_SHARD = IN_WIDTH // N_DEV
OUT_SHARD = D_MODEL // N_DEV
EPS = 1e-6
ROPE_THETA = 10000.0
Q_SCALE = 0.125
R_QA, R_KA, R_VA, R_GA, R_QB, R_KB, R_VB, R_GB, R_END = 0, 512, 640, 768, 1280, 1792, 2304, 2816, 3328
SMALL_W = 1536
ADAM_LR, ADAM_B1, ADAM_B2, ADAM_EPS, ADAM_WD, ADAM_STEP = 0.001, 0.9, 0.999, 1e-08, 0.01, 10
TM = 256
TM_DW = 512
TM_FWD = 512
ACC_ROWS = 256
VMEM_LIMIT = 56 * 1024 * 1024

MESH = pl.DeviceIdType.MESH
NT = (((1,), (1,)), ((), ()))
TN = (((0,), (0,)), ((), ()))


def _params(sem, limit=VMEM_LIMIT):
    return pltpu.CompilerParams(dimension_semantics=sem, vmem_limit_bytes=limit)


def _dot(a, b, dims=None):
    if dims is None:
        return jnp.dot(a, b, preferred_element_type=F32)
    return lax.dot_general(a, b, dims, preferred_element_type=F32)


def _split(x):
    hi = x.astype(BF16)
    return hi, (x - hi.astype(F32)).astype(BF16)


def _lane(shape):
    return lax.broadcasted_iota(jnp.int32, shape, len(shape) - 1)


def _row(shape):
    return lax.broadcasted_iota(jnp.int32, shape, 0)


def _head_blockdiag():
    return ((_row((SLAB, SLAB)) >> HEAD_SHIFT) == (_lane((SLAB, SLAB)) >> HEAD_SHIFT)).astype(BF16)


def _head_sum(x, bd):
    hi, lo = _split(x)
    return _dot(hi, bd) + _dot(lo, bd)


def _swap_half(y, lane):
    return jnp.where((lane & 32) != 0, pltpu.roll(y, 32, 1), pltpu.roll(y, 96, 1))


def _stack_heads(q, lane):
    zero = jnp.zeros_like(q)
    return jnp.concatenate([jnp.where(lane < HEAD_DIM, q, zero), jnp.where(lane >= HEAD_DIM, q, zero)], axis=0)


def _unstack_heads(x2, lane):
    return jnp.where(lane < HEAD_DIM, x2[:BLK], x2[BLK:])


def _sigmoid(x):
    return 1.0 / (1.0 + jnp.exp(-x))


def _mesh_pos():
    return lax.axis_index("x"), lax.axis_index("y"), lax.axis_index("c")


def _flip(pos, mask):
    return tuple(1 - p if m else p for p, m in zip(pos, mask))


def _lin(pos):
    return 4 * pos[0] + 2 * pos[1] + pos[2]


DEV_FLIPS = [(fx, fy, fc) for fx in (0, 1) for fy in (0, 1) for fc in (0, 1)][1:]


def _direct_exchange(src_for, dst_slot, send_sems, recv_sems, local_sem):
    me = _mesh_pos()

    def copy(k, to):
        return pltpu.make_async_remote_copy(
            src_ref=src_for(to), dst_ref=dst_slot(me), send_sem=send_sems.at[k], recv_sem=recv_sems.at[k],
            device_id=to, device_id_type=MESH)

    def landed(k, frm):
        return pltpu.make_async_remote_copy(
            src_ref=src_for(frm), dst_ref=dst_slot(frm), send_sem=send_sems.at[k], recv_sem=recv_sems.at[k],
            device_id=frm, device_id_type=MESH)

    local = None if local_sem is None else pltpu.make_async_copy(src_for(me), dst_slot(me), local_sem)
    peers = [_flip(me, f) for f in DEV_FLIPS]

    def start():
        if local is not None:
            local.start()
        for k, to in enumerate(peers):
            copy(k, to).start()

    def finish():
        for k, frm in enumerate(peers):
            landed(k, frm).wait_recv()
        for k, to in enumerate(peers):
            copy(k, to).wait_send()
        if local is not None:
            local.wait()

    return start, finish


def gather_weights(shard):
    m = shard.shape[0]

    def body(a_ref, o_ref, ici_send, ici_recv, d2d_send, d2d_recv, local_sem):
        x, y, c = _mesh_pos()
        me, sibling = (x, y, c), (x, y, 1 - c)
        chip_x, chip_y, chip_d = (1 - x, y), (x, 1 - y), (1 - x, 1 - y)

        def rows(pos):
            return o_ref.at[pl.ds(_lin(pos) * m, m), :]

        def ici(k, block, to, src=None):
            return pltpu.make_async_remote_copy(
                src_ref=rows(block) if src is None else src, dst_ref=rows(block),
                send_sem=ici_send.at[k], recv_sem=ici_recv.at[k], device_id=to, device_id_type=MESH)

        def d2d(k, chip, mine, src=None):
            block = (*chip, c) if mine else (*chip, 1 - c)
            return pltpu.make_async_remote_copy(
                src_ref=rows(block) if src is None else src, dst_ref=rows(block),
                send_sem=d2d_send.at[k], recv_sem=d2d_recv.at[k], device_id=sibling, device_id_type=MESH)

        local = pltpu.make_async_copy(a_ref, rows(me), local_sem)
        local.start()
        sends = [ici(0, me, (*chip_x, c), src=a_ref), ici(1, me, (*chip_y, c), src=a_ref),
                 d2d(0, (x, y), True, src=a_ref)]
        for cp in sends:
            cp.start()

        def pass_on(first, k_first, second, k_second, onward):
            ici(k_first, (*first, c), me).wait_recv()
            relay = ici(2, (*first, c), (*onward, c))
            relay.start()
            hand = [d2d(1 + k_first, first, True)]
            hand[0].start()
            ici(k_second, (*second, c), me).wait_recv()
            hand.append(d2d(1 + k_second, second, True))
            hand[1].start()
            ici(2, (*chip_d, c), me).wait_recv()
            hand.append(d2d(3, chip_d, True))
            hand[2].start()
            for cp in [relay] + hand:
                cp.wait_send()

        @pl.when(c == 0)
        def _():
            pass_on(chip_y, 1, chip_x, 0, chip_x)

        @pl.when(c == 1)
        def _():
            pass_on(chip_x, 0, chip_y, 1, chip_y)

        for k, chip in enumerate([(x, y), chip_x, chip_y, chip_d]):
            d2d(k, chip, False).wait_recv()
        for cp in sends:
            cp.wait_send()
        local.wait()

    vmem = pl.BlockSpec(memory_space=pltpu.VMEM)
    return pl.pallas_call(
        body, name="gather_weights",
        out_shape=jax.ShapeDtypeStruct((N_DEV * m, shard.shape[1]), shard.dtype),
        in_specs=[vmem], out_specs=vmem,
        scratch_shapes=[pltpu.SemaphoreType.DMA((3,)), pltpu.SemaphoreType.DMA((3,)),
                        pltpu.SemaphoreType.DMA((4,)), pltpu.SemaphoreType.DMA((4,)), pltpu.SemaphoreType.DMA],
        compiler_params=pltpu.CompilerParams(vmem_limit_bytes=VMEM_LIMIT),
    )(shard)


def _norm_rope(xs, gain2, cos, sin_s, bd, lane):
    r = lax.rsqrt(_head_sum(xs * xs, bd) * (1.0 / HEAD_DIM) + EPS)
    y = xs * r * gain2
    return y * cos + _swap_half(y, lane) * sin_s


def _dup_heads(xs, lane):
    r = pltpu.roll(xs, HEAD_DIM, 1)
    lo = lane < HEAD_DIM
    return jnp.concatenate([jnp.where(lo, xs, r), jnp.where(lo, r, xs)], axis=1)


def fwd_proj(x, pos, norm_gain, win_t, inv_freq, sin_sign, q_gain2, k_gain2, wout_shard):
    n_tiles = T_LOC // TM_FWD

    def body(x_ref, pos_ref, ng_ref, w_ref, if_ref, sg_ref, qg_ref, kg_ref, ws_hbm,
             qa_raw_ref, ka_raw_ref, q_rot_ref, k_dup_ref, v_dup_ref, ga_ref, qb_ref, kb_ref, vb_ref, gb_ref,
             cos_ref, sin_ref, wo_hbm, wo_send, wo_recv, wo_local):
        start_wout, finish_wout = _direct_exchange(
            lambda dev: ws_hbm, lambda dev: wo_hbm.at[pl.ds(_lin(dev) * OUT_SHARD, OUT_SHARD), :],
            wo_send, wo_recv, wo_local)
        pl.when(pl.program_id(0) == 0)(start_wout)

        xv = x_ref[...]
        rstd = lax.rsqrt(jnp.mean(xv * xv, axis=-1, keepdims=True) + EPS)
        h = (xv * rstd * ng_ref[...]).astype(BF16)

        def proj(r0, r1):
            return _dot(h, w_ref[r0:r1, :], NT)

        ang = pos_ref[...].astype(F32) * if_ref[...]
        cos = jnp.cos(ang)
        sin_s = jnp.sin(ang) * sg_ref[...]
        cos_ref[...] = cos
        sin_ref[...] = sin_s
        lane = _lane((TM_FWD, SLAB))
        bd = _head_blockdiag()

        qa = proj(R_QA, R_KA)
        qa_raw_ref[...] = qa
        for p in range(4):
            sl = slice(p * SLAB, (p + 1) * SLAB)
            q_rot_ref[:, sl] = (_norm_rope(qa[:, sl], qg_ref[...], cos, sin_s, bd, lane) * Q_SCALE).astype(BF16)
        ka = proj(R_KA, R_VA)
        ka_raw_ref[...] = ka
        k_dup_ref[...] = _dup_heads(_norm_rope(ka, kg_ref[...], cos, sin_s, bd, lane), lane).astype(BF16)
        v_dup_ref[...] = _dup_heads(proj(R_VA, R_GA), lane).astype(BF16)
        ga_ref[...] = proj(R_GA, R_QB)
        qb_ref[...] = (proj(R_QB, R_KB) * Q_SCALE).astype(BF16)
        kb_ref[...] = proj(R_KB, R_VB).astype(BF16)
        vb_ref[...] = proj(R_VB, R_GB).astype(BF16)
        gb_ref[...] = proj(R_GB, R_END)
        pl.when(pl.program_id(0) == n_tiles - 1)(finish_wout)

    def tile(w):
        return pl.BlockSpec((TM_FWD, w), lambda i: (i, 0))

    def whole(a):
        return pl.BlockSpec(a.shape, lambda i: (0, 0))

    hbm = pl.BlockSpec(memory_space=pl.ANY)
    widths = [(512, F32), (128, F32), (512, BF16), (256, BF16), (256, BF16), (512, F32), (512, BF16), (512, BF16),
              (512, BF16), (512, F32), (128, F32), (128, F32)]
    return pl.pallas_call(
        body, name="fwd_proj", grid=(n_tiles,),
        in_specs=[tile(D_MODEL), tile(1), whole(norm_gain), whole(win_t), whole(inv_freq), whole(sin_sign),
                  whole(q_gain2), whole(k_gain2), hbm],
        out_specs=[tile(w) for w, _ in widths] + [hbm],
        out_shape=[jax.ShapeDtypeStruct((T_LOC, w), dt) for w, dt in widths]
        + [jax.ShapeDtypeStruct((D_MODEL, D_MODEL), BF16)],
        scratch_shapes=[pltpu.SemaphoreType.DMA((7,)), pltpu.SemaphoreType.DMA((7,)), pltpu.SemaphoreType.DMA],
        compiler_params=_params(("arbitrary",)),
    )(x, pos, norm_gain, win_t, inv_freq, sin_sign, q_gain2, k_gain2, wout_shard)


def _swa_window(prev_ref, cur_ref, p):
    gsl = _slab(p // 2)
    return jnp.concatenate([prev_ref[:, gsl], cur_ref[:, gsl]], axis=0)


def _swa_probs(s, sinks_ref, p, i):
    shape = (2 * BLK, 2 * BLK)
    r = _row(shape) & (BLK - 1)
    cidx = _lane(shape)
    valid = (cidx > r) & (cidx <= r + BLK) & ((cidx >= BLK) | (i > 0))
    s = jnp.where(valid, s, -jnp.inf)
    sink = jnp.where(_row((2 * BLK, 1)) < BLK, sinks_ref[0, 2 * p], sinks_ref[0, 2 * p + 1])
    m = jnp.maximum(jnp.max(s, axis=-1, keepdims=True), sink)
    e = jnp.exp(s - m)
    e_sink = jnp.exp(sink - m)
    den = jnp.sum(e, axis=-1, keepdims=True) + e_sink
    return e / den, e_sink / den


SWA_CHAINS = [(b, p) for b in range(B_LOC) for p in range(4)]


def _swa_specs():
    def cur(w):
        return pl.BlockSpec((B_LOC, BLK, w), lambda i: (0, i, 0))

    def prev(w):
        return pl.BlockSpec((B_LOC, BLK, w), lambda i: (0, jnp.maximum(i - 1, 0), 0))

    return cur, prev


def swa_fwd(q_rot, k_dup, v_dup, sinks):
    def body(q_ref, kp_ref, kc_ref, vp_ref, vc_ref, sinks_ref, o_ref):
        i = pl.program_id(0)
        lane = _lane((BLK, SLAB))
        s = [_dot(_stack_heads(q_ref[b, :, _slab(p)], lane), _swa_window(kp_ref.at[b], kc_ref.at[b], p), NT)
             for b, p in SWA_CHAINS]
        pn = [_swa_probs(s[c], sinks_ref, p, i)[0].astype(BF16) for c, (b, p) in enumerate(SWA_CHAINS)]
        for c, (b, p) in enumerate(SWA_CHAINS):
            o_ref[b, :, _slab(p)] = _unstack_heads(_dot(pn[c], _swa_window(vp_ref.at[b], vc_ref.at[b], p)), lane)

    cur, prev = _swa_specs()
    q3, k3, v3 = (a.reshape(B_LOC, SEQ, a.shape[1]) for a in (q_rot, k_dup, v_dup))
    return pl.pallas_call(
        body, name="swa_fwd", grid=(N_BLK,),
        in_specs=[cur(512), prev(256), cur(256), prev(256), cur(256), pl.BlockSpec(memory_space=pltpu.SMEM)],
        out_specs=cur(512),
        out_shape=jax.ShapeDtypeStruct((B_LOC, SEQ, 512), F32),
        compiler_params=_params(("arbitrary",)),
    )(q3, k3, k3, v3, v3, sinks).reshape(T_LOC, 512)


def _tri(suffix):
    r, cidx = _row((BLK + 16, BLK)), _lane((BLK + 16, BLK))
    tri = (cidx > r) if suffix else (cidx < r)
    return (tri | (r >= BLK)).astype(BF16)


def _key_sums(tri, x):
    res = _dot(tri, x.astype(BF16))
    return res[:BLK], res[BLK:BLK + 1]


def _sb_softplus(zt, valid):
    neg_abs = lax.bitcast_convert_type(lax.bitcast_convert_type(zt, jnp.uint32) | jnp.uint32(0x80000000), F32)
    sp = jnp.maximum(zt, 0.0) + jnp.log(1.0 + jnp.exp(neg_abs))
    return sp if valid is None else jnp.where(valid, sp, 0.0)


def _sb_weights(zt, sp, later, valid):
    w = jnp.exp(zt - sp - later)
    return w if valid is None else jnp.where(valid, w, 0.0)


def _slab(pp):
    return slice(pp * SLAB, (pp + 1) * SLAB)


def _blk(j):
    return pl.ds(pl.multiple_of(j * BLK, BLK), BLK)


def _causal_t():
    return _row((BLK, 2 * BLK)) < (_lane((BLK, 2 * BLK)) & (BLK - 1))


def _sb_rows(b, j):
    return pl.ds(pl.multiple_of(b * SEQ + j * BLK, BLK), BLK)


SB_CHAINS = [(b, pp) for b in range(B_LOC) for pp in range(4)]


def sb_fwd(qb, kb, vb):
    def body(q_ref, k_ref, v_ref, o_ref, c_ref, vt_ref, ot_ref):
        for c, (b, pp) in enumerate(SB_CHAINS):
            for j in range(N_BLK):
                vt_ref[c, j] = v_ref[b * SEQ + j * BLK:b * SEQ + (j + 1) * BLK, _slab(pp)].T
        lane = _lane((BLK, SLAB))
        tri = _tri(True)
        valid = _causal_t()
        jrow = _row((N_BLK, 2 * BLK))
        chains = range(len(SB_CHAINS))

        def q_block(i, _):
            q2 = [_stack_heads(q_ref[_sb_rows(b, i), _slab(pp)], lane) for b, pp in SB_CHAINS]

            def key_block(j, carry, mask, first):
                zt = [_dot(k_ref[_sb_rows(b, j), _slab(pp)], q2[c], NT) for c, (b, pp) in enumerate(SB_CHAINS)]
                sp = [_sb_softplus(zt[c], mask) for c in chains]
                sums = [_key_sums(tri, sp[c]) for c in chains]
                w = [_sb_weights(zt[c], sp[c], sums[c][0] + carry[c], mask) for c in chains]
                for c in chains:
                    pv = _dot(vt_ref[c, j], w[c].astype(BF16))
                    if first:
                        ot_ref[c] = pv
                    else:
                        ot_ref[c] += pv
                return tuple(carry[c] + sums[c][1] for c in chains)

            def earlier(jj, state):
                carry, saved = state
                j = i - 1 - jj
                saved = tuple(jnp.where(jrow == j, carry[c], saved[c]) for c in chains)
                return key_block(j, carry, None, False), saved

            zero = tuple(jnp.zeros((1, 2 * BLK), F32) for _ in chains)
            carry = key_block(i, zero, valid, True)
            _, saved = lax.fori_loop(0, i, earlier, (carry, tuple(jnp.zeros((N_BLK, 2 * BLK), F32) for _ in chains)))
            for c, (b, pp) in enumerate(SB_CHAINS):
                o_ref[_sb_rows(b, i), _slab(pp)] = _unstack_heads(ot_ref[c].T, lane)
                c_ref[c * N_BLK + i] = saved[c]
            return 0

        lax.fori_loop(0, N_BLK, q_block, 0)

    n_ch = len(SB_CHAINS)
    vmem = pl.BlockSpec(memory_space=pltpu.VMEM)
    return pl.pallas_call(
        body, name="sb_fwd",
        in_specs=[vmem] * 3, out_specs=[vmem] * 2,
        out_shape=[jax.ShapeDtypeStruct((T_LOC, 512), F32), jax.ShapeDtypeStruct((n_ch * N_BLK, N_BLK, 2 * BLK), F32)],
        scratch_shapes=[pltpu.VMEM((n_ch, N_BLK, SLAB, BLK), BF16), pltpu.VMEM((n_ch, SLAB, 2 * BLK), F32)],
        compiler_params=pltpu.CompilerParams(vmem_limit_bytes=VMEM_LIMIT),
    )(qb, kb, vb)


def out_loss(o_a, o_b, ga, gb, x, target, wout):
    n_tiles = T_LOC // TM_FWD

    def body(oa_ref, ob_ref, ga_ref, gb_ref, x_ref, t_ref, w_ref,
             dout_ref, doa_ref, dob_ref, dga_ref, dgb_ref, dw_ref, loss_ref, acc_ref):
        step = pl.program_id(0)

        @pl.when(step == 0)
        def _():
            acc_ref[...] = jnp.zeros_like(acc_ref)
            loss_ref[...] = jnp.zeros_like(loss_ref)

        oa, ob, gav, gbv = oa_ref[...], ob_ref[...], ga_ref[...], gb_ref[...]
        sa, sb = _sigmoid(gav), _sigmoid(gbv)
        silu_a, silu_b = gav * sa, gbv * sb
        y = jnp.concatenate([oa * silu_a, ob * silu_b], axis=1).astype(BF16)
        err = x_ref[...] + _dot(y, w_ref[...]) - t_ref[...]
        e2 = err * err
        part = jnp.sum(e2.reshape(TM_FWD // 8, 8, D_MODEL), axis=0)
        loss_ref[...] += functools.reduce(lambda a, b: a + b, [part[:, k * 128:(k + 1) * 128] for k in range(8)])
        dout = err * (1.0 / D_MODEL)
        dout_ref[...] = dout
        dob16 = dout.astype(BF16)
        for r0 in range(0, D_MODEL, ACC_ROWS):
            acc_ref[r0:r0 + ACC_ROWS, :] += _dot(y[:, r0:r0 + ACC_ROWS], dob16, TN)
        dy = _dot(dob16, w_ref[...], NT)
        dya, dyb = dy[:, :512], dy[:, 512:]
        doa_ref[...] = (dya * silu_a).astype(BF16)
        dob_ref[...] = (dyb * silu_b).astype(BF16)
        dga_ref[...] = (dya * oa * (sa * (1.0 + gav * (1.0 - sa)))).astype(BF16)
        dgb_ref[...] = (dyb * ob * (sb * (1.0 + gbv * (1.0 - sb)))).astype(BF16)

        @pl.when(step == n_tiles - 1)
        def _():
            dw_ref[...] = acc_ref[...].astype(BF16)

    def tile(w):
        return pl.BlockSpec((TM_FWD, w), lambda i: (i, 0))

    const = lambda i: (0, 0)
    return pl.pallas_call(
        body, name="out_loss", grid=(n_tiles,),
        in_specs=[tile(512)] * 4 + [tile(D_MODEL)] * 2 + [pl.BlockSpec((D_MODEL, D_MODEL), const)],
        out_specs=[tile(D_MODEL), tile(512), tile(512), tile(512), tile(512),
                   pl.BlockSpec((D_MODEL, D_MODEL), const), pl.BlockSpec((8, 128), const)],
        out_shape=[jax.ShapeDtypeStruct((T_LOC, D_MODEL), F32)] + [jax.ShapeDtypeStruct((T_LOC, 512), BF16)] * 4
        + [jax.ShapeDtypeStruct((D_MODEL, D_MODEL), BF16), jax.ShapeDtypeStruct((8, 128), F32)],
        scratch_shapes=[pltpu.VMEM((D_MODEL, D_MODEL), F32)],
        compiler_params=_params(("arbitrary",)),
    )(o_a, o_b, ga, gb, x, target, wout)


def swa_bwd(q_rot, k_dup, v_dup, o_a, d_oa, sinks):
    def body(q_ref, kp_ref, kc_ref, vp_ref, vc_ref, o_ref, do_ref, sinks_ref, dq_ref, dk_ref, dv_ref, dsink_ref):
        i = pl.program_id(0)

        @pl.when(i == 0)
        def _():
            dk_ref[...] = jnp.zeros_like(dk_ref)
            dv_ref[...] = jnp.zeros_like(dv_ref)
            dsink_ref[...] = jnp.zeros_like(dsink_ref)

        lane = _lane((BLK, SLAB))
        rows_prev, rows_cur = _blk(jnp.maximum(i - 1, 0)), _blk(i)
        chains = range(len(SWA_CHAINS))
        q2 = [_stack_heads(q_ref[b, :, _slab(p)], lane) for b, p in SWA_CHAINS]
        do2 = [_stack_heads(do_ref[b, :, _slab(p)], lane) for b, p in SWA_CHAINS]
        keys = [_swa_window(kp_ref.at[b], kc_ref.at[b], p) for b, p in SWA_CHAINS]
        s = [_dot(q2[c], keys[c], NT) for c in chains]
        dp = [_dot(do2[c], _swa_window(vp_ref.at[b], vc_ref.at[b], p), NT) for c, (b, p) in enumerate(SWA_CHAINS)]
        ds, pn16, cols = [], [], []
        for c, (b, p) in enumerate(SWA_CHAINS):
            pn, p_sink = _swa_probs(s[c], sinks_ref, p, i)
            o = o_ref[b, :, _slab(p)]
            delta = jnp.sum(do2[c].astype(F32) * jnp.concatenate([o, o], axis=0), axis=-1, keepdims=True)
            ds.append((pn * (dp[c] - delta)).astype(BF16))
            pn16.append(pn.astype(BF16))
            cols.append(-p_sink * delta)
        for c, (b, p) in enumerate(SWA_CHAINS):
            dq_ref[b, :, _slab(p)] = _unstack_heads(_dot(ds[c], keys[c]), lane) * Q_SCALE
        dk2 = [_dot(ds[c], q2[c], TN) for c in chains]
        dv2 = [_dot(pn16[c], do2[c], TN) for c in chains]
        for c, (b, p) in enumerate(SWA_CHAINS):
            gsl = _slab(p // 2)
            dk_ref[b, rows_prev, gsl] += dk2[c][:BLK]
            dk_ref[b, rows_cur, gsl] += dk2[c][BLK:]
            dv_ref[b, rows_prev, gsl] += dv2[c][:BLK]
            dv_ref[b, rows_cur, gsl] += dv2[c][BLK:]
            for e in range(2):
                dsink_ref[2 * p + e:2 * p + e + 1, :] += jnp.sum(cols[c][e * BLK:(e + 1) * BLK], axis=0, keepdims=True)

    cur, prev = _swa_specs()
    whole = pl.BlockSpec((B_LOC, SEQ, 256), lambda i: (0, 0, 0))
    q3, k3, v3, o3, do3 = (a.reshape(B_LOC, SEQ, a.shape[1]) for a in (q_rot, k_dup, v_dup, o_a, d_oa))
    dq, dk, dv, dsink = pl.pallas_call(
        body, name="swa_bwd", grid=(N_BLK,),
        in_specs=[cur(512), prev(256), cur(256), prev(256), cur(256), cur(512), cur(512),
                  pl.BlockSpec(memory_space=pltpu.SMEM)],
        out_specs=[cur(512), whole, whole, pl.BlockSpec((8, 128), lambda i: (0, 0))],
        out_shape=[jax.ShapeDtypeStruct((B_LOC, SEQ, 512), F32), jax.ShapeDtypeStruct((B_LOC, SEQ, 256), F32),
                   jax.ShapeDtypeStruct((B_LOC, SEQ, 256), F32), jax.ShapeDtypeStruct((8, 128), F32)],
        compiler_params=_params(("arbitrary",)),
    )(q3, k3, k3, v3, v3, o3, do3, sinks)
    return dq.reshape(T_LOC, 512), dk.reshape(T_LOC, 256), dv.reshape(T_LOC, 256), dsink


def sb_bwd(qb, kb, vb, d_ob, carries, dwout):
    def body(q_ref, k_ref, v_ref, do_ref, c_ref, dw_hbm, dq_ref, dk_ref, dv_ref, rw_hbm, kt_ref, dqt_ref,
             rw_send, rw_recv, rw_local):
        start_dwout, finish_dwout = _direct_exchange(
            lambda dev: dw_hbm.at[pl.ds(_lin(dev) * OUT_SHARD, OUT_SHARD), :], lambda dev: rw_hbm.at[_lin(dev)],
            rw_send, rw_recv, rw_local)
        start_dwout()
        for c, (b, pp) in enumerate(SB_CHAINS):
            for j in range(N_BLK):
                kt_ref[c, j] = k_ref[b * SEQ + j * BLK:b * SEQ + (j + 1) * BLK, _slab(pp)].T
        dk_ref[...] = jnp.zeros_like(dk_ref)
        dv_ref[...] = jnp.zeros_like(dv_ref)
        dqt_ref[...] = jnp.zeros_like(dqt_ref)
        lane = _lane((BLK, SLAB))
        tri_after, tri_before = _tri(True), _tri(False)
        valid = _causal_t()
        jrow = _row((N_BLK, 2 * BLK))
        chains = range(len(SB_CHAINS))

        def q_block(i, _):
            q2 = [_stack_heads(q_ref[_sb_rows(b, i), _slab(pp)], lane) for b, pp in SB_CHAINS]
            do2 = [_stack_heads(do_ref[_sb_rows(b, i), _slab(pp)], lane) for b, pp in SB_CHAINS]

            def key_block(j, carry_sp, before_u, mask):
                at = [(_sb_rows(b, j), _slab(pp)) for b, pp in SB_CHAINS]
                zt = [_dot(k_ref[at[c]], q2[c], NT) for c in chains]
                dw = [_dot(v_ref[at[c]], do2[c], NT) for c in chains]
                sp = [_sb_softplus(zt[c], mask) for c in chains]
                later = [_key_sums(tri_after, sp[c])[0] for c in chains]
                w = [_sb_weights(zt[c], sp[c], later[c] + carry_sp[c], mask) for c in chains]
                u = [dw[c] * w[c] for c in chains]
                for c in chains:
                    dv_ref[at[c]] += _dot(w[c].astype(BF16), do2[c])
                sums = [_key_sums(tri_before, u[c]) for c in chains]
                dz16 = []
                for c in chains:
                    sig = jnp.exp(zt[c] - sp[c])
                    dz = u[c] - sig * (u[c] + before_u[c] + sums[c][0])
                    if mask is not None:
                        dz = jnp.where(mask, dz, 0.0)
                    dz16.append(dz.astype(BF16))
                for c in chains:
                    dk_ref[at[c]] += _dot(dz16[c], q2[c])
                    dqt_ref[c] += _dot(kt_ref[c, j], dz16[c])
                return tuple(before_u[c] + sums[c][1] for c in chains)

            def earlier(j, before_u):
                carry_sp = [jnp.sum(jnp.where(jrow == j, c_ref[c * N_BLK + i], 0.0), axis=0, keepdims=True)
                            for c in chains]
                return key_block(j, carry_sp, before_u, None)

            zero = tuple(jnp.zeros((1, 2 * BLK), F32) for _ in chains)
            before_u = lax.fori_loop(0, i, earlier, zero)
            key_block(i, zero, before_u, valid)
            for c, (b, pp) in enumerate(SB_CHAINS):
                dq_ref[_sb_rows(b, i), _slab(pp)] = (_unstack_heads(dqt_ref[c].T, lane) * Q_SCALE).astype(BF16)
                dqt_ref[c] = jnp.zeros((SLAB, 2 * BLK), F32)
            return 0

        lax.fori_loop(0, N_BLK, q_block, 0)
        finish_dwout()

    n_ch = len(SB_CHAINS)
    vmem, hbm = pl.BlockSpec(memory_space=pltpu.VMEM), pl.BlockSpec(memory_space=pl.ANY)
    return pl.pallas_call(
        body, name="sb_bwd",
        in_specs=[vmem] * 5 + [hbm], out_specs=[vmem] * 3 + [hbm],
        out_shape=[jax.ShapeDtypeStruct((T_LOC, 512), BF16)] + [jax.ShapeDtypeStruct((T_LOC, 512), F32)] * 2
        + [jax.ShapeDtypeStruct((N_DEV, OUT_SHARD, D_MODEL), BF16)],
        scratch_shapes=[pltpu.VMEM((n_ch, N_BLK, SLAB, BLK), BF16), pltpu.VMEM((n_ch, SLAB, 2 * BLK), F32),
                        pltpu.SemaphoreType.DMA((7,)), pltpu.SemaphoreType.DMA((7,)), pltpu.SemaphoreType.DMA],
        compiler_params=pltpu.CompilerParams(vmem_limit_bytes=VMEM_LIMIT),
    )(qb, kb, vb, d_ob, carries, dwout)


def bwd_dw(x, norm_gain, dq_rot, dk_dup, dv_dup, qa_raw, ka_raw, cos, sin_s, q_gain2, k_gain2, dga, dgb, dqb, dkb, dvb):
    n_tiles = T_LOC // TM_DW

    def body(x_ref, ng_ref, dq_ref, dk_ref, dv_ref, qa_ref, ka_ref, cos_ref, sin_ref, qg_ref, kg_ref,
             dga_ref, dgb_ref, dqb_ref, dkb_ref, dvb_ref,
             dproj_ref, dw_hbm, dqg_ref, dkg_ref, acc_ref, stage_ref):
        step = pl.program_id(0)

        @pl.when(step == 0)
        def _():
            acc_ref[...] = jnp.zeros_like(acc_ref)
            dqg_ref[...] = jnp.zeros_like(dqg_ref)
            dkg_ref[...] = jnp.zeros_like(dkg_ref)

        lane = _lane((TM_DW, SLAB))
        bd = _head_blockdiag()
        cos, sin_s = cos_ref[...], sin_ref[...]

        def norm_rope_bwd(d_rot, raw, gain2):
            dy = d_rot * cos + _swap_half(d_rot * sin_s, lane)
            r = lax.rsqrt(_head_sum(raw * raw, bd) * (1.0 / HEAD_DIM) + EPS)
            xhat = raw * r
            dgain = jnp.sum(dy * xhat, axis=0, keepdims=True)
            dxh = dy * gain2
            mean = _head_sum(dxh * xhat, bd) * (1.0 / HEAD_DIM)
            return r * (dxh - xhat * mean), dgain

        def fold_dup(d_dup):
            a, b2 = d_dup[:, :SLAB], d_dup[:, SLAB:]
            return jnp.where(lane < HEAD_DIM, a + pltpu.roll(a, HEAD_DIM, 1), b2 + pltpu.roll(b2, HEAD_DIM, 1))

        pieces = []
        dqg = jnp.zeros((1, SLAB), F32)
        for p in range(4):
            sl = slice(p * SLAB, (p + 1) * SLAB)
            d_raw, dg = norm_rope_bwd(dq_ref[:, sl], qa_ref[:, sl], qg_ref[...])
            pieces.append(d_raw.astype(BF16))
            dqg = dqg + dg
        d_raw, dkg = norm_rope_bwd(fold_dup(dk_ref[...]), ka_ref[...], kg_ref[...])
        pieces.append(d_raw.astype(BF16))
        pieces.append(fold_dup(dv_ref[...]).astype(BF16))
        pieces += [dga_ref[...], dqb_ref[...], dkb_ref[...].astype(BF16), dvb_ref[...].astype(BF16),
                   dgb_ref[...]]
        dproj = jnp.concatenate(pieces, axis=1)
        dproj_ref[...] = dproj
        dqg_ref[0:1, :] += dqg + pltpu.roll(dqg, HEAD_DIM, 1)
        dkg_ref[0:1, :] += dkg + pltpu.roll(dkg, HEAD_DIM, 1)

        xv = x_ref[...]
        rstd = lax.rsqrt(jnp.mean(xv * xv, axis=-1, keepdims=True) + EPS)
        h = (xv * rstd * ng_ref[...]).astype(BF16)
        for r0 in range(0, IN_WIDTH, ACC_ROWS):
            acc_ref[r0:r0 + ACC_ROWS, :] += _dot(dproj[:, r0:r0 + ACC_ROWS], h, TN)

        @pl.when(step == n_tiles - 1)
        def _():
            for r0 in range(0, IN_WIDTH, ACC_ROWS):
                stage_ref[...] = acc_ref[r0:r0 + ACC_ROWS, :].astype(BF16)
                pltpu.sync_copy(stage_ref, dw_hbm.at[r0:r0 + ACC_ROWS, :])

    def tile(w):
        return pl.BlockSpec((TM_DW, w), lambda i: (i, 0))

    def whole(a):
        return pl.BlockSpec(a.shape, lambda i: (0, 0))

    const = lambda i: (0, 0)
    return pl.pallas_call(
        body, name="bwd_dw", grid=(n_tiles,),
        in_specs=[tile(D_MODEL), whole(norm_gain),
                  tile(512), tile(256), tile(256), tile(512), tile(128), tile(128), tile(128),
                  whole(q_gain2), whole(k_gain2), tile(512), tile(512), tile(512), tile(512), tile(512)],
        out_specs=[tile(IN_WIDTH), pl.BlockSpec(memory_space=pl.ANY),
                   pl.BlockSpec((8, SLAB), const), pl.BlockSpec((8, SLAB), const)],
        out_shape=[jax.ShapeDtypeStruct((T_LOC, IN_WIDTH), BF16), jax.ShapeDtypeStruct((IN_WIDTH, D_MODEL), BF16),
                   jax.ShapeDtypeStruct((8, SLAB), F32), jax.ShapeDtypeStruct((8, SLAB), F32)],
        scratch_shapes=[pltpu.VMEM((IN_WIDTH, D_MODEL), F32), pltpu.VMEM((ACC_ROWS, D_MODEL), BF16)],
        compiler_params=_params(("arbitrary",)),
    )(x, norm_gain, dq_rot, dk_dup, dv_dup, qa_raw, ka_raw, cos, sin_s, q_gain2, k_gain2, dga, dgb, dqb, dkb, dvb)


def bwd_dx(x, dout, norm_gain, win_t, dproj, dwin_t, dqg, dkg, dsink, loss_part):
    n_tiles = T_LOC // TM
    rows_per = IN_SHARD
    step_sums, step_merge = 3, 7

    def body(x_ref, dout_ref, ng_ref, w_hbm, dp_ref, a_hbm, dqg_ref, dkg_ref, dsink_ref, loss_ref,
             gx_ref, ra_hbm, rs_hbm, w_ref, dng_ref, s_ref, own_ref, sib_ref, snd_ref, extra_ref,
             w_sem, d2d_send, d2d_recv, ici_send, ici_recv, own_sems, s_send, s_recv, out_sem):
        step = pl.program_id(0)
        x, y, c = _mesh_pos()
        me, sibling = (x, y, c), (x, y, 1 - c)
        chips = {"own": (x, y), "x": (1 - x, y), "y": (x, 1 - y), "d": (1 - x, 1 - y)}
        index = {"own": 0, "x": 1, "y": 2, "d": 3}
        order = ("d", "x", "y", "own")

        def rows(pos):
            return a_hbm.at[pl.ds(_lin(pos) * rows_per, rows_per), :]

        def to_sibling(k):
            return pltpu.make_async_remote_copy(
                src_ref=rows((*chips[k], 1 - c)), dst_ref=sib_ref.at[index[k]],
                send_sem=d2d_send.at[index[k]], recv_sem=d2d_recv.at[index[k]], device_id=sibling, device_id_type=MESH)

        def mine(k):
            return pltpu.make_async_copy(rows((*chips[k], c)), own_ref.at[index[k]], own_sems.at[index[k]])

        def ici(n, to_chip, dst):
            return pltpu.make_async_remote_copy(
                src_ref=snd_ref.at[n], dst_ref=dst, send_sem=ici_send.at[n], recv_sem=ici_recv.at[n],
                device_id=(*chips[to_chip], c), device_id_type=MESH)

        def chip_sum(k):
            to_sibling(k).wait_recv()
            mine(k).wait()
            return own_ref[index[k]].astype(F32) + sib_ref[index[k]].astype(F32)

        def by_core(fn):
            pl.when(c == 0)(lambda: fn("x", "y"))
            pl.when(c == 1)(lambda: fn("y", "x"))

        @pl.when(step == 0)
        def _():
            cp = pltpu.make_async_copy(w_hbm, w_ref, w_sem)
            cp.start()
            for k in order:
                to_sibling(k).start()
                mine(k).start()
            dng_ref[...] = jnp.zeros_like(dng_ref)
            cp.wait()

        @pl.when(step == step_sums)
        def _():
            def first_sends(direct, via):
                snd_ref[0] = chip_sum("d").astype(BF16)
                ici(0, direct, extra_ref).start()
                snd_ref[1] = chip_sum(direct).astype(BF16)
                ici(1, direct, ra_hbm.at[index[direct]]).start()
            by_core(first_sends)

        @pl.when(step == step_merge)
        def _():
            def merge(direct, via):
                merged = chip_sum(via)
                ici(0, direct, extra_ref).wait_recv()
                snd_ref[2] = (merged + extra_ref[...].astype(F32)).astype(BF16)
                ici(2, via, ra_hbm.at[index[via]]).start()
                own_ref[0] = chip_sum("own").astype(BF16)
                pltpu.make_async_copy(own_ref.at[0], ra_hbm.at[0], out_sem).start()
            by_core(merge)

        xv = x_ref[...]
        rstd = lax.rsqrt(jnp.mean(xv * xv, axis=-1, keepdims=True) + EPS)
        xhat = xv * rstd
        gain = ng_ref[...]
        dh = _dot(dp_ref[...], w_ref[...])
        dng_ref[0:1, :] += jnp.sum(dh * xhat, axis=0, keepdims=True)
        dxh = dh * gain
        gx_ref[...] = dout_ref[...] + rstd * (dxh - xhat * jnp.mean(dxh * xhat, axis=-1, keepdims=True))

        @pl.when(step == n_tiles - 1)
        def _():
            s_ref[...] = jnp.concatenate(
                [dng_ref[...], dqg_ref[...], dkg_ref[...], dsink_ref[...], loss_ref[...]], axis=1)
            start_small, finish_small = _direct_exchange(
                lambda dev: s_ref, lambda dev: rs_hbm.at[_lin(dev)], s_send, s_recv, out_sem)

            def finish(direct, via):
                ici(1, direct, ra_hbm.at[index[direct]]).wait_recv()
                ici(2, via, ra_hbm.at[index[via]]).wait_recv()
                for n, to in ((0, direct), (1, direct), (2, via)):
                    ici(n, to, extra_ref).wait_send()
            by_core(finish)
            pltpu.make_async_copy(own_ref.at[0], ra_hbm.at[0], out_sem).wait()
            for k in order:
                to_sibling(k).wait_send()
            start_small()
            finish_small()

    def tile(w):
        return pl.BlockSpec((TM, w), lambda i: (i, 0))

    def whole(a):
        return pl.BlockSpec(a.shape, lambda i: (0, 0))

    hbm = pl.BlockSpec(memory_space=pl.ANY)
    block = (rows_per, D_MODEL)
    return pl.pallas_call(
        body, name="bwd_dx", grid=(n_tiles,),
        in_specs=[tile(D_MODEL), tile(D_MODEL), whole(norm_gain), hbm, tile(IN_WIDTH), hbm,
                  whole(dqg), whole(dkg), whole(dsink), whole(loss_part)],
        out_specs=[tile(D_MODEL), hbm, hbm],
        out_shape=[jax.ShapeDtypeStruct((T_LOC, D_MODEL), F32), jax.ShapeDtypeStruct((3,) + block, BF16),
                   jax.ShapeDtypeStruct((N_DEV, 8, SMALL_W), F32)],
        scratch_shapes=[pltpu.VMEM((IN_WIDTH, D_MODEL), BF16), pltpu.VMEM((8, D_MODEL), F32),
                        pltpu.VMEM((8, SMALL_W), F32),
                        pltpu.VMEM((4,) + block, BF16), pltpu.VMEM((4,) + block, BF16), pltpu.VMEM((3,) + block, BF16),
                        pltpu.VMEM(block, BF16),
                        pltpu.SemaphoreType.DMA, pltpu.SemaphoreType.DMA((4,)), pltpu.SemaphoreType.DMA((4,)),
                        pltpu.SemaphoreType.DMA((3,)), pltpu.SemaphoreType.DMA((3,)), pltpu.SemaphoreType.DMA((4,)),
                        pltpu.SemaphoreType.DMA((7,)), pltpu.SemaphoreType.DMA((7,)), pltpu.SemaphoreType.DMA],
        compiler_params=_params(("arbitrary",)),
    )(x, dout, norm_gain, win_t, dproj, dwin_t, dqg, dkg, dsink, loss_part)


def _adamw(w, g, m, v):
    m = ADAM_B1 * m + (1.0 - ADAM_B1) * g
    v = ADAM_B2 * v + (1.0 - ADAM_B2) * (g * g)
    m_hat = m / (1.0 - ADAM_B1 ** ADAM_STEP)
    v_hat = v / (1.0 - ADAM_B2 ** ADAM_STEP)
    delta = -ADAM_LR * (m_hat / (jnp.sqrt(v_hat) + ADAM_EPS) + ADAM_WD * w)
    return delta, m, v


def _sum_slots(r_ref):
    g = r_ref[0].astype(F32)
    for s in range(1, r_ref.shape[0]):
        g = g + r_ref[s].astype(F32)
    return g


def adamw_rows(name, recv, w, m, v):
    def body(r_ref, w_ref, m_ref, v_ref, g_ref, d_ref, nm_ref, nv_ref):
        g = _sum_slots(r_ref)
        g_ref[...] = g
        d_ref[...], nm_ref[...], nv_ref[...] = _adamw(w_ref[...], g, m_ref[...], v_ref[...])

    return pl.pallas_call(
        body, name=name,
        out_shape=[jax.ShapeDtypeStruct(w.shape, F32)] * 4,
        compiler_params=pltpu.CompilerParams(vmem_limit_bytes=VMEM_LIMIT),
    )(recv, w, m, v)


def adamw_small(recv, weights, moments_m, moments_v):
    n = len(weights)

    def body(r_ref, *refs):
        ins, outs = refs[:3 * n], refs[3 * n:]
        s = _sum_slots(r_ref)
        eye = (_row((8, SLAB)) == _lane((8, SLAB))).astype(F32)
        sinks = jnp.sum(s[:, 1280:1408] * eye, axis=0, keepdims=True)
        grads = [s[0:1, :D_MODEL], s[0:1, 1024:1024 + HEAD_DIM], s[0:1, 1152:1152 + HEAD_DIM], sinks[:, :8]]
        for k in range(n):
            outs[k][...] = grads[k]
            outs[n + k][...], outs[2 * n + k][...], outs[3 * n + k][...] = _adamw(
                ins[k][...], grads[k], ins[n + k][...], ins[2 * n + k][...])
        loss = jnp.sum(jnp.sum(s[:, 1408:1536], axis=1, keepdims=True), axis=0, keepdims=True) * (0.5 / D_MODEL)
        outs[4 * n][...] = loss

    res = pl.pallas_call(
        body, name="adamw_small",
        out_shape=[jax.ShapeDtypeStruct(w.shape, F32) for w in weights] * 4 + [jax.ShapeDtypeStruct((1, 1), F32)],
        compiler_params=pltpu.CompilerParams(vmem_limit_bytes=VMEM_LIMIT),
    )(recv, *weights, *moments_m, *moments_v)
    return res[:n], res[n:2 * n], res[2 * n:3 * n], res[3 * n:4 * n], res[4 * n]


def kernel(x, positions, norm_gain, w_in, q_norm_gain, k_norm_gain, sinks, w_out, loss_target, m_norm_gain, m_w_in, m_q_norm_gain, m_k_norm_gain, m_sinks, m_w_out, v_norm_gain, v_w_in, v_q_norm_gain, v_k_norm_gain, v_sinks, v_w_out):
    x2 = x.reshape(T_LOC, D_MODEL)
    tgt2 = loss_target.reshape(T_LOC, D_MODEL)
    pos2 = positions.reshape(T_LOC, 1)
    half = HEAD_DIM // 2
    inv_freq = ROPE_THETA ** (-jnp.arange(half, dtype=F32) * 2.0 / HEAD_DIM)
    inv_freq = jnp.tile(inv_freq, SLAB // half).reshape(1, SLAB)
    sin_sign = jnp.tile(jnp.concatenate([-jnp.ones((half,), F32), jnp.ones((half,), F32)]), 2).reshape(1, SLAB)
    q_gain2 = jnp.tile(q_norm_gain, (1, 2))
    k_gain2 = jnp.tile(k_norm_gain, (1, 2))

    win_t = gather_weights(w_in.reshape(D_MODEL, IN_SHARD).T.astype(BF16))

    (qa_raw, ka_raw, q_rot, k_dup, v_dup, ga, qb, kb, vb, gb, cos, sin_s, wout) = fwd_proj(
        x2, pos2, norm_gain, win_t, inv_freq, sin_sign, q_gain2, k_gain2, w_out.reshape(OUT_SHARD, D_MODEL).astype(BF16))
    o_a = swa_fwd(q_rot, k_dup, v_dup, sinks)
    o_b, carries = sb_fwd(qb, kb, vb)
    dout, d_oa, d_ob, dga, dgb, dwout, loss_part = out_loss(o_a, o_b, ga, gb, x2, tgt2, wout)
    dq_rot, dk_dup, dv_dup, dsink = swa_bwd(q_rot, k_dup, v_dup, o_a, d_oa, sinks)
    dqb, dkb, dvb, r_out = sb_bwd(qb, kb, vb, d_ob, carries, dwout)
    dproj, dwin_t, dqg, dkg = bwd_dw(
        x2, norm_gain, dq_rot, dk_dup, dv_dup, qa_raw, ka_raw, cos, sin_s, q_gain2, k_gain2, dga, dgb, dqb, dkb, dvb)
    grad_x, r_win, r_small = bwd_dx(x2, dout, norm_gain, win_t, dproj, dwin_t, dqg, dkg, dsink, loss_part)

    w_in2, m_in2, v_in2 = (a.reshape(D_MODEL, IN_SHARD).T for a in (w_in, m_w_in, v_w_in))
    w_out2, m_out2, v_out2 = (a.reshape(OUT_SHARD, D_MODEL) for a in (w_out, m_w_out, v_w_out))
    big_in = adamw_rows("adamw_w_in", r_win, w_in2, m_in2, v_in2)
    big_out = adamw_rows("adamw_w_out", r_out, w_out2, m_out2, v_out2)
    *small_out, loss = adamw_small(
        r_small, (norm_gain, q_norm_gain, k_norm_gain, sinks),
        (m_norm_gain, m_q_norm_gain, m_k_norm_gain, m_sinks), (v_norm_gain, v_q_norm_gain, v_k_norm_gain, v_sinks))

    def leaves(k):
        ng, qg, kg, sk = small_out[k]
        return (ng, big_in[k].T.reshape(1, D_MODEL, IN_SHARD), qg, kg, sk, big_out[k].reshape(1, OUT_SHARD, D_MODEL))

    return (loss.reshape(()), grad_x.reshape(B_LOC, SEQ, D_MODEL), *leaves(0), *leaves(1), *leaves(2), *leaves(3))
```

```python
import functools

import jax
import jax.numpy as jnp
from jax import lax
from jax.experimental import pallas as pl
from jax.experimental.pallas import tpu as pltpu

F32 = jnp.float32
BF16 = jnp.bfloat16

N_DEV = 8
D_MODEL = 1024
SEQ = 2048
B_LOC = 2
T_LOC = B_LOC * SEQ
HEAD_DIM = 64
HEAD_SHIFT = 6
BLK = 128
N_BLK = SEQ // BLK
SLAB = 128
IN_WIDTH = 3328
IN_SHARD = IN_WIDTH // N_DEV
OUT_SHARD = D_MODEL // N_DEV
EPS = 1e-6
ROPE_THETA = 10000.0
Q_SCALE = 0.125
R_QA, R_KA, R_VA, R_GA, R_QB, R_KB, R_VB, R_GB, R_END = 0, 512, 640, 768, 1280, 1792, 2304, 2816, 3328
SMALL_W = 1536
ADAM_LR, ADAM_B1, ADAM_B2, ADAM_EPS, ADAM_WD, ADAM_STEP = 0.001, 0.9, 0.999, 1e-08, 0.01, 10
TM = 256
TM_DW = 512
TM_FWD = 512
ACC_ROWS = 256
VMEM_LIMIT = 56 * 1024 * 1024

MESH = pl.DeviceIdType.MESH
NT = (((1,), (1,)), ((), ()))
TN = (((0,), (0,)), ((), ()))


def _params(sem, limit=VMEM_LIMIT):
    return pltpu.CompilerParams(dimension_semantics=sem, vmem_limit_bytes=limit)


def _dot(a, b, dims=None):
    if dims is None:
        return jnp.dot(a, b, preferred_element_type=F32)
    return lax.dot_general(a, b, dims, preferred_element_type=F32)


def _lane(shape):
    return lax.broadcasted_iota(jnp.int32, shape, len(shape) - 1)


def _row(shape):
    return lax.broadcasted_iota(jnp.int32, shape, 0)


def _head_blockdiag():
    return ((_row((SLAB, SLAB)) >> HEAD_SHIFT) == (_lane((SLAB, SLAB)) >> HEAD_SHIFT)).astype(BF16)


def _head_sum(x, bd):
    return _dot(x.astype(BF16), bd)


def _swap_half(y, lane):
    return jnp.where((lane & 32) != 0, pltpu.roll(y, 32, 1), pltpu.roll(y, 96, 1))


def _stack_heads(q, lane):
    zero = jnp.zeros_like(q)
    return jnp.concatenate([jnp.where(lane < HEAD_DIM, q, zero), jnp.where(lane >= HEAD_DIM, q, zero)], axis=0)


def _unstack_heads(x2, lane):
    return jnp.where(lane < HEAD_DIM, x2[:BLK], x2[BLK:])


def _sigmoid(x):
    return 1.0 / (1.0 + jnp.exp(-x))


def _mesh_pos():
    return lax.axis_index("x"), lax.axis_index("y"), lax.axis_index("c")


def _flip(pos, mask):
    return tuple(1 - p if m else p for p, m in zip(pos, mask))


def _lin(pos):
    return 4 * pos[0] + 2 * pos[1] + pos[2]


DEV_FLIPS = [(fx, fy, fc) for fx in (0, 1) for fy in (0, 1) for fc in (0, 1)][1:]


def _direct_exchange(src_for, dst_slot, send_sems, recv_sems, local_sem):
    me = _mesh_pos()

    def copy(k, to):
        return pltpu.make_async_remote_copy(
            src_ref=src_for(to), dst_ref=dst_slot(me), send_sem=send_sems.at[k], recv_sem=recv_sems.at[k],
            device_id=to, device_id_type=MESH)

    def landed(k, frm):
        return pltpu.make_async_remote_copy(
            src_ref=src_for(frm), dst_ref=dst_slot(frm), send_sem=send_sems.at[k], recv_sem=recv_sems.at[k],
            device_id=frm, device_id_type=MESH)

    local = None if local_sem is None else pltpu.make_async_copy(src_for(me), dst_slot(me), local_sem)
    peers = [_flip(me, f) for f in DEV_FLIPS]

    def start():
        if local is not None:
            local.start()
        for k, to in enumerate(peers):
            copy(k, to).start()

    def finish():
        for k, frm in enumerate(peers):
            landed(k, frm).wait_recv()
        for k, to in enumerate(peers):
            copy(k, to).wait_send()
        if local is not None:
            local.wait()

    return start, finish


def gather_weights(shard):
    m = shard.shape[0]

    def body(f32_ref, o_ref, a_ref, ici_send, ici_recv, d2d_send, d2d_recv, local_sem):
        a_ref[...] = f32_ref[...].astype(BF16)
        x, y, c = _mesh_pos()
        me, sibling = (x, y, c), (x, y, 1 - c)
        chip_x, chip_y, chip_d = (1 - x, y), (x, 1 - y), (1 - x, 1 - y)

        def rows(pos):
            return o_ref.at[pl.ds(_lin(pos) * m, m), :]

        def ici(k, block, to, src=None):
            return pltpu.make_async_remote_copy(
                src_ref=rows(block) if src is None else src, dst_ref=rows(block),
                send_sem=ici_send.at[k], recv_sem=ici_recv.at[k], device_id=to, device_id_type=MESH)

        def d2d(k, chip, mine, src=None):
            block = (*chip, c) if mine else (*chip, 1 - c)
            return pltpu.make_async_remote_copy(
                src_ref=rows(block) if src is None else src, dst_ref=rows(block),
                send_sem=d2d_send.at[k], recv_sem=d2d_recv.at[k], device_id=sibling, device_id_type=MESH)

        local = pltpu.make_async_copy(a_ref, rows(me), local_sem)
        local.start()
        sends = [ici(0, me, (*chip_x, c), src=a_ref), ici(1, me, (*chip_y, c), src=a_ref),
                 d2d(0, (x, y), True, src=a_ref)]
        for cp in sends:
            cp.start()

        def pass_on(first, k_first, second, k_second, onward):
            ici(k_first, (*first, c), me).wait_recv()
            relay = ici(2, (*first, c), (*onward, c))
            relay.start()
            hand = [d2d(1 + k_first, first, True)]
            hand[0].start()
            ici(k_second, (*second, c), me).wait_recv()
            hand.append(d2d(1 + k_second, second, True))
            hand[1].start()
            ici(2, (*chip_d, c), me).wait_recv()
            hand.append(d2d(3, chip_d, True))
            hand[2].start()
            for cp in [relay] + hand:
                cp.wait_send()

        @pl.when(c == 0)
        def _():
            pass_on(chip_y, 1, chip_x, 0, chip_x)

        @pl.when(c == 1)
        def _():
            pass_on(chip_x, 0, chip_y, 1, chip_y)

        for k, chip in enumerate([(x, y), chip_x, chip_y, chip_d]):
            d2d(k, chip, False).wait_recv()
        for cp in sends:
            cp.wait_send()
        local.wait()

    vmem = pl.BlockSpec(memory_space=pltpu.VMEM)
    return pl.pallas_call(
        body, name="gather_weights",
        out_shape=jax.ShapeDtypeStruct((N_DEV * m, shard.shape[1]), BF16),
        in_specs=[vmem], out_specs=vmem,
        scratch_shapes=[pltpu.VMEM(shard.shape, BF16), pltpu.SemaphoreType.DMA((3,)), pltpu.SemaphoreType.DMA((3,)),
                        pltpu.SemaphoreType.DMA((4,)), pltpu.SemaphoreType.DMA((4,)), pltpu.SemaphoreType.DMA],
        compiler_params=pltpu.CompilerParams(vmem_limit_bytes=VMEM_LIMIT),
    )(shard)


def _norm_rope(xs, gain2, cos, sin_s, bd, lane):
    r = lax.rsqrt(_head_sum(xs * xs, bd) * (1.0 / HEAD_DIM) + EPS)
    y = xs * r * gain2
    return y * cos + _swap_half(y, lane) * sin_s


def _dup_heads(xs, lane):
    r = pltpu.roll(xs, HEAD_DIM, 1)
    lo = lane < HEAD_DIM
    return jnp.concatenate([jnp.where(lo, xs, r), jnp.where(lo, r, xs)], axis=1)


def fwd_proj(x, pos, norm_gain, win_t, inv_freq, sin_sign, q_gain2, k_gain2, wout_shard):
    n_tiles = T_LOC // TM_FWD

    def body(x_ref, pos_ref, ng_ref, w_ref, if_ref, sg_ref, qg_ref, kg_ref, ws_hbm,
             qa_raw_ref, ka_raw_ref, q_rot_ref, k_dup_ref, v_dup_ref, ga_ref, qb_ref, kb_ref, vb_ref, gb_ref,
             cos_ref, sin_ref, wo_hbm, wo_send, wo_recv, wo_local):
        start_wout, finish_wout = _direct_exchange(
            lambda dev: ws_hbm, lambda dev: wo_hbm.at[pl.ds(_lin(dev) * OUT_SHARD, OUT_SHARD), :],
            wo_send, wo_recv, wo_local)
        pl.when(pl.program_id(0) == 0)(start_wout)

        xv = x_ref[...]
        rstd = lax.rsqrt(jnp.mean(xv * xv, axis=-1, keepdims=True) + EPS)
        h = (xv * rstd * ng_ref[...]).astype(BF16)

        def proj(r0, r1):
            return _dot(h, w_ref[r0:r1, :], NT)

        ang_t = if_ref[...] * pos_ref[...].astype(F32)
        cos = jnp.cos(ang_t).T
        sin_s = jnp.sin(ang_t).T * sg_ref[...]
        cos_ref[...] = cos
        sin_ref[...] = sin_s
        lane = _lane((TM_FWD, SLAB))
        bd = _head_blockdiag()

        qa = proj(R_QA, R_KA)
        qa_raw_ref[...] = qa
        for p in range(4):
            sl = slice(p * SLAB, (p + 1) * SLAB)
            q_rot_ref[:, sl] = (_norm_rope(qa[:, sl], qg_ref[...], cos, sin_s, bd, lane) * Q_SCALE).astype(BF16)
        ka = proj(R_KA, R_VA)
        ka_raw_ref[...] = ka
        k_dup_ref[...] = _dup_heads(_norm_rope(ka, kg_ref[...], cos, sin_s, bd, lane), lane).astype(BF16)
        v_dup_ref[...] = _dup_heads(proj(R_VA, R_GA), lane).astype(BF16)
        ga_ref[...] = proj(R_GA, R_QB)
        qb_ref[...] = (proj(R_QB, R_KB) * Q_SCALE).astype(BF16)
        kb_ref[...] = proj(R_KB, R_VB).astype(BF16)
        vb_ref[...] = proj(R_VB, R_GB).astype(BF16)
        gb_ref[...] = proj(R_GB, R_END)
        pl.when(pl.program_id(0) == n_tiles - 1)(finish_wout)

    def tile(w):
        return pl.BlockSpec((TM_FWD, w), lambda i: (i, 0))

    def whole(a):
        return pl.BlockSpec(a.shape, lambda i: (0, 0))

    hbm = pl.BlockSpec(memory_space=pl.ANY)
    widths = [(512, F32), (128, F32), (512, BF16), (256, BF16), (256, BF16), (512, F32), (512, BF16), (512, BF16),
              (512, BF16), (512, F32), (128, F32), (128, F32)]
    return pl.pallas_call(
        body, name="fwd_proj", grid=(n_tiles,),
        in_specs=[tile(D_MODEL), pl.BlockSpec((1, TM_FWD), lambda i: (0, i)), whole(norm_gain), whole(win_t),
                  whole(inv_freq), whole(sin_sign),
                  whole(q_gain2), whole(k_gain2), hbm],
        out_specs=[tile(w) for w, _ in widths] + [hbm],
        out_shape=[jax.ShapeDtypeStruct((T_LOC, w), dt) for w, dt in widths]
        + [jax.ShapeDtypeStruct((D_MODEL, D_MODEL), BF16)],
        scratch_shapes=[pltpu.SemaphoreType.DMA((7,)), pltpu.SemaphoreType.DMA((7,)), pltpu.SemaphoreType.DMA],
        compiler_params=_params(("arbitrary",)),
    )(x, pos, norm_gain, win_t, inv_freq, sin_sign, q_gain2, k_gain2, wout_shard)


def _swa_window(prev_ref, cur_ref, p):
    gsl = _slab(p // 2)
    return jnp.concatenate([prev_ref[:, gsl], cur_ref[:, gsl]], axis=0)


def _swa_probs(s, sinks_ref, p, i):
    shape = (2 * BLK, 2 * BLK)
    r = _row(shape) & (BLK - 1)
    cidx = _lane(shape)
    valid = (cidx > r) & (cidx <= r + BLK) & ((cidx >= BLK) | (i > 0))
    s = jnp.where(valid, s, -jnp.inf)
    sink = jnp.where(_row((2 * BLK, 1)) < BLK, sinks_ref[0, 2 * p], sinks_ref[0, 2 * p + 1])
    m = jnp.maximum(jnp.max(s, axis=-1, keepdims=True), sink)
    e = jnp.exp(s - m)
    e_sink = jnp.exp(sink - m)
    den = jnp.sum(e, axis=-1, keepdims=True) + e_sink
    return e / den, e_sink / den


SWA_CHAINS = [(b, p) for b in range(B_LOC) for p in range(4)]


def _swa_specs():
    def cur(w):
        return pl.BlockSpec((B_LOC, BLK, w), lambda i: (0, i, 0))

    def prev(w):
        return pl.BlockSpec((B_LOC, BLK, w), lambda i: (0, jnp.maximum(i - 1, 0), 0))

    return cur, prev


def swa_fwd(q_rot, k_dup, v_dup, sinks):
    def body(q_ref, kp_ref, kc_ref, vp_ref, vc_ref, sinks_ref, o_ref):
        i = pl.program_id(0)
        lane = _lane((BLK, SLAB))
        s = [_dot(_stack_heads(q_ref[b, :, _slab(p)], lane), _swa_window(kp_ref.at[b], kc_ref.at[b], p), NT)
             for b, p in SWA_CHAINS]
        pn = [_swa_probs(s[c], sinks_ref, p, i)[0].astype(BF16) for c, (b, p) in enumerate(SWA_CHAINS)]
        for c, (b, p) in enumerate(SWA_CHAINS):
            o_ref[b, :, _slab(p)] = _unstack_heads(_dot(pn[c], _swa_window(vp_ref.at[b], vc_ref.at[b], p)), lane)

    cur, prev = _swa_specs()
    q3, k3, v3 = (a.reshape(B_LOC, SEQ, a.shape[1]) for a in (q_rot, k_dup, v_dup))
    return pl.pallas_call(
        body, name="swa_fwd", grid=(N_BLK,),
        in_specs=[cur(512), prev(256), cur(256), prev(256), cur(256), pl.BlockSpec(memory_space=pltpu.SMEM)],
        out_specs=cur(512),
        out_shape=jax.ShapeDtypeStruct((B_LOC, SEQ, 512), F32),
        compiler_params=_params(("arbitrary",)),
    )(q3, k3, k3, v3, v3, sinks).reshape(T_LOC, 512)


def _tri(suffix):
    r, cidx = _row((BLK + 16, BLK)), _lane((BLK + 16, BLK))
    tri = (cidx > r) if suffix else (cidx < r)
    return (tri | (r >= BLK)).astype(BF16)


def _key_sums(tri, x):
    res = _dot(tri, x.astype(BF16))
    return res[:BLK], res[BLK:BLK + 1]


def _sb_softplus(zt, valid):
    neg_abs = lax.bitcast_convert_type(lax.bitcast_convert_type(zt, jnp.uint32) | jnp.uint32(0x80000000), F32)
    sp = jnp.maximum(zt, 0.0) + jnp.log(1.0 + jnp.exp(neg_abs))
    return sp if valid is None else jnp.where(valid, sp, 0.0)


def _sb_weights(zt, sp, later, valid):
    w = jnp.exp(zt - sp - later)
    return w if valid is None else jnp.where(valid, w, 0.0)


def _slab(pp):
    return slice(pp * SLAB, (pp + 1) * SLAB)


def _blk(j):
    return pl.ds(pl.multiple_of(j * BLK, BLK), BLK)


def _causal_t():
    return _row((BLK, 2 * BLK)) < (_lane((BLK, 2 * BLK)) & (BLK - 1))


def _sb_rows(b, j):
    return pl.ds(pl.multiple_of(b * SEQ + j * BLK, BLK), BLK)


SB_CHAINS = [(b, pp) for b in range(B_LOC) for pp in range(4)]


def sb_fwd(qb, kb, vb):
    def body(q_ref, k_ref, v_ref, o_ref, c_ref, vt_ref, ot_ref):
        for c, (b, pp) in enumerate(SB_CHAINS):
            for j in range(N_BLK):
                vt_ref[c, j] = v_ref[b * SEQ + j * BLK:b * SEQ + (j + 1) * BLK, _slab(pp)].T
        lane = _lane((BLK, SLAB))
        tri = _tri(True)
        valid = _causal_t()
        jrow = _row((N_BLK, 2 * BLK))
        chains = range(len(SB_CHAINS))

        def q_block(i, _):
            q2 = [_stack_heads(q_ref[_sb_rows(b, i), _slab(pp)], lane) for b, pp in SB_CHAINS]

            def key_block(j, carry, mask, first):
                zt = [_dot(k_ref[_sb_rows(b, j), _slab(pp)], q2[c], NT) for c, (b, pp) in enumerate(SB_CHAINS)]
                sp = [_sb_softplus(zt[c], mask) for c in chains]
                sums = [_key_sums(tri, sp[c]) for c in chains]
                w = [_sb_weights(zt[c], sp[c], sums[c][0] + carry[c], mask) for c in chains]
                for c in chains:
                    pv = _dot(vt_ref[c, j], w[c].astype(BF16))
                    if first:
                        ot_ref[c] = pv
                    else:
                        ot_ref[c] += pv
                return tuple(carry[c] + sums[c][1] for c in chains)

            def earlier(jj, state):
                carry, saved = state
                j = i - 1 - jj
                saved = tuple(jnp.where(jrow == j, carry[c], saved[c]) for c in chains)
                return key_block(j, carry, None, False), saved

            zero = tuple(jnp.zeros((1, 2 * BLK), F32) for _ in chains)
            carry = key_block(i, zero, valid, True)
            _, saved = lax.fori_loop(0, i, earlier, (carry, tuple(jnp.zeros((N_BLK, 2 * BLK), F32) for _ in chains)))
            for c, (b, pp) in enumerate(SB_CHAINS):
                o_ref[_sb_rows(b, i), _slab(pp)] = _unstack_heads(ot_ref[c].T, lane)
                c_ref[c * N_BLK + i] = saved[c]
            return 0

        lax.fori_loop(0, N_BLK, q_block, 0)

    n_ch = len(SB_CHAINS)
    vmem = pl.BlockSpec(memory_space=pltpu.VMEM)
    return pl.pallas_call(
        body, name="sb_fwd",
        in_specs=[vmem] * 3, out_specs=[vmem] * 2,
        out_shape=[jax.ShapeDtypeStruct((T_LOC, 512), F32), jax.ShapeDtypeStruct((n_ch * N_BLK, N_BLK, 2 * BLK), F32)],
        scratch_shapes=[pltpu.VMEM((n_ch, N_BLK, SLAB, BLK), BF16), pltpu.VMEM((n_ch, SLAB, 2 * BLK), F32)],
        compiler_params=pltpu.CompilerParams(vmem_limit_bytes=VMEM_LIMIT),
    )(qb, kb, vb)


def out_loss(o_a, o_b, ga, gb, x, target, wout):
    n_tiles = T_LOC // TM_FWD

    def body(oa_ref, ob_ref, ga_ref, gb_ref, x_ref, t_ref, w_ref,
             dout_ref, doa_ref, dob_ref, dga_ref, dgb_ref, dw_ref, loss_ref, acc_ref):
        step = pl.program_id(0)

        @pl.when(step == 0)
        def _():
            acc_ref[...] = jnp.zeros_like(acc_ref)
            loss_ref[...] = jnp.zeros_like(loss_ref)

        oa, ob, gav, gbv = oa_ref[...], ob_ref[...], ga_ref[...], gb_ref[...]
        sa, sb = _sigmoid(gav), _sigmoid(gbv)
        silu_a, silu_b = gav * sa, gbv * sb
        y = jnp.concatenate([oa * silu_a, ob * silu_b], axis=1).astype(BF16)
        err = x_ref[...] + _dot(y, w_ref[...]) - t_ref[...]
        e2 = err * err
        part = jnp.sum(e2.reshape(TM_FWD // 8, 8, D_MODEL), axis=0)
        loss_ref[...] += functools.reduce(lambda a, b: a + b, [part[:, k * 128:(k + 1) * 128] for k in range(8)])
        dout = err * (1.0 / D_MODEL)
        dout_ref[...] = dout
        dob16 = dout.astype(BF16)
        for r0 in range(0, D_MODEL, ACC_ROWS):
            acc_ref[r0:r0 + ACC_ROWS, :] += _dot(y[:, r0:r0 + ACC_ROWS], dob16, TN)
        dy = _dot(dob16, w_ref[...], NT)
        dya, dyb = dy[:, :512], dy[:, 512:]
        doa_ref[...] = (dya * silu_a).astype(BF16)
        dob_ref[...] = (dyb * silu_b).astype(BF16)
        dga_ref[...] = (dya * oa * (sa * (1.0 + gav * (1.0 - sa)))).astype(BF16)
        dgb_ref[...] = (dyb * ob * (sb * (1.0 + gbv * (1.0 - sb)))).astype(BF16)

        @pl.when(step == n_tiles - 1)
        def _():
            dw_ref[...] = acc_ref[...].astype(BF16)

    def tile(w):
        return pl.BlockSpec((TM_FWD, w), lambda i: (i, 0))

    const = lambda i: (0, 0)
    return pl.pallas_call(
        body, name="out_loss", grid=(n_tiles,),
        in_specs=[tile(512)] * 4 + [tile(D_MODEL)] * 2 + [pl.BlockSpec((D_MODEL, D_MODEL), const)],
        out_specs=[tile(D_MODEL), tile(512), tile(512), tile(512), tile(512),
                   pl.BlockSpec((D_MODEL, D_MODEL), const), pl.BlockSpec((8, 128), const)],
        out_shape=[jax.ShapeDtypeStruct((T_LOC, D_MODEL), F32)] + [jax.ShapeDtypeStruct((T_LOC, 512), BF16)] * 4
        + [jax.ShapeDtypeStruct((D_MODEL, D_MODEL), BF16), jax.ShapeDtypeStruct((8, 128), F32)],
        scratch_shapes=[pltpu.VMEM((D_MODEL, D_MODEL), F32)],
        compiler_params=_params(("arbitrary",)),
    )(o_a, o_b, ga, gb, x, target, wout)


def swa_bwd(q_rot, k_dup, v_dup, o_a, d_oa, sinks):
    def body(q_ref, kp_ref, kc_ref, vp_ref, vc_ref, o_ref, do_ref, sinks_ref, dq_ref, dk_ref, dv_ref, dsink_ref):
        i = pl.program_id(0)

        @pl.when(i == 0)
        def _():
            dk_ref[...] = jnp.zeros_like(dk_ref)
            dv_ref[...] = jnp.zeros_like(dv_ref)
            dsink_ref[...] = jnp.zeros_like(dsink_ref)

        lane = _lane((BLK, SLAB))
        rows_prev, rows_cur = _blk(jnp.maximum(i - 1, 0)), _blk(i)
        chains = range(len(SWA_CHAINS))
        q2 = [_stack_heads(q_ref[b, :, _slab(p)], lane) for b, p in SWA_CHAINS]
        do2 = [_stack_heads(do_ref[b, :, _slab(p)], lane) for b, p in SWA_CHAINS]
        keys = [_swa_window(kp_ref.at[b], kc_ref.at[b], p) for b, p in SWA_CHAINS]
        s = [_dot(q2[c], keys[c], NT) for c in chains]
        dp = [_dot(do2[c], _swa_window(vp_ref.at[b], vc_ref.at[b], p), NT) for c, (b, p) in enumerate(SWA_CHAINS)]
        ds, pn16, cols = [], [], []
        for c, (b, p) in enumerate(SWA_CHAINS):
            pn, p_sink = _swa_probs(s[c], sinks_ref, p, i)
            o = o_ref[b, :, _slab(p)]
            delta = jnp.sum(do2[c].astype(F32) * jnp.concatenate([o, o], axis=0), axis=-1, keepdims=True)
            ds.append((pn * (dp[c] - delta)).astype(BF16))
            pn16.append(pn.astype(BF16))
            cols.append(-p_sink * delta)
        for c, (b, p) in enumerate(SWA_CHAINS):
            dq_ref[b, :, _slab(p)] = _unstack_heads(_dot(ds[c], keys[c]), lane) * Q_SCALE
        dk2 = [_dot(ds[c], q2[c], TN) for c in chains]
        dv2 = [_dot(pn16[c], do2[c], TN) for c in chains]
        for c, (b, p) in enumerate(SWA_CHAINS):
            gsl = _slab(p // 2)
            dk_ref[b, rows_prev, gsl] += dk2[c][:BLK]
            dk_ref[b, rows_cur, gsl] += dk2[c][BLK:]
            dv_ref[b, rows_prev, gsl] += dv2[c][:BLK]
            dv_ref[b, rows_cur, gsl] += dv2[c][BLK:]
            for e in range(2):
                dsink_ref[2 * p + e:2 * p + e + 1, :] += jnp.sum(cols[c][e * BLK:(e + 1) * BLK], axis=0, keepdims=True)

    cur, prev = _swa_specs()
    whole = pl.BlockSpec((B_LOC, SEQ, 256), lambda i: (0, 0, 0))
    q3, k3, v3, o3, do3 = (a.reshape(B_LOC, SEQ, a.shape[1]) for a in (q_rot, k_dup, v_dup, o_a, d_oa))
    dq, dk, dv, dsink = pl.pallas_call(
        body, name="swa_bwd", grid=(N_BLK,),
        in_specs=[cur(512), prev(256), cur(256), prev(256), cur(256), cur(512), cur(512),
                  pl.BlockSpec(memory_space=pltpu.SMEM)],
        out_specs=[cur(512), whole, whole, pl.BlockSpec((8, 128), lambda i: (0, 0))],
        out_shape=[jax.ShapeDtypeStruct((B_LOC, SEQ, 512), F32), jax.ShapeDtypeStruct((B_LOC, SEQ, 256), F32),
                   jax.ShapeDtypeStruct((B_LOC, SEQ, 256), F32), jax.ShapeDtypeStruct((8, 128), F32)],
        compiler_params=_params(("arbitrary",)),
    )(q3, k3, k3, v3, v3, o3, do3, sinks)
    return dq.reshape(T_LOC, 512), dk.reshape(T_LOC, 256), dv.reshape(T_LOC, 256), dsink


def sb_bwd(qb, kb, vb, d_ob, carries, dwout):
    def body(q_ref, k_ref, v_ref, do_ref, c_ref, dw_hbm, dq_ref, dk_ref, dv_ref, rw_hbm, kt_ref, dqt_ref,
             rw_send, rw_recv, rw_local):
        start_dwout, finish_dwout = _direct_exchange(
            lambda dev: dw_hbm.at[pl.ds(_lin(dev) * OUT_SHARD, OUT_SHARD), :], lambda dev: rw_hbm.at[_lin(dev)],
            rw_send, rw_recv, rw_local)
        start_dwout()
        for c, (b, pp) in enumerate(SB_CHAINS):
            for j in range(N_BLK):
                kt_ref[c, j] = k_ref[b * SEQ + j * BLK:b * SEQ + (j + 1) * BLK, _slab(pp)].T
        dk_ref[...] = jnp.zeros_like(dk_ref)
        dv_ref[...] = jnp.zeros_like(dv_ref)
        dqt_ref[...] = jnp.zeros_like(dqt_ref)
        lane = _lane((BLK, SLAB))
        tri_after, tri_before = _tri(True), _tri(False)
        valid = _causal_t()
        jrow = _row((N_BLK, 2 * BLK))
        chains = range(len(SB_CHAINS))

        def q_block(i, _):
            q2 = [_stack_heads(q_ref[_sb_rows(b, i), _slab(pp)], lane) for b, pp in SB_CHAINS]
            do2 = [_stack_heads(do_ref[_sb_rows(b, i), _slab(pp)], lane) for b, pp in SB_CHAINS]

            def key_block(j, carry_sp, before_u, mask):
                at = [(_sb_rows(b, j), _slab(pp)) for b, pp in SB_CHAINS]
                zt = [_dot(k_ref[at[c]], q2[c], NT) for c in chains]
                dw = [_dot(v_ref[at[c]], do2[c], NT) for c in chains]
                sp = [_sb_softplus(zt[c], mask) for c in chains]
                later = [_key_sums(tri_after, sp[c])[0] for c in chains]
                w = [_sb_weights(zt[c], sp[c], later[c] + carry_sp[c], mask) for c in chains]
                u = [dw[c] * w[c] for c in chains]
                for c in chains:
                    dv_ref[at[c]] += _dot(w[c].astype(BF16), do2[c])
                sums = [_key_sums(tri_before, u[c]) for c in chains]
                dz16 = []
                for c in chains:
                    sig = jnp.exp(zt[c] - sp[c])
                    dz = u[c] - sig * (u[c] + before_u[c] + sums[c][0])
                    if mask is not None:
                        dz = jnp.where(mask, dz, 0.0)
                    dz16.append(dz.astype(BF16))
                for c in chains:
                    dk_ref[at[c]] += _dot(dz16[c], q2[c])
                    dqt_ref[c] += _dot(kt_ref[c, j], dz16[c])
                return tuple(before_u[c] + sums[c][1] for c in chains)

            def earlier(j, before_u):
                carry_sp = [jnp.sum(jnp.where(jrow == j, c_ref[c * N_BLK + i], 0.0), axis=0, keepdims=True)
                            for c in chains]
                return key_block(j, carry_sp, before_u, None)

            zero = tuple(jnp.zeros((1, 2 * BLK), F32) for _ in chains)
            before_u = lax.fori_loop(0, i, earlier, zero)
            key_block(i, zero, before_u, valid)
            for c, (b, pp) in enumerate(SB_CHAINS):
                dq_ref[_sb_rows(b, i), _slab(pp)] = (_unstack_heads(dqt_ref[c].T, lane) * Q_SCALE).astype(BF16)
                dqt_ref[c] = jnp.zeros((SLAB, 2 * BLK), F32)
            return 0

        lax.fori_loop(0, N_BLK, q_block, 0)
        finish_dwout()

    n_ch = len(SB_CHAINS)
    vmem, hbm = pl.BlockSpec(memory_space=pltpu.VMEM), pl.BlockSpec(memory_space=pl.ANY)
    return pl.pallas_call(
        body, name="sb_bwd",
        in_specs=[vmem] * 5 + [hbm], out_specs=[vmem] * 3 + [hbm],
        out_shape=[jax.ShapeDtypeStruct((T_LOC, 512), BF16)] + [jax.ShapeDtypeStruct((T_LOC, 512), F32)] * 2
        + [jax.ShapeDtypeStruct((N_DEV, OUT_SHARD, D_MODEL), BF16)],
        scratch_shapes=[pltpu.VMEM((n_ch, N_BLK, SLAB, BLK), BF16), pltpu.VMEM((n_ch, SLAB, 2 * BLK), F32),
                        pltpu.SemaphoreType.DMA((7,)), pltpu.SemaphoreType.DMA((7,)), pltpu.SemaphoreType.DMA],
        compiler_params=pltpu.CompilerParams(vmem_limit_bytes=VMEM_LIMIT),
    )(qb, kb, vb, d_ob, carries, dwout)


def bwd_dw(x, norm_gain, dq_rot, dk_dup, dv_dup, qa_raw, ka_raw, cos, sin_s, q_gain2, k_gain2, dga, dgb, dqb, dkb, dvb):
    n_tiles = T_LOC // TM_DW

    def body(x_ref, ng_ref, dq_ref, dk_ref, dv_ref, qa_ref, ka_ref, cos_ref, sin_ref, qg_ref, kg_ref,
             dga_ref, dgb_ref, dqb_ref, dkb_ref, dvb_ref,
             dproj_ref, dw_hbm, dqg_ref, dkg_ref, acc_ref, stage_ref):
        step = pl.program_id(0)

        @pl.when(step == 0)
        def _():
            acc_ref[...] = jnp.zeros_like(acc_ref)
            dqg_ref[...] = jnp.zeros_like(dqg_ref)
            dkg_ref[...] = jnp.zeros_like(dkg_ref)

        lane = _lane((TM_DW, SLAB))
        bd = _head_blockdiag()
        cos, sin_s = cos_ref[...], sin_ref[...]

        def norm_rope_bwd(d_rot, raw, gain2):
            dy = d_rot * cos + _swap_half(d_rot * sin_s, lane)
            r = lax.rsqrt(_head_sum(raw * raw, bd) * (1.0 / HEAD_DIM) + EPS)
            xhat = raw * r
            dgain = jnp.sum(dy * xhat, axis=0, keepdims=True)
            dxh = dy * gain2
            mean = _head_sum(dxh * xhat, bd) * (1.0 / HEAD_DIM)
            return r * (dxh - xhat * mean), dgain

        def fold_dup(d_dup):
            a, b2 = d_dup[:, :SLAB], d_dup[:, SLAB:]
            return jnp.where(lane < HEAD_DIM, a + pltpu.roll(a, HEAD_DIM, 1), b2 + pltpu.roll(b2, HEAD_DIM, 1))

        pieces = []
        dqg = jnp.zeros((1, SLAB), F32)
        for p in range(4):
            sl = slice(p * SLAB, (p + 1) * SLAB)
            d_raw, dg = norm_rope_bwd(dq_ref[:, sl], qa_ref[:, sl], qg_ref[...])
            pieces.append(d_raw.astype(BF16))
            dqg = dqg + dg
        d_raw, dkg = norm_rope_bwd(fold_dup(dk_ref[...]), ka_ref[...], kg_ref[...])
        pieces.append(d_raw.astype(BF16))
        pieces.append(fold_dup(dv_ref[...]).astype(BF16))
        pieces += [dga_ref[...], dqb_ref[...], dkb_ref[...].astype(BF16), dvb_ref[...].astype(BF16),
                   dgb_ref[...]]
        dproj = jnp.concatenate(pieces, axis=1)
        dproj_ref[...] = dproj
        dqg_ref[0:1, :] += dqg + pltpu.roll(dqg, HEAD_DIM, 1)
        dkg_ref[0:1, :] += dkg + pltpu.roll(dkg, HEAD_DIM, 1)

        xv = x_ref[...]
        rstd = lax.rsqrt(jnp.mean(xv * xv, axis=-1, keepdims=True) + EPS)
        h = (xv * rstd * ng_ref[...]).astype(BF16)
        for r0 in range(0, IN_WIDTH, ACC_ROWS):
            acc_ref[r0:r0 + ACC_ROWS, :] += _dot(dproj[:, r0:r0 + ACC_ROWS], h, TN)

        @pl.when(step == n_tiles - 1)
        def _():
            for r0 in range(0, IN_WIDTH, ACC_ROWS):
                stage_ref[...] = acc_ref[r0:r0 + ACC_ROWS, :].astype(BF16)
                pltpu.sync_copy(stage_ref, dw_hbm.at[r0:r0 + ACC_ROWS, :])

    def tile(w):
        return pl.BlockSpec((TM_DW, w), lambda i: (i, 0))

    def whole(a):
        return pl.BlockSpec(a.shape, lambda i: (0, 0))

    const = lambda i: (0, 0)
    return pl.pallas_call(
        body, name="bwd_dw", grid=(n_tiles,),
        in_specs=[tile(D_MODEL), whole(norm_gain),
                  tile(512), tile(256), tile(256), tile(512), tile(128), tile(128), tile(128),
                  whole(q_gain2), whole(k_gain2), tile(512), tile(512), tile(512), tile(512), tile(512)],
        out_specs=[tile(IN_WIDTH), pl.BlockSpec(memory_space=pl.ANY),
                   pl.BlockSpec((8, SLAB), const), pl.BlockSpec((8, SLAB), const)],
        out_shape=[jax.ShapeDtypeStruct((T_LOC, IN_WIDTH), BF16), jax.ShapeDtypeStruct((IN_WIDTH, D_MODEL), BF16),
                   jax.ShapeDtypeStruct((8, SLAB), F32), jax.ShapeDtypeStruct((8, SLAB), F32)],
        scratch_shapes=[pltpu.VMEM((IN_WIDTH, D_MODEL), F32), pltpu.VMEM((ACC_ROWS, D_MODEL), BF16)],
        compiler_params=_params(("arbitrary",)),
    )(x, norm_gain, dq_rot, dk_dup, dv_dup, qa_raw, ka_raw, cos, sin_s, q_gain2, k_gain2, dga, dgb, dqb, dkb, dvb)


def bwd_dx(x, dout, norm_gain, win_t, dproj, dwin_t, dqg, dkg, dsink, loss_part):
    n_tiles = T_LOC // TM
    rows_per = IN_SHARD
    step_sums, step_merge = 3, 7

    def body(x_ref, dout_ref, ng_ref, w_hbm, dp_ref, a_hbm, dqg_ref, dkg_ref, dsink_ref, loss_ref,
             gx_ref, ra_hbm, rs_hbm, w_ref, dng_ref, s_ref, own_ref, sib_ref, snd_ref, extra_ref,
             w_sem, d2d_send, d2d_recv, ici_send, ici_recv, own_sems, s_send, s_recv, out_sem):
        step = pl.program_id(0)
        x, y, c = _mesh_pos()
        me, sibling = (x, y, c), (x, y, 1 - c)
        chips = {"own": (x, y), "x": (1 - x, y), "y": (x, 1 - y), "d": (1 - x, 1 - y)}
        index = {"own": 0, "x": 1, "y": 2, "d": 3}
        order = ("d", "x", "y", "own")

        def rows(pos):
            return a_hbm.at[pl.ds(_lin(pos) * rows_per, rows_per), :]

        def to_sibling(k):
            return pltpu.make_async_remote_copy(
                src_ref=rows((*chips[k], 1 - c)), dst_ref=sib_ref.at[index[k]],
                send_sem=d2d_send.at[index[k]], recv_sem=d2d_recv.at[index[k]], device_id=sibling, device_id_type=MESH)

        def mine(k):
            return pltpu.make_async_copy(rows((*chips[k], c)), own_ref.at[index[k]], own_sems.at[index[k]])

        def ici(n, to_chip, dst):
            return pltpu.make_async_remote_copy(
                src_ref=snd_ref.at[n], dst_ref=dst, send_sem=ici_send.at[n], recv_sem=ici_recv.at[n],
                device_id=(*chips[to_chip], c), device_id_type=MESH)

        def chip_sum(k):
            to_sibling(k).wait_recv()
            mine(k).wait()
            return own_ref[index[k]].astype(F32) + sib_ref[index[k]].astype(F32)

        def by_core(fn):
            pl.when(c == 0)(lambda: fn("x", "y"))
            pl.when(c == 1)(lambda: fn("y", "x"))

        @pl.when(step == 0)
        def _():
            cp = pltpu.make_async_copy(w_hbm, w_ref, w_sem)
            cp.start()
            for k in order:
                to_sibling(k).start()
                mine(k).start()
            dng_ref[...] = jnp.zeros_like(dng_ref)
            cp.wait()

        @pl.when(step == step_sums)
        def _():
            def first_sends(direct, via):
                snd_ref[0] = chip_sum("d").astype(BF16)
                ici(0, direct, extra_ref).start()
                snd_ref[1] = chip_sum(direct).astype(BF16)
                ici(1, direct, ra_hbm.at[index[direct]]).start()
            by_core(first_sends)

        @pl.when(step == step_merge)
        def _():
            def merge(direct, via):
                merged = chip_sum(via)
                ici(0, direct, extra_ref).wait_recv()
                snd_ref[2] = (merged + extra_ref[...].astype(F32)).astype(BF16)
                ici(2, via, ra_hbm.at[index[via]]).start()
                own_ref[0] = chip_sum("own").astype(BF16)
                pltpu.make_async_copy(own_ref.at[0], ra_hbm.at[0], out_sem).start()
            by_core(merge)

        xv = x_ref[...]
        rstd = lax.rsqrt(jnp.mean(xv * xv, axis=-1, keepdims=True) + EPS)
        xhat = xv * rstd
        gain = ng_ref[...]
        dh = _dot(dp_ref[...], w_ref[...])
        dng_ref[0:1, :] += jnp.sum(dh * xhat, axis=0, keepdims=True)
        dxh = dh * gain
        gx_ref[...] = dout_ref[...] + rstd * (dxh - xhat * jnp.mean(dxh * xhat, axis=-1, keepdims=True))

        @pl.when(step == n_tiles - 1)
        def _():
            s_ref[...] = jnp.concatenate(
                [dng_ref[...], dqg_ref[...], dkg_ref[...], dsink_ref[...], loss_ref[...]], axis=1)
            start_small, finish_small = _direct_exchange(
                lambda dev: s_ref, lambda dev: rs_hbm.at[_lin(dev)], s_send, s_recv, out_sem)

            def finish(direct, via):
                ici(1, direct, ra_hbm.at[index[direct]]).wait_recv()
                ici(2, via, ra_hbm.at[index[via]]).wait_recv()
                for n, to in ((0, direct), (1, direct), (2, via)):
                    ici(n, to, extra_ref).wait_send()
            by_core(finish)
            pltpu.make_async_copy(own_ref.at[0], ra_hbm.at[0], out_sem).wait()
            for k in order:
                to_sibling(k).wait_send()
            start_small()
            finish_small()

    def tile(w):
        return pl.BlockSpec((TM, w), lambda i: (i, 0))

    def whole(a):
        return pl.BlockSpec(a.shape, lambda i: (0, 0))

    hbm = pl.BlockSpec(memory_space=pl.ANY)
    block = (rows_per, D_MODEL)
    return pl.pallas_call(
        body, name="bwd_dx", grid=(n_tiles,),
        in_specs=[tile(D_MODEL), tile(D_MODEL), whole(norm_gain), hbm, tile(IN_WIDTH), hbm,
                  whole(dqg), whole(dkg), whole(dsink), whole(loss_part)],
        out_specs=[tile(D_MODEL), hbm, hbm],
        out_shape=[jax.ShapeDtypeStruct((T_LOC, D_MODEL), F32), jax.ShapeDtypeStruct((3,) + block, BF16),
                   jax.ShapeDtypeStruct((N_DEV, 8, SMALL_W), F32)],
        scratch_shapes=[pltpu.VMEM((IN_WIDTH, D_MODEL), BF16), pltpu.VMEM((8, D_MODEL), F32),
                        pltpu.VMEM((8, SMALL_W), F32),
                        pltpu.VMEM((4,) + block, BF16), pltpu.VMEM((4,) + block, BF16), pltpu.VMEM((3,) + block, BF16),
                        pltpu.VMEM(block, BF16),
                        pltpu.SemaphoreType.DMA, pltpu.SemaphoreType.DMA((4,)), pltpu.SemaphoreType.DMA((4,)),
                        pltpu.SemaphoreType.DMA((3,)), pltpu.SemaphoreType.DMA((3,)), pltpu.SemaphoreType.DMA((4,)),
                        pltpu.SemaphoreType.DMA((7,)), pltpu.SemaphoreType.DMA((7,)), pltpu.SemaphoreType.DMA],
        compiler_params=_params(("arbitrary",)),
    )(x, dout, norm_gain, win_t, dproj, dwin_t, dqg, dkg, dsink, loss_part)


def _adamw(w, g, m, v):
    m = ADAM_B1 * m + (1.0 - ADAM_B1) * g
    v = ADAM_B2 * v + (1.0 - ADAM_B2) * (g * g)
    m_hat = m / (1.0 - ADAM_B1 ** ADAM_STEP)
    v_hat = v / (1.0 - ADAM_B2 ** ADAM_STEP)
    delta = -ADAM_LR * (m_hat / (jnp.sqrt(v_hat) + ADAM_EPS) + ADAM_WD * w)
    return delta, m, v


def _sum_slots(r_ref):
    g = r_ref[0].astype(F32)
    for s in range(1, r_ref.shape[0]):
        g = g + r_ref[s].astype(F32)
    return g


def adamw_rows(name, recv, w, m, v):
    def body(r_ref, w_ref, m_ref, v_ref, g_ref, d_ref, nm_ref, nv_ref):
        g = _sum_slots(r_ref)
        g_ref[...] = g
        d_ref[...], nm_ref[...], nv_ref[...] = _adamw(w_ref[...], g, m_ref[...], v_ref[...])

    return pl.pallas_call(
        body, name=name,
        out_shape=[jax.ShapeDtypeStruct(w.shape, F32)] * 4,
        compiler_params=pltpu.CompilerParams(vmem_limit_bytes=VMEM_LIMIT),
    )(recv, w, m, v)


def adamw_small(recv, weights, moments_m, moments_v):
    n = len(weights)

    def body(r_ref, *refs):
        ins, outs = refs[:3 * n], refs[3 * n:]
        s = _sum_slots(r_ref)
        eye = (_row((8, SLAB)) == _lane((8, SLAB))).astype(F32)
        sinks = jnp.sum(s[:, 1280:1408] * eye, axis=0, keepdims=True)
        grads = [s[0:1, :D_MODEL], s[0:1, 1024:1024 + HEAD_DIM], s[0:1, 1152:1152 + HEAD_DIM], sinks[:, :8]]
        for k in range(n):
            outs[k][...] = grads[k]
            outs[n + k][...], outs[2 * n + k][...], outs[3 * n + k][...] = _adamw(
                ins[k][...], grads[k], ins[n + k][...], ins[2 * n + k][...])
        loss = jnp.sum(jnp.sum(s[:, 1408:1536], axis=1, keepdims=True), axis=0, keepdims=True) * (0.5 / D_MODEL)
        outs[4 * n][...] = loss

    res = pl.pallas_call(
        body, name="adamw_small",
        out_shape=[jax.ShapeDtypeStruct(w.shape, F32) for w in weights] * 4 + [jax.ShapeDtypeStruct((1, 1), F32)],
        compiler_params=pltpu.CompilerParams(vmem_limit_bytes=VMEM_LIMIT),
    )(recv, *weights, *moments_m, *moments_v)
    return res[:n], res[n:2 * n], res[2 * n:3 * n], res[3 * n:4 * n], res[4 * n]


def kernel(x, positions, norm_gain, w_in, q_norm_gain, k_norm_gain, sinks, w_out, loss_target, m_norm_gain, m_w_in, m_q_norm_gain, m_k_norm_gain, m_sinks, m_w_out, v_norm_gain, v_w_in, v_q_norm_gain, v_k_norm_gain, v_sinks, v_w_out):
    x2 = x.reshape(T_LOC, D_MODEL)
    tgt2 = loss_target.reshape(T_LOC, D_MODEL)
    pos2 = positions.reshape(1, T_LOC)
    half = HEAD_DIM // 2
    inv_freq = ROPE_THETA ** (-jnp.arange(half, dtype=F32) * 2.0 / HEAD_DIM)
    inv_freq = jnp.tile(inv_freq, SLAB // half).reshape(SLAB, 1)
    sin_sign = jnp.tile(jnp.concatenate([-jnp.ones((half,), F32), jnp.ones((half,), F32)]), 2).reshape(1, SLAB)
    q_gain2 = jnp.tile(q_norm_gain, (1, 2))
    k_gain2 = jnp.tile(k_norm_gain, (1, 2))

    win_t = gather_weights(w_in.reshape(D_MODEL, IN_SHARD).T)

    (qa_raw, ka_raw, q_rot, k_dup, v_dup, ga, qb, kb, vb, gb, cos, sin_s, wout) = fwd_proj(
        x2, pos2, norm_gain, win_t, inv_freq, sin_sign, q_gain2, k_gain2, w_out.reshape(OUT_SHARD, D_MODEL).astype(BF16))
    o_a = swa_fwd(q_rot, k_dup, v_dup, sinks)
    o_b, carries = sb_fwd(qb, kb, vb)
    dout, d_oa, d_ob, dga, dgb, dwout, loss_part = out_loss(o_a, o_b, ga, gb, x2, tgt2, wout)
    dq_rot, dk_dup, dv_dup, dsink = swa_bwd(q_rot, k_dup, v_dup, o_a, d_oa, sinks)
    dqb, dkb, dvb, r_out = sb_bwd(qb, kb, vb, d_ob, carries, dwout)
    dproj, dwin_t, dqg, dkg = bwd_dw(
        x2, norm_gain, dq_rot, dk_dup, dv_dup, qa_raw, ka_raw, cos, sin_s, q_gain2, k_gain2, dga, dgb, dqb, dkb, dvb)
    grad_x, r_win, r_small = bwd_dx(x2, dout, norm_gain, win_t, dproj, dwin_t, dqg, dkg, dsink, loss_part)

    w_in2, m_in2, v_in2 = (a.reshape(D_MODEL, IN_SHARD).T for a in (w_in, m_w_in, v_w_in))
    w_out2, m_out2, v_out2 = (a.reshape(OUT_SHARD, D_MODEL) for a in (w_out, m_w_out, v_w_out))
    big_in = adamw_rows("adamw_w_in", r_win, w_in2, m_in2, v_in2)
    big_out = adamw_rows("adamw_w_out", r_out, w_out2, m_out2, v_out2)
    *small_out, loss = adamw_small(
        r_small, (norm_gain, q_norm_gain, k_norm_gain, sinks),
        (m_norm_gain, m_q_norm_gain, m_k_norm_gain, m_sinks), (v_norm_gain, v_q_norm_gain, v_k_norm_gain, v_sinks))

    def leaves(k):
        ng, qg, kg, sk = small_out[k]
        return (ng, big_in[k].T.reshape(1, D_MODEL, IN_SHARD), qg, kg, sk, big_out[k].reshape(1, OUT_SHARD, D_MODEL))

    return (loss.reshape(()), grad_x.reshape(B_LOC, SEQ, D_MODEL), *leaves(0), *leaves(1), *leaves(2), *leaves(3))
```

```python
import functools

import jax
import jax.numpy as jnp
from jax import lax
from jax.experimental import pallas as pl
from jax.experimental.pallas import tpu as pltpu

F32 = jnp.float32
BF16 = jnp.bfloat16

N_DEV = 8
D_MODEL = 1024
SEQ = 2048
B_LOC = 2
T_LOC = B_LOC * SEQ
HEAD_DIM = 64
HEAD_SHIFT = 6
BLK = 128
N_BLK = SEQ // BLK
SLAB = 128
IN_WIDTH = 3328
IN_SHARD = IN_WIDTH // N_DEV
OUT_SHARD = D_MODEL // N_DEV
EPS = 1e-6
ROPE_THETA = 10000.0
Q_SCALE = 0.125
R_QA, R_KA, R_VA, R_GA, R_QB, R_KB, R_VB, R_GB, R_END = 0, 512, 640, 768, 1280, 1792, 2304, 2816, 3328
SMALL_W = 1536
ADAM_LR, ADAM_B1, ADAM_B2, ADAM_EPS, ADAM_WD, ADAM_STEP = 0.001, 0.9, 0.999, 1e-08, 0.01, 10
TM = 256
TM_DW = 512
TM_FWD = 512
ACC_ROWS = 256
VMEM_LIMIT = 56 * 1024 * 1024

MESH = pl.DeviceIdType.MESH
NT = (((1,), (1,)), ((), ()))
TN = (((0,), (0,)), ((), ()))


def _params(sem, limit=VMEM_LIMIT):
    return pltpu.CompilerParams(dimension_semantics=sem, vmem_limit_bytes=limit)


def _dot(a, b, dims=None):
    if dims is None:
        return jnp.dot(a, b, preferred_element_type=F32)
    return lax.dot_general(a, b, dims, preferred_element_type=F32)


def _lane(shape):
    return lax.broadcasted_iota(jnp.int32, shape, len(shape) - 1)


def _row(shape):
    return lax.broadcasted_iota(jnp.int32, shape, 0)


def _head_blockdiag():
    return ((_row((SLAB, SLAB)) >> HEAD_SHIFT) == (_lane((SLAB, SLAB)) >> HEAD_SHIFT)).astype(BF16)


def _head_sum(x, bd):
    return _dot(x.astype(BF16), bd)


def _swap_half(y, lane):
    return jnp.where((lane & 32) != 0, pltpu.roll(y, 32, 1), pltpu.roll(y, 96, 1))


def _stack_heads(q, lane):
    zero = jnp.zeros_like(q)
    return jnp.concatenate([jnp.where(lane < HEAD_DIM, q, zero), jnp.where(lane >= HEAD_DIM, q, zero)], axis=0)


def _unstack_heads(x2, lane):
    return jnp.where(lane < HEAD_DIM, x2[:BLK], x2[BLK:])


def _sigmoid(x):
    return 1.0 / (1.0 + jnp.exp(-x))


def _mesh_pos():
    return lax.axis_index("x"), lax.axis_index("y"), lax.axis_index("c")


def _flip(pos, mask):
    return tuple(1 - p if m else p for p, m in zip(pos, mask))


def _lin(pos):
    return 4 * pos[0] + 2 * pos[1] + pos[2]


DEV_FLIPS = [(fx, fy, fc) for fx in (0, 1) for fy in (0, 1) for fc in (0, 1)][1:]


def _direct_exchange(src_for, dst_slot, send_sems, recv_sems, local_sem):
    me = _mesh_pos()

    def copy(k, to):
        return pltpu.make_async_remote_copy(
            src_ref=src_for(to), dst_ref=dst_slot(me), send_sem=send_sems.at[k], recv_sem=recv_sems.at[k],
            device_id=to, device_id_type=MESH)

    def landed(k, frm):
        return pltpu.make_async_remote_copy(
            src_ref=src_for(frm), dst_ref=dst_slot(frm), send_sem=send_sems.at[k], recv_sem=recv_sems.at[k],
            device_id=frm, device_id_type=MESH)

    local = None if local_sem is None else pltpu.make_async_copy(src_for(me), dst_slot(me), local_sem)
    peers = [_flip(me, f) for f in DEV_FLIPS]

    def start():
        if local is not None:
            local.start()
        for k, to in enumerate(peers):
            copy(k, to).start()

    def finish():
        for k, frm in enumerate(peers):
            landed(k, frm).wait_recv()
        for k, to in enumerate(peers):
            copy(k, to).wait_send()
        if local is not None:
            local.wait()

    return start, finish


def gather_weights(shard):
    m = shard.shape[0]

    def body(f32_ref, o_ref, a_ref, ici_send, ici_recv, d2d_send, d2d_recv, local_sem):
        a_ref[...] = f32_ref[...].astype(BF16)
        x, y, c = _mesh_pos()
        me, sibling = (x, y, c), (x, y, 1 - c)
        chip_x, chip_y, chip_d = (1 - x, y), (x, 1 - y), (1 - x, 1 - y)

        def rows(pos):
            return o_ref.at[pl.ds(_lin(pos) * m, m), :]

        def ici(k, block, to, src=None):
            return pltpu.make_async_remote_copy(
                src_ref=rows(block) if src is None else src, dst_ref=rows(block),
                send_sem=ici_send.at[k], recv_sem=ici_recv.at[k], device_id=to, device_id_type=MESH)

        def d2d(k, chip, mine, src=None):
            block = (*chip, c) if mine else (*chip, 1 - c)
            return pltpu.make_async_remote_copy(
                src_ref=rows(block) if src is None else src, dst_ref=rows(block),
                send_sem=d2d_send.at[k], recv_sem=d2d_recv.at[k], device_id=sibling, device_id_type=MESH)

        local = pltpu.make_async_copy(a_ref, rows(me), local_sem)
        local.start()
        sends = [ici(0, me, (*chip_x, c), src=a_ref), ici(1, me, (*chip_y, c), src=a_ref),
                 d2d(0, (x, y), True, src=a_ref)]
        for cp in sends:
            cp.start()

        def pass_on(first, k_first, second, k_second, onward):
            ici(k_first, (*first, c), me).wait_recv()
            relay = ici(2, (*first, c), (*onward, c))
            relay.start()
            hand = [d2d(1 + k_first, first, True)]
            hand[0].start()
            ici(k_second, (*second, c), me).wait_recv()
            hand.append(d2d(1 + k_second, second, True))
            hand[1].start()
            ici(2, (*chip_d, c), me).wait_recv()
            hand.append(d2d(3, chip_d, True))
            hand[2].start()
            for cp in [relay] + hand:
                cp.wait_send()

        @pl.when(c == 0)
        def _():
            pass_on(chip_y, 1, chip_x, 0, chip_x)

        @pl.when(c == 1)
        def _():
            pass_on(chip_x, 0, chip_y, 1, chip_y)

        for k, chip in enumerate([(x, y), chip_x, chip_y, chip_d]):
            d2d(k, chip, False).wait_recv()
        for cp in sends:
            cp.wait_send()
        local.wait()

    vmem = pl.BlockSpec(memory_space=pltpu.VMEM)
    return pl.pallas_call(
        body, name="gather_weights",
        out_shape=jax.ShapeDtypeStruct((N_DEV * m, shard.shape[1]), BF16),
        in_specs=[vmem], out_specs=vmem,
        scratch_shapes=[pltpu.VMEM(shard.shape, BF16), pltpu.SemaphoreType.DMA((3,)), pltpu.SemaphoreType.DMA((3,)),
                        pltpu.SemaphoreType.DMA((4,)), pltpu.SemaphoreType.DMA((4,)), pltpu.SemaphoreType.DMA],
        compiler_params=pltpu.CompilerParams(vmem_limit_bytes=VMEM_LIMIT),
    )(shard)


def _norm_rope(xs, gain2, cos, sin_s, bd, lane):
    r = lax.rsqrt(_head_sum(xs * xs, bd) * (1.0 / HEAD_DIM) + EPS)
    y = xs * r * gain2
    return y * cos + _swap_half(y, lane) * sin_s


def _dup_heads(xs, lane):
    r = pltpu.roll(xs, HEAD_DIM, 1)
    lo = lane < HEAD_DIM
    return jnp.concatenate([jnp.where(lo, xs, r), jnp.where(lo, r, xs)], axis=1)


def fwd_proj(x, pos, norm_gain, win_t, inv_freq, sin_sign, q_gain2, k_gain2, wout_shard):
    n_tiles = T_LOC // TM_FWD

    def body(x_ref, pos_ref, ng_ref, w_ref, if_ref, sg_ref, qg_ref, kg_ref, ws_hbm,
             qa_raw_ref, ka_raw_ref, q_rot_ref, k_dup_ref, v_dup_ref, ga_ref, qb_ref, kb_ref, vb_ref, gb_ref,
             cos_ref, sin_ref, wo_hbm, wo_send, wo_recv, wo_local):
        start_wout, finish_wout = _direct_exchange(
            lambda dev: ws_hbm, lambda dev: wo_hbm.at[pl.ds(_lin(dev) * OUT_SHARD, OUT_SHARD), :],
            wo_send, wo_recv, wo_local)
        pl.when(pl.program_id(0) == 0)(start_wout)

        xv = x_ref[...]
        rstd = lax.rsqrt(jnp.mean(xv * xv, axis=-1, keepdims=True) + EPS)
        h = (xv * rstd * ng_ref[...]).astype(BF16)

        def proj(r0, r1):
            return _dot(h, w_ref[r0:r1, :], NT)

        ang_t = if_ref[...] * pos_ref[...].astype(F32)
        cos = jnp.cos(ang_t).T
        sin_s = jnp.sin(ang_t).T * sg_ref[...]
        cos_ref[...] = cos
        sin_ref[...] = sin_s
        lane = _lane((TM_FWD, SLAB))
        bd = _head_blockdiag()

        qa = proj(R_QA, R_KA)
        qa_raw_ref[...] = qa
        for p in range(4):
            sl = slice(p * SLAB, (p + 1) * SLAB)
            q_rot_ref[:, sl] = (_norm_rope(qa[:, sl], qg_ref[...], cos, sin_s, bd, lane) * Q_SCALE).astype(BF16)
        ka = proj(R_KA, R_VA)
        ka_raw_ref[...] = ka
        k_dup_ref[...] = _dup_heads(_norm_rope(ka, kg_ref[...], cos, sin_s, bd, lane), lane).astype(BF16)
        v_dup_ref[...] = _dup_heads(proj(R_VA, R_GA), lane).astype(BF16)
        ga_ref[...] = proj(R_GA, R_QB).astype(BF16)
        qb_ref[...] = (proj(R_QB, R_KB) * Q_SCALE).astype(BF16)
        kb_ref[...] = proj(R_KB, R_VB).astype(BF16)
        vb_ref[...] = proj(R_VB, R_GB).astype(BF16)
        gb_ref[...] = proj(R_GB, R_END).astype(BF16)
        pl.when(pl.program_id(0) == n_tiles - 1)(finish_wout)

    def tile(w):
        return pl.BlockSpec((TM_FWD, w), lambda i: (i, 0))

    def whole(a):
        return pl.BlockSpec(a.shape, lambda i: (0, 0))

    hbm = pl.BlockSpec(memory_space=pl.ANY)
    widths = [(512, F32), (128, F32), (512, BF16), (256, BF16), (256, BF16), (512, BF16), (512, BF16), (512, BF16),
              (512, BF16), (512, BF16), (128, F32), (128, F32)]
    return pl.pallas_call(
        body, name="fwd_proj", grid=(n_tiles,),
        in_specs=[tile(D_MODEL), pl.BlockSpec((1, TM_FWD), lambda i: (0, i)), whole(norm_gain), whole(win_t),
                  whole(inv_freq), whole(sin_sign),
                  whole(q_gain2), whole(k_gain2), hbm],
        out_specs=[tile(w) for w, _ in widths] + [hbm],
        out_shape=[jax.ShapeDtypeStruct((T_LOC, w), dt) for w, dt in widths]
        + [jax.ShapeDtypeStruct((D_MODEL, D_MODEL), BF16)],
        scratch_shapes=[pltpu.SemaphoreType.DMA((7,)), pltpu.SemaphoreType.DMA((7,)), pltpu.SemaphoreType.DMA],
        compiler_params=_params(("arbitrary",)),
    )(x, pos, norm_gain, win_t, inv_freq, sin_sign, q_gain2, k_gain2, wout_shard)


def _swa_window(prev_ref, cur_ref, p):
    gsl = _slab(p // 2)
    return jnp.concatenate([prev_ref[:, gsl], cur_ref[:, gsl]], axis=0)


def _swa_probs(s, sinks_ref, p, i):
    shape = (2 * BLK, 2 * BLK)
    r = _row(shape) & (BLK - 1)
    cidx = _lane(shape)
    valid = (cidx > r) & (cidx <= r + BLK) & ((cidx >= BLK) | (i > 0))
    s = jnp.where(valid, s, -jnp.inf)
    sink = jnp.where(_row((2 * BLK, 1)) < BLK, sinks_ref[0, 2 * p], sinks_ref[0, 2 * p + 1])
    m = jnp.maximum(jnp.max(s, axis=-1, keepdims=True), sink)
    e = jnp.exp(s - m)
    e_sink = jnp.exp(sink - m)
    den = jnp.sum(e, axis=-1, keepdims=True) + e_sink
    return e / den, e_sink / den


SWA_CHAINS = [(b, p) for b in range(B_LOC) for p in range(4)]


def _swa_specs():
    def cur(w):
        return pl.BlockSpec((B_LOC, BLK, w), lambda i: (0, i, 0))

    def prev(w):
        return pl.BlockSpec((B_LOC, BLK, w), lambda i: (0, jnp.maximum(i - 1, 0), 0))

    return cur, prev


def swa_fwd(q_rot, k_dup, v_dup, sinks):
    def body(q_ref, kp_ref, kc_ref, vp_ref, vc_ref, sinks_ref, o_ref):
        i = pl.program_id(0)
        lane = _lane((BLK, SLAB))
        s = [_dot(_stack_heads(q_ref[b, :, _slab(p)], lane), _swa_window(kp_ref.at[b], kc_ref.at[b], p), NT)
             for b, p in SWA_CHAINS]
        pn = [_swa_probs(s[c], sinks_ref, p, i)[0].astype(BF16) for c, (b, p) in enumerate(SWA_CHAINS)]
        for c, (b, p) in enumerate(SWA_CHAINS):
            o = _unstack_heads(_dot(pn[c], _swa_window(vp_ref.at[b], vc_ref.at[b], p)), lane)
            o_ref[b, :, _slab(p)] = o.astype(BF16)

    cur, prev = _swa_specs()
    q3, k3, v3 = (a.reshape(B_LOC, SEQ, a.shape[1]) for a in (q_rot, k_dup, v_dup))
    return pl.pallas_call(
        body, name="swa_fwd", grid=(N_BLK,),
        in_specs=[cur(512), prev(256), cur(256), prev(256), cur(256), pl.BlockSpec(memory_space=pltpu.SMEM)],
        out_specs=cur(512),
        out_shape=jax.ShapeDtypeStruct((B_LOC, SEQ, 512), BF16),
        compiler_params=_params(("arbitrary",)),
    )(q3, k3, k3, v3, v3, sinks).reshape(T_LOC, 512)


def _tri(suffix):
    r, cidx = _row((BLK + 16, BLK)), _lane((BLK + 16, BLK))
    tri = (cidx > r) if suffix else (cidx < r)
    return (tri | (r >= BLK)).astype(BF16)


def _key_sums(tri, x):
    res = _dot(tri, x.astype(BF16))
    return res[:BLK], res[BLK:BLK + 1]


def _sb_softplus(zt, valid):
    neg_abs = lax.bitcast_convert_type(lax.bitcast_convert_type(zt, jnp.uint32) | jnp.uint32(0x80000000), F32)
    sp = jnp.maximum(zt, 0.0) + jnp.log(1.0 + jnp.exp(neg_abs))
    return sp if valid is None else jnp.where(valid, sp, 0.0)


def _sb_weights(zt, sp, later, valid):
    w = jnp.exp(zt - sp - later)
    return w if valid is None else jnp.where(valid, w, 0.0)


def _slab(pp):
    return slice(pp * SLAB, (pp + 1) * SLAB)


def _blk(j):
    return pl.ds(pl.multiple_of(j * BLK, BLK), BLK)


def _causal_t():
    return _row((BLK, 2 * BLK)) < (_lane((BLK, 2 * BLK)) & (BLK - 1))


def _sb_rows(b, j):
    return pl.ds(pl.multiple_of(b * SEQ + j * BLK, BLK), BLK)


SB_CHAINS = [(b, pp) for b in range(B_LOC) for pp in range(4)]


def sb_fwd(qb, kb, vb):
    def body(q_ref, k_ref, v_ref, o_ref, c_ref, vt_ref, ot_ref):
        for c, (b, pp) in enumerate(SB_CHAINS):
            for j in range(N_BLK):
                vt_ref[c, j] = v_ref[b * SEQ + j * BLK:b * SEQ + (j + 1) * BLK, _slab(pp)].T
        lane = _lane((BLK, SLAB))
        tri = _tri(True)
        valid = _causal_t()
        jrow = _row((N_BLK, 2 * BLK))
        chains = range(len(SB_CHAINS))

        def q_block(i, _):
            q2 = [_stack_heads(q_ref[_sb_rows(b, i), _slab(pp)], lane) for b, pp in SB_CHAINS]

            def key_block(j, carry, mask, first):
                zt = [_dot(k_ref[_sb_rows(b, j), _slab(pp)], q2[c], NT) for c, (b, pp) in enumerate(SB_CHAINS)]
                sp = [_sb_softplus(zt[c], mask) for c in chains]
                sums = [_key_sums(tri, sp[c]) for c in chains]
                w = [_sb_weights(zt[c], sp[c], sums[c][0] + carry[c], mask) for c in chains]
                for c in chains:
                    pv = _dot(vt_ref[c, j], w[c].astype(BF16))
                    if first:
                        ot_ref[c] = pv
                    else:
                        ot_ref[c] += pv
                return tuple(carry[c] + sums[c][1] for c in chains)

            def earlier(jj, state):
                carry, saved = state
                j = i - 1 - jj
                saved = tuple(jnp.where(jrow == j, carry[c], saved[c]) for c in chains)
                return key_block(j, carry, None, False), saved

            zero = tuple(jnp.zeros((1, 2 * BLK), F32) for _ in chains)
            carry = key_block(i, zero, valid, True)
            _, saved = lax.fori_loop(0, i, earlier, (carry, tuple(jnp.zeros((N_BLK, 2 * BLK), F32) for _ in chains)))
            for c, (b, pp) in enumerate(SB_CHAINS):
                o_ref[_sb_rows(b, i), _slab(pp)] = _unstack_heads(ot_ref[c].T, lane).astype(BF16)
                c_ref[c * N_BLK + i] = saved[c]
            return 0

        lax.fori_loop(0, N_BLK, q_block, 0)

    n_ch = len(SB_CHAINS)
    vmem = pl.BlockSpec(memory_space=pltpu.VMEM)
    return pl.pallas_call(
        body, name="sb_fwd",
        in_specs=[vmem] * 3, out_specs=[vmem] * 2,
        out_shape=[jax.ShapeDtypeStruct((T_LOC, 512), BF16), jax.ShapeDtypeStruct((n_ch * N_BLK, N_BLK, 2 * BLK), F32)],
        scratch_shapes=[pltpu.VMEM((n_ch, N_BLK, SLAB, BLK), BF16), pltpu.VMEM((n_ch, SLAB, 2 * BLK), F32)],
        compiler_params=pltpu.CompilerParams(vmem_limit_bytes=VMEM_LIMIT),
    )(qb, kb, vb)


def out_loss(o_a, o_b, ga, gb, x, target, wout):
    n_tiles = T_LOC // TM_FWD

    def body(oa_ref, ob_ref, ga_ref, gb_ref, x_ref, t_ref, w_ref,
             dout_ref, doa_ref, dob_ref, dga_ref, dgb_ref, dw_ref, loss_ref, acc_ref):
        step = pl.program_id(0)

        @pl.when(step == 0)
        def _():
            acc_ref[...] = jnp.zeros_like(acc_ref)
            loss_ref[...] = jnp.zeros_like(loss_ref)

        oa, ob, gav, gbv = (r[...].astype(F32) for r in (oa_ref, ob_ref, ga_ref, gb_ref))
        sa, sb = _sigmoid(gav), _sigmoid(gbv)
        silu_a, silu_b = gav * sa, gbv * sb
        y = jnp.concatenate([oa * silu_a, ob * silu_b], axis=1).astype(BF16)
        err = x_ref[...] + _dot(y, w_ref[...]) - t_ref[...]
        e2 = err * err
        part = jnp.sum(e2.reshape(TM_FWD // 8, 8, D_MODEL), axis=0)
        loss_ref[...] += functools.reduce(lambda a, b: a + b, [part[:, k * 128:(k + 1) * 128] for k in range(8)])
        dout = err * (1.0 / D_MODEL)
        dout_ref[...] = dout
        dob16 = dout.astype(BF16)
        for r0 in range(0, D_MODEL, ACC_ROWS):
            acc_ref[r0:r0 + ACC_ROWS, :] += _dot(y[:, r0:r0 + ACC_ROWS], dob16, TN)
        dy = _dot(dob16, w_ref[...], NT)
        dya, dyb = dy[:, :512], dy[:, 512:]
        doa_ref[...] = (dya * silu_a).astype(BF16)
        dob_ref[...] = (dyb * silu_b).astype(BF16)
        dga_ref[...] = (dya * oa * (sa * (1.0 + gav * (1.0 - sa)))).astype(BF16)
        dgb_ref[...] = (dyb * ob * (sb * (1.0 + gbv * (1.0 - sb)))).astype(BF16)

        @pl.when(step == n_tiles - 1)
        def _():
            dw_ref[...] = acc_ref[...].astype(BF16)

    def tile(w):
        return pl.BlockSpec((TM_FWD, w), lambda i: (i, 0))

    const = lambda i: (0, 0)
    return pl.pallas_call(
        body, name="out_loss", grid=(n_tiles,),
        in_specs=[tile(512)] * 4 + [tile(D_MODEL)] * 2 + [pl.BlockSpec((D_MODEL, D_MODEL), const)],
        out_specs=[tile(D_MODEL), tile(512), tile(512), tile(512), tile(512),
                   pl.BlockSpec((D_MODEL, D_MODEL), const), pl.BlockSpec((8, 128), const)],
        out_shape=[jax.ShapeDtypeStruct((T_LOC, D_MODEL), F32)] + [jax.ShapeDtypeStruct((T_LOC, 512), BF16)] * 4
        + [jax.ShapeDtypeStruct((D_MODEL, D_MODEL), BF16), jax.ShapeDtypeStruct((8, 128), F32)],
        scratch_shapes=[pltpu.VMEM((D_MODEL, D_MODEL), F32)],
        compiler_params=_params(("arbitrary",)),
    )(o_a, o_b, ga, gb, x, target, wout)


def swa_bwd(q_rot, k_dup, v_dup, o_a, d_oa, sinks):
    def body(q_ref, kp_ref, kc_ref, vp_ref, vc_ref, o_ref, do_ref, sinks_ref, dq_ref, dk_ref, dv_ref, dsink_ref):
        i = pl.program_id(0)

        @pl.when(i == 0)
        def _():
            dk_ref[...] = jnp.zeros_like(dk_ref)
            dv_ref[...] = jnp.zeros_like(dv_ref)
            dsink_ref[...] = jnp.zeros_like(dsink_ref)

        lane = _lane((BLK, SLAB))
        rows_prev, rows_cur = _blk(jnp.maximum(i - 1, 0)), _blk(i)
        chains = range(len(SWA_CHAINS))
        q2 = [_stack_heads(q_ref[b, :, _slab(p)], lane) for b, p in SWA_CHAINS]
        do2 = [_stack_heads(do_ref[b, :, _slab(p)], lane) for b, p in SWA_CHAINS]
        keys = [_swa_window(kp_ref.at[b], kc_ref.at[b], p) for b, p in SWA_CHAINS]
        s = [_dot(q2[c], keys[c], NT) for c in chains]
        dp = [_dot(do2[c], _swa_window(vp_ref.at[b], vc_ref.at[b], p), NT) for c, (b, p) in enumerate(SWA_CHAINS)]
        ds, pn16, cols = [], [], []
        for c, (b, p) in enumerate(SWA_CHAINS):
            pn, p_sink = _swa_probs(s[c], sinks_ref, p, i)
            o = o_ref[b, :, _slab(p)].astype(F32)
            delta =jnp.sum(do2[c].astype(F32) * jnp.concatenate([o, o], axis=0), axis=-1, keepdims=True)
            ds.append((pn * (dp[c] - delta)).astype(BF16))
            pn16.append(pn.astype(BF16))
            cols.append(-p_sink * delta)
        for c, (b, p) in enumerate(SWA_CHAINS):
            dq_ref[b, :, _slab(p)] = _unstack_heads(_dot(ds[c], keys[c]), lane) * Q_SCALE
        dk2 = [_dot(ds[c], q2[c], TN) for c in chains]
        dv2 = [_dot(pn16[c], do2[c], TN) for c in chains]
        for c, (b, p) in enumerate(SWA_CHAINS):
            gsl = _slab(p // 2)
            dk_ref[b, rows_prev, gsl] += dk2[c][:BLK]
            dk_ref[b, rows_cur, gsl] += dk2[c][BLK:]
            dv_ref[b, rows_prev, gsl] += dv2[c][:BLK]
            dv_ref[b, rows_cur, gsl] += dv2[c][BLK:]
            for e in range(2):
                dsink_ref[2 * p + e:2 * p + e + 1, :] += jnp.sum(cols[c][e * BLK:(e + 1) * BLK], axis=0, keepdims=True)

    cur, prev = _swa_specs()
    whole = pl.BlockSpec((B_LOC, SEQ, 256), lambda i: (0, 0, 0))
    q3, k3, v3, o3, do3 = (a.reshape(B_LOC, SEQ, a.shape[1]) for a in (q_rot, k_dup, v_dup, o_a, d_oa))
    dq, dk, dv, dsink = pl.pallas_call(
        body, name="swa_bwd", grid=(N_BLK,),
        in_specs=[cur(512), prev(256), cur(256), prev(256), cur(256), cur(512), cur(512),
                  pl.BlockSpec(memory_space=pltpu.SMEM)],
        out_specs=[cur(512), whole, whole, pl.BlockSpec((8, 128), lambda i: (0, 0))],
        out_shape=[jax.ShapeDtypeStruct((B_LOC, SEQ, 512), F32), jax.ShapeDtypeStruct((B_LOC, SEQ, 256), F32),
                   jax.ShapeDtypeStruct((B_LOC, SEQ, 256), F32), jax.ShapeDtypeStruct((8, 128), F32)],
        compiler_params=_params(("arbitrary",)),
    )(q3, k3, k3, v3, v3, o3, do3, sinks)
    return dq.reshape(T_LOC, 512), dk.reshape(T_LOC, 256), dv.reshape(T_LOC, 256), dsink


def sb_bwd(qb, kb, vb, d_ob, carries, dwout):
    def body(q_ref, k_ref, v_ref, do_ref, c_ref, dw_hbm, dq_ref, dk_ref, dv_ref, rw_hbm, kt_ref, dqt_ref,
             rw_send, rw_recv, rw_local):
        start_dwout, finish_dwout = _direct_exchange(
            lambda dev: dw_hbm.at[pl.ds(_lin(dev) * OUT_SHARD, OUT_SHARD), :], lambda dev: rw_hbm.at[_lin(dev)],
            rw_send, rw_recv, rw_local)
        start_dwout()
        for c, (b, pp) in enumerate(SB_CHAINS):
            for j in range(N_BLK):
                kt_ref[c, j] = k_ref[b * SEQ + j * BLK:b * SEQ + (j + 1) * BLK, _slab(pp)].T
        dk_ref[...] = jnp.zeros_like(dk_ref)
        dv_ref[...] = jnp.zeros_like(dv_ref)
        dqt_ref[...] = jnp.zeros_like(dqt_ref)
        lane = _lane((BLK, SLAB))
        tri_after, tri_before = _tri(True), _tri(False)
        valid = _causal_t()
        jrow = _row((N_BLK, 2 * BLK))
        chains = range(len(SB_CHAINS))

        def q_block(i, _):
            q2 = [_stack_heads(q_ref[_sb_rows(b, i), _slab(pp)], lane) for b, pp in SB_CHAINS]
            do2 = [_stack_heads(do_ref[_sb_rows(b, i), _slab(pp)], lane) for b, pp in SB_CHAINS]

            def key_block(j, carry_sp, before_u, mask):
                at = [(_sb_rows(b, j), _slab(pp)) for b, pp in SB_CHAINS]
                zt = [_dot(k_ref[at[c]], q2[c], NT) for c in chains]
                dw = [_dot(v_ref[at[c]], do2[c], NT) for c in chains]
                sp = [_sb_softplus(zt[c], mask) for c in chains]
                later = [_key_sums(tri_after, sp[c])[0] for c in chains]
                w = [_sb_weights(zt[c], sp[c], later[c] + carry_sp[c], mask) for c in chains]
                u = [dw[c] * w[c] for c in chains]
                sums = [_key_sums(tri_before, u[c]) for c in chains]
                for c in chains:
                    dv_ref[at[c]] += _dot(w[c].astype(BF16), do2[c])
                dz16 = []
                for c in chains:
                    sig = jnp.exp(zt[c] - sp[c])
                    dz = u[c] - sig * (u[c] + before_u[c] + sums[c][0])
                    if mask is not None:
                        dz = jnp.where(mask, dz, 0.0)
                    dz16.append(dz.astype(BF16))
                for c in chains:
                    dk_ref[at[c]] += _dot(dz16[c], q2[c])
                    dqt_ref[c] += _dot(kt_ref[c, j], dz16[c])
                return tuple(before_u[c] + sums[c][1] for c in chains)

            def earlier(j, before_u):
                carry_sp = [jnp.sum(jnp.where(jrow == j, c_ref[c * N_BLK + i], 0.0), axis=0, keepdims=True)
                            for c in chains]
                return key_block(j, carry_sp, before_u, None)

            zero = tuple(jnp.zeros((1, 2 * BLK), F32) for _ in chains)
            before_u = lax.fori_loop(0, i, earlier, zero)
            key_block(i, zero, before_u, valid)
            for c, (b, pp) in enumerate(SB_CHAINS):
                dq_ref[_sb_rows(b, i), _slab(pp)] = (_unstack_heads(dqt_ref[c].T, lane) * Q_SCALE).astype(BF16)
                dqt_ref[c] = jnp.zeros((SLAB, 2 * BLK), F32)
            return 0

        lax.fori_loop(0, N_BLK, q_block, 0)
        finish_dwout()

    n_ch = len(SB_CHAINS)
    vmem, hbm = pl.BlockSpec(memory_space=pltpu.VMEM), pl.BlockSpec(memory_space=pl.ANY)
    return pl.pallas_call(
        body, name="sb_bwd",
        in_specs=[vmem] * 5 + [hbm], out_specs=[vmem] * 3 + [hbm],
        out_shape=[jax.ShapeDtypeStruct((T_LOC, 512), BF16)] + [jax.ShapeDtypeStruct((T_LOC, 512), F32)] * 2
        + [jax.ShapeDtypeStruct((N_DEV, OUT_SHARD, D_MODEL), BF16)],
        scratch_shapes=[pltpu.VMEM((n_ch, N_BLK, SLAB, BLK), BF16), pltpu.VMEM((n_ch, SLAB, 2 * BLK), F32),
                        pltpu.SemaphoreType.DMA((7,)), pltpu.SemaphoreType.DMA((7,)), pltpu.SemaphoreType.DMA],
        compiler_params=pltpu.CompilerParams(vmem_limit_bytes=VMEM_LIMIT),
    )(qb, kb, vb, d_ob, carries, dwout)


def bwd_dw(x, norm_gain, dq_rot, dk_dup, dv_dup, qa_raw, ka_raw, cos, sin_s, q_gain2, k_gain2, dga, dgb, dqb, dkb, dvb):
    n_tiles = T_LOC // TM_DW

    def body(x_ref, ng_ref, dq_ref, dk_ref, dv_ref, qa_ref, ka_ref, cos_ref, sin_ref, qg_ref, kg_ref,
             dga_ref, dgb_ref, dqb_ref, dkb_ref, dvb_ref,
             dproj_ref, dw_hbm, dqg_ref, dkg_ref, acc_ref, stage_ref):
        step = pl.program_id(0)

        @pl.when(step == 0)
        def _():
            acc_ref[...] = jnp.zeros_like(acc_ref)
            dqg_ref[...] = jnp.zeros_like(dqg_ref)
            dkg_ref[...] = jnp.zeros_like(dkg_ref)

        lane = _lane((TM_DW, SLAB))
        bd = _head_blockdiag()
        cos, sin_s = cos_ref[...], sin_ref[...]

        def norm_rope_bwd(d_rot, raw, gain2):
            dy = d_rot * cos + _swap_half(d_rot * sin_s, lane)
            r = lax.rsqrt(_head_sum(raw * raw, bd) * (1.0 / HEAD_DIM) + EPS)
            xhat = raw * r
            dgain = jnp.sum(dy * xhat, axis=0, keepdims=True)
            dxh = dy * gain2
            mean = _head_sum(dxh * xhat, bd) * (1.0 / HEAD_DIM)
            return r * (dxh - xhat * mean), dgain

        def fold_dup(d_dup):
            a, b2 = d_dup[:, :SLAB], d_dup[:, SLAB:]
            return jnp.where(lane < HEAD_DIM, a + pltpu.roll(a, HEAD_DIM, 1), b2 + pltpu.roll(b2, HEAD_DIM, 1))

        pieces = []
        dqg = jnp.zeros((1, SLAB), F32)
        for p in range(4):
            sl = slice(p * SLAB, (p + 1) * SLAB)
            d_raw, dg = norm_rope_bwd(dq_ref[:, sl], qa_ref[:, sl], qg_ref[...])
            pieces.append(d_raw.astype(BF16))
            dqg = dqg + dg
        d_raw, dkg = norm_rope_bwd(fold_dup(dk_ref[...]), ka_ref[...], kg_ref[...])
        pieces.append(d_raw.astype(BF16))
        pieces.append(fold_dup(dv_ref[...]).astype(BF16))
        pieces += [dga_ref[...], dqb_ref[...], dkb_ref[...].astype(BF16), dvb_ref[...].astype(BF16),
                   dgb_ref[...]]
        dproj = jnp.concatenate(pieces, axis=1)
        dproj_ref[...] = dproj
        dqg_ref[0:1, :] += dqg + pltpu.roll(dqg, HEAD_DIM, 1)
        dkg_ref[0:1, :] += dkg + pltpu.roll(dkg, HEAD_DIM, 1)

        xv = x_ref[...]
        rstd = lax.rsqrt(jnp.mean(xv * xv, axis=-1, keepdims=True) + EPS)
        h = (xv * rstd * ng_ref[...]).astype(BF16)
        for r0 in range(0, IN_WIDTH, ACC_ROWS):
            acc_ref[r0:r0 + ACC_ROWS, :] += _dot(dproj[:, r0:r0 + ACC_ROWS], h, TN)

        @pl.when(step == n_tiles - 1)
        def _():
            for r0 in range(0, IN_WIDTH, ACC_ROWS):
                stage_ref[...] = acc_ref[r0:r0 + ACC_ROWS, :].astype(BF16)
                pltpu.sync_copy(stage_ref, dw_hbm.at[r0:r0 + ACC_ROWS, :])

    def tile(w):
        return pl.BlockSpec((TM_DW, w), lambda i: (i, 0))

    def whole(a):
        return pl.BlockSpec(a.shape, lambda i: (0, 0))

    const = lambda i: (0, 0)
    return pl.pallas_call(
        body, name="bwd_dw", grid=(n_tiles,),
        in_specs=[tile(D_MODEL), whole(norm_gain),
                  tile(512), tile(256), tile(256), tile(512), tile(128), tile(128), tile(128),
                  whole(q_gain2), whole(k_gain2), tile(512), tile(512), tile(512), tile(512), tile(512)],
        out_specs=[tile(IN_WIDTH), pl.BlockSpec(memory_space=pl.ANY),
                   pl.BlockSpec((8, SLAB), const), pl.BlockSpec((8, SLAB), const)],
        out_shape=[jax.ShapeDtypeStruct((T_LOC, IN_WIDTH), BF16), jax.ShapeDtypeStruct((IN_WIDTH, D_MODEL), BF16),
                   jax.ShapeDtypeStruct((8, SLAB), F32), jax.ShapeDtypeStruct((8, SLAB), F32)],
        scratch_shapes=[pltpu.VMEM((IN_WIDTH, D_MODEL), F32), pltpu.VMEM((ACC_ROWS, D_MODEL), BF16)],
        compiler_params=_params(("arbitrary",)),
    )(x, norm_gain, dq_rot, dk_dup, dv_dup, qa_raw, ka_raw, cos, sin_s, q_gain2, k_gain2, dga, dgb, dqb, dkb, dvb)


def bwd_dx(x, dout, norm_gain, win_t, dproj, dwin_t, dqg, dkg, dsink, loss_part):
    n_tiles = T_LOC // TM
    rows_per = IN_SHARD
    step_sums, step_merge = 3, 7

    def body(x_ref, dout_ref, ng_ref, w_hbm, dp_ref, a_hbm, dqg_ref, dkg_ref, dsink_ref, loss_ref,
             gx_ref, ra_hbm, rs_hbm, w_ref, dng_ref, s_ref, own_ref, sib_ref, snd_ref, extra_ref,
             w_sem, d2d_send, d2d_recv, ici_send, ici_recv, own_sems, s_send, s_recv, out_sem):
        step = pl.program_id(0)
        x, y, c = _mesh_pos()
        me, sibling = (x, y, c), (x, y, 1 - c)
        chips = {"own": (x, y), "x": (1 - x, y), "y": (x, 1 - y), "d": (1 - x, 1 - y)}
        index = {"own": 0, "x": 1, "y": 2, "d": 3}
        order = ("d", "x", "y", "own")

        def rows(pos):
            return a_hbm.at[pl.ds(_lin(pos) * rows_per, rows_per), :]

        def to_sibling(k):
            return pltpu.make_async_remote_copy(
                src_ref=rows((*chips[k], 1 - c)), dst_ref=sib_ref.at[index[k]],
                send_sem=d2d_send.at[index[k]], recv_sem=d2d_recv.at[index[k]], device_id=sibling, device_id_type=MESH)

        def mine(k):
            return pltpu.make_async_copy(rows((*chips[k], c)), own_ref.at[index[k]], own_sems.at[index[k]])

        def ici(n, to_chip, dst):
            return pltpu.make_async_remote_copy(
                src_ref=snd_ref.at[n], dst_ref=dst, send_sem=ici_send.at[n], recv_sem=ici_recv.at[n],
                device_id=(*chips[to_chip], c), device_id_type=MESH)

        def chip_sum(k):
            to_sibling(k).wait_recv()
            mine(k).wait()
            return own_ref[index[k]].astype(F32) + sib_ref[index[k]].astype(F32)

        def by_core(fn):
            pl.when(c == 0)(lambda: fn("x", "y"))
            pl.when(c == 1)(lambda: fn("y", "x"))

        @pl.when(step == 0)
        def _():
            cp = pltpu.make_async_copy(w_hbm, w_ref, w_sem)
            cp.start()
            for k in order:
                to_sibling(k).start()
                mine(k).start()
            dng_ref[...] = jnp.zeros_like(dng_ref)
            cp.wait()

        @pl.when(step == step_sums)
        def _():
            def first_sends(direct, via):
                snd_ref[0] = chip_sum("d").astype(BF16)
                ici(0, direct, extra_ref).start()
                snd_ref[1] = chip_sum(direct).astype(BF16)
                ici(1, direct, ra_hbm.at[index[direct]]).start()
            by_core(first_sends)

        @pl.when(step == step_merge)
        def _():
            def merge(direct, via):
                merged = chip_sum(via)
                ici(0, direct, extra_ref).wait_recv()
                snd_ref[2] = (merged + extra_ref[...].astype(F32)).astype(BF16)
                ici(2, via, ra_hbm.at[index[via]]).start()
                own_ref[0] = chip_sum("own").astype(BF16)
                pltpu.make_async_copy(own_ref.at[0], ra_hbm.at[0], out_sem).start()
            by_core(merge)

        xv = x_ref[...]
        rstd = lax.rsqrt(jnp.mean(xv * xv, axis=-1, keepdims=True) + EPS)
        xhat = xv * rstd
        gain = ng_ref[...]
        dh = _dot(dp_ref[...], w_ref[...])
        dng_ref[0:1, :] += jnp.sum(dh * xhat, axis=0, keepdims=True)
        dxh = dh * gain
        gx_ref[...] = dout_ref[...] + rstd * (dxh - xhat * jnp.mean(dxh * xhat, axis=-1, keepdims=True))

        @pl.when(step == n_tiles - 1)
        def _():
            s_ref[...] = jnp.concatenate(
                [dng_ref[...], dqg_ref[...], dkg_ref[...], dsink_ref[...], loss_ref[...]], axis=1)
            start_small, finish_small = _direct_exchange(
                lambda dev: s_ref, lambda dev: rs_hbm.at[_lin(dev)], s_send, s_recv, out_sem)

            def finish(direct, via):
                ici(1, direct, ra_hbm.at[index[direct]]).wait_recv()
                ici(2, via, ra_hbm.at[index[via]]).wait_recv()
                for n, to in ((0, direct), (1, direct), (2, via)):
                    ici(n, to, extra_ref).wait_send()
            by_core(finish)
            pltpu.make_async_copy(own_ref.at[0], ra_hbm.at[0], out_sem).wait()
            for k in order:
                to_sibling(k).wait_send()
            start_small()
            finish_small()

    def tile(w):
        return pl.BlockSpec((TM, w), lambda i: (i, 0))

    def whole(a):
        return pl.BlockSpec(a.shape, lambda i: (0, 0))

    hbm = pl.BlockSpec(memory_space=pl.ANY)
    block = (rows_per, D_MODEL)
    return pl.pallas_call(
        body, name="bwd_dx", grid=(n_tiles,),
        in_specs=[tile(D_MODEL), tile(D_MODEL), whole(norm_gain), hbm, tile(IN_WIDTH), hbm,
                  whole(dqg), whole(dkg), whole(dsink), whole(loss_part)],
        out_specs=[tile(D_MODEL), hbm, hbm],
        out_shape=[jax.ShapeDtypeStruct((T_LOC, D_MODEL), F32), jax.ShapeDtypeStruct((3,) + block, BF16),
                   jax.ShapeDtypeStruct((N_DEV, 8, SMALL_W), F32)],
        scratch_shapes=[pltpu.VMEM((IN_WIDTH, D_MODEL), BF16), pltpu.VMEM((8, D_MODEL), F32),
                        pltpu.VMEM((8, SMALL_W), F32),
                        pltpu.VMEM((4,) + block, BF16), pltpu.VMEM((4,) + block, BF16), pltpu.VMEM((3,) + block, BF16),
                        pltpu.VMEM(block, BF16),
                        pltpu.SemaphoreType.DMA, pltpu.SemaphoreType.DMA((4,)), pltpu.SemaphoreType.DMA((4,)),
                        pltpu.SemaphoreType.DMA((3,)), pltpu.SemaphoreType.DMA((3,)), pltpu.SemaphoreType.DMA((4,)),
                        pltpu.SemaphoreType.DMA((7,)), pltpu.SemaphoreType.DMA((7,)), pltpu.SemaphoreType.DMA],
        compiler_params=_params(("arbitrary",)),
    )(x, dout, norm_gain, win_t, dproj, dwin_t, dqg, dkg, dsink, loss_part)


def _adamw(w, g, m, v):
    m = ADAM_B1 * m + (1.0 - ADAM_B1) * g
    v = ADAM_B2 * v + (1.0 - ADAM_B2) * (g * g)
    m_hat = m / (1.0 - ADAM_B1 ** ADAM_STEP)
    v_hat = v / (1.0 - ADAM_B2 ** ADAM_STEP)
    delta = -ADAM_LR * (m_hat / (jnp.sqrt(v_hat) + ADAM_EPS) + ADAM_WD * w)
    return delta, m, v


def _sum_slots(r_ref):
    g = r_ref[0].astype(F32)
    for s in range(1, r_ref.shape[0]):
        g = g + r_ref[s].astype(F32)
    return g


def adamw_rows(name, recv, w, m, v):
    def body(r_ref, w_ref, m_ref, v_ref, g_ref, d_ref, nm_ref, nv_ref):
        g = _sum_slots(r_ref)
        g_ref[...] = g
        d_ref[...], nm_ref[...], nv_ref[...] = _adamw(w_ref[...], g, m_ref[...], v_ref[...])

    return pl.pallas_call(
        body, name=name,
        out_shape=[jax.ShapeDtypeStruct(w.shape, F32)] * 4,
        compiler_params=pltpu.CompilerParams(vmem_limit_bytes=VMEM_LIMIT),
    )(recv, w, m, v)


def adamw_small(recv, weights, moments_m, moments_v):
    n = len(weights)

    def body(r_ref, *refs):
        ins, outs = refs[:3 * n], refs[3 * n:]
        s = _sum_slots(r_ref)
        eye = (_row((8, SLAB)) == _lane((8, SLAB))).astype(F32)
        sinks = jnp.sum(s[:, 1280:1408] * eye, axis=0, keepdims=True)
        grads = [s[0:1, :D_MODEL], s[0:1, 1024:1024 + HEAD_DIM], s[0:1, 1152:1152 + HEAD_DIM], sinks[:, :8]]
        for k in range(n):
            outs[k][...] = grads[k]
            outs[n + k][...], outs[2 * n + k][...], outs[3 * n + k][...] = _adamw(
                ins[k][...], grads[k], ins[n + k][...], ins[2 * n + k][...])
        loss = jnp.sum(jnp.sum(s[:, 1408:1536], axis=1, keepdims=True), axis=0, keepdims=True) * (0.5 / D_MODEL)
        outs[4 * n][...] = loss

    res = pl.pallas_call(
        body, name="adamw_small",
        out_shape=[jax.ShapeDtypeStruct(w.shape, F32) for w in weights] * 4 + [jax.ShapeDtypeStruct((1, 1), F32)],
        compiler_params=pltpu.CompilerParams(vmem_limit_bytes=VMEM_LIMIT),
    )(recv, *weights, *moments_m, *moments_v)
    return res[:n], res[n:2 * n], res[2 * n:3 * n], res[3 * n:4 * n], res[4 * n]


def kernel(x, positions, norm_gain, w_in, q_norm_gain, k_norm_gain, sinks, w_out, loss_target, m_norm_gain, m_w_in, m_q_norm_gain, m_k_norm_gain, m_sinks, m_w_out, v_norm_gain, v_w_in, v_q_norm_gain, v_k_norm_gain, v_sinks, v_w_out):
    x2 = x.reshape(T_LOC, D_MODEL)
    tgt2 = loss_target.reshape(T_LOC, D_MODEL)
    pos2 = positions.reshape(1, T_LOC)
    half = HEAD_DIM // 2
    inv_freq = ROPE_THETA ** (-jnp.arange(half, dtype=F32) * 2.0 / HEAD_DIM)
    inv_freq = jnp.tile(inv_freq, SLAB // half).reshape(SLAB, 1)
    sin_sign = jnp.tile(jnp.concatenate([-jnp.ones((half,), F32), jnp.ones((half,), F32)]), 2).reshape(1, SLAB)
    q_gain2 = jnp.tile(q_norm_gain, (1, 2))
    k_gain2 = jnp.tile(k_norm_gain, (1, 2))

    win_t = gather_weights(w_in.reshape(D_MODEL, IN_SHARD).T)

    (qa_raw, ka_raw, q_rot, k_dup, v_dup, ga, qb, kb, vb, gb, cos, sin_s, wout) = fwd_proj(
        x2, pos2, norm_gain, win_t, inv_freq, sin_sign, q_gain2, k_gain2, w_out.reshape(OUT_SHARD, D_MODEL).astype(BF16))
    o_a = swa_fwd(q_rot, k_dup, v_dup, sinks)
    o_b, carries = sb_fwd(qb, kb, vb)
    dout, d_oa, d_ob, dga, dgb, dwout, loss_part = out_loss(o_a, o_b, ga, gb, x2, tgt2, wout)
    dq_rot, dk_dup, dv_dup, dsink = swa_bwd(q_rot, k_dup, v_dup, o_a, d_oa, sinks)
    dqb, dkb, dvb, r_out = sb_bwd(qb, kb, vb, d_ob, carries, dwout)
    dproj, dwin_t, dqg, dkg = bwd_dw(
        x2, norm_gain, dq_rot, dk_dup, dv_dup, qa_raw, ka_raw, cos, sin_s, q_gain2, k_gain2, dga, dgb, dqb, dkb, dvb)
    grad_x, r_win, r_small = bwd_dx(x2, dout, norm_gain, win_t, dproj, dwin_t, dqg, dkg, dsink, loss_part)

    w_in2, m_in2, v_in2 = (a.reshape(D_MODEL, IN_SHARD).T for a in (w_in, m_w_in, v_w_in))
    w_out2, m_out2, v_out2 = (a.reshape(OUT_SHARD, D_MODEL) for a in (w_out, m_w_out, v_w_out))
    big_in = adamw_rows("adamw_w_in", r_win, w_in2, m_in2, v_in2)
    big_out = adamw_rows("adamw_w_out", r_out, w_out2, m_out2, v_out2)
    *small_out, loss = adamw_small(
        r_small, (norm_gain, q_norm_gain, k_norm_gain, sinks),
        (m_norm_gain, m_q_norm_gain, m_k_norm_gain, m_sinks), (v_norm_gain, v_q_norm_gain, v_k_norm_gain, v_sinks))

    def leaves(k):
        ng, qg, kg, sk = small_out[k]
        return (ng, big_in[k].T.reshape(1, D_MODEL, IN_SHARD), qg, kg, sk, big_out[k].reshape(1, OUT_SHARD, D_MODEL))

    return (loss.reshape(()), grad_x.reshape(B_LOC, SEQ, D_MODEL), *leaves(0), *leaves(1), *leaves(2), *leaves(3))
```

```python
import functools

import jax
import jax.numpy as jnp
from jax import lax
from jax.experimental import pallas as pl
from jax.experimental.pallas import tpu as pltpu

F32 = jnp.float32
BF16 = jnp.bfloat16

N_DEV = 8
D_MODEL = 1024
SEQ = 2048
B_LOC = 2
T_LOC = B_LOC * SEQ
HEAD_DIM = 64
HEAD_SHIFT = 6
BLK = 128
N_BLK = SEQ // BLK
SLAB = 128
IN_WIDTH = 3328
IN_SHARD = IN_WIDTH // N_DEV
OUT_SHARD = D_MODEL // N_DEV
EPS = 1e-6
ROPE_THETA = 10000.0
Q_SCALE = 0.125
R_QA, R_KA, R_VA, R_GA, R_QB, R_KB, R_VB, R_GB, R_END = 0, 512, 640, 768, 1280, 1792, 2304, 2816, 3328
SMALL_W = 1536
ADAM_LR, ADAM_B1, ADAM_B2, ADAM_EPS, ADAM_WD, ADAM_STEP = 0.001, 0.9, 0.999, 1e-08, 0.01, 10
TM = 256
TM_DW = 512
TM_FWD = 512
ACC_ROWS = 256
VMEM_LIMIT = 56 * 1024 * 1024

MESH = pl.DeviceIdType.MESH
NT = (((1,), (1,)), ((), ()))
TN = (((0,), (0,)), ((), ()))


def _params(sem, limit=VMEM_LIMIT):
    return pltpu.CompilerParams(dimension_semantics=sem, vmem_limit_bytes=limit)


def _dot(a, b, dims=None):
    if dims is None:
        return jnp.dot(a, b, preferred_element_type=F32)
    return lax.dot_general(a, b, dims, preferred_element_type=F32)


def _lane(shape):
    return lax.broadcasted_iota(jnp.int32, shape, len(shape) - 1)


def _row(shape):
    return lax.broadcasted_iota(jnp.int32, shape, 0)


def _head_blockdiag():
    return ((_row((SLAB, SLAB)) >> HEAD_SHIFT) == (_lane((SLAB, SLAB)) >> HEAD_SHIFT)).astype(BF16)


def _head_sum(x, bd):
    return _dot(x.astype(BF16), bd)


def _swap_half(y, lane):
    return jnp.where((lane & 32) != 0, pltpu.roll(y, 32, 1), pltpu.roll(y, 96, 1))


def _stack_heads(q, lane):
    zero = jnp.zeros_like(q)
    return jnp.concatenate([jnp.where(lane < HEAD_DIM, q, zero), jnp.where(lane >= HEAD_DIM, q, zero)], axis=0)


def _unstack_heads(x2, lane):
    return jnp.where(lane < HEAD_DIM, x2[:BLK], x2[BLK:])


def _sigmoid(x):
    return 1.0 / (1.0 + jnp.exp(-x))


def _mesh_pos():
    return lax.axis_index("x"), lax.axis_index("y"), lax.axis_index("c")


def _flip(pos, mask):
    return tuple(1 - p if m else p for p, m in zip(pos, mask))


def _lin(pos):
    return 4 * pos[0] + 2 * pos[1] + pos[2]


DEV_FLIPS = [(fx, fy, fc) for fx in (0, 1) for fy in (0, 1) for fc in (0, 1)][1:]


def _direct_exchange(src_for, dst_slot, send_sems, recv_sems, local_sem):
    me = _mesh_pos()

    def copy(k, to):
        return pltpu.make_async_remote_copy(
            src_ref=src_for(to), dst_ref=dst_slot(me), send_sem=send_sems.at[k], recv_sem=recv_sems.at[k],
            device_id=to, device_id_type=MESH)

    def landed(k, frm):
        return pltpu.make_async_remote_copy(
            src_ref=src_for(frm), dst_ref=dst_slot(frm), send_sem=send_sems.at[k], recv_sem=recv_sems.at[k],
            device_id=frm, device_id_type=MESH)

    local = None if local_sem is None else pltpu.make_async_copy(src_for(me), dst_slot(me), local_sem)
    peers = [_flip(me, f) for f in DEV_FLIPS]

    def start():
        if local is not None:
            local.start()
        for k, to in enumerate(peers):
            copy(k, to).start()

    def finish():
        for k, frm in enumerate(peers):
            landed(k, frm).wait_recv()
        for k, to in enumerate(peers):
            copy(k, to).wait_send()
        if local is not None:
            local.wait()

    return start, finish


def gather_weights(shard):
    m = shard.shape[0]

    def body(f32_ref, o_ref, a_ref, ici_send, ici_recv, d2d_send, d2d_recv, local_sem):
        a_ref[...] = f32_ref[...].astype(BF16)
        x, y, c = _mesh_pos()
        me, sibling = (x, y, c), (x, y, 1 - c)
        chip_x, chip_y, chip_d = (1 - x, y), (x, 1 - y), (1 - x, 1 - y)

        def rows(pos):
            return o_ref.at[pl.ds(_lin(pos) * m, m), :]

        def ici(k, block, to, src=None):
            return pltpu.make_async_remote_copy(
                src_ref=rows(block) if src is None else src, dst_ref=rows(block),
                send_sem=ici_send.at[k], recv_sem=ici_recv.at[k], device_id=to, device_id_type=MESH)

        def d2d(k, chip, mine, src=None):
            block = (*chip, c) if mine else (*chip, 1 - c)
            return pltpu.make_async_remote_copy(
                src_ref=rows(block) if src is None else src, dst_ref=rows(block),
                send_sem=d2d_send.at[k], recv_sem=d2d_recv.at[k], device_id=sibling, device_id_type=MESH)

        local = pltpu.make_async_copy(a_ref, rows(me), local_sem)
        local.start()
        sends = [ici(0, me, (*chip_x, c), src=a_ref), ici(1, me, (*chip_y, c), src=a_ref),
                 d2d(0, (x, y), True, src=a_ref)]
        for cp in sends:
            cp.start()

        def pass_on(first, k_first, second, k_second, onward):
            ici(k_first, (*first, c), me).wait_recv()
            relay = ici(2, (*first, c), (*onward, c))
            relay.start()
            hand = [d2d(1 + k_first, first, True)]
            hand[0].start()
            ici(k_second, (*second, c), me).wait_recv()
            hand.append(d2d(1 + k_second, second, True))
            hand[1].start()
            ici(2, (*chip_d, c), me).wait_recv()
            hand.append(d2d(3, chip_d, True))
            hand[2].start()
            for cp in [relay] + hand:
                cp.wait_send()

        @pl.when(c == 0)
        def _():
            pass_on(chip_y, 1, chip_x, 0, chip_x)

        @pl.when(c == 1)
        def _():
            pass_on(chip_x, 0, chip_y, 1, chip_y)

        for k, chip in enumerate([(x, y), chip_x, chip_y, chip_d]):
            d2d(k, chip, False).wait_recv()
        for cp in sends:
            cp.wait_send()
        local.wait()

    vmem = pl.BlockSpec(memory_space=pltpu.VMEM)
    return pl.pallas_call(
        body, name="gather_weights",
        out_shape=jax.ShapeDtypeStruct((N_DEV * m, shard.shape[1]), BF16),
        in_specs=[vmem], out_specs=vmem,
        scratch_shapes=[pltpu.VMEM(shard.shape, BF16), pltpu.SemaphoreType.DMA((3,)), pltpu.SemaphoreType.DMA((3,)),
                        pltpu.SemaphoreType.DMA((4,)), pltpu.SemaphoreType.DMA((4,)), pltpu.SemaphoreType.DMA],
        compiler_params=pltpu.CompilerParams(vmem_limit_bytes=VMEM_LIMIT),
    )(shard)


def _norm_rope(xs, gain2, cos, sin_s, bd, lane):
    r = lax.rsqrt(_head_sum(xs * xs, bd) * (1.0 / HEAD_DIM) + EPS)
    y = xs * r * gain2
    return y * cos + _swap_half(y, lane) * sin_s


def _dup_heads(xs, lane):
    r = pltpu.roll(xs, HEAD_DIM, 1)
    lo = lane < HEAD_DIM
    return jnp.concatenate([jnp.where(lo, xs, r), jnp.where(lo, r, xs)], axis=1)


def fwd_proj(x, pos, norm_gain, win_t, inv_freq, sin_sign, q_gain2, k_gain2, wout_shard):
    n_tiles = T_LOC // TM_FWD

    def body(x_ref, pos_ref, ng_ref, w_ref, if_ref, sg_ref, qg_ref, kg_ref, ws_hbm,
             qa_raw_ref, ka_raw_ref, q_rot_ref, k_dup_ref, v_dup_ref, ga_ref, qb_ref, kb_ref, vb_ref, gb_ref,
             cos_ref, sin_ref, wo_hbm, wo_send, wo_recv, wo_local):
        start_wout, finish_wout = _direct_exchange(
            lambda dev: ws_hbm, lambda dev: wo_hbm.at[pl.ds(_lin(dev) * OUT_SHARD, OUT_SHARD), :],
            wo_send, wo_recv, wo_local)
        pl.when(pl.program_id(0) == 0)(start_wout)

        xv = x_ref[...]
        rstd = lax.rsqrt(jnp.mean(xv * xv, axis=-1, keepdims=True) + EPS)
        h = (xv * rstd * ng_ref[...]).astype(BF16)

        def proj(r0, r1):
            return _dot(h, w_ref[r0:r1, :], NT)

        ang_t = if_ref[...] * pos_ref[...].astype(F32)
        cos = jnp.cos(ang_t).T
        sin_s = jnp.sin(ang_t).T * sg_ref[...]
        cos_ref[...] = cos
        sin_ref[...] = sin_s
        lane = _lane((TM_FWD, SLAB))
        bd = _head_blockdiag()

        qa = proj(R_QA, R_KA)
        qa_raw_ref[...] = qa
        for p in range(4):
            sl = slice(p * SLAB, (p + 1) * SLAB)
            q_rot_ref[:, sl] = (_norm_rope(qa[:, sl], qg_ref[...], cos, sin_s, bd, lane) * Q_SCALE).astype(BF16)
        ka = proj(R_KA, R_VA)
        ka_raw_ref[...] = ka
        k_dup_ref[...] = _dup_heads(_norm_rope(ka, kg_ref[...], cos, sin_s, bd, lane), lane).astype(BF16)
        v_dup_ref[...] = _dup_heads(proj(R_VA, R_GA), lane).astype(BF16)
        ga_ref[...] = proj(R_GA, R_QB).astype(BF16)
        qb_ref[...] = (proj(R_QB, R_KB) * Q_SCALE).astype(BF16)
        kb_ref[...] = proj(R_KB, R_VB).astype(BF16)
        vb_ref[...] = proj(R_VB, R_GB).astype(BF16)
        gb_ref[...] = proj(R_GB, R_END).astype(BF16)
        pl.when(pl.program_id(0) == n_tiles - 1)(finish_wout)

    def tile(w):
        return pl.BlockSpec((TM_FWD, w), lambda i: (i, 0))

    def whole(a):
        return pl.BlockSpec(a.shape, lambda i: (0, 0))

    hbm = pl.BlockSpec(memory_space=pl.ANY)
    widths = [(512, F32), (128, F32), (512, BF16), (256, BF16), (256, BF16), (512, BF16), (512, BF16), (512, BF16),
              (512, BF16), (512, BF16), (128, F32), (128, F32)]
    return pl.pallas_call(
        body, name="fwd_proj", grid=(n_tiles,),
        in_specs=[tile(D_MODEL), pl.BlockSpec((1, TM_FWD), lambda i: (0, i)), whole(norm_gain), whole(win_t),
                  whole(inv_freq), whole(sin_sign),
                  whole(q_gain2), whole(k_gain2), hbm],
        out_specs=[tile(w) for w, _ in widths] + [hbm],
        out_shape=[jax.ShapeDtypeStruct((T_LOC, w), dt) for w, dt in widths]
        + [jax.ShapeDtypeStruct((D_MODEL, D_MODEL), BF16)],
        scratch_shapes=[pltpu.SemaphoreType.DMA((7,)), pltpu.SemaphoreType.DMA((7,)), pltpu.SemaphoreType.DMA],
        compiler_params=_params(("arbitrary",)),
    )(x, pos, norm_gain, win_t, inv_freq, sin_sign, q_gain2, k_gain2, wout_shard)


def _swa_window(prev_ref, cur_ref, p):
    gsl = _slab(p // 2)
    return jnp.concatenate([prev_ref[:, gsl], cur_ref[:, gsl]], axis=0)


def _swa_probs(s, sinks_ref, p, i):
    shape = (2 * BLK, 2 * BLK)
    r = _row(shape) & (BLK - 1)
    cidx = _lane(shape)
    valid = (cidx > r) & (cidx <= r + BLK) & ((cidx >= BLK) | (i > 0))
    s = jnp.where(valid, s, -jnp.inf)
    sink = jnp.where(_row((2 * BLK, 1)) < BLK, sinks_ref[0, 2 * p], sinks_ref[0, 2 * p + 1])
    m = jnp.maximum(jnp.max(s, axis=-1, keepdims=True), sink)
    e = jnp.exp(s - m)
    e_sink = jnp.exp(sink - m)
    den = jnp.sum(e, axis=-1, keepdims=True) + e_sink
    return e / den, e_sink / den


SWA_CHAINS = [(b, p) for b in range(B_LOC) for p in range(4)]


def _swa_specs():
    def cur(w):
        return pl.BlockSpec((B_LOC, BLK, w), lambda i: (0, i, 0))

    def prev(w):
        return pl.BlockSpec((B_LOC, BLK, w), lambda i: (0, jnp.maximum(i - 1, 0), 0))

    return cur, prev


def swa_fwd(q_rot, k_dup, v_dup, sinks):
    def body(q_ref, kp_ref, kc_ref, vp_ref, vc_ref, sinks_ref, o_ref):
        i = pl.program_id(0)
        lane = _lane((BLK, SLAB))
        s = [_dot(_stack_heads(q_ref[b, :, _slab(p)], lane), _swa_window(kp_ref.at[b], kc_ref.at[b], p), NT)
             for b, p in SWA_CHAINS]
        pn = [_swa_probs(s[c], sinks_ref, p, i)[0].astype(BF16) for c, (b, p) in enumerate(SWA_CHAINS)]
        for c, (b, p) in enumerate(SWA_CHAINS):
            o = _unstack_heads(_dot(pn[c], _swa_window(vp_ref.at[b], vc_ref.at[b], p)), lane)
            o_ref[b, :, _slab(p)] = o.astype(BF16)

    cur, prev = _swa_specs()
    q3, k3, v3 = (a.reshape(B_LOC, SEQ, a.shape[1]) for a in (q_rot, k_dup, v_dup))
    return pl.pallas_call(
        body, name="swa_fwd", grid=(N_BLK,),
        in_specs=[cur(512), prev(256), cur(256), prev(256), cur(256), pl.BlockSpec(memory_space=pltpu.SMEM)],
        out_specs=cur(512),
        out_shape=jax.ShapeDtypeStruct((B_LOC, SEQ, 512), BF16),
        compiler_params=_params(("arbitrary",)),
    )(q3, k3, k3, v3, v3, sinks).reshape(T_LOC, 512)


def _tri(suffix):
    r, cidx = _row((BLK + 16, BLK)), _lane((BLK + 16, BLK))
    tri = (cidx > r) if suffix else (cidx < r)
    return (tri | (r >= BLK)).astype(BF16)


def _key_sums(tri, x):
    res = _dot(tri, x.astype(BF16))
    return res[:BLK], res[BLK:BLK + 1]


def _sb_softplus(zt, valid):
    neg_abs = lax.bitcast_convert_type(lax.bitcast_convert_type(zt, jnp.uint32) | jnp.uint32(0x80000000), F32)
    sp = jnp.maximum(zt, 0.0) + jnp.log(1.0 + jnp.exp(neg_abs))
    return sp if valid is None else jnp.where(valid, sp, 0.0)


def _sb_weights(zt, sp, later, valid):
    w = jnp.exp(zt - sp - later)
    return w if valid is None else jnp.where(valid, w, 0.0)


def _slab(pp):
    return slice(pp * SLAB, (pp + 1) * SLAB)


def _blk(j):
    return pl.ds(pl.multiple_of(j * BLK, BLK), BLK)


def _causal_t():
    return _row((BLK, 2 * BLK)) < (_lane((BLK, 2 * BLK)) & (BLK - 1))


def _sb_rows(b, j):
    return pl.ds(pl.multiple_of(b * SEQ + j * BLK, BLK), BLK)


SB_CHAINS = [(b, pp) for b in range(B_LOC) for pp in range(4)]


def sb_fwd(qb, kb, vb):
    def body(q_ref, k_ref, v_ref, o_ref, c_ref, vt_ref, ot_ref):
        for c, (b, pp) in enumerate(SB_CHAINS):
            for j in range(N_BLK):
                vt_ref[c, j] = v_ref[b * SEQ + j * BLK:b * SEQ + (j + 1) * BLK, _slab(pp)].T
        lane = _lane((BLK, SLAB))
        tri = _tri(True)
        valid = _causal_t()
        jrow = _row((N_BLK, 2 * BLK))
        chains = range(len(SB_CHAINS))

        def q_block(i, _):
            q2 = [_stack_heads(q_ref[_sb_rows(b, i), _slab(pp)], lane) for b, pp in SB_CHAINS]

            def key_block(j, carry, mask, first):
                zt = [_dot(k_ref[_sb_rows(b, j), _slab(pp)], q2[c], NT) for c, (b, pp) in enumerate(SB_CHAINS)]
                sp = [_sb_softplus(zt[c], mask) for c in chains]
                sums = [_key_sums(tri, sp[c]) for c in chains]
                w = [_sb_weights(zt[c], sp[c], sums[c][0] + carry[c], mask) for c in chains]
                for c in chains:
                    pv = _dot(vt_ref[c, j], w[c].astype(BF16))
                    if first:
                        ot_ref[c] = pv
                    else:
                        ot_ref[c] += pv
                return tuple(carry[c] + sums[c][1] for c in chains)

            def earlier(jj, state):
                carry, saved = state
                j = i - 1 - jj
                saved = tuple(jnp.where(jrow == j, carry[c], saved[c]) for c in chains)
                return key_block(j, carry, None, False), saved

            zero = tuple(jnp.zeros((1, 2 * BLK), F32) for _ in chains)
            carry = key_block(i, zero, valid, True)
            _, saved = lax.fori_loop(0, i, earlier, (carry, tuple(jnp.zeros((N_BLK, 2 * BLK), F32) for _ in chains)))
            for c, (b, pp) in enumerate(SB_CHAINS):
                o_ref[_sb_rows(b, i), _slab(pp)] = _unstack_heads(ot_ref[c].T, lane).astype(BF16)
                c_ref[c * N_BLK + i] = saved[c]
            return 0

        lax.fori_loop(0, N_BLK, q_block, 0)

    n_ch = len(SB_CHAINS)
    vmem = pl.BlockSpec(memory_space=pltpu.VMEM)
    return pl.pallas_call(
        body, name="sb_fwd",
        in_specs=[vmem] * 3, out_specs=[vmem] * 2,
        out_shape=[jax.ShapeDtypeStruct((T_LOC, 512), BF16), jax.ShapeDtypeStruct((n_ch * N_BLK, N_BLK, 2 * BLK), F32)],
        scratch_shapes=[pltpu.VMEM((n_ch, N_BLK, SLAB, BLK), BF16), pltpu.VMEM((n_ch, SLAB, 2 * BLK), F32)],
        compiler_params=pltpu.CompilerParams(vmem_limit_bytes=VMEM_LIMIT),
    )(qb, kb, vb)


def out_loss(o_a, o_b, ga, gb, x, target, wout):
    n_tiles = T_LOC // TM_FWD

    def body(oa_ref, ob_ref, ga_ref, gb_ref, x_ref, t_ref, w_ref,
             dout_ref, doa_ref, dob_ref, dga_ref, dgb_ref, dw_ref, loss_ref, acc_ref):
        step = pl.program_id(0)

        @pl.when(step == 0)
        def _():
            acc_ref[...] = jnp.zeros_like(acc_ref)
            loss_ref[...] = jnp.zeros_like(loss_ref)

        oa, ob, gav, gbv = (r[...].astype(F32) for r in (oa_ref, ob_ref, ga_ref, gb_ref))
        sa, sb = _sigmoid(gav), _sigmoid(gbv)
        silu_a, silu_b = gav * sa, gbv * sb
        y = jnp.concatenate([oa * silu_a, ob * silu_b], axis=1).astype(BF16)
        err = x_ref[...] + _dot(y, w_ref[...]) - t_ref[...]
        e2 = err * err
        part = jnp.sum(e2.reshape(TM_FWD // 8, 8, D_MODEL), axis=0)
        loss_ref[...] += functools.reduce(lambda a, b: a + b, [part[:, k * 128:(k + 1) * 128] for k in range(8)])
        dout = err * (1.0 / D_MODEL)
        dout_ref[...] = dout
        dob16 = dout.astype(BF16)
        for r0 in range(0, D_MODEL, ACC_ROWS):
            acc_ref[r0:r0 + ACC_ROWS, :] += _dot(y[:, r0:r0 + ACC_ROWS], dob16, TN)
        dy = _dot(dob16, w_ref[...], NT)
        dya, dyb = dy[:, :512], dy[:, 512:]
        doa_ref[...] = (dya * silu_a).astype(BF16)
        dob_ref[...] = (dyb * silu_b).astype(BF16)
        dga_ref[...] = (dya * oa * (sa * (1.0 + gav * (1.0 - sa)))).astype(BF16)
        dgb_ref[...] = (dyb * ob * (sb * (1.0 + gbv * (1.0 - sb)))).astype(BF16)

        @pl.when(step == n_tiles - 1)
        def _():
            dw_ref[...] = acc_ref[...].astype(BF16)

    def tile(w):
        return pl.BlockSpec((TM_FWD, w), lambda i: (i, 0))

    const = lambda i: (0, 0)
    return pl.pallas_call(
        body, name="out_loss", grid=(n_tiles,),
        in_specs=[tile(512)] * 4 + [tile(D_MODEL)] * 2 + [pl.BlockSpec((D_MODEL, D_MODEL), const)],
        out_specs=[tile(D_MODEL), tile(512), tile(512), tile(512), tile(512),
                   pl.BlockSpec((D_MODEL, D_MODEL), const), pl.BlockSpec((8, 128), const)],
        out_shape=[jax.ShapeDtypeStruct((T_LOC, D_MODEL), F32)] + [jax.ShapeDtypeStruct((T_LOC, 512), BF16)] * 4
        + [jax.ShapeDtypeStruct((D_MODEL, D_MODEL), BF16), jax.ShapeDtypeStruct((8, 128), F32)],
        scratch_shapes=[pltpu.VMEM((D_MODEL, D_MODEL), F32)],
        compiler_params=_params(("arbitrary",)),
    )(o_a, o_b, ga, gb, x, target, wout)


def swa_bwd(q_rot, k_dup, v_dup, o_a, d_oa, sinks):
    def body(q_ref, kp_ref, kc_ref, vp_ref, vc_ref, o_ref, do_ref, sinks_ref, dq_ref, dk_ref, dv_ref, dsink_ref):
        i = pl.program_id(0)

        @pl.when(i == 0)
        def _():
            dk_ref[...] = jnp.zeros_like(dk_ref)
            dv_ref[...] = jnp.zeros_like(dv_ref)
            dsink_ref[...] = jnp.zeros_like(dsink_ref)

        lane = _lane((BLK, SLAB))
        rows_prev, rows_cur = _blk(jnp.maximum(i - 1, 0)), _blk(i)
        chains = range(len(SWA_CHAINS))
        q2 = [_stack_heads(q_ref[b, :, _slab(p)], lane) for b, p in SWA_CHAINS]
        do2 = [_stack_heads(do_ref[b, :, _slab(p)], lane) for b, p in SWA_CHAINS]
        keys = [_swa_window(kp_ref.at[b], kc_ref.at[b], p) for b, p in SWA_CHAINS]
        s = [_dot(q2[c], keys[c], NT) for c in chains]
        dp = [_dot(do2[c], _swa_window(vp_ref.at[b], vc_ref.at[b], p), NT) for c, (b, p) in enumerate(SWA_CHAINS)]
        ds, pn16, cols = [], [], []
        for c, (b, p) in enumerate(SWA_CHAINS):
            pn, p_sink = _swa_probs(s[c], sinks_ref, p, i)
            o = o_ref[b, :, _slab(p)].astype(F32)
            delta =jnp.sum(do2[c].astype(F32) * jnp.concatenate([o, o], axis=0), axis=-1, keepdims=True)
            ds.append((pn * (dp[c] - delta)).astype(BF16))
            pn16.append(pn.astype(BF16))
            cols.append(-p_sink * delta)
        for c, (b, p) in enumerate(SWA_CHAINS):
            dq_ref[b, :, _slab(p)] = _unstack_heads(_dot(ds[c], keys[c]), lane) * Q_SCALE
        dk2 = [_dot(ds[c], q2[c], TN) for c in chains]
        dv2 = [_dot(pn16[c], do2[c], TN) for c in chains]
        for c, (b, p) in enumerate(SWA_CHAINS):
            gsl = _slab(p // 2)
            dk_ref[b, rows_prev, gsl] += dk2[c][:BLK]
            dk_ref[b, rows_cur, gsl] += dk2[c][BLK:]
            dv_ref[b, rows_prev, gsl] += dv2[c][:BLK]
            dv_ref[b, rows_cur, gsl] += dv2[c][BLK:]
            for e in range(2):
                dsink_ref[2 * p + e:2 * p + e + 1, :] += jnp.sum(cols[c][e * BLK:(e + 1) * BLK], axis=0, keepdims=True)

    cur, prev = _swa_specs()
    whole = pl.BlockSpec((B_LOC, SEQ, 256), lambda i: (0, 0, 0))
    q3, k3, v3, o3, do3 = (a.reshape(B_LOC, SEQ, a.shape[1]) for a in (q_rot, k_dup, v_dup, o_a, d_oa))
    dq, dk, dv, dsink = pl.pallas_call(
        body, name="swa_bwd", grid=(N_BLK,),
        in_specs=[cur(512), prev(256), cur(256), prev(256), cur(256), cur(512), cur(512),
                  pl.BlockSpec(memory_space=pltpu.SMEM)],
        out_specs=[cur(512), whole, whole, pl.BlockSpec((8, 128), lambda i: (0, 0))],
        out_shape=[jax.ShapeDtypeStruct((B_LOC, SEQ, 512), F32), jax.ShapeDtypeStruct((B_LOC, SEQ, 256), F32),
                   jax.ShapeDtypeStruct((B_LOC, SEQ, 256), F32), jax.ShapeDtypeStruct((8, 128), F32)],
        compiler_params=_params(("arbitrary",)),
    )(q3, k3, k3, v3, v3, o3, do3, sinks)
    return dq.reshape(T_LOC, 512), dk.reshape(T_LOC, 256), dv.reshape(T_LOC, 256), dsink


def sb_bwd(qb, kb, vb, d_ob, carries, dwout):
    def body(q_ref, k_ref, v_ref, do_ref, c_ref, dw_hbm, dq_ref, dk_ref, dv_ref, rw_hbm, kt_ref, dqt_ref,
             rw_send, rw_recv, rw_local):
        start_dwout, finish_dwout = _direct_exchange(
            lambda dev: dw_hbm.at[pl.ds(_lin(dev) * OUT_SHARD, OUT_SHARD), :], lambda dev: rw_hbm.at[_lin(dev)],
            rw_send, rw_recv, rw_local)
        start_dwout()
        for c, (b, pp) in enumerate(SB_CHAINS):
            for j in range(N_BLK):
                kt_ref[c, j] = k_ref[b * SEQ + j * BLK:b * SEQ + (j + 1) * BLK, _slab(pp)].T
        dk_ref[...] = jnp.zeros_like(dk_ref)
        dv_ref[...] = jnp.zeros_like(dv_ref)
        dqt_ref[...] = jnp.zeros_like(dqt_ref)
        lane = _lane((BLK, SLAB))
        tri_after, tri_before = _tri(True), _tri(False)
        valid = _causal_t()
        jrow = _row((N_BLK, 2 * BLK))
        chains = range(len(SB_CHAINS))

        def q_block(i, _):
            q2 = [_stack_heads(q_ref[_sb_rows(b, i), _slab(pp)], lane) for b, pp in SB_CHAINS]
            do2 = [_stack_heads(do_ref[_sb_rows(b, i), _slab(pp)], lane) for b, pp in SB_CHAINS]

            def key_block(j, carry_sp, before_u, mask):
                at = [(_sb_rows(b, j), _slab(pp)) for b, pp in SB_CHAINS]
                zt = [_dot(k_ref[at[c]], q2[c], NT) for c in chains]
                sp = [_sb_softplus(zt[c], mask) for c in chains]
                later = [_key_sums(tri_after, sp[c])[0] for c in chains]
                dw = [_dot(v_ref[at[c]], do2[c], NT) for c in chains]
                w = [_sb_weights(zt[c], sp[c], later[c] + carry_sp[c], mask) for c in chains]
                u = [dw[c] * w[c] for c in chains]
                for c in chains:
                    dv_ref[at[c]] += _dot(w[c].astype(BF16), do2[c])
                sums = [_key_sums(tri_before, u[c]) for c in chains]
                dz16 = []
                for c in chains:
                    sig = jnp.exp(zt[c] - sp[c])
                    dz = u[c] - sig * (u[c] + before_u[c] + sums[c][0])
                    if mask is not None:
                        dz = jnp.where(mask, dz, 0.0)
                    dz16.append(dz.astype(BF16))
                for c in chains:
                    dk_ref[at[c]] += _dot(dz16[c], q2[c])
                    dqt_ref[c] += _dot(kt_ref[c, j], dz16[c])
                return tuple(before_u[c] + sums[c][1] for c in chains)

            def earlier(j, before_u):
                carry_sp = [jnp.sum(jnp.where(jrow == j, c_ref[c * N_BLK + i], 0.0), axis=0, keepdims=True)
                            for c in chains]
                return key_block(j, carry_sp, before_u, None)

            zero = tuple(jnp.zeros((1, 2 * BLK), F32) for _ in chains)
            before_u = lax.fori_loop(0, i, earlier, zero)
            key_block(i, zero, before_u, valid)
            for c, (b, pp) in enumerate(SB_CHAINS):
                dq_ref[_sb_rows(b, i), _slab(pp)] = (_unstack_heads(dqt_ref[c].T, lane) * Q_SCALE).astype(BF16)
                dqt_ref[c] = jnp.zeros((SLAB, 2 * BLK), F32)
            return 0

        lax.fori_loop(0, N_BLK, q_block, 0)
        finish_dwout()

    n_ch = len(SB_CHAINS)
    vmem, hbm = pl.BlockSpec(memory_space=pltpu.VMEM), pl.BlockSpec(memory_space=pl.ANY)
    return pl.pallas_call(
        body, name="sb_bwd",
        in_specs=[vmem] * 5 + [hbm], out_specs=[vmem] * 3 + [hbm],
        out_shape=[jax.ShapeDtypeStruct((T_LOC, 512), BF16)] + [jax.ShapeDtypeStruct((T_LOC, 512), F32)] * 2
        + [jax.ShapeDtypeStruct((N_DEV, OUT_SHARD, D_MODEL), BF16)],
        scratch_shapes=[pltpu.VMEM((n_ch, N_BLK, SLAB, BLK), BF16), pltpu.VMEM((n_ch, SLAB, 2 * BLK), F32),
                        pltpu.SemaphoreType.DMA((7,)), pltpu.SemaphoreType.DMA((7,)), pltpu.SemaphoreType.DMA],
        compiler_params=pltpu.CompilerParams(vmem_limit_bytes=VMEM_LIMIT),
    )(qb, kb, vb, d_ob, carries, dwout)


def bwd_dw(x, norm_gain, dq_rot, dk_dup, dv_dup, qa_raw, ka_raw, cos, sin_s, q_gain2, k_gain2, dga, dgb, dqb, dkb, dvb):
    n_tiles = T_LOC // TM_DW

    def body(x_ref, ng_ref, dq_ref, dk_ref, dv_ref, qa_ref, ka_ref, cos_ref, sin_ref, qg_ref, kg_ref,
             dga_ref, dgb_ref, dqb_ref, dkb_ref, dvb_ref,
             dproj_ref, dw_hbm, dqg_ref, dkg_ref, acc_ref, stage_ref):
        step = pl.program_id(0)

        @pl.when(step == 0)
        def _():
            acc_ref[...] = jnp.zeros_like(acc_ref)
            dqg_ref[...] = jnp.zeros_like(dqg_ref)
            dkg_ref[...] = jnp.zeros_like(dkg_ref)

        lane = _lane((TM_DW, SLAB))
        bd = _head_blockdiag()
        cos, sin_s = cos_ref[...], sin_ref[...]

        def norm_rope_bwd(d_rot, raw, gain2):
            dy = d_rot * cos + _swap_half(d_rot * sin_s, lane)
            r = lax.rsqrt(_head_sum(raw * raw, bd) * (1.0 / HEAD_DIM) + EPS)
            xhat = raw * r
            dgain = jnp.sum(dy * xhat, axis=0, keepdims=True)
            dxh = dy * gain2
            mean = _head_sum(dxh * xhat, bd) * (1.0 / HEAD_DIM)
            return r * (dxh - xhat * mean), dgain

        def fold_dup(d_dup):
            a, b2 = d_dup[:, :SLAB], d_dup[:, SLAB:]
            return jnp.where(lane < HEAD_DIM, a + pltpu.roll(a, HEAD_DIM, 1), b2 + pltpu.roll(b2, HEAD_DIM, 1))

        pieces = []
        dqg = jnp.zeros((1, SLAB), F32)
        for p in range(4):
            sl = slice(p * SLAB, (p + 1) * SLAB)
            d_raw, dg = norm_rope_bwd(dq_ref[:, sl], qa_ref[:, sl], qg_ref[...])
            pieces.append(d_raw.astype(BF16))
            dqg = dqg + dg
        d_raw, dkg = norm_rope_bwd(fold_dup(dk_ref[...]), ka_ref[...], kg_ref[...])
        pieces.append(d_raw.astype(BF16))
        pieces.append(fold_dup(dv_ref[...]).astype(BF16))
        pieces += [dga_ref[...], dqb_ref[...], dkb_ref[...].astype(BF16), dvb_ref[...].astype(BF16),
                   dgb_ref[...]]
        dproj = jnp.concatenate(pieces, axis=1)
        dproj_ref[...] = dproj
        dqg_ref[0:1, :] += dqg + pltpu.roll(dqg, HEAD_DIM, 1)
        dkg_ref[0:1, :] += dkg + pltpu.roll(dkg, HEAD_DIM, 1)

        xv = x_ref[...]
        rstd = lax.rsqrt(jnp.mean(xv * xv, axis=-1, keepdims=True) + EPS)
        h = (xv * rstd * ng_ref[...]).astype(BF16)
        for r0 in range(0, IN_WIDTH, ACC_ROWS):
            acc_ref[r0:r0 + ACC_ROWS, :] += _dot(dproj[:, r0:r0 + ACC_ROWS], h, TN)

        @pl.when(step == n_tiles - 1)
        def _():
            for r0 in range(0, IN_WIDTH, ACC_ROWS):
                stage_ref[...] = acc_ref[r0:r0 + ACC_ROWS, :].astype(BF16)
                pltpu.sync_copy(stage_ref, dw_hbm.at[r0:r0 + ACC_ROWS, :])

    def tile(w):
        return pl.BlockSpec((TM_DW, w), lambda i: (i, 0))

    def whole(a):
        return pl.BlockSpec(a.shape, lambda i: (0, 0))

    const = lambda i: (0, 0)
    return pl.pallas_call(
        body, name="bwd_dw", grid=(n_tiles,),
        in_specs=[tile(D_MODEL), whole(norm_gain),
                  tile(512), tile(256), tile(256), tile(512), tile(128), tile(128), tile(128),
                  whole(q_gain2), whole(k_gain2), tile(512), tile(512), tile(512), tile(512), tile(512)],
        out_specs=[tile(IN_WIDTH), pl.BlockSpec(memory_space=pl.ANY),
                   pl.BlockSpec((8, SLAB), const), pl.BlockSpec((8, SLAB), const)],
        out_shape=[jax.ShapeDtypeStruct((T_LOC, IN_WIDTH), BF16), jax.ShapeDtypeStruct((IN_WIDTH, D_MODEL), BF16),
                   jax.ShapeDtypeStruct((8, SLAB), F32), jax.ShapeDtypeStruct((8, SLAB), F32)],
        scratch_shapes=[pltpu.VMEM((IN_WIDTH, D_MODEL), F32), pltpu.VMEM((ACC_ROWS, D_MODEL), BF16)],
        compiler_params=_params(("arbitrary",)),
    )(x, norm_gain, dq_rot, dk_dup, dv_dup, qa_raw, ka_raw, cos, sin_s, q_gain2, k_gain2, dga, dgb, dqb, dkb, dvb)


def bwd_dx(x, dout, norm_gain, win_t, dproj, dwin_t, dqg, dkg, dsink, loss_part):
    n_tiles = T_LOC // TM
    rows_per = IN_SHARD
    step_sums, step_merge = 3, 7

    def body(x_ref, dout_ref, ng_ref, w_hbm, dp_ref, a_hbm, dqg_ref, dkg_ref, dsink_ref, loss_ref,
             gx_ref, ra_hbm, rs_hbm, w_ref, dng_ref, s_ref, own_ref, sib_ref, snd_ref, extra_ref,
             w_sem, d2d_send, d2d_recv, ici_send, ici_recv, own_sems, s_send, s_recv, out_sem):
        step = pl.program_id(0)
        x, y, c = _mesh_pos()
        me, sibling = (x, y, c), (x, y, 1 - c)
        chips = {"own": (x, y), "x": (1 - x, y), "y": (x, 1 - y), "d": (1 - x, 1 - y)}
        index = {"own": 0, "x": 1, "y": 2, "d": 3}
        order = ("d", "x", "y", "own")

        def rows(pos):
            return a_hbm.at[pl.ds(_lin(pos) * rows_per, rows_per), :]

        def to_sibling(k):
            return pltpu.make_async_remote_copy(
                src_ref=rows((*chips[k], 1 - c)), dst_ref=sib_ref.at[index[k]],
                send_sem=d2d_send.at[index[k]], recv_sem=d2d_recv.at[index[k]], device_id=sibling, device_id_type=MESH)

        def mine(k):
            return pltpu.make_async_copy(rows((*chips[k], c)), own_ref.at[index[k]], own_sems.at[index[k]])

        def ici(n, to_chip, dst):
            return pltpu.make_async_remote_copy(
                src_ref=snd_ref.at[n], dst_ref=dst, send_sem=ici_send.at[n], recv_sem=ici_recv.at[n],
                device_id=(*chips[to_chip], c), device_id_type=MESH)

        def chip_sum(k):
            to_sibling(k).wait_recv()
            mine(k).wait()
            return own_ref[index[k]].astype(F32) + sib_ref[index[k]].astype(F32)

        def by_core(fn):
            pl.when(c == 0)(lambda: fn("x", "y"))
            pl.when(c == 1)(lambda: fn("y", "x"))

        @pl.when(step == 0)
        def _():
            cp = pltpu.make_async_copy(w_hbm, w_ref, w_sem)
            cp.start()
            for k in order:
                to_sibling(k).start()
                mine(k).start()
            dng_ref[...] = jnp.zeros_like(dng_ref)
            cp.wait()

        @pl.when(step == step_sums)
        def _():
            def first_sends(direct, via):
                snd_ref[0] = chip_sum("d").astype(BF16)
                ici(0, direct, extra_ref).start()
                snd_ref[1] = chip_sum(direct).astype(BF16)
                ici(1, direct, ra_hbm.at[index[direct]]).start()
            by_core(first_sends)

        @pl.when(step == step_merge)
        def _():
            def merge(direct, via):
                merged = chip_sum(via)
                ici(0, direct, extra_ref).wait_recv()
                snd_ref[2] = (merged + extra_ref[...].astype(F32)).astype(BF16)
                ici(2, via, ra_hbm.at[index[via]]).start()
                own_ref[0] = chip_sum("own").astype(BF16)
                pltpu.make_async_copy(own_ref.at[0], ra_hbm.at[0], out_sem).start()
            by_core(merge)

        xv = x_ref[...]
        rstd = lax.rsqrt(jnp.mean(xv * xv, axis=-1, keepdims=True) + EPS)
        xhat = xv * rstd
        gain = ng_ref[...]
        dh = _dot(dp_ref[...], w_ref[...])
        dng_ref[0:1, :] += jnp.sum(dh * xhat, axis=0, keepdims=True)
        dxh = dh * gain
        gx_ref[...] = dout_ref[...] + rstd * (dxh - xhat * jnp.mean(dxh * xhat, axis=-1, keepdims=True))

        @pl.when(step == n_tiles - 1)
        def _():
            s_ref[...] = jnp.concatenate(
                [dng_ref[...], dqg_ref[...], dkg_ref[...], dsink_ref[...], loss_ref[...]], axis=1)
            start_small, finish_small = _direct_exchange(
                lambda dev: s_ref, lambda dev: rs_hbm.at[_lin(dev)], s_send, s_recv, out_sem)

            def finish(direct, via):
                ici(1, direct, ra_hbm.at[index[direct]]).wait_recv()
                ici(2, via, ra_hbm.at[index[via]]).wait_recv()
                for n, to in ((0, direct), (1, direct), (2, via)):
                    ici(n, to, extra_ref).wait_send()
            by_core(finish)
            pltpu.make_async_copy(own_ref.at[0], ra_hbm.at[0], out_sem).wait()
            for k in order:
                to_sibling(k).wait_send()
            start_small()
            finish_small()

    def tile(w):
        return pl.BlockSpec((TM, w), lambda i: (i, 0))

    def whole(a):
        return pl.BlockSpec(a.shape, lambda i: (0, 0))

    hbm = pl.BlockSpec(memory_space=pl.ANY)
    block = (rows_per, D_MODEL)
    return pl.pallas_call(
        body, name="bwd_dx", grid=(n_tiles,),
        in_specs=[tile(D_MODEL), tile(D_MODEL), whole(norm_gain), hbm, tile(IN_WIDTH), hbm,
                  whole(dqg), whole(dkg), whole(dsink), whole(loss_part)],
        out_specs=[tile(D_MODEL), hbm, hbm],
        out_shape=[jax.ShapeDtypeStruct((T_LOC, D_MODEL), F32), jax.ShapeDtypeStruct((3,) + block, BF16),
                   jax.ShapeDtypeStruct((N_DEV, 8, SMALL_W), F32)],
        scratch_shapes=[pltpu.VMEM((IN_WIDTH, D_MODEL), BF16), pltpu.VMEM((8, D_MODEL), F32),
                        pltpu.VMEM((8, SMALL_W), F32),
                        pltpu.VMEM((4,) + block, BF16), pltpu.VMEM((4,) + block, BF16), pltpu.VMEM((3,) + block, BF16),
                        pltpu.VMEM(block, BF16),
                        pltpu.SemaphoreType.DMA, pltpu.SemaphoreType.DMA((4,)), pltpu.SemaphoreType.DMA((4,)),
                        pltpu.SemaphoreType.DMA((3,)), pltpu.SemaphoreType.DMA((3,)), pltpu.SemaphoreType.DMA((4,)),
                        pltpu.SemaphoreType.DMA((7,)), pltpu.SemaphoreType.DMA((7,)), pltpu.SemaphoreType.DMA],
        compiler_params=_params(("arbitrary",)),
    )(x, dout, norm_gain, win_t, dproj, dwin_t, dqg, dkg, dsink, loss_part)


def _adamw(w, g, m, v):
    m = ADAM_B1 * m + (1.0 - ADAM_B1) * g
    v = ADAM_B2 * v + (1.0 - ADAM_B2) * (g * g)
    m_hat = m / (1.0 - ADAM_B1 ** ADAM_STEP)
    v_hat = v / (1.0 - ADAM_B2 ** ADAM_STEP)
    delta = -ADAM_LR * (m_hat / (jnp.sqrt(v_hat) + ADAM_EPS) + ADAM_WD * w)
    return delta, m, v


def _sum_slots(r_ref):
    g = r_ref[0].astype(F32)
    for s in range(1, r_ref.shape[0]):
        g = g + r_ref[s].astype(F32)
    return g


def adamw_rows(name, recv, w, m, v):
    def body(r_ref, w_ref, m_ref, v_ref, g_ref, d_ref, nm_ref, nv_ref):
        g = _sum_slots(r_ref)
        g_ref[...] = g
        d_ref[...], nm_ref[...], nv_ref[...] = _adamw(w_ref[...], g, m_ref[...], v_ref[...])

    return pl.pallas_call(
        body, name=name,
        out_shape=[jax.ShapeDtypeStruct(w.shape, F32)] * 4,
        compiler_params=pltpu.CompilerParams(vmem_limit_bytes=VMEM_LIMIT),
    )(recv, w, m, v)


def adamw_small(recv, weights, moments_m, moments_v):
    n = len(weights)

    def body(r_ref, *refs):
        ins, outs = refs[:3 * n], refs[3 * n:]
        s = _sum_slots(r_ref)
        eye = (_row((8, SLAB)) == _lane((8, SLAB))).astype(F32)
        sinks = jnp.sum(s[:, 1280:1408] * eye, axis=0, keepdims=True)
        grads = [s[0:1, :D_MODEL], s[0:1, 1024:1024 + HEAD_DIM], s[0:1, 1152:1152 + HEAD_DIM], sinks[:, :8]]
        for k in range(n):
            outs[k][...] = grads[k]
            outs[n + k][...], outs[2 * n + k][...], outs[3 * n + k][...] = _adamw(
                ins[k][...], grads[k], ins[n + k][...], ins[2 * n + k][...])
        loss = jnp.sum(jnp.sum(s[:, 1408:1536], axis=1, keepdims=True), axis=0, keepdims=True) * (0.5 / D_MODEL)
        outs[4 * n][...] = loss

    res = pl.pallas_call(
        body, name="adamw_small",
        out_shape=[jax.ShapeDtypeStruct(w.shape, F32) for w in weights] * 4 + [jax.ShapeDtypeStruct((1, 1), F32)],
        compiler_params=pltpu.CompilerParams(vmem_limit_bytes=VMEM_LIMIT),
    )(recv, *weights, *moments_m, *moments_v)
    return res[:n], res[n:2 * n], res[2 * n:3 * n], res[3 * n:4 * n], res[4 * n]


def kernel(x, positions, norm_gain, w_in, q_norm_gain, k_norm_gain, sinks, w_out, loss_target, m_norm_gain, m_w_in, m_q_norm_gain, m_k_norm_gain, m_sinks, m_w_out, v_norm_gain, v_w_in, v_q_norm_gain, v_k_norm_gain, v_sinks, v_w_out):
    x2 = x.reshape(T_LOC, D_MODEL)
    tgt2 = loss_target.reshape(T_LOC, D_MODEL)
    pos2 = positions.reshape(1, T_LOC)
    half = HEAD_DIM // 2
    inv_freq = ROPE_THETA ** (-jnp.arange(half, dtype=F32) * 2.0 / HEAD_DIM)
    inv_freq = jnp.tile(inv_freq, SLAB // half).reshape(SLAB, 1)
    sin_sign = jnp.tile(jnp.concatenate([-jnp.ones((half,), F32), jnp.ones((half,), F32)]), 2).reshape(1, SLAB)
    q_gain2 = jnp.tile(q_norm_gain, (1, 2))
    k_gain2 = jnp.tile(k_norm_gain, (1, 2))

    win_t = gather_weights(w_in.reshape(D_MODEL, IN_SHARD).T)

    (qa_raw, ka_raw, q_rot, k_dup, v_dup, ga, qb, kb, vb, gb, cos, sin_s, wout) = fwd_proj(
        x2, pos2, norm_gain, win_t, inv_freq, sin_sign, q_gain2, k_gain2, w_out.reshape(OUT_SHARD, D_MODEL).astype(BF16))
    o_a = swa_fwd(q_rot, k_dup, v_dup, sinks)
    o_b, carries = sb_fwd(qb, kb, vb)
    dout, d_oa, d_ob, dga, dgb, dwout, loss_part = out_loss(o_a, o_b, ga, gb, x2, tgt2, wout)
    dq_rot, dk_dup, dv_dup, dsink = swa_bwd(q_rot, k_dup, v_dup, o_a, d_oa, sinks)
    dqb, dkb, dvb, r_out = sb_bwd(qb, kb, vb, d_ob, carries, dwout)
    dproj, dwin_t, dqg, dkg = bwd_dw(
        x2, norm_gain, dq_rot, dk_dup, dv_dup, qa_raw, ka_raw, cos, sin_s, q_gain2, k_gain2, dga, dgb, dqb, dkb, dvb)
    grad_x, r_win, r_small = bwd_dx(x2, dout, norm_gain, win_t, dproj, dwin_t, dqg, dkg, dsink, loss_part)

    w_in2, m_in2, v_in2 = (a.reshape(D_MODEL, IN_SHARD).T for a in (w_in, m_w_in, v_w_in))
    w_out2, m_out2, v_out2 = (a.reshape(OUT_SHARD, D_MODEL) for a in (w_out, m_w_out, v_w_out))
    big_in = adamw_rows("adamw_w_in", r_win, w_in2, m_in2, v_in2)
    big_out = adamw_rows("adamw_w_out", r_out, w_out2, m_out2, v_out2)
    *small_out, loss = adamw_small(
        r_small, (norm_gain, q_norm_gain, k_norm_gain, sinks),
        (m_norm_gain, m_q_norm_gain, m_k_norm_gain, m_sinks), (v_norm_gain, v_q_norm_gain, v_k_norm_gain, v_sinks))

    def leaves(k):
        ng, qg, kg, sk = small_out[k]
        return (ng, big_in[k].T.reshape(1, D_MODEL, IN_SHARD), qg, kg, sk, big_out[k].reshape(1, OUT_SHARD, D_MODEL))

    return (loss.reshape(()), grad_x.reshape(B_LOC, SEQ, D_MODEL), *leaves(0), *leaves(1), *leaves(2), *leaves(3))
```

```python
import functools

import jax
import jax.numpy as jnp
from jax import lax
from jax.experimental import pallas as pl
from jax.experimental.pallas import tpu as pltpu

F32 = jnp.float32
BF16 = jnp.bfloat16

N_DEV = 8
D_MODEL = 1024
SEQ = 2048
B_LOC = 2
T_LOC = B_LOC * SEQ
HEAD_DIM = 64
HEAD_SHIFT = 6
BLK = 128
N_BLK = SEQ // BLK
SLAB = 128
IN_WIDTH = 3328
IN_SHARD = IN_WIDTH // N_DEV
OUT_SHARD = D_MODEL // N_DEV
EPS = 1e-6
ROPE_THETA = 10000.0
Q_SCALE = 0.125
R_QA, R_KA, R_VA, R_GA, R_QB, R_KB, R_VB, R_GB, R_END = 0, 512, 640, 768, 1280, 1792, 2304, 2816, 3328
SMALL_W = 1536
ADAM_LR, ADAM_B1, ADAM_B2, ADAM_EPS, ADAM_WD, ADAM_STEP = 0.001, 0.9, 0.999, 1e-08, 0.01, 10
TM = 256
TM_DW = 512
TM_FWD = 512
ACC_ROWS = 256
VMEM_LIMIT = 56 * 1024 * 1024

MESH = pl.DeviceIdType.MESH
NT = (((1,), (1,)), ((), ()))
TN = (((0,), (0,)), ((), ()))


def _params(sem, limit=VMEM_LIMIT):
    return pltpu.CompilerParams(dimension_semantics=sem, vmem_limit_bytes=limit)


def _dot(a, b, dims=None):
    if dims is None:
        return jnp.dot(a, b, preferred_element_type=F32)
    return lax.dot_general(a, b, dims, preferred_element_type=F32)


def _lane(shape):
    return lax.broadcasted_iota(jnp.int32, shape, len(shape) - 1)


def _row(shape):
    return lax.broadcasted_iota(jnp.int32, shape, 0)


def _head_blockdiag():
    return ((_row((SLAB, SLAB)) >> HEAD_SHIFT) == (_lane((SLAB, SLAB)) >> HEAD_SHIFT)).astype(BF16)


def _head_sum(x, bd):
    return _dot(x.astype(BF16), bd)


def _swap_half(y, lane):
    return jnp.where((lane & 32) != 0, pltpu.roll(y, 32, 1), pltpu.roll(y, 96, 1))


def _stack_heads(q, lane):
    zero = jnp.zeros_like(q)
    return jnp.concatenate([jnp.where(lane < HEAD_DIM, q, zero), jnp.where(lane >= HEAD_DIM, q, zero)], axis=0)


def _unstack_heads(x2, lane):
    return jnp.where(lane < HEAD_DIM, x2[:BLK], x2[BLK:])


def _sigmoid(x):
    return 1.0 / (1.0 + jnp.exp(-x))


def _mesh_pos():
    return lax.axis_index("x"), lax.axis_index("y"), lax.axis_index("c")


def _flip(pos, mask):
    return tuple(1 - p if m else p for p, m in zip(pos, mask))


def _lin(pos):
    return 4 * pos[0] + 2 * pos[1] + pos[2]


DEV_FLIPS = [(fx, fy, fc) for fx in (0, 1) for fy in (0, 1) for fc in (0, 1)][1:]


def _direct_exchange(src_for, dst_slot, send_sems, recv_sems, local_sem):
    me = _mesh_pos()

    def copy(k, to):
        return pltpu.make_async_remote_copy(
            src_ref=src_for(to), dst_ref=dst_slot(me), send_sem=send_sems.at[k], recv_sem=recv_sems.at[k],
            device_id=to, device_id_type=MESH)

    def landed(k, frm):
        return pltpu.make_async_remote_copy(
            src_ref=src_for(frm), dst_ref=dst_slot(frm), send_sem=send_sems.at[k], recv_sem=recv_sems.at[k],
            device_id=frm, device_id_type=MESH)

    local = None if local_sem is None else pltpu.make_async_copy(src_for(me), dst_slot(me), local_sem)
    peers = [_flip(me, f) for f in DEV_FLIPS]

    def start():
        if local is not None:
            local.start()
        for k, to in enumerate(peers):
            copy(k, to).start()

    def finish():
        for k, frm in enumerate(peers):
            landed(k, frm).wait_recv()
        for k, to in enumerate(peers):
            copy(k, to).wait_send()
        if local is not None:
            local.wait()

    return start, finish


def gather_weights(shard):
    m = shard.shape[0]

    def body(f32_ref, o_ref, a_ref, ici_send, ici_recv, d2d_send, d2d_recv, local_sem):
        a_ref[...] = f32_ref[...].astype(BF16)
        x, y, c = _mesh_pos()
        me, sibling = (x, y, c), (x, y, 1 - c)
        chip_x, chip_y, chip_d = (1 - x, y), (x, 1 - y), (1 - x, 1 - y)

        def rows(pos):
            return o_ref.at[pl.ds(_lin(pos) * m, m), :]

        def ici(k, block, to, src=None):
            return pltpu.make_async_remote_copy(
                src_ref=rows(block) if src is None else src, dst_ref=rows(block),
                send_sem=ici_send.at[k], recv_sem=ici_recv.at[k], device_id=to, device_id_type=MESH)

        def d2d(k, chip, mine, src=None):
            block = (*chip, c) if mine else (*chip, 1 - c)
            return pltpu.make_async_remote_copy(
                src_ref=rows(block) if src is None else src, dst_ref=rows(block),
                send_sem=d2d_send.at[k], recv_sem=d2d_recv.at[k], device_id=sibling, device_id_type=MESH)

        local = pltpu.make_async_copy(a_ref, rows(me), local_sem)
        local.start()
        sends = [ici(0, me, (*chip_x, c), src=a_ref), ici(1, me, (*chip_y, c), src=a_ref),
                 d2d(0, (x, y), True, src=a_ref)]
        for cp in sends:
            cp.start()

        def pass_on(first, k_first, second, k_second, onward):
            ici(k_first, (*first, c), me).wait_recv()
            relay = ici(2, (*first, c), (*onward, c))
            relay.start()
            hand = [d2d(1 + k_first, first, True)]
            hand[0].start()
            ici(k_second, (*second, c), me).wait_recv()
            hand.append(d2d(1 + k_second, second, True))
            hand[1].start()
            ici(2, (*chip_d, c), me).wait_recv()
            hand.append(d2d(3, chip_d, True))
            hand[2].start()
            for cp in [relay] + hand:
                cp.wait_send()

        @pl.when(c == 0)
        def _():
            pass_on(chip_y, 1, chip_x, 0, chip_x)

        @pl.when(c == 1)
        def _():
            pass_on(chip_x, 0, chip_y, 1, chip_y)

        for k, chip in enumerate([(x, y), chip_x, chip_y, chip_d]):
            d2d(k, chip, False).wait_recv()
        for cp in sends:
            cp.wait_send()
        local.wait()

    vmem = pl.BlockSpec(memory_space=pltpu.VMEM)
    return pl.pallas_call(
        body, name="gather_weights",
        out_shape=jax.ShapeDtypeStruct((N_DEV * m, shard.shape[1]), BF16),
        in_specs=[vmem], out_specs=vmem,
        scratch_shapes=[pltpu.VMEM(shard.shape, BF16), pltpu.SemaphoreType.DMA((3,)), pltpu.SemaphoreType.DMA((3,)),
                        pltpu.SemaphoreType.DMA((4,)), pltpu.SemaphoreType.DMA((4,)), pltpu.SemaphoreType.DMA],
        compiler_params=pltpu.CompilerParams(vmem_limit_bytes=VMEM_LIMIT),
    )(shard)


def _norm_rope(xs, gain2, cos, sin_s, bd, lane):
    r = lax.rsqrt(_head_sum(xs * xs, bd) * (1.0 / HEAD_DIM) + EPS)
    y = xs * r * gain2
    return y * cos + _swap_half(y, lane) * sin_s


def _dup_heads(xs, lane):
    r = pltpu.roll(xs, HEAD_DIM, 1)
    lo = lane < HEAD_DIM
    return jnp.concatenate([jnp.where(lo, xs, r), jnp.where(lo, r, xs)], axis=1)


def fwd_proj(x, pos, norm_gain, win_t, inv_freq, sin_sign, q_gain2, k_gain2, wout_shard):
    n_tiles = T_LOC // TM_FWD

    def body(x_ref, pos_ref, ng_ref, w_ref, if_ref, sg_ref, qg_ref, kg_ref, ws_hbm,
             qa_raw_ref, ka_raw_ref, q_rot_ref, k_dup_ref, v_dup_ref, ga_ref, qb_ref, kb_ref, vb_ref, gb_ref,
             cos_ref, sin_ref, wo_hbm, wo_send, wo_recv, wo_local):
        start_wout, finish_wout = _direct_exchange(
            lambda dev: ws_hbm, lambda dev: wo_hbm.at[pl.ds(_lin(dev) * OUT_SHARD, OUT_SHARD), :],
            wo_send, wo_recv, wo_local)
        pl.when(pl.program_id(0) == 0)(start_wout)

        xv = x_ref[...]
        rstd = lax.rsqrt(jnp.mean(xv * xv, axis=-1, keepdims=True) + EPS)
        h = (xv * rstd * ng_ref[...]).astype(BF16)

        def proj(r0, r1):
            return _dot(h, w_ref[r0:r1, :], NT)

        ang_t = if_ref[...] * pos_ref[...].astype(F32)
        cos = jnp.cos(ang_t).T
        sin_s = jnp.sin(ang_t).T * sg_ref[...]
        cos_ref[...] = cos
        sin_ref[...] = sin_s
        lane = _lane((TM_FWD, SLAB))
        bd = _head_blockdiag()

        qa = proj(R_QA, R_KA)
        qa_raw_ref[...] = qa
        for p in range(4):
            sl = slice(p * SLAB, (p + 1) * SLAB)
            q_rot_ref[:, sl] = (_norm_rope(qa[:, sl], qg_ref[...], cos, sin_s, bd, lane) * Q_SCALE).astype(BF16)
        ka = proj(R_KA, R_VA)
        ka_raw_ref[...] = ka
        k_dup_ref[...] = _dup_heads(_norm_rope(ka, kg_ref[...], cos, sin_s, bd, lane), lane).astype(BF16)
        v_dup_ref[...] = _dup_heads(proj(R_VA, R_GA), lane).astype(BF16)
        ga_ref[...] = proj(R_GA, R_QB).astype(BF16)
        qb_ref[...] = (proj(R_QB, R_KB) * Q_SCALE).astype(BF16)
        kb_ref[...] = proj(R_KB, R_VB).astype(BF16)
        vb_ref[...] = proj(R_VB, R_GB).astype(BF16)
        gb_ref[...] = proj(R_GB, R_END).astype(BF16)
        pl.when(pl.program_id(0) == n_tiles - 1)(finish_wout)

    def tile(w):
        return pl.BlockSpec((TM_FWD, w), lambda i: (i, 0))

    def whole(a):
        return pl.BlockSpec(a.shape, lambda i: (0, 0))

    hbm = pl.BlockSpec(memory_space=pl.ANY)
    widths = [(512, F32), (128, F32), (512, BF16), (256, BF16), (256, BF16), (512, BF16), (512, BF16), (512, BF16),
              (512, BF16), (512, BF16), (128, F32), (128, F32)]
    return pl.pallas_call(
        body, name="fwd_proj", grid=(n_tiles,),
        in_specs=[tile(D_MODEL), pl.BlockSpec((1, TM_FWD), lambda i: (0, i)), whole(norm_gain), whole(win_t),
                  whole(inv_freq), whole(sin_sign),
                  whole(q_gain2), whole(k_gain2), hbm],
        out_specs=[tile(w) for w, _ in widths] + [hbm],
        out_shape=[jax.ShapeDtypeStruct((T_LOC, w), dt) for w, dt in widths]
        + [jax.ShapeDtypeStruct((D_MODEL, D_MODEL), BF16)],
        scratch_shapes=[pltpu.SemaphoreType.DMA((7,)), pltpu.SemaphoreType.DMA((7,)), pltpu.SemaphoreType.DMA],
        compiler_params=_params(("arbitrary",)),
    )(x, pos, norm_gain, win_t, inv_freq, sin_sign, q_gain2, k_gain2, wout_shard)


def _swa_window(prev_ref, cur_ref, p):
    gsl = _slab(p // 2)
    return jnp.concatenate([prev_ref[:, gsl], cur_ref[:, gsl]], axis=0)


def _swa_probs(s, sinks_ref, p, i):
    shape = (2 * BLK, 2 * BLK)
    r = _row(shape) & (BLK - 1)
    cidx = _lane(shape)
    valid = (cidx > r) & (cidx <= r + BLK) & ((cidx >= BLK) | (i > 0))
    s = jnp.where(valid, s, -jnp.inf)
    sink = jnp.where(_row((2 * BLK, 1)) < BLK, sinks_ref[0, 2 * p], sinks_ref[0, 2 * p + 1])
    m = jnp.maximum(jnp.max(s, axis=-1, keepdims=True), sink)
    e = jnp.exp(s - m)
    e_sink = jnp.exp(sink - m)
    den = jnp.sum(e, axis=-1, keepdims=True) + e_sink
    return e / den, e_sink / den


SWA_CHAINS = [(b, p) for b in range(B_LOC) for p in range(4)]


def _swa_specs():
    def cur(w):
        return pl.BlockSpec((B_LOC, BLK, w), lambda i: (0, i, 0))

    def prev(w):
        return pl.BlockSpec((B_LOC, BLK, w), lambda i: (0, jnp.maximum(i - 1, 0), 0))

    return cur, prev


def swa_fwd(q_rot, k_dup, v_dup, sinks):
    def body(q_ref, kp_ref, kc_ref, vp_ref, vc_ref, sinks_ref, o_ref):
        i = pl.program_id(0)
        lane = _lane((BLK, SLAB))
        s = [_dot(_stack_heads(q_ref[b, :, _slab(p)], lane), _swa_window(kp_ref.at[b], kc_ref.at[b], p), NT)
             for b, p in SWA_CHAINS]
        pn = [_swa_probs(s[c], sinks_ref, p, i)[0].astype(BF16) for c, (b, p) in enumerate(SWA_CHAINS)]
        for c, (b, p) in enumerate(SWA_CHAINS):
            o = _unstack_heads(_dot(pn[c], _swa_window(vp_ref.at[b], vc_ref.at[b], p)), lane)
            o_ref[b, :, _slab(p)] = o.astype(BF16)

    cur, prev = _swa_specs()
    q3, k3, v3 = (a.reshape(B_LOC, SEQ, a.shape[1]) for a in (q_rot, k_dup, v_dup))
    return pl.pallas_call(
        body, name="swa_fwd", grid=(N_BLK,),
        in_specs=[cur(512), prev(256), cur(256), prev(256), cur(256), pl.BlockSpec(memory_space=pltpu.SMEM)],
        out_specs=cur(512),
        out_shape=jax.ShapeDtypeStruct((B_LOC, SEQ, 512), BF16),
        compiler_params=_params(("arbitrary",)),
    )(q3, k3, k3, v3, v3, sinks).reshape(T_LOC, 512)


def _tri(suffix):
    r, cidx = _row((BLK + 16, BLK)), _lane((BLK + 16, BLK))
    tri = (cidx > r) if suffix else (cidx < r)
    return (tri | (r >= BLK)).astype(BF16)


def _key_sums(tri, x):
    res = _dot(tri, x.astype(BF16))
    return res[:BLK], res[BLK:BLK + 1]


def _sb_softplus(zt, valid):
    neg_abs = lax.bitcast_convert_type(lax.bitcast_convert_type(zt, jnp.uint32) | jnp.uint32(0x80000000), F32)
    sp = jnp.maximum(zt, 0.0) + jnp.log(1.0 + jnp.exp(neg_abs))
    return sp if valid is None else jnp.where(valid, sp, 0.0)


def _sb_weights(zt, sp, later, valid):
    w = jnp.exp(zt - sp - later)
    return w if valid is None else jnp.where(valid, w, 0.0)


def _slab(pp):
    return slice(pp * SLAB, (pp + 1) * SLAB)


def _blk(j):
    return pl.ds(pl.multiple_of(j * BLK, BLK), BLK)


def _causal_t():
    return _row((BLK, 2 * BLK)) < (_lane((BLK, 2 * BLK)) & (BLK - 1))


def _sb_rows(b, j):
    return pl.ds(pl.multiple_of(b * SEQ + j * BLK, BLK), BLK)


SB_CHAINS = [(b, pp) for b in range(B_LOC) for pp in range(4)]


def sb_fwd(qb, kb, vb):
    def body(q_ref, k_ref, v_ref, o_ref, c_ref, vt_ref, ot_ref):
        for c, (b, pp) in enumerate(SB_CHAINS):
            for j in range(N_BLK):
                vt_ref[c, j] = v_ref[b * SEQ + j * BLK:b * SEQ + (j + 1) * BLK, _slab(pp)].T
        lane = _lane((BLK, SLAB))
        tri = _tri(True)
        valid = _causal_t()
        jrow = _row((N_BLK, 2 * BLK))
        chains = range(len(SB_CHAINS))

        def q_block(i, _):
            q2 = [_stack_heads(q_ref[_sb_rows(b, i), _slab(pp)], lane) for b, pp in SB_CHAINS]

            def key_block(j, carry, mask, first):
                zt = [_dot(k_ref[_sb_rows(b, j), _slab(pp)], q2[c], NT) for c, (b, pp) in enumerate(SB_CHAINS)]
                sp = [_sb_softplus(zt[c], mask) for c in chains]
                sums = [_key_sums(tri, sp[c]) for c in chains]
                w = [_sb_weights(zt[c], sp[c], sums[c][0] + carry[c], mask) for c in chains]
                for c in chains:
                    pv = _dot(vt_ref[c, j], w[c].astype(BF16))
                    if first:
                        ot_ref[c] = pv
                    else:
                        ot_ref[c] += pv
                return tuple(carry[c] + sums[c][1] for c in chains)

            def earlier(jj, state):
                carry, saved = state
                j = i - 1 - jj
                saved = tuple(jnp.where(jrow == j, carry[c], saved[c]) for c in chains)
                return key_block(j, carry, None, False), saved

            zero = tuple(jnp.zeros((1, 2 * BLK), F32) for _ in chains)
            carry = key_block(i, zero, valid, True)
            _, saved = lax.fori_loop(0, i, earlier, (carry, tuple(jnp.zeros((N_BLK, 2 * BLK), F32) for _ in chains)))
            for c, (b, pp) in enumerate(SB_CHAINS):
                o_ref[_sb_rows(b, i), _slab(pp)] = _unstack_heads(ot_ref[c].T, lane).astype(BF16)
                c_ref[c * N_BLK + i] = saved[c]
            return 0

        lax.fori_loop(0, N_BLK, q_block, 0)

    n_ch = len(SB_CHAINS)
    vmem = pl.BlockSpec(memory_space=pltpu.VMEM)
    return pl.pallas_call(
        body, name="sb_fwd",
        in_specs=[vmem] * 3, out_specs=[vmem] * 2,
        out_shape=[jax.ShapeDtypeStruct((T_LOC, 512), BF16), jax.ShapeDtypeStruct((n_ch * N_BLK, N_BLK, 2 * BLK), F32)],
        scratch_shapes=[pltpu.VMEM((n_ch, N_BLK, SLAB, BLK), BF16), pltpu.VMEM((n_ch, SLAB, 2 * BLK), F32)],
        compiler_params=pltpu.CompilerParams(vmem_limit_bytes=VMEM_LIMIT),
    )(qb, kb, vb)


def out_loss(o_a, o_b, ga, gb, x, target, wout):
    n_tiles = T_LOC // TM_FWD

    def body(oa_ref, ob_ref, ga_ref, gb_ref, x_ref, t_ref, w_ref,
             dout_ref, doa_ref, dob_ref, dga_ref, dgb_ref, dw_ref, loss_ref, acc_ref):
        step = pl.program_id(0)

        @pl.when(step == 0)
        def _():
            acc_ref[...] = jnp.zeros_like(acc_ref)
            loss_ref[...] = jnp.zeros_like(loss_ref)

        oa, ob, gav, gbv = (r[...].astype(F32) for r in (oa_ref, ob_ref, ga_ref, gb_ref))
        sa, sb = _sigmoid(gav), _sigmoid(gbv)
        silu_a, silu_b = gav * sa, gbv * sb
        y = jnp.concatenate([oa * silu_a, ob * silu_b], axis=1).astype(BF16)
        err = x_ref[...] + _dot(y, w_ref[...]) - t_ref[...]
        e2 = err * err
        part = jnp.sum(e2.reshape(TM_FWD // 8, 8, D_MODEL), axis=0)
        loss_ref[...] += functools.reduce(lambda a, b: a + b, [part[:, k * 128:(k + 1) * 128] for k in range(8)])
        dout = err * (1.0 / D_MODEL)
        dout_ref[...] = dout
        dob16 = dout.astype(BF16)
        for r0 in range(0, D_MODEL, ACC_ROWS):
            acc_ref[r0:r0 + ACC_ROWS, :] += _dot(y[:, r0:r0 + ACC_ROWS], dob16, TN)
        dy = _dot(dob16, w_ref[...], NT)
        dya, dyb = dy[:, :512], dy[:, 512:]
        doa_ref[...] = (dya * silu_a).astype(BF16)
        dob_ref[...] = (dyb * silu_b).astype(BF16)
        dga_ref[...] = (dya * oa * (sa * (1.0 + gav * (1.0 - sa)))).astype(BF16)
        dgb_ref[...] = (dyb * ob * (sb * (1.0 + gbv * (1.0 - sb)))).astype(BF16)

        @pl.when(step == n_tiles - 1)
        def _():
            dw_ref[...] = acc_ref[...].astype(BF16)

    def tile(w):
        return pl.BlockSpec((TM_FWD, w), lambda i: (i, 0))

    const = lambda i: (0, 0)
    return pl.pallas_call(
        body, name="out_loss", grid=(n_tiles,),
        in_specs=[tile(512)] * 4 + [tile(D_MODEL)] * 2 + [pl.BlockSpec((D_MODEL, D_MODEL), const)],
        out_specs=[tile(D_MODEL), tile(512), tile(512), tile(512), tile(512),
                   pl.BlockSpec((D_MODEL, D_MODEL), const), pl.BlockSpec((8, 128), const)],
        out_shape=[jax.ShapeDtypeStruct((T_LOC, D_MODEL), F32)] + [jax.ShapeDtypeStruct((T_LOC, 512), BF16)] * 4
        + [jax.ShapeDtypeStruct((D_MODEL, D_MODEL), BF16), jax.ShapeDtypeStruct((8, 128), F32)],
        scratch_shapes=[pltpu.VMEM((D_MODEL, D_MODEL), F32)],
        compiler_params=_params(("arbitrary",)),
    )(o_a, o_b, ga, gb, x, target, wout)


def swa_bwd(q_rot, k_dup, v_dup, o_a, d_oa, sinks):
    def body(q_ref, kp_ref, kc_ref, vp_ref, vc_ref, o_ref, do_ref, sinks_ref, dq_ref, dk_ref, dv_ref, dsink_ref):
        i = pl.program_id(0)

        @pl.when(i == 0)
        def _():
            dk_ref[...] = jnp.zeros_like(dk_ref)
            dv_ref[...] = jnp.zeros_like(dv_ref)
            dsink_ref[...] = jnp.zeros_like(dsink_ref)

        lane = _lane((BLK, SLAB))
        rows_prev, rows_cur = _blk(jnp.maximum(i - 1, 0)), _blk(i)
        chains = range(len(SWA_CHAINS))
        q2 = [_stack_heads(q_ref[b, :, _slab(p)], lane) for b, p in SWA_CHAINS]
        do2 = [_stack_heads(do_ref[b, :, _slab(p)], lane) for b, p in SWA_CHAINS]
        keys = [_swa_window(kp_ref.at[b], kc_ref.at[b], p) for b, p in SWA_CHAINS]
        s = [_dot(q2[c], keys[c], NT) for c in chains]
        dp = [_dot(do2[c], _swa_window(vp_ref.at[b], vc_ref.at[b], p), NT) for c, (b, p) in enumerate(SWA_CHAINS)]
        ds, pn16, cols = [], [], []
        for c, (b, p) in enumerate(SWA_CHAINS):
            pn, p_sink = _swa_probs(s[c], sinks_ref, p, i)
            o = o_ref[b, :, _slab(p)].astype(F32)
            delta =jnp.sum(do2[c].astype(F32) * jnp.concatenate([o, o], axis=0), axis=-1, keepdims=True)
            ds.append((pn * (dp[c] - delta)).astype(BF16))
            pn16.append(pn.astype(BF16))
            cols.append(-p_sink * delta)
        for c, (b, p) in enumerate(SWA_CHAINS):
            dq_ref[b, :, _slab(p)] = _unstack_heads(_dot(ds[c], keys[c]), lane) * Q_SCALE
        dk2 = [_dot(ds[c], q2[c], TN) for c in chains]
        dv2 = [_dot(pn16[c], do2[c], TN) for c in chains]
        for c, (b, p) in enumerate(SWA_CHAINS):
            gsl = _slab(p // 2)
            dk_ref[b, rows_prev, gsl] += dk2[c][:BLK]
            dk_ref[b, rows_cur, gsl] += dk2[c][BLK:]
            dv_ref[b, rows_prev, gsl] += dv2[c][:BLK]
            dv_ref[b, rows_cur, gsl] += dv2[c][BLK:]
            for e in range(2):
                dsink_ref[2 * p + e:2 * p + e + 1, :] += jnp.sum(cols[c][e * BLK:(e + 1) * BLK], axis=0, keepdims=True)

    cur, prev = _swa_specs()
    whole = pl.BlockSpec((B_LOC, SEQ, 256), lambda i: (0, 0, 0))
    q3, k3, v3, o3, do3 = (a.reshape(B_LOC, SEQ, a.shape[1]) for a in (q_rot, k_dup, v_dup, o_a, d_oa))
    dq, dk, dv, dsink = pl.pallas_call(
        body, name="swa_bwd", grid=(N_BLK,),
        in_specs=[cur(512), prev(256), cur(256), prev(256), cur(256), cur(512), cur(512),
                  pl.BlockSpec(memory_space=pltpu.SMEM)],
        out_specs=[cur(512), whole, whole, pl.BlockSpec((8, 128), lambda i: (0, 0))],
        out_shape=[jax.ShapeDtypeStruct((B_LOC, SEQ, 512), F32), jax.ShapeDtypeStruct((B_LOC, SEQ, 256), F32),
                   jax.ShapeDtypeStruct((B_LOC, SEQ, 256), F32), jax.ShapeDtypeStruct((8, 128), F32)],
        compiler_params=_params(("arbitrary",)),
    )(q3, k3, k3, v3, v3, o3, do3, sinks)
    return dq.reshape(T_LOC, 512), dk.reshape(T_LOC, 256), dv.reshape(T_LOC, 256), dsink


def sb_bwd(qb, kb, vb, d_ob, carries, dwout):
    def body(q_ref, k_ref, v_ref, do_ref, c_ref, dw_hbm, dq_ref, dk_ref, dv_ref, rw_hbm, kt_ref, dqt_ref,
             rw_send, rw_recv, rw_local):
        start_dwout, finish_dwout = _direct_exchange(
            lambda dev: dw_hbm.at[pl.ds(_lin(dev) * OUT_SHARD, OUT_SHARD), :], lambda dev: rw_hbm.at[_lin(dev)],
            rw_send, rw_recv, rw_local)
        start_dwout()
        for c, (b, pp) in enumerate(SB_CHAINS):
            for j in range(N_BLK):
                kt_ref[c, j] = k_ref[b * SEQ + j * BLK:b * SEQ + (j + 1) * BLK, _slab(pp)].T
        dk_ref[...] = jnp.zeros_like(dk_ref)
        dv_ref[...] = jnp.zeros_like(dv_ref)
        dqt_ref[...] = jnp.zeros_like(dqt_ref)
        lane = _lane((BLK, SLAB))
        tri_after, tri_before = _tri(True), _tri(False)
        valid = _causal_t()
        jrow = _row((N_BLK, 2 * BLK))
        chains = range(len(SB_CHAINS))

        def q_block(i, _):
            q2 = [_stack_heads(q_ref[_sb_rows(b, i), _slab(pp)], lane) for b, pp in SB_CHAINS]
            do2 = [_stack_heads(do_ref[_sb_rows(b, i), _slab(pp)], lane) for b, pp in SB_CHAINS]

            def key_block(j, carry_sp, before_u, mask):
                at = [(_sb_rows(b, j), _slab(pp)) for b, pp in SB_CHAINS]
                zt = [_dot(k_ref[at[c]], q2[c], NT) for c in chains]
                dw = [_dot(v_ref[at[c]], do2[c], NT) for c in chains]
                sp = [_sb_softplus(zt[c], mask) for c in chains]
                later = [_key_sums(tri_after, sp[c])[0] for c in chains]
                w = [_sb_weights(zt[c], sp[c], later[c] + carry_sp[c], mask) for c in chains]
                u = [dw[c] * w[c] for c in chains]
                for c in chains:
                    dv_ref[at[c]] += _dot(w[c].astype(BF16), do2[c])
                sums = [_key_sums(tri_before, u[c]) for c in chains]
                dz16 = []
                for c in chains:
                    sig = jnp.exp(zt[c] - sp[c])
                    dz = u[c] - sig * (u[c] + before_u[c] + sums[c][0])
                    if mask is not None:
                        dz = jnp.where(mask, dz, 0.0)
                    dz16.append(dz.astype(BF16))
                for c in chains:
                    dk_ref[at[c]] += _dot(dz16[c], q2[c])
                    dqt_ref[c] += _dot(kt_ref[c, j], dz16[c])
                return tuple(before_u[c] + sums[c][1] for c in chains)

            def earlier(j, before_u):
                carry_sp = [jnp.sum(jnp.where(jrow == j, c_ref[c * N_BLK + i], 0.0), axis=0, keepdims=True)
                            for c in chains]
                return key_block(j, carry_sp, before_u, None)

            zero = tuple(jnp.zeros((1, 2 * BLK), F32) for _ in chains)
            before_u = lax.fori_loop(0, i, earlier, zero)
            key_block(i, zero, before_u, valid)
            for c, (b, pp) in enumerate(SB_CHAINS):
                dq_ref[_sb_rows(b, i), _slab(pp)] = (_unstack_heads(dqt_ref[c].T, lane) * Q_SCALE).astype(BF16)
                dqt_ref[c] = jnp.zeros((SLAB, 2 * BLK), F32)
            return 0

        lax.fori_loop(0, N_BLK, q_block, 0)
        finish_dwout()

    n_ch = len(SB_CHAINS)
    vmem, hbm = pl.BlockSpec(memory_space=pltpu.VMEM), pl.BlockSpec(memory_space=pl.ANY)
    return pl.pallas_call(
        body, name="sb_bwd",
        in_specs=[vmem] * 5 + [hbm], out_specs=[vmem] * 3 + [hbm],
        out_shape=[jax.ShapeDtypeStruct((T_LOC, 512), BF16)] + [jax.ShapeDtypeStruct((T_LOC, 512), F32)] * 2
        + [jax.ShapeDtypeStruct((N_DEV, OUT_SHARD, D_MODEL), BF16)],
        scratch_shapes=[pltpu.VMEM((n_ch, N_BLK, SLAB, BLK), BF16), pltpu.VMEM((n_ch, SLAB, 2 * BLK), F32),
                        pltpu.SemaphoreType.DMA((7,)), pltpu.SemaphoreType.DMA((7,)), pltpu.SemaphoreType.DMA],
        compiler_params=pltpu.CompilerParams(vmem_limit_bytes=VMEM_LIMIT),
    )(qb, kb, vb, d_ob, carries, dwout)


def bwd_dw(x, norm_gain, dq_rot, dk_dup, dv_dup, qa_raw, ka_raw, cos, sin_s, q_gain2, k_gain2, dga, dgb, dqb, dkb, dvb):
    n_tiles = T_LOC // TM_DW

    def body(x_ref, ng_ref, dq_ref, dk_ref, dv_ref, qa_ref, ka_ref, cos_ref, sin_ref, qg_ref, kg_ref,
             dga_ref, dgb_ref, dqb_ref, dkb_ref, dvb_ref,
             dproj_ref, dw_hbm, dqg_ref, dkg_ref, acc_ref, stage_ref):
        step = pl.program_id(0)

        @pl.when(step == 0)
        def _():
            acc_ref[...] = jnp.zeros_like(acc_ref)
            dqg_ref[...] = jnp.zeros_like(dqg_ref)
            dkg_ref[...] = jnp.zeros_like(dkg_ref)

        lane = _lane((TM_DW, SLAB))
        bd = _head_blockdiag()
        cos, sin_s = cos_ref[...], sin_ref[...]

        def norm_rope_bwd(d_rot, raw, gain2):
            dy = d_rot * cos + _swap_half(d_rot * sin_s, lane)
            r = lax.rsqrt(_head_sum(raw * raw, bd) * (1.0 / HEAD_DIM) + EPS)
            xhat = raw * r
            dgain = jnp.sum(dy * xhat, axis=0, keepdims=True)
            dxh = dy * gain2
            mean = _head_sum(dxh * xhat, bd) * (1.0 / HEAD_DIM)
            return r * (dxh - xhat * mean), dgain

        def fold_dup(d_dup):
            a, b2 = d_dup[:, :SLAB], d_dup[:, SLAB:]
            return jnp.where(lane < HEAD_DIM, a + pltpu.roll(a, HEAD_DIM, 1), b2 + pltpu.roll(b2, HEAD_DIM, 1))

        pieces = []
        dqg = jnp.zeros((1, SLAB), F32)
        for p in range(4):
            sl = slice(p * SLAB, (p + 1) * SLAB)
            d_raw, dg = norm_rope_bwd(dq_ref[:, sl], qa_ref[:, sl], qg_ref[...])
            pieces.append(d_raw.astype(BF16))
            dqg = dqg + dg
        d_raw, dkg = norm_rope_bwd(fold_dup(dk_ref[...]), ka_ref[...], kg_ref[...])
        pieces.append(d_raw.astype(BF16))
        pieces.append(fold_dup(dv_ref[...]).astype(BF16))
        pieces += [dga_ref[...], dqb_ref[...], dkb_ref[...].astype(BF16), dvb_ref[...].astype(BF16),
                   dgb_ref[...]]
        dproj = jnp.concatenate(pieces, axis=1)
        dproj_ref[...] = dproj
        dqg_ref[0:1, :] += dqg + pltpu.roll(dqg, HEAD_DIM, 1)
        dkg_ref[0:1, :] += dkg + pltpu.roll(dkg, HEAD_DIM, 1)

        xv = x_ref[...]
        rstd = lax.rsqrt(jnp.mean(xv * xv, axis=-1, keepdims=True) + EPS)
        h = (xv * rstd * ng_ref[...]).astype(BF16)
        for r0 in range(0, IN_WIDTH, ACC_ROWS):
            acc_ref[r0:r0 + ACC_ROWS, :] += _dot(dproj[:, r0:r0 + ACC_ROWS], h, TN)

        @pl.when(step == n_tiles - 1)
        def _():
            for r0 in range(0, IN_WIDTH, ACC_ROWS):
                stage_ref[...] = acc_ref[r0:r0 + ACC_ROWS, :].astype(BF16)
                pltpu.sync_copy(stage_ref, dw_hbm.at[r0:r0 + ACC_ROWS, :])

    def tile(w):
        return pl.BlockSpec((TM_DW, w), lambda i: (i, 0))

    def whole(a):
        return pl.BlockSpec(a.shape, lambda i: (0, 0))

    const = lambda i: (0, 0)
    return pl.pallas_call(
        body, name="bwd_dw", grid=(n_tiles,),
        in_specs=[tile(D_MODEL), whole(norm_gain),
                  tile(512), tile(256), tile(256), tile(512), tile(128), tile(128), tile(128),
                  whole(q_gain2), whole(k_gain2), tile(512), tile(512), tile(512), tile(512), tile(512)],
        out_specs=[tile(IN_WIDTH), pl.BlockSpec(memory_space=pl.ANY),
                   pl.BlockSpec((8, SLAB), const), pl.BlockSpec((8, SLAB), const)],
        out_shape=[jax.ShapeDtypeStruct((T_LOC, IN_WIDTH), BF16), jax.ShapeDtypeStruct((IN_WIDTH, D_MODEL), BF16),
                   jax.ShapeDtypeStruct((8, SLAB), F32), jax.ShapeDtypeStruct((8, SLAB), F32)],
        scratch_shapes=[pltpu.VMEM((IN_WIDTH, D_MODEL), F32), pltpu.VMEM((ACC_ROWS, D_MODEL), BF16)],
        compiler_params=_params(("arbitrary",)),
    )(x, norm_gain, dq_rot, dk_dup, dv_dup, qa_raw, ka_raw, cos, sin_s, q_gain2, k_gain2, dga, dgb, dqb, dkb, dvb)


def bwd_dx(x, dout, norm_gain, win_t, dproj, dwin_t, dqg, dkg, dsink, loss_part):
    n_tiles = T_LOC // TM
    rows_per = IN_SHARD
    step_sums, step_merge = 3, 7

    def body(x_ref, dout_ref, ng_ref, w_hbm, dp_ref, a_hbm, dqg_ref, dkg_ref, dsink_ref, loss_ref,
             gx_ref, ra_hbm, rs_hbm, w_ref, dng_ref, s_ref, own_ref, sib_ref, snd_ref, extra_ref,
             w_sem, d2d_send, d2d_recv, ici_send, ici_recv, own_sems, s_send, s_recv, out_sem):
        step = pl.program_id(0)
        x, y, c = _mesh_pos()
        me, sibling = (x, y, c), (x, y, 1 - c)
        chips = {"own": (x, y), "x": (1 - x, y), "y": (x, 1 - y), "d": (1 - x, 1 - y)}
        index = {"own": 0, "x": 1, "y": 2, "d": 3}
        order = ("d", "x", "y", "own")

        def rows(pos):
            return a_hbm.at[pl.ds(_lin(pos) * rows_per, rows_per), :]

        def to_sibling(k):
            return pltpu.make_async_remote_copy(
                src_ref=rows((*chips[k], 1 - c)), dst_ref=sib_ref.at[index[k]],
                send_sem=d2d_send.at[index[k]], recv_sem=d2d_recv.at[index[k]], device_id=sibling, device_id_type=MESH)

        def mine(k):
            return pltpu.make_async_copy(rows((*chips[k], c)), own_ref.at[index[k]], own_sems.at[index[k]])

        def ici(n, to_chip, dst):
            return pltpu.make_async_remote_copy(
                src_ref=snd_ref.at[n], dst_ref=dst, send_sem=ici_send.at[n], recv_sem=ici_recv.at[n],
                device_id=(*chips[to_chip], c), device_id_type=MESH)

        def chip_sum(k):
            to_sibling(k).wait_recv()
            mine(k).wait()
            return own_ref[index[k]].astype(F32) + sib_ref[index[k]].astype(F32)

        def by_core(fn):
            pl.when(c == 0)(lambda: fn("x", "y"))
            pl.when(c == 1)(lambda: fn("y", "x"))

        @pl.when(step == 0)
        def _():
            cp = pltpu.make_async_copy(w_hbm, w_ref, w_sem)
            cp.start()
            for k in order:
                to_sibling(k).start()
                mine(k).start()
            dng_ref[...] = jnp.zeros_like(dng_ref)
            cp.wait()

        @pl.when(step == step_sums)
        def _():
            def first_sends(direct, via):
                snd_ref[0] = chip_sum("d").astype(BF16)
                ici(0, direct, extra_ref).start()
                snd_ref[1] = chip_sum(direct).astype(BF16)
                ici(1, direct, ra_hbm.at[index[direct]]).start()
            by_core(first_sends)

        @pl.when(step == step_merge)
        def _():
            def merge(direct, via):
                merged = chip_sum(via)
                ici(0, direct, extra_ref).wait_recv()
                snd_ref[2] = (merged + extra_ref[...].astype(F32)).astype(BF16)
                ici(2, via, ra_hbm.at[index[via]]).start()
                own_ref[0] = chip_sum("own").astype(BF16)
                pltpu.make_async_copy(own_ref.at[0], ra_hbm.at[0], out_sem).start()
            by_core(merge)

        xv = x_ref[...]
        rstd = lax.rsqrt(jnp.mean(xv * xv, axis=-1, keepdims=True) + EPS)
        xhat = xv * rstd
        gain = ng_ref[...]
        dh = _dot(dp_ref[...], w_ref[...])
        dng_ref[0:1, :] += jnp.sum(dh * xhat, axis=0, keepdims=True)
        dxh = dh * gain
        gx_ref[...] = dout_ref[...] + rstd * (dxh - xhat * jnp.mean(dxh * xhat, axis=-1, keepdims=True))

        @pl.when(step == n_tiles - 1)
        def _():
            s_ref[...] = jnp.concatenate(
                [dng_ref[...], dqg_ref[...], dkg_ref[...], dsink_ref[...], loss_ref[...]], axis=1)
            start_small, finish_small = _direct_exchange(
                lambda dev: s_ref, lambda dev: rs_hbm.at[_lin(dev)], s_send, s_recv, out_sem)

            def finish(direct, via):
                ici(1, direct, ra_hbm.at[index[direct]]).wait_recv()
                ici(2, via, ra_hbm.at[index[via]]).wait_recv()
                for n, to in ((0, direct), (1, direct), (2, via)):
                    ici(n, to, extra_ref).wait_send()
            by_core(finish)
            pltpu.make_async_copy(own_ref.at[0], ra_hbm.at[0], out_sem).wait()
            for k in order:
                to_sibling(k).wait_send()
            start_small()
            finish_small()

    def tile(w):
        return pl.BlockSpec((TM, w), lambda i: (i, 0))

    def whole(a):
        return pl.BlockSpec(a.shape, lambda i: (0, 0))

    hbm = pl.BlockSpec(memory_space=pl.ANY)
    block = (rows_per, D_MODEL)
    return pl.pallas_call(
        body, name="bwd_dx", grid=(n_tiles,),
        in_specs=[tile(D_MODEL), tile(D_MODEL), whole(norm_gain), hbm, tile(IN_WIDTH), hbm,
                  whole(dqg), whole(dkg), whole(dsink), whole(loss_part)],
        out_specs=[tile(D_MODEL), hbm, hbm],
        out_shape=[jax.ShapeDtypeStruct((T_LOC, D_MODEL), F32), jax.ShapeDtypeStruct((3,) + block, BF16),
                   jax.ShapeDtypeStruct((N_DEV, 8, SMALL_W), F32)],
        scratch_shapes=[pltpu.VMEM((IN_WIDTH, D_MODEL), BF16), pltpu.VMEM((8, D_MODEL), F32),
                        pltpu.VMEM((8, SMALL_W), F32),
                        pltpu.VMEM((4,) + block, BF16), pltpu.VMEM((4,) + block, BF16), pltpu.VMEM((3,) + block, BF16),
                        pltpu.VMEM(block, BF16),
                        pltpu.SemaphoreType.DMA, pltpu.SemaphoreType.DMA((4,)), pltpu.SemaphoreType.DMA((4,)),
                        pltpu.SemaphoreType.DMA((3,)), pltpu.SemaphoreType.DMA((3,)), pltpu.SemaphoreType.DMA((4,)),
                        pltpu.SemaphoreType.DMA((7,)), pltpu.SemaphoreType.DMA((7,)), pltpu.SemaphoreType.DMA],
        compiler_params=_params(("arbitrary",)),
    )(x, dout, norm_gain, win_t, dproj, dwin_t, dqg, dkg, dsink, loss_part)


def _adamw(w, g, m, v):
    m = ADAM_B1 * m + (1.0 - ADAM_B1) * g
    v = ADAM_B2 * v + (1.0 - ADAM_B2) * (g * g)
    m_hat = m / (1.0 - ADAM_B1 ** ADAM_STEP)
    v_hat = v / (1.0 - ADAM_B2 ** ADAM_STEP)
    delta = -ADAM_LR * (m_hat / (jnp.sqrt(v_hat) + ADAM_EPS) + ADAM_WD * w)
    return delta, m, v


def _sum_slots(r_ref):
    g = r_ref[0].astype(F32)
    for s in range(1, r_ref.shape[0]):
        g = g + r_ref[s].astype(F32)
    return g


def adamw_rows(name, recv, w, m, v):
    def body(r_ref, w_ref, m_ref, v_ref, g_ref, d_ref, nm_ref, nv_ref):
        g = _sum_slots(r_ref)
        g_ref[...] = g
        d_ref[...], nm_ref[...], nv_ref[...] = _adamw(w_ref[...], g, m_ref[...], v_ref[...])

    return pl.pallas_call(
        body, name=name,
        out_shape=[jax.ShapeDtypeStruct(w.shape, F32)] * 4,
        compiler_params=pltpu.CompilerParams(vmem_limit_bytes=VMEM_LIMIT),
    )(recv, w, m, v)


def adamw_small(recv, weights, moments_m, moments_v):
    n = len(weights)

    def body(r_ref, *refs):
        ins, outs = refs[:3 * n], refs[3 * n:]
        s = _sum_slots(r_ref)
        eye = (_row((8, SLAB)) == _lane((8, SLAB))).astype(F32)
        sinks = jnp.sum(s[:, 1280:1408] * eye, axis=0, keepdims=True)
        grads = [s[0:1, :D_MODEL], s[0:1, 1024:1024 + HEAD_DIM], s[0:1, 1152:1152 + HEAD_DIM], sinks[:, :8]]
        for k in range(n):
            outs[k][...] = grads[k]
            outs[n + k][...], outs[2 * n + k][...], outs[3 * n + k][...] = _adamw(
                ins[k][...], grads[k], ins[n + k][...], ins[2 * n + k][...])
        loss = jnp.sum(jnp.sum(s[:, 1408:1536], axis=1, keepdims=True), axis=0, keepdims=True) * (0.5 / D_MODEL)
        outs[4 * n][...] = loss

    res = pl.pallas_call(
        body, name="adamw_small",
        out_shape=[jax.ShapeDtypeStruct(w.shape, F32) for w in weights] * 4 + [jax.ShapeDtypeStruct((1, 1), F32)],
        compiler_params=pltpu.CompilerParams(vmem_limit_bytes=VMEM_LIMIT),
    )(recv, *weights, *moments_m, *moments_v)
    return res[:n], res[n:2 * n], res[2 * n:3 * n], res[3 * n:4 * n], res[4 * n]


def kernel(x, positions, norm_gain, w_in, q_norm_gain, k_norm_gain, sinks, w_out, loss_target, m_norm_gain, m_w_in, m_q_norm_gain, m_k_norm_gain, m_sinks, m_w_out, v_norm_gain, v_w_in, v_q_norm_gain, v_k_norm_gain, v_sinks, v_w_out):
    x2 = x.reshape(T_LOC, D_MODEL)
    tgt2 = loss_target.reshape(T_LOC, D_MODEL)
    pos2 = positions.reshape(1, T_LOC)
    half = HEAD_DIM // 2
    inv_freq = ROPE_THETA ** (-jnp.arange(half, dtype=F32) * 2.0 / HEAD_DIM)
    inv_freq = jnp.tile(inv_freq, SLAB // half).reshape(SLAB, 1)
    sin_sign = jnp.tile(jnp.concatenate([-jnp.ones((half,), F32), jnp.ones((half,), F32)]), 2).reshape(1, SLAB)
    q_gain2 = jnp.tile(q_norm_gain, (1, 2))
    k_gain2 = jnp.tile(k_norm_gain, (1, 2))

    win_t = gather_weights(w_in.reshape(D_MODEL, IN_SHARD).T)

    (qa_raw, ka_raw, q_rot, k_dup, v_dup, ga, qb, kb, vb, gb, cos, sin_s, wout) = fwd_proj(
        x2, pos2, norm_gain, win_t, inv_freq, sin_sign, q_gain2, k_gain2, w_out.reshape(OUT_SHARD, D_MODEL).astype(BF16))
    o_a = swa_fwd(q_rot, k_dup, v_dup, sinks)
    o_b, carries = sb_fwd(qb, kb, vb)
    dout, d_oa, d_ob, dga, dgb, dwout, loss_part = out_loss(o_a, o_b, ga, gb, x2, tgt2, wout)
    dq_rot, dk_dup, dv_dup, dsink = swa_bwd(q_rot, k_dup, v_dup, o_a, d_oa, sinks)
    dqb, dkb, dvb, r_out = sb_bwd(qb, kb, vb, d_ob, carries, dwout)
    dproj, dwin_t, dqg, dkg = bwd_dw(
        x2, norm_gain, dq_rot, dk_dup, dv_dup, qa_raw, ka_raw, cos, sin_s, q_gain2, k_gain2, dga, dgb, dqb, dkb, dvb)
    grad_x, r_win, r_small = bwd_dx(x2, dout, norm_gain, win_t, dproj, dwin_t, dqg, dkg, dsink, loss_part)

    w_in2, m_in2, v_in2 = (a.reshape(D_MODEL, IN_SHARD).T for a in (w_in, m_w_in, v_w_in))
    w_out2, m_out2, v_out2 = (a.reshape(OUT_SHARD, D_MODEL) for a in (w_out, m_w_out, v_w_out))
    big_in = adamw_rows("adamw_w_in", r_win, w_in2, m_in2, v_in2)
    big_out = adamw_rows("adamw_w_out", r_out, w_out2, m_out2, v_out2)
    *small_out, loss = adamw_small(
        r_small, (norm_gain, q_norm_gain, k_norm_gain, sinks),
        (m_norm_gain, m_q_norm_gain, m_k_norm_gain, m_sinks), (v_norm_gain, v_q_norm_gain, v_k_norm_gain, v_sinks))

    def leaves(k):
        ng, qg, kg, sk = small_out[k]
        return (ng, big_in[k].T.reshape(1, D_MODEL, IN_SHARD), qg, kg, sk, big_out[k].reshape(1, OUT_SHARD, D_MODEL))

    return (loss.reshape(()), grad_x.reshape(B_LOC, SEQ, D_MODEL), *leaves(0), *leaves(1), *leaves(2), *leaves(3))
```

```python
import functools

import jax
import jax.numpy as jnp
from jax import lax
from jax.experimental import pallas as pl
from jax.experimental.pallas import tpu as pltpu

F32 = jnp.float32
BF16 = jnp.bfloat16

N_DEV = 8
D_MODEL = 1024
SEQ = 2048
B_LOC = 2
T_LOC = B_LOC * SEQ
HEAD_DIM = 64
HEAD_SHIFT = 6
BLK = 128
N_BLK = SEQ // BLK
SLAB = 128
IN_WIDTH = 3328
IN_SHARD = IN_WIDTH // N_DEV
OUT_SHARD = D_MODEL // N_DEV
EPS = 1e-6
ROPE_THETA = 10000.0
Q_SCALE = 0.125
R_QA, R_KA, R_VA, R_GA, R_QB, R_KB, R_VB, R_GB, R_END = 0, 512, 640, 768, 1280, 1792, 2304, 2816, 3328
SMALL_W = 1536
ADAM_LR, ADAM_B1, ADAM_B2, ADAM_EPS, ADAM_WD, ADAM_STEP = 0.001, 0.9, 0.999, 1e-08, 0.01, 10
TM = 256
TM_DW = 512
TM_FWD = 512
ACC_ROWS = 256
VMEM_LIMIT = 56 * 1024 * 1024

MESH = pl.DeviceIdType.MESH
NT = (((1,), (1,)), ((), ()))
TN = (((0,), (0,)), ((), ()))


def _params(sem, limit=VMEM_LIMIT):
    return pltpu.CompilerParams(dimension_semantics=sem, vmem_limit_bytes=limit)


def _dot(a, b, dims=None):
    if dims is None:
        return jnp.dot(a, b, preferred_element_type=F32)
    return lax.dot_general(a, b, dims, preferred_element_type=F32)


def _lane(shape):
    return lax.broadcasted_iota(jnp.int32, shape, len(shape) - 1)


def _row(shape):
    return lax.broadcasted_iota(jnp.int32, shape, 0)


def _head_blockdiag():
    return ((_row((SLAB, SLAB)) >> HEAD_SHIFT) == (_lane((SLAB, SLAB)) >> HEAD_SHIFT)).astype(BF16)


def _head_sum(x, bd):
    return _dot(x.astype(BF16), bd)


def _swap_half(y, lane):
    return jnp.where((lane & 32) != 0, pltpu.roll(y, 32, 1), pltpu.roll(y, 96, 1))


def _stack_heads(q, lane):
    zero = jnp.zeros_like(q)
    return jnp.concatenate([jnp.where(lane < HEAD_DIM, q, zero), jnp.where(lane >= HEAD_DIM, q, zero)], axis=0)


def _unstack_heads(x2, lane):
    return jnp.where(lane < HEAD_DIM, x2[:BLK], x2[BLK:])


def _sigmoid(x):
    return 1.0 / (1.0 + jnp.exp(-x))


def _mesh_pos():
    return lax.axis_index("x"), lax.axis_index("y"), lax.axis_index("c")


def _flip(pos, mask):
    return tuple(1 - p if m else p for p, m in zip(pos, mask))


def _lin(pos):
    return 4 * pos[0] + 2 * pos[1] + pos[2]


DEV_FLIPS = [(fx, fy, fc) for fx in (0, 1) for fy in (0, 1) for fc in (0, 1)][1:]


def _direct_exchange(src_for, dst_slot, send_sems, recv_sems, local_sem):
    me = _mesh_pos()

    def copy(k, to):
        return pltpu.make_async_remote_copy(
            src_ref=src_for(to), dst_ref=dst_slot(me), send_sem=send_sems.at[k], recv_sem=recv_sems.at[k],
            device_id=to, device_id_type=MESH)

    def landed(k, frm):
        return pltpu.make_async_remote_copy(
            src_ref=src_for(frm), dst_ref=dst_slot(frm), send_sem=send_sems.at[k], recv_sem=recv_sems.at[k],
            device_id=frm, device_id_type=MESH)

    local = None if local_sem is None else pltpu.make_async_copy(src_for(me), dst_slot(me), local_sem)
    peers = [_flip(me, f) for f in DEV_FLIPS]

    def start():
        if local is not None:
            local.start()
        for k, to in enumerate(peers):
            copy(k, to).start()

    def finish():
        for k, frm in enumerate(peers):
            landed(k, frm).wait_recv()
        for k, to in enumerate(peers):
            copy(k, to).wait_send()
        if local is not None:
            local.wait()

    return start, finish


def gather_weights(shard):
    m = shard.shape[0]

    def body(f32_ref, o_ref, a_ref, ici_send, ici_recv, d2d_send, d2d_recv, local_sem):
        a_ref[...] = f32_ref[...].astype(BF16)
        x, y, c = _mesh_pos()
        me, sibling = (x, y, c), (x, y, 1 - c)
        chip_x, chip_y, chip_d = (1 - x, y), (x, 1 - y), (1 - x, 1 - y)

        def rows(pos):
            return o_ref.at[pl.ds(_lin(pos) * m, m), :]

        def ici(k, block, to, src=None):
            return pltpu.make_async_remote_copy(
                src_ref=rows(block) if src is None else src, dst_ref=rows(block),
                send_sem=ici_send.at[k], recv_sem=ici_recv.at[k], device_id=to, device_id_type=MESH)

        def d2d(k, chip, mine, src=None):
            block = (*chip, c) if mine else (*chip, 1 - c)
            return pltpu.make_async_remote_copy(
                src_ref=rows(block) if src is None else src, dst_ref=rows(block),
                send_sem=d2d_send.at[k], recv_sem=d2d_recv.at[k], device_id=sibling, device_id_type=MESH)

        local = pltpu.make_async_copy(a_ref, rows(me), local_sem)
        local.start()
        sends = [ici(0, me, (*chip_x, c), src=a_ref), ici(1, me, (*chip_y, c), src=a_ref),
                 d2d(0, (x, y), True, src=a_ref)]
        for cp in sends:
            cp.start()

        def pass_on(first, k_first, second, k_second, onward):
            ici(k_first, (*first, c), me).wait_recv()
            relay = ici(2, (*first, c), (*onward, c))
            relay.start()
            hand = [d2d(1 + k_first, first, True)]
            hand[0].start()
            ici(k_second, (*second, c), me).wait_recv()
            hand.append(d2d(1 + k_second, second, True))
            hand[1].start()
            ici(2, (*chip_d, c), me).wait_recv()
            hand.append(d2d(3, chip_d, True))
            hand[2].start()
            for cp in [relay] + hand:
                cp.wait_send()

        @pl.when(c == 0)
        def _():
            pass_on(chip_y, 1, chip_x, 0, chip_x)

        @pl.when(c == 1)
        def _():
            pass_on(chip_x, 0, chip_y, 1, chip_y)

        for k, chip in enumerate([(x, y), chip_x, chip_y, chip_d]):
            d2d(k, chip, False).wait_recv()
        for cp in sends:
            cp.wait_send()
        local.wait()

    vmem = pl.BlockSpec(memory_space=pltpu.VMEM)
    return pl.pallas_call(
        body, name="gather_weights",
        out_shape=jax.ShapeDtypeStruct((N_DEV * m, shard.shape[1]), BF16),
        in_specs=[vmem], out_specs=vmem,
        scratch_shapes=[pltpu.VMEM(shard.shape, BF16), pltpu.SemaphoreType.DMA((3,)), pltpu.SemaphoreType.DMA((3,)),
                        pltpu.SemaphoreType.DMA((4,)), pltpu.SemaphoreType.DMA((4,)), pltpu.SemaphoreType.DMA],
        compiler_params=pltpu.CompilerParams(vmem_limit_bytes=VMEM_LIMIT),
    )(shard)


def _norm_rope(xs, gain2, cos, sin_s, bd, lane):
    r = lax.rsqrt(_head_sum(xs * xs, bd) * (1.0 / HEAD_DIM) + EPS)
    y = xs * r * gain2
    return y * cos + _swap_half(y, lane) * sin_s


def _dup_heads(xs, lane):
    r = pltpu.roll(xs, HEAD_DIM, 1)
    lo = lane < HEAD_DIM
    return jnp.concatenate([jnp.where(lo, xs, r), jnp.where(lo, r, xs)], axis=1)


def fwd_proj(x, pos, norm_gain, win_t, inv_freq, sin_sign, q_gain2, k_gain2, wout_shard):
    n_tiles = T_LOC // TM_FWD

    def body(x_ref, pos_ref, ng_ref, w_ref, if_ref, sg_ref, qg_ref, kg_ref, ws_hbm,
             qa_raw_ref, ka_raw_ref, q_rot_ref, k_dup_ref, v_dup_ref, ga_ref, qb_ref, kb_ref, vb_ref, gb_ref,
             cos_ref, sin_ref, wo_hbm, wo_send, wo_recv, wo_local):
        start_wout, finish_wout = _direct_exchange(
            lambda dev: ws_hbm, lambda dev: wo_hbm.at[pl.ds(_lin(dev) * OUT_SHARD, OUT_SHARD), :],
            wo_send, wo_recv, wo_local)
        pl.when(pl.program_id(0) == 0)(start_wout)

        xv = x_ref[...]
        rstd = lax.rsqrt(jnp.mean(xv * xv, axis=-1, keepdims=True) + EPS)
        h = (xv * rstd * ng_ref[...]).astype(BF16)

        def proj(r0, r1):
            return _dot(h, w_ref[r0:r1, :], NT)

        ang_t = if_ref[...] * pos_ref[...].astype(F32)
        cos = jnp.cos(ang_t).T
        sin_s = jnp.sin(ang_t).T * sg_ref[...]
        cos_ref[...] = cos
        sin_ref[...] = sin_s
        lane = _lane((TM_FWD, SLAB))
        bd = _head_blockdiag()

        qa = proj(R_QA, R_KA)
        qa_raw_ref[...] = qa
        for p in range(4):
            sl = slice(p * SLAB, (p + 1) * SLAB)
            q_rot_ref[:, sl] = (_norm_rope(qa[:, sl], qg_ref[...], cos, sin_s, bd, lane) * Q_SCALE).astype(BF16)
        ka = proj(R_KA, R_VA)
        ka_raw_ref[...] = ka
        k_dup_ref[...] = _dup_heads(_norm_rope(ka, kg_ref[...], cos, sin_s, bd, lane), lane).astype(BF16)
        v_dup_ref[...] = _dup_heads(proj(R_VA, R_GA), lane).astype(BF16)
        ga_ref[...] = proj(R_GA, R_QB).astype(BF16)
        qb_ref[...] = (proj(R_QB, R_KB) * Q_SCALE).astype(BF16)
        kb_ref[...] = proj(R_KB, R_VB).astype(BF16)
        vb_ref[...] = proj(R_VB, R_GB).astype(BF16)
        gb_ref[...] = proj(R_GB, R_END).astype(BF16)
        pl.when(pl.program_id(0) == n_tiles - 1)(finish_wout)

    def tile(w):
        return pl.BlockSpec((TM_FWD, w), lambda i: (i, 0))

    def whole(a):
        return pl.BlockSpec(a.shape, lambda i: (0, 0))

    hbm = pl.BlockSpec(memory_space=pl.ANY)
    widths = [(512, F32), (128, F32), (512, BF16), (256, BF16), (256, BF16), (512, BF16), (512, BF16), (512, BF16),
              (512, BF16), (512, BF16), (128, F32), (128, F32)]
    return pl.pallas_call(
        body, name="fwd_proj", grid=(n_tiles,),
        in_specs=[tile(D_MODEL), pl.BlockSpec((1, TM_FWD), lambda i: (0, i)), whole(norm_gain), whole(win_t),
                  whole(inv_freq), whole(sin_sign),
                  whole(q_gain2), whole(k_gain2), hbm],
        out_specs=[tile(w) for w, _ in widths] + [hbm],
        out_shape=[jax.ShapeDtypeStruct((T_LOC, w), dt) for w, dt in widths]
        + [jax.ShapeDtypeStruct((D_MODEL, D_MODEL), BF16)],
        scratch_shapes=[pltpu.SemaphoreType.DMA((7,)), pltpu.SemaphoreType.DMA((7,)), pltpu.SemaphoreType.DMA],
        compiler_params=_params(("arbitrary",)),
    )(x, pos, norm_gain, win_t, inv_freq, sin_sign, q_gain2, k_gain2, wout_shard)


def _swa_window(prev_ref, cur_ref, p):
    gsl = _slab(p // 2)
    return jnp.concatenate([prev_ref[:, gsl], cur_ref[:, gsl]], axis=0)


def _swa_probs(s, sinks_ref, p, i):
    shape = (2 * BLK, 2 * BLK)
    r = _row(shape) & (BLK - 1)
    cidx = _lane(shape)
    valid = (cidx > r) & (cidx <= r + BLK) & ((cidx >= BLK) | (i > 0))
    s = jnp.where(valid, s, -jnp.inf)
    sink = jnp.where(_row((2 * BLK, 1)) < BLK, sinks_ref[0, 2 * p], sinks_ref[0, 2 * p + 1])
    m = jnp.maximum(jnp.max(s, axis=-1, keepdims=True), sink)
    e = jnp.exp(s - m)
    e_sink = jnp.exp(sink - m)
    den = jnp.sum(e, axis=-1, keepdims=True) + e_sink
    return e / den, e_sink / den


SWA_CHAINS = [(b, p) for b in range(B_LOC) for p in range(4)]


def _swa_specs():
    def cur(w):
        return pl.BlockSpec((B_LOC, BLK, w), lambda i: (0, i, 0))

    def prev(w):
        return pl.BlockSpec((B_LOC, BLK, w), lambda i: (0, jnp.maximum(i - 1, 0), 0))

    return cur, prev


def swa_fwd(q_rot, k_dup, v_dup, sinks):
    def body(q_ref, kp_ref, kc_ref, vp_ref, vc_ref, sinks_ref, o_ref):
        i = pl.program_id(0)
        lane = _lane((BLK, SLAB))
        s = [_dot(_stack_heads(q_ref[b, :, _slab(p)], lane), _swa_window(kp_ref.at[b], kc_ref.at[b], p), NT)
             for b, p in SWA_CHAINS]
        pn = [_swa_probs(s[c], sinks_ref, p, i)[0].astype(BF16) for c, (b, p) in enumerate(SWA_CHAINS)]
        for c, (b, p) in enumerate(SWA_CHAINS):
            o = _unstack_heads(_dot(pn[c], _swa_window(vp_ref.at[b], vc_ref.at[b], p)), lane)
            o_ref[b, :, _slab(p)] = o.astype(BF16)

    cur, prev = _swa_specs()
    q3, k3, v3 = (a.reshape(B_LOC, SEQ, a.shape[1]) for a in (q_rot, k_dup, v_dup))
    return pl.pallas_call(
        body, name="swa_fwd", grid=(N_BLK,),
        in_specs=[cur(512), prev(256), cur(256), prev(256), cur(256), pl.BlockSpec(memory_space=pltpu.SMEM)],
        out_specs=cur(512),
        out_shape=jax.ShapeDtypeStruct((B_LOC, SEQ, 512), BF16),
        compiler_params=_params(("arbitrary",)),
    )(q3, k3, k3, v3, v3, sinks).reshape(T_LOC, 512)


def _tri(suffix):
    r, cidx = _row((BLK + 16, BLK)), _lane((BLK + 16, BLK))
    tri = (cidx > r) if suffix else (cidx < r)
    return (tri | (r >= BLK)).astype(BF16)


def _key_sums(tri, x):
    res = _dot(tri, x.astype(BF16))
    return res[:BLK], res[BLK:BLK + 1]


def _sb_softplus(zt, valid):
    neg_abs = lax.bitcast_convert_type(lax.bitcast_convert_type(zt, jnp.uint32) | jnp.uint32(0x80000000), F32)
    sp = jnp.maximum(zt, 0.0) + jnp.log(1.0 + jnp.exp(neg_abs))
    return sp if valid is None else jnp.where(valid, sp, 0.0)


def _sb_weights(zt, sp, later, valid):
    w = jnp.exp(zt - sp - later)
    return w if valid is None else jnp.where(valid, w, 0.0)


def _slab(pp):
    return slice(pp * SLAB, (pp + 1) * SLAB)


def _blk(j):
    return pl.ds(pl.multiple_of(j * BLK, BLK), BLK)


def _causal_t():
    return _row((BLK, 2 * BLK)) < (_lane((BLK, 2 * BLK)) & (BLK - 1))


def _sb_rows(b, j):
    return pl.ds(pl.multiple_of(b * SEQ + j * BLK, BLK), BLK)


SB_CHAINS = [(b, pp) for b in range(B_LOC) for pp in range(4)]


def sb_fwd(qb, kb, vb):
    def body(q_ref, k_ref, v_ref, o_ref, c_ref, vt_ref, ot_ref):
        for c, (b, pp) in enumerate(SB_CHAINS):
            for j in range(N_BLK):
                vt_ref[c, j] = v_ref[b * SEQ + j * BLK:b * SEQ + (j + 1) * BLK, _slab(pp)].T
        lane = _lane((BLK, SLAB))
        tri = _tri(True)
        valid = _causal_t()
        jrow = _row((N_BLK, 2 * BLK))
        chains = range(len(SB_CHAINS))

        def q_block(i, _):
            q2 = [_stack_heads(q_ref[_sb_rows(b, i), _slab(pp)], lane) for b, pp in SB_CHAINS]

            def key_block(j, carry, mask, first):
                zt = [_dot(k_ref[_sb_rows(b, j), _slab(pp)], q2[c], NT) for c, (b, pp) in enumerate(SB_CHAINS)]
                sp = [_sb_softplus(zt[c], mask) for c in chains]
                sums = [_key_sums(tri, sp[c]) for c in chains]
                w = [_sb_weights(zt[c], sp[c], sums[c][0] + carry[c], mask) for c in chains]
                for c in chains:
                    pv = _dot(vt_ref[c, j], w[c].astype(BF16))
                    if first:
                        ot_ref[c] = pv
                    else:
                        ot_ref[c] += pv
                return tuple(carry[c] + sums[c][1] for c in chains)

            def earlier(jj, state):
                carry, saved = state
                j = i - 1 - jj
                saved = tuple(jnp.where(jrow == j, carry[c], saved[c]) for c in chains)
                return key_block(j, carry, None, False), saved

            zero = tuple(jnp.zeros((1, 2 * BLK), F32) for _ in chains)
            carry = key_block(i, zero, valid, True)
            _, saved = lax.fori_loop(0, i, earlier, (carry, tuple(jnp.zeros((N_BLK, 2 * BLK), F32) for _ in chains)))
            for c, (b, pp) in enumerate(SB_CHAINS):
                o_ref[_sb_rows(b, i), _slab(pp)] = _unstack_heads(ot_ref[c].T, lane).astype(BF16)
                c_ref[c * N_BLK + i] = saved[c]
            return 0

        lax.fori_loop(0, N_BLK, q_block, 0)

    n_ch = len(SB_CHAINS)
    vmem = pl.BlockSpec(memory_space=pltpu.VMEM)
    return pl.pallas_call(
        body, name="sb_fwd",
        in_specs=[vmem] * 3, out_specs=[vmem] * 2,
        out_shape=[jax.ShapeDtypeStruct((T_LOC, 512), BF16), jax.ShapeDtypeStruct((n_ch * N_BLK, N_BLK, 2 * BLK), F32)],
        scratch_shapes=[pltpu.VMEM((n_ch, N_BLK, SLAB, BLK), BF16), pltpu.VMEM((n_ch, SLAB, 2 * BLK), F32)],
        compiler_params=pltpu.CompilerParams(vmem_limit_bytes=VMEM_LIMIT),
    )(qb, kb, vb)


def out_loss(o_a, o_b, ga, gb, x, target, wout):
    n_tiles = T_LOC // TM_FWD

    def body(oa_ref, ob_ref, ga_ref, gb_ref, x_ref, t_ref, w_ref,
             dout_ref, doa_ref, dob_ref, dga_ref, dgb_ref, dw_ref, loss_ref, acc_ref):
        step = pl.program_id(0)

        @pl.when(step == 0)
        def _():
            acc_ref[...] = jnp.zeros_like(acc_ref)
            loss_ref[...] = jnp.zeros_like(loss_ref)

        oa, ob, gav, gbv = (r[...].astype(F32) for r in (oa_ref, ob_ref, ga_ref, gb_ref))
        sa, sb = _sigmoid(gav), _sigmoid(gbv)
        silu_a, silu_b = gav * sa, gbv * sb
        y = jnp.concatenate([oa * silu_a, ob * silu_b], axis=1).astype(BF16)
        err = x_ref[...] + _dot(y, w_ref[...]) - t_ref[...]
        e2 = err * err
        part = jnp.sum(e2.reshape(TM_FWD // 8, 8, D_MODEL), axis=0)
        loss_ref[...] += functools.reduce(lambda a, b: a + b, [part[:, k * 128:(k + 1) * 128] for k in range(8)])
        dout = err * (1.0 / D_MODEL)
        dout_ref[...] = dout
        dob16 = dout.astype(BF16)
        for r0 in range(0, D_MODEL, ACC_ROWS):
            acc_ref[r0:r0 + ACC_ROWS, :] += _dot(y[:, r0:r0 + ACC_ROWS], dob16, TN)
        dy = _dot(dob16, w_ref[...], NT)
        dya, dyb = dy[:, :512], dy[:, 512:]
        doa_ref[...] = (dya * silu_a).astype(BF16)
        dob_ref[...] = (dyb * silu_b).astype(BF16)
        dga_ref[...] = (dya * oa * (sa * (1.0 + gav * (1.0 - sa)))).astype(BF16)
        dgb_ref[...] = (dyb * ob * (sb * (1.0 + gbv * (1.0 - sb)))).astype(BF16)

        @pl.when(step == n_tiles - 1)
        def _():
            dw_ref[...] = acc_ref[...].astype(BF16)

    def tile(w):
        return pl.BlockSpec((TM_FWD, w), lambda i: (i, 0))

    const = lambda i: (0, 0)
    return pl.pallas_call(
        body, name="out_loss", grid=(n_tiles,),
        in_specs=[tile(512)] * 4 + [tile(D_MODEL)] * 2 + [pl.BlockSpec((D_MODEL, D_MODEL), const)],
        out_specs=[tile(D_MODEL), tile(512), tile(512), tile(512), tile(512),
                   pl.BlockSpec((D_MODEL, D_MODEL), const), pl.BlockSpec((8, 128), const)],
        out_shape=[jax.ShapeDtypeStruct((T_LOC, D_MODEL), F32)] + [jax.ShapeDtypeStruct((T_LOC, 512), BF16)] * 4
        + [jax.ShapeDtypeStruct((D_MODEL, D_MODEL), BF16), jax.ShapeDtypeStruct((8, 128), F32)],
        scratch_shapes=[pltpu.VMEM((D_MODEL, D_MODEL), F32)],
        compiler_params=_params(("arbitrary",)),
    )(o_a, o_b, ga, gb, x, target, wout)


def swa_bwd(q_rot, k_dup, v_dup, o_a, d_oa, sinks):
    def body(q_ref, kp_ref, kc_ref, vp_ref, vc_ref, o_ref, do_ref, sinks_ref, dq_ref, dk_ref, dv_ref, dsink_ref):
        i = pl.program_id(0)

        @pl.when(i == 0)
        def _():
            dk_ref[...] = jnp.zeros_like(dk_ref)
            dv_ref[...] = jnp.zeros_like(dv_ref)
            dsink_ref[...] = jnp.zeros_like(dsink_ref)

        lane = _lane((BLK, SLAB))
        rows_prev, rows_cur = _blk(jnp.maximum(i - 1, 0)), _blk(i)
        chains = range(len(SWA_CHAINS))
        q2 = [_stack_heads(q_ref[b, :, _slab(p)], lane) for b, p in SWA_CHAINS]
        do2 = [_stack_heads(do_ref[b, :, _slab(p)], lane) for b, p in SWA_CHAINS]
        keys = [_swa_window(kp_ref.at[b], kc_ref.at[b], p) for b, p in SWA_CHAINS]
        s = [_dot(q2[c], keys[c], NT) for c in chains]
        dp = [_dot(do2[c], _swa_window(vp_ref.at[b], vc_ref.at[b], p), NT) for c, (b, p) in enumerate(SWA_CHAINS)]
        ds, pn16, cols = [], [], []
        for c, (b, p) in enumerate(SWA_CHAINS):
            pn, p_sink = _swa_probs(s[c], sinks_ref, p, i)
            o = o_ref[b, :, _slab(p)].astype(F32)
            delta =jnp.sum(do2[c].astype(F32) * jnp.concatenate([o, o], axis=0), axis=-1, keepdims=True)
            ds.append((pn * (dp[c] - delta)).astype(BF16))
            pn16.append(pn.astype(BF16))
            cols.append(-p_sink * delta)
        for c, (b, p) in enumerate(SWA_CHAINS):
            dq_ref[b, :, _slab(p)] = _unstack_heads(_dot(ds[c], keys[c]), lane) * Q_SCALE
        dk2 = [_dot(ds[c], q2[c], TN) for c in chains]
        dv2 = [_dot(pn16[c], do2[c], TN) for c in chains]
        for c, (b, p) in enumerate(SWA_CHAINS):
            gsl = _slab(p // 2)
            dk_ref[b, rows_prev, gsl] += dk2[c][:BLK]
            dk_ref[b, rows_cur, gsl] += dk2[c][BLK:]
            dv_ref[b, rows_prev, gsl] += dv2[c][:BLK]
            dv_ref[b, rows_cur, gsl] += dv2[c][BLK:]
            for e in range(2):
                dsink_ref[2 * p + e:2 * p + e + 1, :] += jnp.sum(cols[c][e * BLK:(e + 1) * BLK], axis=0, keepdims=True)

    cur, prev = _swa_specs()
    whole = pl.BlockSpec((B_LOC, SEQ, 256), lambda i: (0, 0, 0))
    q3, k3, v3, o3, do3 = (a.reshape(B_LOC, SEQ, a.shape[1]) for a in (q_rot, k_dup, v_dup, o_a, d_oa))
    dq, dk, dv, dsink = pl.pallas_call(
        body, name="swa_bwd", grid=(N_BLK,),
        in_specs=[cur(512), prev(256), cur(256), prev(256), cur(256), cur(512), cur(512),
                  pl.BlockSpec(memory_space=pltpu.SMEM)],
        out_specs=[cur(512), whole, whole, pl.BlockSpec((8, 128), lambda i: (0, 0))],
        out_shape=[jax.ShapeDtypeStruct((B_LOC, SEQ, 512), F32), jax.ShapeDtypeStruct((B_LOC, SEQ, 256), F32),
                   jax.ShapeDtypeStruct((B_LOC, SEQ, 256), F32), jax.ShapeDtypeStruct((8, 128), F32)],
        compiler_params=_params(("arbitrary",)),
    )(q3, k3, k3, v3, v3, o3, do3, sinks)
    return dq.reshape(T_LOC, 512), dk.reshape(T_LOC, 256), dv.reshape(T_LOC, 256), dsink


def sb_bwd(qb, kb, vb, d_ob, carries, dwout):
    def body(q_ref, k_ref, v_ref, do_ref, c_ref, dw_hbm, dq_ref, dk_ref, dv_ref, rw_hbm, kt_ref, dqt_ref,
             rw_send, rw_recv, rw_local):
        start_dwout, finish_dwout = _direct_exchange(
            lambda dev: dw_hbm.at[pl.ds(_lin(dev) * OUT_SHARD, OUT_SHARD), :], lambda dev: rw_hbm.at[_lin(dev)],
            rw_send, rw_recv, rw_local)
        start_dwout()
        for c, (b, pp) in enumerate(SB_CHAINS):
            for j in range(N_BLK):
                kt_ref[c, j] = k_ref[b * SEQ + j * BLK:b * SEQ + (j + 1) * BLK, _slab(pp)].T
        dk_ref[...] = jnp.zeros_like(dk_ref)
        dv_ref[...] = jnp.zeros_like(dv_ref)
        dqt_ref[...] = jnp.zeros_like(dqt_ref)
        lane = _lane((BLK, SLAB))
        tri_after, tri_before = _tri(True), _tri(False)
        valid = _causal_t()
        jrow = _row((N_BLK, 2 * BLK))
        chains = range(len(SB_CHAINS))

        def q_block(i, _):
            q2 = [_stack_heads(q_ref[_sb_rows(b, i), _slab(pp)], lane) for b, pp in SB_CHAINS]
            do2 = [_stack_heads(do_ref[_sb_rows(b, i), _slab(pp)], lane) for b, pp in SB_CHAINS]

            def key_block(j, carry_sp, before_u, mask):
                at = [(_sb_rows(b, j), _slab(pp)) for b, pp in SB_CHAINS]
                zt = [_dot(k_ref[at[c]], q2[c], NT) for c in chains]
                dw = [_dot(v_ref[at[c]], do2[c], NT) for c in chains]
                sp = [_sb_softplus(zt[c], mask) for c in chains]
                later = [_key_sums(tri_after, sp[c])[0] for c in chains]
                w = [_sb_weights(zt[c], sp[c], later[c] + carry_sp[c], mask) for c in chains]
                u = [dw[c] * w[c] for c in chains]
                for c in chains:
                    dv_ref[at[c]] += _dot(w[c].astype(BF16), do2[c])
                sums = [_key_sums(tri_before, u[c]) for c in chains]
                dz16 = []
                for c in chains:
                    sig = jnp.exp(zt[c] - sp[c])
                    dz = u[c] - sig * (u[c] + before_u[c] + sums[c][0])
                    if mask is not None:
                        dz = jnp.where(mask, dz, 0.0)
                    dz16.append(dz.astype(BF16))
                for c in chains:
                    dk_ref[at[c]] += _dot(dz16[c], q2[c])
                    dqt_ref[c] += _dot(kt_ref[c, j], dz16[c])
                return tuple(before_u[c] + sums[c][1] for c in chains)

            def earlier(j, before_u):
                carry_sp = [jnp.sum(jnp.where(jrow == j, c_ref[c * N_BLK + i], 0.0), axis=0, keepdims=True)
                            for c in chains]
                return key_block(j, carry_sp, before_u, None)

            zero = tuple(jnp.zeros((1, 2 * BLK), F32) for _ in chains)
            before_u = lax.fori_loop(0, i, earlier, zero)
            key_block(i, zero, before_u, valid)
            for c, (b, pp) in enumerate(SB_CHAINS):
                dq_ref[_sb_rows(b, i), _slab(pp)] = (_unstack_heads(dqt_ref[c].T, lane) * Q_SCALE).astype(BF16)
                dqt_ref[c] = jnp.zeros((SLAB, 2 * BLK), F32)
            return 0

        lax.fori_loop(0, N_BLK, q_block, 0)
        finish_dwout()

    n_ch = len(SB_CHAINS)
    vmem, hbm = pl.BlockSpec(memory_space=pltpu.VMEM), pl.BlockSpec(memory_space=pl.ANY)
    return pl.pallas_call(
        body, name="sb_bwd",
        in_specs=[vmem] * 5 + [hbm], out_specs=[vmem] * 3 + [hbm],
        out_shape=[jax.ShapeDtypeStruct((T_LOC, 512), BF16)] + [jax.ShapeDtypeStruct((T_LOC, 512), F32)] * 2
        + [jax.ShapeDtypeStruct((N_DEV, OUT_SHARD, D_MODEL), BF16)],
        scratch_shapes=[pltpu.VMEM((n_ch, N_BLK, SLAB, BLK), BF16), pltpu.VMEM((n_ch, SLAB, 2 * BLK), F32),
                        pltpu.SemaphoreType.DMA((7,)), pltpu.SemaphoreType.DMA((7,)), pltpu.SemaphoreType.DMA],
        compiler_params=pltpu.CompilerParams(vmem_limit_bytes=VMEM_LIMIT),
    )(qb, kb, vb, d_ob, carries, dwout)


def bwd_dw(x, norm_gain, dq_rot, dk_dup, dv_dup, qa_raw, ka_raw, cos, sin_s, q_gain2, k_gain2, dga, dgb, dqb, dkb, dvb):
    n_tiles = T_LOC // TM_DW

    def body(x_ref, ng_ref, dq_ref, dk_ref, dv_ref, qa_ref, ka_ref, cos_ref, sin_ref, qg_ref, kg_ref,
             dga_ref, dgb_ref, dqb_ref, dkb_ref, dvb_ref,
             dproj_ref, dw_hbm, dqg_ref, dkg_ref, acc_ref, stage_ref):
        step = pl.program_id(0)

        @pl.when(step == 0)
        def _():
            acc_ref[...] = jnp.zeros_like(acc_ref)
            dqg_ref[...] = jnp.zeros_like(dqg_ref)
            dkg_ref[...] = jnp.zeros_like(dkg_ref)

        lane = _lane((TM_DW, SLAB))
        bd = _head_blockdiag()
        cos, sin_s = cos_ref[...], sin_ref[...]

        def norm_rope_bwd(d_rot, raw, gain2):
            dy = d_rot * cos + _swap_half(d_rot * sin_s, lane)
            r = lax.rsqrt(_head_sum(raw * raw, bd) * (1.0 / HEAD_DIM) + EPS)
            xhat = raw * r
            dgain = jnp.sum(dy * xhat, axis=0, keepdims=True)
            dxh = dy * gain2
            mean = _head_sum(dxh * xhat, bd) * (1.0 / HEAD_DIM)
            return r * (dxh - xhat * mean), dgain

        def fold_dup(d_dup):
            a, b2 = d_dup[:, :SLAB], d_dup[:, SLAB:]
            return jnp.where(lane < HEAD_DIM, a + pltpu.roll(a, HEAD_DIM, 1), b2 + pltpu.roll(b2, HEAD_DIM, 1))

        pieces = []
        dqg = jnp.zeros((1, SLAB), F32)
        for p in range(4):
            sl = slice(p * SLAB, (p + 1) * SLAB)
            d_raw, dg = norm_rope_bwd(dq_ref[:, sl], qa_ref[:, sl], qg_ref[...])
            pieces.append(d_raw.astype(BF16))
            dqg = dqg + dg
        d_raw, dkg = norm_rope_bwd(fold_dup(dk_ref[...]), ka_ref[...], kg_ref[...])
        pieces.append(d_raw.astype(BF16))
        pieces.append(fold_dup(dv_ref[...]).astype(BF16))
        pieces += [dga_ref[...], dqb_ref[...], dkb_ref[...].astype(BF16), dvb_ref[...].astype(BF16),
                   dgb_ref[...]]
        dproj = jnp.concatenate(pieces, axis=1)
        dproj_ref[...] = dproj
        dqg_ref[0:1, :] += dqg + pltpu.roll(dqg, HEAD_DIM, 1)
        dkg_ref[0:1, :] += dkg + pltpu.roll(dkg, HEAD_DIM, 1)

        xv = x_ref[...]
        rstd = lax.rsqrt(jnp.mean(xv * xv, axis=-1, keepdims=True) + EPS)
        h = (xv * rstd * ng_ref[...]).astype(BF16)
        for r0 in range(0, IN_WIDTH, ACC_ROWS):
            acc_ref[r0:r0 + ACC_ROWS, :] += _dot(dproj[:, r0:r0 + ACC_ROWS], h, TN)

        @pl.when(step == n_tiles - 1)
        def _():
            for r0 in range(0, IN_WIDTH, ACC_ROWS):
                stage_ref[...] = acc_ref[r0:r0 + ACC_ROWS, :].astype(BF16)
                pltpu.sync_copy(stage_ref, dw_hbm.at[r0:r0 + ACC_ROWS, :])

    def tile(w):
        return pl.BlockSpec((TM_DW, w), lambda i: (i, 0))

    def whole(a):
        return pl.BlockSpec(a.shape, lambda i: (0, 0))

    const = lambda i: (0, 0)
    return pl.pallas_call(
        body, name="bwd_dw", grid=(n_tiles,),
        in_specs=[tile(D_MODEL), whole(norm_gain),
                  tile(512), tile(256), tile(256), tile(512), tile(128), tile(128), tile(128),
                  whole(q_gain2), whole(k_gain2), tile(512), tile(512), tile(512), tile(512), tile(512)],
        out_specs=[tile(IN_WIDTH), pl.BlockSpec(memory_space=pl.ANY),
                   pl.BlockSpec((8, SLAB), const), pl.BlockSpec((8, SLAB), const)],
        out_shape=[jax.ShapeDtypeStruct((T_LOC, IN_WIDTH), BF16), jax.ShapeDtypeStruct((IN_WIDTH, D_MODEL), BF16),
                   jax.ShapeDtypeStruct((8, SLAB), F32), jax.ShapeDtypeStruct((8, SLAB), F32)],
        scratch_shapes=[pltpu.VMEM((IN_WIDTH, D_MODEL), F32), pltpu.VMEM((ACC_ROWS, D_MODEL), BF16)],
        compiler_params=_params(("arbitrary",)),
    )(x, norm_gain, dq_rot, dk_dup, dv_dup, qa_raw, ka_raw, cos, sin_s, q_gain2, k_gain2, dga, dgb, dqb, dkb, dvb)


def bwd_dx(x, dout, norm_gain, win_t, dproj, dwin_t, dqg, dkg, dsink, loss_part):
    n_tiles = T_LOC // TM
    rows_per = IN_SHARD
    step_sums, step_merge = 3, 7

    def body(x_ref, dout_ref, ng_ref, w_hbm, dp_ref, a_hbm, dqg_ref, dkg_ref, dsink_ref, loss_ref,
             gx_ref, ra_hbm, rs_hbm, w_ref, dng_ref, s_ref, own_ref, sib_ref, snd_ref, extra_ref,
             w_sem, d2d_send, d2d_recv, ici_send, ici_recv, own_sems, s_send, s_recv, out_sem):
        step = pl.program_id(0)
        x, y, c = _mesh_pos()
        me, sibling = (x, y, c), (x, y, 1 - c)
        chips = {"own": (x, y), "x": (1 - x, y), "y": (x, 1 - y), "d": (1 - x, 1 - y)}
        index = {"own": 0, "x": 1, "y": 2, "d": 3}
        order = ("d", "x", "y", "own")

        def rows(pos):
            return a_hbm.at[pl.ds(_lin(pos) * rows_per, rows_per), :]

        def to_sibling(k):
            return pltpu.make_async_remote_copy(
                src_ref=rows((*chips[k], 1 - c)), dst_ref=sib_ref.at[index[k]],
                send_sem=d2d_send.at[index[k]], recv_sem=d2d_recv.at[index[k]], device_id=sibling, device_id_type=MESH)

        def mine(k):
            return pltpu.make_async_copy(rows((*chips[k], c)), own_ref.at[index[k]], own_sems.at[index[k]])

        def ici(n, to_chip, dst):
            return pltpu.make_async_remote_copy(
                src_ref=snd_ref.at[n], dst_ref=dst, send_sem=ici_send.at[n], recv_sem=ici_recv.at[n],
                device_id=(*chips[to_chip], c), device_id_type=MESH)

        def chip_sum(k):
            to_sibling(k).wait_recv()
            mine(k).wait()
            return own_ref[index[k]].astype(F32) + sib_ref[index[k]].astype(F32)

        def by_core(fn):
            pl.when(c == 0)(lambda: fn("x", "y"))
            pl.when(c == 1)(lambda: fn("y", "x"))

        @pl.when(step == 0)
        def _():
            cp = pltpu.make_async_copy(w_hbm, w_ref, w_sem)
            cp.start()
            for k in order:
                to_sibling(k).start()
                mine(k).start()
            dng_ref[...] = jnp.zeros_like(dng_ref)
            cp.wait()

        @pl.when(step == step_sums)
        def _():
            def first_sends(direct, via):
                snd_ref[0] = chip_sum("d").astype(BF16)
                ici(0, direct, extra_ref).start()
                snd_ref[1] = chip_sum(direct).astype(BF16)
                ici(1, direct, ra_hbm.at[index[direct]]).start()
            by_core(first_sends)

        @pl.when(step == step_merge)
        def _():
            def merge(direct, via):
                merged = chip_sum(via)
                ici(0, direct, extra_ref).wait_recv()
                snd_ref[2] = (merged + extra_ref[...].astype(F32)).astype(BF16)
                ici(2, via, ra_hbm.at[index[via]]).start()
                own_ref[0] = chip_sum("own").astype(BF16)
                pltpu.make_async_copy(own_ref.at[0], ra_hbm.at[0], out_sem).start()
            by_core(merge)

        xv = x_ref[...]
        rstd = lax.rsqrt(jnp.mean(xv * xv, axis=-1, keepdims=True) + EPS)
        xhat = xv * rstd
        gain = ng_ref[...]
        dh = _dot(dp_ref[...], w_ref[...])
        dng_ref[0:1, :] += jnp.sum(dh * xhat, axis=0, keepdims=True)
        dxh = dh * gain
        gx_ref[...] = dout_ref[...] + rstd * (dxh - xhat * jnp.mean(dxh * xhat, axis=-1, keepdims=True))

        @pl.when(step == n_tiles - 1)
        def _():
            s_ref[...] = jnp.concatenate(
                [dng_ref[...], dqg_ref[...], dkg_ref[...], dsink_ref[...], loss_ref[...]], axis=1)
            start_small, finish_small = _direct_exchange(
                lambda dev: s_ref, lambda dev: rs_hbm.at[_lin(dev)], s_send, s_recv, out_sem)

            def finish(direct, via):
                ici(1, direct, ra_hbm.at[index[direct]]).wait_recv()
                ici(2, via, ra_hbm.at[index[via]]).wait_recv()
                for n, to in ((0, direct), (1, direct), (2, via)):
                    ici(n, to, extra_ref).wait_send()
            by_core(finish)
            pltpu.make_async_copy(own_ref.at[0], ra_hbm.at[0], out_sem).wait()
            for k in order:
                to_sibling(k).wait_send()
            start_small()
            finish_small()

    def tile(w):
        return pl.BlockSpec((TM, w), lambda i: (i, 0))

    def whole(a):
        return pl.BlockSpec(a.shape, lambda i: (0, 0))

    hbm = pl.BlockSpec(memory_space=pl.ANY)
    block = (rows_per, D_MODEL)
    return pl.pallas_call(
        body, name="bwd_dx", grid=(n_tiles,),
        in_specs=[tile(D_MODEL), tile(D_MODEL), whole(norm_gain), hbm, tile(IN_WIDTH), hbm,
                  whole(dqg), whole(dkg), whole(dsink), whole(loss_part)],
        out_specs=[tile(D_MODEL), hbm, hbm],
        out_shape=[jax.ShapeDtypeStruct((T_LOC, D_MODEL), F32), jax.ShapeDtypeStruct((3,) + block, BF16),
                   jax.ShapeDtypeStruct((N_DEV, 8, SMALL_W), F32)],
        scratch_shapes=[pltpu.VMEM((IN_WIDTH, D_MODEL), BF16), pltpu.VMEM((8, D_MODEL), F32),
                        pltpu.VMEM((8, SMALL_W), F32),
                        pltpu.VMEM((4,) + block, BF16), pltpu.VMEM((4,) + block, BF16), pltpu.VMEM((3,) + block, BF16),
                        pltpu.VMEM(block, BF16),
                        pltpu.SemaphoreType.DMA, pltpu.SemaphoreType.DMA((4,)), pltpu.SemaphoreType.DMA((4,)),
                        pltpu.SemaphoreType.DMA((3,)), pltpu.SemaphoreType.DMA((3,)), pltpu.SemaphoreType.DMA((4,)),
                        pltpu.SemaphoreType.DMA((7,)), pltpu.SemaphoreType.DMA((7,)), pltpu.SemaphoreType.DMA],
        compiler_params=_params(("arbitrary",)),
    )(x, dout, norm_gain, win_t, dproj, dwin_t, dqg, dkg, dsink, loss_part)


def _adamw(w, g, m, v):
    m = ADAM_B1 * m + (1.0 - ADAM_B1) * g
    v = ADAM_B2 * v + (1.0 - ADAM_B2) * (g * g)
    m_hat = m / (1.0 - ADAM_B1 ** ADAM_STEP)
    v_hat = v / (1.0 - ADAM_B2 ** ADAM_STEP)
    delta = -ADAM_LR * (m_hat / (jnp.sqrt(v_hat) + ADAM_EPS) + ADAM_WD * w)
    return delta, m, v


def _sum_slots(r_ref):
    g = r_ref[0].astype(F32)
    for s in range(1, r_ref.shape[0]):
        g = g + r_ref[s].astype(F32)
    return g


def adamw_all(r_win, r_out, r_small, big_in, big_out, weights, moments_m, moments_v):
    n = len(weights)
    params = [big_in[0], big_out[0], *weights]

    def body(rw_ref, ro_ref, rs_ref, *refs):
        n_p = n + 2
        ins, outs = refs[:3 * n_p], refs[3 * n_p:]
        s = _sum_slots(rs_ref)
        eye = (_row((8, SLAB)) == _lane((8, SLAB))).astype(F32)
        sinks = jnp.sum(s[:, 1280:1408] * eye, axis=0, keepdims=True)
        grads = [_sum_slots(rw_ref), _sum_slots(ro_ref),
                 s[0:1, :D_MODEL], s[0:1, 1024:1024 + HEAD_DIM], s[0:1, 1152:1152 + HEAD_DIM], sinks[:, :8]]
        for k in range(n_p):
            outs[k][...] = grads[k]
            outs[n_p + k][...], outs[2 * n_p + k][...], outs[3 * n_p + k][...] = _adamw(
                ins[k][...], grads[k], ins[n_p + k][...], ins[2 * n_p + k][...])
        loss = jnp.sum(jnp.sum(s[:, 1408:1536], axis=1, keepdims=True), axis=0, keepdims=True) * (0.5 / D_MODEL)
        outs[4 * n_p][...] = loss

    n_p = n + 2
    res = pl.pallas_call(
        body, name="adamw_all",
        out_shape=[jax.ShapeDtypeStruct(p.shape, F32) for p in params] * 4 + [jax.ShapeDtypeStruct((1, 1), F32)],
        compiler_params=pltpu.CompilerParams(vmem_limit_bytes=VMEM_LIMIT),
    )(r_win, r_out, r_small, big_in[0], big_out[0], *weights, big_in[1], big_out[1], *moments_m,
      big_in[2], big_out[2], *moments_v)
    return [res[k * n_p:(k + 1) * n_p] for k in range(4)], res[4 * n_p]


def kernel(x, positions, norm_gain, w_in, q_norm_gain, k_norm_gain, sinks, w_out, loss_target, m_norm_gain, m_w_in, m_q_norm_gain, m_k_norm_gain, m_sinks, m_w_out, v_norm_gain, v_w_in, v_q_norm_gain, v_k_norm_gain, v_sinks, v_w_out):
    x2 = x.reshape(T_LOC, D_MODEL)
    tgt2 = loss_target.reshape(T_LOC, D_MODEL)
    pos2 = positions.reshape(1, T_LOC)
    half = HEAD_DIM // 2
    inv_freq = ROPE_THETA ** (-jnp.arange(half, dtype=F32) * 2.0 / HEAD_DIM)
    inv_freq = jnp.tile(inv_freq, SLAB // half).reshape(SLAB, 1)
    sin_sign = jnp.tile(jnp.concatenate([-jnp.ones((half,), F32), jnp.ones((half,), F32)]), 2).reshape(1, SLAB)
    q_gain2 = jnp.tile(q_norm_gain, (1, 2))
    k_gain2 = jnp.tile(k_norm_gain, (1, 2))

    win_t = gather_weights(w_in.reshape(D_MODEL, IN_SHARD).T)

    (qa_raw, ka_raw, q_rot, k_dup, v_dup, ga, qb, kb, vb, gb, cos, sin_s, wout) = fwd_proj(
        x2, pos2, norm_gain, win_t, inv_freq, sin_sign, q_gain2, k_gain2, w_out.reshape(OUT_SHARD, D_MODEL).astype(BF16))
    o_a = swa_fwd(q_rot, k_dup, v_dup, sinks)
    o_b, carries = sb_fwd(qb, kb, vb)
    dout, d_oa, d_ob, dga, dgb, dwout, loss_part = out_loss(o_a, o_b, ga, gb, x2, tgt2, wout)
    dq_rot, dk_dup, dv_dup, dsink = swa_bwd(q_rot, k_dup, v_dup, o_a, d_oa, sinks)
    dqb, dkb, dvb, r_out = sb_bwd(qb, kb, vb, d_ob, carries, dwout)
    dproj, dwin_t, dqg, dkg = bwd_dw(
        x2, norm_gain, dq_rot, dk_dup, dv_dup, qa_raw, ka_raw, cos, sin_s, q_gain2, k_gain2, dga, dgb, dqb, dkb, dvb)
    grad_x, r_win, r_small = bwd_dx(x2, dout, norm_gain, win_t, dproj, dwin_t, dqg, dkg, dsink, loss_part)

    w_in2, m_in2, v_in2 = (a.reshape(D_MODEL, IN_SHARD).T for a in (w_in, m_w_in, v_w_in))
    w_out2, m_out2, v_out2 = (a.reshape(OUT_SHARD, D_MODEL) for a in (w_out, m_w_out, v_w_out))
    kinds, loss = adamw_all(
        r_win, r_out, r_small, (w_in2, m_in2, v_in2), (w_out2, m_out2, v_out2),
        (norm_gain, q_norm_gain, k_norm_gain, sinks),
        (m_norm_gain, m_q_norm_gain, m_k_norm_gain, m_sinks), (v_norm_gain, v_q_norm_gain, v_k_norm_gain, v_sinks))

    def leaves(k):
        big_in, big_out, ng, qg, kg, sk = kinds[k]
        return (ng, big_in.T.reshape(1, D_MODEL, IN_SHARD), qg, kg, sk, big_out.reshape(1, OUT_SHARD, D_MODEL))

    return (loss.reshape(()), grad_x.reshape(B_LOC, SEQ, D_MODEL), *leaves(0), *leaves(1), *leaves(2), *leaves(3))
```

```python
import functools

import jax
import jax.numpy as jnp
from jax import lax
from jax.experimental import pallas as pl
from jax.experimental.pallas import tpu as pltpu

F32 = jnp.float32
BF16 = jnp.bfloat16

N_DEV = 8
D_MODEL = 1024
SEQ = 2048
B_LOC = 2
T_LOC = B_LOC * SEQ
HEAD_DIM = 64
HEAD_SHIFT = 6
BLK = 128
N_BLK = SEQ // BLK
SLAB = 128
IN_WIDTH = 3328
IN_SHARD = IN_WIDTH // N_DEV
OUT_SHARD = D_MODEL // N_DEV
EPS = 1e-6
ROPE_THETA = 10000.0
Q_SCALE = 0.125
R_QA, R_KA, R_VA, R_GA, R_QB, R_KB, R_VB, R_GB, R_END = 0, 512, 640, 768, 1280, 1792, 2304, 2816, 3328
SMALL_W = 1536
ADAM_LR, ADAM_B1, ADAM_B2, ADAM_EPS, ADAM_WD, ADAM_STEP = 0.001, 0.9, 0.999, 1e-08, 0.01, 10
TM = 256
TM_DW = 512
TM_FWD = 512
ACC_ROWS = 256
GATHER_CHUNKS = 2
VMEM_LIMIT = 56 * 1024 * 1024

MESH = pl.DeviceIdType.MESH
NT = (((1,), (1,)), ((), ()))
TN = (((0,), (0,)), ((), ()))


def _params(sem, limit=VMEM_LIMIT):
    return pltpu.CompilerParams(dimension_semantics=sem, vmem_limit_bytes=limit)


def _dot(a, b, dims=None):
    if dims is None:
        return jnp.dot(a, b, preferred_element_type=F32)
    return lax.dot_general(a, b, dims, preferred_element_type=F32)


def _lane(shape):
    return lax.broadcasted_iota(jnp.int32, shape, len(shape) - 1)


def _row(shape):
    return lax.broadcasted_iota(jnp.int32, shape, 0)


def _head_blockdiag():
    return ((_row((SLAB, SLAB)) >> HEAD_SHIFT) == (_lane((SLAB, SLAB)) >> HEAD_SHIFT)).astype(BF16)


def _head_sum(x, bd):
    return _dot(x.astype(BF16), bd)


def _swap_half(y, lane):
    return jnp.where((lane & 32) != 0, pltpu.roll(y, 32, 1), pltpu.roll(y, 96, 1))


def _stack_heads(q, lane):
    zero = jnp.zeros_like(q)
    return jnp.concatenate([jnp.where(lane < HEAD_DIM, q, zero), jnp.where(lane >= HEAD_DIM, q, zero)], axis=0)


def _unstack_heads(x2, lane):
    return jnp.where(lane < HEAD_DIM, x2[:BLK], x2[BLK:])


def _sigmoid(x):
    return 1.0 / (1.0 + jnp.exp(-x))


def _mesh_pos():
    return lax.axis_index("x"), lax.axis_index("y"), lax.axis_index("c")


def _flip(pos, mask):
    return tuple(1 - p if m else p for p, m in zip(pos, mask))


def _lin(pos):
    return 4 * pos[0] + 2 * pos[1] + pos[2]


DEV_FLIPS = [(fx, fy, fc) for fx in (0, 1) for fy in (0, 1) for fc in (0, 1)][1:]


def _direct_exchange(src_for, dst_slot, send_sems, recv_sems, local_sem):
    me = _mesh_pos()

    def copy(k, to):
        return pltpu.make_async_remote_copy(
            src_ref=src_for(to), dst_ref=dst_slot(me), send_sem=send_sems.at[k], recv_sem=recv_sems.at[k],
            device_id=to, device_id_type=MESH)

    def landed(k, frm):
        return pltpu.make_async_remote_copy(
            src_ref=src_for(frm), dst_ref=dst_slot(frm), send_sem=send_sems.at[k], recv_sem=recv_sems.at[k],
            device_id=frm, device_id_type=MESH)

    local = None if local_sem is None else pltpu.make_async_copy(src_for(me), dst_slot(me), local_sem)
    peers = [_flip(me, f) for f in DEV_FLIPS]

    def start():
        if local is not None:
            local.start()
        for k, to in enumerate(peers):
            copy(k, to).start()

    def finish():
        for k, frm in enumerate(peers):
            landed(k, frm).wait_recv()
        for k, to in enumerate(peers):
            copy(k, to).wait_send()
        if local is not None:
            local.wait()

    return start, finish


def gather_weights(shard):
    m = shard.shape[0]
    piece = m // GATHER_CHUNKS
    pieces = range(GATHER_CHUNKS)

    def body(f32_ref, o_ref, a_ref, ici_send, ici_recv, d2d_send, d2d_recv, local_sem):
        a_ref[...] = f32_ref[...].astype(BF16)
        x, y, c = _mesh_pos()
        me, sibling = (x, y, c), (x, y, 1 - c)
        chip_x, chip_y, chip_d = (1 - x, y), (x, 1 - y), (1 - x, 1 - y)

        def rows(pos, q):
            return o_ref.at[pl.ds(_lin(pos) * m + q * piece, piece), :]

        def own(q):
            return a_ref.at[pl.ds(q * piece, piece), :]

        def ici(k, q, block, to, src=None):
            return pltpu.make_async_remote_copy(
                src_ref=rows(block, q) if src is None else src, dst_ref=rows(block, q),
                send_sem=ici_send.at[k, q], recv_sem=ici_recv.at[k, q], device_id=to, device_id_type=MESH)

        def d2d(k, q, chip, mine, src=None):
            block = (*chip, c) if mine else (*chip, 1 - c)
            return pltpu.make_async_remote_copy(
                src_ref=rows(block, q) if src is None else src, dst_ref=rows(block, q),
                send_sem=d2d_send.at[k, q], recv_sem=d2d_recv.at[k, q], device_id=sibling, device_id_type=MESH)

        local = pltpu.make_async_copy(a_ref, o_ref.at[pl.ds(_lin(me) * m, m), :], local_sem)
        local.start()
        sends = []
        for q in pieces:
            sends += [ici(0, q, me, (*chip_x, c), src=own(q)), ici(1, q, me, (*chip_y, c), src=own(q)),
                      d2d(0, q, (x, y), True, src=own(q))]
        for cp in sends:
            cp.start()

        def pass_on(first, k_first, second, k_second, onward):
            moved = []
            for q in pieces:
                ici(k_first, q, (*first, c), me).wait_recv()
                moved += [ici(2, q, (*first, c), (*onward, c)), d2d(1 + k_first, q, first, True)]
                for cp in moved[-2:]:
                    cp.start()
            for q in pieces:
                ici(k_second, q, (*second, c), me).wait_recv()
                moved.append(d2d(1 + k_second, q, second, True))
                moved[-1].start()
            for q in pieces:
                ici(2, q, (*chip_d, c), me).wait_recv()
                moved.append(d2d(3, q, chip_d, True))
                moved[-1].start()
            for cp in moved:
                cp.wait_send()

        @pl.when(c == 0)
        def _():
            pass_on(chip_y, 1, chip_x, 0, chip_x)

        @pl.when(c == 1)
        def _():
            pass_on(chip_x, 0, chip_y, 1, chip_y)

        for k, chip in enumerate([(x, y), chip_x, chip_y, chip_d]):
            for q in pieces:
                d2d(k, q, chip, False).wait_recv()
        for cp in sends:
            cp.wait_send()
        local.wait()

    vmem = pl.BlockSpec(memory_space=pltpu.VMEM)
    n_q = GATHER_CHUNKS
    return pl.pallas_call(
        body, name="gather_weights",
        out_shape=jax.ShapeDtypeStruct((N_DEV * m, shard.shape[1]), BF16),
        in_specs=[vmem], out_specs=vmem,
        scratch_shapes=[pltpu.VMEM(shard.shape, BF16), pltpu.SemaphoreType.DMA((3, n_q)), pltpu.SemaphoreType.DMA((3, n_q)),
                        pltpu.SemaphoreType.DMA((4, n_q)), pltpu.SemaphoreType.DMA((4, n_q)), pltpu.SemaphoreType.DMA],
        compiler_params=pltpu.CompilerParams(vmem_limit_bytes=VMEM_LIMIT),
    )(shard)


def _norm_rope(xs, gain2, cos, sin_s, bd, lane):
    r = lax.rsqrt(_head_sum(xs * xs, bd) * (1.0 / HEAD_DIM) + EPS)
    y = xs * r * gain2
    return y * cos + _swap_half(y, lane) * sin_s


def _dup_heads(xs, lane):
    r = pltpu.roll(xs, HEAD_DIM, 1)
    lo = lane < HEAD_DIM
    return jnp.concatenate([jnp.where(lo, xs, r), jnp.where(lo, r, xs)], axis=1)


def fwd_proj(x, pos, norm_gain, win_t, inv_freq, sin_sign, q_gain2, k_gain2, wout_shard):
    n_tiles = T_LOC // TM_FWD

    def body(x_ref, pos_ref, ng_ref, w_ref, if_ref, sg_ref, qg_ref, kg_ref, ws_hbm,
             qa_raw_ref, ka_raw_ref, q_rot_ref, k_dup_ref, v_dup_ref, ga_ref, qb_ref, kb_ref, vb_ref, gb_ref,
             cos_ref, sin_ref, wo_hbm, wo_send, wo_recv, wo_local):
        start_wout, finish_wout = _direct_exchange(
            lambda dev: ws_hbm, lambda dev: wo_hbm.at[pl.ds(_lin(dev) * OUT_SHARD, OUT_SHARD), :],
            wo_send, wo_recv, wo_local)
        pl.when(pl.program_id(0) == 0)(start_wout)

        xv = x_ref[...]
        rstd = lax.rsqrt(jnp.mean(xv * xv, axis=-1, keepdims=True) + EPS)
        h = (xv * rstd * ng_ref[...]).astype(BF16)

        def proj(r0, r1):
            return _dot(h, w_ref[r0:r1, :], NT)

        ang_t = if_ref[...] * pos_ref[...].astype(F32)
        cos = jnp.cos(ang_t).T
        sin_s = jnp.sin(ang_t).T * sg_ref[...]
        cos_ref[...] = cos
        sin_ref[...] = sin_s
        lane = _lane((TM_FWD, SLAB))
        bd = _head_blockdiag()

        qa = proj(R_QA, R_KA)
        qa_raw_ref[...] = qa
        for p in range(4):
            sl = slice(p * SLAB, (p + 1) * SLAB)
            q_rot_ref[:, sl] = (_norm_rope(qa[:, sl], qg_ref[...], cos, sin_s, bd, lane) * Q_SCALE).astype(BF16)
        ka = proj(R_KA, R_VA)
        ka_raw_ref[...] = ka
        k_dup_ref[...] = _dup_heads(_norm_rope(ka, kg_ref[...], cos, sin_s, bd, lane), lane).astype(BF16)
        v_dup_ref[...] = _dup_heads(proj(R_VA, R_GA), lane).astype(BF16)
        ga_ref[...] = proj(R_GA, R_QB).astype(BF16)
        qb_ref[...] = (proj(R_QB, R_KB) * Q_SCALE).astype(BF16)
        kb_ref[...] = proj(R_KB, R_VB).astype(BF16)
        vb_ref[...] = proj(R_VB, R_GB).astype(BF16)
        gb_ref[...] = proj(R_GB, R_END).astype(BF16)
        pl.when(pl.program_id(0) == n_tiles - 1)(finish_wout)

    def tile(w):
        return pl.BlockSpec((TM_FWD, w), lambda i: (i, 0))

    def whole(a):
        return pl.BlockSpec(a.shape, lambda i: (0, 0))

    hbm = pl.BlockSpec(memory_space=pl.ANY)
    widths = [(512, F32), (128, F32), (512, BF16), (256, BF16), (256, BF16), (512, BF16), (512, BF16), (512, BF16),
              (512, BF16), (512, BF16), (128, F32), (128, F32)]
    return pl.pallas_call(
        body, name="fwd_proj", grid=(n_tiles,),
        in_specs=[tile(D_MODEL), pl.BlockSpec((1, TM_FWD), lambda i: (0, i)), whole(norm_gain), whole(win_t),
                  whole(inv_freq), whole(sin_sign),
                  whole(q_gain2), whole(k_gain2), hbm],
        out_specs=[tile(w) for w, _ in widths] + [hbm],
        out_shape=[jax.ShapeDtypeStruct((T_LOC, w), dt) for w, dt in widths]
        + [jax.ShapeDtypeStruct((D_MODEL, D_MODEL), BF16)],
        scratch_shapes=[pltpu.SemaphoreType.DMA((7,)), pltpu.SemaphoreType.DMA((7,)), pltpu.SemaphoreType.DMA],
        compiler_params=_params(("arbitrary",)),
    )(x, pos, norm_gain, win_t, inv_freq, sin_sign, q_gain2, k_gain2, wout_shard)


def _swa_window(prev_ref, cur_ref, p):
    gsl = _slab(p // 2)
    return jnp.concatenate([prev_ref[:, gsl], cur_ref[:, gsl]], axis=0)


def _swa_probs(s, sinks_ref, p, i):
    shape = (2 * BLK, 2 * BLK)
    r = _row(shape) & (BLK - 1)
    cidx = _lane(shape)
    valid = (cidx > r) & (cidx <= r + BLK) & ((cidx >= BLK) | (i > 0))
    s = jnp.where(valid, s, -jnp.inf)
    sink = jnp.where(_row((2 * BLK, 1)) < BLK, sinks_ref[0, 2 * p], sinks_ref[0, 2 * p + 1])
    m = jnp.maximum(jnp.max(s, axis=-1, keepdims=True), sink)
    e = jnp.exp(s - m)
    e_sink = jnp.exp(sink - m)
    den = jnp.sum(e, axis=-1, keepdims=True) + e_sink
    return e / den, e_sink / den


SWA_CHAINS = [(b, p) for b in range(B_LOC) for p in range(4)]


def _swa_specs():
    def cur(w):
        return pl.BlockSpec((B_LOC, BLK, w), lambda i: (0, i, 0))

    def prev(w):
        return pl.BlockSpec((B_LOC, BLK, w), lambda i: (0, jnp.maximum(i - 1, 0), 0))

    return cur, prev


def swa_fwd(q_rot, k_dup, v_dup, sinks):
    def body(q_ref, kp_ref, kc_ref, vp_ref, vc_ref, sinks_ref, o_ref):
        i = pl.program_id(0)
        lane = _lane((BLK, SLAB))
        s = [_dot(_stack_heads(q_ref[b, :, _slab(p)], lane), _swa_window(kp_ref.at[b], kc_ref.at[b], p), NT)
             for b, p in SWA_CHAINS]
        pn = [_swa_probs(s[c], sinks_ref, p, i)[0].astype(BF16) for c, (b, p) in enumerate(SWA_CHAINS)]
        for c, (b, p) in enumerate(SWA_CHAINS):
            o = _unstack_heads(_dot(pn[c], _swa_window(vp_ref.at[b], vc_ref.at[b], p)), lane)
            o_ref[b, :, _slab(p)] = o.astype(BF16)

    cur, prev = _swa_specs()
    q3, k3, v3 = (a.reshape(B_LOC, SEQ, a.shape[1]) for a in (q_rot, k_dup, v_dup))
    return pl.pallas_call(
        body, name="swa_fwd", grid=(N_BLK,),
        in_specs=[cur(512), prev(256), cur(256), prev(256), cur(256), pl.BlockSpec(memory_space=pltpu.SMEM)],
        out_specs=cur(512),
        out_shape=jax.ShapeDtypeStruct((B_LOC, SEQ, 512), BF16),
        compiler_params=_params(("arbitrary",)),
    )(q3, k3, k3, v3, v3, sinks).reshape(T_LOC, 512)


def _tri(suffix):
    r, cidx = _row((BLK + 16, BLK)), _lane((BLK + 16, BLK))
    tri = (cidx > r) if suffix else (cidx < r)
    return (tri | (r >= BLK)).astype(BF16)


def _key_sums(tri, x):
    res = _dot(tri, x.astype(BF16))
    return res[:BLK], res[BLK:BLK + 1]


def _sb_softplus(zt, valid):
    neg_abs = lax.bitcast_convert_type(lax.bitcast_convert_type(zt, jnp.uint32) | jnp.uint32(0x80000000), F32)
    sp = jnp.maximum(zt, 0.0) + jnp.log(1.0 + jnp.exp(neg_abs))
    return sp if valid is None else jnp.where(valid, sp, 0.0)


def _sb_weights(zt, sp, later, valid):
    w = jnp.exp(zt - sp - later)
    return w if valid is None else jnp.where(valid, w, 0.0)


def _slab(pp):
    return slice(pp * SLAB, (pp + 1) * SLAB)


def _blk(j):
    return pl.ds(pl.multiple_of(j * BLK, BLK), BLK)


def _causal_t():
    return _row((BLK, 2 * BLK)) < (_lane((BLK, 2 * BLK)) & (BLK - 1))


def _sb_rows(b, j):
    return pl.ds(pl.multiple_of(b * SEQ + j * BLK, BLK), BLK)


SB_CHAINS = [(b, pp) for b in range(B_LOC) for pp in range(4)]


def sb_fwd(qb, kb, vb):
    def body(q_ref, k_ref, v_ref, o_ref, c_ref, vt_ref, ot_ref):
        for c, (b, pp) in enumerate(SB_CHAINS):
            for j in range(N_BLK):
                vt_ref[c, j] = v_ref[b * SEQ + j * BLK:b * SEQ + (j + 1) * BLK, _slab(pp)].T
        lane = _lane((BLK, SLAB))
        tri = _tri(True)
        valid = _causal_t()
        jrow = _row((N_BLK, 2 * BLK))
        chains = range(len(SB_CHAINS))

        def q_block(i, _):
            q2 = [_stack_heads(q_ref[_sb_rows(b, i), _slab(pp)], lane) for b, pp in SB_CHAINS]

            def key_block(j, carry, mask, first):
                zt = [_dot(k_ref[_sb_rows(b, j), _slab(pp)], q2[c], NT) for c, (b, pp) in enumerate(SB_CHAINS)]
                sp = [_sb_softplus(zt[c], mask) for c in chains]
                sums = [_key_sums(tri, sp[c]) for c in chains]
                w = [_sb_weights(zt[c], sp[c], sums[c][0] + carry[c], mask) for c in chains]
                for c in chains:
                    pv = _dot(vt_ref[c, j], w[c].astype(BF16))
                    if first:
                        ot_ref[c] = pv
                    else:
                        ot_ref[c] += pv
                return tuple(carry[c] + sums[c][1] for c in chains)

            def earlier(jj, state):
                carry, saved = state
                j = i - 1 - jj
                saved = tuple(jnp.where(jrow == j, carry[c], saved[c]) for c in chains)
                return key_block(j, carry, None, False), saved

            zero = tuple(jnp.zeros((1, 2 * BLK), F32) for _ in chains)
            carry = key_block(i, zero, valid, True)
            _, saved = lax.fori_loop(0, i, earlier, (carry, tuple(jnp.zeros((N_BLK, 2 * BLK), F32) for _ in chains)))
            for c, (b, pp) in enumerate(SB_CHAINS):
                o_ref[_sb_rows(b, i), _slab(pp)] = _unstack_heads(ot_ref[c].T, lane).astype(BF16)
                c_ref[c * N_BLK + i] = saved[c]
            return 0

        lax.fori_loop(0, N_BLK, q_block, 0)

    n_ch = len(SB_CHAINS)
    vmem = pl.BlockSpec(memory_space=pltpu.VMEM)
    return pl.pallas_call(
        body, name="sb_fwd",
        in_specs=[vmem] * 3, out_specs=[vmem] * 2,
        out_shape=[jax.ShapeDtypeStruct((T_LOC, 512), BF16), jax.ShapeDtypeStruct((n_ch * N_BLK, N_BLK, 2 * BLK), F32)],
        scratch_shapes=[pltpu.VMEM((n_ch, N_BLK, SLAB, BLK), BF16), pltpu.VMEM((n_ch, SLAB, 2 * BLK), F32)],
        compiler_params=pltpu.CompilerParams(vmem_limit_bytes=VMEM_LIMIT),
    )(qb, kb, vb)


def out_loss(o_a, o_b, ga, gb, x, target, wout):
    n_tiles = T_LOC // TM_FWD

    def body(oa_ref, ob_ref, ga_ref, gb_ref, x_ref, t_ref, w_ref,
             dout_ref, doa_ref, dob_ref, dga_ref, dgb_ref, dw_ref, loss_ref, acc_ref):
        step = pl.program_id(0)

        @pl.when(step == 0)
        def _():
            acc_ref[...] = jnp.zeros_like(acc_ref)
            loss_ref[...] = jnp.zeros_like(loss_ref)

        oa, ob, gav, gbv = (r[...].astype(F32) for r in (oa_ref, ob_ref, ga_ref, gb_ref))
        sa, sb = _sigmoid(gav), _sigmoid(gbv)
        silu_a, silu_b = gav * sa, gbv * sb
        y = jnp.concatenate([oa * silu_a, ob * silu_b], axis=1).astype(BF16)
        err = x_ref[...] + _dot(y, w_ref[...]) - t_ref[...]
        e2 = err * err
        part = jnp.sum(e2.reshape(TM_FWD // 8, 8, D_MODEL), axis=0)
        loss_ref[...] += functools.reduce(lambda a, b: a + b, [part[:, k * 128:(k + 1) * 128] for k in range(8)])
        dout = err * (1.0 / D_MODEL)
        dout_ref[...] = dout
        dob16 = dout.astype(BF16)
        for r0 in range(0, D_MODEL, ACC_ROWS):
            acc_ref[r0:r0 + ACC_ROWS, :] += _dot(y[:, r0:r0 + ACC_ROWS], dob16, TN)
        dy = _dot(dob16, w_ref[...], NT)
        dya, dyb = dy[:, :512], dy[:, 512:]
        doa_ref[...] = (dya * silu_a).astype(BF16)
        dob_ref[...] = (dyb * silu_b).astype(BF16)
        dga_ref[...] = (dya * oa * (sa * (1.0 + gav * (1.0 - sa)))).astype(BF16)
        dgb_ref[...] = (dyb * ob * (sb * (1.0 + gbv * (1.0 - sb)))).astype(BF16)

        @pl.when(step == n_tiles - 1)
        def _():
            dw_ref[...] = acc_ref[...].astype(BF16)

    def tile(w):
        return pl.BlockSpec((TM_FWD, w), lambda i: (i, 0))

    const = lambda i: (0, 0)
    return pl.pallas_call(
        body, name="out_loss", grid=(n_tiles,),
        in_specs=[tile(512)] * 4 + [tile(D_MODEL)] * 2 + [pl.BlockSpec((D_MODEL, D_MODEL), const)],
        out_specs=[tile(D_MODEL), tile(512), tile(512), tile(512), tile(512),
                   pl.BlockSpec((D_MODEL, D_MODEL), const), pl.BlockSpec((8, 128), const)],
        out_shape=[jax.ShapeDtypeStruct((T_LOC, D_MODEL), F32)] + [jax.ShapeDtypeStruct((T_LOC, 512), BF16)] * 4
        + [jax.ShapeDtypeStruct((D_MODEL, D_MODEL), BF16), jax.ShapeDtypeStruct((8, 128), F32)],
        scratch_shapes=[pltpu.VMEM((D_MODEL, D_MODEL), F32)],
        compiler_params=_params(("arbitrary",)),
    )(o_a, o_b, ga, gb, x, target, wout)


def swa_bwd(q_rot, k_dup, v_dup, o_a, d_oa, sinks):
    def body(q_ref, kp_ref, kc_ref, vp_ref, vc_ref, o_ref, do_ref, sinks_ref, dq_ref, dk_ref, dv_ref, dsink_ref):
        i = pl.program_id(0)

        @pl.when(i == 0)
        def _():
            dk_ref[...] = jnp.zeros_like(dk_ref)
            dv_ref[...] = jnp.zeros_like(dv_ref)
            dsink_ref[...] = jnp.zeros_like(dsink_ref)

        lane = _lane((BLK, SLAB))
        rows_prev, rows_cur = _blk(jnp.maximum(i - 1, 0)), _blk(i)
        chains = range(len(SWA_CHAINS))
        q2 = [_stack_heads(q_ref[b, :, _slab(p)], lane) for b, p in SWA_CHAINS]
        do2 = [_stack_heads(do_ref[b, :, _slab(p)], lane) for b, p in SWA_CHAINS]
        keys = [_swa_window(kp_ref.at[b], kc_ref.at[b], p) for b, p in SWA_CHAINS]
        s = [_dot(q2[c], keys[c], NT) for c in chains]
        dp = [_dot(do2[c], _swa_window(vp_ref.at[b], vc_ref.at[b], p), NT) for c, (b, p) in enumerate(SWA_CHAINS)]
        ds, pn16, cols = [], [], []
        for c, (b, p) in enumerate(SWA_CHAINS):
            pn, p_sink = _swa_probs(s[c], sinks_ref, p, i)
            o = o_ref[b, :, _slab(p)].astype(F32)
            delta =jnp.sum(do2[c].astype(F32) * jnp.concatenate([o, o], axis=0), axis=-1, keepdims=True)
            ds.append((pn * (dp[c] - delta)).astype(BF16))
            pn16.append(pn.astype(BF16))
            cols.append(-p_sink * delta)
        for c, (b, p) in enumerate(SWA_CHAINS):
            dq_ref[b, :, _slab(p)] = _unstack_heads(_dot(ds[c], keys[c]), lane) * Q_SCALE
        dk2 = [_dot(ds[c], q2[c], TN) for c in chains]
        dv2 = [_dot(pn16[c], do2[c], TN) for c in chains]
        for c, (b, p) in enumerate(SWA_CHAINS):
            gsl = _slab(p // 2)
            dk_ref[b, rows_prev, gsl] += dk2[c][:BLK]
            dk_ref[b, rows_cur, gsl] += dk2[c][BLK:]
            dv_ref[b, rows_prev, gsl] += dv2[c][:BLK]
            dv_ref[b, rows_cur, gsl] += dv2[c][BLK:]
            for e in range(2):
                dsink_ref[2 * p + e:2 * p + e + 1, :] += jnp.sum(cols[c][e * BLK:(e + 1) * BLK], axis=0, keepdims=True)

    cur, prev = _swa_specs()
    whole = pl.BlockSpec((B_LOC, SEQ, 256), lambda i: (0, 0, 0))
    q3, k3, v3, o3, do3 = (a.reshape(B_LOC, SEQ, a.shape[1]) for a in (q_rot, k_dup, v_dup, o_a, d_oa))
    dq, dk, dv, dsink = pl.pallas_call(
        body, name="swa_bwd", grid=(N_BLK,),
        in_specs=[cur(512), prev(256), cur(256), prev(256), cur(256), cur(512), cur(512),
                  pl.BlockSpec(memory_space=pltpu.SMEM)],
        out_specs=[cur(512), whole, whole, pl.BlockSpec((8, 128), lambda i: (0, 0))],
        out_shape=[jax.ShapeDtypeStruct((B_LOC, SEQ, 512), F32), jax.ShapeDtypeStruct((B_LOC, SEQ, 256), F32),
                   jax.ShapeDtypeStruct((B_LOC, SEQ, 256), F32), jax.ShapeDtypeStruct((8, 128), F32)],
        compiler_params=_params(("arbitrary",)),
    )(q3, k3, k3, v3, v3, o3, do3, sinks)
    return dq.reshape(T_LOC, 512), dk.reshape(T_LOC, 256), dv.reshape(T_LOC, 256), dsink


def sb_bwd(qb, kb, vb, d_ob, carries, dwout):
    def body(q_ref, k_ref, v_ref, do_ref, c_ref, dw_hbm, dq_ref, dk_ref, dv_ref, rw_hbm, kt_ref, dqt_ref,
             rw_send, rw_recv, rw_local):
        start_dwout, finish_dwout = _direct_exchange(
            lambda dev: dw_hbm.at[pl.ds(_lin(dev) * OUT_SHARD, OUT_SHARD), :], lambda dev: rw_hbm.at[_lin(dev)],
            rw_send, rw_recv, rw_local)
        start_dwout()
        for c, (b, pp) in enumerate(SB_CHAINS):
            for j in range(N_BLK):
                kt_ref[c, j] = k_ref[b * SEQ + j * BLK:b * SEQ + (j + 1) * BLK, _slab(pp)].T
        dk_ref[...] = jnp.zeros_like(dk_ref)
        dv_ref[...] = jnp.zeros_like(dv_ref)
        dqt_ref[...] = jnp.zeros_like(dqt_ref)
        lane = _lane((BLK, SLAB))
        tri_after, tri_before = _tri(True), _tri(False)
        valid = _causal_t()
        jrow = _row((N_BLK, 2 * BLK))
        chains = range(len(SB_CHAINS))

        def q_block(i, _):
            q2 = [_stack_heads(q_ref[_sb_rows(b, i), _slab(pp)], lane) for b, pp in SB_CHAINS]
            do2 = [_stack_heads(do_ref[_sb_rows(b, i), _slab(pp)], lane) for b, pp in SB_CHAINS]

            def key_block(j, carry_sp, before_u, mask):
                at = [(_sb_rows(b, j), _slab(pp)) for b, pp in SB_CHAINS]
                zt = [_dot(k_ref[at[c]], q2[c], NT) for c in chains]
                dw = [_dot(v_ref[at[c]], do2[c], NT) for c in chains]
                sp = [_sb_softplus(zt[c], mask) for c in chains]
                later = [_key_sums(tri_after, sp[c])[0] for c in chains]
                w = [_sb_weights(zt[c], sp[c], later[c] + carry_sp[c], mask) for c in chains]
                u = [dw[c] * w[c] for c in chains]
                for c in chains:
                    dv_ref[at[c]] += _dot(w[c].astype(BF16), do2[c])
                sums = [_key_sums(tri_before, u[c]) for c in chains]
                dz16 = []
                for c in chains:
                    sig = jnp.exp(zt[c] - sp[c])
                    dz = u[c] - sig * (u[c] + before_u[c] + sums[c][0])
                    if mask is not None:
                        dz = jnp.where(mask, dz, 0.0)
                    dz16.append(dz.astype(BF16))
                for c in chains:
                    dk_ref[at[c]] += _dot(dz16[c], q2[c])
                    dqt_ref[c] += _dot(kt_ref[c, j], dz16[c])
                return tuple(before_u[c] + sums[c][1] for c in chains)

            def earlier(j, before_u):
                carry_sp = [jnp.sum(jnp.where(jrow == j, c_ref[c * N_BLK + i], 0.0), axis=0, keepdims=True)
                            for c in chains]
                return key_block(j, carry_sp, before_u, None)

            zero = tuple(jnp.zeros((1, 2 * BLK), F32) for _ in chains)
            before_u = lax.fori_loop(0, i, earlier, zero)
            key_block(i, zero, before_u, valid)
            for c, (b, pp) in enumerate(SB_CHAINS):
                dq_ref[_sb_rows(b, i), _slab(pp)] = (_unstack_heads(dqt_ref[c].T, lane) * Q_SCALE).astype(BF16)
                dqt_ref[c] = jnp.zeros((SLAB, 2 * BLK), F32)
            return 0

        lax.fori_loop(0, N_BLK, q_block, 0)
        finish_dwout()

    n_ch = len(SB_CHAINS)
    vmem, hbm = pl.BlockSpec(memory_space=pltpu.VMEM), pl.BlockSpec(memory_space=pl.ANY)
    return pl.pallas_call(
        body, name="sb_bwd",
        in_specs=[vmem] * 5 + [hbm], out_specs=[vmem] * 3 + [hbm],
        out_shape=[jax.ShapeDtypeStruct((T_LOC, 512), BF16)] + [jax.ShapeDtypeStruct((T_LOC, 512), F32)] * 2
        + [jax.ShapeDtypeStruct((N_DEV, OUT_SHARD, D_MODEL), BF16)],
        scratch_shapes=[pltpu.VMEM((n_ch, N_BLK, SLAB, BLK), BF16), pltpu.VMEM((n_ch, SLAB, 2 * BLK), F32),
                        pltpu.SemaphoreType.DMA((7,)), pltpu.SemaphoreType.DMA((7,)), pltpu.SemaphoreType.DMA],
        compiler_params=pltpu.CompilerParams(vmem_limit_bytes=VMEM_LIMIT),
    )(qb, kb, vb, d_ob, carries, dwout)


def bwd_dw(x, norm_gain, dq_rot, dk_dup, dv_dup, qa_raw, ka_raw, cos, sin_s, q_gain2, k_gain2, dga, dgb, dqb, dkb, dvb):
    n_tiles = T_LOC // TM_DW

    def body(x_ref, ng_ref, dq_ref, dk_ref, dv_ref, qa_ref, ka_ref, cos_ref, sin_ref, qg_ref, kg_ref,
             dga_ref, dgb_ref, dqb_ref, dkb_ref, dvb_ref,
             dproj_ref, dw_hbm, dqg_ref, dkg_ref, acc_ref, stage_ref):
        step = pl.program_id(0)

        @pl.when(step == 0)
        def _():
            acc_ref[...] = jnp.zeros_like(acc_ref)
            dqg_ref[...] = jnp.zeros_like(dqg_ref)
            dkg_ref[...] = jnp.zeros_like(dkg_ref)

        lane = _lane((TM_DW, SLAB))
        bd = _head_blockdiag()
        cos, sin_s = cos_ref[...], sin_ref[...]

        def norm_rope_bwd(d_rot, raw, gain2):
            dy = d_rot * cos + _swap_half(d_rot * sin_s, lane)
            r = lax.rsqrt(_head_sum(raw * raw, bd) * (1.0 / HEAD_DIM) + EPS)
            xhat = raw * r
            dgain = jnp.sum(dy * xhat, axis=0, keepdims=True)
            dxh = dy * gain2
            mean = _head_sum(dxh * xhat, bd) * (1.0 / HEAD_DIM)
            return r * (dxh - xhat * mean), dgain

        def fold_dup(d_dup):
            a, b2 = d_dup[:, :SLAB], d_dup[:, SLAB:]
            return jnp.where(lane < HEAD_DIM, a + pltpu.roll(a, HEAD_DIM, 1), b2 + pltpu.roll(b2, HEAD_DIM, 1))

        pieces = []
        dqg = jnp.zeros((1, SLAB), F32)
        for p in range(4):
            sl = slice(p * SLAB, (p + 1) * SLAB)
            d_raw, dg = norm_rope_bwd(dq_ref[:, sl], qa_ref[:, sl], qg_ref[...])
            pieces.append(d_raw.astype(BF16))
            dqg = dqg + dg
        d_raw, dkg = norm_rope_bwd(fold_dup(dk_ref[...]), ka_ref[...], kg_ref[...])
        pieces.append(d_raw.astype(BF16))
        pieces.append(fold_dup(dv_ref[...]).astype(BF16))
        pieces += [dga_ref[...], dqb_ref[...], dkb_ref[...].astype(BF16), dvb_ref[...].astype(BF16),
                   dgb_ref[...]]
        dproj = jnp.concatenate(pieces, axis=1)
        dproj_ref[...] = dproj
        dqg_ref[0:1, :] += dqg + pltpu.roll(dqg, HEAD_DIM, 1)
        dkg_ref[0:1, :] += dkg + pltpu.roll(dkg, HEAD_DIM, 1)

        xv = x_ref[...]
        rstd = lax.rsqrt(jnp.mean(xv * xv, axis=-1, keepdims=True) + EPS)
        h = (xv * rstd * ng_ref[...]).astype(BF16)
        for r0 in range(0, IN_WIDTH, ACC_ROWS):
            acc_ref[r0:r0 + ACC_ROWS, :] += _dot(dproj[:, r0:r0 + ACC_ROWS], h, TN)

        @pl.when(step == n_tiles - 1)
        def _():
            for r0 in range(0, IN_WIDTH, ACC_ROWS):
                stage_ref[...] = acc_ref[r0:r0 + ACC_ROWS, :].astype(BF16)
                pltpu.sync_copy(stage_ref, dw_hbm.at[r0:r0 + ACC_ROWS, :])

    def tile(w):
        return pl.BlockSpec((TM_DW, w), lambda i: (i, 0))

    def whole(a):
        return pl.BlockSpec(a.shape, lambda i: (0, 0))

    const = lambda i: (0, 0)
    return pl.pallas_call(
        body, name="bwd_dw", grid=(n_tiles,),
        in_specs=[tile(D_MODEL), whole(norm_gain),
                  tile(512), tile(256), tile(256), tile(512), tile(128), tile(128), tile(128),
                  whole(q_gain2), whole(k_gain2), tile(512), tile(512), tile(512), tile(512), tile(512)],
        out_specs=[tile(IN_WIDTH), pl.BlockSpec(memory_space=pl.ANY),
                   pl.BlockSpec((8, SLAB), const), pl.BlockSpec((8, SLAB), const)],
        out_shape=[jax.ShapeDtypeStruct((T_LOC, IN_WIDTH), BF16), jax.ShapeDtypeStruct((IN_WIDTH, D_MODEL), BF16),
                   jax.ShapeDtypeStruct((8, SLAB), F32), jax.ShapeDtypeStruct((8, SLAB), F32)],
        scratch_shapes=[pltpu.VMEM((IN_WIDTH, D_MODEL), F32), pltpu.VMEM((ACC_ROWS, D_MODEL), BF16)],
        compiler_params=_params(("arbitrary",)),
    )(x, norm_gain, dq_rot, dk_dup, dv_dup, qa_raw, ka_raw, cos, sin_s, q_gain2, k_gain2, dga, dgb, dqb, dkb, dvb)


def bwd_dx(x, dout, norm_gain, win_t, dproj, dwin_t, dqg, dkg, dsink, loss_part):
    n_tiles = T_LOC // TM
    rows_per = IN_SHARD
    step_sums, step_merge = 3, 7

    def body(x_ref, dout_ref, ng_ref, w_hbm, dp_ref, a_hbm, dqg_ref, dkg_ref, dsink_ref, loss_ref,
             gx_ref, ra_hbm, rs_hbm, w_ref, dng_ref, s_ref, own_ref, sib_ref, snd_ref, extra_ref,
             w_sem, d2d_send, d2d_recv, ici_send, ici_recv, own_sems, s_send, s_recv, out_sem):
        step = pl.program_id(0)
        x, y, c = _mesh_pos()
        me, sibling = (x, y, c), (x, y, 1 - c)
        chips = {"own": (x, y), "x": (1 - x, y), "y": (x, 1 - y), "d": (1 - x, 1 - y)}
        index = {"own": 0, "x": 1, "y": 2, "d": 3}
        order = ("d", "x", "y", "own")

        def rows(pos):
            return a_hbm.at[pl.ds(_lin(pos) * rows_per, rows_per), :]

        def to_sibling(k):
            return pltpu.make_async_remote_copy(
                src_ref=rows((*chips[k], 1 - c)), dst_ref=sib_ref.at[index[k]],
                send_sem=d2d_send.at[index[k]], recv_sem=d2d_recv.at[index[k]], device_id=sibling, device_id_type=MESH)

        def mine(k):
            return pltpu.make_async_copy(rows((*chips[k], c)), own_ref.at[index[k]], own_sems.at[index[k]])

        def ici(n, to_chip, dst):
            return pltpu.make_async_remote_copy(
                src_ref=snd_ref.at[n], dst_ref=dst, send_sem=ici_send.at[n], recv_sem=ici_recv.at[n],
                device_id=(*chips[to_chip], c), device_id_type=MESH)

        def chip_sum(k):
            to_sibling(k).wait_recv()
            mine(k).wait()
            return own_ref[index[k]].astype(F32) + sib_ref[index[k]].astype(F32)

        def by_core(fn):
            pl.when(c == 0)(lambda: fn("x", "y"))
            pl.when(c == 1)(lambda: fn("y", "x"))

        @pl.when(step == 0)
        def _():
            cp = pltpu.make_async_copy(w_hbm, w_ref, w_sem)
            cp.start()
            for k in order:
                to_sibling(k).start()
                mine(k).start()
            dng_ref[...] = jnp.zeros_like(dng_ref)
            cp.wait()

        @pl.when(step == step_sums)
        def _():
            def first_sends(direct, via):
                snd_ref[0] = chip_sum("d").astype(BF16)
                ici(0, direct, extra_ref).start()
                snd_ref[1] = chip_sum(direct).astype(BF16)
                ici(1, direct, ra_hbm.at[index[direct]]).start()
            by_core(first_sends)

        @pl.when(step == step_merge)
        def _():
            def merge(direct, via):
                merged = chip_sum(via)
                ici(0, direct, extra_ref).wait_recv()
                snd_ref[2] = (merged + extra_ref[...].astype(F32)).astype(BF16)
                ici(2, via, ra_hbm.at[index[via]]).start()
                own_ref[0] = chip_sum("own").astype(BF16)
                pltpu.make_async_copy(own_ref.at[0], ra_hbm.at[0], out_sem).start()
            by_core(merge)

        xv = x_ref[...]
        rstd = lax.rsqrt(jnp.mean(xv * xv, axis=-1, keepdims=True) + EPS)
        xhat = xv * rstd
        gain = ng_ref[...]
        dh = _dot(dp_ref[...], w_ref[...])
        dng_ref[0:1, :] += jnp.sum(dh * xhat, axis=0, keepdims=True)
        dxh = dh * gain
        gx_ref[...] = dout_ref[...] + rstd * (dxh - xhat * jnp.mean(dxh * xhat, axis=-1, keepdims=True))

        @pl.when(step == n_tiles - 1)
        def _():
            s_ref[...] = jnp.concatenate(
                [dng_ref[...], dqg_ref[...], dkg_ref[...], dsink_ref[...], loss_ref[...]], axis=1)
            start_small, finish_small = _direct_exchange(
                lambda dev: s_ref, lambda dev: rs_hbm.at[_lin(dev)], s_send, s_recv, out_sem)

            def finish(direct, via):
                ici(1, direct, ra_hbm.at[index[direct]]).wait_recv()
                ici(2, via, ra_hbm.at[index[via]]).wait_recv()
                for n, to in ((0, direct), (1, direct), (2, via)):
                    ici(n, to, extra_ref).wait_send()
            by_core(finish)
            pltpu.make_async_copy(own_ref.at[0], ra_hbm.at[0], out_sem).wait()
            for k in order:
                to_sibling(k).wait_send()
            start_small()
            finish_small()

    def tile(w):
        return pl.BlockSpec((TM, w), lambda i: (i, 0))

    def whole(a):
        return pl.BlockSpec(a.shape, lambda i: (0, 0))

    hbm = pl.BlockSpec(memory_space=pl.ANY)
    block = (rows_per, D_MODEL)
    return pl.pallas_call(
        body, name="bwd_dx", grid=(n_tiles,),
        in_specs=[tile(D_MODEL), tile(D_MODEL), whole(norm_gain), hbm, tile(IN_WIDTH), hbm,
                  whole(dqg), whole(dkg), whole(dsink), whole(loss_part)],
        out_specs=[tile(D_MODEL), hbm, hbm],
        out_shape=[jax.ShapeDtypeStruct((T_LOC, D_MODEL), F32), jax.ShapeDtypeStruct((3,) + block, BF16),
                   jax.ShapeDtypeStruct((N_DEV, 8, SMALL_W), F32)],
        scratch_shapes=[pltpu.VMEM((IN_WIDTH, D_MODEL), BF16), pltpu.VMEM((8, D_MODEL), F32),
                        pltpu.VMEM((8, SMALL_W), F32),
                        pltpu.VMEM((4,) + block, BF16), pltpu.VMEM((4,) + block, BF16), pltpu.VMEM((3,) + block, BF16),
                        pltpu.VMEM(block, BF16),
                        pltpu.SemaphoreType.DMA, pltpu.SemaphoreType.DMA((4,)), pltpu.SemaphoreType.DMA((4,)),
                        pltpu.SemaphoreType.DMA((3,)), pltpu.SemaphoreType.DMA((3,)), pltpu.SemaphoreType.DMA((4,)),
                        pltpu.SemaphoreType.DMA((7,)), pltpu.SemaphoreType.DMA((7,)), pltpu.SemaphoreType.DMA],
        compiler_params=_params(("arbitrary",)),
    )(x, dout, norm_gain, win_t, dproj, dwin_t, dqg, dkg, dsink, loss_part)


def _adamw(w, g, m, v):
    m = ADAM_B1 * m + (1.0 - ADAM_B1) * g
    v = ADAM_B2 * v + (1.0 - ADAM_B2) * (g * g)
    m_hat = m / (1.0 - ADAM_B1 ** ADAM_STEP)
    v_hat = v / (1.0 - ADAM_B2 ** ADAM_STEP)
    delta = -ADAM_LR * (m_hat / (jnp.sqrt(v_hat) + ADAM_EPS) + ADAM_WD * w)
    return delta, m, v


def _sum_slots(r_ref):
    g = r_ref[0].astype(F32)
    for s in range(1, r_ref.shape[0]):
        g = g + r_ref[s].astype(F32)
    return g


def adamw_all(r_win, r_out, r_small, big_in, big_out, weights, moments_m, moments_v):
    n = len(weights)
    params = [big_in[0], big_out[0], *weights]

    def body(rw_ref, ro_ref, rs_ref, *refs):
        n_p = n + 2
        ins, outs = refs[:3 * n_p], refs[3 * n_p:]
        s = _sum_slots(rs_ref)
        eye = (_row((8, SLAB)) == _lane((8, SLAB))).astype(F32)
        sinks = jnp.sum(s[:, 1280:1408] * eye, axis=0, keepdims=True)
        grads = [_sum_slots(rw_ref), _sum_slots(ro_ref),
                 s[0:1, :D_MODEL], s[0:1, 1024:1024 + HEAD_DIM], s[0:1, 1152:1152 + HEAD_DIM], sinks[:, :8]]
        for k in range(n_p):
            outs[k][...] = grads[k]
            outs[n_p + k][...], outs[2 * n_p + k][...], outs[3 * n_p + k][...] = _adamw(
                ins[k][...], grads[k], ins[n_p + k][...], ins[2 * n_p + k][...])
        loss = jnp.sum(jnp.sum(s[:, 1408:1536], axis=1, keepdims=True), axis=0, keepdims=True) * (0.5 / D_MODEL)
        outs[4 * n_p][...] = loss

    n_p = n + 2
    res = pl.pallas_call(
        body, name="adamw_all",
        out_shape=[jax.ShapeDtypeStruct(p.shape, F32) for p in params] * 4 + [jax.ShapeDtypeStruct((1, 1), F32)],
        compiler_params=pltpu.CompilerParams(vmem_limit_bytes=VMEM_LIMIT),
    )(r_win, r_out, r_small, big_in[0], big_out[0], *weights, big_in[1], big_out[1], *moments_m,
      big_in[2], big_out[2], *moments_v)
    return [res[k * n_p:(k + 1) * n_p] for k in range(4)], res[4 * n_p]


def kernel(x, positions, norm_gain, w_in, q_norm_gain, k_norm_gain, sinks, w_out, loss_target, m_norm_gain, m_w_in, m_q_norm_gain, m_k_norm_gain, m_sinks, m_w_out, v_norm_gain, v_w_in, v_q_norm_gain, v_k_norm_gain, v_sinks, v_w_out):
    x2 = x.reshape(T_LOC, D_MODEL)
    tgt2 = loss_target.reshape(T_LOC, D_MODEL)
    pos2 = positions.reshape(1, T_LOC)
    half = HEAD_DIM // 2
    inv_freq = ROPE_THETA ** (-jnp.arange(half, dtype=F32) * 2.0 / HEAD_DIM)
    inv_freq = jnp.tile(inv_freq, SLAB // half).reshape(SLAB, 1)
    sin_sign = jnp.tile(jnp.concatenate([-jnp.ones((half,), F32), jnp.ones((half,), F32)]), 2).reshape(1, SLAB)
    q_gain2 = jnp.tile(q_norm_gain, (1, 2))
    k_gain2 = jnp.tile(k_norm_gain, (1, 2))

    win_t = gather_weights(w_in.reshape(D_MODEL, IN_SHARD).T)

    (qa_raw, ka_raw, q_rot, k_dup, v_dup, ga, qb, kb, vb, gb, cos, sin_s, wout) = fwd_proj(
        x2, pos2, norm_gain, win_t, inv_freq, sin_sign, q_gain2, k_gain2, w_out.reshape(OUT_SHARD, D_MODEL).astype(BF16))
    o_a = swa_fwd(q_rot, k_dup, v_dup, sinks)
    o_b, carries = sb_fwd(qb, kb, vb)
    dout, d_oa, d_ob, dga, dgb, dwout, loss_part = out_loss(o_a, o_b, ga, gb, x2, tgt2, wout)
    dq_rot, dk_dup, dv_dup, dsink = swa_bwd(q_rot, k_dup, v_dup, o_a, d_oa, sinks)
    dqb, dkb, dvb, r_out = sb_bwd(qb, kb, vb, d_ob, carries, dwout)
    dproj, dwin_t, dqg, dkg = bwd_dw(
        x2, norm_gain, dq_rot, dk_dup, dv_dup, qa_raw, ka_raw, cos, sin_s, q_gain2, k_gain2, dga, dgb, dqb, dkb, dvb)
    grad_x, r_win, r_small = bwd_dx(x2, dout, norm_gain, win_t, dproj, dwin_t, dqg, dkg, dsink, loss_part)

    w_in2, m_in2, v_in2 = (a.reshape(D_MODEL, IN_SHARD).T for a in (w_in, m_w_in, v_w_in))
    w_out2, m_out2, v_out2 = (a.reshape(OUT_SHARD, D_MODEL) for a in (w_out, m_w_out, v_w_out))
    kinds, loss = adamw_all(
        r_win, r_out, r_small, (w_in2, m_in2, v_in2), (w_out2, m_out2, v_out2),
        (norm_gain, q_norm_gain, k_norm_gain, sinks),
        (m_norm_gain, m_q_norm_gain, m_k_norm_gain, m_sinks), (v_norm_gain, v_q_norm_gain, v_k_norm_gain, v_sinks))

    def leaves(k):
        big_in, big_out, ng, qg, kg, sk = kinds[k]
        return (ng, big_in.T.reshape(1, D_MODEL, IN_SHARD), qg, kg, sk, big_out.reshape(1, OUT_SHARD, D_MODEL))

    return (loss.reshape(()), grad_x.reshape(B_LOC, SEQ, D_MODEL), *leaves(0), *leaves(1), *leaves(2), *leaves(3))
```

```python
import functools

import jax
import jax.numpy as jnp
from jax import lax
from jax.experimental import pallas as pl
from jax.experimental.pallas import tpu as pltpu

F32 = jnp.float32
BF16 = jnp.bfloat16

N_DEV = 8
D_MODEL = 1024
SEQ = 2048
B_LOC = 2
T_LOC = B_LOC * SEQ
HEAD_DIM = 64
HEAD_SHIFT = 6
BLK = 128
N_BLK = SEQ // BLK
SLAB = 128
IN_WIDTH = 3328
IN_SHARD = IN_WIDTH // N_DEV
OUT_SHARD = D_MODEL // N_DEV
EPS = 1e-6
ROPE_THETA = 10000.0
Q_SCALE = 0.125
R_QA, R_KA, R_VA, R_GA, R_QB, R_KB, R_VB, R_GB, R_END = 0, 512, 640, 768, 1280, 1792, 2304, 2816, 3328
SMALL_W = 1536
ADAM_LR, ADAM_B1, ADAM_B2, ADAM_EPS, ADAM_WD, ADAM_STEP = 0.001, 0.9, 0.999, 1e-08, 0.01, 10
TM = 512
ACC_ROWS = 256
GATHER_CHUNKS = 2
VMEM_LIMIT = 56 * 1024 * 1024

MESH = pl.DeviceIdType.MESH
NT = (((1,), (1,)), ((), ()))
TN = (((0,), (0,)), ((), ()))


def _params(sem, limit=VMEM_LIMIT):
    return pltpu.CompilerParams(dimension_semantics=sem, vmem_limit_bytes=limit)


def _dot(a, b, dims=None):
    if dims is None:
        return jnp.dot(a, b, preferred_element_type=F32)
    return lax.dot_general(a, b, dims, preferred_element_type=F32)


def _lane(shape):
    return lax.broadcasted_iota(jnp.int32, shape, len(shape) - 1)


def _row(shape):
    return lax.broadcasted_iota(jnp.int32, shape, 0)


def _head_blockdiag():
    return ((_row((SLAB, SLAB)) >> HEAD_SHIFT) == (_lane((SLAB, SLAB)) >> HEAD_SHIFT)).astype(BF16)


def _head_sum(x, bd):
    return _dot(x.astype(BF16), bd)


def _swap_half(y, lane):
    return jnp.where((lane & 32) != 0, pltpu.roll(y, 32, 1), pltpu.roll(y, 96, 1))


def _stack_heads(q, lane):
    zero = jnp.zeros_like(q)
    return jnp.concatenate([jnp.where(lane < HEAD_DIM, q, zero), jnp.where(lane >= HEAD_DIM, q, zero)], axis=0)


def _unstack_heads(x2, lane):
    return jnp.where(lane < HEAD_DIM, x2[:BLK], x2[BLK:])


def _sigmoid(x):
    return 1.0 / (1.0 + jnp.exp(-x))


def _mesh_pos():
    return lax.axis_index("x"), lax.axis_index("y"), lax.axis_index("c")


def _flip(pos, mask):
    return tuple(1 - p if m else p for p, m in zip(pos, mask))


def _lin(pos):
    return 4 * pos[0] + 2 * pos[1] + pos[2]


DEV_FLIPS = [(fx, fy, fc) for fx in (0, 1) for fy in (0, 1) for fc in (0, 1)][1:]


def _direct_exchange(src_for, dst_slot, send_sems, recv_sems, local_sem):
    me = _mesh_pos()

    def copy(k, to):
        return pltpu.make_async_remote_copy(
            src_ref=src_for(to), dst_ref=dst_slot(me), send_sem=send_sems.at[k], recv_sem=recv_sems.at[k],
            device_id=to, device_id_type=MESH)

    def landed(k, frm):
        return pltpu.make_async_remote_copy(
            src_ref=src_for(frm), dst_ref=dst_slot(frm), send_sem=send_sems.at[k], recv_sem=recv_sems.at[k],
            device_id=frm, device_id_type=MESH)

    local = None if local_sem is None else pltpu.make_async_copy(src_for(me), dst_slot(me), local_sem)
    peers = [_flip(me, f) for f in DEV_FLIPS]

    def start():
        if local is not None:
            local.start()
        for k, to in enumerate(peers):
            copy(k, to).start()

    def finish():
        for k, frm in enumerate(peers):
            landed(k, frm).wait_recv()
        for k, to in enumerate(peers):
            copy(k, to).wait_send()
        if local is not None:
            local.wait()

    return start, finish


def gather_weights(shard):
    m = shard.shape[0]
    piece = m // GATHER_CHUNKS
    pieces = range(GATHER_CHUNKS)

    def body(f32_ref, o_ref, a_ref, ici_send, ici_recv, d2d_send, d2d_recv, local_sem):
        a_ref[...] = f32_ref[...].astype(BF16)
        x, y, c = _mesh_pos()
        me, sibling = (x, y, c), (x, y, 1 - c)
        chip_x, chip_y, chip_d = (1 - x, y), (x, 1 - y), (1 - x, 1 - y)

        def rows(pos, q):
            return o_ref.at[pl.ds(_lin(pos) * m + q * piece, piece), :]

        def own(q):
            return a_ref.at[pl.ds(q * piece, piece), :]

        def ici(k, q, block, to, src=None):
            return pltpu.make_async_remote_copy(
                src_ref=rows(block, q) if src is None else src, dst_ref=rows(block, q),
                send_sem=ici_send.at[k, q], recv_sem=ici_recv.at[k, q], device_id=to, device_id_type=MESH)

        def d2d(k, q, chip, mine, src=None):
            block = (*chip, c) if mine else (*chip, 1 - c)
            return pltpu.make_async_remote_copy(
                src_ref=rows(block, q) if src is None else src, dst_ref=rows(block, q),
                send_sem=d2d_send.at[k, q], recv_sem=d2d_recv.at[k, q], device_id=sibling, device_id_type=MESH)

        local = pltpu.make_async_copy(a_ref, o_ref.at[pl.ds(_lin(me) * m, m), :], local_sem)
        local.start()
        sends = []
        for q in pieces:
            sends += [ici(0, q, me, (*chip_x, c), src=own(q)), ici(1, q, me, (*chip_y, c), src=own(q)),
                      d2d(0, q, (x, y), True, src=own(q))]
        for cp in sends:
            cp.start()

        def pass_on(first, k_first, second, k_second, onward):
            moved = []
            for q in pieces:
                ici(k_first, q, (*first, c), me).wait_recv()
                moved += [ici(2, q, (*first, c), (*onward, c)), d2d(1 + k_first, q, first, True)]
                for cp in moved[-2:]:
                    cp.start()
            for q in pieces:
                ici(k_second, q, (*second, c), me).wait_recv()
                moved.append(d2d(1 + k_second, q, second, True))
                moved[-1].start()
            for q in pieces:
                ici(2, q, (*chip_d, c), me).wait_recv()
                moved.append(d2d(3, q, chip_d, True))
                moved[-1].start()
            for cp in moved:
                cp.wait_send()

        @pl.when(c == 0)
        def _():
            pass_on(chip_y, 1, chip_x, 0, chip_x)

        @pl.when(c == 1)
        def _():
            pass_on(chip_x, 0, chip_y, 1, chip_y)

        for k, chip in enumerate([(x, y), chip_x, chip_y, chip_d]):
            for q in pieces:
                d2d(k, q, chip, False).wait_recv()
        for cp in sends:
            cp.wait_send()
        local.wait()

    vmem = pl.BlockSpec(memory_space=pltpu.VMEM)
    n_q = GATHER_CHUNKS
    return pl.pallas_call(
        body, name="gather_weights",
        out_shape=jax.ShapeDtypeStruct((N_DEV * m, shard.shape[1]), BF16),
        in_specs=[vmem], out_specs=vmem,
        scratch_shapes=[pltpu.VMEM(shard.shape, BF16), pltpu.SemaphoreType.DMA((3, n_q)), pltpu.SemaphoreType.DMA((3, n_q)),
                        pltpu.SemaphoreType.DMA((4, n_q)), pltpu.SemaphoreType.DMA((4, n_q)), pltpu.SemaphoreType.DMA],
        compiler_params=pltpu.CompilerParams(vmem_limit_bytes=VMEM_LIMIT),
    )(shard)


def _norm_rope(xs, gain2, cos, sin_s, bd, lane):
    r = lax.rsqrt(_head_sum(xs * xs, bd) * (1.0 / HEAD_DIM) + EPS)
    y = xs * r * gain2
    return y * cos + _swap_half(y, lane) * sin_s


def _dup_heads(xs, lane):
    r = pltpu.roll(xs, HEAD_DIM, 1)
    lo = lane < HEAD_DIM
    return jnp.concatenate([jnp.where(lo, xs, r), jnp.where(lo, r, xs)], axis=1)


def fwd_proj(x, pos, norm_gain, win_t, inv_freq, sin_sign, q_gain2, k_gain2, wout_shard):
    n_tiles = T_LOC // TM

    def body(x_ref, pos_ref, ng_ref, w_ref, if_ref, sg_ref, qg_ref, kg_ref, ws_hbm,
             qa_raw_ref, ka_raw_ref, q_rot_ref, k_dup_ref, v_dup_ref, ga_ref, qb_ref, kb_ref, vb_ref, gb_ref,
             cos_ref, sin_ref, wo_hbm, wo_send, wo_recv, wo_local):
        start_wout, finish_wout = _direct_exchange(
            lambda dev: ws_hbm, lambda dev: wo_hbm.at[pl.ds(_lin(dev) * OUT_SHARD, OUT_SHARD), :],
            wo_send, wo_recv, wo_local)
        pl.when(pl.program_id(0) == 0)(start_wout)

        xv = x_ref[...]
        rstd = lax.rsqrt(jnp.mean(xv * xv, axis=-1, keepdims=True) + EPS)
        h = (xv * rstd * ng_ref[...]).astype(BF16)

        def proj(r0, r1):
            return _dot(h, w_ref[r0:r1, :], NT)

        ang_t = if_ref[...] * pos_ref[...].astype(F32)
        cos = jnp.cos(ang_t).T
        sin_s = jnp.sin(ang_t).T * sg_ref[...]
        cos_ref[...] = cos
        sin_ref[...] = sin_s
        lane = _lane((TM, SLAB))
        bd = _head_blockdiag()

        qa = proj(R_QA, R_KA)
        qa_raw_ref[...] = qa
        for p in range(4):
            sl = slice(p * SLAB, (p + 1) * SLAB)
            q_rot_ref[:, sl] = (_norm_rope(qa[:, sl], qg_ref[...], cos, sin_s, bd, lane) * Q_SCALE).astype(BF16)
        ka = proj(R_KA, R_VA)
        ka_raw_ref[...] = ka
        k_dup_ref[...] = _dup_heads(_norm_rope(ka, kg_ref[...], cos, sin_s, bd, lane), lane).astype(BF16)
        v_dup_ref[...] = _dup_heads(proj(R_VA, R_GA), lane).astype(BF16)
        ga_ref[...] = proj(R_GA, R_QB).astype(BF16)
        qb_ref[...] = (proj(R_QB, R_KB) * Q_SCALE).astype(BF16)
        kb_ref[...] = proj(R_KB, R_VB).astype(BF16)
        vb_ref[...] = proj(R_VB, R_GB).astype(BF16)
        gb_ref[...] = proj(R_GB, R_END).astype(BF16)
        pl.when(pl.program_id(0) == n_tiles - 1)(finish_wout)

    def tile(w):
        return pl.BlockSpec((TM, w), lambda i: (i, 0))

    def whole(a):
        return pl.BlockSpec(a.shape, lambda i: (0, 0))

    hbm = pl.BlockSpec(memory_space=pl.ANY)
    widths = [(512, F32), (128, F32), (512, BF16), (256, BF16), (256, BF16), (512, BF16), (512, BF16), (512, BF16),
              (512, BF16), (512, BF16), (128, F32), (128, F32)]
    return pl.pallas_call(
        body, name="fwd_proj", grid=(n_tiles,),
        in_specs=[tile(D_MODEL), pl.BlockSpec((1, TM), lambda i: (0, i)), whole(norm_gain), whole(win_t),
                  whole(inv_freq), whole(sin_sign),
                  whole(q_gain2), whole(k_gain2), hbm],
        out_specs=[tile(w) for w, _ in widths] + [hbm],
        out_shape=[jax.ShapeDtypeStruct((T_LOC, w), dt) for w, dt in widths]
        + [jax.ShapeDtypeStruct((D_MODEL, D_MODEL), BF16)],
        scratch_shapes=[pltpu.SemaphoreType.DMA((7,)), pltpu.SemaphoreType.DMA((7,)), pltpu.SemaphoreType.DMA],
        compiler_params=_params(("arbitrary",)),
    )(x, pos, norm_gain, win_t, inv_freq, sin_sign, q_gain2, k_gain2, wout_shard)


def _swa_window(prev_ref, cur_ref, p):
    gsl = _slab(p // 2)
    return jnp.concatenate([prev_ref[:, gsl], cur_ref[:, gsl]], axis=0)


def _swa_probs(s, sinks_ref, p, i):
    shape = (2 * BLK, 2 * BLK)
    r = _row(shape) & (BLK - 1)
    cidx = _lane(shape)
    valid = (cidx > r) & (cidx <= r + BLK) & ((cidx >= BLK) | (i > 0))
    s = jnp.where(valid, s, -jnp.inf)
    sink = jnp.where(_row((2 * BLK, 1)) < BLK, sinks_ref[0, 2 * p], sinks_ref[0, 2 * p + 1])
    m = jnp.maximum(jnp.max(s, axis=-1, keepdims=True), sink)
    e = jnp.exp(s - m)
    e_sink = jnp.exp(sink - m)
    den = jnp.sum(e, axis=-1, keepdims=True) + e_sink
    return e / den, e_sink / den


SWA_CHAINS = [(b, p) for b in range(B_LOC) for p in range(4)]


def _swa_specs():
    def cur(w):
        return pl.BlockSpec((B_LOC, BLK, w), lambda i: (0, i, 0))

    def prev(w):
        return pl.BlockSpec((B_LOC, BLK, w), lambda i: (0, jnp.maximum(i - 1, 0), 0))

    return cur, prev


def swa_fwd(q_rot, k_dup, v_dup, sinks):
    def body(q_ref, kp_ref, kc_ref, vp_ref, vc_ref, sinks_ref, o_ref):
        i = pl.program_id(0)
        lane = _lane((BLK, SLAB))
        s = [_dot(_stack_heads(q_ref[b, :, _slab(p)], lane), _swa_window(kp_ref.at[b], kc_ref.at[b], p), NT)
             for b, p in SWA_CHAINS]
        pn = [_swa_probs(s[c], sinks_ref, p, i)[0].astype(BF16) for c, (b, p) in enumerate(SWA_CHAINS)]
        for c, (b, p) in enumerate(SWA_CHAINS):
            o = _unstack_heads(_dot(pn[c], _swa_window(vp_ref.at[b], vc_ref.at[b], p)), lane)
            o_ref[b, :, _slab(p)] = o.astype(BF16)

    cur, prev = _swa_specs()
    q3, k3, v3 = (a.reshape(B_LOC, SEQ, a.shape[1]) for a in (q_rot, k_dup, v_dup))
    return pl.pallas_call(
        body, name="swa_fwd", grid=(N_BLK,),
        in_specs=[cur(512), prev(256), cur(256), prev(256), cur(256), pl.BlockSpec(memory_space=pltpu.SMEM)],
        out_specs=cur(512),
        out_shape=jax.ShapeDtypeStruct((B_LOC, SEQ, 512), BF16),
        compiler_params=_params(("arbitrary",)),
    )(q3, k3, k3, v3, v3, sinks).reshape(T_LOC, 512)


def _tri(suffix):
    r, cidx = _row((BLK + 16, BLK)), _lane((BLK + 16, BLK))
    tri = (cidx > r) if suffix else (cidx < r)
    return (tri | (r >= BLK)).astype(BF16)


def _key_sums(tri, x):
    res = _dot(tri, x.astype(BF16))
    return res[:BLK], res[BLK:BLK + 1]


def _sb_softplus(zt, valid):
    neg_abs = lax.bitcast_convert_type(lax.bitcast_convert_type(zt, jnp.uint32) | jnp.uint32(0x80000000), F32)
    sp = jnp.maximum(zt, 0.0) + jnp.log(1.0 + jnp.exp(neg_abs))
    return sp if valid is None else jnp.where(valid, sp, 0.0)


def _sb_weights(zt, sp, later, valid):
    w = jnp.exp(zt - sp - later)
    return w if valid is None else jnp.where(valid, w, 0.0)


def _slab(pp):
    return slice(pp * SLAB, (pp + 1) * SLAB)


def _blk(j):
    return pl.ds(pl.multiple_of(j * BLK, BLK), BLK)


def _causal_t():
    return _row((BLK, 2 * BLK)) < (_lane((BLK, 2 * BLK)) & (BLK - 1))


def _sb_rows(b, j):
    return pl.ds(pl.multiple_of(b * SEQ + j * BLK, BLK), BLK)


SB_CHAINS = [(b, pp) for b in range(B_LOC) for pp in range(4)]


def sb_fwd(qb, kb, vb):
    def body(q_ref, k_ref, v_ref, o_ref, c_ref, vt_ref, ot_ref):
        for c, (b, pp) in enumerate(SB_CHAINS):
            for j in range(N_BLK):
                vt_ref[c, j] = v_ref[b * SEQ + j * BLK:b * SEQ + (j + 1) * BLK, _slab(pp)].T
        lane = _lane((BLK, SLAB))
        tri = _tri(True)
        valid = _causal_t()
        jrow = _row((N_BLK, 2 * BLK))
        chains = range(len(SB_CHAINS))

        def q_block(i, _):
            q2 = [_stack_heads(q_ref[_sb_rows(b, i), _slab(pp)], lane) for b, pp in SB_CHAINS]

            def key_block(j, carry, mask, first):
                zt = [_dot(k_ref[_sb_rows(b, j), _slab(pp)], q2[c], NT) for c, (b, pp) in enumerate(SB_CHAINS)]
                sp = [_sb_softplus(zt[c], mask) for c in chains]
                sums = [_key_sums(tri, sp[c]) for c in chains]
                w = [_sb_weights(zt[c], sp[c], sums[c][0] + carry[c], mask) for c in chains]
                for c in chains:
                    pv = _dot(vt_ref[c, j], w[c].astype(BF16))
                    if first:
                        ot_ref[c] = pv
                    else:
                        ot_ref[c] += pv
                return tuple(carry[c] + sums[c][1] for c in chains)

            def earlier(jj, state):
                carry, saved = state
                j = i - 1 - jj
                saved = tuple(jnp.where(jrow == j, carry[c], saved[c]) for c in chains)
                return key_block(j, carry, None, False), saved

            zero = tuple(jnp.zeros((1, 2 * BLK), F32) for _ in chains)
            carry = key_block(i, zero, valid, True)
            _, saved = lax.fori_loop(0, i, earlier, (carry, tuple(jnp.zeros((N_BLK, 2 * BLK), F32) for _ in chains)))
            for c, (b, pp) in enumerate(SB_CHAINS):
                o_ref[_sb_rows(b, i), _slab(pp)] = _unstack_heads(ot_ref[c].T, lane).astype(BF16)
                c_ref[c * N_BLK + i] = saved[c]
            return 0

        lax.fori_loop(0, N_BLK, q_block, 0)

    n_ch = len(SB_CHAINS)
    vmem = pl.BlockSpec(memory_space=pltpu.VMEM)
    return pl.pallas_call(
        body, name="sb_fwd",
        in_specs=[vmem] * 3, out_specs=[vmem] * 2,
        out_shape=[jax.ShapeDtypeStruct((T_LOC, 512), BF16), jax.ShapeDtypeStruct((n_ch * N_BLK, N_BLK, 2 * BLK), F32)],
        scratch_shapes=[pltpu.VMEM((n_ch, N_BLK, SLAB, BLK), BF16), pltpu.VMEM((n_ch, SLAB, 2 * BLK), F32)],
        compiler_params=pltpu.CompilerParams(vmem_limit_bytes=VMEM_LIMIT),
    )(qb, kb, vb)


def out_loss(o_a, o_b, ga, gb, x, target, wout):
    n_tiles = T_LOC // TM

    def body(oa_ref, ob_ref, ga_ref, gb_ref, x_ref, t_ref, w_ref,
             dout_ref, doa_ref, dob_ref, dga_ref, dgb_ref, dw_ref, loss_ref, acc_ref):
        step = pl.program_id(0)

        @pl.when(step == 0)
        def _():
            acc_ref[...] = jnp.zeros_like(acc_ref)
            loss_ref[...] = jnp.zeros_like(loss_ref)

        oa, ob, gav, gbv = (r[...].astype(F32) for r in (oa_ref, ob_ref, ga_ref, gb_ref))
        sa, sb = _sigmoid(gav), _sigmoid(gbv)
        silu_a, silu_b = gav * sa, gbv * sb
        y = jnp.concatenate([oa * silu_a, ob * silu_b], axis=1).astype(BF16)
        err = x_ref[...] + _dot(y, w_ref[...]) - t_ref[...]
        e2 = err * err
        part = jnp.sum(e2.reshape(TM // 8, 8, D_MODEL), axis=0)
        loss_ref[...] += functools.reduce(lambda a, b: a + b, [part[:, k * 128:(k + 1) * 128] for k in range(8)])
        dout = err * (1.0 / D_MODEL)
        dout_ref[...] = dout
        dob16 = dout.astype(BF16)
        for r0 in range(0, D_MODEL, ACC_ROWS):
            acc_ref[r0:r0 + ACC_ROWS, :] += _dot(y[:, r0:r0 + ACC_ROWS], dob16, TN)
        dy = _dot(dob16, w_ref[...], NT)
        dya, dyb = dy[:, :512], dy[:, 512:]
        doa_ref[...] = (dya * silu_a).astype(BF16)
        dob_ref[...] = (dyb * silu_b).astype(BF16)
        dga_ref[...] = (dya * oa * (sa * (1.0 + gav * (1.0 - sa)))).astype(BF16)
        dgb_ref[...] = (dyb * ob * (sb * (1.0 + gbv * (1.0 - sb)))).astype(BF16)

        @pl.when(step == n_tiles - 1)
        def _():
            dw_ref[...] = acc_ref[...].astype(BF16)

    def tile(w):
        return pl.BlockSpec((TM, w), lambda i: (i, 0))

    const = lambda i: (0, 0)
    return pl.pallas_call(
        body, name="out_loss", grid=(n_tiles,),
        in_specs=[tile(512)] * 4 + [tile(D_MODEL)] * 2 + [pl.BlockSpec((D_MODEL, D_MODEL), const)],
        out_specs=[tile(D_MODEL), tile(512), tile(512), tile(512), tile(512),
                   pl.BlockSpec((D_MODEL, D_MODEL), const), pl.BlockSpec((8, 128), const)],
        out_shape=[jax.ShapeDtypeStruct((T_LOC, D_MODEL), F32)] + [jax.ShapeDtypeStruct((T_LOC, 512), BF16)] * 4
        + [jax.ShapeDtypeStruct((D_MODEL, D_MODEL), BF16), jax.ShapeDtypeStruct((8, 128), F32)],
        scratch_shapes=[pltpu.VMEM((D_MODEL, D_MODEL), F32)],
        compiler_params=_params(("arbitrary",)),
    )(o_a, o_b, ga, gb, x, target, wout)


def swa_bwd(q_rot, k_dup, v_dup, o_a, d_oa, sinks):
    def body(q_ref, kp_ref, kc_ref, vp_ref, vc_ref, o_ref, do_ref, sinks_ref, dq_ref, dk_ref, dv_ref, dsink_ref):
        i = pl.program_id(0)

        @pl.when(i == 0)
        def _():
            dk_ref[...] = jnp.zeros_like(dk_ref)
            dv_ref[...] = jnp.zeros_like(dv_ref)
            dsink_ref[...] = jnp.zeros_like(dsink_ref)

        lane = _lane((BLK, SLAB))
        rows_prev, rows_cur = _blk(jnp.maximum(i - 1, 0)), _blk(i)
        chains = range(len(SWA_CHAINS))
        q2 = [_stack_heads(q_ref[b, :, _slab(p)], lane) for b, p in SWA_CHAINS]
        do2 = [_stack_heads(do_ref[b, :, _slab(p)], lane) for b, p in SWA_CHAINS]
        keys = [_swa_window(kp_ref.at[b], kc_ref.at[b], p) for b, p in SWA_CHAINS]
        s = [_dot(q2[c], keys[c], NT) for c in chains]
        dp = [_dot(do2[c], _swa_window(vp_ref.at[b], vc_ref.at[b], p), NT) for c, (b, p) in enumerate(SWA_CHAINS)]
        ds, pn16, cols = [], [], []
        for c, (b, p) in enumerate(SWA_CHAINS):
            pn, p_sink = _swa_probs(s[c], sinks_ref, p, i)
            o = o_ref[b, :, _slab(p)].astype(F32)
            delta =jnp.sum(do2[c].astype(F32) * jnp.concatenate([o, o], axis=0), axis=-1, keepdims=True)
            ds.append((pn * (dp[c] - delta)).astype(BF16))
            pn16.append(pn.astype(BF16))
            cols.append(-p_sink * delta)
        for c, (b, p) in enumerate(SWA_CHAINS):
            dq_ref[b, :, _slab(p)] = _unstack_heads(_dot(ds[c], keys[c]), lane) * Q_SCALE
        dk2 = [_dot(ds[c], q2[c], TN) for c in chains]
        dv2 = [_dot(pn16[c], do2[c], TN) for c in chains]
        for c, (b, p) in enumerate(SWA_CHAINS):
            gsl = _slab(p // 2)
            dk_ref[b, rows_prev, gsl] += dk2[c][:BLK]
            dk_ref[b, rows_cur, gsl] += dk2[c][BLK:]
            dv_ref[b, rows_prev, gsl] += dv2[c][:BLK]
            dv_ref[b, rows_cur, gsl] += dv2[c][BLK:]
            for e in range(2):
                dsink_ref[2 * p + e:2 * p + e + 1, :] += jnp.sum(cols[c][e * BLK:(e + 1) * BLK], axis=0, keepdims=True)

    cur, prev = _swa_specs()
    whole = pl.BlockSpec((B_LOC, SEQ, 256), lambda i: (0, 0, 0))
    q3, k3, v3, o3, do3 = (a.reshape(B_LOC, SEQ, a.shape[1]) for a in (q_rot, k_dup, v_dup, o_a, d_oa))
    dq, dk, dv, dsink = pl.pallas_call(
        body, name="swa_bwd", grid=(N_BLK,),
        in_specs=[cur(512), prev(256), cur(256), prev(256), cur(256), cur(512), cur(512),
                  pl.BlockSpec(memory_space=pltpu.SMEM)],
        out_specs=[cur(512), whole, whole, pl.BlockSpec((8, 128), lambda i: (0, 0))],
        out_shape=[jax.ShapeDtypeStruct((B_LOC, SEQ, 512), F32), jax.ShapeDtypeStruct((B_LOC, SEQ, 256), F32),
                   jax.ShapeDtypeStruct((B_LOC, SEQ, 256), F32), jax.ShapeDtypeStruct((8, 128), F32)],
        compiler_params=_params(("arbitrary",)),
    )(q3, k3, k3, v3, v3, o3, do3, sinks)
    return dq.reshape(T_LOC, 512), dk.reshape(T_LOC, 256), dv.reshape(T_LOC, 256), dsink


def sb_bwd(qb, kb, vb, d_ob, carries, dwout):
    def body(q_ref, k_ref, v_ref, do_ref, c_ref, dw_hbm, dq_ref, dk_ref, dv_ref, rw_hbm, kt_ref, dqt_ref,
             rw_send, rw_recv, rw_local):
        start_dwout, finish_dwout = _direct_exchange(
            lambda dev: dw_hbm.at[pl.ds(_lin(dev) * OUT_SHARD, OUT_SHARD), :], lambda dev: rw_hbm.at[_lin(dev)],
            rw_send, rw_recv, rw_local)
        start_dwout()
        for c, (b, pp) in enumerate(SB_CHAINS):
            for j in range(N_BLK):
                kt_ref[c, j] = k_ref[b * SEQ + j * BLK:b * SEQ + (j + 1) * BLK, _slab(pp)].T
        dk_ref[...] = jnp.zeros_like(dk_ref)
        dv_ref[...] = jnp.zeros_like(dv_ref)
        dqt_ref[...] = jnp.zeros_like(dqt_ref)
        lane = _lane((BLK, SLAB))
        tri_after, tri_before = _tri(True), _tri(False)
        valid = _causal_t()
        jrow = _row((N_BLK, 2 * BLK))
        chains = range(len(SB_CHAINS))

        def q_block(i, _):
            q2 = [_stack_heads(q_ref[_sb_rows(b, i), _slab(pp)], lane) for b, pp in SB_CHAINS]
            do2 = [_stack_heads(do_ref[_sb_rows(b, i), _slab(pp)], lane) for b, pp in SB_CHAINS]

            def key_block(j, carry_sp, before_u, mask):
                at = [(_sb_rows(b, j), _slab(pp)) for b, pp in SB_CHAINS]
                zt = [_dot(k_ref[at[c]], q2[c], NT) for c in chains]
                dw = [_dot(v_ref[at[c]], do2[c], NT) for c in chains]
                sp = [_sb_softplus(zt[c], mask) for c in chains]
                later = [_key_sums(tri_after, sp[c])[0] for c in chains]
                w = [_sb_weights(zt[c], sp[c], later[c] + carry_sp[c], mask) for c in chains]
                u = [dw[c] * w[c] for c in chains]
                for c in chains:
                    dv_ref[at[c]] += _dot(w[c].astype(BF16), do2[c])
                sums = [_key_sums(tri_before, u[c]) for c in chains]
                dz16 = []
                for c in chains:
                    sig = jnp.exp(zt[c] - sp[c])
                    dz = u[c] - sig * (u[c] + before_u[c] + sums[c][0])
                    if mask is not None:
                        dz = jnp.where(mask, dz, 0.0)
                    dz16.append(dz.astype(BF16))
                for c in chains:
                    dk_ref[at[c]] += _dot(dz16[c], q2[c])
                    dqt_ref[c] += _dot(kt_ref[c, j], dz16[c])
                return tuple(before_u[c] + sums[c][1] for c in chains)

            def earlier(j, before_u):
                carry_sp = [jnp.sum(jnp.where(jrow == j, c_ref[c * N_BLK + i], 0.0), axis=0, keepdims=True)
                            for c in chains]
                return key_block(j, carry_sp, before_u, None)

            zero = tuple(jnp.zeros((1, 2 * BLK), F32) for _ in chains)
            before_u = lax.fori_loop(0, i, earlier, zero)
            key_block(i, zero, before_u, valid)
            for c, (b, pp) in enumerate(SB_CHAINS):
                dq_ref[_sb_rows(b, i), _slab(pp)] = (_unstack_heads(dqt_ref[c].T, lane) * Q_SCALE).astype(BF16)
                dqt_ref[c] = jnp.zeros((SLAB, 2 * BLK), F32)
            return 0

        lax.fori_loop(0, N_BLK, q_block, 0)
        finish_dwout()

    n_ch = len(SB_CHAINS)
    vmem, hbm = pl.BlockSpec(memory_space=pltpu.VMEM), pl.BlockSpec(memory_space=pl.ANY)
    return pl.pallas_call(
        body, name="sb_bwd",
        in_specs=[vmem] * 5 + [hbm], out_specs=[vmem] * 3 + [hbm],
        out_shape=[jax.ShapeDtypeStruct((T_LOC, 512), BF16)] + [jax.ShapeDtypeStruct((T_LOC, 512), F32)] * 2
        + [jax.ShapeDtypeStruct((N_DEV, OUT_SHARD, D_MODEL), BF16)],
        scratch_shapes=[pltpu.VMEM((n_ch, N_BLK, SLAB, BLK), BF16), pltpu.VMEM((n_ch, SLAB, 2 * BLK), F32),
                        pltpu.SemaphoreType.DMA((7,)), pltpu.SemaphoreType.DMA((7,)), pltpu.SemaphoreType.DMA],
        compiler_params=pltpu.CompilerParams(vmem_limit_bytes=VMEM_LIMIT),
    )(qb, kb, vb, d_ob, carries, dwout)


def bwd_dw(x, norm_gain, dq_rot, dk_dup, dv_dup, qa_raw, ka_raw, cos, sin_s, q_gain2, k_gain2, dga, dgb, dqb, dkb, dvb):
    n_tiles = T_LOC // TM

    def body(x_ref, ng_ref, dq_ref, dk_ref, dv_ref, qa_ref, ka_ref, cos_ref, sin_ref, qg_ref, kg_ref,
             dga_ref, dgb_ref, dqb_ref, dkb_ref, dvb_ref,
             dproj_ref, dw_hbm, dqg_ref, dkg_ref, acc_ref, stage_ref):
        step = pl.program_id(0)

        @pl.when(step == 0)
        def _():
            acc_ref[...] = jnp.zeros_like(acc_ref)
            dqg_ref[...] = jnp.zeros_like(dqg_ref)
            dkg_ref[...] = jnp.zeros_like(dkg_ref)

        lane = _lane((TM, SLAB))
        bd = _head_blockdiag()
        cos, sin_s = cos_ref[...], sin_ref[...]

        def norm_rope_bwd(d_rot, raw, gain2):
            dy = d_rot * cos + _swap_half(d_rot * sin_s, lane)
            r = lax.rsqrt(_head_sum(raw * raw, bd) * (1.0 / HEAD_DIM) + EPS)
            xhat = raw * r
            dgain = jnp.sum(dy * xhat, axis=0, keepdims=True)
            dxh = dy * gain2
            mean = _head_sum(dxh * xhat, bd) * (1.0 / HEAD_DIM)
            return r * (dxh - xhat * mean), dgain

        def fold_dup(d_dup):
            a, b2 = d_dup[:, :SLAB], d_dup[:, SLAB:]
            return jnp.where(lane < HEAD_DIM, a + pltpu.roll(a, HEAD_DIM, 1), b2 + pltpu.roll(b2, HEAD_DIM, 1))

        pieces = []
        dqg = jnp.zeros((1, SLAB), F32)
        for p in range(4):
            sl = slice(p * SLAB, (p + 1) * SLAB)
            d_raw, dg = norm_rope_bwd(dq_ref[:, sl], qa_ref[:, sl], qg_ref[...])
            pieces.append(d_raw.astype(BF16))
            dqg = dqg + dg
        d_raw, dkg = norm_rope_bwd(fold_dup(dk_ref[...]), ka_ref[...], kg_ref[...])
        pieces.append(d_raw.astype(BF16))
        pieces.append(fold_dup(dv_ref[...]).astype(BF16))
        pieces += [dga_ref[...], dqb_ref[...], dkb_ref[...].astype(BF16), dvb_ref[...].astype(BF16),
                   dgb_ref[...]]
        dproj = jnp.concatenate(pieces, axis=1)
        dproj_ref[...] = dproj
        dqg_ref[0:1, :] += dqg + pltpu.roll(dqg, HEAD_DIM, 1)
        dkg_ref[0:1, :] += dkg + pltpu.roll(dkg, HEAD_DIM, 1)

        xv = x_ref[...]
        rstd = lax.rsqrt(jnp.mean(xv * xv, axis=-1, keepdims=True) + EPS)
        h = (xv * rstd * ng_ref[...]).astype(BF16)
        for r0 in range(0, IN_WIDTH, ACC_ROWS):
            acc_ref[r0:r0 + ACC_ROWS, :] += _dot(dproj[:, r0:r0 + ACC_ROWS], h, TN)

        @pl.when(step == n_tiles - 1)
        def _():
            for r0 in range(0, IN_WIDTH, ACC_ROWS):
                stage_ref[...] = acc_ref[r0:r0 + ACC_ROWS, :].astype(BF16)
                pltpu.sync_copy(stage_ref, dw_hbm.at[r0:r0 + ACC_ROWS, :])

    def tile(w):
        return pl.BlockSpec((TM, w), lambda i: (i, 0))

    def whole(a):
        return pl.BlockSpec(a.shape, lambda i: (0, 0))

    const = lambda i: (0, 0)
    return pl.pallas_call(
        body, name="bwd_dw", grid=(n_tiles,),
        in_specs=[tile(D_MODEL), whole(norm_gain),
                  tile(512), tile(256), tile(256), tile(512), tile(128), tile(128), tile(128),
                  whole(q_gain2), whole(k_gain2), tile(512), tile(512), tile(512), tile(512), tile(512)],
        out_specs=[tile(IN_WIDTH), pl.BlockSpec(memory_space=pl.ANY),
                   pl.BlockSpec((8, SLAB), const), pl.BlockSpec((8, SLAB), const)],
        out_shape=[jax.ShapeDtypeStruct((T_LOC, IN_WIDTH), BF16), jax.ShapeDtypeStruct((IN_WIDTH, D_MODEL), BF16),
                   jax.ShapeDtypeStruct((8, SLAB), F32), jax.ShapeDtypeStruct((8, SLAB), F32)],
        scratch_shapes=[pltpu.VMEM((IN_WIDTH, D_MODEL), F32), pltpu.VMEM((ACC_ROWS, D_MODEL), BF16)],
        compiler_params=_params(("arbitrary",)),
    )(x, norm_gain, dq_rot, dk_dup, dv_dup, qa_raw, ka_raw, cos, sin_s, q_gain2, k_gain2, dga, dgb, dqb, dkb, dvb)


def bwd_dx(x, dout, norm_gain, win_t, dproj, dwin_t, dqg, dkg, dsink, loss_part):
    n_tiles = T_LOC // TM
    rows_per = IN_SHARD
    step_sums, step_merge = 1, 3

    def body(x_ref, dout_ref, ng_ref, w_hbm, dp_ref, a_hbm, dqg_ref, dkg_ref, dsink_ref, loss_ref,
             gx_ref, ra_hbm, rs_hbm, w_ref, dng_ref, s_ref, own_ref, sib_ref, snd_ref, extra_ref,
             w_sem, d2d_send, d2d_recv, ici_send, ici_recv, own_sems, s_send, s_recv, out_sem):
        step = pl.program_id(0)
        x, y, c = _mesh_pos()
        me, sibling = (x, y, c), (x, y, 1 - c)
        chips = {"own": (x, y), "x": (1 - x, y), "y": (x, 1 - y), "d": (1 - x, 1 - y)}
        index = {"own": 0, "x": 1, "y": 2, "d": 3}
        order = ("d", "x", "y", "own")

        def rows(pos):
            return a_hbm.at[pl.ds(_lin(pos) * rows_per, rows_per), :]

        def to_sibling(k):
            return pltpu.make_async_remote_copy(
                src_ref=rows((*chips[k], 1 - c)), dst_ref=sib_ref.at[index[k]],
                send_sem=d2d_send.at[index[k]], recv_sem=d2d_recv.at[index[k]], device_id=sibling, device_id_type=MESH)

        def mine(k):
            return pltpu.make_async_copy(rows((*chips[k], c)), own_ref.at[index[k]], own_sems.at[index[k]])

        def ici(n, to_chip, dst):
            return pltpu.make_async_remote_copy(
                src_ref=snd_ref.at[n], dst_ref=dst, send_sem=ici_send.at[n], recv_sem=ici_recv.at[n],
                device_id=(*chips[to_chip], c), device_id_type=MESH)

        def chip_sum(k):
            to_sibling(k).wait_recv()
            mine(k).wait()
            return own_ref[index[k]].astype(F32) + sib_ref[index[k]].astype(F32)

        def by_core(fn):
            pl.when(c == 0)(lambda: fn("x", "y"))
            pl.when(c == 1)(lambda: fn("y", "x"))

        @pl.when(step == 0)
        def _():
            cp = pltpu.make_async_copy(w_hbm, w_ref, w_sem)
            cp.start()
            for k in order:
                to_sibling(k).start()
                mine(k).start()
            dng_ref[...] = jnp.zeros_like(dng_ref)
            cp.wait()

        @pl.when(step == step_sums)
        def _():
            def first_sends(direct, via):
                snd_ref[0] = chip_sum("d").astype(BF16)
                ici(0, direct, extra_ref).start()
                snd_ref[1] = chip_sum(direct).astype(BF16)
                ici(1, direct, ra_hbm.at[index[direct]]).start()
            by_core(first_sends)

        @pl.when(step == step_merge)
        def _():
            def merge(direct, via):
                merged = chip_sum(via)
                ici(0, direct, extra_ref).wait_recv()
                snd_ref[2] = (merged + extra_ref[...].astype(F32)).astype(BF16)
                ici(2, via, ra_hbm.at[index[via]]).start()
                own_ref[0] = chip_sum("own").astype(BF16)
                pltpu.make_async_copy(own_ref.at[0], ra_hbm.at[0], out_sem).start()
            by_core(merge)

        xv = x_ref[...]
        rstd = lax.rsqrt(jnp.mean(xv * xv, axis=-1, keepdims=True) + EPS)
        xhat = xv * rstd
        gain = ng_ref[...]
        dh = _dot(dp_ref[...], w_ref[...])
        dng_ref[0:1, :] += jnp.sum(dh * xhat, axis=0, keepdims=True)
        dxh = dh * gain
        gx_ref[...] = dout_ref[...] + rstd * (dxh - xhat * jnp.mean(dxh * xhat, axis=-1, keepdims=True))

        @pl.when(step == n_tiles - 1)
        def _():
            s_ref[...] = jnp.concatenate(
                [dng_ref[...], dqg_ref[...], dkg_ref[...], dsink_ref[...], loss_ref[...]], axis=1)
            start_small, finish_small = _direct_exchange(
                lambda dev: s_ref, lambda dev: rs_hbm.at[_lin(dev)], s_send, s_recv, out_sem)

            def finish(direct, via):
                ici(1, direct, ra_hbm.at[index[direct]]).wait_recv()
                ici(2, via, ra_hbm.at[index[via]]).wait_recv()
                for n, to in ((0, direct), (1, direct), (2, via)):
                    ici(n, to, extra_ref).wait_send()
            by_core(finish)
            pltpu.make_async_copy(own_ref.at[0], ra_hbm.at[0], out_sem).wait()
            for k in order:
                to_sibling(k).wait_send()
            start_small()
            finish_small()

    def tile(w):
        return pl.BlockSpec((TM, w), lambda i: (i, 0))

    def whole(a):
        return pl.BlockSpec(a.shape, lambda i: (0, 0))

    hbm = pl.BlockSpec(memory_space=pl.ANY)
    block = (rows_per, D_MODEL)
    return pl.pallas_call(
        body, name="bwd_dx", grid=(n_tiles,),
        in_specs=[tile(D_MODEL), tile(D_MODEL), whole(norm_gain), hbm, tile(IN_WIDTH), hbm,
                  whole(dqg), whole(dkg), whole(dsink), whole(loss_part)],
        out_specs=[tile(D_MODEL), hbm, hbm],
        out_shape=[jax.ShapeDtypeStruct((T_LOC, D_MODEL), F32), jax.ShapeDtypeStruct((3,) + block, BF16),
                   jax.ShapeDtypeStruct((N_DEV, 8, SMALL_W), F32)],
        scratch_shapes=[pltpu.VMEM((IN_WIDTH, D_MODEL), BF16), pltpu.VMEM((8, D_MODEL), F32),
                        pltpu.VMEM((8, SMALL_W), F32),
                        pltpu.VMEM((4,) + block, BF16), pltpu.VMEM((4,) + block, BF16), pltpu.VMEM((3,) + block, BF16),
                        pltpu.VMEM(block, BF16),
                        pltpu.SemaphoreType.DMA, pltpu.SemaphoreType.DMA((4,)), pltpu.SemaphoreType.DMA((4,)),
                        pltpu.SemaphoreType.DMA((3,)), pltpu.SemaphoreType.DMA((3,)), pltpu.SemaphoreType.DMA((4,)),
                        pltpu.SemaphoreType.DMA((7,)), pltpu.SemaphoreType.DMA((7,)), pltpu.SemaphoreType.DMA],
        compiler_params=_params(("arbitrary",)),
    )(x, dout, norm_gain, win_t, dproj, dwin_t, dqg, dkg, dsink, loss_part)


def _adamw(w, g, m, v):
    m = ADAM_B1 * m + (1.0 - ADAM_B1) * g
    v = ADAM_B2 * v + (1.0 - ADAM_B2) * (g * g)
    m_hat = m / (1.0 - ADAM_B1 ** ADAM_STEP)
    v_hat = v / (1.0 - ADAM_B2 ** ADAM_STEP)
    delta = -ADAM_LR * (m_hat / (jnp.sqrt(v_hat) + ADAM_EPS) + ADAM_WD * w)
    return delta, m, v


def _sum_slots(r_ref):
    g = r_ref[0].astype(F32)
    for s in range(1, r_ref.shape[0]):
        g = g + r_ref[s].astype(F32)
    return g


def adamw_all(r_win, r_out, r_small, big_in, big_out, weights, moments_m, moments_v):
    n = len(weights)
    params = [big_in[0], big_out[0], *weights]

    def body(rw_ref, ro_ref, rs_ref, *refs):
        n_p = n + 2
        ins, outs = refs[:3 * n_p], refs[3 * n_p:]
        s = _sum_slots(rs_ref)
        eye = (_row((8, SLAB)) == _lane((8, SLAB))).astype(F32)
        sinks = jnp.sum(s[:, 1280:1408] * eye, axis=0, keepdims=True)
        grads = [_sum_slots(rw_ref), _sum_slots(ro_ref),
                 s[0:1, :D_MODEL], s[0:1, 1024:1024 + HEAD_DIM], s[0:1, 1152:1152 + HEAD_DIM], sinks[:, :8]]
        for k in range(n_p):
            outs[k][...] = grads[k]
            outs[n_p + k][...], outs[2 * n_p + k][...], outs[3 * n_p + k][...] = _adamw(
                ins[k][...], grads[k], ins[n_p + k][...], ins[2 * n_p + k][...])
        loss = jnp.sum(jnp.sum(s[:, 1408:1536], axis=1, keepdims=True), axis=0, keepdims=True) * (0.5 / D_MODEL)
        outs[4 * n_p][...] = loss

    n_p = n + 2
    res = pl.pallas_call(
        body, name="adamw_all",
        out_shape=[jax.ShapeDtypeStruct(p.shape, F32) for p in params] * 4 + [jax.ShapeDtypeStruct((1, 1), F32)],
        compiler_params=pltpu.CompilerParams(vmem_limit_bytes=VMEM_LIMIT),
    )(r_win, r_out, r_small, big_in[0], big_out[0], *weights, big_in[1], big_out[1], *moments_m,
      big_in[2], big_out[2], *moments_v)
    return [res[k * n_p:(k + 1) * n_p] for k in range(4)], res[4 * n_p]


def kernel(x, positions, norm_gain, w_in, q_norm_gain, k_norm_gain, sinks, w_out, loss_target, m_norm_gain, m_w_in, m_q_norm_gain, m_k_norm_gain, m_sinks, m_w_out, v_norm_gain, v_w_in, v_q_norm_gain, v_k_norm_gain, v_sinks, v_w_out):
    x2 = x.reshape(T_LOC, D_MODEL)
    tgt2 = loss_target.reshape(T_LOC, D_MODEL)
    pos2 = positions.reshape(1, T_LOC)
    half = HEAD_DIM // 2
    inv_freq = ROPE_THETA ** (-jnp.arange(half, dtype=F32) * 2.0 / HEAD_DIM)
    inv_freq = jnp.tile(inv_freq, SLAB // half).reshape(SLAB, 1)
    sin_sign = jnp.tile(jnp.concatenate([-jnp.ones((half,), F32), jnp.ones((half,), F32)]), 2).reshape(1, SLAB)
    q_gain2 = jnp.tile(q_norm_gain, (1, 2))
    k_gain2 = jnp.tile(k_norm_gain, (1, 2))

    win_t = gather_weights(w_in.reshape(D_MODEL, IN_SHARD).T)

    (qa_raw, ka_raw, q_rot, k_dup, v_dup, ga, qb, kb, vb, gb, cos, sin_s, wout) = fwd_proj(
        x2, pos2, norm_gain, win_t, inv_freq, sin_sign, q_gain2, k_gain2, w_out.reshape(OUT_SHARD, D_MODEL).astype(BF16))
    o_a = swa_fwd(q_rot, k_dup, v_dup, sinks)
    o_b, carries = sb_fwd(qb, kb, vb)
    dout, d_oa, d_ob, dga, dgb, dwout, loss_part = out_loss(o_a, o_b, ga, gb, x2, tgt2, wout)
    dq_rot, dk_dup, dv_dup, dsink = swa_bwd(q_rot, k_dup, v_dup, o_a, d_oa, sinks)
    dqb, dkb, dvb, r_out = sb_bwd(qb, kb, vb, d_ob, carries, dwout)
    dproj, dwin_t, dqg, dkg = bwd_dw(
        x2, norm_gain, dq_rot, dk_dup, dv_dup, qa_raw, ka_raw, cos, sin_s, q_gain2, k_gain2, dga, dgb, dqb, dkb, dvb)
    grad_x, r_win, r_small = bwd_dx(x2, dout, norm_gain, win_t, dproj, dwin_t, dqg, dkg, dsink, loss_part)

    w_in2, m_in2, v_in2 = (a.reshape(D_MODEL, IN_SHARD).T for a in (w_in, m_w_in, v_w_in))
    w_out2, m_out2, v_out2 = (a.reshape(OUT_SHARD, D_MODEL) for a in (w_out, m_w_out, v_w_out))
    kinds, loss = adamw_all(
        r_win, r_out, r_small, (w_in2, m_in2, v_in2), (w_out2, m_out2, v_out2),
        (norm_gain, q_norm_gain, k_norm_gain, sinks),
        (m_norm_gain, m_q_norm_gain, m_k_norm_gain, m_sinks), (v_norm_gain, v_q_norm_gain, v_k_norm_gain, v_sinks))

    def leaves(k):
        big_in, big_out, ng, qg, kg, sk = kinds[k]
        return (ng, big_in.T.reshape(1, D_MODEL, IN_SHARD), qg, kg, sk, big_out.reshape(1, OUT_SHARD, D_MODEL))

    return (loss.reshape(()), grad_x.reshape(B_LOC, SEQ, D_MODEL), *leaves(0), *leaves(1), *leaves(2), *leaves(3))
```

```python
import functools

import jax
import jax.numpy as jnp
from jax import lax
from jax.experimental import pallas as pl
from jax.experimental.pallas import tpu as pltpu

F32 = jnp.float32
BF16 = jnp.bfloat16

N_DEV = 8
D_MODEL = 1024
SEQ = 2048
B_LOC = 2
T_LOC = B_LOC * SEQ
HEAD_DIM = 64
HEAD_SHIFT = 6
BLK = 128
N_BLK = SEQ // BLK
SLAB = 128
IN_WIDTH = 3328
IN_SHARD = IN_WIDTH // N_DEV
OUT_SHARD = D_MODEL // N_DEV
EPS = 1e-6
ROPE_THETA = 10000.0
Q_SCALE = 0.125
R_QA, R_KA, R_VA, R_GA, R_QB, R_KB, R_VB, R_GB, R_END = 0, 512, 640, 768, 1280, 1792, 2304, 2816, 3328
SMALL_W = 1536
ADAM_LR, ADAM_B1, ADAM_B2, ADAM_EPS, ADAM_WD, ADAM_STEP = 0.001, 0.9, 0.999, 1e-08, 0.01, 10
TM = 512
ACC_ROWS = 256
GATHER_CHUNKS = 2
VMEM_LIMIT = 56 * 1024 * 1024

MESH = pl.DeviceIdType.MESH
NT = (((1,), (1,)), ((), ()))
TN = (((0,), (0,)), ((), ()))


def _params(sem, limit=VMEM_LIMIT):
    return pltpu.CompilerParams(dimension_semantics=sem, vmem_limit_bytes=limit)


def _dot(a, b, dims=None):
    if dims is None:
        return jnp.dot(a, b, preferred_element_type=F32)
    return lax.dot_general(a, b, dims, preferred_element_type=F32)


def _lane(shape):
    return lax.broadcasted_iota(jnp.int32, shape, len(shape) - 1)


def _row(shape):
    return lax.broadcasted_iota(jnp.int32, shape, 0)


def _head_blockdiag():
    return ((_row((SLAB, SLAB)) >> HEAD_SHIFT) == (_lane((SLAB, SLAB)) >> HEAD_SHIFT)).astype(BF16)


def _head_sum(x, bd):
    return _dot(x.astype(BF16), bd)


def _swap_half(y, lane):
    return jnp.where((lane & 32) != 0, pltpu.roll(y, 32, 1), pltpu.roll(y, 96, 1))


def _stack_heads(q, lane):
    zero = jnp.zeros_like(q)
    return jnp.concatenate([jnp.where(lane < HEAD_DIM, q, zero), jnp.where(lane >= HEAD_DIM, q, zero)], axis=0)


def _unstack_heads(x2, lane):
    return jnp.where(lane < HEAD_DIM, x2[:BLK], x2[BLK:])


def _sigmoid(x):
    return 1.0 / (1.0 + jnp.exp(-x))


def _mesh_pos():
    return lax.axis_index("x"), lax.axis_index("y"), lax.axis_index("c")


def _flip(pos, mask):
    return tuple(1 - p if m else p for p, m in zip(pos, mask))


def _lin(pos):
    return 4 * pos[0] + 2 * pos[1] + pos[2]


DEV_FLIPS = [(fx, fy, fc) for fx in (0, 1) for fy in (0, 1) for fc in (0, 1)][1:]


def _direct_exchange(src_for, dst_slot, send_sems, recv_sems, local_sem):
    me = _mesh_pos()

    def copy(k, to):
        return pltpu.make_async_remote_copy(
            src_ref=src_for(to), dst_ref=dst_slot(me), send_sem=send_sems.at[k], recv_sem=recv_sems.at[k],
            device_id=to, device_id_type=MESH)

    def landed(k, frm):
        return pltpu.make_async_remote_copy(
            src_ref=src_for(frm), dst_ref=dst_slot(frm), send_sem=send_sems.at[k], recv_sem=recv_sems.at[k],
            device_id=frm, device_id_type=MESH)

    local = None if local_sem is None else pltpu.make_async_copy(src_for(me), dst_slot(me), local_sem)
    peers = [_flip(me, f) for f in DEV_FLIPS]

    def start():
        if local is not None:
            local.start()
        for k, to in enumerate(peers):
            copy(k, to).start()

    def finish():
        for k, frm in enumerate(peers):
            landed(k, frm).wait_recv()
        for k, to in enumerate(peers):
            copy(k, to).wait_send()
        if local is not None:
            local.wait()

    return start, finish


def gather_weights(shard):
    m = shard.shape[0]
    piece = m // GATHER_CHUNKS
    pieces = range(GATHER_CHUNKS)

    def body(f32_ref, o_ref, a_ref, ici_send, ici_recv, d2d_send, d2d_recv, local_sem):
        a_ref[...] = f32_ref[...].astype(BF16)
        x, y, c = _mesh_pos()
        me, sibling = (x, y, c), (x, y, 1 - c)
        chip_x, chip_y, chip_d = (1 - x, y), (x, 1 - y), (1 - x, 1 - y)

        def rows(pos, q):
            return o_ref.at[pl.ds(_lin(pos) * m + q * piece, piece), :]

        def own(q):
            return a_ref.at[pl.ds(q * piece, piece), :]

        def ici(k, q, block, to, src=None):
            return pltpu.make_async_remote_copy(
                src_ref=rows(block, q) if src is None else src, dst_ref=rows(block, q),
                send_sem=ici_send.at[k, q], recv_sem=ici_recv.at[k, q], device_id=to, device_id_type=MESH)

        def d2d(k, q, chip, mine, src=None):
            block = (*chip, c) if mine else (*chip, 1 - c)
            return pltpu.make_async_remote_copy(
                src_ref=rows(block, q) if src is None else src, dst_ref=rows(block, q),
                send_sem=d2d_send.at[k, q], recv_sem=d2d_recv.at[k, q], device_id=sibling, device_id_type=MESH)

        local = pltpu.make_async_copy(a_ref, o_ref.at[pl.ds(_lin(me) * m, m), :], local_sem)
        local.start()
        sends = []
        for q in pieces:
            sends += [ici(0, q, me, (*chip_x, c), src=own(q)), ici(1, q, me, (*chip_y, c), src=own(q)),
                      d2d(0, q, (x, y), True, src=own(q))]
        for cp in sends:
            cp.start()

        def pass_on(first, k_first, second, k_second, onward):
            moved = []
            for q in pieces:
                ici(k_first, q, (*first, c), me).wait_recv()
                moved += [ici(2, q, (*first, c), (*onward, c)), d2d(1 + k_first, q, first, True)]
                for cp in moved[-2:]:
                    cp.start()
            for q in pieces:
                ici(k_second, q, (*second, c), me).wait_recv()
                moved.append(d2d(1 + k_second, q, second, True))
                moved[-1].start()
            for q in pieces:
                ici(2, q, (*chip_d, c), me).wait_recv()
                moved.append(d2d(3, q, chip_d, True))
                moved[-1].start()
            for cp in moved:
                cp.wait_send()

        @pl.when(c == 0)
        def _():
            pass_on(chip_y, 1, chip_x, 0, chip_x)

        @pl.when(c == 1)
        def _():
            pass_on(chip_x, 0, chip_y, 1, chip_y)

        for k, chip in enumerate([(x, y), chip_x, chip_y, chip_d]):
            for q in pieces:
                d2d(k, q, chip, False).wait_recv()
        for cp in sends:
            cp.wait_send()
        local.wait()

    vmem = pl.BlockSpec(memory_space=pltpu.VMEM)
    n_q = GATHER_CHUNKS
    return pl.pallas_call(
        body, name="gather_weights",
        out_shape=jax.ShapeDtypeStruct((N_DEV * m, shard.shape[1]), BF16),
        in_specs=[vmem], out_specs=vmem,
        scratch_shapes=[pltpu.VMEM(shard.shape, BF16), pltpu.SemaphoreType.DMA((3, n_q)), pltpu.SemaphoreType.DMA((3, n_q)),
                        pltpu.SemaphoreType.DMA((4, n_q)), pltpu.SemaphoreType.DMA((4, n_q)), pltpu.SemaphoreType.DMA],
        compiler_params=pltpu.CompilerParams(vmem_limit_bytes=VMEM_LIMIT),
    )(shard)


def _norm_rope(xs, gain2, cos, sin_s, bd, lane):
    r = lax.rsqrt(_head_sum(xs * xs, bd) * (1.0 / HEAD_DIM) + EPS)
    y = xs * r * gain2
    return y * cos + _swap_half(y, lane) * sin_s


def _dup_heads(xs, lane):
    r = pltpu.roll(xs, HEAD_DIM, 1)
    lo = lane < HEAD_DIM
    return jnp.concatenate([jnp.where(lo, xs, r), jnp.where(lo, r, xs)], axis=1)


def fwd_proj(x, pos, norm_gain, win_t, inv_freq, sin_sign, q_gain2, k_gain2, wout_shard):
    n_tiles = T_LOC // TM

    def body(x_ref, pos_ref, ng_ref, w_ref, if_ref, sg_ref, qg_ref, kg_ref, ws_hbm,
             qa_raw_ref, ka_raw_ref, q_rot_ref, k_dup_ref, v_dup_ref, ga_ref, qb_ref, kb_ref, vb_ref, gb_ref,
             cos_ref, sin_ref, wo_hbm, wo_send, wo_recv, wo_local):
        start_wout, finish_wout = _direct_exchange(
            lambda dev: ws_hbm, lambda dev: wo_hbm.at[pl.ds(_lin(dev) * OUT_SHARD, OUT_SHARD), :],
            wo_send, wo_recv, wo_local)
        pl.when(pl.program_id(0) == 0)(start_wout)

        xv = x_ref[...]
        rstd = lax.rsqrt(jnp.mean(xv * xv, axis=-1, keepdims=True) + EPS)
        h = (xv * rstd * ng_ref[...]).astype(BF16)

        def proj(r0, r1):
            return _dot(h, w_ref[r0:r1, :], NT)

        ang_t = if_ref[...] * pos_ref[...].astype(F32)
        cos = jnp.cos(ang_t).T
        sin_s = jnp.sin(ang_t).T * sg_ref[...]
        cos_ref[...] = cos
        sin_ref[...] = sin_s
        lane = _lane((TM, SLAB))
        bd = _head_blockdiag()

        qa = proj(R_QA, R_KA)
        qa_raw_ref[...] = qa
        for p in range(4):
            sl = slice(p * SLAB, (p + 1) * SLAB)
            q_rot_ref[:, sl] = (_norm_rope(qa[:, sl], qg_ref[...], cos, sin_s, bd, lane) * Q_SCALE).astype(BF16)
        ka = proj(R_KA, R_VA)
        ka_raw_ref[...] = ka
        k_dup_ref[...] = _dup_heads(_norm_rope(ka, kg_ref[...], cos, sin_s, bd, lane), lane).astype(BF16)
        v_dup_ref[...] = _dup_heads(proj(R_VA, R_GA), lane).astype(BF16)
        ga_ref[...] = proj(R_GA, R_QB).astype(BF16)
        qb_ref[...] = (proj(R_QB, R_KB) * Q_SCALE).astype(BF16)
        kb_ref[...] = proj(R_KB, R_VB).astype(BF16)
        vb_ref[...] = proj(R_VB, R_GB).astype(BF16)
        gb_ref[...] = proj(R_GB, R_END).astype(BF16)
        pl.when(pl.program_id(0) == n_tiles - 1)(finish_wout)

    def tile(w):
        return pl.BlockSpec((TM, w), lambda i: (i, 0))

    def whole(a):
        return pl.BlockSpec(a.shape, lambda i: (0, 0))

    hbm = pl.BlockSpec(memory_space=pl.ANY)
    widths = [(512, F32), (128, F32), (512, BF16), (256, BF16), (256, BF16), (512, BF16), (512, BF16), (512, BF16),
              (512, BF16), (512, BF16), (128, F32), (128, F32)]
    return pl.pallas_call(
        body, name="fwd_proj", grid=(n_tiles,),
        in_specs=[tile(D_MODEL), pl.BlockSpec((1, TM), lambda i: (0, i)), whole(norm_gain), whole(win_t),
                  whole(inv_freq), whole(sin_sign),
                  whole(q_gain2), whole(k_gain2), hbm],
        out_specs=[tile(w) for w, _ in widths] + [hbm],
        out_shape=[jax.ShapeDtypeStruct((T_LOC, w), dt) for w, dt in widths]
        + [jax.ShapeDtypeStruct((D_MODEL, D_MODEL), BF16)],
        scratch_shapes=[pltpu.SemaphoreType.DMA((7,)), pltpu.SemaphoreType.DMA((7,)), pltpu.SemaphoreType.DMA],
        compiler_params=_params(("arbitrary",)),
    )(x, pos, norm_gain, win_t, inv_freq, sin_sign, q_gain2, k_gain2, wout_shard)


def _swa_window(prev_ref, cur_ref, p):
    gsl = _slab(p // 2)
    return jnp.concatenate([prev_ref[:, gsl], cur_ref[:, gsl]], axis=0)


def _swa_probs(s, sinks_ref, p, i):
    shape = (2 * BLK, 2 * BLK)
    r = _row(shape) & (BLK - 1)
    cidx = _lane(shape)
    valid = (cidx > r) & (cidx <= r + BLK) & ((cidx >= BLK) | (i > 0))
    s = jnp.where(valid, s, -jnp.inf)
    sink = jnp.where(_row((2 * BLK, 1)) < BLK, sinks_ref[0, 2 * p], sinks_ref[0, 2 * p + 1])
    m = jnp.maximum(jnp.max(s, axis=-1, keepdims=True), sink)
    e = jnp.exp(s - m)
    e_sink = jnp.exp(sink - m)
    den = jnp.sum(e, axis=-1, keepdims=True) + e_sink
    return e / den, e_sink / den


SWA_CHAINS = [(b, p) for b in range(B_LOC) for p in range(4)]


def _swa_specs():
    def cur(w):
        return pl.BlockSpec((B_LOC, BLK, w), lambda i: (0, i, 0))

    def prev(w):
        return pl.BlockSpec((B_LOC, BLK, w), lambda i: (0, jnp.maximum(i - 1, 0), 0))

    return cur, prev


def swa_fwd(q_rot, k_dup, v_dup, sinks):
    def body(q_ref, kp_ref, kc_ref, vp_ref, vc_ref, sinks_ref, o_ref):
        i = pl.program_id(0)
        lane = _lane((BLK, SLAB))
        s = [_dot(_stack_heads(q_ref[b, :, _slab(p)], lane), _swa_window(kp_ref.at[b], kc_ref.at[b], p), NT)
             for b, p in SWA_CHAINS]
        pn = [_swa_probs(s[c], sinks_ref, p, i)[0].astype(BF16) for c, (b, p) in enumerate(SWA_CHAINS)]
        for c, (b, p) in enumerate(SWA_CHAINS):
            o = _unstack_heads(_dot(pn[c], _swa_window(vp_ref.at[b], vc_ref.at[b], p)), lane)
            o_ref[b, :, _slab(p)] = o.astype(BF16)

    cur, prev = _swa_specs()
    q3, k3, v3 = (a.reshape(B_LOC, SEQ, a.shape[1]) for a in (q_rot, k_dup, v_dup))
    return pl.pallas_call(
        body, name="swa_fwd", grid=(N_BLK,),
        in_specs=[cur(512), prev(256), cur(256), prev(256), cur(256), pl.BlockSpec(memory_space=pltpu.SMEM)],
        out_specs=cur(512),
        out_shape=jax.ShapeDtypeStruct((B_LOC, SEQ, 512), BF16),
        compiler_params=_params(("arbitrary",)),
    )(q3, k3, k3, v3, v3, sinks).reshape(T_LOC, 512)


def _tri(suffix):
    r, cidx = _row((BLK + 16, BLK)), _lane((BLK + 16, BLK))
    tri = (cidx > r) if suffix else (cidx < r)
    return (tri | (r >= BLK)).astype(BF16)


def _key_sums(tri, x):
    res = _dot(tri, x.astype(BF16))
    return res[:BLK], res[BLK:BLK + 1]


def _sb_softplus(zt, valid):
    neg_abs = lax.bitcast_convert_type(lax.bitcast_convert_type(zt, jnp.uint32) | jnp.uint32(0x80000000), F32)
    sp = jnp.maximum(zt, 0.0) + jnp.log(1.0 + jnp.exp(neg_abs))
    return sp if valid is None else jnp.where(valid, sp, 0.0)


def _sb_weights(zt, sp, later, valid):
    w = jnp.exp(zt - sp - later)
    return w if valid is None else jnp.where(valid, w, 0.0)


def _slab(pp):
    return slice(pp * SLAB, (pp + 1) * SLAB)


def _blk(j):
    return pl.ds(pl.multiple_of(j * BLK, BLK), BLK)


def _causal_t():
    return _row((BLK, 2 * BLK)) < (_lane((BLK, 2 * BLK)) & (BLK - 1))


def _sb_rows(b, j):
    return pl.ds(pl.multiple_of(b * SEQ + j * BLK, BLK), BLK)


SB_CHAINS = [(b, pp) for b in range(B_LOC) for pp in range(4)]


def sb_fwd(qb, kb, vb):
    def body(q_ref, k_ref, v_ref, o_ref, c_ref, vt_ref, ot_ref):
        for c, (b, pp) in enumerate(SB_CHAINS):
            for j in range(N_BLK):
                vt_ref[c, j] = v_ref[b * SEQ + j * BLK:b * SEQ + (j + 1) * BLK, _slab(pp)].T
        lane = _lane((BLK, SLAB))
        tri = _tri(True)
        valid = _causal_t()
        jrow = _row((N_BLK, 2 * BLK))
        chains = range(len(SB_CHAINS))

        def q_block(i, _):
            q2 = [_stack_heads(q_ref[_sb_rows(b, i), _slab(pp)], lane) for b, pp in SB_CHAINS]

            def key_block(j, carry, mask, first):
                zt = [_dot(k_ref[_sb_rows(b, j), _slab(pp)], q2[c], NT) for c, (b, pp) in enumerate(SB_CHAINS)]
                sp = [_sb_softplus(zt[c], mask) for c in chains]
                sums = [_key_sums(tri, sp[c]) for c in chains]
                w = [_sb_weights(zt[c], sp[c], sums[c][0] + carry[c], mask) for c in chains]
                for c in chains:
                    pv = _dot(vt_ref[c, j], w[c].astype(BF16))
                    if first:
                        ot_ref[c] = pv
                    else:
                        ot_ref[c] += pv
                return tuple(carry[c] + sums[c][1] for c in chains)

            def earlier(jj, state):
                carry, saved = state
                j = i - 1 - jj
                saved = tuple(jnp.where(jrow == j, carry[c], saved[c]) for c in chains)
                return key_block(j, carry, None, False), saved

            zero = tuple(jnp.zeros((1, 2 * BLK), F32) for _ in chains)
            carry = key_block(i, zero, valid, True)
            _, saved = lax.fori_loop(0, i, earlier, (carry, tuple(jnp.zeros((N_BLK, 2 * BLK), F32) for _ in chains)))
            for c, (b, pp) in enumerate(SB_CHAINS):
                o_ref[_sb_rows(b, i), _slab(pp)] = _unstack_heads(ot_ref[c].T, lane).astype(BF16)
                c_ref[c * N_BLK + i] = saved[c]
            return 0

        lax.fori_loop(0, N_BLK, q_block, 0)

    n_ch = len(SB_CHAINS)
    vmem = pl.BlockSpec(memory_space=pltpu.VMEM)
    return pl.pallas_call(
        body, name="sb_fwd",
        in_specs=[vmem] * 3, out_specs=[vmem] * 2,
        out_shape=[jax.ShapeDtypeStruct((T_LOC, 512), BF16), jax.ShapeDtypeStruct((n_ch * N_BLK, N_BLK, 2 * BLK), F32)],
        scratch_shapes=[pltpu.VMEM((n_ch, N_BLK, SLAB, BLK), BF16), pltpu.VMEM((n_ch, SLAB, 2 * BLK), F32)],
        compiler_params=pltpu.CompilerParams(vmem_limit_bytes=VMEM_LIMIT),
    )(qb, kb, vb)


def out_loss(o_a, o_b, ga, gb, x, target, wout):
    n_tiles = T_LOC // TM

    def body(oa_ref, ob_ref, ga_ref, gb_ref, x_ref, t_ref, w_ref,
             dout_ref, doa_ref, dob_ref, dga_ref, dgb_ref, dw_ref, loss_ref, acc_ref):
        step = pl.program_id(0)

        @pl.when(step == 0)
        def _():
            acc_ref[...] = jnp.zeros_like(acc_ref)
            loss_ref[...] = jnp.zeros_like(loss_ref)

        oa, ob, gav, gbv = (r[...].astype(F32) for r in (oa_ref, ob_ref, ga_ref, gb_ref))
        sa, sb = _sigmoid(gav), _sigmoid(gbv)
        silu_a, silu_b = gav * sa, gbv * sb
        y = jnp.concatenate([oa * silu_a, ob * silu_b], axis=1).astype(BF16)
        err = x_ref[...] + _dot(y, w_ref[...]) - t_ref[...]
        e2 = err * err
        part = jnp.sum(e2.reshape(TM // 8, 8, D_MODEL), axis=0)
        loss_ref[...] += functools.reduce(lambda a, b: a + b, [part[:, k * 128:(k + 1) * 128] for k in range(8)])
        dout = err * (1.0 / D_MODEL)
        dout_ref[...] = dout
        dob16 = dout.astype(BF16)
        for r0 in range(0, D_MODEL, ACC_ROWS):
            acc_ref[r0:r0 + ACC_ROWS, :] += _dot(y[:, r0:r0 + ACC_ROWS], dob16, TN)
        dy = _dot(dob16, w_ref[...], NT)
        dya, dyb = dy[:, :512], dy[:, 512:]
        doa_ref[...] = (dya * silu_a).astype(BF16)
        dob_ref[...] = (dyb * silu_b).astype(BF16)
        dga_ref[...] = (dya * oa * (sa * (1.0 + gav * (1.0 - sa)))).astype(BF16)
        dgb_ref[...] = (dyb * ob * (sb * (1.0 + gbv * (1.0 - sb)))).astype(BF16)

        @pl.when(step == n_tiles - 1)
        def _():
            dw_ref[...] = acc_ref[...].astype(BF16)

    def tile(w):
        return pl.BlockSpec((TM, w), lambda i: (i, 0))

    const = lambda i: (0, 0)
    return pl.pallas_call(
        body, name="out_loss", grid=(n_tiles,),
        in_specs=[tile(512)] * 4 + [tile(D_MODEL)] * 2 + [pl.BlockSpec((D_MODEL, D_MODEL), const)],
        out_specs=[tile(D_MODEL), tile(512), tile(512), tile(512), tile(512),
                   pl.BlockSpec((D_MODEL, D_MODEL), const), pl.BlockSpec((8, 128), const)],
        out_shape=[jax.ShapeDtypeStruct((T_LOC, D_MODEL), F32)] + [jax.ShapeDtypeStruct((T_LOC, 512), BF16)] * 4
        + [jax.ShapeDtypeStruct((D_MODEL, D_MODEL), BF16), jax.ShapeDtypeStruct((8, 128), F32)],
        scratch_shapes=[pltpu.VMEM((D_MODEL, D_MODEL), F32)],
        compiler_params=_params(("arbitrary",)),
    )(o_a, o_b, ga, gb, x, target, wout)


def swa_bwd(q_rot, k_dup, v_dup, o_a, d_oa, sinks):
    def body(q_ref, kp_ref, kc_ref, vp_ref, vc_ref, o_ref, do_ref, sinks_ref, dq_ref, dk_ref, dv_ref, dsink_ref):
        i = pl.program_id(0)

        @pl.when(i == 0)
        def _():
            dk_ref[...] = jnp.zeros_like(dk_ref)
            dv_ref[...] = jnp.zeros_like(dv_ref)
            dsink_ref[...] = jnp.zeros_like(dsink_ref)

        lane = _lane((BLK, SLAB))
        rows_prev, rows_cur = _blk(jnp.maximum(i - 1, 0)), _blk(i)
        chains = range(len(SWA_CHAINS))
        q2 = [_stack_heads(q_ref[b, :, _slab(p)], lane) for b, p in SWA_CHAINS]
        do2 = [_stack_heads(do_ref[b, :, _slab(p)], lane) for b, p in SWA_CHAINS]
        keys = [_swa_window(kp_ref.at[b], kc_ref.at[b], p) for b, p in SWA_CHAINS]
        s = [_dot(q2[c], keys[c], NT) for c in chains]
        dp = [_dot(do2[c], _swa_window(vp_ref.at[b], vc_ref.at[b], p), NT) for c, (b, p) in enumerate(SWA_CHAINS)]
        ds, pn16, cols = [], [], []
        for c, (b, p) in enumerate(SWA_CHAINS):
            pn, p_sink = _swa_probs(s[c], sinks_ref, p, i)
            o = o_ref[b, :, _slab(p)].astype(F32)
            delta =jnp.sum(do2[c].astype(F32) * jnp.concatenate([o, o], axis=0), axis=-1, keepdims=True)
            ds.append((pn * (dp[c] - delta)).astype(BF16))
            pn16.append(pn.astype(BF16))
            cols.append(-p_sink * delta)
        for c, (b, p) in enumerate(SWA_CHAINS):
            dq_ref[b, :, _slab(p)] = _unstack_heads(_dot(ds[c], keys[c]), lane) * Q_SCALE
        dk2 = [_dot(ds[c], q2[c], TN) for c in chains]
        dv2 = [_dot(pn16[c], do2[c], TN) for c in chains]
        for c, (b, p) in enumerate(SWA_CHAINS):
            gsl = _slab(p // 2)
            dk_ref[b, rows_prev, gsl] += dk2[c][:BLK]
            dk_ref[b, rows_cur, gsl] += dk2[c][BLK:]
            dv_ref[b, rows_prev, gsl] += dv2[c][:BLK]
            dv_ref[b, rows_cur, gsl] += dv2[c][BLK:]
            for e in range(2):
                dsink_ref[2 * p + e:2 * p + e + 1, :] += jnp.sum(cols[c][e * BLK:(e + 1) * BLK], axis=0, keepdims=True)

    cur, prev = _swa_specs()
    whole = pl.BlockSpec((B_LOC, SEQ, 256), lambda i: (0, 0, 0))
    q3, k3, v3, o3, do3 = (a.reshape(B_LOC, SEQ, a.shape[1]) for a in (q_rot, k_dup, v_dup, o_a, d_oa))
    dq, dk, dv, dsink = pl.pallas_call(
        body, name="swa_bwd", grid=(N_BLK,),
        in_specs=[cur(512), prev(256), cur(256), prev(256), cur(256), cur(512), cur(512),
                  pl.BlockSpec(memory_space=pltpu.SMEM)],
        out_specs=[cur(512), whole, whole, pl.BlockSpec((8, 128), lambda i: (0, 0))],
        out_shape=[jax.ShapeDtypeStruct((B_LOC, SEQ, 512), F32), jax.ShapeDtypeStruct((B_LOC, SEQ, 256), F32),
                   jax.ShapeDtypeStruct((B_LOC, SEQ, 256), F32), jax.ShapeDtypeStruct((8, 128), F32)],
        compiler_params=_params(("arbitrary",)),
    )(q3, k3, k3, v3, v3, o3, do3, sinks)
    return dq.reshape(T_LOC, 512), dk.reshape(T_LOC, 256), dv.reshape(T_LOC, 256), dsink


def sb_bwd(qb, kb, vb, d_ob, carries, dwout):
    def body(q_ref, k_ref, v_ref, do_ref, c_ref, dw_hbm, dq_ref, dk_ref, dv_ref, rw_hbm, kt_ref, dqt_ref,
             rw_send, rw_recv, rw_local):
        start_dwout, finish_dwout = _direct_exchange(
            lambda dev: dw_hbm.at[pl.ds(_lin(dev) * OUT_SHARD, OUT_SHARD), :], lambda dev: rw_hbm.at[_lin(dev)],
            rw_send, rw_recv, rw_local)
        start_dwout()
        for c, (b, pp) in enumerate(SB_CHAINS):
            for j in range(N_BLK):
                kt_ref[c, j] = k_ref[b * SEQ + j * BLK:b * SEQ + (j + 1) * BLK, _slab(pp)].T
        dk_ref[...] = jnp.zeros_like(dk_ref)
        dv_ref[...] = jnp.zeros_like(dv_ref)
        dqt_ref[...] = jnp.zeros_like(dqt_ref)
        lane = _lane((BLK, SLAB))
        tri_after, tri_before = _tri(True), _tri(False)
        valid = _causal_t()
        jrow = _row((N_BLK, 2 * BLK))
        chains = range(len(SB_CHAINS))

        def q_block(i, _):
            q2 = [_stack_heads(q_ref[_sb_rows(b, i), _slab(pp)], lane) for b, pp in SB_CHAINS]
            do2 = [_stack_heads(do_ref[_sb_rows(b, i), _slab(pp)], lane) for b, pp in SB_CHAINS]

            def key_block(j, carry_sp, before_u, mask):
                at = [(_sb_rows(b, j), _slab(pp)) for b, pp in SB_CHAINS]
                zt = [_dot(k_ref[at[c]], q2[c], NT) for c in chains]
                dw = [_dot(v_ref[at[c]], do2[c], NT) for c in chains]
                sp = [_sb_softplus(zt[c], mask) for c in chains]
                later = [_key_sums(tri_after, sp[c])[0] for c in chains]
                w = [_sb_weights(zt[c], sp[c], later[c] + carry_sp[c], mask) for c in chains]
                u = [dw[c] * w[c] for c in chains]
                for c in chains:
                    dv_ref[at[c]] += _dot(w[c].astype(BF16), do2[c])
                sums = [_key_sums(tri_before, u[c]) for c in chains]
                dz16 = []
                for c in chains:
                    sig = jnp.exp(zt[c] - sp[c])
                    dz = u[c] - sig * (u[c] + before_u[c] + sums[c][0])
                    if mask is not None:
                        dz = jnp.where(mask, dz, 0.0)
                    dz16.append(dz.astype(BF16))
                for c in chains:
                    dk_ref[at[c]] += _dot(dz16[c], q2[c])
                    dqt_ref[c] += _dot(kt_ref[c, j], dz16[c])
                return tuple(before_u[c] + sums[c][1] for c in chains)

            def earlier(j, before_u):
                carry_sp = [jnp.sum(jnp.where(jrow == j, c_ref[c * N_BLK + i], 0.0), axis=0, keepdims=True)
                            for c in chains]
                return key_block(j, carry_sp, before_u, None)

            zero = tuple(jnp.zeros((1, 2 * BLK), F32) for _ in chains)
            before_u = lax.fori_loop(0, i, earlier, zero)
            key_block(i, zero, before_u, valid)
            for c, (b, pp) in enumerate(SB_CHAINS):
                dq_ref[_sb_rows(b, i), _slab(pp)] = (_unstack_heads(dqt_ref[c].T, lane) * Q_SCALE).astype(BF16)
                dqt_ref[c] = jnp.zeros((SLAB, 2 * BLK), F32)
            return 0

        lax.fori_loop(0, N_BLK, q_block, 0)
        finish_dwout()

    n_ch = len(SB_CHAINS)
    vmem, hbm = pl.BlockSpec(memory_space=pltpu.VMEM), pl.BlockSpec(memory_space=pl.ANY)
    return pl.pallas_call(
        body, name="sb_bwd",
        in_specs=[vmem] * 5 + [hbm], out_specs=[vmem] * 3 + [hbm],
        out_shape=[jax.ShapeDtypeStruct((T_LOC, 512), BF16)] + [jax.ShapeDtypeStruct((T_LOC, 512), F32)] * 2
        + [jax.ShapeDtypeStruct((N_DEV, OUT_SHARD, D_MODEL), BF16)],
        scratch_shapes=[pltpu.VMEM((n_ch, N_BLK, SLAB, BLK), BF16), pltpu.VMEM((n_ch, SLAB, 2 * BLK), F32),
                        pltpu.SemaphoreType.DMA((7,)), pltpu.SemaphoreType.DMA((7,)), pltpu.SemaphoreType.DMA],
        compiler_params=pltpu.CompilerParams(vmem_limit_bytes=VMEM_LIMIT),
    )(qb, kb, vb, d_ob, carries, dwout)


def bwd_dw(x, norm_gain, dq_rot, dk_dup, dv_dup, qa_raw, ka_raw, cos, sin_s, q_gain2, k_gain2, dga, dgb, dqb, dkb, dvb):
    n_tiles = T_LOC // TM

    def body(x_ref, ng_ref, dq_ref, dk_ref, dv_ref, qa_ref, ka_ref, cos_ref, sin_ref, qg_ref, kg_ref,
             dga_ref, dgb_ref, dqb_ref, dkb_ref, dvb_ref,
             dproj_ref, dw_hbm, dqg_ref, dkg_ref, acc_ref, stage_ref):
        step = pl.program_id(0)

        @pl.when(step == 0)
        def _():
            acc_ref[...] = jnp.zeros_like(acc_ref)
            dqg_ref[...] = jnp.zeros_like(dqg_ref)
            dkg_ref[...] = jnp.zeros_like(dkg_ref)

        lane = _lane((TM, SLAB))
        bd = _head_blockdiag()
        cos, sin_s = cos_ref[...], sin_ref[...]

        def norm_rope_bwd(d_rot, raw, gain2):
            dy = d_rot * cos + _swap_half(d_rot * sin_s, lane)
            r = lax.rsqrt(_head_sum(raw * raw, bd) * (1.0 / HEAD_DIM) + EPS)
            xhat = raw * r
            dgain = jnp.sum(dy * xhat, axis=0, keepdims=True)
            dxh = dy * gain2
            mean = _head_sum(dxh * xhat, bd) * (1.0 / HEAD_DIM)
            return r * (dxh - xhat * mean), dgain

        def fold_dup(d_dup):
            a, b2 = d_dup[:, :SLAB], d_dup[:, SLAB:]
            return jnp.where(lane < HEAD_DIM, a + pltpu.roll(a, HEAD_DIM, 1), b2 + pltpu.roll(b2, HEAD_DIM, 1))

        pieces = []
        dqg = jnp.zeros((1, SLAB), F32)
        for p in range(4):
            sl = slice(p * SLAB, (p + 1) * SLAB)
            d_raw, dg = norm_rope_bwd(dq_ref[:, sl], qa_ref[:, sl], qg_ref[...])
            pieces.append(d_raw.astype(BF16))
            dqg = dqg + dg
        d_raw, dkg = norm_rope_bwd(fold_dup(dk_ref[...]), ka_ref[...], kg_ref[...])
        pieces.append(d_raw.astype(BF16))
        pieces.append(fold_dup(dv_ref[...]).astype(BF16))
        pieces += [dga_ref[...], dqb_ref[...], dkb_ref[...].astype(BF16), dvb_ref[...].astype(BF16),
                   dgb_ref[...]]
        dproj = jnp.concatenate(pieces, axis=1)
        dproj_ref[...] = dproj
        dqg_ref[0:1, :] += dqg + pltpu.roll(dqg, HEAD_DIM, 1)
        dkg_ref[0:1, :] += dkg + pltpu.roll(dkg, HEAD_DIM, 1)

        xv = x_ref[...]
        rstd = lax.rsqrt(jnp.mean(xv * xv, axis=-1, keepdims=True) + EPS)
        h = (xv * rstd * ng_ref[...]).astype(BF16)
        for r0 in range(0, IN_WIDTH, ACC_ROWS):
            acc_ref[r0:r0 + ACC_ROWS, :] += _dot(dproj[:, r0:r0 + ACC_ROWS], h, TN)

        @pl.when(step == n_tiles - 1)
        def _():
            for r0 in range(0, IN_WIDTH, ACC_ROWS):
                stage_ref[...] = acc_ref[r0:r0 + ACC_ROWS, :].astype(BF16)
                pltpu.sync_copy(stage_ref, dw_hbm.at[r0:r0 + ACC_ROWS, :])

    def tile(w):
        return pl.BlockSpec((TM, w), lambda i: (i, 0))

    def whole(a):
        return pl.BlockSpec(a.shape, lambda i: (0, 0))

    const = lambda i: (0, 0)
    return pl.pallas_call(
        body, name="bwd_dw", grid=(n_tiles,),
        in_specs=[tile(D_MODEL), whole(norm_gain),
                  tile(512), tile(256), tile(256), tile(512), tile(128), tile(128), tile(128),
                  whole(q_gain2), whole(k_gain2), tile(512), tile(512), tile(512), tile(512), tile(512)],
        out_specs=[tile(IN_WIDTH), pl.BlockSpec(memory_space=pl.ANY),
                   pl.BlockSpec((8, SLAB), const), pl.BlockSpec((8, SLAB), const)],
        out_shape=[jax.ShapeDtypeStruct((T_LOC, IN_WIDTH), BF16), jax.ShapeDtypeStruct((IN_WIDTH, D_MODEL), BF16),
                   jax.ShapeDtypeStruct((8, SLAB), F32), jax.ShapeDtypeStruct((8, SLAB), F32)],
        scratch_shapes=[pltpu.VMEM((IN_WIDTH, D_MODEL), F32), pltpu.VMEM((ACC_ROWS, D_MODEL), BF16)],
        compiler_params=_params(("arbitrary",)),
    )(x, norm_gain, dq_rot, dk_dup, dv_dup, qa_raw, ka_raw, cos, sin_s, q_gain2, k_gain2, dga, dgb, dqb, dkb, dvb)


def bwd_dx(x, dout, norm_gain, win_t, dproj, dwin_t, dqg, dkg, dsink, loss_part):
    n_tiles = T_LOC // TM
    rows_per = IN_SHARD
    step_sums, step_merge = 1, 3

    def body(x_ref, dout_ref, ng_ref, w_hbm, dp_ref, a_hbm, dqg_ref, dkg_ref, dsink_ref, loss_ref,
             gx_ref, ra_hbm, rs_hbm, w_ref, dng_ref, s_ref, own_ref, sib_ref, snd_ref, extra_ref,
             w_sem, d2d_send, d2d_recv, ici_send, ici_recv, own_sems, s_send, s_recv, out_sem):
        step = pl.program_id(0)
        x, y, c = _mesh_pos()
        me, sibling = (x, y, c), (x, y, 1 - c)
        chips = {"own": (x, y), "x": (1 - x, y), "y": (x, 1 - y), "d": (1 - x, 1 - y)}
        index = {"own": 0, "x": 1, "y": 2, "d": 3}
        order = ("d", "x", "y", "own")

        def rows(pos):
            return a_hbm.at[pl.ds(_lin(pos) * rows_per, rows_per), :]

        def to_sibling(k):
            return pltpu.make_async_remote_copy(
                src_ref=rows((*chips[k], 1 - c)), dst_ref=sib_ref.at[index[k]],
                send_sem=d2d_send.at[index[k]], recv_sem=d2d_recv.at[index[k]], device_id=sibling, device_id_type=MESH)

        def mine(k):
            return pltpu.make_async_copy(rows((*chips[k], c)), own_ref.at[index[k]], own_sems.at[index[k]])

        def ici(n, to_chip, dst):
            return pltpu.make_async_remote_copy(
                src_ref=snd_ref.at[n], dst_ref=dst, send_sem=ici_send.at[n], recv_sem=ici_recv.at[n],
                device_id=(*chips[to_chip], c), device_id_type=MESH)

        def chip_sum(k):
            to_sibling(k).wait_recv()
            mine(k).wait()
            return own_ref[index[k]].astype(F32) + sib_ref[index[k]].astype(F32)

        def by_core(fn):
            pl.when(c == 0)(lambda: fn("x", "y"))
            pl.when(c == 1)(lambda: fn("y", "x"))

        @pl.when(step == 0)
        def _():
            cp = pltpu.make_async_copy(w_hbm, w_ref, w_sem)
            cp.start()
            for k in order:
                to_sibling(k).start()
                mine(k).start()
            dng_ref[...] = jnp.zeros_like(dng_ref)
            cp.wait()

        @pl.when(step == step_sums)
        def _():
            def first_sends(direct, via):
                snd_ref[0] = chip_sum("d").astype(BF16)
                ici(0, direct, extra_ref).start()
                snd_ref[1] = chip_sum(direct).astype(BF16)
                ici(1, direct, ra_hbm.at[index[direct]]).start()
            by_core(first_sends)

        @pl.when(step == step_merge)
        def _():
            def merge(direct, via):
                merged = chip_sum(via)
                ici(0, direct, extra_ref).wait_recv()
                snd_ref[2] = (merged + extra_ref[...].astype(F32)).astype(BF16)
                ici(2, via, ra_hbm.at[index[via]]).start()
                own_ref[0] = chip_sum("own").astype(BF16)
                pltpu.make_async_copy(own_ref.at[0], ra_hbm.at[0], out_sem).start()
            by_core(merge)

        halves = [slice(k * (TM // 2), (k + 1) * (TM // 2)) for k in range(2)]
        gain = ng_ref[...]
        dh = [_dot(dp_ref[half, :], w_ref[...]) for half in halves]
        for k, half in enumerate(halves):
            xv = x_ref[half, :]
            rstd = lax.rsqrt(jnp.mean(xv * xv, axis=-1, keepdims=True) + EPS)
            xhat = xv * rstd
            dng_ref[0:1, :] += jnp.sum(dh[k] * xhat, axis=0, keepdims=True)
            dxh = dh[k] * gain
            gx_ref[half, :] = dout_ref[half, :] + rstd * (dxh - xhat * jnp.mean(dxh * xhat, axis=-1, keepdims=True))

        @pl.when(step == n_tiles - 1)
        def _():
            s_ref[...] = jnp.concatenate(
                [dng_ref[...], dqg_ref[...], dkg_ref[...], dsink_ref[...], loss_ref[...]], axis=1)
            start_small, finish_small = _direct_exchange(
                lambda dev: s_ref, lambda dev: rs_hbm.at[_lin(dev)], s_send, s_recv, out_sem)

            def finish(direct, via):
                ici(1, direct, ra_hbm.at[index[direct]]).wait_recv()
                ici(2, via, ra_hbm.at[index[via]]).wait_recv()
                for n, to in ((0, direct), (1, direct), (2, via)):
                    ici(n, to, extra_ref).wait_send()
            by_core(finish)
            pltpu.make_async_copy(own_ref.at[0], ra_hbm.at[0], out_sem).wait()
            for k in order:
                to_sibling(k).wait_send()
            start_small()
            finish_small()

    def tile(w):
        return pl.BlockSpec((TM, w), lambda i: (i, 0))

    def whole(a):
        return pl.BlockSpec(a.shape, lambda i: (0, 0))

    hbm = pl.BlockSpec(memory_space=pl.ANY)
    block = (rows_per, D_MODEL)
    return pl.pallas_call(
        body, name="bwd_dx", grid=(n_tiles,),
        in_specs=[tile(D_MODEL), tile(D_MODEL), whole(norm_gain), hbm, tile(IN_WIDTH), hbm,
                  whole(dqg), whole(dkg), whole(dsink), whole(loss_part)],
        out_specs=[tile(D_MODEL), hbm, hbm],
        out_shape=[jax.ShapeDtypeStruct((T_LOC, D_MODEL), F32), jax.ShapeDtypeStruct((3,) + block, BF16),
                   jax.ShapeDtypeStruct((N_DEV, 8, SMALL_W), F32)],
        scratch_shapes=[pltpu.VMEM((IN_WIDTH, D_MODEL), BF16), pltpu.VMEM((8, D_MODEL), F32),
                        pltpu.VMEM((8, SMALL_W), F32),
                        pltpu.VMEM((4,) + block, BF16), pltpu.VMEM((4,) + block, BF16), pltpu.VMEM((3,) + block, BF16),
                        pltpu.VMEM(block, BF16),
                        pltpu.SemaphoreType.DMA, pltpu.SemaphoreType.DMA((4,)), pltpu.SemaphoreType.DMA((4,)),
                        pltpu.SemaphoreType.DMA((3,)), pltpu.SemaphoreType.DMA((3,)), pltpu.SemaphoreType.DMA((4,)),
                        pltpu.SemaphoreType.DMA((7,)), pltpu.SemaphoreType.DMA((7,)), pltpu.SemaphoreType.DMA],
        compiler_params=_params(("arbitrary",)),
    )(x, dout, norm_gain, win_t, dproj, dwin_t, dqg, dkg, dsink, loss_part)


def _adamw(w, g, m, v):
    m = ADAM_B1 * m + (1.0 - ADAM_B1) * g
    v = ADAM_B2 * v + (1.0 - ADAM_B2) * (g * g)
    m_hat = m / (1.0 - ADAM_B1 ** ADAM_STEP)
    v_hat = v / (1.0 - ADAM_B2 ** ADAM_STEP)
    delta = -ADAM_LR * (m_hat / (jnp.sqrt(v_hat) + ADAM_EPS) + ADAM_WD * w)
    return delta, m, v


def _sum_slots(r_ref):
    g = r_ref[0].astype(F32)
    for s in range(1, r_ref.shape[0]):
        g = g + r_ref[s].astype(F32)
    return g


def adamw_all(r_win, r_out, r_small, big_in, big_out, weights, moments_m, moments_v):
    n = len(weights)
    params = [big_in[0], big_out[0], *weights]

    def body(rw_ref, ro_ref, rs_ref, *refs):
        n_p = n + 2
        ins, outs = refs[:3 * n_p], refs[3 * n_p:]
        s = _sum_slots(rs_ref)
        eye = (_row((8, SLAB)) == _lane((8, SLAB))).astype(F32)
        sinks = jnp.sum(s[:, 1280:1408] * eye, axis=0, keepdims=True)
        grads = [_sum_slots(rw_ref), _sum_slots(ro_ref),
                 s[0:1, :D_MODEL], s[0:1, 1024:1024 + HEAD_DIM], s[0:1, 1152:1152 + HEAD_DIM], sinks[:, :8]]
        for k in range(n_p):
            outs[k][...] = grads[k]
            outs[n_p + k][...], outs[2 * n_p + k][...], outs[3 * n_p + k][...] = _adamw(
                ins[k][...], grads[k], ins[n_p + k][...], ins[2 * n_p + k][...])
        loss = jnp.sum(jnp.sum(s[:, 1408:1536], axis=1, keepdims=True), axis=0, keepdims=True) * (0.5 / D_MODEL)
        outs[4 * n_p][...] = loss

    n_p = n + 2
    res = pl.pallas_call(
        body, name="adamw_all",
        out_shape=[jax.ShapeDtypeStruct(p.shape, F32) for p in params] * 4 + [jax.ShapeDtypeStruct((1, 1), F32)],
        compiler_params=pltpu.CompilerParams(vmem_limit_bytes=VMEM_LIMIT),
    )(r_win, r_out, r_small, big_in[0], big_out[0], *weights, big_in[1], big_out[1], *moments_m,
      big_in[2], big_out[2], *moments_v)
    return [res[k * n_p:(k + 1) * n_p] for k in range(4)], res[4 * n_p]


def kernel(x, positions, norm_gain, w_in, q_norm_gain, k_norm_gain, sinks, w_out, loss_target, m_norm_gain, m_w_in, m_q_norm_gain, m_k_norm_gain, m_sinks, m_w_out, v_norm_gain, v_w_in, v_q_norm_gain, v_k_norm_gain, v_sinks, v_w_out):
    x2 = x.reshape(T_LOC, D_MODEL)
    tgt2 = loss_target.reshape(T_LOC, D_MODEL)
    pos2 = positions.reshape(1, T_LOC)
    half = HEAD_DIM // 2
    inv_freq = ROPE_THETA ** (-jnp.arange(half, dtype=F32) * 2.0 / HEAD_DIM)
    inv_freq = jnp.tile(inv_freq, SLAB // half).reshape(SLAB, 1)
    sin_sign = jnp.tile(jnp.concatenate([-jnp.ones((half,), F32), jnp.ones((half,), F32)]), 2).reshape(1, SLAB)
    q_gain2 = jnp.tile(q_norm_gain, (1, 2))
    k_gain2 = jnp.tile(k_norm_gain, (1, 2))

    win_t = gather_weights(w_in.reshape(D_MODEL, IN_SHARD).T)

    (qa_raw, ka_raw, q_rot, k_dup, v_dup, ga, qb, kb, vb, gb, cos, sin_s, wout) = fwd_proj(
        x2, pos2, norm_gain, win_t, inv_freq, sin_sign, q_gain2, k_gain2, w_out.reshape(OUT_SHARD, D_MODEL).astype(BF16))
    o_a = swa_fwd(q_rot, k_dup, v_dup, sinks)
    o_b, carries = sb_fwd(qb, kb, vb)
    dout, d_oa, d_ob, dga, dgb, dwout, loss_part = out_loss(o_a, o_b, ga, gb, x2, tgt2, wout)
    dq_rot, dk_dup, dv_dup, dsink = swa_bwd(q_rot, k_dup, v_dup, o_a, d_oa, sinks)
    dqb, dkb, dvb, r_out = sb_bwd(qb, kb, vb, d_ob, carries, dwout)
    dproj, dwin_t, dqg, dkg = bwd_dw(
        x2, norm_gain, dq_rot, dk_dup, dv_dup, qa_raw, ka_raw, cos, sin_s, q_gain2, k_gain2, dga, dgb, dqb, dkb, dvb)
    grad_x, r_win, r_small = bwd_dx(x2, dout, norm_gain, win_t, dproj, dwin_t, dqg, dkg, dsink, loss_part)

    w_in2, m_in2, v_in2 = (a.reshape(D_MODEL, IN_SHARD).T for a in (w_in, m_w_in, v_w_in))
    w_out2, m_out2, v_out2 = (a.reshape(OUT_SHARD, D_MODEL) for a in (w_out, m_w_out, v_w_out))
    kinds, loss = adamw_all(
        r_win, r_out, r_small, (w_in2, m_in2, v_in2), (w_out2, m_out2, v_out2),
        (norm_gain, q_norm_gain, k_norm_gain, sinks),
        (m_norm_gain, m_q_norm_gain, m_k_norm_gain, m_sinks), (v_norm_gain, v_q_norm_gain, v_k_norm_gain, v_sinks))

    def leaves(k):
        big_in, big_out, ng, qg, kg, sk = kinds[k]
        return (ng, big_in.T.reshape(1, D_MODEL, IN_SHARD), qg, kg, sk, big_out.reshape(1, OUT_SHARD, D_MODEL))

    return (loss.reshape(()), grad_x.reshape(B_LOC, SEQ, D_MODEL), *leaves(0), *leaves(1), *leaves(2), *leaves(3))
```

```python
import functools

import jax
import jax.numpy as jnp
from jax import lax
from jax.experimental import pallas as pl
from jax.experimental.pallas import tpu as pltpu

F32 = jnp.float32
BF16 = jnp.bfloat16

N_DEV = 8
D_MODEL = 1024
SEQ = 2048
B_LOC = 2
T_LOC = B_LOC * SEQ
HEAD_DIM = 64
HEAD_SHIFT = 6
BLK = 128
N_BLK = SEQ // BLK
SLAB = 128
IN_WIDTH = 3328
IN_SHARD = IN_WIDTH // N_DEV
OUT_SHARD = D_MODEL // N_DEV
EPS = 1e-6
ROPE_THETA = 10000.0
Q_SCALE = 0.125
R_QA, R_KA, R_VA, R_GA, R_QB, R_KB, R_VB, R_GB, R_END = 0, 512, 640, 768, 1280, 1792, 2304, 2816, 3328
SMALL_W = 1536
ADAM_LR, ADAM_B1, ADAM_B2, ADAM_EPS, ADAM_WD, ADAM_STEP = 0.001, 0.9, 0.999, 1e-08, 0.01, 10
TM = 512
ACC_ROWS = 256
GATHER_PIECES = (128, 96, 96, 96)
VMEM_LIMIT = 56 * 1024 * 1024

MESH = pl.DeviceIdType.MESH
NT = (((1,), (1,)), ((), ()))
TN = (((0,), (0,)), ((), ()))


def _params(sem, limit=VMEM_LIMIT):
    return pltpu.CompilerParams(dimension_semantics=sem, vmem_limit_bytes=limit)


def _dot(a, b, dims=None):
    if dims is None:
        return jnp.dot(a, b, preferred_element_type=F32)
    return lax.dot_general(a, b, dims, preferred_element_type=F32)


def _lane(shape):
    return lax.broadcasted_iota(jnp.int32, shape, len(shape) - 1)


def _row(shape):
    return lax.broadcasted_iota(jnp.int32, shape, 0)


def _head_blockdiag():
    return ((_row((SLAB, SLAB)) >> HEAD_SHIFT) == (_lane((SLAB, SLAB)) >> HEAD_SHIFT)).astype(BF16)


def _head_sum(x, bd):
    return _dot(x.astype(BF16), bd)


def _swap_half(y, lane):
    return jnp.where((lane & 32) != 0, pltpu.roll(y, 32, 1), pltpu.roll(y, 96, 1))


def _stack_heads(q, lane):
    zero = jnp.zeros_like(q)
    return jnp.concatenate([jnp.where(lane < HEAD_DIM, q, zero), jnp.where(lane >= HEAD_DIM, q, zero)], axis=0)


def _unstack_heads(x2, lane):
    return jnp.where(lane < HEAD_DIM, x2[:BLK], x2[BLK:])


def _sigmoid(x):
    return 1.0 / (1.0 + jnp.exp(-x))


def _mesh_pos():
    return lax.axis_index("x"), lax.axis_index("y"), lax.axis_index("c")


def _flip(pos, mask):
    return tuple(1 - p if m else p for p, m in zip(pos, mask))


def _lin(pos):
    return 4 * pos[0] + 2 * pos[1] + pos[2]


DEV_FLIPS = [(fx, fy, fc) for fx in (0, 1) for fy in (0, 1) for fc in (0, 1)][1:]


def _direct_exchange(src_for, dst_slot, send_sems, recv_sems, local_sem):
    me = _mesh_pos()

    def copy(k, to):
        return pltpu.make_async_remote_copy(
            src_ref=src_for(to), dst_ref=dst_slot(me), send_sem=send_sems.at[k], recv_sem=recv_sems.at[k],
            device_id=to, device_id_type=MESH)

    def landed(k, frm):
        return pltpu.make_async_remote_copy(
            src_ref=src_for(frm), dst_ref=dst_slot(frm), send_sem=send_sems.at[k], recv_sem=recv_sems.at[k],
            device_id=frm, device_id_type=MESH)

    local = None if local_sem is None else pltpu.make_async_copy(src_for(me), dst_slot(me), local_sem)
    peers = [_flip(me, f) for f in DEV_FLIPS]

    def start():
        if local is not None:
            local.start()
        for k, to in enumerate(peers):
            copy(k, to).start()

    def finish():
        for k, frm in enumerate(peers):
            landed(k, frm).wait_recv()
        for k, to in enumerate(peers):
            copy(k, to).wait_send()
        if local is not None:
            local.wait()

    return start, finish


def gather_weights(shard):
    m = shard.shape[0]
    assert sum(GATHER_PIECES) == m
    starts = [sum(GATHER_PIECES[:q]) for q in range(len(GATHER_PIECES))]
    pieces = range(len(GATHER_PIECES))

    def body(f32_ref, o_ref, a_ref, ici_send, ici_recv, d2d_send, d2d_recv, local_sem):
        a_ref[...] = f32_ref[...].astype(BF16)
        x, y, c = _mesh_pos()
        me, sibling = (x, y, c), (x, y, 1 - c)
        chip_x, chip_y, chip_d = (1 - x, y), (x, 1 - y), (1 - x, 1 - y)

        def rows(pos, q):
            return o_ref.at[pl.ds(_lin(pos) * m + starts[q], GATHER_PIECES[q]), :]

        def own(q):
            return a_ref.at[pl.ds(starts[q], GATHER_PIECES[q]), :]

        def ici(k, q, block, to, src=None):
            return pltpu.make_async_remote_copy(
                src_ref=rows(block, q) if src is None else src, dst_ref=rows(block, q),
                send_sem=ici_send.at[k, q], recv_sem=ici_recv.at[k, q], device_id=to, device_id_type=MESH)

        def d2d(k, q, chip, mine, src=None):
            block = (*chip, c) if mine else (*chip, 1 - c)
            return pltpu.make_async_remote_copy(
                src_ref=rows(block, q) if src is None else src, dst_ref=rows(block, q),
                send_sem=d2d_send.at[k, q], recv_sem=d2d_recv.at[k, q], device_id=sibling, device_id_type=MESH)

        local = pltpu.make_async_copy(a_ref, o_ref.at[pl.ds(_lin(me) * m, m), :], local_sem)
        local.start()
        sends = []
        for q in pieces:
            sends += [ici(0, q, me, (*chip_x, c), src=own(q)), ici(1, q, me, (*chip_y, c), src=own(q)),
                      d2d(0, q, (x, y), True, src=own(q))]
        for cp in sends:
            cp.start()

        def pass_on(first, k_first, second, k_second, onward):
            moved = []
            for q in pieces:
                ici(k_first, q, (*first, c), me).wait_recv()
                moved += [ici(2, q, (*first, c), (*onward, c)), d2d(1 + k_first, q, first, True)]
                for cp in moved[-2:]:
                    cp.start()
            for q in pieces:
                ici(k_second, q, (*second, c), me).wait_recv()
                moved.append(d2d(1 + k_second, q, second, True))
                moved[-1].start()
            for q in pieces:
                ici(2, q, (*chip_d, c), me).wait_recv()
                moved.append(d2d(3, q, chip_d, True))
                moved[-1].start()
            for cp in moved:
                cp.wait_send()

        @pl.when(c == 0)
        def _():
            pass_on(chip_y, 1, chip_x, 0, chip_x)

        @pl.when(c == 1)
        def _():
            pass_on(chip_x, 0, chip_y, 1, chip_y)

        for k, chip in enumerate([(x, y), chip_x, chip_y, chip_d]):
            for q in pieces:
                d2d(k, q, chip, False).wait_recv()
        for cp in sends:
            cp.wait_send()
        local.wait()

    vmem = pl.BlockSpec(memory_space=pltpu.VMEM)
    n_q = len(GATHER_PIECES)
    return pl.pallas_call(
        body, name="gather_weights",
        out_shape=jax.ShapeDtypeStruct((N_DEV * m, shard.shape[1]), BF16),
        in_specs=[vmem], out_specs=vmem,
        scratch_shapes=[pltpu.VMEM(shard.shape, BF16), pltpu.SemaphoreType.DMA((3, n_q)), pltpu.SemaphoreType.DMA((3, n_q)),
                        pltpu.SemaphoreType.DMA((4, n_q)), pltpu.SemaphoreType.DMA((4, n_q)), pltpu.SemaphoreType.DMA],
        compiler_params=pltpu.CompilerParams(vmem_limit_bytes=VMEM_LIMIT),
    )(shard)


def _norm_rope(xs, gain2, cos, sin_s, bd, lane):
    r = lax.rsqrt(_head_sum(xs * xs, bd) * (1.0 / HEAD_DIM) + EPS)
    y = xs * r * gain2
    return y * cos + _swap_half(y, lane) * sin_s


def _dup_heads(xs, lane):
    r = pltpu.roll(xs, HEAD_DIM, 1)
    lo = lane < HEAD_DIM
    return jnp.concatenate([jnp.where(lo, xs, r), jnp.where(lo, r, xs)], axis=1)


def fwd_proj(x, pos, norm_gain, win_t, inv_freq, sin_sign, q_gain2, k_gain2, wout_shard):
    n_tiles = T_LOC // TM

    def body(x_ref, pos_ref, ng_ref, w_ref, if_ref, sg_ref, qg_ref, kg_ref, ws_hbm,
             qa_raw_ref, ka_raw_ref, q_rot_ref, k_dup_ref, v_dup_ref, ga_ref, qb_ref, kb_ref, vb_ref, gb_ref,
             cos_ref, sin_ref, wo_hbm, wo_send, wo_recv, wo_local):
        start_wout, finish_wout = _direct_exchange(
            lambda dev: ws_hbm, lambda dev: wo_hbm.at[pl.ds(_lin(dev) * OUT_SHARD, OUT_SHARD), :],
            wo_send, wo_recv, wo_local)
        pl.when(pl.program_id(0) == 0)(start_wout)

        xv = x_ref[...]
        rstd = lax.rsqrt(jnp.mean(xv * xv, axis=-1, keepdims=True) + EPS)
        h = (xv * rstd * ng_ref[...]).astype(BF16)

        def proj(r0, r1):
            return _dot(h, w_ref[r0:r1, :], NT)

        ang_t = if_ref[...] * pos_ref[...].astype(F32)
        cos = jnp.cos(ang_t).T
        sin_s = jnp.sin(ang_t).T * sg_ref[...]
        cos_ref[...] = cos
        sin_ref[...] = sin_s
        lane = _lane((TM, SLAB))
        bd = _head_blockdiag()

        qa = proj(R_QA, R_KA)
        qa_raw_ref[...] = qa
        for p in range(4):
            sl = slice(p * SLAB, (p + 1) * SLAB)
            q_rot_ref[:, sl] = (_norm_rope(qa[:, sl], qg_ref[...], cos, sin_s, bd, lane) * Q_SCALE).astype(BF16)
        ka = proj(R_KA, R_VA)
        ka_raw_ref[...] = ka
        k_dup_ref[...] = _dup_heads(_norm_rope(ka, kg_ref[...], cos, sin_s, bd, lane), lane).astype(BF16)
        v_dup_ref[...] = _dup_heads(proj(R_VA, R_GA), lane).astype(BF16)
        ga_ref[...] = proj(R_GA, R_QB).astype(BF16)
        qb_ref[...] = (proj(R_QB, R_KB) * Q_SCALE).astype(BF16)
        kb_ref[...] = proj(R_KB, R_VB).astype(BF16)
        vb_ref[...] = proj(R_VB, R_GB).astype(BF16)
        gb_ref[...] = proj(R_GB, R_END).astype(BF16)
        pl.when(pl.program_id(0) == n_tiles - 1)(finish_wout)

    def tile(w):
        return pl.BlockSpec((TM, w), lambda i: (i, 0))

    def whole(a):
        return pl.BlockSpec(a.shape, lambda i: (0, 0))

    hbm = pl.BlockSpec(memory_space=pl.ANY)
    widths = [(512, F32), (128, F32), (512, BF16), (256, BF16), (256, BF16), (512, BF16), (512, BF16), (512, BF16),
              (512, BF16), (512, BF16), (128, F32), (128, F32)]
    return pl.pallas_call(
        body, name="fwd_proj", grid=(n_tiles,),
        in_specs=[tile(D_MODEL), pl.BlockSpec((1, TM), lambda i: (0, i)), whole(norm_gain), whole(win_t),
                  whole(inv_freq), whole(sin_sign),
                  whole(q_gain2), whole(k_gain2), hbm],
        out_specs=[tile(w) for w, _ in widths] + [hbm],
        out_shape=[jax.ShapeDtypeStruct((T_LOC, w), dt) for w, dt in widths]
        + [jax.ShapeDtypeStruct((D_MODEL, D_MODEL), BF16)],
        scratch_shapes=[pltpu.SemaphoreType.DMA((7,)), pltpu.SemaphoreType.DMA((7,)), pltpu.SemaphoreType.DMA],
        compiler_params=_params(("arbitrary",)),
    )(x, pos, norm_gain, win_t, inv_freq, sin_sign, q_gain2, k_gain2, wout_shard)


def _swa_window(prev_ref, cur_ref, p):
    gsl = _slab(p // 2)
    return jnp.concatenate([prev_ref[:, gsl], cur_ref[:, gsl]], axis=0)


def _swa_probs(s, sinks_ref, p, i):
    shape = (2 * BLK, 2 * BLK)
    r = _row(shape) & (BLK - 1)
    cidx = _lane(shape)
    valid = (cidx > r) & (cidx <= r + BLK) & ((cidx >= BLK) | (i > 0))
    s = jnp.where(valid, s, -jnp.inf)
    sink = jnp.where(_row((2 * BLK, 1)) < BLK, sinks_ref[0, 2 * p], sinks_ref[0, 2 * p + 1])
    m = jnp.maximum(jnp.max(s, axis=-1, keepdims=True), sink)
    e = jnp.exp(s - m)
    e_sink = jnp.exp(sink - m)
    den = jnp.sum(e, axis=-1, keepdims=True) + e_sink
    return e / den, e_sink / den


SWA_CHAINS = [(b, p) for b in range(B_LOC) for p in range(4)]


def _swa_specs():
    def cur(w):
        return pl.BlockSpec((B_LOC, BLK, w), lambda i: (0, i, 0))

    def prev(w):
        return pl.BlockSpec((B_LOC, BLK, w), lambda i: (0, jnp.maximum(i - 1, 0), 0))

    return cur, prev


def swa_fwd(q_rot, k_dup, v_dup, sinks):
    def body(q_ref, kp_ref, kc_ref, vp_ref, vc_ref, sinks_ref, o_ref):
        i = pl.program_id(0)
        lane = _lane((BLK, SLAB))
        s = [_dot(_stack_heads(q_ref[b, :, _slab(p)], lane), _swa_window(kp_ref.at[b], kc_ref.at[b], p), NT)
             for b, p in SWA_CHAINS]
        pn = [_swa_probs(s[c], sinks_ref, p, i)[0].astype(BF16) for c, (b, p) in enumerate(SWA_CHAINS)]
        for c, (b, p) in enumerate(SWA_CHAINS):
            o = _unstack_heads(_dot(pn[c], _swa_window(vp_ref.at[b], vc_ref.at[b], p)), lane)
            o_ref[b, :, _slab(p)] = o.astype(BF16)

    cur, prev = _swa_specs()
    q3, k3, v3 = (a.reshape(B_LOC, SEQ, a.shape[1]) for a in (q_rot, k_dup, v_dup))
    return pl.pallas_call(
        body, name="swa_fwd", grid=(N_BLK,),
        in_specs=[cur(512), prev(256), cur(256), prev(256), cur(256), pl.BlockSpec(memory_space=pltpu.SMEM)],
        out_specs=cur(512),
        out_shape=jax.ShapeDtypeStruct((B_LOC, SEQ, 512), BF16),
        compiler_params=_params(("arbitrary",)),
    )(q3, k3, k3, v3, v3, sinks).reshape(T_LOC, 512)


def _tri(suffix):
    r, cidx = _row((BLK + 16, BLK)), _lane((BLK + 16, BLK))
    tri = (cidx > r) if suffix else (cidx < r)
    return (tri | (r >= BLK)).astype(BF16)


def _key_sums(tri, x):
    res = _dot(tri, x.astype(BF16))
    return res[:BLK], res[BLK:BLK + 1]


def _sb_softplus(zt, valid):
    neg_abs = lax.bitcast_convert_type(lax.bitcast_convert_type(zt, jnp.uint32) | jnp.uint32(0x80000000), F32)
    sp = jnp.maximum(zt, 0.0) + jnp.log(1.0 + jnp.exp(neg_abs))
    return sp if valid is None else jnp.where(valid, sp, 0.0)


def _sb_weights(zt, sp, later, valid):
    w = jnp.exp(zt - sp - later)
    return w if valid is None else jnp.where(valid, w, 0.0)


def _slab(pp):
    return slice(pp * SLAB, (pp + 1) * SLAB)


def _blk(j):
    return pl.ds(pl.multiple_of(j * BLK, BLK), BLK)


def _causal_t():
    return _row((BLK, 2 * BLK)) < (_lane((BLK, 2 * BLK)) & (BLK - 1))


def _sb_rows(b, j):
    return pl.ds(pl.multiple_of(b * SEQ + j * BLK, BLK), BLK)


SB_CHAINS = [(b, pp) for b in range(B_LOC) for pp in range(4)]


def sb_fwd(qb, kb, vb):
    def body(q_ref, k_ref, v_ref, o_ref, c_ref, vt_ref, ot_ref):
        for c, (b, pp) in enumerate(SB_CHAINS):
            for j in range(N_BLK):
                vt_ref[c, j] = v_ref[b * SEQ + j * BLK:b * SEQ + (j + 1) * BLK, _slab(pp)].T
        lane = _lane((BLK, SLAB))
        tri = _tri(True)
        valid = _causal_t()
        jrow = _row((N_BLK, 2 * BLK))
        chains = range(len(SB_CHAINS))

        def q_block(i, _):
            q2 = [_stack_heads(q_ref[_sb_rows(b, i), _slab(pp)], lane) for b, pp in SB_CHAINS]

            def key_block(j, carry, mask, first):
                zt = [_dot(k_ref[_sb_rows(b, j), _slab(pp)], q2[c], NT) for c, (b, pp) in enumerate(SB_CHAINS)]
                sp = [_sb_softplus(zt[c], mask) for c in chains]
                sums = [_key_sums(tri, sp[c]) for c in chains]
                w = [_sb_weights(zt[c], sp[c], sums[c][0] + carry[c], mask) for c in chains]
                for c in chains:
                    pv = _dot(vt_ref[c, j], w[c].astype(BF16))
                    if first:
                        ot_ref[c] = pv
                    else:
                        ot_ref[c] += pv
                return tuple(carry[c] + sums[c][1] for c in chains)

            def earlier(jj, state):
                carry, saved = state
                j = i - 1 - jj
                saved = tuple(jnp.where(jrow == j, carry[c], saved[c]) for c in chains)
                return key_block(j, carry, None, False), saved

            zero = tuple(jnp.zeros((1, 2 * BLK), F32) for _ in chains)
            carry = key_block(i, zero, valid, True)
            _, saved = lax.fori_loop(0, i, earlier, (carry, tuple(jnp.zeros((N_BLK, 2 * BLK), F32) for _ in chains)))
            for c, (b, pp) in enumerate(SB_CHAINS):
                o_ref[_sb_rows(b, i), _slab(pp)] = _unstack_heads(ot_ref[c].T, lane).astype(BF16)
                c_ref[c * N_BLK + i] = saved[c]
            return 0

        lax.fori_loop(0, N_BLK, q_block, 0)

    n_ch = len(SB_CHAINS)
    vmem = pl.BlockSpec(memory_space=pltpu.VMEM)
    return pl.pallas_call(
        body, name="sb_fwd",
        in_specs=[vmem] * 3, out_specs=[vmem] * 2,
        out_shape=[jax.ShapeDtypeStruct((T_LOC, 512), BF16), jax.ShapeDtypeStruct((n_ch * N_BLK, N_BLK, 2 * BLK), F32)],
        scratch_shapes=[pltpu.VMEM((n_ch, N_BLK, SLAB, BLK), BF16), pltpu.VMEM((n_ch, SLAB, 2 * BLK), F32)],
        compiler_params=pltpu.CompilerParams(vmem_limit_bytes=VMEM_LIMIT),
    )(qb, kb, vb)


def out_loss(o_a, o_b, ga, gb, x, target, wout):
    n_tiles = T_LOC // TM

    def body(oa_ref, ob_ref, ga_ref, gb_ref, x_ref, t_ref, w_ref,
             dout_ref, doa_ref, dob_ref, dga_ref, dgb_ref, dw_ref, loss_ref, acc_ref):
        step = pl.program_id(0)

        @pl.when(step == 0)
        def _():
            acc_ref[...] = jnp.zeros_like(acc_ref)
            loss_ref[...] = jnp.zeros_like(loss_ref)

        oa, ob, gav, gbv = (r[...].astype(F32) for r in (oa_ref, ob_ref, ga_ref, gb_ref))
        sa, sb = _sigmoid(gav), _sigmoid(gbv)
        silu_a, silu_b = gav * sa, gbv * sb
        y = jnp.concatenate([oa * silu_a, ob * silu_b], axis=1).astype(BF16)
        err = x_ref[...] + _dot(y, w_ref[...]) - t_ref[...]
        e2 = err * err
        part = jnp.sum(e2.reshape(TM // 8, 8, D_MODEL), axis=0)
        loss_ref[...] += functools.reduce(lambda a, b: a + b, [part[:, k * 128:(k + 1) * 128] for k in range(8)])
        dout = err * (1.0 / D_MODEL)
        dout_ref[...] = dout
        dob16 = dout.astype(BF16)
        for r0 in range(0, D_MODEL, ACC_ROWS):
            acc_ref[r0:r0 + ACC_ROWS, :] += _dot(y[:, r0:r0 + ACC_ROWS], dob16, TN)
        dy = _dot(dob16, w_ref[...], NT)
        dya, dyb = dy[:, :512], dy[:, 512:]
        doa_ref[...] = (dya * silu_a).astype(BF16)
        dob_ref[...] = (dyb * silu_b).astype(BF16)
        dga_ref[...] = (dya * oa * (sa * (1.0 + gav * (1.0 - sa)))).astype(BF16)
        dgb_ref[...] = (dyb * ob * (sb * (1.0 + gbv * (1.0 - sb)))).astype(BF16)

        @pl.when(step == n_tiles - 1)
        def _():
            dw_ref[...] = acc_ref[...].astype(BF16)

    def tile(w):
        return pl.BlockSpec((TM, w), lambda i: (i, 0))

    const = lambda i: (0, 0)
    return pl.pallas_call(
        body, name="out_loss", grid=(n_tiles,),
        in_specs=[tile(512)] * 4 + [tile(D_MODEL)] * 2 + [pl.BlockSpec((D_MODEL, D_MODEL), const)],
        out_specs=[tile(D_MODEL), tile(512), tile(512), tile(512), tile(512),
                   pl.BlockSpec((D_MODEL, D_MODEL), const), pl.BlockSpec((8, 128), const)],
        out_shape=[jax.ShapeDtypeStruct((T_LOC, D_MODEL), F32)] + [jax.ShapeDtypeStruct((T_LOC, 512), BF16)] * 4
        + [jax.ShapeDtypeStruct((D_MODEL, D_MODEL), BF16), jax.ShapeDtypeStruct((8, 128), F32)],
        scratch_shapes=[pltpu.VMEM((D_MODEL, D_MODEL), F32)],
        compiler_params=_params(("arbitrary",)),
    )(o_a, o_b, ga, gb, x, target, wout)


def swa_bwd(q_rot, k_dup, v_dup, o_a, d_oa, sinks):
    def body(q_ref, kp_ref, kc_ref, vp_ref, vc_ref, o_ref, do_ref, sinks_ref, dq_ref, dk_ref, dv_ref, dsink_ref):
        i = pl.program_id(0)

        @pl.when(i == 0)
        def _():
            dk_ref[...] = jnp.zeros_like(dk_ref)
            dv_ref[...] = jnp.zeros_like(dv_ref)
            dsink_ref[...] = jnp.zeros_like(dsink_ref)

        lane = _lane((BLK, SLAB))
        rows_prev, rows_cur = _blk(jnp.maximum(i - 1, 0)), _blk(i)
        chains = range(len(SWA_CHAINS))
        q2 = [_stack_heads(q_ref[b, :, _slab(p)], lane) for b, p in SWA_CHAINS]
        do2 = [_stack_heads(do_ref[b, :, _slab(p)], lane) for b, p in SWA_CHAINS]
        keys = [_swa_window(kp_ref.at[b], kc_ref.at[b], p) for b, p in SWA_CHAINS]
        s = [_dot(q2[c], keys[c], NT) for c in chains]
        dp = [_dot(do2[c], _swa_window(vp_ref.at[b], vc_ref.at[b], p), NT) for c, (b, p) in enumerate(SWA_CHAINS)]
        ds, pn16, cols = [], [], []
        for c, (b, p) in enumerate(SWA_CHAINS):
            pn, p_sink = _swa_probs(s[c], sinks_ref, p, i)
            o = o_ref[b, :, _slab(p)].astype(F32)
            delta =jnp.sum(do2[c].astype(F32) * jnp.concatenate([o, o], axis=0), axis=-1, keepdims=True)
            ds.append((pn * (dp[c] - delta)).astype(BF16))
            pn16.append(pn.astype(BF16))
            cols.append(-p_sink * delta)
        for c, (b, p) in enumerate(SWA_CHAINS):
            dq_ref[b, :, _slab(p)] = _unstack_heads(_dot(ds[c], keys[c]), lane) * Q_SCALE
        dk2 = [_dot(ds[c], q2[c], TN) for c in chains]
        dv2 = [_dot(pn16[c], do2[c], TN) for c in chains]
        for c, (b, p) in enumerate(SWA_CHAINS):
            gsl = _slab(p // 2)
            dk_ref[b, rows_prev, gsl] += dk2[c][:BLK]
            dk_ref[b, rows_cur, gsl] += dk2[c][BLK:]
            dv_ref[b, rows_prev, gsl] += dv2[c][:BLK]
            dv_ref[b, rows_cur, gsl] += dv2[c][BLK:]
            for e in range(2):
                dsink_ref[2 * p + e:2 * p + e + 1, :] += jnp.sum(cols[c][e * BLK:(e + 1) * BLK], axis=0, keepdims=True)

    cur, prev = _swa_specs()
    whole = pl.BlockSpec((B_LOC, SEQ, 256), lambda i: (0, 0, 0))
    q3, k3, v3, o3, do3 = (a.reshape(B_LOC, SEQ, a.shape[1]) for a in (q_rot, k_dup, v_dup, o_a, d_oa))
    dq, dk, dv, dsink = pl.pallas_call(
        body, name="swa_bwd", grid=(N_BLK,),
        in_specs=[cur(512), prev(256), cur(256), prev(256), cur(256), cur(512), cur(512),
                  pl.BlockSpec(memory_space=pltpu.SMEM)],
        out_specs=[cur(512), whole, whole, pl.BlockSpec((8, 128), lambda i: (0, 0))],
        out_shape=[jax.ShapeDtypeStruct((B_LOC, SEQ, 512), F32), jax.ShapeDtypeStruct((B_LOC, SEQ, 256), F32),
                   jax.ShapeDtypeStruct((B_LOC, SEQ, 256), F32), jax.ShapeDtypeStruct((8, 128), F32)],
        compiler_params=_params(("arbitrary",)),
    )(q3, k3, k3, v3, v3, o3, do3, sinks)
    return dq.reshape(T_LOC, 512), dk.reshape(T_LOC, 256), dv.reshape(T_LOC, 256), dsink


def sb_bwd(qb, kb, vb, d_ob, carries, dwout):
    def body(q_ref, k_ref, v_ref, do_ref, c_ref, dw_hbm, dq_ref, dk_ref, dv_ref, rw_hbm, kt_ref, dqt_ref,
             rw_send, rw_recv, rw_local):
        start_dwout, finish_dwout = _direct_exchange(
            lambda dev: dw_hbm.at[pl.ds(_lin(dev) * OUT_SHARD, OUT_SHARD), :], lambda dev: rw_hbm.at[_lin(dev)],
            rw_send, rw_recv, rw_local)
        start_dwout()
        for c, (b, pp) in enumerate(SB_CHAINS):
            for j in range(N_BLK):
                kt_ref[c, j] = k_ref[b * SEQ + j * BLK:b * SEQ + (j + 1) * BLK, _slab(pp)].T
        dk_ref[...] = jnp.zeros_like(dk_ref)
        dv_ref[...] = jnp.zeros_like(dv_ref)
        dqt_ref[...] = jnp.zeros_like(dqt_ref)
        lane = _lane((BLK, SLAB))
        tri_after, tri_before = _tri(True), _tri(False)
        valid = _causal_t()
        jrow = _row((N_BLK, 2 * BLK))
        chains = range(len(SB_CHAINS))

        def q_block(i, _):
            q2 = [_stack_heads(q_ref[_sb_rows(b, i), _slab(pp)], lane) for b, pp in SB_CHAINS]
            do2 = [_stack_heads(do_ref[_sb_rows(b, i), _slab(pp)], lane) for b, pp in SB_CHAINS]

            def key_block(j, carry_sp, before_u, mask):
                at = [(_sb_rows(b, j), _slab(pp)) for b, pp in SB_CHAINS]
                zt = [_dot(k_ref[at[c]], q2[c], NT) for c in chains]
                dw = [_dot(v_ref[at[c]], do2[c], NT) for c in chains]
                sp = [_sb_softplus(zt[c], mask) for c in chains]
                later = [_key_sums(tri_after, sp[c])[0] for c in chains]
                w = [_sb_weights(zt[c], sp[c], later[c] + carry_sp[c], mask) for c in chains]
                u = [dw[c] * w[c] for c in chains]
                for c in chains:
                    dv_ref[at[c]] += _dot(w[c].astype(BF16), do2[c])
                sums = [_key_sums(tri_before, u[c]) for c in chains]
                dz16 = []
                for c in chains:
                    sig = jnp.exp(zt[c] - sp[c])
                    dz = u[c] - sig * (u[c] + before_u[c] + sums[c][0])
                    if mask is not None:
                        dz = jnp.where(mask, dz, 0.0)
                    dz16.append(dz.astype(BF16))
                for c in chains:
                    dk_ref[at[c]] += _dot(dz16[c], q2[c])
                    dqt_ref[c] += _dot(kt_ref[c, j], dz16[c])
                return tuple(before_u[c] + sums[c][1] for c in chains)

            def earlier(j, before_u):
                carry_sp = [jnp.sum(jnp.where(jrow == j, c_ref[c * N_BLK + i], 0.0), axis=0, keepdims=True)
                            for c in chains]
                return key_block(j, carry_sp, before_u, None)

            zero = tuple(jnp.zeros((1, 2 * BLK), F32) for _ in chains)
            before_u = lax.fori_loop(0, i, earlier, zero)
            key_block(i, zero, before_u, valid)
            for c, (b, pp) in enumerate(SB_CHAINS):
                dq_ref[_sb_rows(b, i), _slab(pp)] = (_unstack_heads(dqt_ref[c].T, lane) * Q_SCALE).astype(BF16)
                dqt_ref[c] = jnp.zeros((SLAB, 2 * BLK), F32)
            return 0

        lax.fori_loop(0, N_BLK, q_block, 0)
        finish_dwout()

    n_ch = len(SB_CHAINS)
    vmem, hbm = pl.BlockSpec(memory_space=pltpu.VMEM), pl.BlockSpec(memory_space=pl.ANY)
    return pl.pallas_call(
        body, name="sb_bwd",
        in_specs=[vmem] * 5 + [hbm], out_specs=[vmem] * 3 + [hbm],
        out_shape=[jax.ShapeDtypeStruct((T_LOC, 512), BF16)] + [jax.ShapeDtypeStruct((T_LOC, 512), F32)] * 2
        + [jax.ShapeDtypeStruct((N_DEV, OUT_SHARD, D_MODEL), BF16)],
        scratch_shapes=[pltpu.VMEM((n_ch, N_BLK, SLAB, BLK), BF16), pltpu.VMEM((n_ch, SLAB, 2 * BLK), F32),
                        pltpu.SemaphoreType.DMA((7,)), pltpu.SemaphoreType.DMA((7,)), pltpu.SemaphoreType.DMA],
        compiler_params=pltpu.CompilerParams(vmem_limit_bytes=VMEM_LIMIT),
    )(qb, kb, vb, d_ob, carries, dwout)


def bwd_dw(x, norm_gain, dq_rot, dk_dup, dv_dup, qa_raw, ka_raw, cos, sin_s, q_gain2, k_gain2, dga, dgb, dqb, dkb, dvb):
    n_tiles = T_LOC // TM

    def body(x_ref, ng_ref, dq_ref, dk_ref, dv_ref, qa_ref, ka_ref, cos_ref, sin_ref, qg_ref, kg_ref,
             dga_ref, dgb_ref, dqb_ref, dkb_ref, dvb_ref,
             dproj_ref, dw_hbm, dqg_ref, dkg_ref, acc_ref, stage_ref):
        step = pl.program_id(0)

        @pl.when(step == 0)
        def _():
            acc_ref[...] = jnp.zeros_like(acc_ref)
            dqg_ref[...] = jnp.zeros_like(dqg_ref)
            dkg_ref[...] = jnp.zeros_like(dkg_ref)

        lane = _lane((TM, SLAB))
        bd = _head_blockdiag()
        cos, sin_s = cos_ref[...], sin_ref[...]

        def norm_rope_bwd(d_rot, raw, gain2):
            dy = d_rot * cos + _swap_half(d_rot * sin_s, lane)
            r = lax.rsqrt(_head_sum(raw * raw, bd) * (1.0 / HEAD_DIM) + EPS)
            xhat = raw * r
            dgain = jnp.sum(dy * xhat, axis=0, keepdims=True)
            dxh = dy * gain2
            mean = _head_sum(dxh * xhat, bd) * (1.0 / HEAD_DIM)
            return r * (dxh - xhat * mean), dgain

        def fold_dup(d_dup):
            a, b2 = d_dup[:, :SLAB], d_dup[:, SLAB:]
            return jnp.where(lane < HEAD_DIM, a + pltpu.roll(a, HEAD_DIM, 1), b2 + pltpu.roll(b2, HEAD_DIM, 1))

        pieces = []
        dqg = jnp.zeros((1, SLAB), F32)
        for p in range(4):
            sl = slice(p * SLAB, (p + 1) * SLAB)
            d_raw, dg = norm_rope_bwd(dq_ref[:, sl], qa_ref[:, sl], qg_ref[...])
            pieces.append(d_raw.astype(BF16))
            dqg = dqg + dg
        d_raw, dkg = norm_rope_bwd(fold_dup(dk_ref[...]), ka_ref[...], kg_ref[...])
        pieces.append(d_raw.astype(BF16))
        pieces.append(fold_dup(dv_ref[...]).astype(BF16))
        pieces += [dga_ref[...], dqb_ref[...], dkb_ref[...].astype(BF16), dvb_ref[...].astype(BF16),
                   dgb_ref[...]]
        dproj = jnp.concatenate(pieces, axis=1)
        dproj_ref[...] = dproj
        dqg_ref[0:1, :] += dqg + pltpu.roll(dqg, HEAD_DIM, 1)
        dkg_ref[0:1, :] += dkg + pltpu.roll(dkg, HEAD_DIM, 1)

        xv = x_ref[...]
        rstd = lax.rsqrt(jnp.mean(xv * xv, axis=-1, keepdims=True) + EPS)
        h = (xv * rstd * ng_ref[...]).astype(BF16)
        for r0 in range(0, IN_WIDTH, ACC_ROWS):
            acc_ref[r0:r0 + ACC_ROWS, :] += _dot(dproj[:, r0:r0 + ACC_ROWS], h, TN)

        @pl.when(step == n_tiles - 1)
        def _():
            for r0 in range(0, IN_WIDTH, ACC_ROWS):
                stage_ref[...] = acc_ref[r0:r0 + ACC_ROWS, :].astype(BF16)
                pltpu.sync_copy(stage_ref, dw_hbm.at[r0:r0 + ACC_ROWS, :])

    def tile(w):
        return pl.BlockSpec((TM, w), lambda i: (i, 0))

    def whole(a):
        return pl.BlockSpec(a.shape, lambda i: (0, 0))

    const = lambda i: (0, 0)
    return pl.pallas_call(
        body, name="bwd_dw", grid=(n_tiles,),
        in_specs=[tile(D_MODEL), whole(norm_gain),
                  tile(512), tile(256), tile(256), tile(512), tile(128), tile(128), tile(128),
                  whole(q_gain2), whole(k_gain2), tile(512), tile(512), tile(512), tile(512), tile(512)],
        out_specs=[tile(IN_WIDTH), pl.BlockSpec(memory_space=pl.ANY),
                   pl.BlockSpec((8, SLAB), const), pl.BlockSpec((8, SLAB), const)],
        out_shape=[jax.ShapeDtypeStruct((T_LOC, IN_WIDTH), BF16), jax.ShapeDtypeStruct((IN_WIDTH, D_MODEL), BF16),
                   jax.ShapeDtypeStruct((8, SLAB), F32), jax.ShapeDtypeStruct((8, SLAB), F32)],
        scratch_shapes=[pltpu.VMEM((IN_WIDTH, D_MODEL), F32), pltpu.VMEM((ACC_ROWS, D_MODEL), BF16)],
        compiler_params=_params(("arbitrary",)),
    )(x, norm_gain, dq_rot, dk_dup, dv_dup, qa_raw, ka_raw, cos, sin_s, q_gain2, k_gain2, dga, dgb, dqb, dkb, dvb)


def bwd_dx(x, dout, norm_gain, win_t, dproj, dwin_t, dqg, dkg, dsink, loss_part):
    n_tiles = T_LOC // TM
    rows_per = IN_SHARD
    step_sums, step_merge = 1, 3

    def body(x_ref, dout_ref, ng_ref, w_hbm, dp_ref, a_hbm, dqg_ref, dkg_ref, dsink_ref, loss_ref,
             gx_ref, ra_hbm, rs_hbm, w_ref, dng_ref, s_ref, own_ref, sib_ref, snd_ref, extra_ref,
             w_sem, d2d_send, d2d_recv, ici_send, ici_recv, own_sems, s_send, s_recv, out_sem):
        step = pl.program_id(0)
        x, y, c = _mesh_pos()
        me, sibling = (x, y, c), (x, y, 1 - c)
        chips = {"own": (x, y), "x": (1 - x, y), "y": (x, 1 - y), "d": (1 - x, 1 - y)}
        index = {"own": 0, "x": 1, "y": 2, "d": 3}
        order = ("d", "x", "y", "own")

        def rows(pos):
            return a_hbm.at[pl.ds(_lin(pos) * rows_per, rows_per), :]

        def to_sibling(k):
            return pltpu.make_async_remote_copy(
                src_ref=rows((*chips[k], 1 - c)), dst_ref=sib_ref.at[index[k]],
                send_sem=d2d_send.at[index[k]], recv_sem=d2d_recv.at[index[k]], device_id=sibling, device_id_type=MESH)

        def mine(k):
            return pltpu.make_async_copy(rows((*chips[k], c)), own_ref.at[index[k]], own_sems.at[index[k]])

        def ici(n, to_chip, dst):
            return pltpu.make_async_remote_copy(
                src_ref=snd_ref.at[n], dst_ref=dst, send_sem=ici_send.at[n], recv_sem=ici_recv.at[n],
                device_id=(*chips[to_chip], c), device_id_type=MESH)

        def chip_sum(k):
            to_sibling(k).wait_recv()
            mine(k).wait()
            return own_ref[index[k]].astype(F32) + sib_ref[index[k]].astype(F32)

        def by_core(fn):
            pl.when(c == 0)(lambda: fn("x", "y"))
            pl.when(c == 1)(lambda: fn("y", "x"))

        @pl.when(step == 0)
        def _():
            cp = pltpu.make_async_copy(w_hbm, w_ref, w_sem)
            cp.start()
            for k in order:
                to_sibling(k).start()
                mine(k).start()
            dng_ref[...] = jnp.zeros_like(dng_ref)
            cp.wait()

        @pl.when(step == step_sums)
        def _():
            def first_sends(direct, via):
                snd_ref[0] = chip_sum("d").astype(BF16)
                ici(0, direct, extra_ref).start()
                snd_ref[1] = chip_sum(direct).astype(BF16)
                ici(1, direct, ra_hbm.at[index[direct]]).start()
            by_core(first_sends)

        @pl.when(step == step_merge)
        def _():
            def merge(direct, via):
                merged = chip_sum(via)
                ici(0, direct, extra_ref).wait_recv()
                snd_ref[2] = (merged + extra_ref[...].astype(F32)).astype(BF16)
                ici(2, via, ra_hbm.at[index[via]]).start()
                own_ref[0] = chip_sum("own").astype(BF16)
                pltpu.make_async_copy(own_ref.at[0], ra_hbm.at[0], out_sem).start()
            by_core(merge)

        xv = x_ref[...]
        rstd = lax.rsqrt(jnp.mean(xv * xv, axis=-1, keepdims=True) + EPS)
        xhat = xv * rstd
        gain = ng_ref[...]
        dh = _dot(dp_ref[...], w_ref[...])
        dng_ref[0:1, :] += jnp.sum(dh * xhat, axis=0, keepdims=True)
        dxh = dh * gain
        gx_ref[...] = dout_ref[...] + rstd * (dxh - xhat * jnp.mean(dxh * xhat, axis=-1, keepdims=True))

        @pl.when(step == n_tiles - 1)
        def _():
            s_ref[...] = jnp.concatenate(
                [dng_ref[...], dqg_ref[...], dkg_ref[...], dsink_ref[...], loss_ref[...]], axis=1)
            start_small, finish_small = _direct_exchange(
                lambda dev: s_ref, lambda dev: rs_hbm.at[_lin(dev)], s_send, s_recv, out_sem)

            def finish(direct, via):
                ici(1, direct, ra_hbm.at[index[direct]]).wait_recv()
                ici(2, via, ra_hbm.at[index[via]]).wait_recv()
                for n, to in ((0, direct), (1, direct), (2, via)):
                    ici(n, to, extra_ref).wait_send()
            by_core(finish)
            pltpu.make_async_copy(own_ref.at[0], ra_hbm.at[0], out_sem).wait()
            for k in order:
                to_sibling(k).wait_send()
            start_small()
            finish_small()

    def tile(w):
        return pl.BlockSpec((TM, w), lambda i: (i, 0))

    def whole(a):
        return pl.BlockSpec(a.shape, lambda i: (0, 0))

    hbm = pl.BlockSpec(memory_space=pl.ANY)
    block = (rows_per, D_MODEL)
    return pl.pallas_call(
        body, name="bwd_dx", grid=(n_tiles,),
        in_specs=[tile(D_MODEL), tile(D_MODEL), whole(norm_gain), hbm, tile(IN_WIDTH), hbm,
                  whole(dqg), whole(dkg), whole(dsink), whole(loss_part)],
        out_specs=[tile(D_MODEL), hbm, hbm],
        out_shape=[jax.ShapeDtypeStruct((T_LOC, D_MODEL), F32), jax.ShapeDtypeStruct((3,) + block, BF16),
                   jax.ShapeDtypeStruct((N_DEV, 8, SMALL_W), F32)],
        scratch_shapes=[pltpu.VMEM((IN_WIDTH, D_MODEL), BF16), pltpu.VMEM((8, D_MODEL), F32),
                        pltpu.VMEM((8, SMALL_W), F32),
                        pltpu.VMEM((4,) + block, BF16), pltpu.VMEM((4,) + block, BF16), pltpu.VMEM((3,) + block, BF16),
                        pltpu.VMEM(block, BF16),
                        pltpu.SemaphoreType.DMA, pltpu.SemaphoreType.DMA((4,)), pltpu.SemaphoreType.DMA((4,)),
                        pltpu.SemaphoreType.DMA((3,)), pltpu.SemaphoreType.DMA((3,)), pltpu.SemaphoreType.DMA((4,)),
                        pltpu.SemaphoreType.DMA((7,)), pltpu.SemaphoreType.DMA((7,)), pltpu.SemaphoreType.DMA],
        compiler_params=_params(("arbitrary",)),
    )(x, dout, norm_gain, win_t, dproj, dwin_t, dqg, dkg, dsink, loss_part)


def _adamw(w, g, m, v):
    m = ADAM_B1 * m + (1.0 - ADAM_B1) * g
    v = ADAM_B2 * v + (1.0 - ADAM_B2) * (g * g)
    m_hat = m / (1.0 - ADAM_B1 ** ADAM_STEP)
    v_hat = v / (1.0 - ADAM_B2 ** ADAM_STEP)
    delta = -ADAM_LR * (m_hat / (jnp.sqrt(v_hat) + ADAM_EPS) + ADAM_WD * w)
    return delta, m, v


def _sum_slots(r_ref):
    g = r_ref[0].astype(F32)
    for s in range(1, r_ref.shape[0]):
        g = g + r_ref[s].astype(F32)
    return g


def adamw_all(r_win, r_out, r_small, big_in, big_out, weights, moments_m, moments_v):
    n = len(weights)
    params = [big_in[0], big_out[0], *weights]

    def body(rw_ref, ro_ref, rs_ref, *refs):
        n_p = n + 2
        ins, outs = refs[:3 * n_p], refs[3 * n_p:]
        s = _sum_slots(rs_ref)
        eye = (_row((8, SLAB)) == _lane((8, SLAB))).astype(F32)
        sinks = jnp.sum(s[:, 1280:1408] * eye, axis=0, keepdims=True)
        grads = [_sum_slots(rw_ref), _sum_slots(ro_ref),
                 s[0:1, :D_MODEL], s[0:1, 1024:1024 + HEAD_DIM], s[0:1, 1152:1152 + HEAD_DIM], sinks[:, :8]]
        for k in range(n_p):
            outs[k][...] = grads[k]
            outs[n_p + k][...], outs[2 * n_p + k][...], outs[3 * n_p + k][...] = _adamw(
                ins[k][...], grads[k], ins[n_p + k][...], ins[2 * n_p + k][...])
        loss = jnp.sum(jnp.sum(s[:, 1408:1536], axis=1, keepdims=True), axis=0, keepdims=True) * (0.5 / D_MODEL)
        outs[4 * n_p][...] = loss

    n_p = n + 2
    res = pl.pallas_call(
        body, name="adamw_all",
        out_shape=[jax.ShapeDtypeStruct(p.shape, F32) for p in params] * 4 + [jax.ShapeDtypeStruct((1, 1), F32)],
        compiler_params=pltpu.CompilerParams(vmem_limit_bytes=VMEM_LIMIT),
    )(r_win, r_out, r_small, big_in[0], big_out[0], *weights, big_in[1], big_out[1], *moments_m,
      big_in[2], big_out[2], *moments_v)
    return [res[k * n_p:(k + 1) * n_p] for k in range(4)], res[4 * n_p]


def kernel(x, positions, norm_gain, w_in, q_norm_gain, k_norm_gain, sinks, w_out, loss_target, m_norm_gain, m_w_in, m_q_norm_gain, m_k_norm_gain, m_sinks, m_w_out, v_norm_gain, v_w_in, v_q_norm_gain, v_k_norm_gain, v_sinks, v_w_out):
    x2 = x.reshape(T_LOC, D_MODEL)
    tgt2 = loss_target.reshape(T_LOC, D_MODEL)
    pos2 = positions.reshape(1, T_LOC)
    half = HEAD_DIM // 2
    inv_freq = ROPE_THETA ** (-jnp.arange(half, dtype=F32) * 2.0 / HEAD_DIM)
    inv_freq = jnp.tile(inv_freq, SLAB // half).reshape(SLAB, 1)
    sin_sign = jnp.tile(jnp.concatenate([-jnp.ones((half,), F32), jnp.ones((half,), F32)]), 2).reshape(1, SLAB)
    q_gain2 = jnp.tile(q_norm_gain, (1, 2))
    k_gain2 = jnp.tile(k_norm_gain, (1, 2))

    win_t = gather_weights(w_in.reshape(D_MODEL, IN_SHARD).T)

    (qa_raw, ka_raw, q_rot, k_dup, v_dup, ga, qb, kb, vb, gb, cos, sin_s, wout) = fwd_proj(
        x2, pos2, norm_gain, win_t, inv_freq, sin_sign, q_gain2, k_gain2, w_out.reshape(OUT_SHARD, D_MODEL).astype(BF16))
    o_a = swa_fwd(q_rot, k_dup, v_dup, sinks)
    o_b, carries = sb_fwd(qb, kb, vb)
    dout, d_oa, d_ob, dga, dgb, dwout, loss_part = out_loss(o_a, o_b, ga, gb, x2, tgt2, wout)
    dq_rot, dk_dup, dv_dup, dsink = swa_bwd(q_rot, k_dup, v_dup, o_a, d_oa, sinks)
    dqb, dkb, dvb, r_out = sb_bwd(qb, kb, vb, d_ob, carries, dwout)
    dproj, dwin_t, dqg, dkg = bwd_dw(
        x2, norm_gain, dq_rot, dk_dup, dv_dup, qa_raw, ka_raw, cos, sin_s, q_gain2, k_gain2, dga, dgb, dqb, dkb, dvb)
    grad_x, r_win, r_small = bwd_dx(x2, dout, norm_gain, win_t, dproj, dwin_t, dqg, dkg, dsink, loss_part)

    w_in2, m_in2, v_in2 = (a.reshape(D_MODEL, IN_SHARD).T for a in (w_in, m_w_in, v_w_in))
    w_out2, m_out2, v_out2 = (a.reshape(OUT_SHARD, D_MODEL) for a in (w_out, m_w_out, v_w_out))
    kinds, loss = adamw_all(
        r_win, r_out, r_small, (w_in2, m_in2, v_in2), (w_out2, m_out2, v_out2),
        (norm_gain, q_norm_gain, k_norm_gain, sinks),
        (m_norm_gain, m_q_norm_gain, m_k_norm_gain, m_sinks), (v_norm_gain, v_q_norm_gain, v_k_norm_gain, v_sinks))

    def leaves(k):
        big_in, big_out, ng, qg, kg, sk = kinds[k]
        return (ng, big_in.T.reshape(1, D_MODEL, IN_SHARD), qg, kg, sk, big_out.reshape(1, OUT_SHARD, D_MODEL))

    return (loss.reshape(()), grad_x.reshape(B_LOC, SEQ, D_MODEL), *leaves(0), *leaves(1), *leaves(2), *leaves(3))
```

```python
import functools

import jax
import jax.numpy as jnp
from jax import lax
from jax.experimental import pallas as pl
from jax.experimental.pallas import tpu as pltpu

F32 = jnp.float32
BF16 = jnp.bfloat16

N_DEV = 8
D_MODEL = 1024
SEQ = 2048
B_LOC = 2
T_LOC = B_LOC * SEQ
HEAD_DIM = 64
HEAD_SHIFT = 6
BLK = 128
N_BLK = SEQ // BLK
SLAB = 128
IN_WIDTH = 3328
IN_SHARD = IN_WIDTH // N_DEV
OUT_SHARD = D_MODEL // N_DEV
EPS = 1e-6
ROPE_THETA = 10000.0
Q_SCALE = 0.125
R_QA, R_KA, R_VA, R_GA, R_QB, R_KB, R_VB, R_GB, R_END = 0, 512, 640, 768, 1280, 1792, 2304, 2816, 3328
SMALL_W = 1536
ADAM_LR, ADAM_B1, ADAM_B2, ADAM_EPS, ADAM_WD, ADAM_STEP = 0.001, 0.9, 0.999, 1e-08, 0.01, 10
TM = 512
ACC_ROWS = 256
GATHER_CHUNKS = 2
VMEM_LIMIT = 56 * 1024 * 1024

MESH = pl.DeviceIdType.MESH
NT = (((1,), (1,)), ((), ()))
TN = (((0,), (0,)), ((), ()))


def _params(sem, limit=VMEM_LIMIT):
    return pltpu.CompilerParams(dimension_semantics=sem, vmem_limit_bytes=limit)


def _dot(a, b, dims=None):
    if dims is None:
        return jnp.dot(a, b, preferred_element_type=F32)
    return lax.dot_general(a, b, dims, preferred_element_type=F32)


def _lane(shape):
    return lax.broadcasted_iota(jnp.int32, shape, len(shape) - 1)


def _row(shape):
    return lax.broadcasted_iota(jnp.int32, shape, 0)


def _head_blockdiag():
    return ((_row((SLAB, SLAB)) >> HEAD_SHIFT) == (_lane((SLAB, SLAB)) >> HEAD_SHIFT)).astype(BF16)


def _head_sum(x, bd):
    return _dot(x.astype(BF16), bd)


def _swap_half(y, lane):
    return jnp.where((lane & 32) != 0, pltpu.roll(y, 32, 1), pltpu.roll(y, 96, 1))


def _stack_heads(q, lane):
    zero = jnp.zeros_like(q)
    return jnp.concatenate([jnp.where(lane < HEAD_DIM, q, zero), jnp.where(lane >= HEAD_DIM, q, zero)], axis=0)


def _unstack_heads(x2, lane):
    return jnp.where(lane < HEAD_DIM, x2[:BLK], x2[BLK:])


def _sigmoid(x):
    return 1.0 / (1.0 + jnp.exp(-x))


def _mesh_pos():
    return lax.axis_index("x"), lax.axis_index("y"), lax.axis_index("c")


def _flip(pos, mask):
    return tuple(1 - p if m else p for p, m in zip(pos, mask))


def _lin(pos):
    return 4 * pos[0] + 2 * pos[1] + pos[2]


DEV_FLIPS = [(fx, fy, fc) for fx in (0, 1) for fy in (0, 1) for fc in (0, 1)][1:]


def _direct_exchange(src_for, dst_slot, send_sems, recv_sems, local_sem):
    me = _mesh_pos()

    def copy(k, to):
        return pltpu.make_async_remote_copy(
            src_ref=src_for(to), dst_ref=dst_slot(me), send_sem=send_sems.at[k], recv_sem=recv_sems.at[k],
            device_id=to, device_id_type=MESH)

    def landed(k, frm):
        return pltpu.make_async_remote_copy(
            src_ref=src_for(frm), dst_ref=dst_slot(frm), send_sem=send_sems.at[k], recv_sem=recv_sems.at[k],
            device_id=frm, device_id_type=MESH)

    local = None if local_sem is None else pltpu.make_async_copy(src_for(me), dst_slot(me), local_sem)
    peers = [_flip(me, f) for f in DEV_FLIPS]

    def start():
        if local is not None:
            local.start()
        for k, to in enumerate(peers):
            copy(k, to).start()

    def finish():
        for k, frm in enumerate(peers):
            landed(k, frm).wait_recv()
        for k, to in enumerate(peers):
            copy(k, to).wait_send()
        if local is not None:
            local.wait()

    return start, finish


def gather_weights(shard):
    m = shard.shape[0]
    piece = m // GATHER_CHUNKS
    pieces = range(GATHER_CHUNKS)

    def body(f32_ref, o_ref, a_ref, ici_send, ici_recv, d2d_send, d2d_recv, local_sem):
        a_ref[...] = f32_ref[...].astype(BF16)
        x, y, c = _mesh_pos()
        me, sibling = (x, y, c), (x, y, 1 - c)
        chip_x, chip_y, chip_d = (1 - x, y), (x, 1 - y), (1 - x, 1 - y)

        def rows(pos, q):
            return o_ref.at[pl.ds(_lin(pos) * m + q * piece, piece), :]

        def own(q):
            return a_ref.at[pl.ds(q * piece, piece), :]

        def ici(k, q, block, to, src=None):
            return pltpu.make_async_remote_copy(
                src_ref=rows(block, q) if src is None else src, dst_ref=rows(block, q),
                send_sem=ici_send.at[k, q], recv_sem=ici_recv.at[k, q], device_id=to, device_id_type=MESH)

        def d2d(k, q, chip, mine, src=None):
            block = (*chip, c) if mine else (*chip, 1 - c)
            return pltpu.make_async_remote_copy(
                src_ref=rows(block, q) if src is None else src, dst_ref=rows(block, q),
                send_sem=d2d_send.at[k, q], recv_sem=d2d_recv.at[k, q], device_id=sibling, device_id_type=MESH)

        local = pltpu.make_async_copy(a_ref, o_ref.at[pl.ds(_lin(me) * m, m), :], local_sem)
        local.start()
        sends = []
        for q in pieces:
            sends += [ici(0, q, me, (*chip_x, c), src=own(q)), ici(1, q, me, (*chip_y, c), src=own(q)),
                      d2d(0, q, (x, y), True, src=own(q))]
        for cp in sends:
            cp.start()

        def pass_on(first, k_first, second, k_second, onward):
            moved = []
            for q in pieces:
                ici(k_first, q, (*first, c), me).wait_recv()
                moved += [ici(2, q, (*first, c), (*onward, c)), d2d(1 + k_first, q, first, True)]
                for cp in moved[-2:]:
                    cp.start()
            for q in pieces:
                ici(k_second, q, (*second, c), me).wait_recv()
                moved.append(d2d(1 + k_second, q, second, True))
                moved[-1].start()
            for q in pieces:
                ici(2, q, (*chip_d, c), me).wait_recv()
                moved.append(d2d(3, q, chip_d, True))
                moved[-1].start()
            for cp in moved:
                cp.wait_send()

        @pl.when(c == 0)
        def _():
            pass_on(chip_y, 1, chip_x, 0, chip_x)

        @pl.when(c == 1)
        def _():
            pass_on(chip_x, 0, chip_y, 1, chip_y)

        for k, chip in enumerate([(x, y), chip_x, chip_y, chip_d]):
            for q in pieces:
                d2d(k, q, chip, False).wait_recv()
        for cp in sends:
            cp.wait_send()
        local.wait()

    vmem = pl.BlockSpec(memory_space=pltpu.VMEM)
    n_q = GATHER_CHUNKS
    return pl.pallas_call(
        body, name="gather_weights",
        out_shape=jax.ShapeDtypeStruct((N_DEV * m, shard.shape[1]), BF16),
        in_specs=[vmem], out_specs=vmem,
        scratch_shapes=[pltpu.VMEM(shard.shape, BF16), pltpu.SemaphoreType.DMA((3, n_q)), pltpu.SemaphoreType.DMA((3, n_q)),
                        pltpu.SemaphoreType.DMA((4, n_q)), pltpu.SemaphoreType.DMA((4, n_q)), pltpu.SemaphoreType.DMA],
        compiler_params=pltpu.CompilerParams(vmem_limit_bytes=VMEM_LIMIT),
    )(shard)


def _norm_rope(xs, gain2, cos, sin_s, bd, lane):
    r = lax.rsqrt(_head_sum(xs * xs, bd) * (1.0 / HEAD_DIM) + EPS)
    y = xs * r * gain2
    return y * cos + _swap_half(y, lane) * sin_s


def _dup_heads(xs, lane):
    r = pltpu.roll(xs, HEAD_DIM, 1)
    lo = lane < HEAD_DIM
    return jnp.concatenate([jnp.where(lo, xs, r), jnp.where(lo, r, xs)], axis=1)


def fwd_proj(x, pos, norm_gain, win_t, inv_freq, sin_sign, q_gain2, k_gain2, wout_shard):
    n_tiles = T_LOC // TM

    def body(x_ref, pos_ref, ng_ref, w_ref, if_ref, sg_ref, qg_ref, kg_ref, ws_hbm,
             qa_raw_ref, ka_raw_ref, q_rot_ref, k_dup_ref, v_dup_ref, ga_ref, qb_ref, kb_ref, vb_ref, gb_ref,
             cos_ref, sin_ref, wo_hbm, wo_send, wo_recv, wo_local):
        start_wout, finish_wout = _direct_exchange(
            lambda dev: ws_hbm, lambda dev: wo_hbm.at[pl.ds(_lin(dev) * OUT_SHARD, OUT_SHARD), :],
            wo_send, wo_recv, wo_local)
        pl.when(pl.program_id(0) == 0)(start_wout)

        xv = x_ref[...]
        rstd = lax.rsqrt(jnp.mean(xv * xv, axis=-1, keepdims=True) + EPS)
        h = (xv * rstd * ng_ref[...]).astype(BF16)

        def proj(r0, r1):
            return _dot(h, w_ref[r0:r1, :], NT)

        ang_t = if_ref[...] * pos_ref[...].astype(F32)
        cos = jnp.cos(ang_t).T
        sin_s = jnp.sin(ang_t).T * sg_ref[...]
        cos_ref[...] = cos
        sin_ref[...] = sin_s
        lane = _lane((TM, SLAB))
        bd = _head_blockdiag()

        qa = proj(R_QA, R_KA)
        qa_raw_ref[...] = qa
        for p in range(4):
            sl = slice(p * SLAB, (p + 1) * SLAB)
            q_rot_ref[:, sl] = (_norm_rope(qa[:, sl], qg_ref[...], cos, sin_s, bd, lane) * Q_SCALE).astype(BF16)
        ka = proj(R_KA, R_VA)
        ka_raw_ref[...] = ka
        k_dup_ref[...] = _dup_heads(_norm_rope(ka, kg_ref[...], cos, sin_s, bd, lane), lane).astype(BF16)
        v_dup_ref[...] = _dup_heads(proj(R_VA, R_GA), lane).astype(BF16)
        ga_ref[...] = proj(R_GA, R_QB).astype(BF16)
        qb_ref[...] = (proj(R_QB, R_KB) * Q_SCALE).astype(BF16)
        kb_ref[...] = proj(R_KB, R_VB).astype(BF16)
        vb_ref[...] = proj(R_VB, R_GB).astype(BF16)
        gb_ref[...] = proj(R_GB, R_END).astype(BF16)
        pl.when(pl.program_id(0) == n_tiles - 1)(finish_wout)

    def tile(w):
        return pl.BlockSpec((TM, w), lambda i: (i, 0))

    def whole(a):
        return pl.BlockSpec(a.shape, lambda i: (0, 0))

    hbm = pl.BlockSpec(memory_space=pl.ANY)
    widths = [(512, F32), (128, F32), (512, BF16), (256, BF16), (256, BF16), (512, BF16), (512, BF16), (512, BF16),
              (512, BF16), (512, BF16), (128, F32), (128, F32)]
    return pl.pallas_call(
        body, name="fwd_proj", grid=(n_tiles,),
        in_specs=[tile(D_MODEL), pl.BlockSpec((1, TM), lambda i: (0, i)), whole(norm_gain), whole(win_t),
                  whole(inv_freq), whole(sin_sign),
                  whole(q_gain2), whole(k_gain2), hbm],
        out_specs=[tile(w) for w, _ in widths] + [hbm],
        out_shape=[jax.ShapeDtypeStruct((T_LOC, w), dt) for w, dt in widths]
        + [jax.ShapeDtypeStruct((D_MODEL, D_MODEL), BF16)],
        scratch_shapes=[pltpu.SemaphoreType.DMA((7,)), pltpu.SemaphoreType.DMA((7,)), pltpu.SemaphoreType.DMA],
        compiler_params=_params(("arbitrary",)),
    )(x, pos, norm_gain, win_t, inv_freq, sin_sign, q_gain2, k_gain2, wout_shard)


def _swa_window(prev_ref, cur_ref, p):
    gsl = _slab(p // 2)
    return jnp.concatenate([prev_ref[:, gsl], cur_ref[:, gsl]], axis=0)


def _swa_probs(s, sinks_ref, p, i):
    shape = (2 * BLK, 2 * BLK)
    r = _row(shape) & (BLK - 1)
    cidx = _lane(shape)
    valid = (cidx > r) & (cidx <= r + BLK) & ((cidx >= BLK) | (i > 0))
    s = jnp.where(valid, s, -jnp.inf)
    sink = jnp.where(_row((2 * BLK, 1)) < BLK, sinks_ref[0, 2 * p], sinks_ref[0, 2 * p + 1])
    m = jnp.maximum(jnp.max(s, axis=-1, keepdims=True), sink)
    e = jnp.exp(s - m)
    e_sink = jnp.exp(sink - m)
    den = jnp.sum(e, axis=-1, keepdims=True) + e_sink
    return e / den, e_sink / den


SWA_CHAINS = [(b, p) for b in range(B_LOC) for p in range(4)]


def _swa_specs():
    def cur(w):
        return pl.BlockSpec((B_LOC, BLK, w), lambda i: (0, i, 0))

    def prev(w):
        return pl.BlockSpec((B_LOC, BLK, w), lambda i: (0, jnp.maximum(i - 1, 0), 0))

    return cur, prev


def swa_fwd(q_rot, k_dup, v_dup, sinks):
    def body(q_ref, kp_ref, kc_ref, vp_ref, vc_ref, sinks_ref, o_ref):
        i = pl.program_id(0)
        lane = _lane((BLK, SLAB))
        s = [_dot(_stack_heads(q_ref[b, :, _slab(p)], lane), _swa_window(kp_ref.at[b], kc_ref.at[b], p), NT)
             for b, p in SWA_CHAINS]
        pn = [_swa_probs(s[c], sinks_ref, p, i)[0].astype(BF16) for c, (b, p) in enumerate(SWA_CHAINS)]
        for c, (b, p) in enumerate(SWA_CHAINS):
            o = _unstack_heads(_dot(pn[c], _swa_window(vp_ref.at[b], vc_ref.at[b], p)), lane)
            o_ref[b, :, _slab(p)] = o.astype(BF16)

    cur, prev = _swa_specs()
    q3, k3, v3 = (a.reshape(B_LOC, SEQ, a.shape[1]) for a in (q_rot, k_dup, v_dup))
    return pl.pallas_call(
        body, name="swa_fwd", grid=(N_BLK,),
        in_specs=[cur(512), prev(256), cur(256), prev(256), cur(256), pl.BlockSpec(memory_space=pltpu.SMEM)],
        out_specs=cur(512),
        out_shape=jax.ShapeDtypeStruct((B_LOC, SEQ, 512), BF16),
        compiler_params=_params(("arbitrary",)),
    )(q3, k3, k3, v3, v3, sinks).reshape(T_LOC, 512)


def _tri(suffix):
    r, cidx = _row((BLK + 16, BLK)), _lane((BLK + 16, BLK))
    tri = (cidx > r) if suffix else (cidx < r)
    return (tri | (r >= BLK)).astype(BF16)


def _key_sums(tri, x):
    res = _dot(tri, x.astype(BF16))
    return res[:BLK], res[BLK:BLK + 1]


def _sb_softplus(zt, valid):
    neg_abs = lax.bitcast_convert_type(lax.bitcast_convert_type(zt, jnp.uint32) | jnp.uint32(0x80000000), F32)
    sp = jnp.maximum(zt, 0.0) + jnp.log(1.0 + jnp.exp(neg_abs))
    return sp if valid is None else jnp.where(valid, sp, 0.0)


def _sb_weights(zt, sp, later, valid):
    w = jnp.exp(zt - sp - later)
    return w if valid is None else jnp.where(valid, w, 0.0)


def _slab(pp):
    return slice(pp * SLAB, (pp + 1) * SLAB)


def _blk(j):
    return pl.ds(pl.multiple_of(j * BLK, BLK), BLK)


def _causal_t():
    return _row((BLK, 2 * BLK)) < (_lane((BLK, 2 * BLK)) & (BLK - 1))


def _sb_rows(b, j):
    return pl.ds(pl.multiple_of(b * SEQ + j * BLK, BLK), BLK)


SB_CHAINS = [(b, pp) for b in range(B_LOC) for pp in range(4)]


def sb_fwd(qb, kb, vb):
    def body(q_ref, k_ref, v_ref, o_ref, c_ref, vt_ref, ot_ref):
        for c, (b, pp) in enumerate(SB_CHAINS):
            for j in range(N_BLK):
                vt_ref[c, j] = v_ref[b * SEQ + j * BLK:b * SEQ + (j + 1) * BLK, _slab(pp)].T
        lane = _lane((BLK, SLAB))
        tri = _tri(True)
        valid = _causal_t()
        jrow = _row((N_BLK, 2 * BLK))
        chains = range(len(SB_CHAINS))

        def q_block(i, _):
            q2 = [_stack_heads(q_ref[_sb_rows(b, i), _slab(pp)], lane) for b, pp in SB_CHAINS]

            def key_block(j, carry, mask, first):
                zt = [_dot(k_ref[_sb_rows(b, j), _slab(pp)], q2[c], NT) for c, (b, pp) in enumerate(SB_CHAINS)]
                sp = [_sb_softplus(zt[c], mask) for c in chains]
                sums = [_key_sums(tri, sp[c]) for c in chains]
                w = [_sb_weights(zt[c], sp[c], sums[c][0] + carry[c], mask) for c in chains]
                for c in chains:
                    pv = _dot(vt_ref[c, j], w[c].astype(BF16))
                    if first:
                        ot_ref[c] = pv
                    else:
                        ot_ref[c] += pv
                return tuple(carry[c] + sums[c][1] for c in chains)

            def earlier(jj, state):
                carry, saved = state
                j = i - 1 - jj
                saved = tuple(jnp.where(jrow == j, carry[c], saved[c]) for c in chains)
                return key_block(j, carry, None, False), saved

            zero = tuple(jnp.zeros((1, 2 * BLK), F32) for _ in chains)
            carry = key_block(i, zero, valid, True)
            _, saved = lax.fori_loop(0, i, earlier, (carry, tuple(jnp.zeros((N_BLK, 2 * BLK), F32) for _ in chains)))
            for c, (b, pp) in enumerate(SB_CHAINS):
                o_ref[_sb_rows(b, i), _slab(pp)] = _unstack_heads(ot_ref[c].T, lane).astype(BF16)
                c_ref[c * N_BLK + i] = saved[c]
            return 0

        lax.fori_loop(0, N_BLK, q_block, 0)

    n_ch = len(SB_CHAINS)
    vmem = pl.BlockSpec(memory_space=pltpu.VMEM)
    return pl.pallas_call(
        body, name="sb_fwd",
        in_specs=[vmem] * 3, out_specs=[vmem] * 2,
        out_shape=[jax.ShapeDtypeStruct((T_LOC, 512), BF16), jax.ShapeDtypeStruct((n_ch * N_BLK, N_BLK, 2 * BLK), F32)],
        scratch_shapes=[pltpu.VMEM((n_ch, N_BLK, SLAB, BLK), BF16), pltpu.VMEM((n_ch, SLAB, 2 * BLK), F32)],
        compiler_params=pltpu.CompilerParams(vmem_limit_bytes=VMEM_LIMIT),
    )(qb, kb, vb)


def out_loss(o_a, o_b, ga, gb, x, target, wout):
    n_tiles = T_LOC // TM

    def body(oa_ref, ob_ref, ga_ref, gb_ref, x_ref, t_ref, w_ref,
             dout_ref, doa_ref, dob_ref, dga_ref, dgb_ref, dw_ref, loss_ref, acc_ref):
        step = pl.program_id(0)

        @pl.when(step == 0)
        def _():
            acc_ref[...] = jnp.zeros_like(acc_ref)
            loss_ref[...] = jnp.zeros_like(loss_ref)

        halves = [slice(k * (TM // 2), (k + 1) * (TM // 2)) for k in range(2)]
        gate = []
        for rows in halves:
            oa, ob, gav, gbv = (r[rows, :].astype(F32) for r in (oa_ref, ob_ref, ga_ref, gb_ref))
            sa, sb = _sigmoid(gav), _sigmoid(gbv)
            gate.append((oa, ob, gav, gbv, sa, sb, gav * sa, gbv * sb))
        y = [jnp.concatenate([g[0] * g[6], g[1] * g[7]], axis=1).astype(BF16) for g in gate]
        out = [_dot(y[k], w_ref[...]) for k in range(2)]
        dob16 = []
        for k, rows in enumerate(halves):
            err = x_ref[rows, :] + out[k] - t_ref[rows, :]
            e2 = err * err
            part = jnp.sum(e2.reshape(TM // 16, 8, D_MODEL), axis=0)
            loss_ref[...] += functools.reduce(lambda a, b: a + b, [part[:, j * 128:(j + 1) * 128] for j in range(8)])
            dout = err * (1.0 / D_MODEL)
            dout_ref[rows, :] = dout
            dob16.append(dout.astype(BF16))
        dy = [_dot(dob16[k], w_ref[...], NT) for k in range(2)]
        for k in range(2):
            for r0 in range(0, D_MODEL, ACC_ROWS):
                acc_ref[r0:r0 + ACC_ROWS, :] += _dot(y[k][:, r0:r0 + ACC_ROWS], dob16[k], TN)
        for k, rows in enumerate(halves):
            oa, ob, gav, gbv, sa, sb, silu_a, silu_b = gate[k]
            dya, dyb = dy[k][:, :512], dy[k][:, 512:]
            doa_ref[rows, :] = (dya * silu_a).astype(BF16)
            dob_ref[rows, :] = (dyb * silu_b).astype(BF16)
            dga_ref[rows, :] = (dya * oa * (sa * (1.0 + gav * (1.0 - sa)))).astype(BF16)
            dgb_ref[rows, :] = (dyb * ob * (sb * (1.0 + gbv * (1.0 - sb)))).astype(BF16)

        @pl.when(step == n_tiles - 1)
        def _():
            dw_ref[...] = acc_ref[...].astype(BF16)

    def tile(w):
        return pl.BlockSpec((TM, w), lambda i: (i, 0))

    const = lambda i: (0, 0)
    return pl.pallas_call(
        body, name="out_loss", grid=(n_tiles,),
        in_specs=[tile(512)] * 4 + [tile(D_MODEL)] * 2 + [pl.BlockSpec((D_MODEL, D_MODEL), const)],
        out_specs=[tile(D_MODEL), tile(512), tile(512), tile(512), tile(512),
                   pl.BlockSpec((D_MODEL, D_MODEL), const), pl.BlockSpec((8, 128), const)],
        out_shape=[jax.ShapeDtypeStruct((T_LOC, D_MODEL), F32)] + [jax.ShapeDtypeStruct((T_LOC, 512), BF16)] * 4
        + [jax.ShapeDtypeStruct((D_MODEL, D_MODEL), BF16), jax.ShapeDtypeStruct((8, 128), F32)],
        scratch_shapes=[pltpu.VMEM((D_MODEL, D_MODEL), F32)],
        compiler_params=_params(("arbitrary",)),
    )(o_a, o_b, ga, gb, x, target, wout)


def swa_bwd(q_rot, k_dup, v_dup, o_a, d_oa, sinks):
    def body(q_ref, kp_ref, kc_ref, vp_ref, vc_ref, o_ref, do_ref, sinks_ref, dq_ref, dk_ref, dv_ref, dsink_ref):
        i = pl.program_id(0)

        @pl.when(i == 0)
        def _():
            dk_ref[...] = jnp.zeros_like(dk_ref)
            dv_ref[...] = jnp.zeros_like(dv_ref)
            dsink_ref[...] = jnp.zeros_like(dsink_ref)

        lane = _lane((BLK, SLAB))
        rows_prev, rows_cur = _blk(jnp.maximum(i - 1, 0)), _blk(i)
        chains = range(len(SWA_CHAINS))
        q2 = [_stack_heads(q_ref[b, :, _slab(p)], lane) for b, p in SWA_CHAINS]
        do2 = [_stack_heads(do_ref[b, :, _slab(p)], lane) for b, p in SWA_CHAINS]
        keys = [_swa_window(kp_ref.at[b], kc_ref.at[b], p) for b, p in SWA_CHAINS]
        s = [_dot(q2[c], keys[c], NT) for c in chains]
        dp = [_dot(do2[c], _swa_window(vp_ref.at[b], vc_ref.at[b], p), NT) for c, (b, p) in enumerate(SWA_CHAINS)]
        ds, pn16, cols = [], [], []
        for c, (b, p) in enumerate(SWA_CHAINS):
            pn, p_sink = _swa_probs(s[c], sinks_ref, p, i)
            o = o_ref[b, :, _slab(p)].astype(F32)
            delta =jnp.sum(do2[c].astype(F32) * jnp.concatenate([o, o], axis=0), axis=-1, keepdims=True)
            ds.append((pn * (dp[c] - delta)).astype(BF16))
            pn16.append(pn.astype(BF16))
            cols.append(-p_sink * delta)
        for c, (b, p) in enumerate(SWA_CHAINS):
            dq_ref[b, :, _slab(p)] = _unstack_heads(_dot(ds[c], keys[c]), lane) * Q_SCALE
        dk2 = [_dot(ds[c], q2[c], TN) for c in chains]
        dv2 = [_dot(pn16[c], do2[c], TN) for c in chains]
        for c, (b, p) in enumerate(SWA_CHAINS):
            gsl = _slab(p // 2)
            dk_ref[b, rows_prev, gsl] += dk2[c][:BLK]
            dk_ref[b, rows_cur, gsl] += dk2[c][BLK:]
            dv_ref[b, rows_prev, gsl] += dv2[c][:BLK]
            dv_ref[b, rows_cur, gsl] += dv2[c][BLK:]
            for e in range(2):
                dsink_ref[2 * p + e:2 * p + e + 1, :] += jnp.sum(cols[c][e * BLK:(e + 1) * BLK], axis=0, keepdims=True)

    cur, prev = _swa_specs()
    whole = pl.BlockSpec((B_LOC, SEQ, 256), lambda i: (0, 0, 0))
    q3, k3, v3, o3, do3 = (a.reshape(B_LOC, SEQ, a.shape[1]) for a in (q_rot, k_dup, v_dup, o_a, d_oa))
    dq, dk, dv, dsink = pl.pallas_call(
        body, name="swa_bwd", grid=(N_BLK,),
        in_specs=[cur(512), prev(256), cur(256), prev(256), cur(256), cur(512), cur(512),
                  pl.BlockSpec(memory_space=pltpu.SMEM)],
        out_specs=[cur(512), whole, whole, pl.BlockSpec((8, 128), lambda i: (0, 0))],
        out_shape=[jax.ShapeDtypeStruct((B_LOC, SEQ, 512), F32), jax.ShapeDtypeStruct((B_LOC, SEQ, 256), F32),
                   jax.ShapeDtypeStruct((B_LOC, SEQ, 256), F32), jax.ShapeDtypeStruct((8, 128), F32)],
        compiler_params=_params(("arbitrary",)),
    )(q3, k3, k3, v3, v3, o3, do3, sinks)
    return dq.reshape(T_LOC, 512), dk.reshape(T_LOC, 256), dv.reshape(T_LOC, 256), dsink


def sb_bwd(qb, kb, vb, d_ob, carries, dwout):
    def body(q_ref, k_ref, v_ref, do_ref, c_ref, dw_hbm, dq_ref, dk_ref, dv_ref, rw_hbm, kt_ref, dqt_ref,
             rw_send, rw_recv, rw_local):
        start_dwout, finish_dwout = _direct_exchange(
            lambda dev: dw_hbm.at[pl.ds(_lin(dev) * OUT_SHARD, OUT_SHARD), :], lambda dev: rw_hbm.at[_lin(dev)],
            rw_send, rw_recv, rw_local)
        start_dwout()
        for c, (b, pp) in enumerate(SB_CHAINS):
            for j in range(N_BLK):
                kt_ref[c, j] = k_ref[b * SEQ + j * BLK:b * SEQ + (j + 1) * BLK, _slab(pp)].T
        dk_ref[...] = jnp.zeros_like(dk_ref)
        dv_ref[...] = jnp.zeros_like(dv_ref)
        dqt_ref[...] = jnp.zeros_like(dqt_ref)
        lane = _lane((BLK, SLAB))
        tri_after, tri_before = _tri(True), _tri(False)
        valid = _causal_t()
        jrow = _row((N_BLK, 2 * BLK))
        chains = range(len(SB_CHAINS))

        def q_block(i, _):
            q2 = [_stack_heads(q_ref[_sb_rows(b, i), _slab(pp)], lane) for b, pp in SB_CHAINS]
            do2 = [_stack_heads(do_ref[_sb_rows(b, i), _slab(pp)], lane) for b, pp in SB_CHAINS]

            def key_block(j, carry_sp, before_u, mask):
                at = [(_sb_rows(b, j), _slab(pp)) for b, pp in SB_CHAINS]
                zt = [_dot(k_ref[at[c]], q2[c], NT) for c in chains]
                dw = [_dot(v_ref[at[c]], do2[c], NT) for c in chains]
                sp = [_sb_softplus(zt[c], mask) for c in chains]
                later = [_key_sums(tri_after, sp[c])[0] for c in chains]
                w = [_sb_weights(zt[c], sp[c], later[c] + carry_sp[c], mask) for c in chains]
                u = [dw[c] * w[c] for c in chains]
                for c in chains:
                    dv_ref[at[c]] += _dot(w[c].astype(BF16), do2[c])
                sums = [_key_sums(tri_before, u[c]) for c in chains]
                dz16 = []
                for c in chains:
                    sig = jnp.exp(zt[c] - sp[c])
                    dz = u[c] - sig * (u[c] + before_u[c] + sums[c][0])
                    if mask is not None:
                        dz = jnp.where(mask, dz, 0.0)
                    dz16.append(dz.astype(BF16))
                for c in chains:
                    dk_ref[at[c]] += _dot(dz16[c], q2[c])
                    dqt_ref[c] += _dot(kt_ref[c, j], dz16[c])
                return tuple(before_u[c] + sums[c][1] for c in chains)

            def earlier(j, before_u):
                carry_sp = [jnp.sum(jnp.where(jrow == j, c_ref[c * N_BLK + i], 0.0), axis=0, keepdims=True)
                            for c in chains]
                return key_block(j, carry_sp, before_u, None)

            zero = tuple(jnp.zeros((1, 2 * BLK), F32) for _ in chains)
            before_u = lax.fori_loop(0, i, earlier, zero)
            key_block(i, zero, before_u, valid)
            for c, (b, pp) in enumerate(SB_CHAINS):
                dq_ref[_sb_rows(b, i), _slab(pp)] = (_unstack_heads(dqt_ref[c].T, lane) * Q_SCALE).astype(BF16)
                dqt_ref[c] = jnp.zeros((SLAB, 2 * BLK), F32)
            return 0

        lax.fori_loop(0, N_BLK, q_block, 0)
        finish_dwout()

    n_ch = len(SB_CHAINS)
    vmem, hbm = pl.BlockSpec(memory_space=pltpu.VMEM), pl.BlockSpec(memory_space=pl.ANY)
    return pl.pallas_call(
        body, name="sb_bwd",
        in_specs=[vmem] * 5 + [hbm], out_specs=[vmem] * 3 + [hbm],
        out_shape=[jax.ShapeDtypeStruct((T_LOC, 512), BF16)] + [jax.ShapeDtypeStruct((T_LOC, 512), F32)] * 2
        + [jax.ShapeDtypeStruct((N_DEV, OUT_SHARD, D_MODEL), BF16)],
        scratch_shapes=[pltpu.VMEM((n_ch, N_BLK, SLAB, BLK), BF16), pltpu.VMEM((n_ch, SLAB, 2 * BLK), F32),
                        pltpu.SemaphoreType.DMA((7,)), pltpu.SemaphoreType.DMA((7,)), pltpu.SemaphoreType.DMA],
        compiler_params=pltpu.CompilerParams(vmem_limit_bytes=VMEM_LIMIT),
    )(qb, kb, vb, d_ob, carries, dwout)


def bwd_dw(x, norm_gain, dq_rot, dk_dup, dv_dup, qa_raw, ka_raw, cos, sin_s, q_gain2, k_gain2, dga, dgb, dqb, dkb, dvb):
    n_tiles = T_LOC // TM

    def body(x_ref, ng_ref, dq_ref, dk_ref, dv_ref, qa_ref, ka_ref, cos_ref, sin_ref, qg_ref, kg_ref,
             dga_ref, dgb_ref, dqb_ref, dkb_ref, dvb_ref,
             dproj_ref, dw_hbm, dqg_ref, dkg_ref, acc_ref, stage_ref):
        step = pl.program_id(0)

        @pl.when(step == 0)
        def _():
            acc_ref[...] = jnp.zeros_like(acc_ref)
            dqg_ref[...] = jnp.zeros_like(dqg_ref)
            dkg_ref[...] = jnp.zeros_like(dkg_ref)

        lane = _lane((TM, SLAB))
        bd = _head_blockdiag()
        cos, sin_s = cos_ref[...], sin_ref[...]

        def norm_rope_bwd(d_rot, raw, gain2):
            dy = d_rot * cos + _swap_half(d_rot * sin_s, lane)
            r = lax.rsqrt(_head_sum(raw * raw, bd) * (1.0 / HEAD_DIM) + EPS)
            xhat = raw * r
            dgain = jnp.sum(dy * xhat, axis=0, keepdims=True)
            dxh = dy * gain2
            mean = _head_sum(dxh * xhat, bd) * (1.0 / HEAD_DIM)
            return r * (dxh - xhat * mean), dgain

        def fold_dup(d_dup):
            a, b2 = d_dup[:, :SLAB], d_dup[:, SLAB:]
            return jnp.where(lane < HEAD_DIM, a + pltpu.roll(a, HEAD_DIM, 1), b2 + pltpu.roll(b2, HEAD_DIM, 1))

        pieces = []
        dqg = jnp.zeros((1, SLAB), F32)
        for p in range(4):
            sl = slice(p * SLAB, (p + 1) * SLAB)
            d_raw, dg = norm_rope_bwd(dq_ref[:, sl], qa_ref[:, sl], qg_ref[...])
            pieces.append(d_raw.astype(BF16))
            dqg = dqg + dg
        d_raw, dkg = norm_rope_bwd(fold_dup(dk_ref[...]), ka_ref[...], kg_ref[...])
        pieces.append(d_raw.astype(BF16))
        pieces.append(fold_dup(dv_ref[...]).astype(BF16))
        pieces += [dga_ref[...], dqb_ref[...], dkb_ref[...].astype(BF16), dvb_ref[...].astype(BF16),
                   dgb_ref[...]]
        dproj = jnp.concatenate(pieces, axis=1)
        dproj_ref[...] = dproj
        dqg_ref[0:1, :] += dqg + pltpu.roll(dqg, HEAD_DIM, 1)
        dkg_ref[0:1, :] += dkg + pltpu.roll(dkg, HEAD_DIM, 1)

        xv = x_ref[...]
        rstd = lax.rsqrt(jnp.mean(xv * xv, axis=-1, keepdims=True) + EPS)
        h = (xv * rstd * ng_ref[...]).astype(BF16)
        for r0 in range(0, IN_WIDTH, ACC_ROWS):
            acc_ref[r0:r0 + ACC_ROWS, :] += _dot(dproj[:, r0:r0 + ACC_ROWS], h, TN)

        @pl.when(step == n_tiles - 1)
        def _():
            for r0 in range(0, IN_WIDTH, ACC_ROWS):
                stage_ref[...] = acc_ref[r0:r0 + ACC_ROWS, :].astype(BF16)
                pltpu.sync_copy(stage_ref, dw_hbm.at[r0:r0 + ACC_ROWS, :])

    def tile(w):
        return pl.BlockSpec((TM, w), lambda i: (i, 0))

    def whole(a):
        return pl.BlockSpec(a.shape, lambda i: (0, 0))

    const = lambda i: (0, 0)
    return pl.pallas_call(
        body, name="bwd_dw", grid=(n_tiles,),
        in_specs=[tile(D_MODEL), whole(norm_gain),
                  tile(512), tile(256), tile(256), tile(512), tile(128), tile(128), tile(128),
                  whole(q_gain2), whole(k_gain2), tile(512), tile(512), tile(512), tile(512), tile(512)],
        out_specs=[tile(IN_WIDTH), pl.BlockSpec(memory_space=pl.ANY),
                   pl.BlockSpec((8, SLAB), const), pl.BlockSpec((8, SLAB), const)],
        out_shape=[jax.ShapeDtypeStruct((T_LOC, IN_WIDTH), BF16), jax.ShapeDtypeStruct((IN_WIDTH, D_MODEL), BF16),
                   jax.ShapeDtypeStruct((8, SLAB), F32), jax.ShapeDtypeStruct((8, SLAB), F32)],
        scratch_shapes=[pltpu.VMEM((IN_WIDTH, D_MODEL), F32), pltpu.VMEM((ACC_ROWS, D_MODEL), BF16)],
        compiler_params=_params(("arbitrary",)),
    )(x, norm_gain, dq_rot, dk_dup, dv_dup, qa_raw, ka_raw, cos, sin_s, q_gain2, k_gain2, dga, dgb, dqb, dkb, dvb)


def bwd_dx(x, dout, norm_gain, win_t, dproj, dwin_t, dqg, dkg, dsink, loss_part):
    n_tiles = T_LOC // TM
    rows_per = IN_SHARD
    step_sums, step_merge = 1, 3

    def body(x_ref, dout_ref, ng_ref, w_hbm, dp_ref, a_hbm, dqg_ref, dkg_ref, dsink_ref, loss_ref,
             gx_ref, ra_hbm, rs_hbm, w_ref, dng_ref, s_ref, own_ref, sib_ref, snd_ref, extra_ref,
             w_sem, d2d_send, d2d_recv, ici_send, ici_recv, own_sems, s_send, s_recv, out_sem):
        step = pl.program_id(0)
        x, y, c = _mesh_pos()
        me, sibling = (x, y, c), (x, y, 1 - c)
        chips = {"own": (x, y), "x": (1 - x, y), "y": (x, 1 - y), "d": (1 - x, 1 - y)}
        index = {"own": 0, "x": 1, "y": 2, "d": 3}
        order = ("d", "x", "y", "own")

        def rows(pos):
            return a_hbm.at[pl.ds(_lin(pos) * rows_per, rows_per), :]

        def to_sibling(k):
            return pltpu.make_async_remote_copy(
                src_ref=rows((*chips[k], 1 - c)), dst_ref=sib_ref.at[index[k]],
                send_sem=d2d_send.at[index[k]], recv_sem=d2d_recv.at[index[k]], device_id=sibling, device_id_type=MESH)

        def mine(k):
            return pltpu.make_async_copy(rows((*chips[k], c)), own_ref.at[index[k]], own_sems.at[index[k]])

        def ici(n, to_chip, dst):
            return pltpu.make_async_remote_copy(
                src_ref=snd_ref.at[n], dst_ref=dst, send_sem=ici_send.at[n], recv_sem=ici_recv.at[n],
                device_id=(*chips[to_chip], c), device_id_type=MESH)

        def chip_sum(k):
            to_sibling(k).wait_recv()
            mine(k).wait()
            return own_ref[index[k]].astype(F32) + sib_ref[index[k]].astype(F32)

        def by_core(fn):
            pl.when(c == 0)(lambda: fn("x", "y"))
            pl.when(c == 1)(lambda: fn("y", "x"))

        @pl.when(step == 0)
        def _():
            cp = pltpu.make_async_copy(w_hbm, w_ref, w_sem)
            cp.start()
            for k in order:
                to_sibling(k).start()
                mine(k).start()
            dng_ref[...] = jnp.zeros_like(dng_ref)
            cp.wait()

        @pl.when(step == step_sums)
        def _():
            def first_sends(direct, via):
                snd_ref[0] = chip_sum("d").astype(BF16)
                ici(0, direct, extra_ref).start()
                snd_ref[1] = chip_sum(direct).astype(BF16)
                ici(1, direct, ra_hbm.at[index[direct]]).start()
            by_core(first_sends)

        @pl.when(step == step_merge)
        def _():
            def merge(direct, via):
                merged = chip_sum(via)
                ici(0, direct, extra_ref).wait_recv()
                snd_ref[2] = (merged + extra_ref[...].astype(F32)).astype(BF16)
                ici(2, via, ra_hbm.at[index[via]]).start()
                own_ref[0] = chip_sum("own").astype(BF16)
                pltpu.make_async_copy(own_ref.at[0], ra_hbm.at[0], out_sem).start()
            by_core(merge)

        halves = [slice(k * (TM // 2), (k + 1) * (TM // 2)) for k in range(2)]
        gain = ng_ref[...]
        dh = [_dot(dp_ref[half, :], w_ref[...]) for half in halves]
        for k, half in enumerate(halves):
            xv = x_ref[half, :]
            rstd = lax.rsqrt(jnp.mean(xv * xv, axis=-1, keepdims=True) + EPS)
            xhat = xv * rstd
            dng_ref[0:1, :] += jnp.sum(dh[k] * xhat, axis=0, keepdims=True)
            dxh = dh[k] * gain
            gx_ref[half, :] = dout_ref[half, :] + rstd * (dxh - xhat * jnp.mean(dxh * xhat, axis=-1, keepdims=True))

        @pl.when(step == n_tiles - 1)
        def _():
            s_ref[...] = jnp.concatenate(
                [dng_ref[...], dqg_ref[...], dkg_ref[...], dsink_ref[...], loss_ref[...]], axis=1)
            start_small, finish_small = _direct_exchange(
                lambda dev: s_ref, lambda dev: rs_hbm.at[_lin(dev)], s_send, s_recv, out_sem)

            def finish(direct, via):
                ici(1, direct, ra_hbm.at[index[direct]]).wait_recv()
                ici(2, via, ra_hbm.at[index[via]]).wait_recv()
                for n, to in ((0, direct), (1, direct), (2, via)):
                    ici(n, to, extra_ref).wait_send()
            by_core(finish)
            pltpu.make_async_copy(own_ref.at[0], ra_hbm.at[0], out_sem).wait()
            for k in order:
                to_sibling(k).wait_send()
            start_small()
            finish_small()

    def tile(w):
        return pl.BlockSpec((TM, w), lambda i: (i, 0))

    def whole(a):
        return pl.BlockSpec(a.shape, lambda i: (0, 0))

    hbm = pl.BlockSpec(memory_space=pl.ANY)
    block = (rows_per, D_MODEL)
    return pl.pallas_call(
        body, name="bwd_dx", grid=(n_tiles,),
        in_specs=[tile(D_MODEL), tile(D_MODEL), whole(norm_gain), hbm, tile(IN_WIDTH), hbm,
                  whole(dqg), whole(dkg), whole(dsink), whole(loss_part)],
        out_specs=[tile(D_MODEL), hbm, hbm],
        out_shape=[jax.ShapeDtypeStruct((T_LOC, D_MODEL), F32), jax.ShapeDtypeStruct((3,) + block, BF16),
                   jax.ShapeDtypeStruct((N_DEV, 8, SMALL_W), F32)],
        scratch_shapes=[pltpu.VMEM((IN_WIDTH, D_MODEL), BF16), pltpu.VMEM((8, D_MODEL), F32),
                        pltpu.VMEM((8, SMALL_W), F32),
                        pltpu.VMEM((4,) + block, BF16), pltpu.VMEM((4,) + block, BF16), pltpu.VMEM((3,) + block, BF16),
                        pltpu.VMEM(block, BF16),
                        pltpu.SemaphoreType.DMA, pltpu.SemaphoreType.DMA((4,)), pltpu.SemaphoreType.DMA((4,)),
                        pltpu.SemaphoreType.DMA((3,)), pltpu.SemaphoreType.DMA((3,)), pltpu.SemaphoreType.DMA((4,)),
                        pltpu.SemaphoreType.DMA((7,)), pltpu.SemaphoreType.DMA((7,)), pltpu.SemaphoreType.DMA],
        compiler_params=_params(("arbitrary",)),
    )(x, dout, norm_gain, win_t, dproj, dwin_t, dqg, dkg, dsink, loss_part)


def _adamw(w, g, m, v):
    m = ADAM_B1 * m + (1.0 - ADAM_B1) * g
    v = ADAM_B2 * v + (1.0 - ADAM_B2) * (g * g)
    m_hat = m / (1.0 - ADAM_B1 ** ADAM_STEP)
    v_hat = v / (1.0 - ADAM_B2 ** ADAM_STEP)
    delta = -ADAM_LR * (m_hat / (jnp.sqrt(v_hat) + ADAM_EPS) + ADAM_WD * w)
    return delta, m, v


def _sum_slots(r_ref):
    g = r_ref[0].astype(F32)
    for s in range(1, r_ref.shape[0]):
        g = g + r_ref[s].astype(F32)
    return g


def adamw_all(r_win, r_out, r_small, big_in, big_out, weights, moments_m, moments_v):
    n = len(weights)
    params = [big_in[0], big_out[0], *weights]

    def body(rw_ref, ro_ref, rs_ref, *refs):
        n_p = n + 2
        ins, outs = refs[:3 * n_p], refs[3 * n_p:]
        s = _sum_slots(rs_ref)
        eye = (_row((8, SLAB)) == _lane((8, SLAB))).astype(F32)
        sinks = jnp.sum(s[:, 1280:1408] * eye, axis=0, keepdims=True)
        grads = [_sum_slots(rw_ref), _sum_slots(ro_ref),
                 s[0:1, :D_MODEL], s[0:1, 1024:1024 + HEAD_DIM], s[0:1, 1152:1152 + HEAD_DIM], sinks[:, :8]]
        for k in range(n_p):
            outs[k][...] = grads[k]
            outs[n_p + k][...], outs[2 * n_p + k][...], outs[3 * n_p + k][...] = _adamw(
                ins[k][...], grads[k], ins[n_p + k][...], ins[2 * n_p + k][...])
        loss = jnp.sum(jnp.sum(s[:, 1408:1536], axis=1, keepdims=True), axis=0, keepdims=True) * (0.5 / D_MODEL)
        outs[4 * n_p][...] = loss

    n_p = n + 2
    res = pl.pallas_call(
        body, name="adamw_all",
        out_shape=[jax.ShapeDtypeStruct(p.shape, F32) for p in params] * 4 + [jax.ShapeDtypeStruct((1, 1), F32)],
        compiler_params=pltpu.CompilerParams(vmem_limit_bytes=VMEM_LIMIT),
    )(r_win, r_out, r_small, big_in[0], big_out[0], *weights, big_in[1], big_out[1], *moments_m,
      big_in[2], big_out[2], *moments_v)
    return [res[k * n_p:(k + 1) * n_p] for k in range(4)], res[4 * n_p]


def kernel(x, positions, norm_gain, w_in, q_norm_gain, k_norm_gain, sinks, w_out, loss_target, m_norm_gain, m_w_in, m_q_norm_gain, m_k_norm_gain, m_sinks, m_w_out, v_norm_gain, v_w_in, v_q_norm_gain, v_k_norm_gain, v_sinks, v_w_out):
    x2 = x.reshape(T_LOC, D_MODEL)
    tgt2 = loss_target.reshape(T_LOC, D_MODEL)
    pos2 = positions.reshape(1, T_LOC)
    half = HEAD_DIM // 2
    inv_freq = ROPE_THETA ** (-jnp.arange(half, dtype=F32) * 2.0 / HEAD_DIM)
    inv_freq = jnp.tile(inv_freq, SLAB // half).reshape(SLAB, 1)
    sin_sign = jnp.tile(jnp.concatenate([-jnp.ones((half,), F32), jnp.ones((half,), F32)]), 2).reshape(1, SLAB)
    q_gain2 = jnp.tile(q_norm_gain, (1, 2))
    k_gain2 = jnp.tile(k_norm_gain, (1, 2))

    win_t = gather_weights(w_in.reshape(D_MODEL, IN_SHARD).T)

    (qa_raw, ka_raw, q_rot, k_dup, v_dup, ga, qb, kb, vb, gb, cos, sin_s, wout) = fwd_proj(
        x2, pos2, norm_gain, win_t, inv_freq, sin_sign, q_gain2, k_gain2, w_out.reshape(OUT_SHARD, D_MODEL).astype(BF16))
    o_a = swa_fwd(q_rot, k_dup, v_dup, sinks)
    o_b, carries = sb_fwd(qb, kb, vb)
    dout, d_oa, d_ob, dga, dgb, dwout, loss_part = out_loss(o_a, o_b, ga, gb, x2, tgt2, wout)
    dq_rot, dk_dup, dv_dup, dsink = swa_bwd(q_rot, k_dup, v_dup, o_a, d_oa, sinks)
    dqb, dkb, dvb, r_out = sb_bwd(qb, kb, vb, d_ob, carries, dwout)
    dproj, dwin_t, dqg, dkg = bwd_dw(
        x2, norm_gain, dq_rot, dk_dup, dv_dup, qa_raw, ka_raw, cos, sin_s, q_gain2, k_gain2, dga, dgb, dqb, dkb, dvb)
    grad_x, r_win, r_small = bwd_dx(x2, dout, norm_gain, win_t, dproj, dwin_t, dqg, dkg, dsink, loss_part)

    w_in2, m_in2, v_in2 = (a.reshape(D_MODEL, IN_SHARD).T for a in (w_in, m_w_in, v_w_in))
    w_out2, m_out2, v_out2 = (a.reshape(OUT_SHARD, D_MODEL) for a in (w_out, m_w_out, v_w_out))
    kinds, loss = adamw_all(
        r_win, r_out, r_small, (w_in2, m_in2, v_in2), (w_out2, m_out2, v_out2),
        (norm_gain, q_norm_gain, k_norm_gain, sinks),
        (m_norm_gain, m_q_norm_gain, m_k_norm_gain, m_sinks), (v_norm_gain, v_q_norm_gain, v_k_norm_gain, v_sinks))

    def leaves(k):
        big_in, big_out, ng, qg, kg, sk = kinds[k]
        return (ng, big_in.T.reshape(1, D_MODEL, IN_SHARD), qg, kg, sk, big_out.reshape(1, OUT_SHARD, D_MODEL))

    return (loss.reshape(()), grad_x.reshape(B_LOC, SEQ, D_MODEL), *leaves(0), *leaves(1), *leaves(2), *leaves(3))
```

```python
import functools

import jax
import jax.numpy as jnp
from jax import lax
from jax.experimental import pallas as pl
from jax.experimental.pallas import tpu as pltpu

F32 = jnp.float32
BF16 = jnp.bfloat16

N_DEV = 8
D_MODEL = 1024
SEQ = 2048
B_LOC = 2
T_LOC = B_LOC * SEQ
HEAD_DIM = 64
HEAD_SHIFT = 6
BLK = 128
N_BLK = SEQ // BLK
SLAB = 128
IN_WIDTH = 3328
IN_SHARD = IN_WIDTH // N_DEV
OUT_SHARD = D_MODEL // N_DEV
EPS = 1e-6
ROPE_THETA = 10000.0
Q_SCALE = 0.125
R_QA, R_KA, R_VA, R_GA, R_QB, R_KB, R_VB, R_GB, R_END = 0, 512, 640, 768, 1280, 1792, 2304, 2816, 3328
SMALL_W = 1536
ADAM_LR, ADAM_B1, ADAM_B2, ADAM_EPS, ADAM_WD, ADAM_STEP = 0.001, 0.9, 0.999, 1e-08, 0.01, 10
TM = 512
ACC_ROWS = 256
GATHER_CHUNKS = 2
VMEM_LIMIT = 56 * 1024 * 1024

MESH = pl.DeviceIdType.MESH
NT = (((1,), (1,)), ((), ()))
TN = (((0,), (0,)), ((), ()))


def _params(sem, limit=VMEM_LIMIT):
    return pltpu.CompilerParams(dimension_semantics=sem, vmem_limit_bytes=limit)


def _dot(a, b, dims=None):
    if dims is None:
        return jnp.dot(a, b, preferred_element_type=F32)
    return lax.dot_general(a, b, dims, preferred_element_type=F32)


def _lane(shape):
    return lax.broadcasted_iota(jnp.int32, shape, len(shape) - 1)


def _row(shape):
    return lax.broadcasted_iota(jnp.int32, shape, 0)


def _head_blockdiag():
    return ((_row((SLAB, SLAB)) >> HEAD_SHIFT) == (_lane((SLAB, SLAB)) >> HEAD_SHIFT)).astype(BF16)


def _head_sum(x, bd):
    return _dot(x.astype(BF16), bd)


def _swap_half(y, lane):
    return jnp.where((lane & 32) != 0, pltpu.roll(y, 32, 1), pltpu.roll(y, 96, 1))


def _stack_heads(q, lane):
    zero = jnp.zeros_like(q)
    return jnp.concatenate([jnp.where(lane < HEAD_DIM, q, zero), jnp.where(lane >= HEAD_DIM, q, zero)], axis=0)


def _unstack_heads(x2, lane):
    return jnp.where(lane < HEAD_DIM, x2[:BLK], x2[BLK:])


def _sigmoid(x):
    return 1.0 / (1.0 + jnp.exp(-x))


def _mesh_pos():
    return lax.axis_index("x"), lax.axis_index("y"), lax.axis_index("c")


def _flip(pos, mask):
    return tuple(1 - p if m else p for p, m in zip(pos, mask))


def _lin(pos):
    return 4 * pos[0] + 2 * pos[1] + pos[2]


DEV_FLIPS = [(fx, fy, fc) for fx in (0, 1) for fy in (0, 1) for fc in (0, 1)][1:]


def _direct_exchange(src_for, dst_slot, send_sems, recv_sems, local_sem):
    me = _mesh_pos()

    def copy(k, to):
        return pltpu.make_async_remote_copy(
            src_ref=src_for(to), dst_ref=dst_slot(me), send_sem=send_sems.at[k], recv_sem=recv_sems.at[k],
            device_id=to, device_id_type=MESH)

    def landed(k, frm):
        return pltpu.make_async_remote_copy(
            src_ref=src_for(frm), dst_ref=dst_slot(frm), send_sem=send_sems.at[k], recv_sem=recv_sems.at[k],
            device_id=frm, device_id_type=MESH)

    local = None if local_sem is None else pltpu.make_async_copy(src_for(me), dst_slot(me), local_sem)
    peers = [_flip(me, f) for f in DEV_FLIPS]

    def start():
        if local is not None:
            local.start()
        for k, to in enumerate(peers):
            copy(k, to).start()

    def finish():
        for k, frm in enumerate(peers):
            landed(k, frm).wait_recv()
        for k, to in enumerate(peers):
            copy(k, to).wait_send()
        if local is not None:
            local.wait()

    return start, finish


def gather_weights(shard):
    m = shard.shape[0]
    piece = m // GATHER_CHUNKS
    pieces = range(GATHER_CHUNKS)

    def body(f32_ref, o_ref, a_ref, ici_send, ici_recv, d2d_send, d2d_recv, local_sem):
        a_ref[...] = f32_ref[...].astype(BF16)
        x, y, c = _mesh_pos()
        me, sibling = (x, y, c), (x, y, 1 - c)
        chip_x, chip_y, chip_d = (1 - x, y), (x, 1 - y), (1 - x, 1 - y)

        def rows(pos, q):
            return o_ref.at[pl.ds(_lin(pos) * m + q * piece, piece), :]

        def own(q):
            return a_ref.at[pl.ds(q * piece, piece), :]

        def ici(k, q, block, to, src=None):
            return pltpu.make_async_remote_copy(
                src_ref=rows(block, q) if src is None else src, dst_ref=rows(block, q),
                send_sem=ici_send.at[k, q], recv_sem=ici_recv.at[k, q], device_id=to, device_id_type=MESH)

        def d2d(k, q, chip, mine, src=None):
            block = (*chip, c) if mine else (*chip, 1 - c)
            return pltpu.make_async_remote_copy(
                src_ref=rows(block, q) if src is None else src, dst_ref=rows(block, q),
                send_sem=d2d_send.at[k, q], recv_sem=d2d_recv.at[k, q], device_id=sibling, device_id_type=MESH)

        local = pltpu.make_async_copy(a_ref, o_ref.at[pl.ds(_lin(me) * m, m), :], local_sem)
        local.start()
        sends = []
        for q in pieces:
            sends += [ici(0, q, me, (*chip_x, c), src=own(q)), ici(1, q, me, (*chip_y, c), src=own(q)),
                      d2d(0, q, (x, y), True, src=own(q))]
        for cp in sends:
            cp.start()

        def pass_on(first, k_first, second, k_second, onward):
            moved = []
            for q in pieces:
                ici(k_first, q, (*first, c), me).wait_recv()
                moved += [ici(2, q, (*first, c), (*onward, c)), d2d(1 + k_first, q, first, True)]
                for cp in moved[-2:]:
                    cp.start()
            for q in pieces:
                ici(k_second, q, (*second, c), me).wait_recv()
                moved.append(d2d(1 + k_second, q, second, True))
                moved[-1].start()
            for q in pieces:
                ici(2, q, (*chip_d, c), me).wait_recv()
                moved.append(d2d(3, q, chip_d, True))
                moved[-1].start()
            for cp in moved:
                cp.wait_send()

        @pl.when(c == 0)
        def _():
            pass_on(chip_y, 1, chip_x, 0, chip_x)

        @pl.when(c == 1)
        def _():
            pass_on(chip_x, 0, chip_y, 1, chip_y)

        for k, chip in enumerate([(x, y), chip_x, chip_y, chip_d]):
            for q in pieces:
                d2d(k, q, chip, False).wait_recv()
        for cp in sends:
            cp.wait_send()
        local.wait()

    vmem = pl.BlockSpec(memory_space=pltpu.VMEM)
    n_q = GATHER_CHUNKS
    return pl.pallas_call(
        body, name="gather_weights",
        out_shape=jax.ShapeDtypeStruct((N_DEV * m, shard.shape[1]), BF16),
        in_specs=[vmem], out_specs=vmem,
        scratch_shapes=[pltpu.VMEM(shard.shape, BF16), pltpu.SemaphoreType.DMA((3, n_q)), pltpu.SemaphoreType.DMA((3, n_q)),
                        pltpu.SemaphoreType.DMA((4, n_q)), pltpu.SemaphoreType.DMA((4, n_q)), pltpu.SemaphoreType.DMA],
        compiler_params=pltpu.CompilerParams(vmem_limit_bytes=VMEM_LIMIT),
    )(shard)


def _norm_rope(xs, gain2, cos, sin_s, bd, lane):
    r = lax.rsqrt(_head_sum(xs * xs, bd) * (1.0 / HEAD_DIM) + EPS)
    y = xs * r * gain2
    return y * cos + _swap_half(y, lane) * sin_s


def _dup_heads(xs, lane):
    r = pltpu.roll(xs, HEAD_DIM, 1)
    lo = lane < HEAD_DIM
    return jnp.concatenate([jnp.where(lo, xs, r), jnp.where(lo, r, xs)], axis=1)


def fwd_proj(x, pos, norm_gain, win_t, inv_freq, sin_sign, q_gain2, k_gain2, wout_shard):
    n_tiles = T_LOC // TM

    def body(x_ref, pos_ref, ng_ref, w_ref, if_ref, sg_ref, qg_ref, kg_ref, ws_hbm,
             qa_raw_ref, ka_raw_ref, q_rot_ref, k_dup_ref, v_dup_ref, ga_ref, qb_ref, kb_ref, vb_ref, gb_ref,
             cos_ref, sin_ref, wo_hbm, wo_send, wo_recv, wo_local):
        start_wout, finish_wout = _direct_exchange(
            lambda dev: ws_hbm, lambda dev: wo_hbm.at[pl.ds(_lin(dev) * OUT_SHARD, OUT_SHARD), :],
            wo_send, wo_recv, wo_local)
        pl.when(pl.program_id(0) == 0)(start_wout)

        xv = x_ref[...]
        rstd = lax.rsqrt(jnp.mean(xv * xv, axis=-1, keepdims=True) + EPS)
        h = (xv * rstd * ng_ref[...]).astype(BF16)

        def proj(r0, r1):
            return _dot(h, w_ref[r0:r1, :], NT)

        ang_t = if_ref[...] * pos_ref[...].astype(F32)
        cos = jnp.cos(ang_t).T
        sin_s = jnp.sin(ang_t).T * sg_ref[...]
        cos_ref[...] = cos
        sin_ref[...] = sin_s
        lane = _lane((TM, SLAB))
        bd = _head_blockdiag()

        qa = proj(R_QA, R_KA)
        qa_raw_ref[...] = qa
        for p in range(4):
            sl = slice(p * SLAB, (p + 1) * SLAB)
            q_rot_ref[:, sl] = (_norm_rope(qa[:, sl], qg_ref[...], cos, sin_s, bd, lane) * Q_SCALE).astype(BF16)
        ka = proj(R_KA, R_VA)
        ka_raw_ref[...] = ka
        k_dup_ref[...] = _dup_heads(_norm_rope(ka, kg_ref[...], cos, sin_s, bd, lane), lane).astype(BF16)
        v_dup_ref[...] = _dup_heads(proj(R_VA, R_GA), lane).astype(BF16)
        ga_ref[...] = proj(R_GA, R_QB).astype(BF16)
        qb_ref[...] = (proj(R_QB, R_KB) * Q_SCALE).astype(BF16)
        kb_ref[...] = proj(R_KB, R_VB).astype(BF16)
        vb_ref[...] = proj(R_VB, R_GB).astype(BF16)
        gb_ref[...] = proj(R_GB, R_END).astype(BF16)
        pl.when(pl.program_id(0) == n_tiles - 1)(finish_wout)

    def tile(w):
        return pl.BlockSpec((TM, w), lambda i: (i, 0))

    def whole(a):
        return pl.BlockSpec(a.shape, lambda i: (0, 0))

    hbm = pl.BlockSpec(memory_space=pl.ANY)
    widths = [(512, F32), (128, F32), (512, BF16), (256, BF16), (256, BF16), (512, BF16), (512, BF16), (512, BF16),
              (512, BF16), (512, BF16), (128, F32), (128, F32)]
    return pl.pallas_call(
        body, name="fwd_proj", grid=(n_tiles,),
        in_specs=[tile(D_MODEL), pl.BlockSpec((1, TM), lambda i: (0, i)), whole(norm_gain), whole(win_t),
                  whole(inv_freq), whole(sin_sign),
                  whole(q_gain2), whole(k_gain2), hbm],
        out_specs=[tile(w) for w, _ in widths] + [hbm],
        out_shape=[jax.ShapeDtypeStruct((T_LOC, w), dt) for w, dt in widths]
        + [jax.ShapeDtypeStruct((D_MODEL, D_MODEL), BF16)],
        scratch_shapes=[pltpu.SemaphoreType.DMA((7,)), pltpu.SemaphoreType.DMA((7,)), pltpu.SemaphoreType.DMA],
        compiler_params=_params(("arbitrary",)),
    )(x, pos, norm_gain, win_t, inv_freq, sin_sign, q_gain2, k_gain2, wout_shard)


def _swa_window(prev_ref, cur_ref, p):
    gsl = _slab(p // 2)
    return jnp.concatenate([prev_ref[:, gsl], cur_ref[:, gsl]], axis=0)


def _swa_probs(s, sinks_ref, p, i):
    shape = (2 * BLK, 2 * BLK)
    r = _row(shape) & (BLK - 1)
    cidx = _lane(shape)
    valid = (cidx > r) & (cidx <= r + BLK) & ((cidx >= BLK) | (i > 0))
    s = jnp.where(valid, s, -jnp.inf)
    sink = jnp.where(_row((2 * BLK, 1)) < BLK, sinks_ref[0, 2 * p], sinks_ref[0, 2 * p + 1])
    m = jnp.maximum(jnp.max(s, axis=-1, keepdims=True), sink)
    e = jnp.exp(s - m)
    e_sink = jnp.exp(sink - m)
    den = jnp.sum(e, axis=-1, keepdims=True) + e_sink
    return e / den, e_sink / den


SWA_CHAINS = [(b, p) for b in range(B_LOC) for p in range(4)]


def _swa_specs():
    def cur(w):
        return pl.BlockSpec((B_LOC, BLK, w), lambda i: (0, i, 0))

    def prev(w):
        return pl.BlockSpec((B_LOC, BLK, w), lambda i: (0, jnp.maximum(i - 1, 0), 0))

    return cur, prev


def swa_fwd(q_rot, k_dup, v_dup, sinks):
    def body(q_ref, kp_ref, kc_ref, vp_ref, vc_ref, sinks_ref, o_ref):
        i = pl.program_id(0)
        lane = _lane((BLK, SLAB))
        s = [_dot(_stack_heads(q_ref[b, :, _slab(p)], lane), _swa_window(kp_ref.at[b], kc_ref.at[b], p), NT)
             for b, p in SWA_CHAINS]
        pn = [_swa_probs(s[c], sinks_ref, p, i)[0].astype(BF16) for c, (b, p) in enumerate(SWA_CHAINS)]
        for c, (b, p) in enumerate(SWA_CHAINS):
            o = _unstack_heads(_dot(pn[c], _swa_window(vp_ref.at[b], vc_ref.at[b], p)), lane)
            o_ref[b, :, _slab(p)] = o.astype(BF16)

    cur, prev = _swa_specs()
    q3, k3, v3 = (a.reshape(B_LOC, SEQ, a.shape[1]) for a in (q_rot, k_dup, v_dup))
    return pl.pallas_call(
        body, name="swa_fwd", grid=(N_BLK,),
        in_specs=[cur(512), prev(256), cur(256), prev(256), cur(256), pl.BlockSpec(memory_space=pltpu.SMEM)],
        out_specs=cur(512),
        out_shape=jax.ShapeDtypeStruct((B_LOC, SEQ, 512), BF16),
        compiler_params=_params(("arbitrary",)),
    )(q3, k3, k3, v3, v3, sinks).reshape(T_LOC, 512)


def _tri(suffix):
    r, cidx = _row((BLK + 16, BLK)), _lane((BLK + 16, BLK))
    tri = (cidx > r) if suffix else (cidx < r)
    return (tri | (r >= BLK)).astype(BF16)


def _key_sums(tri, x):
    res = _dot(tri, x.astype(BF16))
    return res[:BLK], res[BLK:BLK + 1]


def _sb_softplus(zt, valid):
    neg_abs = lax.bitcast_convert_type(lax.bitcast_convert_type(zt, jnp.uint32) | jnp.uint32(0x80000000), F32)
    sp = jnp.maximum(zt, 0.0) + jnp.log(1.0 + jnp.exp(neg_abs))
    return sp if valid is None else jnp.where(valid, sp, 0.0)


def _sb_weights(zt, sp, later, valid):
    w = jnp.exp(zt - sp - later)
    return w if valid is None else jnp.where(valid, w, 0.0)


def _slab(pp):
    return slice(pp * SLAB, (pp + 1) * SLAB)


def _blk(j):
    return pl.ds(pl.multiple_of(j * BLK, BLK), BLK)


def _causal_t():
    return _row((BLK, 2 * BLK)) < (_lane((BLK, 2 * BLK)) & (BLK - 1))


def _sb_rows(b, j):
    return pl.ds(pl.multiple_of(b * SEQ + j * BLK, BLK), BLK)


SB_CHAINS = [(b, pp) for b in range(B_LOC) for pp in range(4)]


def sb_fwd(qb, kb, vb):
    def body(q_ref, k_ref, v_ref, o_ref, c_ref, vt_ref, ot_ref):
        for c, (b, pp) in enumerate(SB_CHAINS):
            for j in range(N_BLK):
                vt_ref[c, j] = v_ref[b * SEQ + j * BLK:b * SEQ + (j + 1) * BLK, _slab(pp)].T
        lane = _lane((BLK, SLAB))
        tri = _tri(True)
        valid = _causal_t()
        jrow = _row((N_BLK, 2 * BLK))
        chains = range(len(SB_CHAINS))

        def q_block(i, _):
            q2 = [_stack_heads(q_ref[_sb_rows(b, i), _slab(pp)], lane) for b, pp in SB_CHAINS]

            def key_block(j, carry, mask, first):
                zt = [_dot(k_ref[_sb_rows(b, j), _slab(pp)], q2[c], NT) for c, (b, pp) in enumerate(SB_CHAINS)]
                sp = [_sb_softplus(zt[c], mask) for c in chains]
                sums = [_key_sums(tri, sp[c]) for c in chains]
                w = [_sb_weights(zt[c], sp[c], sums[c][0] + carry[c], mask) for c in chains]
                for c in chains:
                    pv = _dot(vt_ref[c, j], w[c].astype(BF16))
                    if first:
                        ot_ref[c] = pv
                    else:
                        ot_ref[c] += pv
                return tuple(carry[c] + sums[c][1] for c in chains)

            def earlier(jj, state):
                carry, saved = state
                j = i - 1 - jj
                saved = tuple(jnp.where(jrow == j, carry[c], saved[c]) for c in chains)
                return key_block(j, carry, None, False), saved

            zero = tuple(jnp.zeros((1, 2 * BLK), F32) for _ in chains)
            carry = key_block(i, zero, valid, True)
            _, saved = lax.fori_loop(0, i, earlier, (carry, tuple(jnp.zeros((N_BLK, 2 * BLK), F32) for _ in chains)))
            for c, (b, pp) in enumerate(SB_CHAINS):
                o_ref[_sb_rows(b, i), _slab(pp)] = _unstack_heads(ot_ref[c].T, lane).astype(BF16)
                c_ref[c * N_BLK + i] = saved[c]
            return 0

        lax.fori_loop(0, N_BLK, q_block, 0)

    n_ch = len(SB_CHAINS)
    vmem = pl.BlockSpec(memory_space=pltpu.VMEM)
    return pl.pallas_call(
        body, name="sb_fwd",
        in_specs=[vmem] * 3, out_specs=[vmem] * 2,
        out_shape=[jax.ShapeDtypeStruct((T_LOC, 512), BF16), jax.ShapeDtypeStruct((n_ch * N_BLK, N_BLK, 2 * BLK), F32)],
        scratch_shapes=[pltpu.VMEM((n_ch, N_BLK, SLAB, BLK), BF16), pltpu.VMEM((n_ch, SLAB, 2 * BLK), F32)],
        compiler_params=pltpu.CompilerParams(vmem_limit_bytes=VMEM_LIMIT),
    )(qb, kb, vb)


def out_loss(o_a, o_b, ga, gb, x, target, wout):
    n_tiles = T_LOC // TM

    def body(oa_ref, ob_ref, ga_ref, gb_ref, x_ref, t_ref, w_ref,
             dout_ref, doa_ref, dob_ref, dga_ref, dgb_ref, dw_ref, loss_ref, acc_ref):
        step = pl.program_id(0)

        @pl.when(step == 0)
        def _():
            acc_ref[...] = jnp.zeros_like(acc_ref)
            loss_ref[...] = jnp.zeros_like(loss_ref)

        oa, ob, gav, gbv = (r[...].astype(F32) for r in (oa_ref, ob_ref, ga_ref, gb_ref))
        sa, sb = _sigmoid(gav), _sigmoid(gbv)
        silu_a, silu_b = gav * sa, gbv * sb
        y = jnp.concatenate([oa * silu_a, ob * silu_b], axis=1).astype(BF16)
        err = x_ref[...] + _dot(y, w_ref[...]) - t_ref[...]
        e2 = err * err
        part = jnp.sum(e2.reshape(TM // 8, 8, D_MODEL), axis=0)
        loss_ref[...] += functools.reduce(lambda a, b: a + b, [part[:, k * 128:(k + 1) * 128] for k in range(8)])
        dout = err * (1.0 / D_MODEL)
        dout_ref[...] = dout
        dob16 = dout.astype(BF16)
        for r0 in range(0, D_MODEL, ACC_ROWS):
            acc_ref[r0:r0 + ACC_ROWS, :] += _dot(y[:, r0:r0 + ACC_ROWS], dob16, TN)
        dy = _dot(dob16, w_ref[...], NT)
        dya, dyb = dy[:, :512], dy[:, 512:]
        doa_ref[...] = (dya * silu_a).astype(BF16)
        dob_ref[...] = (dyb * silu_b).astype(BF16)
        dga_ref[...] = (dya * oa * (sa * (1.0 + gav * (1.0 - sa)))).astype(BF16)
        dgb_ref[...] = (dyb * ob * (sb * (1.0 + gbv * (1.0 - sb)))).astype(BF16)

        @pl.when(step == n_tiles - 1)
        def _():
            dw_ref[...] = acc_ref[...].astype(BF16)

    def tile(w):
        return pl.BlockSpec((TM, w), lambda i: (i, 0))

    const = lambda i: (0, 0)
    return pl.pallas_call(
        body, name="out_loss", grid=(n_tiles,),
        in_specs=[tile(512)] * 4 + [tile(D_MODEL)] * 2 + [pl.BlockSpec((D_MODEL, D_MODEL), const)],
        out_specs=[tile(D_MODEL), tile(512), tile(512), tile(512), tile(512),
                   pl.BlockSpec((D_MODEL, D_MODEL), const), pl.BlockSpec((8, 128), const)],
        out_shape=[jax.ShapeDtypeStruct((T_LOC, D_MODEL), F32)] + [jax.ShapeDtypeStruct((T_LOC, 512), BF16)] * 4
        + [jax.ShapeDtypeStruct((D_MODEL, D_MODEL), BF16), jax.ShapeDtypeStruct((8, 128), F32)],
        scratch_shapes=[pltpu.VMEM((D_MODEL, D_MODEL), F32)],
        compiler_params=_params(("arbitrary",)),
    )(o_a, o_b, ga, gb, x, target, wout)


def swa_bwd(q_rot, k_dup, v_dup, o_a, d_oa, sinks):
    def body(q_ref, kp_ref, kc_ref, vp_ref, vc_ref, o_ref, do_ref, sinks_ref, dq_ref, dk_ref, dv_ref, dsink_ref):
        i = pl.program_id(0)

        @pl.when(i == 0)
        def _():
            dk_ref[...] = jnp.zeros_like(dk_ref)
            dv_ref[...] = jnp.zeros_like(dv_ref)
            dsink_ref[...] = jnp.zeros_like(dsink_ref)

        lane = _lane((BLK, SLAB))
        rows_prev, rows_cur = _blk(jnp.maximum(i - 1, 0)), _blk(i)
        chains = range(len(SWA_CHAINS))
        q2 = [_stack_heads(q_ref[b, :, _slab(p)], lane) for b, p in SWA_CHAINS]
        do2 = [_stack_heads(do_ref[b, :, _slab(p)], lane) for b, p in SWA_CHAINS]
        keys = [_swa_window(kp_ref.at[b], kc_ref.at[b], p) for b, p in SWA_CHAINS]
        s = [_dot(q2[c], keys[c], NT) for c in chains]
        dp = [_dot(do2[c], _swa_window(vp_ref.at[b], vc_ref.at[b], p), NT) for c, (b, p) in enumerate(SWA_CHAINS)]
        ds, pn16, cols = [], [], []
        for c, (b, p) in enumerate(SWA_CHAINS):
            pn, p_sink = _swa_probs(s[c], sinks_ref, p, i)
            o = o_ref[b, :, _slab(p)].astype(F32)
            delta =jnp.sum(do2[c].astype(F32) * jnp.concatenate([o, o], axis=0), axis=-1, keepdims=True)
            ds.append((pn * (dp[c] - delta)).astype(BF16))
            pn16.append(pn.astype(BF16))
            cols.append(-p_sink * delta)
        for c, (b, p) in enumerate(SWA_CHAINS):
            dq_ref[b, :, _slab(p)] = _unstack_heads(_dot(ds[c], keys[c]), lane) * Q_SCALE
        dk2 = [_dot(ds[c], q2[c], TN) for c in chains]
        dv2 = [_dot(pn16[c], do2[c], TN) for c in chains]
        for c, (b, p) in enumerate(SWA_CHAINS):
            gsl = _slab(p // 2)
            dk_ref[b, rows_prev, gsl] += dk2[c][:BLK]
            dk_ref[b, rows_cur, gsl] += dk2[c][BLK:]
            dv_ref[b, rows_prev, gsl] += dv2[c][:BLK]
            dv_ref[b, rows_cur, gsl] += dv2[c][BLK:]
            for e in range(2):
                dsink_ref[2 * p + e:2 * p + e + 1, :] += jnp.sum(cols[c][e * BLK:(e + 1) * BLK], axis=0, keepdims=True)

    cur, prev = _swa_specs()
    whole = pl.BlockSpec((B_LOC, SEQ, 256), lambda i: (0, 0, 0))
    q3, k3, v3, o3, do3 = (a.reshape(B_LOC, SEQ, a.shape[1]) for a in (q_rot, k_dup, v_dup, o_a, d_oa))
    dq, dk, dv, dsink = pl.pallas_call(
        body, name="swa_bwd", grid=(N_BLK,),
        in_specs=[cur(512), prev(256), cur(256), prev(256), cur(256), cur(512), cur(512),
                  pl.BlockSpec(memory_space=pltpu.SMEM)],
        out_specs=[cur(512), whole, whole, pl.BlockSpec((8, 128), lambda i: (0, 0))],
        out_shape=[jax.ShapeDtypeStruct((B_LOC, SEQ, 512), F32), jax.ShapeDtypeStruct((B_LOC, SEQ, 256), F32),
                   jax.ShapeDtypeStruct((B_LOC, SEQ, 256), F32), jax.ShapeDtypeStruct((8, 128), F32)],
        compiler_params=_params(("arbitrary",)),
    )(q3, k3, k3, v3, v3, o3, do3, sinks)
    return dq.reshape(T_LOC, 512), dk.reshape(T_LOC, 256), dv.reshape(T_LOC, 256), dsink


def sb_bwd(qb, kb, vb, d_ob, carries):
    def body(q_ref, k_ref, v_ref, do_ref, c_ref, dq_ref, dk_ref, dv_ref, kt_ref, dqt_ref):
        for c, (b, pp) in enumerate(SB_CHAINS):
            for j in range(N_BLK):
                kt_ref[c, j] = k_ref[b * SEQ + j * BLK:b * SEQ + (j + 1) * BLK, _slab(pp)].T
        dk_ref[...] = jnp.zeros_like(dk_ref)
        dv_ref[...] = jnp.zeros_like(dv_ref)
        dqt_ref[...] = jnp.zeros_like(dqt_ref)
        lane = _lane((BLK, SLAB))
        tri_after, tri_before = _tri(True), _tri(False)
        valid = _causal_t()
        jrow = _row((N_BLK, 2 * BLK))
        chains = range(len(SB_CHAINS))

        def q_block(i, _):
            q2 = [_stack_heads(q_ref[_sb_rows(b, i), _slab(pp)], lane) for b, pp in SB_CHAINS]
            do2 = [_stack_heads(do_ref[_sb_rows(b, i), _slab(pp)], lane) for b, pp in SB_CHAINS]

            def key_block(j, carry_sp, before_u, mask):
                at = [(_sb_rows(b, j), _slab(pp)) for b, pp in SB_CHAINS]
                zt = [_dot(k_ref[at[c]], q2[c], NT) for c in chains]
                dw = [_dot(v_ref[at[c]], do2[c], NT) for c in chains]
                sp = [_sb_softplus(zt[c], mask) for c in chains]
                later = [_key_sums(tri_after, sp[c])[0] for c in chains]
                w = [_sb_weights(zt[c], sp[c], later[c] + carry_sp[c], mask) for c in chains]
                u = [dw[c] * w[c] for c in chains]
                for c in chains:
                    dv_ref[at[c]] += _dot(w[c].astype(BF16), do2[c])
                sums = [_key_sums(tri_before, u[c]) for c in chains]
                dz16 = []
                for c in chains:
                    sig = jnp.exp(zt[c] - sp[c])
                    dz = u[c] - sig * (u[c] + before_u[c] + sums[c][0])
                    if mask is not None:
                        dz = jnp.where(mask, dz, 0.0)
                    dz16.append(dz.astype(BF16))
                for c in chains:
                    dk_ref[at[c]] += _dot(dz16[c], q2[c])
                    dqt_ref[c] += _dot(kt_ref[c, j], dz16[c])
                return tuple(before_u[c] + sums[c][1] for c in chains)

            def earlier(j, before_u):
                carry_sp = [jnp.sum(jnp.where(jrow == j, c_ref[c * N_BLK + i], 0.0), axis=0, keepdims=True)
                            for c in chains]
                return key_block(j, carry_sp, before_u, None)

            zero = tuple(jnp.zeros((1, 2 * BLK), F32) for _ in chains)
            before_u = lax.fori_loop(0, i, earlier, zero)
            key_block(i, zero, before_u, valid)
            for c, (b, pp) in enumerate(SB_CHAINS):
                dq_ref[_sb_rows(b, i), _slab(pp)] = (_unstack_heads(dqt_ref[c].T, lane) * Q_SCALE).astype(BF16)
                dqt_ref[c] = jnp.zeros((SLAB, 2 * BLK), F32)
            return 0

        lax.fori_loop(0, N_BLK, q_block, 0)

    n_ch = len(SB_CHAINS)
    vmem = pl.BlockSpec(memory_space=pltpu.VMEM)
    return pl.pallas_call(
        body, name="sb_bwd",
        in_specs=[vmem] * 5, out_specs=[vmem] * 3,
        out_shape=[jax.ShapeDtypeStruct((T_LOC, 512), BF16)] + [jax.ShapeDtypeStruct((T_LOC, 512), F32)] * 2,
        scratch_shapes=[pltpu.VMEM((n_ch, N_BLK, SLAB, BLK), BF16), pltpu.VMEM((n_ch, SLAB, 2 * BLK), F32)],
        compiler_params=pltpu.CompilerParams(vmem_limit_bytes=VMEM_LIMIT),
    )(qb, kb, vb, d_ob, carries)


def bwd_dw(x, norm_gain, dq_rot, dk_dup, dv_dup, qa_raw, ka_raw, cos, sin_s, q_gain2, k_gain2, dga, dgb, dqb, dkb, dvb):
    n_tiles = T_LOC // TM

    def body(x_ref, ng_ref, dq_ref, dk_ref, dv_ref, qa_ref, ka_ref, cos_ref, sin_ref, qg_ref, kg_ref,
             dga_ref, dgb_ref, dqb_ref, dkb_ref, dvb_ref,
             dproj_ref, dw_hbm, dqg_ref, dkg_ref, acc_ref, stage_ref):
        step = pl.program_id(0)

        @pl.when(step == 0)
        def _():
            acc_ref[...] = jnp.zeros_like(acc_ref)
            dqg_ref[...] = jnp.zeros_like(dqg_ref)
            dkg_ref[...] = jnp.zeros_like(dkg_ref)

        lane = _lane((TM, SLAB))
        bd = _head_blockdiag()
        cos, sin_s = cos_ref[...], sin_ref[...]

        def norm_rope_bwd(d_rot, raw, gain2):
            dy = d_rot * cos + _swap_half(d_rot * sin_s, lane)
            r = lax.rsqrt(_head_sum(raw * raw, bd) * (1.0 / HEAD_DIM) + EPS)
            xhat = raw * r
            dgain = jnp.sum(dy * xhat, axis=0, keepdims=True)
            dxh = dy * gain2
            mean = _head_sum(dxh * xhat, bd) * (1.0 / HEAD_DIM)
            return r * (dxh - xhat * mean), dgain

        def fold_dup(d_dup):
            a, b2 = d_dup[:, :SLAB], d_dup[:, SLAB:]
            return jnp.where(lane < HEAD_DIM, a + pltpu.roll(a, HEAD_DIM, 1), b2 + pltpu.roll(b2, HEAD_DIM, 1))

        pieces = []
        dqg = jnp.zeros((1, SLAB), F32)
        for p in range(4):
            sl = slice(p * SLAB, (p + 1) * SLAB)
            d_raw, dg = norm_rope_bwd(dq_ref[:, sl], qa_ref[:, sl], qg_ref[...])
            pieces.append(d_raw.astype(BF16))
            dqg = dqg + dg
        d_raw, dkg = norm_rope_bwd(fold_dup(dk_ref[...]), ka_ref[...], kg_ref[...])
        pieces.append(d_raw.astype(BF16))
        pieces.append(fold_dup(dv_ref[...]).astype(BF16))
        pieces += [dga_ref[...], dqb_ref[...], dkb_ref[...].astype(BF16), dvb_ref[...].astype(BF16),
                   dgb_ref[...]]
        dproj = jnp.concatenate(pieces, axis=1)
        dproj_ref[...] = dproj
        dqg_ref[0:1, :] += dqg + pltpu.roll(dqg, HEAD_DIM, 1)
        dkg_ref[0:1, :] += dkg + pltpu.roll(dkg, HEAD_DIM, 1)

        xv = x_ref[...]
        rstd = lax.rsqrt(jnp.mean(xv * xv, axis=-1, keepdims=True) + EPS)
        h = (xv * rstd * ng_ref[...]).astype(BF16)
        for r0 in range(0, IN_WIDTH, ACC_ROWS):
            acc_ref[r0:r0 + ACC_ROWS, :] += _dot(dproj[:, r0:r0 + ACC_ROWS], h, TN)

        @pl.when(step == n_tiles - 1)
        def _():
            for r0 in range(0, IN_WIDTH, ACC_ROWS):
                stage_ref[...] = acc_ref[r0:r0 + ACC_ROWS, :].astype(BF16)
                pltpu.sync_copy(stage_ref, dw_hbm.at[r0:r0 + ACC_ROWS, :])

    def tile(w):
        return pl.BlockSpec((TM, w), lambda i: (i, 0))

    def whole(a):
        return pl.BlockSpec(a.shape, lambda i: (0, 0))

    const = lambda i: (0, 0)
    return pl.pallas_call(
        body, name="bwd_dw", grid=(n_tiles,),
        in_specs=[tile(D_MODEL), whole(norm_gain),
                  tile(512), tile(256), tile(256), tile(512), tile(128), tile(128), tile(128),
                  whole(q_gain2), whole(k_gain2), tile(512), tile(512), tile(512), tile(512), tile(512)],
        out_specs=[tile(IN_WIDTH), pl.BlockSpec(memory_space=pl.ANY),
                   pl.BlockSpec((8, SLAB), const), pl.BlockSpec((8, SLAB), const)],
        out_shape=[jax.ShapeDtypeStruct((T_LOC, IN_WIDTH), BF16), jax.ShapeDtypeStruct((IN_WIDTH, D_MODEL), BF16),
                   jax.ShapeDtypeStruct((8, SLAB), F32), jax.ShapeDtypeStruct((8, SLAB), F32)],
        scratch_shapes=[pltpu.VMEM((IN_WIDTH, D_MODEL), F32), pltpu.VMEM((ACC_ROWS, D_MODEL), BF16)],
        compiler_params=_params(("arbitrary",)),
    )(x, norm_gain, dq_rot, dk_dup, dv_dup, qa_raw, ka_raw, cos, sin_s, q_gain2, k_gain2, dga, dgb, dqb, dkb, dvb)


def bwd_dx(x, dout, norm_gain, win_t, dproj, dwin_t, dwout, dqg, dkg, dsink, loss_part):
    n_tiles = T_LOC // TM
    rows_per = IN_SHARD
    step_sums, step_merge = 1, 3

    def body(x_ref, dout_ref, ng_ref, w_hbm, dp_ref, a_hbm, dwo_hbm, dqg_ref, dkg_ref, dsink_ref, loss_ref,
             gx_ref, ra_hbm, rs_hbm, rwo_hbm, w_ref, dng_ref, s_ref, own_ref, sib_ref, snd_ref, extra_ref,
             w_sem, d2d_send, d2d_recv, ici_send, ici_recv, own_sems, s_send, s_recv, out_sem,
             wo_send, wo_recv, wo_local):
        step = pl.program_id(0)
        x, y, c = _mesh_pos()
        me, sibling = (x, y, c), (x, y, 1 - c)
        chips = {"own": (x, y), "x": (1 - x, y), "y": (x, 1 - y), "d": (1 - x, 1 - y)}
        index = {"own": 0, "x": 1, "y": 2, "d": 3}
        order = ("d", "x", "y", "own")

        def rows(pos):
            return a_hbm.at[pl.ds(_lin(pos) * rows_per, rows_per), :]

        def to_sibling(k):
            return pltpu.make_async_remote_copy(
                src_ref=rows((*chips[k], 1 - c)), dst_ref=sib_ref.at[index[k]],
                send_sem=d2d_send.at[index[k]], recv_sem=d2d_recv.at[index[k]], device_id=sibling, device_id_type=MESH)

        def mine(k):
            return pltpu.make_async_copy(rows((*chips[k], c)), own_ref.at[index[k]], own_sems.at[index[k]])

        def ici(n, to_chip, dst):
            return pltpu.make_async_remote_copy(
                src_ref=snd_ref.at[n], dst_ref=dst, send_sem=ici_send.at[n], recv_sem=ici_recv.at[n],
                device_id=(*chips[to_chip], c), device_id_type=MESH)

        def chip_sum(k):
            to_sibling(k).wait_recv()
            mine(k).wait()
            return own_ref[index[k]].astype(F32) + sib_ref[index[k]].astype(F32)

        start_dwout, finish_dwout = _direct_exchange(
            lambda dev: dwo_hbm.at[pl.ds(_lin(dev) * OUT_SHARD, OUT_SHARD), :], lambda dev: rwo_hbm.at[_lin(dev)],
            wo_send, wo_recv, wo_local)

        def by_core(fn):
            pl.when(c == 0)(lambda: fn("x", "y"))
            pl.when(c == 1)(lambda: fn("y", "x"))

        @pl.when(step == 0)
        def _():
            cp = pltpu.make_async_copy(w_hbm, w_ref, w_sem)
            cp.start()
            for k in order:
                to_sibling(k).start()
                mine(k).start()
            start_dwout()
            dng_ref[...] = jnp.zeros_like(dng_ref)
            cp.wait()

        @pl.when(step == step_sums)
        def _():
            def first_sends(direct, via):
                snd_ref[0] = chip_sum("d").astype(BF16)
                ici(0, direct, extra_ref).start()
                snd_ref[1] = chip_sum(direct).astype(BF16)
                ici(1, direct, ra_hbm.at[index[direct]]).start()
            by_core(first_sends)

        @pl.when(step == step_merge)
        def _():
            def merge(direct, via):
                merged = chip_sum(via)
                ici(0, direct, extra_ref).wait_recv()
                snd_ref[2] = (merged + extra_ref[...].astype(F32)).astype(BF16)
                ici(2, via, ra_hbm.at[index[via]]).start()
                own_ref[0] = chip_sum("own").astype(BF16)
                pltpu.make_async_copy(own_ref.at[0], ra_hbm.at[0], out_sem).start()
            by_core(merge)

        xv = x_ref[...]
        rstd = lax.rsqrt(jnp.mean(xv * xv, axis=-1, keepdims=True) + EPS)
        xhat = xv * rstd
        gain = ng_ref[...]
        dh = _dot(dp_ref[...], w_ref[...])
        dng_ref[0:1, :] += jnp.sum(dh * xhat, axis=0, keepdims=True)
        dxh = dh * gain
        gx_ref[...] = dout_ref[...] + rstd * (dxh - xhat * jnp.mean(dxh * xhat, axis=-1, keepdims=True))

        @pl.when(step == n_tiles - 1)
        def _():
            s_ref[...] = jnp.concatenate(
                [dng_ref[...], dqg_ref[...], dkg_ref[...], dsink_ref[...], loss_ref[...]], axis=1)
            start_small, finish_small = _direct_exchange(
                lambda dev: s_ref, lambda dev: rs_hbm.at[_lin(dev)], s_send, s_recv, out_sem)

            def finish(direct, via):
                ici(1, direct, ra_hbm.at[index[direct]]).wait_recv()
                ici(2, via, ra_hbm.at[index[via]]).wait_recv()
                for n, to in ((0, direct), (1, direct), (2, via)):
                    ici(n, to, extra_ref).wait_send()
            by_core(finish)
            pltpu.make_async_copy(own_ref.at[0], ra_hbm.at[0], out_sem).wait()
            for k in order:
                to_sibling(k).wait_send()
            finish_dwout()
            start_small()
            finish_small()

    def tile(w):
        return pl.BlockSpec((TM, w), lambda i: (i, 0))

    def whole(a):
        return pl.BlockSpec(a.shape, lambda i: (0, 0))

    hbm = pl.BlockSpec(memory_space=pl.ANY)
    block = (rows_per, D_MODEL)
    return pl.pallas_call(
        body, name="bwd_dx", grid=(n_tiles,),
        in_specs=[tile(D_MODEL), tile(D_MODEL), whole(norm_gain), hbm, tile(IN_WIDTH), hbm, hbm,
                  whole(dqg), whole(dkg), whole(dsink), whole(loss_part)],
        out_specs=[tile(D_MODEL), hbm, hbm, hbm],
        out_shape=[jax.ShapeDtypeStruct((T_LOC, D_MODEL), F32), jax.ShapeDtypeStruct((3,) + block, BF16),
                   jax.ShapeDtypeStruct((N_DEV, 8, SMALL_W), F32),
                   jax.ShapeDtypeStruct((N_DEV, OUT_SHARD, D_MODEL), BF16)],
        scratch_shapes=[pltpu.VMEM((IN_WIDTH, D_MODEL), BF16), pltpu.VMEM((8, D_MODEL), F32),
                        pltpu.VMEM((8, SMALL_W), F32),
                        pltpu.VMEM((4,) + block, BF16), pltpu.VMEM((4,) + block, BF16), pltpu.VMEM((3,) + block, BF16),
                        pltpu.VMEM(block, BF16),
                        pltpu.SemaphoreType.DMA, pltpu.SemaphoreType.DMA((4,)), pltpu.SemaphoreType.DMA((4,)),
                        pltpu.SemaphoreType.DMA((3,)), pltpu.SemaphoreType.DMA((3,)), pltpu.SemaphoreType.DMA((4,)),
                        pltpu.SemaphoreType.DMA((7,)), pltpu.SemaphoreType.DMA((7,)), pltpu.SemaphoreType.DMA,
                        pltpu.SemaphoreType.DMA((7,)), pltpu.SemaphoreType.DMA((7,)), pltpu.SemaphoreType.DMA],
        compiler_params=_params(("arbitrary",)),
    )(x, dout, norm_gain, win_t, dproj, dwin_t, dwout, dqg, dkg, dsink, loss_part)


def _adamw(w, g, m, v):
    m = ADAM_B1 * m + (1.0 - ADAM_B1) * g
    v = ADAM_B2 * v + (1.0 - ADAM_B2) * (g * g)
    m_hat = m / (1.0 - ADAM_B1 ** ADAM_STEP)
    v_hat = v / (1.0 - ADAM_B2 ** ADAM_STEP)
    delta = -ADAM_LR * (m_hat / (jnp.sqrt(v_hat) + ADAM_EPS) + ADAM_WD * w)
    return delta, m, v


def _sum_slots(r_ref):
    g = r_ref[0].astype(F32)
    for s in range(1, r_ref.shape[0]):
        g = g + r_ref[s].astype(F32)
    return g


def adamw_all(r_win, r_out, r_small, big_in, big_out, weights, moments_m, moments_v):
    n = len(weights)
    params = [big_in[0], big_out[0], *weights]

    def body(rw_ref, ro_ref, rs_ref, *refs):
        n_p = n + 2
        ins, outs = refs[:3 * n_p], refs[3 * n_p:]
        s = _sum_slots(rs_ref)
        eye = (_row((8, SLAB)) == _lane((8, SLAB))).astype(F32)
        sinks = jnp.sum(s[:, 1280:1408] * eye, axis=0, keepdims=True)
        grads = [_sum_slots(rw_ref), _sum_slots(ro_ref),
                 s[0:1, :D_MODEL], s[0:1, 1024:1024 + HEAD_DIM], s[0:1, 1152:1152 + HEAD_DIM], sinks[:, :8]]
        for k in range(n_p):
            outs[k][...] = grads[k]
            outs[n_p + k][...], outs[2 * n_p + k][...], outs[3 * n_p + k][...] = _adamw(
                ins[k][...], grads[k], ins[n_p + k][...], ins[2 * n_p + k][...])
        loss = jnp.sum(jnp.sum(s[:, 1408:1536], axis=1, keepdims=True), axis=0, keepdims=True) * (0.5 / D_MODEL)
        outs[4 * n_p][...] = loss

    n_p = n + 2
    res = pl.pallas_call(
        body, name="adamw_all",
        out_shape=[jax.ShapeDtypeStruct(p.shape, F32) for p in params] * 4 + [jax.ShapeDtypeStruct((1, 1), F32)],
        compiler_params=pltpu.CompilerParams(vmem_limit_bytes=VMEM_LIMIT),
    )(r_win, r_out, r_small, big_in[0], big_out[0], *weights, big_in[1], big_out[1], *moments_m,
      big_in[2], big_out[2], *moments_v)
    return [res[k * n_p:(k + 1) * n_p] for k in range(4)], res[4 * n_p]


def kernel(x, positions, norm_gain, w_in, q_norm_gain, k_norm_gain, sinks, w_out, loss_target, m_norm_gain, m_w_in, m_q_norm_gain, m_k_norm_gain, m_sinks, m_w_out, v_norm_gain, v_w_in, v_q_norm_gain, v_k_norm_gain, v_sinks, v_w_out):
    x2 = x.reshape(T_LOC, D_MODEL)
    tgt2 = loss_target.reshape(T_LOC, D_MODEL)
    pos2 = positions.reshape(1, T_LOC)
    half = HEAD_DIM // 2
    inv_freq = ROPE_THETA ** (-jnp.arange(half, dtype=F32) * 2.0 / HEAD_DIM)
    inv_freq = jnp.tile(inv_freq, SLAB // half).reshape(SLAB, 1)
    sin_sign = jnp.tile(jnp.concatenate([-jnp.ones((half,), F32), jnp.ones((half,), F32)]), 2).reshape(1, SLAB)
    q_gain2 = jnp.tile(q_norm_gain, (1, 2))
    k_gain2 = jnp.tile(k_norm_gain, (1, 2))

    win_t = gather_weights(w_in.reshape(D_MODEL, IN_SHARD).T)

    (qa_raw, ka_raw, q_rot, k_dup, v_dup, ga, qb, kb, vb, gb, cos, sin_s, wout) = fwd_proj(
        x2, pos2, norm_gain, win_t, inv_freq, sin_sign, q_gain2, k_gain2, w_out.reshape(OUT_SHARD, D_MODEL).astype(BF16))
    o_a = swa_fwd(q_rot, k_dup, v_dup, sinks)
    o_b, carries = sb_fwd(qb, kb, vb)
    dout, d_oa, d_ob, dga, dgb, dwout, loss_part = out_loss(o_a, o_b, ga, gb, x2, tgt2, wout)
    dq_rot, dk_dup, dv_dup, dsink = swa_bwd(q_rot, k_dup, v_dup, o_a, d_oa, sinks)
    dqb, dkb, dvb = sb_bwd(qb, kb, vb, d_ob, carries)
    dproj, dwin_t, dqg, dkg = bwd_dw(
        x2, norm_gain, dq_rot, dk_dup, dv_dup, qa_raw, ka_raw, cos, sin_s, q_gain2, k_gain2, dga, dgb, dqb, dkb, dvb)
    grad_x, r_win, r_small, r_out = bwd_dx(
        x2, dout, norm_gain, win_t, dproj, dwin_t, dwout, dqg, dkg, dsink, loss_part)

    w_in2, m_in2, v_in2 = (a.reshape(D_MODEL, IN_SHARD).T for a in (w_in, m_w_in, v_w_in))
    w_out2, m_out2, v_out2 = (a.reshape(OUT_SHARD, D_MODEL) for a in (w_out, m_w_out, v_w_out))
    kinds, loss = adamw_all(
        r_win, r_out, r_small, (w_in2, m_in2, v_in2), (w_out2, m_out2, v_out2),
        (norm_gain, q_norm_gain, k_norm_gain, sinks),
        (m_norm_gain, m_q_norm_gain, m_k_norm_gain, m_sinks), (v_norm_gain, v_q_norm_gain, v_k_norm_gain, v_sinks))

    def leaves(k):
        big_in, big_out, ng, qg, kg, sk = kinds[k]
        return (ng, big_in.T.reshape(1, D_MODEL, IN_SHARD), qg, kg, sk, big_out.reshape(1, OUT_SHARD, D_MODEL))

    return (loss.reshape(()), grad_x.reshape(B_LOC, SEQ, D_MODEL), *leaves(0), *leaves(1), *leaves(2), *leaves(3))
```

```python
import functools

import jax
import jax.numpy as jnp
from jax import lax
from jax.experimental import pallas as pl
from jax.experimental.pallas import tpu as pltpu

F32 = jnp.float32
BF16 = jnp.bfloat16

N_DEV = 8
D_MODEL = 1024
SEQ = 2048
B_LOC = 2
T_LOC = B_LOC * SEQ
HEAD_DIM = 64
HEAD_SHIFT = 6
BLK = 128
N_BLK = SEQ // BLK
SLAB = 128
IN_WIDTH = 3328
IN_SHARD = IN_WIDTH // N_DEV
OUT_SHARD = D_MODEL // N_DEV
EPS = 1e-6
ROPE_THETA = 10000.0
Q_SCALE = 0.125
R_QA, R_KA, R_VA, R_GA, R_QB, R_KB, R_VB, R_GB, R_END = 0, 512, 640, 768, 1280, 1792, 2304, 2816, 3328
SMALL_W = 1536
ADAM_LR, ADAM_B1, ADAM_B2, ADAM_EPS, ADAM_WD, ADAM_STEP = 0.001, 0.9, 0.999, 1e-08, 0.01, 10
TM = 512
ACC_ROWS = 256
GATHER_CHUNKS = 2
VMEM_LIMIT = 56 * 1024 * 1024

MESH = pl.DeviceIdType.MESH
NT = (((1,), (1,)), ((), ()))
TN = (((0,), (0,)), ((), ()))


def _params(sem, limit=VMEM_LIMIT):
    return pltpu.CompilerParams(dimension_semantics=sem, vmem_limit_bytes=limit)


def _dot(a, b, dims=None):
    if dims is None:
        return jnp.dot(a, b, preferred_element_type=F32)
    return lax.dot_general(a, b, dims, preferred_element_type=F32)


def _lane(shape):
    return lax.broadcasted_iota(jnp.int32, shape, len(shape) - 1)


def _row(shape):
    return lax.broadcasted_iota(jnp.int32, shape, 0)


def _head_blockdiag():
    return ((_row((SLAB, SLAB)) >> HEAD_SHIFT) == (_lane((SLAB, SLAB)) >> HEAD_SHIFT)).astype(BF16)


def _head_sum(x, bd):
    return _dot(x.astype(BF16), bd)


def _swap_half(y, lane):
    return jnp.where((lane & 32) != 0, pltpu.roll(y, 32, 1), pltpu.roll(y, 96, 1))


def _stack_heads(q, lane):
    zero = jnp.zeros_like(q)
    return jnp.concatenate([jnp.where(lane < HEAD_DIM, q, zero), jnp.where(lane >= HEAD_DIM, q, zero)], axis=0)


def _unstack_heads(x2, lane):
    return jnp.where(lane < HEAD_DIM, x2[:BLK], x2[BLK:])


def _sigmoid(x):
    return 1.0 / (1.0 + jnp.exp(-x))


def _mesh_pos():
    return lax.axis_index("x"), lax.axis_index("y"), lax.axis_index("c")


def _flip(pos, mask):
    return tuple(1 - p if m else p for p, m in zip(pos, mask))


def _lin(pos):
    return 4 * pos[0] + 2 * pos[1] + pos[2]


DEV_FLIPS = [(fx, fy, fc) for fx in (0, 1) for fy in (0, 1) for fc in (0, 1)][1:]


def _direct_exchange(src_for, dst_slot, send_sems, recv_sems, local_sem):
    me = _mesh_pos()

    def copy(k, to):
        return pltpu.make_async_remote_copy(
            src_ref=src_for(to), dst_ref=dst_slot(me), send_sem=send_sems.at[k], recv_sem=recv_sems.at[k],
            device_id=to, device_id_type=MESH)

    def landed(k, frm):
        return pltpu.make_async_remote_copy(
            src_ref=src_for(frm), dst_ref=dst_slot(frm), send_sem=send_sems.at[k], recv_sem=recv_sems.at[k],
            device_id=frm, device_id_type=MESH)

    local = None if local_sem is None else pltpu.make_async_copy(src_for(me), dst_slot(me), local_sem)
    peers = [_flip(me, f) for f in DEV_FLIPS]

    def start():
        if local is not None:
            local.start()
        for k, to in enumerate(peers):
            copy(k, to).start()

    def finish():
        for k, frm in enumerate(peers):
            landed(k, frm).wait_recv()
        for k, to in enumerate(peers):
            copy(k, to).wait_send()
        if local is not None:
            local.wait()

    return start, finish


def gather_weights(shard):
    m = shard.shape[0]
    piece = m // GATHER_CHUNKS
    pieces = range(GATHER_CHUNKS)

    def body(f32_ref, o_ref, a_ref, ici_send, ici_recv, d2d_send, d2d_recv, local_sem):
        a_ref[...] = f32_ref[...].astype(BF16)
        x, y, c = _mesh_pos()
        me, sibling = (x, y, c), (x, y, 1 - c)
        chip_x, chip_y, chip_d = (1 - x, y), (x, 1 - y), (1 - x, 1 - y)

        def rows(pos, q):
            return o_ref.at[pl.ds(_lin(pos) * m + q * piece, piece), :]

        def own(q):
            return a_ref.at[pl.ds(q * piece, piece), :]

        def ici(k, q, block, to, src=None):
            return pltpu.make_async_remote_copy(
                src_ref=rows(block, q) if src is None else src, dst_ref=rows(block, q),
                send_sem=ici_send.at[k, q], recv_sem=ici_recv.at[k, q], device_id=to, device_id_type=MESH)

        def d2d(k, q, chip, mine, src=None):
            block = (*chip, c) if mine else (*chip, 1 - c)
            return pltpu.make_async_remote_copy(
                src_ref=rows(block, q) if src is None else src, dst_ref=rows(block, q),
                send_sem=d2d_send.at[k, q], recv_sem=d2d_recv.at[k, q], device_id=sibling, device_id_type=MESH)

        local = pltpu.make_async_copy(a_ref, o_ref.at[pl.ds(_lin(me) * m, m), :], local_sem)
        local.start()
        sends = []
        for q in pieces:
            sends += [ici(0, q, me, (*chip_x, c), src=own(q)), ici(1, q, me, (*chip_y, c), src=own(q)),
                      d2d(0, q, (x, y), True, src=own(q))]
        for cp in sends:
            cp.start()

        def pass_on(first, k_first, second, k_second, onward):
            moved = []
            for q in pieces:
                ici(k_first, q, (*first, c), me).wait_recv()
                moved += [ici(2, q, (*first, c), (*onward, c)), d2d(1 + k_first, q, first, True)]
                for cp in moved[-2:]:
                    cp.start()
            for q in pieces:
                ici(k_second, q, (*second, c), me).wait_recv()
                moved.append(d2d(1 + k_second, q, second, True))
                moved[-1].start()
            for q in pieces:
                ici(2, q, (*chip_d, c), me).wait_recv()
                moved.append(d2d(3, q, chip_d, True))
                moved[-1].start()
            for cp in moved:
                cp.wait_send()

        @pl.when(c == 0)
        def _():
            pass_on(chip_y, 1, chip_x, 0, chip_x)

        @pl.when(c == 1)
        def _():
            pass_on(chip_x, 0, chip_y, 1, chip_y)

        for k, chip in enumerate([(x, y), chip_x, chip_y, chip_d]):
            for q in pieces:
                d2d(k, q, chip, False).wait_recv()
        for cp in sends:
            cp.wait_send()
        local.wait()

    vmem = pl.BlockSpec(memory_space=pltpu.VMEM)
    n_q = GATHER_CHUNKS
    return pl.pallas_call(
        body, name="gather_weights",
        out_shape=jax.ShapeDtypeStruct((N_DEV * m, shard.shape[1]), BF16),
        in_specs=[vmem], out_specs=vmem,
        scratch_shapes=[pltpu.VMEM(shard.shape, BF16), pltpu.SemaphoreType.DMA((3, n_q)), pltpu.SemaphoreType.DMA((3, n_q)),
                        pltpu.SemaphoreType.DMA((4, n_q)), pltpu.SemaphoreType.DMA((4, n_q)), pltpu.SemaphoreType.DMA],
        compiler_params=pltpu.CompilerParams(vmem_limit_bytes=VMEM_LIMIT),
    )(shard)


def _norm_rope(xs, gain2, cos, sin_s, bd, lane):
    r = lax.rsqrt(_head_sum(xs * xs, bd) * (1.0 / HEAD_DIM) + EPS)
    y = xs * r * gain2
    return y * cos + _swap_half(y, lane) * sin_s


def _dup_heads(xs, lane):
    r = pltpu.roll(xs, HEAD_DIM, 1)
    lo = lane < HEAD_DIM
    return jnp.concatenate([jnp.where(lo, xs, r), jnp.where(lo, r, xs)], axis=1)


def fwd_proj(x, pos, norm_gain, win_t, inv_freq, sin_sign, q_gain2, k_gain2, wout_shard):
    n_tiles = T_LOC // TM

    def body(x_ref, pos_ref, ng_ref, w_ref, if_ref, sg_ref, qg_ref, kg_ref, ws_hbm,
             qa_raw_ref, ka_raw_ref, q_rot_ref, k_dup_ref, v_dup_ref, ga_ref, qb_ref, kb_ref, vb_ref, gb_ref,
             cos_ref, sin_ref, wo_hbm, wo_send, wo_recv, wo_local):
        start_wout, finish_wout = _direct_exchange(
            lambda dev: ws_hbm, lambda dev: wo_hbm.at[pl.ds(_lin(dev) * OUT_SHARD, OUT_SHARD), :],
            wo_send, wo_recv, wo_local)
        pl.when(pl.program_id(0) == 0)(start_wout)

        xv = x_ref[...]
        rstd = lax.rsqrt(jnp.mean(xv * xv, axis=-1, keepdims=True) + EPS)
        h = (xv * rstd * ng_ref[...]).astype(BF16)

        def proj(r0, r1):
            return _dot(h, w_ref[r0:r1, :], NT)

        ang_t = if_ref[...] * pos_ref[...].astype(F32)
        cos = jnp.cos(ang_t).T
        sin_s = jnp.sin(ang_t).T * sg_ref[...]
        cos_ref[...] = cos
        sin_ref[...] = sin_s
        lane = _lane((TM, SLAB))
        bd = _head_blockdiag()

        qa = proj(R_QA, R_KA)
        qa_raw_ref[...] = qa
        for p in range(4):
            sl = slice(p * SLAB, (p + 1) * SLAB)
            q_rot_ref[:, sl] = (_norm_rope(qa[:, sl], qg_ref[...], cos, sin_s, bd, lane) * Q_SCALE).astype(BF16)
        ka = proj(R_KA, R_VA)
        ka_raw_ref[...] = ka
        k_dup_ref[...] = _dup_heads(_norm_rope(ka, kg_ref[...], cos, sin_s, bd, lane), lane).astype(BF16)
        v_dup_ref[...] = _dup_heads(proj(R_VA, R_GA), lane).astype(BF16)
        ga_ref[...] = proj(R_GA, R_QB).astype(BF16)
        qb_ref[...] = (proj(R_QB, R_KB) * Q_SCALE).astype(BF16)
        kb_ref[...] = proj(R_KB, R_VB).astype(BF16)
        vb_ref[...] = proj(R_VB, R_GB).astype(BF16)
        gb_ref[...] = proj(R_GB, R_END).astype(BF16)
        pl.when(pl.program_id(0) == n_tiles - 1)(finish_wout)

    def tile(w):
        return pl.BlockSpec((TM, w), lambda i: (i, 0))

    def whole(a):
        return pl.BlockSpec(a.shape, lambda i: (0, 0))

    hbm = pl.BlockSpec(memory_space=pl.ANY)
    widths = [(512, F32), (128, F32), (512, BF16), (256, BF16), (256, BF16), (512, BF16), (512, BF16), (512, BF16),
              (512, BF16), (512, BF16), (128, F32), (128, F32)]
    return pl.pallas_call(
        body, name="fwd_proj", grid=(n_tiles,),
        in_specs=[tile(D_MODEL), pl.BlockSpec((1, TM), lambda i: (0, i)), whole(norm_gain), whole(win_t),
                  whole(inv_freq), whole(sin_sign),
                  whole(q_gain2), whole(k_gain2), hbm],
        out_specs=[tile(w) for w, _ in widths] + [hbm],
        out_shape=[jax.ShapeDtypeStruct((T_LOC, w), dt) for w, dt in widths]
        + [jax.ShapeDtypeStruct((D_MODEL, D_MODEL), BF16)],
        scratch_shapes=[pltpu.SemaphoreType.DMA((7,)), pltpu.SemaphoreType.DMA((7,)), pltpu.SemaphoreType.DMA],
        compiler_params=_params(("arbitrary",)),
    )(x, pos, norm_gain, win_t, inv_freq, sin_sign, q_gain2, k_gain2, wout_shard)


def _swa_window(prev_ref, cur_ref, p):
    gsl = _slab(p // 2)
    return jnp.concatenate([prev_ref[:, gsl], cur_ref[:, gsl]], axis=0)


def _swa_probs(s, sinks_ref, p, i):
    shape = (2 * BLK, 2 * BLK)
    r = _row(shape) & (BLK - 1)
    cidx = _lane(shape)
    valid = (cidx > r) & (cidx <= r + BLK) & ((cidx >= BLK) | (i > 0))
    s = jnp.where(valid, s, -jnp.inf)
    sink = jnp.where(_row((2 * BLK, 1)) < BLK, sinks_ref[0, 2 * p], sinks_ref[0, 2 * p + 1])
    m = jnp.maximum(jnp.max(s, axis=-1, keepdims=True), sink)
    e = jnp.exp(s - m)
    e_sink = jnp.exp(sink - m)
    den = jnp.sum(e, axis=-1, keepdims=True) + e_sink
    return e / den, e_sink / den


SWA_CHAINS = [(b, p) for b in range(B_LOC) for p in range(4)]


def _swa_specs():
    def cur(w):
        return pl.BlockSpec((B_LOC, BLK, w), lambda i: (0, i, 0))

    def prev(w):
        return pl.BlockSpec((B_LOC, BLK, w), lambda i: (0, jnp.maximum(i - 1, 0), 0))

    return cur, prev


def swa_fwd(q_rot, k_dup, v_dup, sinks):
    def body(q_ref, kp_ref, kc_ref, vp_ref, vc_ref, sinks_ref, o_ref):
        i = pl.program_id(0)
        lane = _lane((BLK, SLAB))
        s = [_dot(_stack_heads(q_ref[b, :, _slab(p)], lane), _swa_window(kp_ref.at[b], kc_ref.at[b], p), NT)
             for b, p in SWA_CHAINS]
        pn = [_swa_probs(s[c], sinks_ref, p, i)[0].astype(BF16) for c, (b, p) in enumerate(SWA_CHAINS)]
        for c, (b, p) in enumerate(SWA_CHAINS):
            o = _unstack_heads(_dot(pn[c], _swa_window(vp_ref.at[b], vc_ref.at[b], p)), lane)
            o_ref[b, :, _slab(p)] = o.astype(BF16)

    cur, prev = _swa_specs()
    q3, k3, v3 = (a.reshape(B_LOC, SEQ, a.shape[1]) for a in (q_rot, k_dup, v_dup))
    return pl.pallas_call(
        body, name="swa_fwd", grid=(N_BLK,),
        in_specs=[cur(512), prev(256), cur(256), prev(256), cur(256), pl.BlockSpec(memory_space=pltpu.SMEM)],
        out_specs=cur(512),
        out_shape=jax.ShapeDtypeStruct((B_LOC, SEQ, 512), BF16),
        compiler_params=_params(("arbitrary",)),
    )(q3, k3, k3, v3, v3, sinks).reshape(T_LOC, 512)


def _tri(suffix):
    r, cidx = _row((BLK + 16, BLK)), _lane((BLK + 16, BLK))
    tri = (cidx > r) if suffix else (cidx < r)
    return (tri | (r >= BLK)).astype(BF16)


def _key_sums(tri, x):
    res = _dot(tri, x.astype(BF16))
    return res[:BLK], res[BLK:BLK + 1]


def _sb_softplus(zt, valid):
    neg_abs = lax.bitcast_convert_type(lax.bitcast_convert_type(zt, jnp.uint32) | jnp.uint32(0x80000000), F32)
    sp = jnp.maximum(zt, 0.0) + jnp.log(1.0 + jnp.exp(neg_abs))
    return sp if valid is None else jnp.where(valid, sp, 0.0)


def _sb_weights(zt, sp, later, valid):
    w = jnp.exp(zt - sp - later)
    return w if valid is None else jnp.where(valid, w, 0.0)


def _slab(pp):
    return slice(pp * SLAB, (pp + 1) * SLAB)


def _blk(j):
    return pl.ds(pl.multiple_of(j * BLK, BLK), BLK)


def _causal_t():
    return _row((BLK, 2 * BLK)) < (_lane((BLK, 2 * BLK)) & (BLK - 1))


def _sb_rows(b, j):
    return pl.ds(pl.multiple_of(b * SEQ + j * BLK, BLK), BLK)


SB_CHAINS = [(b, pp) for b in range(B_LOC) for pp in range(4)]


def sb_fwd(qb, kb, vb):
    def body(q_ref, k_ref, v_ref, o_ref, c_ref, vt_ref, ot_ref):
        for c, (b, pp) in enumerate(SB_CHAINS):
            for j in range(N_BLK):
                vt_ref[c, j] = v_ref[b * SEQ + j * BLK:b * SEQ + (j + 1) * BLK, _slab(pp)].T
        lane = _lane((BLK, SLAB))
        tri = _tri(True)
        valid = _causal_t()
        jrow = _row((N_BLK, 2 * BLK))
        chains = range(len(SB_CHAINS))

        def q_block(i, _):
            q2 = [_stack_heads(q_ref[_sb_rows(b, i), _slab(pp)], lane) for b, pp in SB_CHAINS]

            def key_block(j, carry, mask, first):
                zt = [_dot(k_ref[_sb_rows(b, j), _slab(pp)], q2[c], NT) for c, (b, pp) in enumerate(SB_CHAINS)]
                sp = [_sb_softplus(zt[c], mask) for c in chains]
                sums = [_key_sums(tri, sp[c]) for c in chains]
                w = [_sb_weights(zt[c], sp[c], sums[c][0] + carry[c], mask) for c in chains]
                for c in chains:
                    pv = _dot(vt_ref[c, j], w[c].astype(BF16))
                    if first:
                        ot_ref[c] = pv
                    else:
                        ot_ref[c] += pv
                return tuple(carry[c] + sums[c][1] for c in chains)

            def earlier(jj, state):
                carry, saved = state
                j = i - 1 - jj
                saved = tuple(jnp.where(jrow == j, carry[c], saved[c]) for c in chains)
                return key_block(j, carry, None, False), saved

            zero = tuple(jnp.zeros((1, 2 * BLK), F32) for _ in chains)
            carry = key_block(i, zero, valid, True)
            _, saved = lax.fori_loop(0, i, earlier, (carry, tuple(jnp.zeros((N_BLK, 2 * BLK), F32) for _ in chains)))
            for c, (b, pp) in enumerate(SB_CHAINS):
                o_ref[_sb_rows(b, i), _slab(pp)] = _unstack_heads(ot_ref[c].T, lane).astype(BF16)
                c_ref[c * N_BLK + i] = saved[c]
            return 0

        lax.fori_loop(0, N_BLK, q_block, 0)

    n_ch = len(SB_CHAINS)
    vmem = pl.BlockSpec(memory_space=pltpu.VMEM)
    return pl.pallas_call(
        body, name="sb_fwd",
        in_specs=[vmem] * 3, out_specs=[vmem] * 2,
        out_shape=[jax.ShapeDtypeStruct((T_LOC, 512), BF16), jax.ShapeDtypeStruct((n_ch * N_BLK, N_BLK, 2 * BLK), F32)],
        scratch_shapes=[pltpu.VMEM((n_ch, N_BLK, SLAB, BLK), BF16), pltpu.VMEM((n_ch, SLAB, 2 * BLK), F32)],
        compiler_params=pltpu.CompilerParams(vmem_limit_bytes=VMEM_LIMIT),
    )(qb, kb, vb)


def out_loss(o_a, o_b, ga, gb, x, target, wout):
    n_tiles = T_LOC // TM

    def body(oa_ref, ob_ref, ga_ref, gb_ref, x_ref, t_ref, w_ref,
             dout_ref, doa_ref, dob_ref, dga_ref, dgb_ref, dw_ref, loss_ref, acc_ref):
        step = pl.program_id(0)

        @pl.when(step == 0)
        def _():
            acc_ref[...] = jnp.zeros_like(acc_ref)
            loss_ref[...] = jnp.zeros_like(loss_ref)

        oa, ob, gav, gbv = (r[...].astype(F32) for r in (oa_ref, ob_ref, ga_ref, gb_ref))
        sa, sb = _sigmoid(gav), _sigmoid(gbv)
        silu_a, silu_b = gav * sa, gbv * sb
        y = jnp.concatenate([oa * silu_a, ob * silu_b], axis=1).astype(BF16)
        err = x_ref[...] + _dot(y, w_ref[...]) - t_ref[...]
        e2 = err * err
        part = jnp.sum(e2.reshape(TM // 8, 8, D_MODEL), axis=0)
        loss_ref[...] += functools.reduce(lambda a, b: a + b, [part[:, k * 128:(k + 1) * 128] for k in range(8)])
        dout = err * (1.0 / D_MODEL)
        dout_ref[...] = dout
        dob16 = dout.astype(BF16)
        for r0 in range(0, D_MODEL, ACC_ROWS):
            acc_ref[r0:r0 + ACC_ROWS, :] += _dot(y[:, r0:r0 + ACC_ROWS], dob16, TN)
        dy = _dot(dob16, w_ref[...], NT)
        dya, dyb = dy[:, :512], dy[:, 512:]
        doa_ref[...] = (dya * silu_a).astype(BF16)
        dob_ref[...] = (dyb * silu_b).astype(BF16)
        dga_ref[...] = (dya * oa * (sa * (1.0 + gav * (1.0 - sa)))).astype(BF16)
        dgb_ref[...] = (dyb * ob * (sb * (1.0 + gbv * (1.0 - sb)))).astype(BF16)

        @pl.when(step == n_tiles - 1)
        def _():
            dw_ref[...] = acc_ref[...].astype(BF16)

    def tile(w):
        return pl.BlockSpec((TM, w), lambda i: (i, 0))

    const = lambda i: (0, 0)
    return pl.pallas_call(
        body, name="out_loss", grid=(n_tiles,),
        in_specs=[tile(512)] * 4 + [tile(D_MODEL)] * 2 + [pl.BlockSpec((D_MODEL, D_MODEL), const)],
        out_specs=[tile(D_MODEL), tile(512), tile(512), tile(512), tile(512),
                   pl.BlockSpec((D_MODEL, D_MODEL), const), pl.BlockSpec((8, 128), const)],
        out_shape=[jax.ShapeDtypeStruct((T_LOC, D_MODEL), F32)] + [jax.ShapeDtypeStruct((T_LOC, 512), BF16)] * 4
        + [jax.ShapeDtypeStruct((D_MODEL, D_MODEL), BF16), jax.ShapeDtypeStruct((8, 128), F32)],
        scratch_shapes=[pltpu.VMEM((D_MODEL, D_MODEL), F32)],
        compiler_params=_params(("arbitrary",)),
    )(o_a, o_b, ga, gb, x, target, wout)


def swa_bwd(q_rot, k_dup, v_dup, o_a, d_oa, sinks):
    def body(q_ref, kp_ref, kc_ref, vp_ref, vc_ref, o_ref, do_ref, sinks_ref, dq_ref, dk_ref, dv_ref, dsink_ref):
        i = pl.program_id(0)

        @pl.when(i == 0)
        def _():
            dk_ref[...] = jnp.zeros_like(dk_ref)
            dv_ref[...] = jnp.zeros_like(dv_ref)
            dsink_ref[...] = jnp.zeros_like(dsink_ref)

        lane = _lane((BLK, SLAB))
        rows_prev, rows_cur = _blk(jnp.maximum(i - 1, 0)), _blk(i)
        chains = range(len(SWA_CHAINS))
        q2 = [_stack_heads(q_ref[b, :, _slab(p)], lane) for b, p in SWA_CHAINS]
        do2 = [_stack_heads(do_ref[b, :, _slab(p)], lane) for b, p in SWA_CHAINS]
        keys = [_swa_window(kp_ref.at[b], kc_ref.at[b], p) for b, p in SWA_CHAINS]
        s = [_dot(q2[c], keys[c], NT) for c in chains]
        dp = [_dot(do2[c], _swa_window(vp_ref.at[b], vc_ref.at[b], p), NT) for c, (b, p) in enumerate(SWA_CHAINS)]
        ds, pn16, cols = [], [], []
        for c, (b, p) in enumerate(SWA_CHAINS):
            pn, p_sink = _swa_probs(s[c], sinks_ref, p, i)
            o = o_ref[b, :, _slab(p)].astype(F32)
            delta =jnp.sum(do2[c].astype(F32) * jnp.concatenate([o, o], axis=0), axis=-1, keepdims=True)
            ds.append((pn * (dp[c] - delta)).astype(BF16))
            pn16.append(pn.astype(BF16))
            cols.append(-p_sink * delta)
        for c, (b, p) in enumerate(SWA_CHAINS):
            dq_ref[b, :, _slab(p)] = _unstack_heads(_dot(ds[c], keys[c]), lane) * Q_SCALE
        dk2 = [_dot(ds[c], q2[c], TN) for c in chains]
        dv2 = [_dot(pn16[c], do2[c], TN) for c in chains]
        for c, (b, p) in enumerate(SWA_CHAINS):
            gsl = _slab(p // 2)
            dk_ref[b, rows_prev, gsl] += dk2[c][:BLK]
            dk_ref[b, rows_cur, gsl] += dk2[c][BLK:]
            dv_ref[b, rows_prev, gsl] += dv2[c][:BLK]
            dv_ref[b, rows_cur, gsl] += dv2[c][BLK:]
            for e in range(2):
                dsink_ref[2 * p + e:2 * p + e + 1, :] += jnp.sum(cols[c][e * BLK:(e + 1) * BLK], axis=0, keepdims=True)

    cur, prev = _swa_specs()
    whole = pl.BlockSpec((B_LOC, SEQ, 256), lambda i: (0, 0, 0))
    q3, k3, v3, o3, do3 = (a.reshape(B_LOC, SEQ, a.shape[1]) for a in (q_rot, k_dup, v_dup, o_a, d_oa))
    dq, dk, dv, dsink = pl.pallas_call(
        body, name="swa_bwd", grid=(N_BLK,),
        in_specs=[cur(512), prev(256), cur(256), prev(256), cur(256), cur(512), cur(512),
                  pl.BlockSpec(memory_space=pltpu.SMEM)],
        out_specs=[cur(512), whole, whole, pl.BlockSpec((8, 128), lambda i: (0, 0))],
        out_shape=[jax.ShapeDtypeStruct((B_LOC, SEQ, 512), F32), jax.ShapeDtypeStruct((B_LOC, SEQ, 256), F32),
                   jax.ShapeDtypeStruct((B_LOC, SEQ, 256), F32), jax.ShapeDtypeStruct((8, 128), F32)],
        compiler_params=_params(("arbitrary",)),
    )(q3, k3, k3, v3, v3, o3, do3, sinks)
    return dq.reshape(T_LOC, 512), dk.reshape(T_LOC, 256), dv.reshape(T_LOC, 256), dsink


def sb_bwd(qb, kb, vb, d_ob, carries, dwout):
    def body(q_ref, k_ref, v_ref, do_ref, c_ref, dw_hbm, dq_ref, dk_ref, dv_ref, rw_hbm, kt_ref, dqt_ref,
             rw_send, rw_recv, rw_local):
        start_dwout, finish_dwout = _direct_exchange(
            lambda dev: dw_hbm.at[pl.ds(_lin(dev) * OUT_SHARD, OUT_SHARD), :], lambda dev: rw_hbm.at[_lin(dev)],
            rw_send, rw_recv, rw_local)
        start_dwout()
        for c, (b, pp) in enumerate(SB_CHAINS):
            for j in range(N_BLK):
                kt_ref[c, j] = k_ref[b * SEQ + j * BLK:b * SEQ + (j + 1) * BLK, _slab(pp)].T
        dk_ref[...] = jnp.zeros_like(dk_ref)
        dv_ref[...] = jnp.zeros_like(dv_ref)
        dqt_ref[...] = jnp.zeros_like(dqt_ref)
        lane = _lane((BLK, SLAB))
        tri_after, tri_before = _tri(True), _tri(False)
        valid = _causal_t()
        jrow = _row((N_BLK, 2 * BLK))
        chains = range(len(SB_CHAINS))

        def q_block(i, _):
            q2 = [_stack_heads(q_ref[_sb_rows(b, i), _slab(pp)], lane) for b, pp in SB_CHAINS]
            do2 = [_stack_heads(do_ref[_sb_rows(b, i), _slab(pp)], lane) for b, pp in SB_CHAINS]

            def key_block(j, carry_sp, before_u, mask):
                at = [(_sb_rows(b, j), _slab(pp)) for b, pp in SB_CHAINS]
                zt = [_dot(k_ref[at[c]], q2[c], NT) for c in chains]
                dw = [_dot(v_ref[at[c]], do2[c], NT) for c in chains]
                sp = [_sb_softplus(zt[c], mask) for c in chains]
                later = [_key_sums(tri_after, sp[c])[0] for c in chains]
                w = [_sb_weights(zt[c], sp[c], later[c] + carry_sp[c], mask) for c in chains]
                u = [dw[c] * w[c] for c in chains]
                for c in chains:
                    dv_ref[at[c]] += _dot(w[c].astype(BF16), do2[c])
                sums = [_key_sums(tri_before, u[c]) for c in chains]
                dz16 = []
                for c in chains:
                    sig = jnp.exp(zt[c] - sp[c])
                    dz = u[c] - sig * (u[c] + before_u[c] + sums[c][0])
                    if mask is not None:
                        dz = jnp.where(mask, dz, 0.0)
                    dz16.append(dz.astype(BF16))
                for c in chains:
                    dk_ref[at[c]] += _dot(dz16[c], q2[c])
                    dqt_ref[c] += _dot(kt_ref[c, j], dz16[c])
                return tuple(before_u[c] + sums[c][1] for c in chains)

            def earlier(j, before_u):
                carry_sp = [jnp.sum(jnp.where(jrow == j, c_ref[c * N_BLK + i], 0.0), axis=0, keepdims=True)
                            for c in chains]
                return key_block(j, carry_sp, before_u, None)

            zero = tuple(jnp.zeros((1, 2 * BLK), F32) for _ in chains)
            before_u = lax.fori_loop(0, i, earlier, zero)
            key_block(i, zero, before_u, valid)
            for c, (b, pp) in enumerate(SB_CHAINS):
                dq_ref[_sb_rows(b, i), _slab(pp)] = (_unstack_heads(dqt_ref[c].T, lane) * Q_SCALE).astype(BF16)
                dqt_ref[c] = jnp.zeros((SLAB, 2 * BLK), F32)
            return 0

        lax.fori_loop(0, N_BLK, q_block, 0)
        finish_dwout()

    n_ch = len(SB_CHAINS)
    vmem, hbm = pl.BlockSpec(memory_space=pltpu.VMEM), pl.BlockSpec(memory_space=pl.ANY)
    return pl.pallas_call(
        body, name="sb_bwd",
        in_specs=[vmem] * 5 + [hbm], out_specs=[vmem] * 3 + [hbm],
        out_shape=[jax.ShapeDtypeStruct((T_LOC, 512), BF16)] + [jax.ShapeDtypeStruct((T_LOC, 512), F32)] * 2
        + [jax.ShapeDtypeStruct((N_DEV, OUT_SHARD, D_MODEL), BF16)],
        scratch_shapes=[pltpu.VMEM((n_ch, N_BLK, SLAB, BLK), BF16), pltpu.VMEM((n_ch, SLAB, 2 * BLK), F32),
                        pltpu.SemaphoreType.DMA((7,)), pltpu.SemaphoreType.DMA((7,)), pltpu.SemaphoreType.DMA],
        compiler_params=pltpu.CompilerParams(vmem_limit_bytes=VMEM_LIMIT),
    )(qb, kb, vb, d_ob, carries, dwout)


def bwd_dw(x, norm_gain, dq_rot, dk_dup, dv_dup, qa_raw, ka_raw, cos, sin_s, q_gain2, k_gain2, dga, dgb, dqb, dkb, dvb):
    n_tiles = T_LOC // TM

    def body(x_ref, ng_ref, dq_ref, dk_ref, dv_ref, qa_ref, ka_ref, cos_ref, sin_ref, qg_ref, kg_ref,
             dga_ref, dgb_ref, dqb_ref, dkb_ref, dvb_ref,
             dproj_ref, dw_hbm, dqg_ref, dkg_ref, acc_ref, stage_ref):
        step = pl.program_id(0)

        @pl.when(step == 0)
        def _():
            acc_ref[...] = jnp.zeros_like(acc_ref)
            dqg_ref[...] = jnp.zeros_like(dqg_ref)
            dkg_ref[...] = jnp.zeros_like(dkg_ref)

        lane = _lane((TM, SLAB))
        bd = _head_blockdiag()
        cos, sin_s = cos_ref[...], sin_ref[...]

        def norm_rope_bwd(d_rot, raw, gain2):
            dy = d_rot * cos + _swap_half(d_rot * sin_s, lane)
            r = lax.rsqrt(_head_sum(raw * raw, bd) * (1.0 / HEAD_DIM) + EPS)
            xhat = raw * r
            dgain = jnp.sum(dy * xhat, axis=0, keepdims=True)
            dxh = dy * gain2
            mean = _head_sum(dxh * xhat, bd) * (1.0 / HEAD_DIM)
            return r * (dxh - xhat * mean), dgain

        def fold_dup(d_dup):
            a, b2 = d_dup[:, :SLAB], d_dup[:, SLAB:]
            return jnp.where(lane < HEAD_DIM, a + pltpu.roll(a, HEAD_DIM, 1), b2 + pltpu.roll(b2, HEAD_DIM, 1))

        pieces = []
        dqg = jnp.zeros((1, SLAB), F32)
        for p in range(4):
            sl = slice(p * SLAB, (p + 1) * SLAB)
            d_raw, dg = norm_rope_bwd(dq_ref[:, sl], qa_ref[:, sl], qg_ref[...])
            pieces.append(d_raw.astype(BF16))
            dqg = dqg + dg
        d_raw, dkg = norm_rope_bwd(fold_dup(dk_ref[...]), ka_ref[...], kg_ref[...])
        pieces.append(d_raw.astype(BF16))
        pieces.append(fold_dup(dv_ref[...]).astype(BF16))
        pieces += [dga_ref[...], dqb_ref[...], dkb_ref[...].astype(BF16), dvb_ref[...].astype(BF16),
                   dgb_ref[...]]
        dproj = jnp.concatenate(pieces, axis=1)
        dproj_ref[...] = dproj
        dqg_ref[0:1, :] += dqg + pltpu.roll(dqg, HEAD_DIM, 1)
        dkg_ref[0:1, :] += dkg + pltpu.roll(dkg, HEAD_DIM, 1)

        xv = x_ref[...]
        rstd = lax.rsqrt(jnp.mean(xv * xv, axis=-1, keepdims=True) + EPS)
        h = (xv * rstd * ng_ref[...]).astype(BF16)
        for r0 in range(0, IN_WIDTH, ACC_ROWS):
            acc_ref[r0:r0 + ACC_ROWS, :] += _dot(dproj[:, r0:r0 + ACC_ROWS], h, TN)

        @pl.when(step == n_tiles - 1)
        def _():
            for r0 in range(0, IN_WIDTH, ACC_ROWS):
                stage_ref[...] = acc_ref[r0:r0 + ACC_ROWS, :].astype(BF16)
                pltpu.sync_copy(stage_ref, dw_hbm.at[r0:r0 + ACC_ROWS, :])

    def tile(w):
        return pl.BlockSpec((TM, w), lambda i: (i, 0))

    def whole(a):
        return pl.BlockSpec(a.shape, lambda i: (0, 0))

    const = lambda i: (0, 0)
    return pl.pallas_call(
        body, name="bwd_dw", grid=(n_tiles,),
        in_specs=[tile(D_MODEL), whole(norm_gain),
                  tile(512), tile(256), tile(256), tile(512), tile(128), tile(128), tile(128),
                  whole(q_gain2), whole(k_gain2), tile(512), tile(512), tile(512), tile(512), tile(512)],
        out_specs=[tile(IN_WIDTH), pl.BlockSpec(memory_space=pl.ANY),
                   pl.BlockSpec((8, SLAB), const), pl.BlockSpec((8, SLAB), const)],
        out_shape=[jax.ShapeDtypeStruct((T_LOC, IN_WIDTH), BF16), jax.ShapeDtypeStruct((IN_WIDTH, D_MODEL), BF16),
                   jax.ShapeDtypeStruct((8, SLAB), F32), jax.ShapeDtypeStruct((8, SLAB), F32)],
        scratch_shapes=[pltpu.VMEM((IN_WIDTH, D_MODEL), F32), pltpu.VMEM((ACC_ROWS, D_MODEL), BF16)],
        compiler_params=_params(("arbitrary",)),
    )(x, norm_gain, dq_rot, dk_dup, dv_dup, qa_raw, ka_raw, cos, sin_s, q_gain2, k_gain2, dga, dgb, dqb, dkb, dvb)


def bwd_dx(x, dout, norm_gain, win_t, dproj, dwin_t, dqg, dkg, dsink, loss_part):
    n_tiles = T_LOC // TM
    rows_per = IN_SHARD
    step_sums, step_merge = 1, 3

    def body(x_ref, dout_ref, ng_ref, w_hbm, dp_ref, a_hbm, dqg_ref, dkg_ref, dsink_ref, loss_ref,
             gx_ref, ra_hbm, rs_hbm, w_ref, dng_ref, s_ref, own_ref, sib_ref, snd_ref, extra_ref,
             w_sem, d2d_send, d2d_recv, ici_send, ici_recv, own_sems, s_send, s_recv, out_sem):
        step = pl.program_id(0)
        x, y, c = _mesh_pos()
        me, sibling = (x, y, c), (x, y, 1 - c)
        chips = {"own": (x, y), "x": (1 - x, y), "y": (x, 1 - y), "d": (1 - x, 1 - y)}
        index = {"own": 0, "x": 1, "y": 2, "d": 3}
        order = ("d", "x", "y", "own")

        def rows(pos):
            return a_hbm.at[pl.ds(_lin(pos) * rows_per, rows_per), :]

        def to_sibling(k):
            return pltpu.make_async_remote_copy(
                src_ref=rows((*chips[k], 1 - c)), dst_ref=sib_ref.at[index[k]],
                send_sem=d2d_send.at[index[k]], recv_sem=d2d_recv.at[index[k]], device_id=sibling, device_id_type=MESH)

        def mine(k):
            return pltpu.make_async_copy(rows((*chips[k], c)), own_ref.at[index[k]], own_sems.at[index[k]])

        def ici(n, to_chip, dst):
            return pltpu.make_async_remote_copy(
                src_ref=snd_ref.at[n], dst_ref=dst, send_sem=ici_send.at[n], recv_sem=ici_recv.at[n],
                device_id=(*chips[to_chip], c), device_id_type=MESH)

        def chip_sum(k):
            to_sibling(k).wait_recv()
            mine(k).wait()
            return own_ref[index[k]].astype(F32) + sib_ref[index[k]].astype(F32)

        def by_core(fn):
            pl.when(c == 0)(lambda: fn("x", "y"))
            pl.when(c == 1)(lambda: fn("y", "x"))

        @pl.when(step == 0)
        def _():
            cp = pltpu.make_async_copy(w_hbm, w_ref, w_sem)
            cp.start()
            for k in order:
                to_sibling(k).start()
                mine(k).start()
            dng_ref[...] = jnp.zeros_like(dng_ref)
            cp.wait()

        @pl.when(step == step_sums)
        def _():
            def first_sends(direct, via):
                snd_ref[0] = chip_sum("d").astype(BF16)
                ici(0, direct, extra_ref).start()
                snd_ref[1] = chip_sum(direct).astype(BF16)
                ici(1, direct, ra_hbm.at[index[direct]]).start()
            by_core(first_sends)

        @pl.when(step == step_merge)
        def _():
            def merge(direct, via):
                merged = chip_sum(via)
                ici(0, direct, extra_ref).wait_recv()
                snd_ref[2] = (merged + extra_ref[...].astype(F32)).astype(BF16)
                ici(2, via, ra_hbm.at[index[via]]).start()
                own_ref[0] = chip_sum("own").astype(BF16)
                pltpu.make_async_copy(own_ref.at[0], ra_hbm.at[0], out_sem).start()
            by_core(merge)

        xv = x_ref[...]
        rstd = lax.rsqrt(jnp.mean(xv * xv, axis=-1, keepdims=True) + EPS)
        xhat = xv * rstd
        gain = ng_ref[...]
        dh = _dot(dp_ref[...], w_ref[...])
        dng_ref[0:1, :] += jnp.sum(dh * xhat, axis=0, keepdims=True)
        dxh = dh * gain
        gx_ref[...] = dout_ref[...] + rstd * (dxh - xhat * jnp.mean(dxh * xhat, axis=-1, keepdims=True))

        @pl.when(step == n_tiles - 1)
        def _():
            s_ref[...] = jnp.concatenate(
                [dng_ref[...], dqg_ref[...], dkg_ref[...], dsink_ref[...], loss_ref[...]], axis=1)
            start_small, finish_small = _direct_exchange(
                lambda dev: s_ref, lambda dev: rs_hbm.at[_lin(dev)], s_send, s_recv, out_sem)

            def finish(direct, via):
                ici(1, direct, ra_hbm.at[index[direct]]).wait_recv()
                ici(2, via, ra_hbm.at[index[via]]).wait_recv()
                for n, to in ((0, direct), (1, direct), (2, via)):
                    ici(n, to, extra_ref).wait_send()
            by_core(finish)
            pltpu.make_async_copy(own_ref.at[0], ra_hbm.at[0], out_sem).wait()
            for k in order:
                to_sibling(k).wait_send()
            start_small()
            finish_small()

    def tile(w):
        return pl.BlockSpec((TM, w), lambda i: (i, 0))

    def whole(a):
        return pl.BlockSpec(a.shape, lambda i: (0, 0))

    hbm = pl.BlockSpec(memory_space=pl.ANY)
    block = (rows_per, D_MODEL)
    return pl.pallas_call(
        body, name="bwd_dx", grid=(n_tiles,),
        in_specs=[tile(D_MODEL), tile(D_MODEL), whole(norm_gain), hbm, tile(IN_WIDTH), hbm,
                  whole(dqg), whole(dkg), whole(dsink), whole(loss_part)],
        out_specs=[tile(D_MODEL), hbm, hbm],
        out_shape=[jax.ShapeDtypeStruct((T_LOC, D_MODEL), F32), jax.ShapeDtypeStruct((3,) + block, BF16),
                   jax.ShapeDtypeStruct((N_DEV, 8, SMALL_W), F32)],
        scratch_shapes=[pltpu.VMEM((IN_WIDTH, D_MODEL), BF16), pltpu.VMEM((8, D_MODEL), F32),
                        pltpu.VMEM((8, SMALL_W), F32),
                        pltpu.VMEM((4,) + block, BF16), pltpu.VMEM((4,) + block, BF16), pltpu.VMEM((3,) + block, BF16),
                        pltpu.VMEM(block, BF16),
                        pltpu.SemaphoreType.DMA, pltpu.SemaphoreType.DMA((4,)), pltpu.SemaphoreType.DMA((4,)),
                        pltpu.SemaphoreType.DMA((3,)), pltpu.SemaphoreType.DMA((3,)), pltpu.SemaphoreType.DMA((4,)),
                        pltpu.SemaphoreType.DMA((7,)), pltpu.SemaphoreType.DMA((7,)), pltpu.SemaphoreType.DMA],
        compiler_params=_params(("arbitrary",)),
    )(x, dout, norm_gain, win_t, dproj, dwin_t, dqg, dkg, dsink, loss_part)


def _adamw(w, g, m, v):
    m = ADAM_B1 * m + (1.0 - ADAM_B1) * g
    v = ADAM_B2 * v + (1.0 - ADAM_B2) * (g * g)
    m_hat = m / (1.0 - ADAM_B1 ** ADAM_STEP)
    v_hat = v / (1.0 - ADAM_B2 ** ADAM_STEP)
    delta = -ADAM_LR * (m_hat / (jnp.sqrt(v_hat) + ADAM_EPS) + ADAM_WD * w)
    return delta, m, v


def _sum_slots(r_ref):
    g = r_ref[0].astype(F32)
    for s in range(1, r_ref.shape[0]):
        g = g + r_ref[s].astype(F32)
    return g


def adamw_all(r_win, r_out, r_small, big_in, big_out, weights, moments_m, moments_v):
    n = len(weights)
    params = [big_in[0], big_out[0], *weights]

    def body(rw_ref, ro_ref, rs_ref, *refs):
        n_p = n + 2
        ins, outs = refs[:3 * n_p], refs[3 * n_p:]
        s = _sum_slots(rs_ref)
        eye = (_row((8, SLAB)) == _lane((8, SLAB))).astype(F32)
        sinks = jnp.sum(s[:, 1280:1408] * eye, axis=0, keepdims=True)
        grads = [_sum_slots(rw_ref), _sum_slots(ro_ref),
                 s[0:1, :D_MODEL], s[0:1, 1024:1024 + HEAD_DIM], s[0:1, 1152:1152 + HEAD_DIM], sinks[:, :8]]
        for k in range(n_p):
            outs[k][...] = grads[k]
            outs[n_p + k][...], outs[2 * n_p + k][...], outs[3 * n_p + k][...] = _adamw(
                ins[k][...], grads[k], ins[n_p + k][...], ins[2 * n_p + k][...])
        loss = jnp.sum(jnp.sum(s[:, 1408:1536], axis=1, keepdims=True), axis=0, keepdims=True) * (0.5 / D_MODEL)
        outs[4 * n_p][...] = loss

    n_p = n + 2

    def split(shape):
        n_lead = len(shape) - 2
        return pl.BlockSpec((*shape[:-2], shape[-2] // GATHER_CHUNKS, shape[-1]), lambda i: (0,) * n_lead + (i, 0))

    def whole(shape):
        return pl.BlockSpec(shape, lambda i: (0,) * len(shape))

    param_specs = [split(big_in[0].shape), split(big_out[0].shape)] + [whole(w.shape) for w in weights]
    res = pl.pallas_call(
        body, name="adamw_all", grid=(GATHER_CHUNKS,),
        in_specs=[split(r_win.shape), split(r_out.shape), whole(r_small.shape)] + param_specs * 3,
        out_specs=param_specs * 4 + [whole((1, 1))],
        out_shape=[jax.ShapeDtypeStruct(p.shape, F32) for p in params] * 4 + [jax.ShapeDtypeStruct((1, 1), F32)],
        compiler_params=_params(("arbitrary",)),
    )(r_win, r_out, r_small, big_in[0], big_out[0], *weights, big_in[1], big_out[1], *moments_m,
      big_in[2], big_out[2], *moments_v)
    return [res[k * n_p:(k + 1) * n_p] for k in range(4)], res[4 * n_p]


def kernel(x, positions, norm_gain, w_in, q_norm_gain, k_norm_gain, sinks, w_out, loss_target, m_norm_gain, m_w_in, m_q_norm_gain, m_k_norm_gain, m_sinks, m_w_out, v_norm_gain, v_w_in, v_q_norm_gain, v_k_norm_gain, v_sinks, v_w_out):
    x2 = x.reshape(T_LOC, D_MODEL)
    tgt2 = loss_target.reshape(T_LOC, D_MODEL)
    pos2 = positions.reshape(1, T_LOC)
    half = HEAD_DIM // 2
    inv_freq = ROPE_THETA ** (-jnp.arange(half, dtype=F32) * 2.0 / HEAD_DIM)
    inv_freq = jnp.tile(inv_freq, SLAB // half).reshape(SLAB, 1)
    sin_sign = jnp.tile(jnp.concatenate([-jnp.ones((half,), F32), jnp.ones((half,), F32)]), 2).reshape(1, SLAB)
    q_gain2 = jnp.tile(q_norm_gain, (1, 2))
    k_gain2 = jnp.tile(k_norm_gain, (1, 2))

    win_t = gather_weights(w_in.reshape(D_MODEL, IN_SHARD).T)

    (qa_raw, ka_raw, q_rot, k_dup, v_dup, ga, qb, kb, vb, gb, cos, sin_s, wout) = fwd_proj(
        x2, pos2, norm_gain, win_t, inv_freq, sin_sign, q_gain2, k_gain2, w_out.reshape(OUT_SHARD, D_MODEL).astype(BF16))
    o_a = swa_fwd(q_rot, k_dup, v_dup, sinks)
    o_b, carries = sb_fwd(qb, kb, vb)
    dout, d_oa, d_ob, dga, dgb, dwout, loss_part = out_loss(o_a, o_b, ga, gb, x2, tgt2, wout)
    dq_rot, dk_dup, dv_dup, dsink = swa_bwd(q_rot, k_dup, v_dup, o_a, d_oa, sinks)
    dqb, dkb, dvb, r_out = sb_bwd(qb, kb, vb, d_ob, carries, dwout)
    dproj, dwin_t, dqg, dkg = bwd_dw(
        x2, norm_gain, dq_rot, dk_dup, dv_dup, qa_raw, ka_raw, cos, sin_s, q_gain2, k_gain2, dga, dgb, dqb, dkb, dvb)
    grad_x, r_win, r_small = bwd_dx(x2, dout, norm_gain, win_t, dproj, dwin_t, dqg, dkg, dsink, loss_part)

    w_in2, m_in2, v_in2 = (a.reshape(D_MODEL, IN_SHARD).T for a in (w_in, m_w_in, v_w_in))
    w_out2, m_out2, v_out2 = (a.reshape(OUT_SHARD, D_MODEL) for a in (w_out, m_w_out, v_w_out))
    kinds, loss = adamw_all(
        r_win, r_out, r_small, (w_in2, m_in2, v_in2), (w_out2, m_out2, v_out2),
        (norm_gain, q_norm_gain, k_norm_gain, sinks),
        (m_norm_gain, m_q_norm_gain, m_k_norm_gain, m_sinks), (v_norm_gain, v_q_norm_gain, v_k_norm_gain, v_sinks))

    def leaves(k):
        big_in, big_out, ng, qg, kg, sk = kinds[k]
        return (ng, big_in.T.reshape(1, D_MODEL, IN_SHARD), qg, kg, sk, big_out.reshape(1, OUT_SHARD, D_MODEL))

    return (loss.reshape(()), grad_x.reshape(B_LOC, SEQ, D_MODEL), *leaves(0), *leaves(1), *leaves(2), *leaves(3))
```

```python
import functools

import jax
import jax.numpy as jnp
from jax import lax
from jax.experimental import pallas as pl
from jax.experimental.pallas import tpu as pltpu

F32 = jnp.float32
BF16 = jnp.bfloat16

N_DEV = 8
D_MODEL = 1024
SEQ = 2048
B_LOC = 2
T_LOC = B_LOC * SEQ
HEAD_DIM = 64
HEAD_SHIFT = 6
BLK = 128
N_BLK = SEQ // BLK
SLAB = 128
IN_WIDTH = 3328
IN_SHARD = IN_WIDTH // N_DEV
OUT_SHARD = D_MODEL // N_DEV
EPS = 1e-6
ROPE_THETA = 10000.0
Q_SCALE = 0.125
R_QA, R_KA, R_VA, R_GA, R_QB, R_KB, R_VB, R_GB, R_END = 0, 512, 640, 768, 1280, 1792, 2304, 2816, 3328
SMALL_W = 1536
ADAM_LR, ADAM_B1, ADAM_B2, ADAM_EPS, ADAM_WD, ADAM_STEP = 0.001, 0.9, 0.999, 1e-08, 0.01, 10
TM = 512
ACC_ROWS = 256
GATHER_CHUNKS = 2
VMEM_LIMIT = 56 * 1024 * 1024

MESH = pl.DeviceIdType.MESH
NT = (((1,), (1,)), ((), ()))
TN = (((0,), (0,)), ((), ()))


def _params(sem, limit=VMEM_LIMIT):
    return pltpu.CompilerParams(dimension_semantics=sem, vmem_limit_bytes=limit)


def _dot(a, b, dims=None):
    if dims is None:
        return jnp.dot(a, b, preferred_element_type=F32)
    return lax.dot_general(a, b, dims, preferred_element_type=F32)


def _lane(shape):
    return lax.broadcasted_iota(jnp.int32, shape, len(shape) - 1)


def _row(shape):
    return lax.broadcasted_iota(jnp.int32, shape, 0)


def _head_blockdiag():
    return ((_row((SLAB, SLAB)) >> HEAD_SHIFT) == (_lane((SLAB, SLAB)) >> HEAD_SHIFT)).astype(BF16)


def _head_sum(x, bd):
    return _dot(x.astype(BF16), bd)


def _swap_half(y, lane):
    return jnp.where((lane & 32) != 0, pltpu.roll(y, 32, 1), pltpu.roll(y, 96, 1))


def _stack_heads(q, lane):
    zero = jnp.zeros_like(q)
    return jnp.concatenate([jnp.where(lane < HEAD_DIM, q, zero), jnp.where(lane >= HEAD_DIM, q, zero)], axis=0)


def _unstack_heads(x2, lane):
    return jnp.where(lane < HEAD_DIM, x2[:BLK], x2[BLK:])


def _sigmoid(x):
    return 1.0 / (1.0 + jnp.exp(-x))


def _mesh_pos():
    return lax.axis_index("x"), lax.axis_index("y"), lax.axis_index("c")


def _flip(pos, mask):
    return tuple(1 - p if m else p for p, m in zip(pos, mask))


def _lin(pos):
    return 4 * pos[0] + 2 * pos[1] + pos[2]


DEV_FLIPS = [(fx, fy, fc) for fx in (0, 1) for fy in (0, 1) for fc in (0, 1)][1:]


def _direct_exchange(src_for, dst_slot, send_sems, recv_sems, local_sem):
    me = _mesh_pos()

    def copy(k, to):
        return pltpu.make_async_remote_copy(
            src_ref=src_for(to), dst_ref=dst_slot(me), send_sem=send_sems.at[k], recv_sem=recv_sems.at[k],
            device_id=to, device_id_type=MESH)

    def landed(k, frm):
        return pltpu.make_async_remote_copy(
            src_ref=src_for(frm), dst_ref=dst_slot(frm), send_sem=send_sems.at[k], recv_sem=recv_sems.at[k],
            device_id=frm, device_id_type=MESH)

    local = None if local_sem is None else pltpu.make_async_copy(src_for(me), dst_slot(me), local_sem)
    peers = [_flip(me, f) for f in DEV_FLIPS]

    def start():
        if local is not None:
            local.start()
        for k, to in enumerate(peers):
            copy(k, to).start()

    def finish():
        for k, frm in enumerate(peers):
            landed(k, frm).wait_recv()
        for k, to in enumerate(peers):
            copy(k, to).wait_send()
        if local is not None:
            local.wait()

    return start, finish


def gather_weights(shard):
    m = shard.shape[0]
    piece = m // GATHER_CHUNKS
    pieces = range(GATHER_CHUNKS)

    def body(f32_ref, o_ref, a_ref, ici_send, ici_recv, d2d_send, d2d_recv, local_sem):
        a_ref[...] = f32_ref[...].astype(BF16)
        x, y, c = _mesh_pos()
        me, sibling = (x, y, c), (x, y, 1 - c)
        chip_x, chip_y, chip_d = (1 - x, y), (x, 1 - y), (1 - x, 1 - y)

        def rows(pos, q):
            return o_ref.at[pl.ds(_lin(pos) * m + q * piece, piece), :]

        def own(q):
            return a_ref.at[pl.ds(q * piece, piece), :]

        def ici(k, q, block, to, src=None):
            return pltpu.make_async_remote_copy(
                src_ref=rows(block, q) if src is None else src, dst_ref=rows(block, q),
                send_sem=ici_send.at[k, q], recv_sem=ici_recv.at[k, q], device_id=to, device_id_type=MESH)

        def d2d(k, q, chip, mine, src=None):
            block = (*chip, c) if mine else (*chip, 1 - c)
            return pltpu.make_async_remote_copy(
                src_ref=rows(block, q) if src is None else src, dst_ref=rows(block, q),
                send_sem=d2d_send.at[k, q], recv_sem=d2d_recv.at[k, q], device_id=sibling, device_id_type=MESH)

        local = pltpu.make_async_copy(a_ref, o_ref.at[pl.ds(_lin(me) * m, m), :], local_sem)
        local.start()
        sends = []
        for q in pieces:
            sends += [ici(0, q, me, (*chip_x, c), src=own(q)), ici(1, q, me, (*chip_y, c), src=own(q)),
                      d2d(0, q, (x, y), True, src=own(q))]
        for cp in sends:
            cp.start()

        def pass_on(first, k_first, second, k_second, onward):
            moved = []
            for q in pieces:
                ici(k_first, q, (*first, c), me).wait_recv()
                moved += [ici(2, q, (*first, c), (*onward, c)), d2d(1 + k_first, q, first, True)]
                for cp in moved[-2:]:
                    cp.start()
            for q in pieces:
                ici(k_second, q, (*second, c), me).wait_recv()
                moved.append(d2d(1 + k_second, q, second, True))
                moved[-1].start()
            for q in pieces:
                ici(2, q, (*chip_d, c), me).wait_recv()
                moved.append(d2d(3, q, chip_d, True))
                moved[-1].start()
            for cp in moved:
                cp.wait_send()

        @pl.when(c == 0)
        def _():
            pass_on(chip_y, 1, chip_x, 0, chip_x)

        @pl.when(c == 1)
        def _():
            pass_on(chip_x, 0, chip_y, 1, chip_y)

        for k, chip in enumerate([(x, y), chip_x, chip_y, chip_d]):
            for q in pieces:
                d2d(k, q, chip, False).wait_recv()
        for cp in sends:
            cp.wait_send()
        local.wait()

    vmem = pl.BlockSpec(memory_space=pltpu.VMEM)
    n_q = GATHER_CHUNKS
    return pl.pallas_call(
        body, name="gather_weights",
        out_shape=jax.ShapeDtypeStruct((N_DEV * m, shard.shape[1]), BF16),
        in_specs=[vmem], out_specs=vmem,
        scratch_shapes=[pltpu.VMEM(shard.shape, BF16), pltpu.SemaphoreType.DMA((3, n_q)), pltpu.SemaphoreType.DMA((3, n_q)),
                        pltpu.SemaphoreType.DMA((4, n_q)), pltpu.SemaphoreType.DMA((4, n_q)), pltpu.SemaphoreType.DMA],
        compiler_params=pltpu.CompilerParams(vmem_limit_bytes=VMEM_LIMIT),
    )(shard)


def _norm_rope(xs, gain2, cos, sin_s, bd, lane):
    r = lax.rsqrt(_head_sum(xs * xs, bd) * (1.0 / HEAD_DIM) + EPS)
    y = xs * r * gain2
    return y * cos + _swap_half(y, lane) * sin_s


def _dup_heads(xs, lane):
    r = pltpu.roll(xs, HEAD_DIM, 1)
    lo = lane < HEAD_DIM
    return jnp.concatenate([jnp.where(lo, xs, r), jnp.where(lo, r, xs)], axis=1)


def fwd_proj(x, pos, norm_gain, win_t, inv_freq, sin_sign, q_gain2, k_gain2, wout_shard):
    n_tiles = T_LOC // TM

    def body(x_ref, pos_ref, ng_ref, w_ref, if_ref, sg_ref, qg_ref, kg_ref, ws_hbm,
             qa_raw_ref, ka_raw_ref, q_rot_ref, k_dup_ref, v_dup_ref, ga_ref, qb_ref, kb_ref, vb_ref, gb_ref,
             cos_ref, sin_ref, wo_hbm, wo_send, wo_recv, wo_local):
        start_wout, finish_wout = _direct_exchange(
            lambda dev: ws_hbm, lambda dev: wo_hbm.at[pl.ds(_lin(dev) * OUT_SHARD, OUT_SHARD), :],
            wo_send, wo_recv, wo_local)
        pl.when(pl.program_id(0) == 0)(start_wout)

        xv = x_ref[...]
        rstd = lax.rsqrt(jnp.mean(xv * xv, axis=-1, keepdims=True) + EPS)
        h = (xv * rstd * ng_ref[...]).astype(BF16)

        def proj(r0, r1):
            return _dot(h, w_ref[r0:r1, :], NT)

        ang_t = if_ref[...] * pos_ref[...].astype(F32)
        cos = jnp.cos(ang_t).T
        sin_s = jnp.sin(ang_t).T * sg_ref[...]
        cos_ref[...] = cos
        sin_ref[...] = sin_s
        lane = _lane((TM, SLAB))
        bd = _head_blockdiag()

        qa = proj(R_QA, R_KA)
        qa_raw_ref[...] = qa
        for p in range(4):
            sl = slice(p * SLAB, (p + 1) * SLAB)
            q_rot_ref[:, sl] = (_norm_rope(qa[:, sl], qg_ref[...], cos, sin_s, bd, lane) * Q_SCALE).astype(BF16)
        ka = proj(R_KA, R_VA)
        ka_raw_ref[...] = ka
        k_dup_ref[...] = _dup_heads(_norm_rope(ka, kg_ref[...], cos, sin_s, bd, lane), lane).astype(BF16)
        v_dup_ref[...] = _dup_heads(proj(R_VA, R_GA), lane).astype(BF16)
        ga_ref[...] = proj(R_GA, R_QB).astype(BF16)
        qb_ref[...] = (proj(R_QB, R_KB) * Q_SCALE).astype(BF16)
        kb_ref[...] = proj(R_KB, R_VB).astype(BF16)
        vb_ref[...] = proj(R_VB, R_GB).astype(BF16)
        gb_ref[...] = proj(R_GB, R_END).astype(BF16)
        pl.when(pl.program_id(0) == n_tiles - 1)(finish_wout)

    def tile(w):
        return pl.BlockSpec((TM, w), lambda i: (i, 0))

    def whole(a):
        return pl.BlockSpec(a.shape, lambda i: (0, 0))

    hbm = pl.BlockSpec(memory_space=pl.ANY)
    widths = [(512, F32), (128, F32), (512, BF16), (256, BF16), (256, BF16), (512, BF16), (512, BF16), (512, BF16),
              (512, BF16), (512, BF16), (128, F32), (128, F32)]
    return pl.pallas_call(
        body, name="fwd_proj", grid=(n_tiles,),
        in_specs=[tile(D_MODEL), pl.BlockSpec((1, TM), lambda i: (0, i)), whole(norm_gain), whole(win_t),
                  whole(inv_freq), whole(sin_sign),
                  whole(q_gain2), whole(k_gain2), hbm],
        out_specs=[tile(w) for w, _ in widths] + [hbm],
        out_shape=[jax.ShapeDtypeStruct((T_LOC, w), dt) for w, dt in widths]
        + [jax.ShapeDtypeStruct((D_MODEL, D_MODEL), BF16)],
        scratch_shapes=[pltpu.SemaphoreType.DMA((7,)), pltpu.SemaphoreType.DMA((7,)), pltpu.SemaphoreType.DMA],
        compiler_params=_params(("arbitrary",)),
    )(x, pos, norm_gain, win_t, inv_freq, sin_sign, q_gain2, k_gain2, wout_shard)


def _swa_window(prev_ref, cur_ref, p):
    gsl = _slab(p // 2)
    return jnp.concatenate([prev_ref[:, gsl], cur_ref[:, gsl]], axis=0)


def _swa_probs(s, sinks_ref, p, i):
    shape = (2 * BLK, 2 * BLK)
    r = _row(shape) & (BLK - 1)
    cidx = _lane(shape)
    valid = (cidx > r) & (cidx <= r + BLK) & ((cidx >= BLK) | (i > 0))
    s = jnp.where(valid, s, -jnp.inf)
    sink = jnp.where(_row((2 * BLK, 1)) < BLK, sinks_ref[0, 2 * p], sinks_ref[0, 2 * p + 1])
    m = jnp.maximum(jnp.max(s, axis=-1, keepdims=True), sink)
    e = jnp.exp(s - m)
    e_sink = jnp.exp(sink - m)
    den = jnp.sum(e, axis=-1, keepdims=True) + e_sink
    return e / den, e_sink / den


SWA_CHAINS = [(b, p) for b in range(B_LOC) for p in range(4)]


def _swa_specs():
    def cur(w):
        return pl.BlockSpec((B_LOC, BLK, w), lambda i: (0, i, 0))

    def prev(w):
        return pl.BlockSpec((B_LOC, BLK, w), lambda i: (0, jnp.maximum(i - 1, 0), 0))

    return cur, prev


def swa_fwd(q_rot, k_dup, v_dup, sinks):
    def body(q_ref, kp_ref, kc_ref, vp_ref, vc_ref, sinks_ref, o_ref):
        i = pl.program_id(0)
        lane = _lane((BLK, SLAB))
        s = [_dot(_stack_heads(q_ref[b, :, _slab(p)], lane), _swa_window(kp_ref.at[b], kc_ref.at[b], p), NT)
             for b, p in SWA_CHAINS]
        pn = [_swa_probs(s[c], sinks_ref, p, i)[0].astype(BF16) for c, (b, p) in enumerate(SWA_CHAINS)]
        for c, (b, p) in enumerate(SWA_CHAINS):
            o = _unstack_heads(_dot(pn[c], _swa_window(vp_ref.at[b], vc_ref.at[b], p)), lane)
            o_ref[b, :, _slab(p)] = o.astype(BF16)

    cur, prev = _swa_specs()
    q3, k3, v3 = (a.reshape(B_LOC, SEQ, a.shape[1]) for a in (q_rot, k_dup, v_dup))
    return pl.pallas_call(
        body, name="swa_fwd", grid=(N_BLK,),
        in_specs=[cur(512), prev(256), cur(256), prev(256), cur(256), pl.BlockSpec(memory_space=pltpu.SMEM)],
        out_specs=cur(512),
        out_shape=jax.ShapeDtypeStruct((B_LOC, SEQ, 512), BF16),
        compiler_params=_params(("arbitrary",)),
    )(q3, k3, k3, v3, v3, sinks).reshape(T_LOC, 512)


def _tri(suffix):
    r, cidx = _row((BLK + 16, BLK)), _lane((BLK + 16, BLK))
    tri = (cidx > r) if suffix else (cidx < r)
    return (tri | (r >= BLK)).astype(BF16)


def _key_sums(tri, x):
    res = _dot(tri, x.astype(BF16))
    return res[:BLK], res[BLK:BLK + 1]


def _sb_softplus(zt, valid):
    neg_abs = lax.bitcast_convert_type(lax.bitcast_convert_type(zt, jnp.uint32) | jnp.uint32(0x80000000), F32)
    sp = jnp.maximum(zt, 0.0) + jnp.log(1.0 + jnp.exp(neg_abs))
    return sp if valid is None else jnp.where(valid, sp, 0.0)


def _sb_weights(zt, sp, later, valid):
    w = jnp.exp(zt - sp - later)
    return w if valid is None else jnp.where(valid, w, 0.0)


def _slab(pp):
    return slice(pp * SLAB, (pp + 1) * SLAB)


def _blk(j):
    return pl.ds(pl.multiple_of(j * BLK, BLK), BLK)


def _causal_t():
    return _row((BLK, 2 * BLK)) < (_lane((BLK, 2 * BLK)) & (BLK - 1))


def _sb_rows(b, j):
    return pl.ds(pl.multiple_of(b * SEQ + j * BLK, BLK), BLK)


SB_CHAINS = [(b, pp) for b in range(B_LOC) for pp in range(4)]


def sb_fwd(qb, kb, vb):
    def body(q_ref, k_ref, v_ref, o_ref, c_ref, vt_ref, ot_ref):
        for c, (b, pp) in enumerate(SB_CHAINS):
            for j in range(N_BLK):
                vt_ref[c, j] = v_ref[b * SEQ + j * BLK:b * SEQ + (j + 1) * BLK, _slab(pp)].T
        lane = _lane((BLK, SLAB))
        tri = _tri(True)
        valid = _causal_t()
        jrow = _row((N_BLK, 2 * BLK))
        chains = range(len(SB_CHAINS))

        def q_block(i, _):
            q2 = [_stack_heads(q_ref[_sb_rows(b, i), _slab(pp)], lane) for b, pp in SB_CHAINS]

            def key_block(j, carry, mask, first):
                zt = [_dot(k_ref[_sb_rows(b, j), _slab(pp)], q2[c], NT) for c, (b, pp) in enumerate(SB_CHAINS)]
                sp = [_sb_softplus(zt[c], mask) for c in chains]
                sums = [_key_sums(tri, sp[c]) for c in chains]
                w = [_sb_weights(zt[c], sp[c], sums[c][0] + carry[c], mask) for c in chains]
                for c in chains:
                    pv = _dot(vt_ref[c, j], w[c].astype(BF16))
                    if first:
                        ot_ref[c] = pv
                    else:
                        ot_ref[c] += pv
                return tuple(carry[c] + sums[c][1] for c in chains)

            def earlier(jj, state):
                carry, saved = state
                j = i - 1 - jj
                saved = tuple(jnp.where(jrow == j, carry[c], saved[c]) for c in chains)
                return key_block(j, carry, None, False), saved

            zero = tuple(jnp.zeros((1, 2 * BLK), F32) for _ in chains)
            carry = key_block(i, zero, valid, True)
            _, saved = lax.fori_loop(0, i, earlier, (carry, tuple(jnp.zeros((N_BLK, 2 * BLK), F32) for _ in chains)))
            for c, (b, pp) in enumerate(SB_CHAINS):
                o_ref[_sb_rows(b, i), _slab(pp)] = _unstack_heads(ot_ref[c].T, lane).astype(BF16)
                c_ref[c * N_BLK + i] = saved[c]
            return 0

        lax.fori_loop(0, N_BLK, q_block, 0)

    n_ch = len(SB_CHAINS)
    vmem = pl.BlockSpec(memory_space=pltpu.VMEM)
    return pl.pallas_call(
        body, name="sb_fwd",
        in_specs=[vmem] * 3, out_specs=[vmem] * 2,
        out_shape=[jax.ShapeDtypeStruct((T_LOC, 512), BF16), jax.ShapeDtypeStruct((n_ch * N_BLK, N_BLK, 2 * BLK), F32)],
        scratch_shapes=[pltpu.VMEM((n_ch, N_BLK, SLAB, BLK), BF16), pltpu.VMEM((n_ch, SLAB, 2 * BLK), F32)],
        compiler_params=pltpu.CompilerParams(vmem_limit_bytes=VMEM_LIMIT),
    )(qb, kb, vb)


def out_loss(o_a, o_b, ga, gb, x, target, wout):
    n_tiles = T_LOC // TM

    def body(oa_ref, ob_ref, ga_ref, gb_ref, x_ref, t_ref, w_ref,
             dout_ref, doa_ref, dob_ref, dga_ref, dgb_ref, dw_ref, loss_ref, acc_ref):
        step = pl.program_id(0)

        @pl.when(step == 0)
        def _():
            acc_ref[...] = jnp.zeros_like(acc_ref)
            loss_ref[...] = jnp.zeros_like(loss_ref)

        halves = [slice(k * (TM // 2), (k + 1) * (TM // 2)) for k in range(2)]
        gate = []
        for rows in halves:
            oa, ob, gav, gbv = (r[rows, :].astype(F32) for r in (oa_ref, ob_ref, ga_ref, gb_ref))
            sa, sb = _sigmoid(gav), _sigmoid(gbv)
            gate.append((oa, ob, gav, gbv, sa, sb, gav * sa, gbv * sb))
        y = [jnp.concatenate([g[0] * g[6], g[1] * g[7]], axis=1).astype(BF16) for g in gate]
        out = [_dot(y[k], w_ref[...]) for k in range(2)]
        dob16 = []
        for k, rows in enumerate(halves):
            err = x_ref[rows, :] + out[k] - t_ref[rows, :]
            e2 = err * err
            part = jnp.sum(e2.reshape(TM // 16, 8, D_MODEL), axis=0)
            loss_ref[...] += functools.reduce(lambda a, b: a + b, [part[:, j * 128:(j + 1) * 128] for j in range(8)])
            dout = err * (1.0 / D_MODEL)
            dout_ref[rows, :] = dout
            dob16.append(dout.astype(BF16))
        dy = [_dot(dob16[k], w_ref[...], NT) for k in range(2)]
        for k in range(2):
            for r0 in range(0, D_MODEL, ACC_ROWS):
                acc_ref[r0:r0 + ACC_ROWS, :] += _dot(y[k][:, r0:r0 + ACC_ROWS], dob16[k], TN)
        for k, rows in enumerate(halves):
            oa, ob, gav, gbv, sa, sb, silu_a, silu_b = gate[k]
            dya, dyb = dy[k][:, :512], dy[k][:, 512:]
            doa_ref[rows, :] = (dya * silu_a).astype(BF16)
            dob_ref[rows, :] = (dyb * silu_b).astype(BF16)
            dga_ref[rows, :] = (dya * oa * (sa * (1.0 + gav * (1.0 - sa)))).astype(BF16)
            dgb_ref[rows, :] = (dyb * ob * (sb * (1.0 + gbv * (1.0 - sb)))).astype(BF16)

        @pl.when(step == n_tiles - 1)
        def _():
            dw_ref[...] = acc_ref[...].astype(BF16)

    def tile(w):
        return pl.BlockSpec((TM, w), lambda i: (i, 0))

    const = lambda i: (0, 0)
    return pl.pallas_call(
        body, name="out_loss", grid=(n_tiles,),
        in_specs=[tile(512)] * 4 + [tile(D_MODEL)] * 2 + [pl.BlockSpec((D_MODEL, D_MODEL), const)],
        out_specs=[tile(D_MODEL), tile(512), tile(512), tile(512), tile(512),
                   pl.BlockSpec((D_MODEL, D_MODEL), const), pl.BlockSpec((8, 128), const)],
        out_shape=[jax.ShapeDtypeStruct((T_LOC, D_MODEL), F32)] + [jax.ShapeDtypeStruct((T_LOC, 512), BF16)] * 4
        + [jax.ShapeDtypeStruct((D_MODEL, D_MODEL), BF16), jax.ShapeDtypeStruct((8, 128), F32)],
        scratch_shapes=[pltpu.VMEM((D_MODEL, D_MODEL), F32)],
        compiler_params=_params(("arbitrary",)),
    )(o_a, o_b, ga, gb, x, target, wout)


def swa_bwd(q_rot, k_dup, v_dup, o_a, d_oa, sinks):
    def body(q_ref, kp_ref, kc_ref, vp_ref, vc_ref, o_ref, do_ref, sinks_ref, dq_ref, dk_ref, dv_ref, dsink_ref):
        i = pl.program_id(0)

        @pl.when(i == 0)
        def _():
            dk_ref[...] = jnp.zeros_like(dk_ref)
            dv_ref[...] = jnp.zeros_like(dv_ref)
            dsink_ref[...] = jnp.zeros_like(dsink_ref)

        lane = _lane((BLK, SLAB))
        rows_prev, rows_cur = _blk(jnp.maximum(i - 1, 0)), _blk(i)
        chains = range(len(SWA_CHAINS))
        q2 = [_stack_heads(q_ref[b, :, _slab(p)], lane) for b, p in SWA_CHAINS]
        do2 = [_stack_heads(do_ref[b, :, _slab(p)], lane) for b, p in SWA_CHAINS]
        keys = [_swa_window(kp_ref.at[b], kc_ref.at[b], p) for b, p in SWA_CHAINS]
        s = [_dot(q2[c], keys[c], NT) for c in chains]
        dp = [_dot(do2[c], _swa_window(vp_ref.at[b], vc_ref.at[b], p), NT) for c, (b, p) in enumerate(SWA_CHAINS)]
        ds, pn16, cols = [], [], []
        for c, (b, p) in enumerate(SWA_CHAINS):
            pn, p_sink = _swa_probs(s[c], sinks_ref, p, i)
            o = o_ref[b, :, _slab(p)].astype(F32)
            delta =jnp.sum(do2[c].astype(F32) * jnp.concatenate([o, o], axis=0), axis=-1, keepdims=True)
            ds.append((pn * (dp[c] - delta)).astype(BF16))
            pn16.append(pn.astype(BF16))
            cols.append(-p_sink * delta)
        for c, (b, p) in enumerate(SWA_CHAINS):
            dq_ref[b, :, _slab(p)] = _unstack_heads(_dot(ds[c], keys[c]), lane) * Q_SCALE
        dk2 = [_dot(ds[c], q2[c], TN) for c in chains]
        dv2 = [_dot(pn16[c], do2[c], TN) for c in chains]
        for c, (b, p) in enumerate(SWA_CHAINS):
            gsl = _slab(p // 2)
            dk_ref[b, rows_prev, gsl] += dk2[c][:BLK]
            dk_ref[b, rows_cur, gsl] += dk2[c][BLK:]
            dv_ref[b, rows_prev, gsl] += dv2[c][:BLK]
            dv_ref[b, rows_cur, gsl] += dv2[c][BLK:]
            for e in range(2):
                dsink_ref[2 * p + e:2 * p + e + 1, :] += jnp.sum(cols[c][e * BLK:(e + 1) * BLK], axis=0, keepdims=True)

    cur, prev = _swa_specs()
    whole = pl.BlockSpec((B_LOC, SEQ, 256), lambda i: (0, 0, 0))
    q3, k3, v3, o3, do3 = (a.reshape(B_LOC, SEQ, a.shape[1]) for a in (q_rot, k_dup, v_dup, o_a, d_oa))
    dq, dk, dv, dsink = pl.pallas_call(
        body, name="swa_bwd", grid=(N_BLK,),
        in_specs=[cur(512), prev(256), cur(256), prev(256), cur(256), cur(512), cur(512),
                  pl.BlockSpec(memory_space=pltpu.SMEM)],
        out_specs=[cur(512), whole, whole, pl.BlockSpec((8, 128), lambda i: (0, 0))],
        out_shape=[jax.ShapeDtypeStruct((B_LOC, SEQ, 512), F32), jax.ShapeDtypeStruct((B_LOC, SEQ, 256), F32),
                   jax.ShapeDtypeStruct((B_LOC, SEQ, 256), F32), jax.ShapeDtypeStruct((8, 128), F32)],
        compiler_params=_params(("arbitrary",)),
    )(q3, k3, k3, v3, v3, o3, do3, sinks)
    return dq.reshape(T_LOC, 512), dk.reshape(T_LOC, 256), dv.reshape(T_LOC, 256), dsink


def sb_bwd(qb, kb, vb, d_ob, carries, dwout):
    def body(q_ref, k_ref, v_ref, do_ref, c_ref, dw_hbm, dq_ref, dk_ref, dv_ref, rw_hbm, kt_ref, dqt_ref,
             rw_send, rw_recv, rw_local):
        start_dwout, finish_dwout = _direct_exchange(
            lambda dev: dw_hbm.at[pl.ds(_lin(dev) * OUT_SHARD, OUT_SHARD), :], lambda dev: rw_hbm.at[_lin(dev)],
            rw_send, rw_recv, rw_local)
        start_dwout()
        for c, (b, pp) in enumerate(SB_CHAINS):
            for j in range(N_BLK):
                kt_ref[c, j] = k_ref[b * SEQ + j * BLK:b * SEQ + (j + 1) * BLK, _slab(pp)].T
        dk_ref[...] = jnp.zeros_like(dk_ref)
        dv_ref[...] = jnp.zeros_like(dv_ref)
        dqt_ref[...] = jnp.zeros_like(dqt_ref)
        lane = _lane((BLK, SLAB))
        tri_after, tri_before = _tri(True), _tri(False)
        valid = _causal_t()
        jrow = _row((N_BLK, 2 * BLK))
        chains = range(len(SB_CHAINS))

        def q_block(i, _):
            q2 = [_stack_heads(q_ref[_sb_rows(b, i), _slab(pp)], lane) for b, pp in SB_CHAINS]
            do2 = [_stack_heads(do_ref[_sb_rows(b, i), _slab(pp)], lane) for b, pp in SB_CHAINS]

            def key_block(j, carry_sp, before_u, mask):
                at = [(_sb_rows(b, j), _slab(pp)) for b, pp in SB_CHAINS]
                zt = [_dot(k_ref[at[c]], q2[c], NT) for c in chains]
                dw = [_dot(v_ref[at[c]], do2[c], NT) for c in chains]
                sp = [_sb_softplus(zt[c], mask) for c in chains]
                later = [_key_sums(tri_after, sp[c])[0] for c in chains]
                w = [_sb_weights(zt[c], sp[c], later[c] + carry_sp[c], mask) for c in chains]
                u = [dw[c] * w[c] for c in chains]
                for c in chains:
                    dv_ref[at[c]] += _dot(w[c].astype(BF16), do2[c])
                sums = [_key_sums(tri_before, u[c]) for c in chains]
                dz16 = []
                for c in chains:
                    sig = jnp.exp(zt[c] - sp[c])
                    dz = u[c] - sig * (u[c] + before_u[c] + sums[c][0])
                    if mask is not None:
                        dz = jnp.where(mask, dz, 0.0)
                    dz16.append(dz.astype(BF16))
                for c in chains:
                    dk_ref[at[c]] += _dot(dz16[c], q2[c])
                    dqt_ref[c] += _dot(kt_ref[c, j], dz16[c])
                return tuple(before_u[c] + sums[c][1] for c in chains)

            def earlier(j, before_u):
                carry_sp = [jnp.sum(jnp.where(jrow == j, c_ref[c * N_BLK + i], 0.0), axis=0, keepdims=True)
                            for c in chains]
                return key_block(j, carry_sp, before_u, None)

            zero = tuple(jnp.zeros((1, 2 * BLK), F32) for _ in chains)
            before_u = lax.fori_loop(0, i, earlier, zero)
            key_block(i, zero, before_u, valid)
            for c, (b, pp) in enumerate(SB_CHAINS):
                dq_ref[_sb_rows(b, i), _slab(pp)] = (_unstack_heads(dqt_ref[c].T, lane) * Q_SCALE).astype(BF16)
                dqt_ref[c] = jnp.zeros((SLAB, 2 * BLK), F32)
            return 0

        lax.fori_loop(0, N_BLK, q_block, 0)
        finish_dwout()

    n_ch = len(SB_CHAINS)
    vmem, hbm = pl.BlockSpec(memory_space=pltpu.VMEM), pl.BlockSpec(memory_space=pl.ANY)
    return pl.pallas_call(
        body, name="sb_bwd",
        in_specs=[vmem] * 5 + [hbm], out_specs=[vmem] * 3 + [hbm],
        out_shape=[jax.ShapeDtypeStruct((T_LOC, 512), BF16)] + [jax.ShapeDtypeStruct((T_LOC, 512), F32)] * 2
        + [jax.ShapeDtypeStruct((N_DEV, OUT_SHARD, D_MODEL), BF16)],
        scratch_shapes=[pltpu.VMEM((n_ch, N_BLK, SLAB, BLK), BF16), pltpu.VMEM((n_ch, SLAB, 2 * BLK), F32),
                        pltpu.SemaphoreType.DMA((7,)), pltpu.SemaphoreType.DMA((7,)), pltpu.SemaphoreType.DMA],
        compiler_params=pltpu.CompilerParams(vmem_limit_bytes=VMEM_LIMIT),
    )(qb, kb, vb, d_ob, carries, dwout)


def bwd_dw(x, norm_gain, dq_rot, dk_dup, dv_dup, qa_raw, ka_raw, cos, sin_s, q_gain2, k_gain2, dga, dgb, dqb, dkb, dvb):
    n_tiles = T_LOC // TM

    def body(x_ref, ng_ref, dq_ref, dk_ref, dv_ref, qa_ref, ka_ref, cos_ref, sin_ref, qg_ref, kg_ref,
             dga_ref, dgb_ref, dqb_ref, dkb_ref, dvb_ref,
             dproj_ref, dw_hbm, dqg_ref, dkg_ref, acc_ref, stage_ref):
        step = pl.program_id(0)

        @pl.when(step == 0)
        def _():
            acc_ref[...] = jnp.zeros_like(acc_ref)
            dqg_ref[...] = jnp.zeros_like(dqg_ref)
            dkg_ref[...] = jnp.zeros_like(dkg_ref)

        lane = _lane((TM, SLAB))
        bd = _head_blockdiag()
        cos, sin_s = cos_ref[...], sin_ref[...]

        def norm_rope_bwd(d_rot, raw, gain2):
            dy = d_rot * cos + _swap_half(d_rot * sin_s, lane)
            r = lax.rsqrt(_head_sum(raw * raw, bd) * (1.0 / HEAD_DIM) + EPS)
            xhat = raw * r
            dgain = jnp.sum(dy * xhat, axis=0, keepdims=True)
            dxh = dy * gain2
            mean = _head_sum(dxh * xhat, bd) * (1.0 / HEAD_DIM)
            return r * (dxh - xhat * mean), dgain

        def fold_dup(d_dup):
            a, b2 = d_dup[:, :SLAB], d_dup[:, SLAB:]
            return jnp.where(lane < HEAD_DIM, a + pltpu.roll(a, HEAD_DIM, 1), b2 + pltpu.roll(b2, HEAD_DIM, 1))

        pieces = []
        dqg = jnp.zeros((1, SLAB), F32)
        for p in range(4):
            sl = slice(p * SLAB, (p + 1) * SLAB)
            d_raw, dg = norm_rope_bwd(dq_ref[:, sl], qa_ref[:, sl], qg_ref[...])
            pieces.append(d_raw.astype(BF16))
            dqg = dqg + dg
        d_raw, dkg = norm_rope_bwd(fold_dup(dk_ref[...]), ka_ref[...], kg_ref[...])
        pieces.append(d_raw.astype(BF16))
        pieces.append(fold_dup(dv_ref[...]).astype(BF16))
        pieces += [dga_ref[...], dqb_ref[...], dkb_ref[...].astype(BF16), dvb_ref[...].astype(BF16),
                   dgb_ref[...]]
        dproj = jnp.concatenate(pieces, axis=1)
        dproj_ref[...] = dproj
        dqg_ref[0:1, :] += dqg + pltpu.roll(dqg, HEAD_DIM, 1)
        dkg_ref[0:1, :] += dkg + pltpu.roll(dkg, HEAD_DIM, 1)

        xv = x_ref[...]
        rstd = lax.rsqrt(jnp.mean(xv * xv, axis=-1, keepdims=True) + EPS)
        h = (xv * rstd * ng_ref[...]).astype(BF16)
        for r0 in range(0, IN_WIDTH, ACC_ROWS):
            acc_ref[r0:r0 + ACC_ROWS, :] += _dot(dproj[:, r0:r0 + ACC_ROWS], h, TN)

        @pl.when(step == n_tiles - 1)
        def _():
            for r0 in range(0, IN_WIDTH, ACC_ROWS):
                stage_ref[...] = acc_ref[r0:r0 + ACC_ROWS, :].astype(BF16)
                pltpu.sync_copy(stage_ref, dw_hbm.at[r0:r0 + ACC_ROWS, :])

    def tile(w):
        return pl.BlockSpec((TM, w), lambda i: (i, 0))

    def whole(a):
        return pl.BlockSpec(a.shape, lambda i: (0, 0))

    const = lambda i: (0, 0)
    return pl.pallas_call(
        body, name="bwd_dw", grid=(n_tiles,),
        in_specs=[tile(D_MODEL), whole(norm_gain),
                  tile(512), tile(256), tile(256), tile(512), tile(128), tile(128), tile(128),
                  whole(q_gain2), whole(k_gain2), tile(512), tile(512), tile(512), tile(512), tile(512)],
        out_specs=[tile(IN_WIDTH), pl.BlockSpec(memory_space=pl.ANY),
                   pl.BlockSpec((8, SLAB), const), pl.BlockSpec((8, SLAB), const)],
        out_shape=[jax.ShapeDtypeStruct((T_LOC, IN_WIDTH), BF16), jax.ShapeDtypeStruct((IN_WIDTH, D_MODEL), BF16),
                   jax.ShapeDtypeStruct((8, SLAB), F32), jax.ShapeDtypeStruct((8, SLAB), F32)],
        scratch_shapes=[pltpu.VMEM((IN_WIDTH, D_MODEL), F32), pltpu.VMEM((ACC_ROWS, D_MODEL), BF16)],
        compiler_params=_params(("arbitrary",)),
    )(x, norm_gain, dq_rot, dk_dup, dv_dup, qa_raw, ka_raw, cos, sin_s, q_gain2, k_gain2, dga, dgb, dqb, dkb, dvb)


def bwd_dx(x, dout, norm_gain, win_t, dproj, dwin_t, dqg, dkg, dsink, loss_part):
    n_tiles = T_LOC // TM
    rows_per = IN_SHARD
    step_sums, step_merge = 1, 3

    def body(x_ref, dout_ref, ng_ref, w_hbm, dp_ref, a_hbm, dqg_ref, dkg_ref, dsink_ref, loss_ref,
             gx_ref, ra_hbm, rs_hbm, w_ref, dng_ref, s_ref, own_ref, sib_ref, snd_ref, extra_ref,
             w_sem, d2d_send, d2d_recv, ici_send, ici_recv, own_sems, s_send, s_recv, out_sem):
        step = pl.program_id(0)
        x, y, c = _mesh_pos()
        me, sibling = (x, y, c), (x, y, 1 - c)
        chips = {"own": (x, y), "x": (1 - x, y), "y": (x, 1 - y), "d": (1 - x, 1 - y)}
        index = {"own": 0, "x": 1, "y": 2, "d": 3}
        order = ("d", "x", "y", "own")

        def rows(pos):
            return a_hbm.at[pl.ds(_lin(pos) * rows_per, rows_per), :]

        def to_sibling(k):
            return pltpu.make_async_remote_copy(
                src_ref=rows((*chips[k], 1 - c)), dst_ref=sib_ref.at[index[k]],
                send_sem=d2d_send.at[index[k]], recv_sem=d2d_recv.at[index[k]], device_id=sibling, device_id_type=MESH)

        def mine(k):
            return pltpu.make_async_copy(rows((*chips[k], c)), own_ref.at[index[k]], own_sems.at[index[k]])

        def ici(n, to_chip, dst):
            return pltpu.make_async_remote_copy(
                src_ref=snd_ref.at[n], dst_ref=dst, send_sem=ici_send.at[n], recv_sem=ici_recv.at[n],
                device_id=(*chips[to_chip], c), device_id_type=MESH)

        def chip_sum(k):
            to_sibling(k).wait_recv()
            mine(k).wait()
            return own_ref[index[k]].astype(F32) + sib_ref[index[k]].astype(F32)

        def by_core(fn):
            pl.when(c == 0)(lambda: fn("x", "y"))
            pl.when(c == 1)(lambda: fn("y", "x"))

        @pl.when(step == 0)
        def _():
            cp = pltpu.make_async_copy(w_hbm, w_ref, w_sem)
            cp.start()
            for k in order:
                to_sibling(k).start()
                mine(k).start()
            dng_ref[...] = jnp.zeros_like(dng_ref)
            cp.wait()

        @pl.when(step == step_sums)
        def _():
            def first_sends(direct, via):
                snd_ref[0] = chip_sum("d").astype(BF16)
                ici(0, direct, extra_ref).start()
                snd_ref[1] = chip_sum(direct).astype(BF16)
                ici(1, direct, ra_hbm.at[index[direct]]).start()
            by_core(first_sends)

        @pl.when(step == step_merge)
        def _():
            def merge(direct, via):
                merged = chip_sum(via)
                ici(0, direct, extra_ref).wait_recv()
                snd_ref[2] = (merged + extra_ref[...].astype(F32)).astype(BF16)
                ici(2, via, ra_hbm.at[index[via]]).start()
                own_ref[0] = chip_sum("own").astype(BF16)
                pltpu.make_async_copy(own_ref.at[0], ra_hbm.at[0], out_sem).start()
            by_core(merge)

        halves = [slice(k * (TM // 2), (k + 1) * (TM // 2)) for k in range(2)]
        gain = ng_ref[...]
        dh = [_dot(dp_ref[half, :], w_ref[...]) for half in halves]
        for k, half in enumerate(halves):
            xv = x_ref[half, :]
            rstd = lax.rsqrt(jnp.mean(xv * xv, axis=-1, keepdims=True) + EPS)
            xhat = xv * rstd
            dng_ref[0:1, :] += jnp.sum(dh[k] * xhat, axis=0, keepdims=True)
            dxh = dh[k] * gain
            gx_ref[half, :] = dout_ref[half, :] + rstd * (dxh - xhat * jnp.mean(dxh * xhat, axis=-1, keepdims=True))

        @pl.when(step == n_tiles - 1)
        def _():
            s_ref[...] = jnp.concatenate(
                [dng_ref[...], dqg_ref[...], dkg_ref[...], dsink_ref[...], loss_ref[...]], axis=1)
            start_small, finish_small = _direct_exchange(
                lambda dev: s_ref, lambda dev: rs_hbm.at[_lin(dev)], s_send, s_recv, out_sem)

            def finish(direct, via):
                ici(1, direct, ra_hbm.at[index[direct]]).wait_recv()
                ici(2, via, ra_hbm.at[index[via]]).wait_recv()
                for n, to in ((0, direct), (1, direct), (2, via)):
                    ici(n, to, extra_ref).wait_send()
            by_core(finish)
            pltpu.make_async_copy(own_ref.at[0], ra_hbm.at[0], out_sem).wait()
            for k in order:
                to_sibling(k).wait_send()
            start_small()
            finish_small()

    def tile(w):
        return pl.BlockSpec((TM, w), lambda i: (i, 0))

    def whole(a):
        return pl.BlockSpec(a.shape, lambda i: (0, 0))

    hbm = pl.BlockSpec(memory_space=pl.ANY)
    block = (rows_per, D_MODEL)
    return pl.pallas_call(
        body, name="bwd_dx", grid=(n_tiles,),
        in_specs=[tile(D_MODEL), tile(D_MODEL), whole(norm_gain), hbm, tile(IN_WIDTH), hbm,
                  whole(dqg), whole(dkg), whole(dsink), whole(loss_part)],
        out_specs=[tile(D_MODEL), hbm, hbm],
        out_shape=[jax.ShapeDtypeStruct((T_LOC, D_MODEL), F32), jax.ShapeDtypeStruct((3,) + block, BF16),
                   jax.ShapeDtypeStruct((N_DEV, 8, SMALL_W), F32)],
        scratch_shapes=[pltpu.VMEM((IN_WIDTH, D_MODEL), BF16), pltpu.VMEM((8, D_MODEL), F32),
                        pltpu.VMEM((8, SMALL_W), F32),
                        pltpu.VMEM((4,) + block, BF16), pltpu.VMEM((4,) + block, BF16), pltpu.VMEM((3,) + block, BF16),
                        pltpu.VMEM(block, BF16),
                        pltpu.SemaphoreType.DMA, pltpu.SemaphoreType.DMA((4,)), pltpu.SemaphoreType.DMA((4,)),
                        pltpu.SemaphoreType.DMA((3,)), pltpu.SemaphoreType.DMA((3,)), pltpu.SemaphoreType.DMA((4,)),
                        pltpu.SemaphoreType.DMA((7,)), pltpu.SemaphoreType.DMA((7,)), pltpu.SemaphoreType.DMA],
        compiler_params=_params(("arbitrary",)),
    )(x, dout, norm_gain, win_t, dproj, dwin_t, dqg, dkg, dsink, loss_part)


def _adamw(w, g, m, v):
    m = ADAM_B1 * m + (1.0 - ADAM_B1) * g
    v = ADAM_B2 * v + (1.0 - ADAM_B2) * (g * g)
    m_hat = m / (1.0 - ADAM_B1 ** ADAM_STEP)
    v_hat = v / (1.0 - ADAM_B2 ** ADAM_STEP)
    delta = -ADAM_LR * (m_hat / (jnp.sqrt(v_hat) + ADAM_EPS) + ADAM_WD * w)
    return delta, m, v


def _sum_slots(r_ref):
    g = r_ref[0].astype(F32)
    for s in range(1, r_ref.shape[0]):
        g = g + r_ref[s].astype(F32)
    return g


def adamw_all(r_win, r_out, r_small, big_in, big_out, weights, moments_m, moments_v):
    n = len(weights)
    params = [big_in[0], big_out[0], *weights]

    def body(rw_ref, ro_ref, rs_ref, *refs):
        n_p = n + 2
        ins, outs = refs[:3 * n_p], refs[3 * n_p:]
        s = _sum_slots(rs_ref)
        eye = (_row((8, SLAB)) == _lane((8, SLAB))).astype(F32)
        sinks = jnp.sum(s[:, 1280:1408] * eye, axis=0, keepdims=True)
        grads = [_sum_slots(rw_ref), _sum_slots(ro_ref),
                 s[0:1, :D_MODEL], s[0:1, 1024:1024 + HEAD_DIM], s[0:1, 1152:1152 + HEAD_DIM], sinks[:, :8]]
        for k in range(n_p):
            outs[k][...] = grads[k]
            outs[n_p + k][...], outs[2 * n_p + k][...], outs[3 * n_p + k][...] = _adamw(
                ins[k][...], grads[k], ins[n_p + k][...], ins[2 * n_p + k][...])
        loss = jnp.sum(jnp.sum(s[:, 1408:1536], axis=1, keepdims=True), axis=0, keepdims=True) * (0.5 / D_MODEL)
        outs[4 * n_p][...] = loss

    n_p = n + 2

    def split(shape):
        n_lead = len(shape) - 2
        return pl.BlockSpec((*shape[:-2], shape[-2] // GATHER_CHUNKS, shape[-1]), lambda i: (0,) * n_lead + (i, 0))

    def whole(shape):
        return pl.BlockSpec(shape, lambda i: (0,) * len(shape))

    param_specs = [split(big_in[0].shape), split(big_out[0].shape)] + [whole(w.shape) for w in weights]
    res = pl.pallas_call(
        body, name="adamw_all", grid=(GATHER_CHUNKS,),
        in_specs=[split(r_win.shape), split(r_out.shape), whole(r_small.shape)] + param_specs * 3,
        out_specs=param_specs * 4 + [whole((1, 1))],
        out_shape=[jax.ShapeDtypeStruct(p.shape, F32) for p in params] * 4 + [jax.ShapeDtypeStruct((1, 1), F32)],
        compiler_params=_params(("arbitrary",)),
    )(r_win, r_out, r_small, big_in[0], big_out[0], *weights, big_in[1], big_out[1], *moments_m,
      big_in[2], big_out[2], *moments_v)
    return [res[k * n_p:(k + 1) * n_p] for k in range(4)], res[4 * n_p]


def kernel(x, positions, norm_gain, w_in, q_norm_gain, k_norm_gain, sinks, w_out, loss_target, m_norm_gain, m_w_in, m_q_norm_gain, m_k_norm_gain, m_sinks, m_w_out, v_norm_gain, v_w_in, v_q_norm_gain, v_k_norm_gain, v_sinks, v_w_out):
    x2 = x.reshape(T_LOC, D_MODEL)
    tgt2 = loss_target.reshape(T_LOC, D_MODEL)
    pos2 = positions.reshape(1, T_LOC)
    half = HEAD_DIM // 2
    inv_freq = ROPE_THETA ** (-jnp.arange(half, dtype=F32) * 2.0 / HEAD_DIM)
    inv_freq = jnp.tile(inv_freq, SLAB // half).reshape(SLAB, 1)
    sin_sign = jnp.tile(jnp.concatenate([-jnp.ones((half,), F32), jnp.ones((half,), F32)]), 2).reshape(1, SLAB)
    q_gain2 = jnp.tile(q_norm_gain, (1, 2))
    k_gain2 = jnp.tile(k_norm_gain, (1, 2))

    win_t = gather_weights(w_in.reshape(D_MODEL, IN_SHARD).T)

    (qa_raw, ka_raw, q_rot, k_dup, v_dup, ga, qb, kb, vb, gb, cos, sin_s, wout) = fwd_proj(
        x2, pos2, norm_gain, win_t, inv_freq, sin_sign, q_gain2, k_gain2, w_out.reshape(OUT_SHARD, D_MODEL).astype(BF16))
    o_a = swa_fwd(q_rot, k_dup, v_dup, sinks)
    o_b, carries = sb_fwd(qb, kb, vb)
    dout, d_oa, d_ob, dga, dgb, dwout, loss_part = out_loss(o_a, o_b, ga, gb, x2, tgt2, wout)
    dq_rot, dk_dup, dv_dup, dsink = swa_bwd(q_rot, k_dup, v_dup, o_a, d_oa, sinks)
    dqb, dkb, dvb, r_out = sb_bwd(qb, kb, vb, d_ob, carries, dwout)
    dproj, dwin_t, dqg, dkg = bwd_dw(
        x2, norm_gain, dq_rot, dk_dup, dv_dup, qa_raw, ka_raw, cos, sin_s, q_gain2, k_gain2, dga, dgb, dqb, dkb, dvb)
    grad_x, r_win, r_small = bwd_dx(x2, dout, norm_gain, win_t, dproj, dwin_t, dqg, dkg, dsink, loss_part)

    w_in2, m_in2, v_in2 = (a.reshape(D_MODEL, IN_SHARD).T for a in (w_in, m_w_in, v_w_in))
    w_out2, m_out2, v_out2 = (a.reshape(OUT_SHARD, D_MODEL) for a in (w_out, m_w_out, v_w_out))
    kinds, loss = adamw_all(
        r_win, r_out, r_small, (w_in2, m_in2, v_in2), (w_out2, m_out2, v_out2),
        (norm_gain, q_norm_gain, k_norm_gain, sinks),
        (m_norm_gain, m_q_norm_gain, m_k_norm_gain, m_sinks), (v_norm_gain, v_q_norm_gain, v_k_norm_gain, v_sinks))

    def leaves(k):
        big_in, big_out, ng, qg, kg, sk = kinds[k]
        return (ng, big_in.T.reshape(1, D_MODEL, IN_SHARD), qg, kg, sk, big_out.reshape(1, OUT_SHARD, D_MODEL))

    return (loss.reshape(()), grad_x.reshape(B_LOC, SEQ, D_MODEL), *leaves(0), *leaves(1), *leaves(2), *leaves(3))
```
